```python
import math
import jax, jax.numpy as jnp
from jax import lax
import numpy as np

D_MODEL = 1024
BATCH = 8
SEQ = 2048
DEPTH = 2
DEC_BATCH = 32
DEC_SEQ = 1
PAST_LEN = 8192
PAGE_SIZE = 128

N_EVEN = (DEPTH + 1) // 2
N_ODD = DEPTH // 2
MIX_WIDTH = D_MODEL
CONV_CH = MIX_WIDTH // 2
CONV_WIDTH = 31
NSA_HEADS = 8
NSA_KV_HEADS = 2
NSA_GROUP = NSA_HEADS // NSA_KV_HEADS
NSA_HD = (MIX_WIDTH - CONV_CH) // NSA_HEADS
CMP_BLOCK = 32
SEL_BLOCK = 64
SEL_RATIO = SEL_BLOCK // CMP_BLOCK
TOP_N = 16
WINDOW = 512
Q_BLOCK = 128
FORCE_SCORE = 1.0e4
NUM_BUCKETS = 32
MAX_DISTANCE = 1024
MLSTM_HEADS = 4
MLSTM_DK = D_MODEL // MLSTM_HEADS
MLSTM_DV = D_MODEL // MLSTM_HEADS
MLSTM_CHUNK = 64
MEM_TOKENS = 256
X_HEADS = 4
X_HD = D_MODEL // X_HEADS
D_FF = 2816
N_EXPERTS = 8
TOP_K = 2
D_FF_EXPERT = D_FF // TOP_K
RMS_EPS = 1e-6
LN_EPS = 1e-5
NEG = -1e30
IN_EVEN = 2 * CONV_CH + NSA_HEADS * NSA_HD + 6 * NSA_KV_HEADS * NSA_HD + 3 * NSA_HEADS
IN_ODD = 2 * MLSTM_HEADS * MLSTM_DK + 2 * MLSTM_HEADS * MLSTM_DV + 2 * MLSTM_HEADS

kernel_name = "hybrid_conformer_nsa_mlstm_moe_decode_step"


def rms_norm(x, g):
    xf = x.astype(jnp.float32)
    y = xf * lax.rsqrt(jnp.mean(xf * xf, -1, keepdims=True) + RMS_EPS)
    return (y * g.astype(jnp.float32)).astype(x.dtype)


def rel_bucket(dist):
    n = jnp.maximum(dist, 0)
    max_exact = NUM_BUCKETS // 2
    nf = jnp.maximum(n, 1).astype(jnp.float32)
    large = max_exact + (jnp.log(nf / max_exact) / math.log(MAX_DISTANCE / max_exact)
                         * (NUM_BUCKETS - max_exact)).astype(jnp.int32)
    large = jnp.minimum(large, NUM_BUCKETS - 1)
    return jnp.where(n < max_exact, n, large)


def bias_qk(rel_bias, dist):
    b = rel_bias[rel_bucket(dist)].astype(jnp.float32)
    b = b.reshape(dist.shape + (NSA_KV_HEADS, NSA_GROUP))
    return jnp.transpose(b, (2, 3, 0, 1))


def masked_softmax(s, mask):
    s = jnp.where(mask, s, NEG)
    e = jnp.where(mask, jnp.exp(s - jnp.max(s, -1, keepdims=True)), 0.0)
    den = jnp.sum(e, -1, keepdims=True)
    return e / jnp.where(den > 0, den, 1.0)


def compress_kv(x, pe, w):
    B, T = x.shape[:2]
    nc = T // CMP_BLOCK
    blk = x[:, :nc * CMP_BLOCK].reshape(B, nc, CMP_BLOCK, NSA_KV_HEADS, NSA_HD) + pe[None, None, :, None, :]
    return jnp.einsum('bcjkd,jde->bcke', blk, w)


def nsa_attend(q, qpos, kc, vc, cend, ks_b, vs_b, kw, vw, kwpos, gates, rel_bias):
    f32 = jnp.float32
    B, Q = q.shape[:2]
    qg = q.reshape(B, Q, NSA_KV_HEADS, NSA_GROUP, NSA_HD)
    dist_c = qpos[:, None] - cend[None, :]
    s_c = jnp.einsum('bqkgd,bckd->bkgqc', qg, kc).astype(f32) + bias_qk(rel_bias, dist_c)
    p_c = masked_softmax(s_c, dist_c >= 0)
    o_c = jnp.einsum('bkgqc,bckd->bqkgd', p_c.astype(vc.dtype), vc)
    nsb = ks_b.shape[1]
    nc = kc.shape[1]
    n_sel = min(TOP_N, nsb)
    imp = jnp.sum(p_c, axis=2)
    imp = jnp.pad(imp, ((0, 0), (0, 0), (0, 0), (0, nsb * SEL_RATIO - nc)))
    imp = imp.reshape(B, NSA_KV_HEADS, Q, nsb, SEL_RATIO).sum(-1)
    blk = jnp.arange(nsb)[None, :]
    cur = (qpos // SEL_BLOCK)[:, None]
    forced = (blk == 0) | (blk == cur) | (blk == cur - 1)
    score = jnp.where(forced, FORCE_SCORE, jnp.where(blk <= cur, imp, -1.0))
    top_v, top_i = lax.top_k(score, n_sel)
    b_ix = jnp.arange(B)[:, None, None, None]
    h_ix = jnp.arange(NSA_KV_HEADS)[None, :, None, None]
    ks_t = jnp.transpose(ks_b, (0, 3, 1, 2, 4))
    vs_t = jnp.transpose(vs_b, (0, 3, 1, 2, 4))
    k_g = ks_t[b_ix, h_ix, top_i]
    v_g = vs_t[b_ix, h_ix, top_i]
    kpos = top_i[..., None] * SEL_BLOCK + jnp.arange(SEL_BLOCK)
    dist_s = qpos[None, None, :, None, None] - kpos
    mask_s = (dist_s >= 0) & (top_v >= 0)[..., None]
    tbl = jnp.transpose(rel_bias.reshape(NUM_BUCKETS, NSA_KV_HEADS, NSA_GROUP), (1, 0, 2))
    bias_s = tbl[jnp.arange(NSA_KV_HEADS)[None, :, None, None, None], rel_bucket(dist_s)]
    bias_s = jnp.moveaxis(bias_s, -1, 2).astype(f32)
    s_s = jnp.einsum('bqkgd,bkqnsd->bkgqns', qg, k_g).astype(f32) + bias_s
    m = n_sel * SEL_BLOCK
    p_s = masked_softmax(s_s.reshape(B, NSA_KV_HEADS, NSA_GROUP, Q, m),
                         mask_s.reshape(B, NSA_KV_HEADS, 1, Q, m))
    o_s = jnp.einsum('bkgqm,bkqmd->bqkgd', p_s.astype(v_g.dtype),
                     v_g.reshape(B, NSA_KV_HEADS, Q, m, NSA_HD))
    dist_w = qpos[:, None] - kwpos[None, :]
    mask_w = (dist_w >= 0) & (dist_w < WINDOW) & (kwpos >= 0)[None, :]
    s_w = jnp.einsum('bqkgd,bwkd->bkgqw', qg, kw).astype(f32) + bias_qk(rel_bias, dist_w)
    p_w = masked_softmax(s_w, mask_w)
    o_w = jnp.einsum('bkgqw,bwkd->bqkgd', p_w.astype(vw.dtype), vw)
    g = gates.reshape(B, Q, NSA_KV_HEADS, NSA_GROUP, 3)
    out = g[..., 0:1] * o_c + g[..., 1:2] * o_s + g[..., 2:3] * o_w
    return out.reshape(B, Q, NSA_HEADS * NSA_HD)


def nsa_prompt(q, kv, gates, pe, wcmp, rel_bias):
    B, S = q.shape[:2]
    kc = compress_kv(kv[:, :, 0], pe[0], wcmp[0])
    vc = compress_kv(kv[:, :, 1], pe[1], wcmp[1])
    cend = jnp.arange(kc.shape[1]) * CMP_BLOCK + (CMP_BLOCK - 1)
    nsb = S // SEL_BLOCK
    ks_b = kv[:, :, 2].reshape(B, nsb, SEL_BLOCK, NSA_KV_HEADS, NSA_HD)
    vs_b = kv[:, :, 3].reshape(B, nsb, SEL_BLOCK, NSA_KV_HEADS, NSA_HD)
    nq = S // Q_BLOCK
    wk = WINDOW + Q_BLOCK
    widx = jnp.arange(nq)[:, None] * Q_BLOCK + jnp.arange(wk)[None, :]
    wp = jnp.pad(kv[:, :, 4:6], ((0, 0), (WINDOW, 0), (0, 0), (0, 0), (0, 0)))
    wblk = jnp.moveaxis(wp[:, widx], 1, 0)
    kwpos = widx - WINDOW
    qb = jnp.moveaxis(q.reshape(B, nq, Q_BLOCK, NSA_HEADS, NSA_HD), 1, 0)
    gb = jnp.moveaxis(gates.reshape(B, nq, Q_BLOCK, NSA_HEADS, 3), 1, 0)
    qpos = jnp.arange(S).reshape(nq, Q_BLOCK)

    def one_block(args):
        qi, qp, wi, kp, gi = args
        return nsa_attend(qi, qp, kc, vc, cend, ks_b, vs_b, wi[:, :, 0], wi[:, :, 1], kp, gi, rel_bias)

    out = lax.map(one_block, (qb, qpos, wblk, kwpos, gb))
    return jnp.moveaxis(out, 0, 1).reshape(B, S, NSA_HEADS * NSA_HD)


def nsa_sample(q, kv, gates, past_rows, win_buf, pe, wcmp, rel_bias):
    Bd, Td = q.shape[:2]
    past = past_rows.shape[1]
    wb = win_buf.shape[1]
    rows = jnp.concatenate([past_rows.astype(kv.dtype), kv[:, :, :4]], 1)
    T = past + Td
    kc = compress_kv(rows[:, :, 0], pe[0], wcmp[0])
    vc = compress_kv(rows[:, :, 1], pe[1], wcmp[1])
    cend = jnp.arange(kc.shape[1]) * CMP_BLOCK + (CMP_BLOCK - 1)
    nsb = -(-T // SEL_BLOCK)
    sel = jnp.pad(rows[:, :, 2:4], ((0, 0), (0, nsb * SEL_BLOCK - T), (0, 0), (0, 0), (0, 0)))
    sel = sel.reshape(Bd, nsb, SEL_BLOCK, 2, NSA_KV_HEADS, NSA_HD)
    wall = jnp.concatenate([win_buf.astype(kv.dtype), kv[:, :, 4:6]], 1)
    kwpos = past - wb + jnp.arange(wb + Td)
    qpos = past + jnp.arange(Td)
    out = nsa_attend(q, qpos, kc, vc, cend, sel[:, :, :, 0], sel[:, :, :, 1],
                     wall[:, :, 0], wall[:, :, 1], kwpos, gates, rel_bias)
    return out, wall[:, -wb:]


def split_even(u):
    B, T = u.shape[:2]
    glu = u[..., :CONV_CH] * jax.nn.sigmoid(u[..., CONV_CH:2 * CONV_CH])
    o = 2 * CONV_CH
    q = u[..., o:o + NSA_HEADS * NSA_HD].reshape(B, T, NSA_HEADS, NSA_HD) * (NSA_HD ** -0.5)
    o += NSA_HEADS * NSA_HD
    kv = u[..., o:o + 6 * NSA_KV_HEADS * NSA_HD].reshape(B, T, 6, NSA_KV_HEADS, NSA_HD)
    o += 6 * NSA_KV_HEADS * NSA_HD
    gates = jax.nn.sigmoid(u[..., o:].reshape(B, T, NSA_HEADS, 3))
    return glu, q, kv, gates


def conformer_conv(xe, w, b, g, beta):
    y = lax.conv_general_dilated(xe, w[:, None, :].astype(xe.dtype), window_strides=(1,), padding='VALID',
                                 dimension_numbers=('NWC', 'WIO', 'NWC'), feature_group_count=CONV_CH) + b
    yf = y.astype(jnp.float32)
    mu = jnp.mean(yf, -1, keepdims=True)
    var = jnp.mean(jnp.square(yf - mu), -1, keepdims=True)
    yn = (yf - mu) * lax.rsqrt(var + LN_EPS) * g + beta
    return jax.nn.silu(yn).astype(xe.dtype)


def split_odd(u, b_i, b_f):
    f32 = jnp.float32
    B, T = u.shape[:2]
    hq = MLSTM_HEADS * MLSTM_DK
    hv = MLSTM_HEADS * MLSTM_DV

    def heads(a, d):
        return jnp.transpose(a.reshape(B, T, MLSTM_HEADS, d), (0, 2, 1, 3)).astype(f32)

    q = heads(u[..., :hq], MLSTM_DK)
    k = heads(u[..., hq:2 * hq], MLSTM_DK) * (MLSTM_DK ** -0.5)
    v = heads(u[..., 2 * hq:2 * hq + hv], MLSTM_DV)
    og = jax.nn.sigmoid(u[..., 2 * hq + hv:2 * hq + 2 * hv].astype(f32))
    gi = u[..., 2 * hq + 2 * hv:].astype(f32).reshape(B, T, 2, MLSTM_HEADS)
    ig = jnp.transpose(gi[:, :, 0] + b_i, (0, 2, 1))
    lf = jnp.transpose(jax.nn.log_sigmoid(gi[:, :, 1] + b_f), (0, 2, 1))
    return q, k, v, og, ig, lf


def mlstm_chunk(carry, inp):
    C, n, m = carry
    q, k, v, ig, lf = inp
    L = q.shape[2]
    b = jnp.cumsum(lf, axis=-1)
    causal = jnp.tril(jnp.ones((L, L), dtype=bool))
    dmat = jnp.where(causal, b[..., :, None] - b[..., None, :] + ig[..., None, :], NEG)
    inter = b + m[..., None]
    mt = jnp.maximum(inter, jnp.max(dmat, -1))
    wm = jnp.exp(dmat - mt[..., None])
    a = jnp.exp(inter - mt)
    wqk = wm * jnp.einsum('bhtd,bhsd->bhts', q, k)
    num = a[..., None] * jnp.einsum('bhvd,bhtd->bhtv', C, q) + jnp.einsum('bhts,bhsv->bhtv', wqk, v)
    den = a * jnp.einsum('bhd,bhtd->bht', n, q) + jnp.sum(wqk, -1)
    h = num / jnp.maximum(jnp.abs(den), jnp.exp(-mt))[..., None]
    m_new = mt[..., -1]
    a_end = jnp.exp(b[..., -1] + m - m_new)
    w_s = jnp.exp(b[..., -1:] - b + ig - m_new[..., None])
    C_new = a_end[..., None, None] * C + jnp.einsum('bhsv,bhsd->bhvd', v * w_s[..., None], k)
    n_new = a_end[..., None] * n + jnp.einsum('bhs,bhsd->bhd', w_s, k)
    return (C_new, n_new, m_new), h


def mlstm_prompt(q, k, v, ig, lf):
    B, H, S, _ = q.shape
    nch = S // MLSTM_CHUNK

    def to_chunks(a):
        return jnp.moveaxis(a.reshape(a.shape[:2] + (nch, MLSTM_CHUNK) + a.shape[3:]), 2, 0)

    init = (jnp.zeros((B, H, MLSTM_DV, MLSTM_DK), jnp.float32), jnp.zeros((B, H, MLSTM_DK), jnp.float32),
            jnp.full((B, H), NEG, jnp.float32))
    carry, h = lax.scan(mlstm_chunk, init, (to_chunks(q), to_chunks(k), to_chunks(v), to_chunks(ig), to_chunks(lf)))
    h = jnp.moveaxis(h, 0, 2).reshape(B, H, S, MLSTM_DV)
    return h, carry


def mlstm_out(h, og, gain, w_out, dtype):
    B, H, T, _ = h.shape
    hn = h * lax.rsqrt(jnp.mean(h * h, -1, keepdims=True) + RMS_EPS)
    hn = jnp.transpose(hn, (0, 2, 1, 3)).reshape(B, T, H * MLSTM_DV) * gain.astype(jnp.float32) * og
    return hn.astype(dtype) @ w_out


def cross_attn(h, kv, wq, wo):
    B, T = h.shape[:2]
    q = (h @ wq).reshape(B, T, X_HEADS, X_HD) * (X_HD ** -0.5)
    s = jnp.einsum('bqhd,bmhd->bhqm', q, kv[:, :, 0].astype(q.dtype)).astype(jnp.float32)
    p = jax.nn.softmax(s, axis=-1).astype(q.dtype)
    o = jnp.einsum('bhqm,bmhd->bqhd', p, kv[:, :, 1].astype(q.dtype)).reshape(B, T, X_HEADS * X_HD)
    return o @ wo


def swiglu(h, w_gu, w_dn):
    g, u = jnp.split(h @ w_gu, 2, axis=-1)
    return (jax.nn.silu(g) * u) @ w_dn


def moe_ffn(h, w_r, b_r, w_gu, w_dn):
    f32 = jnp.float32
    shp = h.shape
    hf = h.reshape(-1, shp[-1])
    logits = (hf @ w_r).astype(f32) + b_r.astype(f32)
    top_v, top_i = lax.top_k(logits, TOP_K)
    wts = jax.nn.softmax(top_v, axis=-1)
    comb = jnp.sum(jax.nn.one_hot(top_i, N_EXPERTS, dtype=f32) * wts[..., None], axis=1)
    y = jnp.zeros(hf.shape, f32)
    for ex in range(N_EXPERTS):
        g, u = jnp.split(hf @ w_gu[ex], 2, axis=-1)
        y = y + comb[:, ex:ex + 1] * ((jax.nn.silu(g) * u) @ w_dn[ex]).astype(f32)
    return y.astype(h.dtype).reshape(shp)


def setup_inputs(seed: int = 0) -> dict:
    key = jax.random.key(seed)
    ks = iter(jax.random.split(key, 64))

    def nrm(shape, scale):
        return scale * jax.random.normal(next(ks), shape, jnp.float32)

    def gain(shape):
        return 1.0 + nrm(shape, 0.01)

    n_pages = PAST_LEN // PAGE_SIZE
    n_used = DEC_BATCH * n_pages
    n_pool = n_used + (n_used + 3) // 4
    wb = min(WINDOW, PAST_LEN)
    D = D_MODEL
    inp = {}
    inp['x_prompt'] = nrm((BATCH, SEQ, D), 1.0)
    inp['x_sample'] = nrm((DEC_BATCH, DEC_SEQ, D), 1.0)
    inp['mem_prompt'] = nrm((BATCH, MEM_TOKENS, D), 1.0)
    inp['cache_conv'] = nrm((N_EVEN, DEC_BATCH, CONV_WIDTH - 1, CONV_CH), 0.5)
    inp['cache_nsa_pages'] = nrm((N_EVEN, n_pool, PAGE_SIZE, 4, NSA_KV_HEADS, NSA_HD), 1.0)
    inp['cache_nsa_window'] = nrm((N_EVEN, DEC_BATCH, wb, 2, NSA_KV_HEADS, NSA_HD), 1.0)
    inp['state_mlstm_c'] = nrm((N_ODD, DEC_BATCH, MLSTM_HEADS, MLSTM_DV, MLSTM_DK), 0.1)
    inp['state_mlstm_n'] = nrm((N_ODD, DEC_BATCH, MLSTM_HEADS, MLSTM_DK), 0.1)
    inp['state_mlstm_m'] = nrm((N_ODD, DEC_BATCH, MLSTM_HEADS), 1.0)
    inp['cache_mem_kv'] = nrm((DEPTH, DEC_BATCH, MEM_TOKENS, 2, X_HEADS, X_HD), 1.0)
    inp['page_table'] = jax.random.permutation(next(ks), n_pool)[:n_used].reshape(DEC_BATCH, n_pages).astype(jnp.int32)
    inp['rel_bias'] = nrm((NUM_BUCKETS, NSA_HEADS), 0.1)
    inp['norm_mix'] = gain((DEPTH, D))
    inp['norm_xattn'] = gain((DEPTH, D))
    inp['norm_mem'] = gain((DEPTH, D))
    inp['norm_ffn'] = gain((DEPTH, D))
    inp['norm_final'] = gain((D,))
    inp['w_in_even'] = nrm((N_EVEN, D, IN_EVEN), D ** -0.5)
    inp['w_out_even'] = nrm((N_EVEN, MIX_WIDTH, D), MIX_WIDTH ** -0.5)
    inp['conv_w'] = nrm((N_EVEN, CONV_WIDTH, CONV_CH), CONV_WIDTH ** -0.5)
    inp['conv_b'] = nrm((N_EVEN, CONV_CH), 0.01)
    inp['conv_ln_g'] = gain((N_EVEN, CONV_CH))
    inp['conv_ln_b'] = nrm((N_EVEN, CONV_CH), 0.01)
    inp['nsa_cmp_pe'] = nrm((N_EVEN, 2, CMP_BLOCK, NSA_HD), 0.1)
    inp['nsa_cmp_w'] = nrm((N_EVEN, 2, CMP_BLOCK, NSA_HD, NSA_HD), (CMP_BLOCK * NSA_HD) ** -0.5)
    inp['w_in_odd'] = nrm((N_ODD, D, IN_ODD), D ** -0.5)
    inp['mlstm_b_i'] = nrm((N_ODD, MLSTM_HEADS), 0.1)
    inp['mlstm_b_f'] = jnp.linspace(3.0, 6.0, MLSTM_HEADS)[None, :] + nrm((N_ODD, MLSTM_HEADS), 0.01)
    inp['mlstm_norm'] = gain((N_ODD, MLSTM_HEADS * MLSTM_DV))
    inp['w_out_odd'] = nrm((N_ODD, MLSTM_HEADS * MLSTM_DV, D), (MLSTM_HEADS * MLSTM_DV) ** -0.5)
    inp['xattn_wq'] = nrm((DEPTH, D, X_HEADS * X_HD), D ** -0.5)
    inp['xattn_wkv'] = nrm((DEPTH, D, 2 * X_HEADS * X_HD), D ** -0.5)
    inp['xattn_wo'] = nrm((DEPTH, X_HEADS * X_HD, D), (X_HEADS * X_HD) ** -0.5)
    inp['ffn_w_gu'] = nrm((N_EVEN, D, 2 * D_FF), D ** -0.5)
    inp['ffn_w_dn'] = nrm((N_EVEN, D_FF, D), D_FF ** -0.5)
    inp['router_w'] = nrm((N_ODD, D, N_EXPERTS), D ** -0.5)
    inp['router_b'] = nrm((N_ODD, N_EXPERTS), 0.01)
    inp['expert_w_gu'] = nrm((N_ODD, N_EXPERTS, D, 2 * D_FF_EXPERT), D ** -0.5)
    inp['expert_w_dn'] = nrm((N_ODD, N_EXPERTS, D_FF_EXPERT, D), D_FF_EXPERT ** -0.5)
    return inp


def reference(x_prompt, x_sample, mem_prompt, cache_conv, cache_nsa_pages, cache_nsa_window,
              state_mlstm_c, state_mlstm_n, state_mlstm_m, cache_mem_kv, page_table, rel_bias,
              norm_mix, norm_xattn, norm_mem, norm_ffn, norm_final, w_in_even, w_out_even,
              conv_w, conv_b, conv_ln_g, conv_ln_b, nsa_cmp_pe, nsa_cmp_w, w_in_odd, mlstm_b_i,
              mlstm_b_f, mlstm_norm, w_out_odd, xattn_wq, xattn_wkv, xattn_wo, ffn_w_gu, ffn_w_dn,
              router_w, router_b, expert_w_gu, expert_w_dn):
    B, S, _ = x_prompt.shape
    Bd = x_sample.shape[0]
    past = page_table.shape[1] * PAGE_SIZE
    wb = cache_nsa_window.shape[2]
    xp, xs = x_prompt, x_sample
    conv_p, conv_s, nsa_p, nsa_s, win_p, win_s = [], [], [], [], [], []
    mc_p, mc_s, mn_p, mn_s, mm_p, mm_s, memkv_p = [], [], [], [], [], [], []
    for l in range(DEPTH):
        li = l // 2
        if l % 2 == 0:
            glu, q, kv, gates = split_even(rms_norm(xp, norm_mix[l]) @ w_in_even[li])
            conv_ext = jnp.concatenate([jnp.zeros((B, CONV_WIDTH - 1, CONV_CH), glu.dtype), glu], 1)
            a_out = conformer_conv(conv_ext, conv_w[li], conv_b[li], conv_ln_g[li], conv_ln_b[li])
            b_out = nsa_prompt(q, kv, gates, nsa_cmp_pe[li], nsa_cmp_w[li], rel_bias)
            xp = xp + jnp.concatenate([a_out, b_out], -1) @ w_out_even[li]
            conv_p.append(conv_ext[:, -(CONV_WIDTH - 1):])
            nsa_p.append(kv[:, :, :4])
            win = kv[:, -min(wb, S):, 4:6]
            if win.shape[1] < wb:
                win = jnp.pad(win, ((0, 0), (wb - win.shape[1], 0), (0, 0), (0, 0), (0, 0)))
            win_p.append(win)
            glu, q, kv, gates = split_even(rms_norm(xs, norm_mix[l]) @ w_in_even[li])
            conv_ext = jnp.concatenate([cache_conv[li].astype(glu.dtype), glu], 1)
            a_out = conformer_conv(conv_ext, conv_w[li], conv_b[li], conv_ln_g[li], conv_ln_b[li])
            past_rows = cache_nsa_pages[li][page_table].reshape(Bd, past, 4, NSA_KV_HEADS, NSA_HD)
            b_out, win = nsa_sample(q, kv, gates, past_rows, cache_nsa_window[li], nsa_cmp_pe[li],
                                    nsa_cmp_w[li], rel_bias)
            xs = xs + jnp.concatenate([a_out, b_out], -1) @ w_out_even[li]
            conv_s.append(conv_ext[:, -(CONV_WIDTH - 1):])
            nsa_s.append(kv[:, :, :4])
            win_s.append(win)
        else:
            q, k, v, og, ig, lf = split_odd(rms_norm(xp, norm_mix[l]) @ w_in_odd[li], mlstm_b_i[li], mlstm_b_f[li])
            h, (c_new, n_new, m_new) = mlstm_prompt(q, k, v, ig, lf)
            xp = xp + mlstm_out(h, og, mlstm_norm[li], w_out_odd[li], xp.dtype)
            mc_p.append(c_new)
            mn_p.append(n_new)
            mm_p.append(m_new)
            q, k, v, og, ig, lf = split_odd(rms_norm(xs, norm_mix[l]) @ w_in_odd[li], mlstm_b_i[li], mlstm_b_f[li])
            init = (state_mlstm_c[li].astype(jnp.float32), state_mlstm_n[li].astype(jnp.float32),
                    state_mlstm_m[li].astype(jnp.float32))
            (c_new, n_new, m_new), h = mlstm_chunk(init, (q, k, v, ig, lf))
            xs = xs + mlstm_out(h, og, mlstm_norm[li], w_out_odd[li], xs.dtype)
            mc_s.append(c_new)
            mn_s.append(n_new)
            mm_s.append(m_new)
        mkv = (rms_norm(mem_prompt, norm_mem[l]) @ xattn_wkv[l]).reshape(B, MEM_TOKENS, 2, X_HEADS, X_HD)
        memkv_p.append(mkv)
        xp = xp + cross_attn(rms_norm(xp, norm_xattn[l]), mkv, xattn_wq[l], xattn_wo[l])
        xs = xs + cross_attn(rms_norm(xs, norm_xattn[l]), cache_mem_kv[l], xattn_wq[l], xattn_wo[l])
        if l % 2 == 0:
            xp = xp + swiglu(rms_norm(xp, norm_ffn[l]), ffn_w_gu[li], ffn_w_dn[li])
            xs = xs + swiglu(rms_norm(xs, norm_ffn[l]), ffn_w_gu[li], ffn_w_dn[li])
        else:
            xp = xp + moe_ffn(rms_norm(xp, norm_ffn[l]), router_w[li], router_b[li], expert_w_gu[li], expert_w_dn[li])
            xs = xs + moe_ffn(rms_norm(xs, norm_ffn[l]), router_w[li], router_b[li], expert_w_gu[li], expert_w_dn[li])
    y_prompt = rms_norm(xp, norm_final)
    y_sample = rms_norm(xs, norm_final)
    return (y_prompt, y_sample, jnp.stack(conv_p), jnp.stack(conv_s), jnp.stack(nsa_p), jnp.stack(nsa_s),
            jnp.stack(win_p), jnp.stack(win_s), jnp.stack(mc_p), jnp.stack(mc_s), jnp.stack(mn_p),
            jnp.stack(mn_s), jnp.stack(mm_p), jnp.stack(mm_s), jnp.stack(memkv_p))
```

```python
import functools
import math

import jax
import jax.numpy as jnp
import numpy as np
from jax import lax
from jax.experimental import pallas as pl
from jax.experimental.pallas import tpu as pltpu

F32 = jnp.float32
BF16 = jnp.bfloat16

PAGE_SIZE = 128
CONV_WIDTH = 31
NSA_HEADS = 8
NSA_KV_HEADS = 2
NSA_GROUP = NSA_HEADS // NSA_KV_HEADS
NSA_HD = 64
CMP_BLOCK = 32
SEL_BLOCK = 64
SEL_RATIO = SEL_BLOCK // CMP_BLOCK
TOP_N = 16
WINDOW = 512
Q_BLOCK = 128
FORCE_SCORE = 1.0e4
NUM_BUCKETS = 32
MAX_DISTANCE = 1024
MLSTM_HEADS = 4
X_HEADS = 4
N_EXPERTS = 8
TOP_K = 2
RMS_EPS = 1e-6
LN_EPS = 1e-5
NEG = -1e30

LANES = 128
SUBLANES = 8
VMEM_LIMIT = 56 * 1024 * 1024
MLSTM_CHUNK = 256


def _cparams(sem, vmem=None):
    return pltpu.CompilerParams(dimension_semantics=sem, vmem_limit_bytes=vmem)


def _rms(x, g):
    return x * lax.rsqrt(jnp.mean(x * x, -1, keepdims=True) + RMS_EPS) * g


def _dot(a, b):
    return jnp.dot(a, b, preferred_element_type=F32)


def _dot_nt(a, b):
    return lax.dot_general(a, b, (((1,), (1,)), ((), ())), preferred_element_type=F32)


def _dot_tn(a, b):
    return lax.dot_general(a, b, (((0,), (0,)), ((), ())), preferred_element_type=F32)


def _full(shape):
    n = len(shape)
    return pl.BlockSpec(shape, lambda *_: (0,) * n)


def _row_tile(m, pref):
    t = min(pref, m)
    while m % t:
        t //= 2
    return t


def _rmsnorm_kernel(x_ref, g_ref, o_ref):
    o_ref[...] = _rms(x_ref[...], g_ref[...])


def rmsnorm(x, g):
    m, d = x.shape
    tm = _row_tile(m, 1024)
    return pl.pallas_call(
        _rmsnorm_kernel, grid=(m // tm,),
        in_specs=[pl.BlockSpec((tm, d), lambda i: (i, 0)), _full((1, d))],
        out_specs=pl.BlockSpec((tm, d), lambda i: (i, 0)),
        out_shape=jax.ShapeDtypeStruct((m, d), F32),
        compiler_params=_cparams(("parallel",)), name="rmsnorm",
    )(x, g.reshape(1, d))


def _outproj_kernel(*refs, n_in):
    x_ref = refs[0]
    a_refs = refs[1:1 + n_in]
    w_refs = refs[1 + n_in:1 + 2 * n_in]
    o_ref = refs[1 + 2 * n_in]
    acc = x_ref[...]
    for a_ref, w_ref in zip(a_refs, w_refs):
        acc = acc + _dot(a_ref[...], w_ref[...])
    o_ref[...] = acc


def outproj(x, acts, ws):
    m, d = x.shape
    tm = _row_tile(m, 512)
    n_in = len(acts)
    in_specs = [pl.BlockSpec((tm, d), lambda i: (i, 0))]
    in_specs += [pl.BlockSpec((tm, a.shape[1]), lambda i: (i, 0)) for a in acts]
    in_specs += [_full(w.shape) for w in ws]
    return pl.pallas_call(
        functools.partial(_outproj_kernel, n_in=n_in), grid=(m // tm,),
        in_specs=in_specs, out_specs=pl.BlockSpec((tm, d), lambda i: (i, 0)),
        out_shape=jax.ShapeDtypeStruct((m, d), F32),
        compiler_params=_cparams(("parallel",)), name="outproj",
    )(x, *acts, *ws)


def _inproj_even_kernel(x_ref, g_ref, w_ref, glu_ref, q_ref, kv03_ref, kvb_ref, kv45_ref, gate_ref, *, cc, qd, kvd):
    xn = _rms(x_ref[...], g_ref[...]).astype(BF16)

    def mm(lo, hi):
        return _dot(xn, w_ref[:, lo:hi])

    o = 0
    a = mm(o, o + cc)
    b = mm(o + cc, o + 2 * cc)
    glu_ref[...] = a * jax.nn.sigmoid(b)
    o += 2 * cc
    q_ref[...] = (mm(o, o + qd) * (NSA_HD ** -0.5)).astype(BF16)
    o += qd
    kv03 = mm(o, o + 4 * kvd)
    kv03_ref[...] = kv03
    o += 4 * kvd
    kv45 = mm(o, o + 2 * kvd)
    kv45_ref[...] = kv45
    o += 2 * kvd
    kvb_ref[...] = jnp.concatenate([kv03[:, 2 * kvd:], kv45], axis=1).astype(BF16)
    gate_ref[...] = jax.nn.sigmoid(mm(o, o + LANES))


def inproj_even(x, g, w_pad, cc):
    m, d = x.shape
    qd = NSA_HEADS * NSA_HD
    kvd = NSA_KV_HEADS * NSA_HD
    tm = _row_tile(m, 256)
    row = lambda n: pl.BlockSpec((tm, n), lambda i: (i, 0))
    outs = [(cc, F32), (qd, BF16), (4 * kvd, F32), (4 * kvd, BF16), (2 * kvd, F32), (LANES, F32)]
    return pl.pallas_call(
        functools.partial(_inproj_even_kernel, cc=cc, qd=qd, kvd=kvd), grid=(m // tm,),
        in_specs=[row(d), _full((1, d)), _full(w_pad.shape)],
        out_specs=[row(n) for n, _ in outs],
        out_shape=[jax.ShapeDtypeStruct((m, n), t) for n, t in outs],
        compiler_params=_cparams(("parallel",)), name="inproj_even",
    )(x, g.reshape(1, d), w_pad)


def _conv_post(y, lg, lb):
    mu = jnp.mean(y, -1, keepdims=True)
    var = jnp.mean(jnp.square(y - mu), -1, keepdims=True)
    yn = (y - mu) * lax.rsqrt(var + LN_EPS) * lg + lb
    return yn * jax.nn.sigmoid(yn)


CONV_SUB = 64
CONV_PAD = 32


def _conv_prompt_kernel(glu_ref, cw_ref, cb_ref, lg_ref, lb_ref, o_ref, ext_ref, y_ref, *, ts, s):
    i = pl.program_id(1)
    c = glu_ref.shape[-1]

    @pl.when(i == 0)
    def _():
        ext_ref[0:CONV_PAD, :] = jnp.zeros((CONV_PAD, c), F32)
        ext_ref[CONV_PAD:CONV_PAD + s, :] = glu_ref[0]
        ext_ref[CONV_PAD + s:CONV_PAD + s + SUBLANES, :] = jnp.zeros((SUBLANES, c), F32)

    lead = CONV_PAD - (CONV_WIDTH - 1)
    span = CONV_SUB + CONV_PAD

    def sub(j, carry):
        r0 = pl.multiple_of(i * ts + j * CONV_SUB, CONV_SUB)
        for c0 in range(0, c, LANES):
            xw = ext_ref[pl.ds(r0, span + SUBLANES), c0:c0 + LANES]
            acc = jnp.zeros((CONV_SUB, LANES), F32) + cb_ref[:, c0:c0 + LANES]
            for r in range(SUBLANES):
                xr = xw[r:r + span, :]
                for a in range(span // SUBLANES):
                    w = SUBLANES * a + r - lead
                    if 0 <= w < CONV_WIDTH:
                        acc = acc + xr[SUBLANES * a:SUBLANES * a + CONV_SUB, :] * cw_ref[w:w + 1, c0:c0 + LANES]
            y_ref[:, c0:c0 + LANES] = acc
        o_ref[0, pl.ds(pl.multiple_of(j * CONV_SUB, CONV_SUB), CONV_SUB), :] = _conv_post(
            y_ref[...], lg_ref[...], lb_ref[...]).astype(o_ref.dtype)
        return carry

    lax.fori_loop(0, ts // CONV_SUB, sub, 0)


def conv_prompt(glu, cw, cb, lg, lb, b, s):
    c = glu.shape[-1]
    ts = _row_tile(s, 256)
    vec = lambda a: a.reshape(1, c)
    out = pl.pallas_call(
        functools.partial(_conv_prompt_kernel, ts=ts, s=s), grid=(b, s // ts),
        in_specs=[pl.BlockSpec((1, s, c), lambda bi, i: (bi, 0, 0)), _full((CONV_WIDTH, c)),
                  _full((1, c)), _full((1, c)), _full((1, c))],
        out_specs=pl.BlockSpec((1, ts, c), lambda bi, i: (bi, i, 0)),
        out_shape=jax.ShapeDtypeStruct((b, s, c), BF16),
        scratch_shapes=[pltpu.VMEM((CONV_PAD + s + SUBLANES, c), F32), pltpu.VMEM((CONV_SUB, c), F32)],
        compiler_params=_cparams(("parallel", "arbitrary")), name="conv_prompt",
    )(glu.reshape(b, s, c), cw, vec(cb), vec(lg), vec(lb))
    return out.reshape(b * s, c)


def _conv_sample_kernel(cache_ref, glu_ref, cw_ref, cb_ref, lg_ref, lb_ref, o_ref, st_ref):
    hist = CONV_WIDTH - 1
    cache = cache_ref[...]
    glu = glu_ref[...]
    y = jnp.sum(cache * cw_ref[0:hist, :][None], axis=1) + glu * cw_ref[hist:hist + 1, :] + cb_ref[...]
    o_ref[...] = _conv_post(y, lg_ref[...], lb_ref[...]).astype(o_ref.dtype)
    st_ref[:, 0:hist - 1, :] = cache[:, 1:hist, :]
    st_ref[:, hist - 1:hist, :] = glu[:, None, :]


def conv_sample(cache, glu, cw, cb, lg, lb):
    bd, hist, c = cache.shape
    vec = lambda a: a.reshape(1, c)
    return pl.pallas_call(
        _conv_sample_kernel,
        out_shape=[jax.ShapeDtypeStruct((bd, c), BF16), jax.ShapeDtypeStruct((bd, hist, c), F32)],
        name="conv_sample",
    )(cache, glu, cw, vec(cb), vec(lg), vec(lb))


def _rel_bucket(dist):
    n = jnp.maximum(dist, 0)
    max_exact = NUM_BUCKETS // 2
    nf = jnp.maximum(n, 1).astype(F32)
    large = max_exact + (jnp.log(nf / max_exact) / math.log(MAX_DISTANCE / max_exact)
                         * (NUM_BUCKETS - max_exact)).astype(jnp.int32)
    large = jnp.minimum(large, NUM_BUCKETS - 1)
    return jnp.where(n < max_exact, n, large)


def _bias_of(rel_bias, dist):
    return rel_bias[_rel_bucket(dist)].astype(F32)


def _compress_accumulate(load_rows, pe_ref, w_ref, nc):
    accs = []
    for slot in range(2):
        acc = jnp.zeros((nc, w_ref.shape[-1]), F32)
        for j in range(CMP_BLOCK):
            xj = load_rows(slot, j) + pe_ref[slot, j:j + 1, :]
            acc = acc + _dot(xj.astype(BF16), w_ref[slot, j])
        accs.append(acc)
    return jnp.concatenate(accs, axis=1)


def _compress_prompt_kernel(x_ref, pe_ref, w_ref, o_ref, *, nc, ncp, n_slots):
    acc = _compress_accumulate(
        lambda sl, j: x_ref[0, pl.ds(j * n_slots + sl, nc, stride=CMP_BLOCK * n_slots), :], pe_ref, w_ref, nc)
    o_ref[0, 0:nc, :] = acc
    if ncp > nc:
        o_ref[0, nc:ncp, :] = jnp.zeros((ncp - nc, acc.shape[1]), F32)


def compress_prompt(kv03, pe2, wbd, b, s, ncp):
    nc = s // CMP_BLOCK
    kvd = wbd.shape[-1]
    n_slots = kv03.shape[1] // kvd
    return pl.pallas_call(
        functools.partial(_compress_prompt_kernel, nc=nc, ncp=ncp, n_slots=n_slots), grid=(b,),
        in_specs=[pl.BlockSpec((1, s * n_slots, kvd), lambda bi: (bi, 0, 0)), _full(pe2.shape), _full(wbd.shape)],
        out_specs=pl.BlockSpec((1, ncp, 2 * kvd), lambda bi: (bi, 0, 0)),
        out_shape=jax.ShapeDtypeStruct((b, ncp, 2 * kvd), F32),
        compiler_params=_cparams(("parallel",)), name="compress_prompt",
    )(kv03.reshape(b, s * n_slots, kvd), pe2, wbd)


def _select_blocks(score, n_sel, n_cand):
    lane = lax.broadcasted_iota(jnp.int32, score.shape, 1)
    rank = jnp.zeros(score.shape, F32)
    for i in range(n_cand):
        col = score[:, 2 * i:2 * i + 1]
        beats = (col > score) | ((col == score) & (lane > 2 * i))
        rank = rank + beats.astype(F32)
    is_cand = ((lane % 2) == 0) & (lane < 2 * n_cand)
    return (is_cand & (rank < n_sel) & (score >= 0)).astype(F32), rank


def _pair_sum(imp):
    n = imp.shape[1]
    return imp + pltpu.roll(imp, n - 1, 1)


def _softmax_update(s, valid, v_tile, m, l, acc):
    g, r, c = s.shape
    s = jnp.where(valid[None], s, NEG)
    m_new = jnp.maximum(m, jnp.max(s, -1, keepdims=True))
    alpha = jnp.exp(m - m_new)
    p = jnp.where(valid[None], jnp.exp(s - m_new), 0.0)
    l = alpha * l + jnp.sum(p, -1, keepdims=True)
    pv = _dot(p.reshape(g * r, c).astype(BF16), v_tile).reshape(g, r, -1)
    return m_new, l, alpha * acc + pv


def _nsa_prompt_kernel(q_ref, gate_ref, kcv_ref, kvb_ref, biasc_ref, btile_ref, expand_ref, o_ref,
                       *, nc, nsb, n_sel, qb):
    qi = pl.program_id(1)
    g, hd, kvd = NSA_GROUP, NSA_HD, NSA_KV_HEADS * NSA_HD
    q = q_ref[0]
    gates = gate_ref[0]
    ncp = kcv_ref.shape[1]
    kc = kcv_ref[0, :, 0:kvd].astype(BF16)
    vc = kcv_ref[0, :, kvd:2 * kvd].astype(BF16)
    t_pos = qi * qb + lax.broadcasted_iota(jnp.int32, (qb, 1), 0)
    lane_c = lax.broadcasted_iota(jnp.int32, (qb, ncp), 1)
    mask_c = (t_pos >= lane_c * CMP_BLOCK + (CMP_BLOCK - 1)) & (lane_c < nc)
    key_lane = lax.broadcasted_iota(jnp.int32, (qb, qb), 1)
    win_blocks = WINDOW // qb
    outs = []
    for k in range(NSA_KV_HEADS):
        zeros = jnp.zeros((qb, hd), BF16)
        pieces = []
        for gi in range(g):
            h = k * g + gi
            qh = q[:, h * hd:(h + 1) * hd]
            parts = [zeros] * NSA_KV_HEADS
            parts[k] = qh
            pieces.append(jnp.concatenate(parts, axis=1))
        qpad = jnp.concatenate(pieces, axis=0)
        s_c = _dot_nt(qpad, kc).reshape(g, qb, ncp) + biasc_ref[k * g:(k + 1) * g]
        s_c = jnp.where(mask_c[None], s_c, NEG)
        e_c = jnp.where(mask_c[None], jnp.exp(s_c - jnp.max(s_c, -1, keepdims=True)), 0.0)
        den = jnp.sum(e_c, -1, keepdims=True)
        p_c = e_c / jnp.where(den > 0, den, 1.0)
        o_c = _dot(p_c.reshape(g * qb, ncp).astype(BF16), vc).reshape(g, qb, kvd)
        imp = _pair_sum(jnp.sum(p_c, axis=0))
        blk = lane_c // 2
        cur = t_pos // SEL_BLOCK
        forced = (blk == 0) | (blk == cur) | (blk == cur - 1)
        score = jnp.where(forced, FORCE_SCORE, jnp.where(blk <= cur, imp, -1.0))
        score = jnp.where(((lane_c % 2) == 0) & (lane_c < 2 * nsb), score, -2.0)
        sel, _ = _select_blocks(score, n_sel, nsb)
        sel_b = sel.astype(BF16)

        init = (jnp.full((g, qb, 1), NEG, F32), jnp.zeros((g, qb, 1), F32), jnp.zeros((g, qb, kvd), F32))

        def sel_step(kt, carry):
            r0 = pl.multiple_of(kt * qb, qb)
            k_t = kvb_ref[0, pl.ds(r0, qb), 0:kvd]
            v_t = kvb_ref[0, pl.ds(r0, qb), kvd:2 * kvd]
            s = _dot_nt(qpad, k_t).reshape(g, qb, qb) + btile_ref[qi - kt, k * g:(k + 1) * g]
            chosen = _dot(sel_b, expand_ref[kt]) > 0.5
            valid = chosen & (t_pos >= r0 + key_lane)
            return _softmax_update(s, valid, v_t, *carry)

        _, l_s, acc_s = lax.fori_loop(0, qi + 1, sel_step, init)
        o_s = acc_s / jnp.where(l_s > 0, l_s, 1.0)

        def win_step(kt, carry):
            r0 = pl.multiple_of(kt * qb, qb)
            k_t = kvb_ref[0, pl.ds(r0, qb), 2 * kvd:3 * kvd]
            v_t = kvb_ref[0, pl.ds(r0, qb), 3 * kvd:4 * kvd]
            s = _dot_nt(qpad, k_t).reshape(g, qb, qb) + btile_ref[qi - kt, k * g:(k + 1) * g]
            dist = t_pos - (r0 + key_lane)
            valid = (dist >= 0) & (dist < WINDOW)
            return _softmax_update(s, valid, v_t, *carry)

        _, l_w, acc_w = lax.fori_loop(jnp.maximum(qi - win_blocks, 0), qi + 1, win_step, init)
        o_w = acc_w / jnp.where(l_w > 0, l_w, 1.0)
        for gi in range(g):
            h = k * g + gi
            mix = (gates[:, 3 * h:3 * h + 1] * o_c[gi] + gates[:, 3 * h + 1:3 * h + 2] * o_s[gi]
                   + gates[:, 3 * h + 2:3 * h + 3] * o_w[gi])
            outs.append(mix[:, k * hd:(k + 1) * hd])
    o_ref[0] = jnp.concatenate(outs, axis=1).astype(o_ref.dtype)


def nsa_prompt(q, gates, kcv, kvb, rel_bias, b, s):
    qb = Q_BLOCK
    nq = s // qb
    nc = s // CMP_BLOCK
    ncp = kcv.shape[1]
    nsb = s // SEL_BLOCK
    n_sel = min(TOP_N, nsb)
    hq = q.shape[1]
    t = jnp.arange(s)[:, None]
    cend = jnp.arange(ncp)[None, :] * CMP_BLOCK + (CMP_BLOCK - 1)
    biasc = jnp.transpose(_bias_of(rel_bias, t - cend), (2, 0, 1))
    r = jnp.arange(qb)
    dist = jnp.arange(nq)[:, None, None] * qb + r[None, :, None] - r[None, None, :]
    btile = jnp.transpose(_bias_of(rel_bias, dist), (0, 3, 1, 2))
    lane = np.arange(ncp)[None, :, None]
    key = np.arange(nq)[:, None, None] * qb + np.arange(qb)[None, None, :]
    expand = jnp.asarray(((lane % 2 == 0) & (lane // 2 == key // SEL_BLOCK)).astype(np.float32), BF16)
    out = pl.pallas_call(
        functools.partial(_nsa_prompt_kernel, nc=nc, nsb=nsb, n_sel=n_sel, qb=qb), grid=(b, nq),
        in_specs=[pl.BlockSpec((1, qb, hq), lambda bi, i: (bi, i, 0)),
                  pl.BlockSpec((1, qb, LANES), lambda bi, i: (bi, i, 0)),
                  pl.BlockSpec((1, ncp, kcv.shape[2]), lambda bi, i: (bi, 0, 0)),
                  pl.BlockSpec((1, s, kvb.shape[1]), lambda bi, i: (bi, 0, 0)),
                  pl.BlockSpec((NSA_HEADS, qb, ncp), lambda bi, i: (0, i, 0)),
                  _full(btile.shape), _full(expand.shape)],
        out_specs=pl.BlockSpec((1, qb, hq), lambda bi, i: (bi, i, 0)),
        out_shape=jax.ShapeDtypeStruct((b, s, hq), BF16),
        compiler_params=_cparams(("parallel", "arbitrary"), VMEM_LIMIT), name="nsa_prompt",
    )(q.reshape(b, s, hq), gates.reshape(b, s, LANES), kcv, kvb.reshape(b, s, -1), biasc, btile, expand)
    return out.reshape(b * s, hq)


def _round_up(x, m):
    return (x + m - 1) // m * m


def prep_even(w_in, pe, wc):
    d, n = w_in.shape
    n_pad = _round_up(n - 3 * NSA_HEADS, LANES) + LANES
    w_pad = jnp.zeros((d, n_pad), BF16).at[:, :n].set(w_in.astype(BF16))
    hd = NSA_HD
    pe2 = jnp.tile(pe, (1, 1, NSA_KV_HEADS))
    wbd = jnp.zeros((2, CMP_BLOCK, NSA_KV_HEADS * hd, NSA_KV_HEADS * hd), F32)
    for i in range(NSA_KV_HEADS):
        wbd = wbd.at[:, :, i * hd:(i + 1) * hd, i * hd:(i + 1) * hd].set(wc)
    return dict(w_in=w_pad, pe2=pe2, wbd=wbd.astype(BF16))


def _memkv_kernel(x_ref, g_ref, w_ref, o_ref, ob_ref):
    y = _dot(_rms(x_ref[...], g_ref[...]).astype(BF16), w_ref[...])
    o_ref[...] = y
    ob_ref[...] = y.astype(BF16)


def memkv(mem, g, w):
    m, d = mem.shape
    n = w.shape[1]
    tm = _row_tile(m, 256)
    return pl.pallas_call(
        _memkv_kernel, grid=(m // tm,),
        in_specs=[pl.BlockSpec((tm, d), lambda i: (i, 0)), _full((1, d)), _full(w.shape)],
        out_specs=[pl.BlockSpec((tm, n), lambda i: (i, 0))] * 2,
        out_shape=[jax.ShapeDtypeStruct((m, n), F32), jax.ShapeDtypeStruct((m, n), BF16)],
        compiler_params=_cparams(("parallel",)), name="memkv",
    )(mem, g.reshape(1, d), w)


def _xattn_core(q, kv, hd):
    nh = q.shape[1] // hd
    outs = []
    for h in range(nh):
        s = _dot_nt(q[:, h * hd:(h + 1) * hd], kv[:, h * hd:(h + 1) * hd])
        e = jnp.exp(s - jnp.max(s, -1, keepdims=True))
        p = e / jnp.sum(e, -1, keepdims=True)
        outs.append(_dot(p.astype(BF16), kv[:, (nh + h) * hd:(nh + h + 1) * hd]))
    return jnp.concatenate(outs, axis=1).astype(BF16)


def _xattn_prompt_kernel(x_ref, g_ref, wq_ref, kv_ref, wo_ref, o_ref, *, hd):
    x = x_ref[0]
    q = (_dot(_rms(x, g_ref[...]).astype(BF16), wq_ref[...]) * (hd ** -0.5)).astype(BF16)
    o = _xattn_core(q, kv_ref[0], hd)
    o_ref[0] = x + _dot(o, wo_ref[...])


def xattn_prompt(x, g, wq, kvb, wo, b, s):
    d = x.shape[1]
    mt = kvb.shape[0] // b
    tm = _row_tile(s, 512)
    out = pl.pallas_call(
        functools.partial(_xattn_prompt_kernel, hd=d // X_HEADS), grid=(b, s // tm),
        in_specs=[pl.BlockSpec((1, tm, d), lambda bi, i: (bi, i, 0)), _full((1, d)), _full(wq.shape),
                  pl.BlockSpec((1, mt, kvb.shape[1]), lambda bi, i: (bi, 0, 0)), _full(wo.shape)],
        out_specs=pl.BlockSpec((1, tm, d), lambda bi, i: (bi, i, 0)),
        out_shape=jax.ShapeDtypeStruct((b, s, d), F32),
        compiler_params=_cparams(("parallel", "parallel"), VMEM_LIMIT), name="xattn_prompt",
    )(x.reshape(b, s, d), g.reshape(1, d), wq, kvb.reshape(b, mt, -1), wo)
    return out.reshape(b * s, d)


def _xattn_sample_kernel(x_ref, g_ref, wq_ref, kv_ref, wo_ref, o_ref, q_scr, a_scr, *, hd):
    bi = pl.program_id(0)
    nb = pl.num_programs(0)
    nh = wq_ref.shape[1] // hd

    @pl.when(bi == 0)
    def _():
        q_scr[...] = _dot(_rms(x_ref[...], g_ref[...]).astype(BF16), wq_ref[...]) * (hd ** -0.5)

    q = jnp.broadcast_to(q_scr[pl.ds(bi, 1), :], (SUBLANES, q_scr.shape[1])).astype(BF16)
    kv = kv_ref[0].astype(BF16)
    a_scr[pl.ds(bi, 1), :] = _xattn_core(q, kv, hd)[0:1, :].astype(F32)

    @pl.when(bi == nb - 1)
    def _():
        o_ref[...] = x_ref[...] + _dot(a_scr[...].astype(BF16), wo_ref[...])


def xattn_sample(x, g, wq, kv_cache, wo, layer):
    bd, d = x.shape
    mt, n = kv_cache.shape[1:]
    return pl.pallas_call(
        functools.partial(_xattn_sample_kernel, hd=d // X_HEADS), grid=(bd,),
        in_specs=[_full((bd, d)), _full((1, d)), _full(wq.shape),
                  pl.BlockSpec((1, mt, n), lambda bi: (layer * bd + bi, 0, 0)), _full(wo.shape)],
        out_specs=_full((bd, d)),
        out_shape=jax.ShapeDtypeStruct((bd, d), F32),
        scratch_shapes=[pltpu.VMEM((bd, wq.shape[1]), F32), pltpu.VMEM((bd, wq.shape[1]), F32)],
        compiler_params=_cparams(("arbitrary",), VMEM_LIMIT), name="xattn_sample",
    )(x, g.reshape(1, d), wq, kv_cache, wo)


def _ffn_kernel(x_ref, g_ref, wg_ref, wu_ref, wd_ref, o_ref, h_scr, acc_scr):
    c = pl.program_id(1)

    @pl.when(c == 0)
    def _():
        h_scr[...] = _rms(x_ref[...], g_ref[...]).astype(BF16)
        acc_scr[...] = x_ref[...]

    h = h_scr[...]
    gate = _dot(h, wg_ref[...])
    up = _dot(h, wu_ref[...])
    act = (gate * jax.nn.sigmoid(gate) * up).astype(BF16)
    acc_scr[...] += _dot(act, wd_ref[...])

    @pl.when(c == pl.num_programs(1) - 1)
    def _():
        o_ref[...] = acc_scr[...]


def _ff_chunk(dff, pref):
    c = dff
    for n in range(1, dff // LANES + 1):
        if dff % n == 0 and (dff // n) % LANES == 0 and dff // n <= pref:
            c = dff // n
            break
    return c


def ffn(x, g, w_gu, w_dn):
    m, d = x.shape
    dff = w_dn.shape[0]
    tm = _row_tile(m, 512)
    fc = _ff_chunk(dff, 1408)
    nch = dff // fc
    return pl.pallas_call(
        _ffn_kernel, grid=(m // tm, nch),
        in_specs=[pl.BlockSpec((tm, d), lambda i, c: (i, 0)), _full((1, d)),
                  pl.BlockSpec((d, fc), lambda i, c: (0, c)),
                  pl.BlockSpec((d, fc), lambda i, c: (0, nch + c)),
                  pl.BlockSpec((fc, d), lambda i, c: (c, 0))],
        out_specs=pl.BlockSpec((tm, d), lambda i, c: (i, 0)),
        out_shape=jax.ShapeDtypeStruct((m, d), F32),
        scratch_shapes=[pltpu.VMEM((tm, d), BF16), pltpu.VMEM((tm, d), F32)],
        compiler_params=_cparams(("parallel", "arbitrary"), VMEM_LIMIT), name="ffn",
    )(x, g.reshape(1, d), w_gu, w_gu, w_dn)


def _router_kernel(x_ref, g_ref, w_ref, b_ref, comb_ref, h_ref, *, ne):
    h = _rms(x_ref[...], g_ref[...]).astype(BF16)
    h_ref[...] = h
    logits = _dot(h, w_ref[...]) + b_ref[...]
    lane = lax.broadcasted_iota(jnp.int32, logits.shape, 1)
    logits = jnp.where(lane < ne, logits, -jnp.inf)
    v1 = jnp.max(logits, -1, keepdims=True)
    i1 = jnp.min(jnp.where(logits == v1, lane, LANES), -1, keepdims=True)
    rest = jnp.where(lane == i1, -jnp.inf, logits)
    v2 = jnp.max(rest, -1, keepdims=True)
    i2 = jnp.min(jnp.where(rest == v2, lane, LANES), -1, keepdims=True)
    e2 = jnp.exp(v2 - v1)
    den = 1.0 + e2
    comb_ref[...] = jnp.where(lane == i1, 1.0 / den, 0.0) + jnp.where(lane == i2, e2 / den, 0.0)


def router(x, g, w_r, b_r):
    m, d = x.shape
    ne = w_r.shape[1]
    w_pad = jnp.zeros((d, LANES), BF16).at[:, :ne].set(w_r.astype(BF16))
    b_pad = jnp.zeros((1, LANES), F32).at[0, :ne].set(b_r.astype(F32))
    tm = _row_tile(m, 512)
    return pl.pallas_call(
        functools.partial(_router_kernel, ne=ne), grid=(m // tm,),
        in_specs=[pl.BlockSpec((tm, d), lambda i: (i, 0)), _full((1, d)), _full((d, LANES)), _full((1, LANES))],
        out_specs=[pl.BlockSpec((tm, LANES), lambda i: (i, 0)), pl.BlockSpec((tm, d), lambda i: (i, 0))],
        out_shape=[jax.ShapeDtypeStruct((m, LANES), F32), jax.ShapeDtypeStruct((m, d), BF16)],
        compiler_params=_cparams(("parallel",)), name="router",
    )(x, g.reshape(1, d), w_pad, b_pad)


def _moe_kernel(x_ref, h_ref, comb_ref, wg_ref, wu_ref, wd_ref, o_ref, acc_scr):
    e = pl.program_id(1)

    @pl.when(e == 0)
    def _():
        acc_scr[...] = jnp.zeros_like(acc_scr)

    h = h_ref[...]
    gate = _dot(h, wg_ref[0])
    up = _dot(h, wu_ref[0])
    act = (gate * jax.nn.sigmoid(gate) * up).astype(BF16)
    y = _dot(act, wd_ref[0])
    comb = comb_ref[...]
    lane = lax.broadcasted_iota(jnp.int32, comb.shape, 1)
    acc_scr[...] += jnp.sum(jnp.where(lane == e, comb, 0.0), -1, keepdims=True) * y

    @pl.when(e == pl.num_programs(1) - 1)
    def _():
        o_ref[...] = x_ref[...] + acc_scr[...]


def moe(x, h, comb, w_gu, w_dn):
    m, d = x.shape
    ne, dfe = w_dn.shape[:2]
    tm = _row_tile(m, 512)
    return pl.pallas_call(
        _moe_kernel, grid=(m // tm, ne),
        in_specs=[pl.BlockSpec((tm, d), lambda i, e: (i, 0)), pl.BlockSpec((tm, d), lambda i, e: (i, 0)),
                  pl.BlockSpec((tm, LANES), lambda i, e: (i, 0)),
                  pl.BlockSpec((1, d, dfe), lambda i, e: (e, 0, 0)),
                  pl.BlockSpec((1, d, dfe), lambda i, e: (e, 0, 1)),
                  pl.BlockSpec((1, dfe, d), lambda i, e: (e, 0, 0))],
        out_specs=pl.BlockSpec((tm, d), lambda i, e: (i, 0)),
        out_shape=jax.ShapeDtypeStruct((m, d), F32),
        scratch_shapes=[pltpu.VMEM((tm, d), F32)],
        compiler_params=_cparams(("parallel", "arbitrary"), VMEM_LIMIT), name="moe",
    )(x, h, comb, w_gu, w_gu, w_dn)


def _inproj_odd_kernel(x_ref, g_ref, w_ref, bias_ref, q_ref, k_ref, v_ref, og_ref, gc_ref, gr_ref, *, hq, hv, nh):
    xn = _rms(x_ref[...], g_ref[...]).astype(BF16)

    def mm(lo, hi):
        return _dot(xn, w_ref[:, lo:hi])

    dk = hq // nh
    q_ref[...] = mm(0, hq).astype(BF16)
    k_ref[...] = (mm(hq, 2 * hq) * (dk ** -0.5)).astype(BF16)
    v_ref[...] = mm(2 * hq, 2 * hq + hv).astype(BF16)
    og_ref[...] = jax.nn.sigmoid(mm(2 * hq + hv, 2 * hq + 2 * hv))
    gi = mm(2 * hq + 2 * hv, 2 * hq + 2 * hv + LANES) + bias_ref[...]
    lane = lax.broadcasted_iota(jnp.int32, gi.shape, 1)
    gates = jnp.where(lane < nh, gi, jax.nn.log_sigmoid(gi))
    gc_ref[...] = gates
    gr_ref[...] = gates.T[0:SUBLANES, :]


def inproj_odd(x, g, w_pad, gate_bias):
    m, d = x.shape
    nh = MLSTM_HEADS
    hq = hv = d
    tm = _row_tile(m, 256)
    row = lambda n: pl.BlockSpec((tm, n), lambda i: (i, 0))
    outs = [(hq, BF16), (hq, BF16), (hv, BF16), (hv, F32), (LANES, F32)]
    if tm % LANES:
        gr_spec = _full((SUBLANES, m))
    else:
        gr_spec = pl.BlockSpec((SUBLANES, tm), lambda i: (0, i))
    return pl.pallas_call(
        functools.partial(_inproj_odd_kernel, hq=hq, hv=hv, nh=nh), grid=(m // tm,),
        in_specs=[row(d), _full((1, d)), _full(w_pad.shape), _full((1, LANES))],
        out_specs=[row(n) for n, _ in outs] + [gr_spec],
        out_shape=[jax.ShapeDtypeStruct((m, n), t) for n, t in outs] + [jax.ShapeDtypeStruct((SUBLANES, m), F32)],
        compiler_params=_cparams(("parallel",), VMEM_LIMIT), name="inproj_odd",
    )(x, g.reshape(1, d), w_pad, gate_bias)


def prep_odd(w_in, b_i, b_f):
    d, n = w_in.shape
    n_pad = _round_up(n - 2 * MLSTM_HEADS, LANES) + LANES
    w_pad = jnp.zeros((d, n_pad), BF16).at[:, :n].set(w_in.astype(BF16))
    bias = jnp.zeros((1, LANES), F32).at[0, :2 * MLSTM_HEADS].set(jnp.concatenate([b_i, b_f]).astype(F32))
    return dict(w_in=w_pad, bias=bias)


def _mlstm_prompt_kernel(q_ref, k_ref, v_ref, og_ref, gc_ref, gr_ref, gain_ref, hn_ref, c_ref, n_ref, m_ref,
                         *, nh, dk, dv, ln):
    ci = pl.program_id(1)

    @pl.when(ci == 0)
    def _():
        c_ref[...] = jnp.zeros_like(c_ref)
        n_ref[...] = jnp.zeros_like(n_ref)
        m_ref[...] = jnp.full(m_ref.shape, NEG, F32)

    row = lax.broadcasted_iota(jnp.int32, (ln, ln), 0)
    col = lax.broadcasted_iota(jnp.int32, (ln, ln), 1)
    tri = row >= col
    gc = gc_ref[...]
    gr = gr_ref[...]
    for h in range(nh):
        q = q_ref[:, h * dk:(h + 1) * dk]
        k = k_ref[:, h * dk:(h + 1) * dk]
        v = v_ref[:, h * dv:(h + 1) * dv]
        ig_c, lf_c = gc[:, h:h + 1], gc[:, nh + h:nh + h + 1]
        ig_r, lf_r = gr[h:h + 1, :], gr[nh + h:nh + h + 1, :]
        b_c = jnp.sum(jnp.where(tri, lf_r, 0.0), axis=1, keepdims=True)
        b_r = jnp.sum(jnp.where(row <= col, lf_c, 0.0), axis=0, keepdims=True)
        m_prev = m_ref[0, h:h + 1, 0:1]
        c_prev = c_ref[0, h]
        n_prev = n_ref[0, h:h + 1, :]
        dmat = jnp.where(tri, b_c - b_r + ig_r, NEG)
        inter = b_c + m_prev
        mt = jnp.maximum(inter, jnp.max(dmat, -1, keepdims=True))
        wm = jnp.exp(dmat - mt)
        a = jnp.exp(inter - mt)
        wqk = wm * _dot_nt(q, k)
        num = a * _dot_nt(q, c_prev.astype(BF16)) + _dot(wqk.astype(BF16), v)
        den = a * jnp.sum(q.astype(F32) * n_prev, -1, keepdims=True) + jnp.sum(wqk, -1, keepdims=True)
        hh = num / jnp.maximum(jnp.abs(den), jnp.exp(-mt))
        b_end = b_c[ln - 1:ln, :]
        m_new = mt[ln - 1:ln, :]
        a_end = jnp.exp(b_end + m_prev - m_new)
        w_s = jnp.exp(b_end - b_c + ig_c - m_new)
        c_ref[0, h] = a_end * c_prev + _dot_tn((v.astype(F32) * w_s).astype(BF16), k)
        n_ref[0, h:h + 1, :] = a_end * n_prev + jnp.sum(w_s * k.astype(F32), axis=0, keepdims=True)
        m_ref[0, h:h + 1, :] = jnp.broadcast_to(m_new, (1, m_ref.shape[2]))
        hn = hh * lax.rsqrt(jnp.mean(hh * hh, -1, keepdims=True) + RMS_EPS)
        hn = hn * gain_ref[:, h * dv:(h + 1) * dv] * og_ref[:, h * dv:(h + 1) * dv]
        hn_ref[:, h * dv:(h + 1) * dv] = hn.astype(hn_ref.dtype)


def mlstm_prompt(q, k, v, og, gc, gr, gain, b, s):
    m, d = q.shape
    nh = MLSTM_HEADS
    dk = dv = d // nh
    ln = _row_tile(s, MLSTM_CHUNK)
    nch = s // ln
    row = lambda n: pl.BlockSpec((ln, n), lambda bi, ci: (bi * nch + ci, 0))
    return pl.pallas_call(
        functools.partial(_mlstm_prompt_kernel, nh=nh, dk=dk, dv=dv, ln=ln), grid=(b, nch),
        in_specs=[row(d), row(d), row(d), row(d), row(LANES),
                  pl.BlockSpec((SUBLANES, ln), lambda bi, ci: (0, bi * nch + ci)), _full((1, d))],
        out_specs=[row(d), pl.BlockSpec((1, nh, dv, dk), lambda bi, ci: (bi, 0, 0, 0)),
                   pl.BlockSpec((1, nh, dk), lambda bi, ci: (bi, 0, 0)),
                   pl.BlockSpec((1, nh, LANES), lambda bi, ci: (bi, 0, 0))],
        out_shape=[jax.ShapeDtypeStruct((m, d), BF16), jax.ShapeDtypeStruct((b, nh, dv, dk), F32),
                   jax.ShapeDtypeStruct((b, nh, dk), F32), jax.ShapeDtypeStruct((b, nh, LANES), F32)],
        compiler_params=_cparams(("parallel", "arbitrary"), VMEM_LIMIT), name="mlstm_prompt",
    )(q, k, v, og, gc, gr, gain.reshape(1, d))


def _mlstm_sample_kernel(q_ref, k_ref, v_ref, og_ref, g_ref, gain_ref, c_ref, n_ref, m_ref,
                         hn_ref, co_ref, no_ref, mo_ref, *, nh):
    row = lax.broadcasted_iota(jnp.int32, (SUBLANES, 1), 0)
    for h in range(nh):
        q = q_ref[0, h:h + 1, :]
        k = k_ref[0, h:h + 1, :]
        v = v_ref[0, h:h + 1, :].astype(F32)
        ig = g_ref[0, h:h + 1, 0:1]
        lf = g_ref[0, h:h + 1, 1:2]
        m_prev = m_ref[0, h:h + 1, :]
        c_prev = c_ref[0, h]
        n_prev = n_ref[0, h:h + 1, :]
        inter = lf + m_prev
        mt = jnp.maximum(inter, ig)
        wm = jnp.exp(ig - mt)
        a = jnp.exp(inter - mt)
        q8 = jnp.broadcast_to(q, (SUBLANES, q.shape[1]))
        cq = _dot_nt(q8, c_prev.astype(BF16))[0:1, :]
        wqk = wm * jnp.sum(q.astype(F32) * k.astype(F32), -1, keepdims=True)
        num = a * cq + wqk * v
        den = a * jnp.sum(n_prev * q.astype(F32), -1, keepdims=True) + wqk
        hh = num / jnp.maximum(jnp.abs(den), jnp.exp(-mt))
        v8 = jnp.where(row == 0, jnp.broadcast_to(v * wm, (SUBLANES, v.shape[1])), 0.0).astype(BF16)
        k8 = jnp.broadcast_to(k, (SUBLANES, k.shape[1]))
        co_ref[0, h] = a * c_prev + _dot_tn(v8, k8)
        no_ref[0, h:h + 1, :] = a * n_prev + wm * k.astype(F32)
        mo_ref[0, h:h + 1, :] = mt
        hn = hh * lax.rsqrt(jnp.mean(hh * hh, -1, keepdims=True) + RMS_EPS)
        hn_ref[0, h:h + 1, :] = (hn * gain_ref[h:h + 1, :] * og_ref[0, h:h + 1, :]).astype(hn_ref.dtype)


def mlstm_sample(q, k, v, og, gc, gain, c, n, m):
    bd, d = q.shape
    nh = MLSTM_HEADS
    dk = d // nh
    heads = lambda a: a.reshape(bd, nh, dk)
    g2 = jnp.transpose(gc[:, :2 * nh].reshape(bd, 2, nh), (0, 2, 1))
    blk3 = lambda n2: pl.BlockSpec((1, nh, n2), lambda bi: (bi, 0, 0))
    cspec = pl.BlockSpec((1, nh, dk, dk), lambda bi: (bi, 0, 0, 0))
    hn, co, no, mo = pl.pallas_call(
        functools.partial(_mlstm_sample_kernel, nh=nh), grid=(bd,),
        in_specs=[blk3(dk), blk3(dk), blk3(dk), blk3(dk), blk3(2), _full((nh, dk)), cspec, blk3(dk), blk3(1)],
        out_specs=[blk3(dk), cspec, blk3(dk), blk3(1)],
        out_shape=[jax.ShapeDtypeStruct((bd, nh, dk), BF16), jax.ShapeDtypeStruct(c.shape, F32),
                   jax.ShapeDtypeStruct(n.shape, F32), jax.ShapeDtypeStruct((bd, nh, 1), F32)],
        compiler_params=_cparams(("parallel",)), name="mlstm_sample",
    )(heads(q), heads(k), heads(v), heads(og), g2, gain.reshape(nh, dk), c, n, m.reshape(bd, nh, 1))
    return hn.reshape(bd, d), co, no, mo.reshape(bd, nh)


def _head_pad(q, keep):
    q2 = jnp.concatenate([q] * NSA_KV_HEADS, axis=1)
    row = lax.broadcasted_iota(jnp.int32, q2.shape, 0)
    lane = lax.broadcasted_iota(jnp.int32, q2.shape, 1)
    return jnp.where((lane // NSA_HD == row // NSA_GROUP) & keep(row), q2, jnp.zeros_like(q2))


def _nsa_sample_cmp_kernel(pt_ref, q_ref, pages_ref, pe_ref, w_ref, biasc_ref, oc_ref, idx_ref, xbuf, sem,
                           *, n_pages, nc, ncp, nsb, n_sel, past):
    b = pl.program_id(0)
    nb = pl.num_programs(0)
    kvd = NSA_KV_HEADS * NSA_HD

    def page_copy(bb, slot, p, sl):
        return pltpu.make_async_copy(pages_ref.at[pt_ref[bb * n_pages + p], :, pl.ds(sl * kvd, kvd)],
                                     xbuf.at[slot, sl, pl.ds(p * PAGE_SIZE, PAGE_SIZE), :], sem.at[slot])

    def start_all(bb, slot):
        def body(p, c):
            page_copy(bb, slot, p, 0).start()
            page_copy(bb, slot, p, 1).start()
            return c
        lax.fori_loop(0, n_pages, body, 0)

    @pl.when(b == 0)
    def _():
        start_all(0, 0)

    @pl.when(b + 1 < nb)
    def _():
        start_all(b + 1, (b + 1) % 2)

    slot = b % 2

    def wait_body(p, c):
        page_copy(b, slot, p, 0).wait()
        page_copy(b, slot, p, 1).wait()
        return c
    lax.fori_loop(0, n_pages, wait_body, 0)

    acc = _compress_accumulate(lambda sl, j: xbuf[slot, sl, pl.ds(j, nc, stride=CMP_BLOCK), :], pe_ref, w_ref, nc)
    kc = acc[:, 0:kvd].astype(BF16)
    vc = acc[:, kvd:2 * kvd].astype(BF16)
    q = q_ref[0]
    nh = q.shape[0]
    qpad = _head_pad(q, lambda r: r >= 0)
    s = _dot_nt(qpad, kc)
    s = s + biasc_ref[:, 0:nc]
    e = jnp.exp(s - jnp.max(s, -1, keepdims=True))
    p_c = e / jnp.sum(e, -1, keepdims=True)
    o = _dot(p_c.astype(BF16), vc)
    row = lax.broadcasted_iota(jnp.int32, (nh, NSA_HD), 0)
    o_h = o[:, 0:NSA_HD]
    for k in range(1, NSA_KV_HEADS):
        o_h = jnp.where(row // NSA_GROUP == k, o[:, k * NSA_HD:(k + 1) * NSA_HD], o_h)
    oc_ref[0] = o_h
    prow = lax.broadcasted_iota(jnp.int32, p_c.shape, 0)
    lane = lax.broadcasted_iota(jnp.int32, (1, ncp), 1)
    blk = lane // 2
    cur = past // SEL_BLOCK
    forced = (blk == 0) | (blk == cur) | (blk == cur - 1)
    is_cand = ((lane % 2) == 0) & (lane < 2 * nsb)
    nselp = idx_ref.shape[1]
    rsel = lax.broadcasted_iota(jnp.int32, (nselp, 1), 0).astype(F32)
    out_lane = lax.broadcasted_iota(jnp.int32, (nselp, LANES), 1)
    result = jnp.full((nselp, LANES), -1, jnp.int32)
    for k in range(NSA_KV_HEADS):
        imp = jnp.sum(jnp.where(prow // NSA_GROUP == k, p_c, 0.0), axis=0, keepdims=True)
        imp = jnp.concatenate([imp, jnp.zeros((1, ncp - nc), F32)], axis=1)
        imp = _pair_sum(imp)
        score = jnp.where(forced, FORCE_SCORE, jnp.where(blk <= cur, imp, -1.0))
        score = jnp.where(is_cand, score, -2.0)
        sel, rank = _select_blocks(score, n_sel, nsb)
        hit = (rank == rsel) & (sel > 0.5)
        idx = jnp.sum(jnp.where(hit, (blk + 1).astype(F32), 0.0), axis=1, keepdims=True) - 1.0
        result = jnp.where(out_lane == k, idx.astype(jnp.int32), result)
    idx_ref[0] = result


def nsa_sample_cmp(q8, pages, page_table, pe2, wbd, rel_bias):
    bd, nh, hd = q8.shape
    n_pages = page_table.shape[1]
    past = n_pages * PAGE_SIZE
    nc = past // CMP_BLOCK
    nsb = -(-(past + 1) // SEL_BLOCK)
    n_sel = min(TOP_N, nsb)
    ncp = _round_up(max(nc, SEL_RATIO * nsb), LANES)
    nselp = _round_up(n_sel, SUBLANES)
    kvd = wbd.shape[-1]
    cend = jnp.arange(nc) * CMP_BLOCK + (CMP_BLOCK - 1)
    biasc = jnp.zeros((nh, ncp), F32).at[:, :nc].set(_bias_of(rel_bias, past - cend).T)
    grid_spec = pltpu.PrefetchScalarGridSpec(
        num_scalar_prefetch=1, grid=(bd,),
        in_specs=[pl.BlockSpec((1, nh, hd), lambda bi, pt: (bi, 0, 0)),
                  pl.BlockSpec(memory_space=pl.ANY),
                  pl.BlockSpec(pe2.shape, lambda bi, pt: (0, 0, 0)),
                  pl.BlockSpec(wbd.shape, lambda bi, pt: (0, 0, 0, 0)),
                  pl.BlockSpec((nh, ncp), lambda bi, pt: (0, 0))],
        out_specs=[pl.BlockSpec((1, nh, hd), lambda bi, pt: (bi, 0, 0)),
                   pl.BlockSpec((1, nselp, LANES), lambda bi, pt: (bi, 0, 0))],
        scratch_shapes=[pltpu.VMEM((2, 2, past, kvd), F32), pltpu.SemaphoreType.DMA((2,))])
    oc, idx = pl.pallas_call(
        functools.partial(_nsa_sample_cmp_kernel, n_pages=n_pages, nc=nc, ncp=ncp, nsb=nsb, n_sel=n_sel, past=past),
        grid_spec=grid_spec,
        out_shape=[jax.ShapeDtypeStruct((bd, nh, hd), F32), jax.ShapeDtypeStruct((bd, nselp, LANES), jnp.int32)],
        compiler_params=_cparams(("arbitrary",), VMEM_LIMIT), name="nsa_sample_cmp",
    )(page_table.reshape(-1), q8, pages, pe2, wbd, biasc)
    sel_idx = jnp.transpose(idx[:, :n_sel, :NSA_KV_HEADS], (0, 2, 1))
    return oc, sel_idx


def _nsa_sample_att_kernel(pt_ref, si_ref, q_ref, g_ref, oc_ref, kvn_ref, wn_ref, wc_ref, pages_ref,
                           bsel_ref, bwin_ref, ob_ref, win_ref, selbuf, wall, sem,
                           *, n_pages, n_sel, past, wb):
    b = pl.program_id(0)
    nb = pl.num_programs(0)
    kvd = NSA_KV_HEADS * NSA_HD
    hd = NSA_HD
    n_blk_pages = past // SEL_BLOCK
    per_page = PAGE_SIZE // SEL_BLOCK
    n_slots = NSA_KV_HEADS * n_sel

    def blk_of(bb, j):
        return si_ref[bb * n_slots + j]

    def blk_copy(bb, slot, j):
        blk = jnp.clip(blk_of(bb, j), 0, n_blk_pages - 1)
        page = pt_ref[bb * n_pages + blk // per_page]
        r0 = pl.multiple_of((blk % per_page) * SEL_BLOCK, SEL_BLOCK)
        return pltpu.make_async_copy(pages_ref.at[page, pl.ds(r0, SEL_BLOCK), pl.ds(2 * kvd, 2 * kvd)],
                                     selbuf.at[slot, j], sem.at[slot])

    def in_pages(bb, j):
        blk = blk_of(bb, j)
        return (blk >= 0) & (blk < n_blk_pages)

    def start_all(bb, slot):
        def body(j, c):
            @pl.when(in_pages(bb, j))
            def _():
                blk_copy(bb, slot, j).start()
            return c
        lax.fori_loop(0, n_slots, body, 0)

    @pl.when(b == 0)
    def _():
        start_all(0, 0)

    @pl.when(b + 1 < nb)
    def _():
        start_all(b + 1, (b + 1) % 2)

    slot = b % 2
    new_sel = kvn_ref[0][:, 2 * kvd:4 * kvd]
    sub = lax.broadcasted_iota(jnp.int32, (SEL_BLOCK, 1), 0)

    def wait_body(j, c):
        @pl.when(in_pages(b, j))
        def _():
            blk_copy(b, slot, j).wait()

        @pl.when(jnp.logical_not(in_pages(b, j)))
        def _():
            is_new = blk_of(b, j) == n_blk_pages
            selbuf[slot, j] = jnp.where((sub == 0) & is_new, new_sel, 0.0)
        return c
    lax.fori_loop(0, n_slots, wait_body, 0)

    q = q_ref[0]
    nh = q.shape[0]
    gates = g_ref[0]
    o_c = oc_ref[0]

    def q_rows(keep):
        qp = _head_pad(q, keep)
        return jnp.concatenate([qp, jnp.zeros((LANES - nh, qp.shape[1]), qp.dtype)], axis=0)

    def attend(qrows, keys_b, vals, bias, valid):
        s = _dot_nt(keys_b, qrows) + bias
        s = jnp.where(valid, s, NEG)
        e = jnp.where(valid, jnp.exp(s - jnp.max(s, 0, keepdims=True)), 0.0)
        den = jnp.sum(e, 0, keepdims=True)
        p = e / jnp.where(den > 0, den, 1.0)
        return [jnp.sum(p[:, h:h + 1] * vals, axis=0, keepdims=True) for h in range(nh)]

    o_s = [None] * nh
    for k in range(NSA_KV_HEADS):
        qrows = q_rows(lambda r: r // NSA_GROUP == k)
        blks = [blk_of(b, k * n_sel + r) for r in range(n_sel)]
        keys = selbuf[slot, k * n_sel:(k + 1) * n_sel].reshape(n_sel * SEL_BLOCK, 2 * kvd)
        bias = jnp.concatenate([bsel_ref[jnp.maximum(bl, 0)] for bl in blks], axis=0)
        valid = jnp.concatenate([(bl * SEL_BLOCK + sub <= past) & (bl >= 0) for bl in blks], axis=0)
        rows = attend(qrows, keys[:, 0:kvd].astype(BF16), keys[:, kvd:2 * kvd], bias, valid)
        for h in range(k * NSA_GROUP, (k + 1) * NSA_GROUP):
            o_s[h] = rows[h][:, k * hd:(k + 1) * hd]
    wbp = wall.shape[0]
    wall[0:wb, :] = wc_ref[0]
    wall[wb:wb + 1, :] = wn_ref[0]
    wall[wb + 1:wbp, :] = jnp.zeros((wbp - wb - 1, 2 * kvd), F32)
    win_ref[0] = wall[1:wb + 1, :]
    w_pos = lax.broadcasted_iota(jnp.int32, (wbp, 1), 0)
    valid_w = (w_pos <= wb) & (wb - w_pos < WINDOW) & (past - wb + w_pos >= 0)
    qrows = q_rows(lambda r: r >= 0)
    rows = attend(qrows, wall[:, 0:kvd].astype(BF16), wall[:, kvd:2 * kvd], bwin_ref[...], valid_w)
    for h in range(nh):
        k = h // NSA_GROUP
        o_w = rows[h][:, k * hd:(k + 1) * hd]
        mix = gates[h:h + 1, 0:1] * o_c[h:h + 1, :] + gates[h:h + 1, 1:2] * o_s[h] + gates[h:h + 1, 2:3] * o_w
        ob_ref[0, h:h + 1, :] = mix.astype(ob_ref.dtype)


def nsa_sample_att(q8, gates, oc, kv03, kv45, wcache, pages, page_table, sel_idx, rel_bias):
    bd, nh, hd = q8.shape
    n_pages = page_table.shape[1]
    past = n_pages * PAGE_SIZE
    wb = wcache.shape[1]
    kvd = NSA_KV_HEADS * hd
    n_sel = sel_idx.shape[2]
    nsb = -(-(past + 1) // SEL_BLOCK)
    wbp = _round_up(wb + 1, SUBLANES)
    g3 = gates[:, :3 * nh].reshape(bd, nh, 3)
    dist = past - (jnp.arange(nsb)[:, None] * SEL_BLOCK + jnp.arange(SEL_BLOCK)[None, :])
    bsel = jnp.zeros((nsb, SEL_BLOCK, LANES), F32).at[:, :, :nh].set(_bias_of(rel_bias, dist))
    bwin = jnp.zeros((wbp, LANES), F32).at[:, :nh].set(_bias_of(rel_bias, wb - jnp.arange(wbp)))
    blk = lambda n2, n3: pl.BlockSpec((1, n2, n3), lambda bi, pt, si: (bi, 0, 0))
    grid_spec = pltpu.PrefetchScalarGridSpec(
        num_scalar_prefetch=2, grid=(bd,),
        in_specs=[blk(nh, hd), blk(nh, 3), blk(nh, hd), blk(1, 4 * kvd), blk(1, 2 * kvd), blk(wb, 2 * kvd),
                  pl.BlockSpec(memory_space=pl.ANY),
                  pl.BlockSpec(bsel.shape, lambda bi, pt, si: (0, 0, 0)),
                  pl.BlockSpec(bwin.shape, lambda bi, pt, si: (0, 0))],
        out_specs=[blk(nh, hd), blk(wb, 2 * kvd)],
        scratch_shapes=[pltpu.VMEM((2, NSA_KV_HEADS * n_sel, SEL_BLOCK, 2 * kvd), F32),
                        pltpu.VMEM((wbp, 2 * kvd), F32), pltpu.SemaphoreType.DMA((2,))])
    ob, win = pl.pallas_call(
        functools.partial(_nsa_sample_att_kernel, n_pages=n_pages, n_sel=n_sel, past=past, wb=wb),
        grid_spec=grid_spec,
        out_shape=[jax.ShapeDtypeStruct((bd, nh, hd), BF16), jax.ShapeDtypeStruct((bd, wb, 2 * kvd), F32)],
        compiler_params=_cparams(("arbitrary",), VMEM_LIMIT), name="nsa_sample_att",
    )(page_table.reshape(-1), sel_idx.reshape(-1), q8, g3, oc, kv03.reshape(bd, 1, -1), kv45.reshape(bd, 1, -1),
      wcache, pages, bsel, bwin)
    return ob.reshape(bd, nh * hd), win


def kernel(x_prompt, x_sample, mem_prompt, cache_conv, cache_nsa_pages, cache_nsa_window, state_mlstm_c,
           state_mlstm_n, state_mlstm_m, cache_mem_kv, page_table, rel_bias, norm_mix, norm_xattn, norm_mem,
           norm_ffn, norm_final, w_in_even, w_out_even, conv_w, conv_b, conv_ln_g, conv_ln_b, nsa_cmp_pe,
           nsa_cmp_w, w_in_odd, mlstm_b_i, mlstm_b_f, mlstm_norm, w_out_odd, xattn_wq, xattn_wkv, xattn_wo,
           ffn_w_gu, ffn_w_dn, router_w, router_b, expert_w_gu, expert_w_dn):
    b, s, d = x_prompt.shape
    bd, td, _ = x_sample.shape
    assert td == 1, "the sample group decodes one token per sequence"
    depth = norm_mix.shape[0]
    mt = mem_prompt.shape[1]
    cc = conv_w.shape[2]
    hist = conv_w.shape[1] - 1
    wb = cache_nsa_window.shape[2]
    kvh, hd = NSA_KV_HEADS, NSA_HD
    n_pool = cache_nsa_pages.shape[1]
    assert s >= hist and s >= wb and s % Q_BLOCK == 0
    xp = x_prompt.reshape(b * s, d)
    xs = x_sample.reshape(bd, d)
    mem = mem_prompt.reshape(b * mt, d)
    pages_all = cache_nsa_pages.reshape(-1, PAGE_SIZE, 4 * kvh * hd)
    memkv_all = cache_mem_kv.reshape(depth * bd, mt, -1)
    bf = lambda a: a.astype(BF16)
    conv_p, conv_s, nsa_p, nsa_s, win_p, win_s = [], [], [], [], [], []
    mc_p, mc_s, mn_p, mn_s, mm_p, mm_s, memkv_p = [], [], [], [], [], [], []
    for l in range(depth):
        li = l // 2
        if l % 2 == 0:
            prm = prep_even(w_in_even[li], nsa_cmp_pe[li], nsa_cmp_w[li])
            w_out = bf(w_out_even[li])
            w_parts = [w_out[:cc], w_out[cc:]]
            conv_args = (conv_w[li], conv_b[li], conv_ln_g[li], conv_ln_b[li])
            glu, q, kv03, kvb, kv45, gates = inproj_even(xp, norm_mix[l], prm['w_in'], cc)
            a_out = conv_prompt(glu, *conv_args, b, s)
            kcv = compress_prompt(kv03, prm['pe2'], prm['wbd'], b, s, _round_up(s // CMP_BLOCK, LANES))
            b_out = nsa_prompt(q, gates, kcv, kvb, rel_bias, b, s)
            xp = outproj(xp, [a_out, b_out], w_parts)
            conv_p.append(glu.reshape(b, s, cc)[:, s - hist:])
            nsa_p.append(kv03.reshape(b, s, 4, kvh, hd))
            win_p.append(kv45.reshape(b, s, 2, kvh, hd)[:, s - wb:])
            glu, q, kv03, kvb, kv45, gates = inproj_even(xs, norm_mix[l], prm['w_in'], cc)
            a_out, conv_state = conv_sample(cache_conv[li], glu, *conv_args)
            q8 = q.reshape(bd, NSA_HEADS, hd)
            pt = page_table + li * n_pool
            o_c, sel_idx = nsa_sample_cmp(q8, pages_all, pt, prm['pe2'], prm['wbd'], rel_bias)
            b_out, win = nsa_sample_att(q8, gates, o_c, kv03, kv45, cache_nsa_window[li].reshape(bd, wb, -1),
                                        pages_all, pt, sel_idx, rel_bias)
            xs = outproj(xs, [a_out, b_out], w_parts)
            conv_s.append(conv_state)
            nsa_s.append(kv03.reshape(bd, 1, 4, kvh, hd))
            win_s.append(win.reshape(bd, wb, 2, kvh, hd))
        else:
            prm = prep_odd(w_in_odd[li], mlstm_b_i[li], mlstm_b_f[li])
            w_out = bf(w_out_odd[li])
            q, k, v, og, gc, gr = inproj_odd(xp, norm_mix[l], prm['w_in'], prm['bias'])
            hn, c_new, n_new, m_new = mlstm_prompt(q, k, v, og, gc, gr, mlstm_norm[li], b, s)
            xp = outproj(xp, [hn], [w_out])
            mc_p.append(c_new)
            mn_p.append(n_new)
            mm_p.append(m_new[:, :, 0])
            q, k, v, og, gc, gr = inproj_odd(xs, norm_mix[l], prm['w_in'], prm['bias'])
            hn, c_new, n_new, m_new = mlstm_sample(q, k, v, og, gc, mlstm_norm[li], state_mlstm_c[li],
                                                   state_mlstm_n[li], state_mlstm_m[li])
            xs = outproj(xs, [hn], [w_out])
            mc_s.append(c_new)
            mn_s.append(n_new)
            mm_s.append(m_new)
        wq, wo = bf(xattn_wq[l]), bf(xattn_wo[l])
        mkv, mkv_b = memkv(mem, norm_mem[l], bf(xattn_wkv[l]))
        memkv_p.append(mkv.reshape(b, mt, 2, X_HEADS, d // X_HEADS))
        xp = xattn_prompt(xp, norm_xattn[l], wq, mkv_b, wo, b, s)
        xs = xattn_sample(xs, norm_xattn[l], wq, memkv_all, wo, l)
        if l % 2 == 0:
            w_gu, w_dn = bf(ffn_w_gu[li]), bf(ffn_w_dn[li])
            xp = ffn(xp, norm_ffn[l], w_gu, w_dn)
            xs = ffn(xs, norm_ffn[l], w_gu, w_dn)
        else:
            e_gu, e_dn = bf(expert_w_gu[li]), bf(expert_w_dn[li])
            comb, h = router(xp, norm_ffn[l], router_w[li], router_b[li])
            xp = moe(xp, h, comb, e_gu, e_dn)
            comb, h = router(xs, norm_ffn[l], router_w[li], router_b[li])
            xs = moe(xs, h, comb, e_gu, e_dn)
    y_prompt = rmsnorm(xp, norm_final).reshape(b, s, d)
    y_sample = rmsnorm(xs, norm_final).reshape(bd, 1, d)
    return (y_prompt, y_sample, jnp.stack(conv_p), jnp.stack(conv_s), jnp.stack(nsa_p), jnp.stack(nsa_s),
            jnp.stack(win_p), jnp.stack(win_s), jnp.stack(mc_p), jnp.stack(mc_s), jnp.stack(mn_p),
            jnp.stack(mn_s), jnp.stack(mm_p), jnp.stack(mm_s), jnp.stack(memkv_p))
```

```python
import functools
import math

import jax
import jax.numpy as jnp
import numpy as np
from jax import lax
from jax.experimental import pallas as pl
from jax.experimental.pallas import tpu as pltpu

F32 = jnp.float32
BF16 = jnp.bfloat16

PAGE_SIZE = 128
CONV_WIDTH = 31
NSA_HEADS = 8
NSA_KV_HEADS = 2
NSA_GROUP = NSA_HEADS // NSA_KV_HEADS
NSA_HD = 64
CMP_BLOCK = 32
SEL_BLOCK = 64
SEL_RATIO = SEL_BLOCK // CMP_BLOCK
TOP_N = 16
WINDOW = 512
Q_BLOCK = 128
FORCE_SCORE = 1.0e4
NUM_BUCKETS = 32
MAX_DISTANCE = 1024
MLSTM_HEADS = 4
X_HEADS = 4
N_EXPERTS = 8
TOP_K = 2
RMS_EPS = 1e-6
LN_EPS = 1e-5
NEG = -1e30

LANES = 128
SUBLANES = 8
VMEM_LIMIT = 56 * 1024 * 1024
MLSTM_CHUNK = 256


def _cparams(sem, vmem=None):
    return pltpu.CompilerParams(dimension_semantics=sem, vmem_limit_bytes=vmem)


def _rms(x, g):
    return x * lax.rsqrt(jnp.mean(x * x, -1, keepdims=True) + RMS_EPS) * g


def _dot(a, b):
    return jnp.dot(a, b, preferred_element_type=F32)


def _dot_nt(a, b):
    return lax.dot_general(a, b, (((1,), (1,)), ((), ())), preferred_element_type=F32)


def _dot_tn(a, b):
    return lax.dot_general(a, b, (((0,), (0,)), ((), ())), preferred_element_type=F32)


def _full(shape):
    n = len(shape)
    return pl.BlockSpec(shape, lambda *_: (0,) * n)


def _row_tile(m, pref):
    t = min(pref, m)
    while m % t:
        t //= 2
    return t


def _rmsnorm_kernel(x_ref, g_ref, o_ref):
    o_ref[...] = _rms(x_ref[...], g_ref[...])


def rmsnorm(x, g):
    m, d = x.shape
    tm = _row_tile(m, 1024)
    return pl.pallas_call(
        _rmsnorm_kernel, grid=(m // tm,),
        in_specs=[pl.BlockSpec((tm, d), lambda i: (i, 0)), _full((1, d))],
        out_specs=pl.BlockSpec((tm, d), lambda i: (i, 0)),
        out_shape=jax.ShapeDtypeStruct((m, d), F32),
        compiler_params=_cparams(("parallel",)), name="rmsnorm",
    )(x, g.reshape(1, d))


def _outproj_kernel(*refs, n_in):
    x_ref = refs[0]
    a_refs = refs[1:1 + n_in]
    w_refs = refs[1 + n_in:1 + 2 * n_in]
    o_ref = refs[1 + 2 * n_in]
    acc = x_ref[...]
    for a_ref, w_ref in zip(a_refs, w_refs):
        acc = acc + _dot(a_ref[...], w_ref[...])
    o_ref[...] = acc


def outproj(x, acts, ws):
    m, d = x.shape
    tm = _row_tile(m, 512)
    n_in = len(acts)
    in_specs = [pl.BlockSpec((tm, d), lambda i: (i, 0))]
    in_specs += [pl.BlockSpec((tm, a.shape[1]), lambda i: (i, 0)) for a in acts]
    in_specs += [_full(w.shape) for w in ws]
    return pl.pallas_call(
        functools.partial(_outproj_kernel, n_in=n_in), grid=(m // tm,),
        in_specs=in_specs, out_specs=pl.BlockSpec((tm, d), lambda i: (i, 0)),
        out_shape=jax.ShapeDtypeStruct((m, d), F32),
        compiler_params=_cparams(("parallel",)), name="outproj",
    )(x, *acts, *ws)


def _inproj_even_kernel(x_ref, g_ref, w_ref, glu_ref, kv03_ref, kv45_ref, *rest, cc, qd, kvd, tiles):
    xn = _rms(x_ref[...], g_ref[...]).astype(BF16)

    def mm(lo, hi):
        return _dot(xn, w_ref[:, lo:hi])

    o = 0
    a = mm(o, o + cc)
    b = mm(o + cc, o + 2 * cc)
    glu_ref[...] = a * jax.nn.sigmoid(b)
    o += 2 * cc
    q = mm(o, o + qd) * (NSA_HD ** -0.5)
    o += qd
    kv03 = mm(o, o + 4 * kvd)
    kv03_ref[...] = kv03
    o += 4 * kvd
    kv45 = mm(o, o + 2 * kvd)
    kv45_ref[...] = kv45
    o += 2 * kvd
    gates = jax.nn.sigmoid(mm(o, o + LANES))
    if tiles == 0:
        q_ref, gate_ref = rest
        q_ref[...] = q.astype(BF16)
        gate_ref[...] = gates
        return
    keys_ref, qt_ref, vt_ref, gt_ref = rest
    keys_ref[...] = jnp.concatenate([kv03[:, 2 * kvd:3 * kvd], kv45[:, 0:kvd]], axis=1).astype(BF16)
    vals = jnp.concatenate([kv03[:, 3 * kvd:4 * kvd], kv45[:, kvd:2 * kvd]], axis=1)
    for j in range(tiles):
        rows = slice(j * Q_BLOCK, (j + 1) * Q_BLOCK)
        qt_ref[j] = q[rows, :].T.astype(BF16)
        vt_ref[j] = vals[rows, :].T.astype(BF16)
        gt_ref[j] = gates[rows, :].T


def inproj_even(x, g, w_pad, cc, transposed):
    m, d = x.shape
    qd = NSA_HEADS * NSA_HD
    kvd = NSA_KV_HEADS * NSA_HD
    tm = _row_tile(m, 256)
    row = lambda n: pl.BlockSpec((tm, n), lambda i: (i, 0))
    out_specs = [row(cc), row(4 * kvd), row(2 * kvd)]
    out_shape = [jax.ShapeDtypeStruct((m, n), F32) for n in (cc, 4 * kvd, 2 * kvd)]
    tiles = tm // Q_BLOCK if transposed else 0
    if transposed:
        assert tm % Q_BLOCK == 0
        tile = lambda n: pl.BlockSpec((tiles, n, Q_BLOCK), lambda i: (i, 0, 0))
        out_specs += [row(2 * kvd), tile(qd), tile(2 * kvd), tile(LANES)]
        out_shape += [jax.ShapeDtypeStruct((m, 2 * kvd), BF16),
                      jax.ShapeDtypeStruct((m // Q_BLOCK, qd, Q_BLOCK), BF16),
                      jax.ShapeDtypeStruct((m // Q_BLOCK, 2 * kvd, Q_BLOCK), BF16),
                      jax.ShapeDtypeStruct((m // Q_BLOCK, LANES, Q_BLOCK), F32)]
    else:
        out_specs += [row(qd), row(LANES)]
        out_shape += [jax.ShapeDtypeStruct((m, qd), BF16), jax.ShapeDtypeStruct((m, LANES), F32)]
    return pl.pallas_call(
        functools.partial(_inproj_even_kernel, cc=cc, qd=qd, kvd=kvd, tiles=tiles), grid=(m // tm,),
        in_specs=[row(d), _full((1, d)), _full(w_pad.shape)],
        out_specs=out_specs, out_shape=out_shape,
        compiler_params=_cparams(("parallel",)), name="inproj_even",
    )(x, g.reshape(1, d), w_pad)


def _conv_post(y, lg, lb):
    mu = jnp.mean(y, -1, keepdims=True)
    var = jnp.mean(jnp.square(y - mu), -1, keepdims=True)
    yn = (y - mu) * lax.rsqrt(var + LN_EPS) * lg + lb
    return yn * jax.nn.sigmoid(yn)


CONV_SUB = 64
CONV_PAD = 32


def _conv_prompt_kernel(glu_ref, cw_ref, cb_ref, lg_ref, lb_ref, o_ref, ext_ref, y_ref, *, ts, s):
    i = pl.program_id(1)
    c = glu_ref.shape[-1]

    @pl.when(i == 0)
    def _():
        ext_ref[0:CONV_PAD, :] = jnp.zeros((CONV_PAD, c), F32)
        ext_ref[CONV_PAD:CONV_PAD + s, :] = glu_ref[0]
        ext_ref[CONV_PAD + s:CONV_PAD + s + SUBLANES, :] = jnp.zeros((SUBLANES, c), F32)

    lead = CONV_PAD - (CONV_WIDTH - 1)
    span = CONV_SUB + CONV_PAD

    def sub(j, carry):
        r0 = pl.multiple_of(i * ts + j * CONV_SUB, CONV_SUB)
        for c0 in range(0, c, LANES):
            xw = ext_ref[pl.ds(r0, span + SUBLANES), c0:c0 + LANES]
            acc = jnp.zeros((CONV_SUB, LANES), F32) + cb_ref[:, c0:c0 + LANES]
            for r in range(SUBLANES):
                xr = xw[r:r + span, :]
                for a in range(span // SUBLANES):
                    w = SUBLANES * a + r - lead
                    if 0 <= w < CONV_WIDTH:
                        acc = acc + xr[SUBLANES * a:SUBLANES * a + CONV_SUB, :] * cw_ref[w:w + 1, c0:c0 + LANES]
            y_ref[:, c0:c0 + LANES] = acc
        o_ref[0, pl.ds(pl.multiple_of(j * CONV_SUB, CONV_SUB), CONV_SUB), :] = _conv_post(
            y_ref[...], lg_ref[...], lb_ref[...]).astype(o_ref.dtype)
        return carry

    lax.fori_loop(0, ts // CONV_SUB, sub, 0)


def conv_prompt(glu, cw, cb, lg, lb, b, s):
    c = glu.shape[-1]
    ts = _row_tile(s, 256)
    vec = lambda a: a.reshape(1, c)
    out = pl.pallas_call(
        functools.partial(_conv_prompt_kernel, ts=ts, s=s), grid=(b, s // ts),
        in_specs=[pl.BlockSpec((1, s, c), lambda bi, i: (bi, 0, 0)), _full((CONV_WIDTH, c)),
                  _full((1, c)), _full((1, c)), _full((1, c))],
        out_specs=pl.BlockSpec((1, ts, c), lambda bi, i: (bi, i, 0)),
        out_shape=jax.ShapeDtypeStruct((b, s, c), BF16),
        scratch_shapes=[pltpu.VMEM((CONV_PAD + s + SUBLANES, c), F32), pltpu.VMEM((CONV_SUB, c), F32)],
        compiler_params=_cparams(("parallel", "arbitrary")), name="conv_prompt",
    )(glu.reshape(b, s, c), cw, vec(cb), vec(lg), vec(lb))
    return out.reshape(b * s, c)


def _conv_sample_kernel(cache_ref, glu_ref, cw_ref, cb_ref, lg_ref, lb_ref, o_ref, st_ref):
    hist = CONV_WIDTH - 1
    cache = cache_ref[...]
    glu = glu_ref[...]
    y = jnp.sum(cache * cw_ref[0:hist, :][None], axis=1) + glu * cw_ref[hist:hist + 1, :] + cb_ref[...]
    o_ref[...] = _conv_post(y, lg_ref[...], lb_ref[...]).astype(o_ref.dtype)
    st_ref[:, 0:hist - 1, :] = cache[:, 1:hist, :]
    st_ref[:, hist - 1:hist, :] = glu[:, None, :]


def conv_sample(cache, glu, cw, cb, lg, lb):
    bd, hist, c = cache.shape
    vec = lambda a: a.reshape(1, c)
    return pl.pallas_call(
        _conv_sample_kernel,
        out_shape=[jax.ShapeDtypeStruct((bd, c), BF16), jax.ShapeDtypeStruct((bd, hist, c), F32)],
        name="conv_sample",
    )(cache, glu, cw, vec(cb), vec(lg), vec(lb))


def _rel_bucket(dist):
    n = jnp.maximum(dist, 0)
    max_exact = NUM_BUCKETS // 2
    nf = jnp.maximum(n, 1).astype(F32)
    large = max_exact + (jnp.log(nf / max_exact) / math.log(MAX_DISTANCE / max_exact)
                         * (NUM_BUCKETS - max_exact)).astype(jnp.int32)
    large = jnp.minimum(large, NUM_BUCKETS - 1)
    return jnp.where(n < max_exact, n, large)


def _bias_of(rel_bias, dist):
    bucket = _rel_bucket(dist)[..., None]
    out = jnp.zeros(bucket.shape[:-1] + (rel_bias.shape[1],), F32)
    for k in range(NUM_BUCKETS):
        out = jnp.where(bucket == k, rel_bias[k].astype(F32), out)
    return out


def _compress_accumulate(load_rows, pe_ref, w_ref, nc):
    accs = []
    for slot in range(2):
        acc = jnp.zeros((nc, w_ref.shape[-1]), F32)
        for j in range(CMP_BLOCK):
            xj = load_rows(slot, j) + pe_ref[slot, j:j + 1, :]
            acc = acc + _dot(xj.astype(BF16), w_ref[slot, j])
        accs.append(acc)
    return jnp.concatenate(accs, axis=1)


def _compress_prompt_kernel(x_ref, pe_ref, w_ref, kc_ref, vct_ref, *, nc, ncp, n_slots):
    acc = _compress_accumulate(
        lambda sl, j: x_ref[0, pl.ds(j * n_slots + sl, nc, stride=CMP_BLOCK * n_slots), :], pe_ref, w_ref, nc)
    kvd = acc.shape[1] // 2
    if ncp > nc:
        acc = jnp.concatenate([acc, jnp.zeros((ncp - nc, 2 * kvd), F32)], axis=0)
    kc_ref[0] = acc[:, 0:kvd].astype(BF16)
    vct_ref[0] = acc[:, kvd:2 * kvd].T.astype(BF16)


def compress_prompt(kv03, pe2, wbd, b, s, ncp):
    nc = s // CMP_BLOCK
    kvd = wbd.shape[-1]
    n_slots = kv03.shape[1] // kvd
    return pl.pallas_call(
        functools.partial(_compress_prompt_kernel, nc=nc, ncp=ncp, n_slots=n_slots), grid=(b,),
        in_specs=[pl.BlockSpec((1, s * n_slots, kvd), lambda bi: (bi, 0, 0)), _full(pe2.shape), _full(wbd.shape)],
        out_specs=[pl.BlockSpec((1, ncp, kvd), lambda bi: (bi, 0, 0)), pl.BlockSpec((1, kvd, ncp), lambda bi: (bi, 0, 0))],
        out_shape=[jax.ShapeDtypeStruct((b, ncp, kvd), BF16), jax.ShapeDtypeStruct((b, kvd, ncp), BF16)],
        compiler_params=_cparams(("parallel",)), name="compress_prompt",
    )(kv03.reshape(b, s * n_slots, kvd), pe2, wbd)


def _select_blocks(score, n_sel, n_cand):
    lane = lax.broadcasted_iota(jnp.int32, score.shape, 1)
    rank = jnp.zeros(score.shape, F32)
    for i in range(n_cand):
        col = score[:, 2 * i:2 * i + 1]
        beats = (col > score) | ((col == score) & (lane > 2 * i))
        rank = rank + beats.astype(F32)
    is_cand = ((lane % 2) == 0) & (lane < 2 * n_cand)
    return (is_cand & (rank < n_sel) & (score >= 0)).astype(F32), rank


def _pair_sum(imp):
    n = imp.shape[1]
    return imp + pltpu.roll(imp, n - 1, 1)


def _rank_rows(score, n_sel, n_cand):
    blk = lax.broadcasted_iota(jnp.int32, score.shape, 0)
    rank = jnp.zeros(score.shape, F32)
    for i in range(n_cand):
        row = score[i:i + 1, :]
        beats = (row > score) | ((row == score) & (blk > i))
        rank = rank + beats.astype(F32)
    return ((rank < n_sel) & (score >= 0)).astype(F32)


def _nsa_prompt_kernel(qt_ref, gt_ref, kc_ref, vct_ref, keys_ref, vt_ref, biasc_ref, btile_ref, o_ref,
                       qt_scr, oc_scr, acc_s, acc_w, imp_scr, sel_scr, out_scr, *, nc, nsb, n_sel):
    qi = pl.program_id(1)
    g, hd, kvh, qb = NSA_GROUP, NSA_HD, NSA_KV_HEADS, Q_BLOCK
    kvd = kvh * hd
    ncp = kc_ref.shape[1]
    nsbp = sel_scr.shape[1]
    q_pos = qi * qb + lax.broadcasted_iota(jnp.int32, (1, qb), 1)
    key_row = lax.broadcasted_iota(jnp.int32, (qb, 1), 0)
    c_row = lax.broadcasted_iota(jnp.int32, (ncp, 1), 0)
    mask_c = (q_pos >= c_row * CMP_BLOCK + (CMP_BLOCK - 1)) & (c_row < nc)
    blk = lax.broadcasted_iota(jnp.int32, (nsbp, 1), 0)
    cur = q_pos // SEL_BLOCK
    forced = (blk == 0) | (blk == cur) | (blk == cur - 1)
    zeros = jnp.zeros((hd, qb), BF16)
    for k in range(kvh):
        for gi in range(g):
            h = k * g + gi
            parts = [zeros] * kvh
            parts[k] = qt_ref[0, h * hd:(h + 1) * hd, :]
            qt_scr[k, :, gi * qb:(gi + 1) * qb] = jnp.concatenate(parts, axis=0)
        s_c = _dot(kc_ref[0], qt_scr[k])
        imp = jnp.zeros((ncp, qb), F32)
        probs = []
        for gi in range(g):
            s = jnp.where(mask_c, s_c[:, gi * qb:(gi + 1) * qb] + biasc_ref[k * g + gi], NEG)
            e = jnp.where(mask_c, jnp.exp(s - jnp.max(s, 0, keepdims=True)), 0.0)
            den = jnp.sum(e, 0, keepdims=True)
            p = e / jnp.where(den > 0, den, 1.0)
            imp = imp + p
            probs.append(p.astype(BF16))
        oc_scr[k] = _dot(vct_ref[0, k * hd:(k + 1) * hd, :], jnp.concatenate(probs, axis=1))
        imp_scr[...] = imp + pltpu.roll(imp, ncp - 1, 0)
        cand = imp_scr[pl.ds(0, nsbp, stride=SEL_RATIO), :]
        score = jnp.where(forced, FORCE_SCORE, jnp.where(blk <= cur, cand, -1.0))
        sel_scr[k] = _rank_rows(jnp.where(blk < nsb, score, -2.0), n_sel, nsb)

    per_tile = qb // SEL_BLOCK
    n_tiles = keys_ref.shape[1] // qb
    first = ([jnp.full((1, qb), NEG, F32)] * g, [jnp.zeros((1, qb), F32)] * g)

    def tile_step(tiles, carry, key_col, val_row, acc_ref, window):
        kts = [jnp.clip(kt, 0, n_tiles - 1) for kt, _ in tiles]
        starts = [pl.multiple_of(kt * qb, qb) for kt in kts]
        k_t = jnp.concatenate([keys_ref[0, pl.ds(r0, qb), key_col:key_col + kvd] for r0 in starts], axis=0)
        dist = jnp.concatenate([jnp.where(active, q_pos - (r0 + key_row), -1)
                                for r0, (_, active) in zip(starts, tiles)], axis=0)
        in_range = dist >= 0
        scores = [_dot(k_t, qt_scr[k]) for k in range(kvh)]
        new, updates = [], []
        for k in range(kvh):
            if window:
                valid = in_range & (dist < WINDOW)
            else:
                pieces = []
                for kt in kts:
                    chosen = jnp.zeros((qb, qb), F32)
                    for j in range(per_tile):
                        row = sel_scr[k, pl.ds(per_tile * kt + j, 1), :]
                        chosen = jnp.where(key_row // SEL_BLOCK == j, row, chosen)
                    pieces.append(chosen)
                valid = in_range & (jnp.concatenate(pieces, axis=0) > 0.5)
            ms, ls = carry[k]
            ms2, ls2, alphas, probs = [], [], [], []
            for gi in range(g):
                bias = jnp.concatenate([btile_ref[jnp.maximum(qi - kt, 0), k * g + gi] for kt in kts], axis=0)
                s = jnp.where(valid, scores[k][:, gi * qb:(gi + 1) * qb] + bias, NEG)
                m_new = jnp.maximum(ms[gi], jnp.max(s, 0, keepdims=True))
                alpha = jnp.exp(ms[gi] - m_new)
                p = jnp.exp(s - jnp.where(m_new == NEG, 0.0, m_new))
                ms2.append(m_new)
                ls2.append(alpha * ls[gi] + jnp.sum(p, 0, keepdims=True))
                alphas.append(alpha)
                probs.append(p.astype(BF16))
            new.append((ms2, ls2))
            updates.append((jnp.concatenate(alphas, axis=1), jnp.concatenate(probs, axis=1)))
        for k, (alpha, prob) in enumerate(updates):
            v_t = jnp.concatenate([vt_ref[kt, val_row + k * hd:val_row + (k + 1) * hd, :] for kt in kts], axis=1)
            acc_ref[k] = acc_ref[k] * alpha + _dot(v_t, prob)
        return tuple(new)

    acc_s[...] = jnp.zeros_like(acc_s)
    acc_w[...] = jnp.zeros_like(acc_w)
    sel_args = dict(key_col=0, val_row=0, acc_ref=acc_s, window=False)
    win_args = dict(key_col=kvd, val_row=kvd, acc_ref=acc_w, window=True)

    def sel_pair(i, carry):
        return tile_step([(2 * i, 2 * i <= qi), (2 * i + 1, 2 * i + 1 <= qi)], carry, **sel_args)

    stat_s = lax.fori_loop(0, qi // 2 + 1, sel_pair, (first,) * kvh)
    stat_w = (first,) * kvh
    win_tiles = [(qi - j, qi - j >= 0) for j in range(WINDOW // qb, -1, -1)]
    for j in range(0, len(win_tiles), 2):
        stat_w = tile_step(win_tiles[j:j + 2], stat_w, **win_args)
    gt = gt_ref[0]
    for k in range(kvh):
        for gi in range(g):
            h = k * g + gi
            cols = slice(gi * qb, (gi + 1) * qb)
            l_s, l_w = stat_s[k][1][gi], stat_w[k][1][gi]
            o_s = acc_s[k, :, cols] / jnp.where(l_s > 0, l_s, 1.0)
            o_w = acc_w[k, :, cols] / jnp.where(l_w > 0, l_w, 1.0)
            out_scr[h * hd:(h + 1) * hd, :] = (gt[3 * h:3 * h + 1, :] * oc_scr[k, :, cols]
                                               + gt[3 * h + 1:3 * h + 2, :] * o_s + gt[3 * h + 2:3 * h + 3, :] * o_w)
    o_ref[0] = out_scr[...].T.astype(o_ref.dtype)


def nsa_prompt(qt, gt, kc, vct, keys, vt, rel_bias, b, s):
    qb = Q_BLOCK
    nq = s // qb
    nc = s // CMP_BLOCK
    ncp = kc.shape[1]
    nsb = s // SEL_BLOCK
    nsbp = _round_up(nsb, SUBLANES)
    assert SEL_RATIO * nsbp <= ncp
    n_sel = min(TOP_N, nsb)
    nh = NSA_HEADS
    hq = qt.shape[1]
    kvd = kc.shape[2]
    glanes = NSA_GROUP * qb
    cend = jnp.arange(ncp)[:, None] * CMP_BLOCK + (CMP_BLOCK - 1)
    biasc = jnp.transpose(_bias_of(rel_bias, jnp.arange(s)[None, :] - cend), (2, 0, 1))
    r = jnp.arange(qb)
    dist = jnp.arange(nq)[:, None, None] * qb + r[None, None, :] - r[None, :, None]
    btile = jnp.transpose(_bias_of(rel_bias, dist), (0, 3, 1, 2))
    out = pl.pallas_call(
        functools.partial(_nsa_prompt_kernel, nc=nc, nsb=nsb, n_sel=n_sel), grid=(b, nq),
        in_specs=[pl.BlockSpec((1, hq, qb), lambda bi, i: (bi * nq + i, 0, 0)),
                  pl.BlockSpec((1, LANES, qb), lambda bi, i: (bi * nq + i, 0, 0)),
                  pl.BlockSpec((1, ncp, kvd), lambda bi, i: (bi, 0, 0)),
                  pl.BlockSpec((1, kvd, ncp), lambda bi, i: (bi, 0, 0)),
                  pl.BlockSpec((1, s, 2 * kvd), lambda bi, i: (bi, 0, 0)),
                  pl.BlockSpec((nq, 2 * kvd, qb), lambda bi, i: (bi, 0, 0)),
                  pl.BlockSpec((nh, ncp, qb), lambda bi, i: (0, 0, i)),
                  _full(btile.shape)],
        out_specs=pl.BlockSpec((1, qb, hq), lambda bi, i: (bi, i, 0)),
        out_shape=jax.ShapeDtypeStruct((b, s, hq), BF16),
        scratch_shapes=[pltpu.VMEM((NSA_KV_HEADS, kvd, glanes), BF16), pltpu.VMEM((NSA_KV_HEADS, NSA_HD, glanes), F32),
                        pltpu.VMEM((NSA_KV_HEADS, NSA_HD, glanes), F32), pltpu.VMEM((NSA_KV_HEADS, NSA_HD, glanes), F32),
                        pltpu.VMEM((ncp, qb), F32), pltpu.VMEM((NSA_KV_HEADS, nsbp, qb), F32),
                        pltpu.VMEM((hq, qb), F32)],
        compiler_params=_cparams(("parallel", "arbitrary"), VMEM_LIMIT), name="nsa_prompt",
    )(qt, gt, kc, vct, keys.reshape(b, s, 2 * kvd), vt, biasc, btile)
    return out.reshape(b * s, hq)


def _round_up(x, m):
    return (x + m - 1) // m * m


def prep_even(w_in, pe, wc):
    d, n = w_in.shape
    n_pad = _round_up(n - 3 * NSA_HEADS, LANES) + LANES
    w_pad = jnp.zeros((d, n_pad), BF16).at[:, :n].set(w_in.astype(BF16))
    hd = NSA_HD
    pe2 = jnp.tile(pe, (1, 1, NSA_KV_HEADS))
    wbd = jnp.zeros((2, CMP_BLOCK, NSA_KV_HEADS * hd, NSA_KV_HEADS * hd), F32)
    for i in range(NSA_KV_HEADS):
        wbd = wbd.at[:, :, i * hd:(i + 1) * hd, i * hd:(i + 1) * hd].set(wc)
    return dict(w_in=w_pad, pe2=pe2, wbd=wbd.astype(BF16))


def _memkv_kernel(x_ref, g_ref, w_ref, o_ref, ob_ref):
    y = _dot(_rms(x_ref[...], g_ref[...]).astype(BF16), w_ref[...])
    o_ref[...] = y
    ob_ref[...] = y.astype(BF16)


def memkv(mem, g, w):
    m, d = mem.shape
    n = w.shape[1]
    tm = _row_tile(m, 256)
    return pl.pallas_call(
        _memkv_kernel, grid=(m // tm,),
        in_specs=[pl.BlockSpec((tm, d), lambda i: (i, 0)), _full((1, d)), _full(w.shape)],
        out_specs=[pl.BlockSpec((tm, n), lambda i: (i, 0))] * 2,
        out_shape=[jax.ShapeDtypeStruct((m, n), F32), jax.ShapeDtypeStruct((m, n), BF16)],
        compiler_params=_cparams(("parallel",)), name="memkv",
    )(mem, g.reshape(1, d), w)


def _xattn_core(q, kv, hd):
    nh = q.shape[1] // hd
    outs = []
    for h in range(nh):
        s = _dot_nt(q[:, h * hd:(h + 1) * hd], kv[:, h * hd:(h + 1) * hd])
        e = jnp.exp(s - jnp.max(s, -1, keepdims=True))
        p = e / jnp.sum(e, -1, keepdims=True)
        outs.append(_dot(p.astype(BF16), kv[:, (nh + h) * hd:(nh + h + 1) * hd]))
    return jnp.concatenate(outs, axis=1).astype(BF16)


def _xattn_prompt_kernel(x_ref, g_ref, wq_ref, kv_ref, wo_ref, o_ref, *, hd):
    x = x_ref[0]
    q = (_dot(_rms(x, g_ref[...]).astype(BF16), wq_ref[...]) * (hd ** -0.5)).astype(BF16)
    o = _xattn_core(q, kv_ref[0], hd)
    o_ref[0] = x + _dot(o, wo_ref[...])


def xattn_prompt(x, g, wq, kvb, wo, b, s):
    d = x.shape[1]
    mt = kvb.shape[0] // b
    tm = _row_tile(s, 512)
    out = pl.pallas_call(
        functools.partial(_xattn_prompt_kernel, hd=d // X_HEADS), grid=(b, s // tm),
        in_specs=[pl.BlockSpec((1, tm, d), lambda bi, i: (bi, i, 0)), _full((1, d)), _full(wq.shape),
                  pl.BlockSpec((1, mt, kvb.shape[1]), lambda bi, i: (bi, 0, 0)), _full(wo.shape)],
        out_specs=pl.BlockSpec((1, tm, d), lambda bi, i: (bi, i, 0)),
        out_shape=jax.ShapeDtypeStruct((b, s, d), F32),
        compiler_params=_cparams(("parallel", "parallel"), VMEM_LIMIT), name="xattn_prompt",
    )(x.reshape(b, s, d), g.reshape(1, d), wq, kvb.reshape(b, mt, -1), wo)
    return out.reshape(b * s, d)


def _xattn_sample_kernel(x_ref, g_ref, wq_ref, kv_ref, wo_ref, o_ref, q_scr, a_scr, *, hd):
    bi = pl.program_id(0)
    nb = pl.num_programs(0)
    nh = wq_ref.shape[1] // hd

    @pl.when(bi == 0)
    def _():
        q_scr[...] = _dot(_rms(x_ref[...], g_ref[...]).astype(BF16), wq_ref[...]) * (hd ** -0.5)

    q = jnp.broadcast_to(q_scr[pl.ds(bi, 1), :], (SUBLANES, q_scr.shape[1])).astype(BF16)
    kv = kv_ref[0].astype(BF16)
    a_scr[pl.ds(bi, 1), :] = _xattn_core(q, kv, hd)[0:1, :].astype(F32)

    @pl.when(bi == nb - 1)
    def _():
        o_ref[...] = x_ref[...] + _dot(a_scr[...].astype(BF16), wo_ref[...])


def xattn_sample(x, g, wq, kv_cache, wo, layer):
    bd, d = x.shape
    mt, n = kv_cache.shape[1:]
    return pl.pallas_call(
        functools.partial(_xattn_sample_kernel, hd=d // X_HEADS), grid=(bd,),
        in_specs=[_full((bd, d)), _full((1, d)), _full(wq.shape),
                  pl.BlockSpec((1, mt, n), lambda bi: (layer * bd + bi, 0, 0)), _full(wo.shape)],
        out_specs=_full((bd, d)),
        out_shape=jax.ShapeDtypeStruct((bd, d), F32),
        scratch_shapes=[pltpu.VMEM((bd, wq.shape[1]), F32), pltpu.VMEM((bd, wq.shape[1]), F32)],
        compiler_params=_cparams(("arbitrary",), VMEM_LIMIT), name="xattn_sample",
    )(x, g.reshape(1, d), wq, kv_cache, wo)


def _ffn_kernel(x_ref, g_ref, wg_ref, wu_ref, wd_ref, o_ref, h_scr, acc_scr):
    c = pl.program_id(1)

    @pl.when(c == 0)
    def _():
        h_scr[...] = _rms(x_ref[...], g_ref[...]).astype(BF16)
        acc_scr[...] = x_ref[...]

    h = h_scr[...]
    gate = _dot(h, wg_ref[...])
    up = _dot(h, wu_ref[...])
    act = (gate * jax.nn.sigmoid(gate) * up).astype(BF16)
    acc_scr[...] += _dot(act, wd_ref[...])

    @pl.when(c == pl.num_programs(1) - 1)
    def _():
        o_ref[...] = acc_scr[...]


def _ff_chunk(dff, pref):
    c = dff
    for n in range(1, dff // LANES + 1):
        if dff % n == 0 and (dff // n) % LANES == 0 and dff // n <= pref:
            c = dff // n
            break
    return c


def ffn(x, g, w_gu, w_dn):
    m, d = x.shape
    dff = w_dn.shape[0]
    tm = _row_tile(m, 512)
    fc = _ff_chunk(dff, 1408)
    nch = dff // fc
    return pl.pallas_call(
        _ffn_kernel, grid=(m // tm, nch),
        in_specs=[pl.BlockSpec((tm, d), lambda i, c: (i, 0)), _full((1, d)),
                  pl.BlockSpec((d, fc), lambda i, c: (0, c)),
                  pl.BlockSpec((d, fc), lambda i, c: (0, nch + c)),
                  pl.BlockSpec((fc, d), lambda i, c: (c, 0))],
        out_specs=pl.BlockSpec((tm, d), lambda i, c: (i, 0)),
        out_shape=jax.ShapeDtypeStruct((m, d), F32),
        scratch_shapes=[pltpu.VMEM((tm, d), BF16), pltpu.VMEM((tm, d), F32)],
        compiler_params=_cparams(("parallel", "arbitrary"), VMEM_LIMIT), name="ffn",
    )(x, g.reshape(1, d), w_gu, w_gu, w_dn)


def _router_kernel(x_ref, g_ref, w_ref, b_ref, comb_ref, h_ref, mask_ref, cnt_ref, *, ne):
    h = _rms(x_ref[...], g_ref[...]).astype(BF16)
    h_ref[...] = h
    logits = _dot(h, w_ref[...]) + b_ref[...]
    lane = lax.broadcasted_iota(jnp.int32, logits.shape, 1)
    logits = jnp.where(lane < ne, logits, -jnp.inf)
    v1 = jnp.max(logits, -1, keepdims=True)
    i1 = jnp.min(jnp.where(logits == v1, lane, LANES), -1, keepdims=True)
    rest = jnp.where(lane == i1, -jnp.inf, logits)
    v2 = jnp.max(rest, -1, keepdims=True)
    i2 = jnp.min(jnp.where(rest == v2, lane, LANES), -1, keepdims=True)
    e2 = jnp.exp(v2 - v1)
    den = 1.0 + e2
    comb_ref[...] = jnp.where(lane == i1, 1.0 / den, 0.0) + jnp.where(lane == i2, e2 / den, 0.0)
    chosen = jnp.where((lane == i1) | (lane == i2), 1.0, 0.0)
    mask_ref[...] = chosen.astype(BF16)

    @pl.when(pl.program_id(0) == 0)
    def _():
        cnt_ref[...] = jnp.zeros_like(cnt_ref)

    cnt_ref[0:1, :] += jnp.sum(chosen, axis=0, keepdims=True)


def router(x, g, w_r, b_r):
    m, d = x.shape
    ne = w_r.shape[1]
    w_pad = jnp.zeros((d, LANES), BF16).at[:, :ne].set(w_r.astype(BF16))
    b_pad = jnp.zeros((1, LANES), F32).at[0, :ne].set(b_r.astype(F32))
    tm = _row_tile(m, 512)
    row = lambda n: pl.BlockSpec((tm, n), lambda i: (i, 0))
    return pl.pallas_call(
        functools.partial(_router_kernel, ne=ne), grid=(m // tm,),
        in_specs=[row(d), _full((1, d)), _full((d, LANES)), _full((1, LANES))],
        out_specs=[row(LANES), row(d), row(LANES), _full((SUBLANES, LANES))],
        out_shape=[jax.ShapeDtypeStruct((m, LANES), F32), jax.ShapeDtypeStruct((m, d), BF16),
                   jax.ShapeDtypeStruct((m, LANES), BF16), jax.ShapeDtypeStruct((SUBLANES, LANES), F32)],
        compiler_params=_cparams(("arbitrary",)), name="router",
    )(x, g.reshape(1, d), w_pad, b_pad)


def _moe_kernel(x_ref, h_ref, comb_ref, wg_ref, wu_ref, wd_ref, o_ref, acc_scr):
    e = pl.program_id(1)

    @pl.when(e == 0)
    def _():
        acc_scr[...] = jnp.zeros_like(acc_scr)

    h = h_ref[...]
    gate = _dot(h, wg_ref[0])
    up = _dot(h, wu_ref[0])
    act = (gate * jax.nn.sigmoid(gate) * up).astype(BF16)
    y = _dot(act, wd_ref[0])
    comb = comb_ref[...]
    lane = lax.broadcasted_iota(jnp.int32, comb.shape, 1)
    acc_scr[...] += jnp.sum(jnp.where(lane == e, comb, 0.0), -1, keepdims=True) * y

    @pl.when(e == pl.num_programs(1) - 1)
    def _():
        o_ref[...] = x_ref[...] + acc_scr[...]


def moe(x, h, comb, w_gu, w_dn):
    m, d = x.shape
    ne, dfe = w_dn.shape[:2]
    tm = _row_tile(m, 512)
    return pl.pallas_call(
        _moe_kernel, grid=(m // tm, ne),
        in_specs=[pl.BlockSpec((tm, d), lambda i, e: (i, 0)), pl.BlockSpec((tm, d), lambda i, e: (i, 0)),
                  pl.BlockSpec((tm, LANES), lambda i, e: (i, 0)),
                  pl.BlockSpec((1, d, dfe), lambda i, e: (e, 0, 0)),
                  pl.BlockSpec((1, d, dfe), lambda i, e: (e, 0, 1)),
                  pl.BlockSpec((1, dfe, d), lambda i, e: (e, 0, 0))],
        out_specs=pl.BlockSpec((tm, d), lambda i, e: (i, 0)),
        out_shape=jax.ShapeDtypeStruct((m, d), F32),
        scratch_shapes=[pltpu.VMEM((tm, d), F32)],
        compiler_params=_cparams(("parallel", "arbitrary"), VMEM_LIMIT), name="moe",
    )(x, h, comb, w_gu, w_gu, w_dn)


MOE_TILE = 256


def _moe_pos_kernel(mask_ref, comb_ref, tri_ref, base_ref, post_ref, pos2_ref, wab_ref, stab_ref, run_scr, *, nep):
    sb = pl.program_id(0)
    nb = pl.num_programs(0)

    @pl.when(sb == 0)
    def _():
        run_scr[...] = jnp.zeros_like(run_scr)
        stab_ref[...] = jnp.zeros_like(stab_ref)

    a = mask_ref[...]
    af = a.astype(F32)
    start = base_ref[...] + run_scr[...]
    stab_ref[pl.ds(sb, 1), :] = start.astype(jnp.int32)
    rank = _dot(tri_ref[...], a)
    pos = jnp.where(af > 0, start + rank, -1.0)
    lane = lax.broadcasted_iota(jnp.int32, pos.shape, 1)
    first_e = jnp.min(jnp.where(af > 0, lane, LANES), -1, keepdims=True)
    last_e = jnp.max(jnp.where(af > 0, lane, -1), -1, keepdims=True)
    comb = comb_ref[...]
    w_a = jnp.sum(jnp.where(lane == first_e, comb, 0.0), -1, keepdims=True)
    w_b = jnp.sum(jnp.where(lane == last_e, comb, 0.0), -1, keepdims=True)
    wab_ref[...] = jnp.where(lane == 0, w_a, jnp.where(lane == 1, w_b, 0.0))
    pos_t = pos.T[0:nep, :]
    post_ref[0] = pos_t.astype(jnp.int32)
    row = lax.broadcasted_iota(jnp.int32, pos_t.shape, 0)
    first_r = jnp.min(jnp.where(pos_t >= 0, row, nep), 0, keepdims=True)
    last_r = jnp.max(jnp.where(pos_t >= 0, row, -1), 0, keepdims=True)
    pos_a = jnp.sum(jnp.where(row == first_r, pos_t, 0.0), 0, keepdims=True)
    pos_b = jnp.sum(jnp.where(row == last_r, pos_t, 0.0), 0, keepdims=True)
    pos2_ref[0] = jnp.where(row == 0, pos_a, jnp.where(row == 1, pos_b, 0.0)).astype(jnp.int32)
    run_scr[...] += jnp.sum(af, axis=0, keepdims=True)

    @pl.when(sb == nb - 1)
    def _():
        stab_ref[pl.ds(nb, 1), :] = (base_ref[...] + run_scr[...]).astype(jnp.int32)


def _moe_expert_kernel(te_ref, lo_ref, hi_ref, nt_ref, h_ref, post_ref, wg_ref, wu_ref, wd_ref, y_ref,
                       hbuf, xg_scr, sem, *, t):
    i = pl.program_id(0)

    @pl.when(i >= nt_ref[0])
    def _():
        y_ref[...] = jnp.zeros_like(y_ref)

    @pl.when(i < nt_ref[0])
    def _():
        e = te_ref[i]
        lo = lo_ref[i]
        n = hi_ref[i] - lo + 1

        def copy(sb, slot):
            return pltpu.make_async_copy(h_ref.at[pl.ds(pl.multiple_of(sb * t, t), t), :], hbuf.at[slot], sem.at[slot])

        copy(lo, 0).start()
        xg_scr[...] = jnp.zeros_like(xg_scr)
        row = i * t + lax.broadcasted_iota(jnp.int32, (t, 1), 0)

        def body(j, c):
            slot = j % 2
            copy(lo + j, slot).wait()

            @pl.when(j + 1 < n)
            def _():
                copy(lo + j + 1, 1 - slot).start()

            src_pos = post_ref[lo + j, pl.ds(e, 1), :]
            onehot = jnp.where(src_pos == row, 1.0, 0.0).astype(BF16)
            xg_scr[...] += _dot(onehot, hbuf[slot])
            return c

        lax.fori_loop(0, n, body, 0)
        x = xg_scr[...].astype(BF16)
        gate = _dot(x, wg_ref[0])
        up = _dot(x, wu_ref[0])
        act = (gate * jax.nn.sigmoid(gate) * up).astype(BF16)
        y_ref[...] = _dot(act, wd_ref[0])


def _moe_combine_kernel(x_ref, wab_ref, pos_ref, nxt_ref, ys_ref, o_ref, ybuf, sem, *, t):
    sb = pl.program_id(0)
    nb = pl.num_programs(0)

    def row_copy(p_ref, which, slot, tok):
        return pltpu.make_async_copy(ys_ref.at[pl.ds(p_ref[0, which, tok], 1), :],
                                     ybuf.at[slot, which, pl.ds(tok, 1), :], sem.at[slot])

    def start_all(p_ref, slot):
        def body(tok, c):
            row_copy(p_ref, 0, slot, tok).start()
            row_copy(p_ref, 1, slot, tok).start()
            return c
        lax.fori_loop(0, t, body, 0, unroll=8)

    @pl.when(sb == 0)
    def _():
        start_all(pos_ref, 0)

    @pl.when(sb + 1 < nb)
    def _():
        start_all(nxt_ref, (sb + 1) % 2)

    slot = sb % 2

    def wait_body(tok, c):
        row_copy(pos_ref, 0, slot, tok).wait()
        row_copy(pos_ref, 1, slot, tok).wait()
        return c
    lax.fori_loop(0, t, wait_body, 0, unroll=8)
    w = wab_ref[...]
    o_ref[...] = x_ref[...] + w[:, 0:1] * ybuf[slot, 0] + w[:, 1:2] * ybuf[slot, 1]


def moe_grouped(x, h, comb, mask, counts, w_gu, w_dn):
    m, d = x.shape
    ne, dfe = w_dn.shape[:2]
    t = MOE_TILE
    assert m % t == 0
    nb = m // t
    nep = _round_up(ne, SUBLANES)
    nbp = _round_up(nb + 1, SUBLANES)
    k_top = TOP_K
    nt_max = k_top * m // t + ne
    cnt = counts[0, :ne].astype(jnp.int32)
    cnt_pad = (cnt + t - 1) // t * t
    ends = jnp.cumsum(cnt_pad)
    base = ends - cnt_pad
    base_row = jnp.zeros((1, LANES), F32).at[0, :ne].set(base.astype(F32))
    idx = lax.broadcasted_iota(jnp.int32, (t, t), 0)
    tri = jnp.where(lax.broadcasted_iota(jnp.int32, (t, t), 1) < idx, 1.0, 0.0).astype(BF16)
    blk = lambda n2: pl.BlockSpec((t, n2), lambda i: (i, 0))
    post, pos2, wab, stab = pl.pallas_call(
        functools.partial(_moe_pos_kernel, nep=nep), grid=(nb,),
        in_specs=[blk(LANES), blk(LANES), _full((t, t)), _full((1, LANES))],
        out_specs=[pl.BlockSpec((1, nep, t), lambda i: (i, 0, 0)), pl.BlockSpec((1, nep, t), lambda i: (i, 0, 0)),
                   blk(LANES), _full((nbp, LANES))],
        out_shape=[jax.ShapeDtypeStruct((nb, nep, t), jnp.int32), jax.ShapeDtypeStruct((nb, nep, t), jnp.int32),
                   jax.ShapeDtypeStruct((m, LANES), F32), jax.ShapeDtypeStruct((nbp, LANES), jnp.int32)],
        scratch_shapes=[pltpu.VMEM((1, LANES), F32)],
        compiler_params=_cparams(("arbitrary",)), name="moe_positions",
    )(mask, comb, tri, base_row)
    r0 = jnp.arange(nt_max, dtype=jnp.int32) * t
    tile_e = jnp.minimum(jnp.sum(ends[None, :] <= r0[:, None], axis=1), ne - 1).astype(jnp.int32)
    n_tiles = (ends[-1] // t).astype(jnp.int32).reshape(1)
    s_e = stab[:nb + 1, :ne][:, tile_e]
    lo = jnp.sum(s_e[1:] <= r0[None, :], axis=0)
    hi = jnp.sum(s_e[:nb] < r0[None, :] + t, axis=0) - 1
    lo = jnp.clip(lo, 0, nb - 1).astype(jnp.int32)
    hi = jnp.clip(hi, lo, nb - 1).astype(jnp.int32)
    w_spec = lambda shape, col: pl.BlockSpec(shape, lambda i, te, lo_, hi_, nt: (te[i], 0, col))
    grid_spec = pltpu.PrefetchScalarGridSpec(
        num_scalar_prefetch=4, grid=(nt_max,),
        in_specs=[pl.BlockSpec(memory_space=pl.ANY),
                  pl.BlockSpec((nb, nep, t), lambda i, te, lo_, hi_, nt: (0, 0, 0)),
                  w_spec((1, d, dfe), 0), w_spec((1, d, dfe), 1), w_spec((1, dfe, d), 0)],
        out_specs=pl.BlockSpec((t, d), lambda i, te, lo_, hi_, nt: (i, 0)),
        scratch_shapes=[pltpu.VMEM((2, t, d), BF16), pltpu.VMEM((t, d), F32), pltpu.SemaphoreType.DMA((2,))])
    ys = pl.pallas_call(
        functools.partial(_moe_expert_kernel, t=t), grid_spec=grid_spec,
        out_shape=jax.ShapeDtypeStruct((nt_max * t, d), F32),
        compiler_params=_cparams(("arbitrary",), VMEM_LIMIT), name="moe_experts",
    )(tile_e, lo, hi, n_tiles, h, post, w_gu, w_gu, w_dn)
    smem_blk = lambda f: pl.BlockSpec((1, nep, t), f, memory_space=pltpu.SMEM)
    return pl.pallas_call(
        functools.partial(_moe_combine_kernel, t=t), grid=(nb,),
        in_specs=[blk(d), blk(LANES), smem_blk(lambda i: (i, 0, 0)),
                  smem_blk(lambda i: (jnp.minimum(i + 1, nb - 1), 0, 0)), pl.BlockSpec(memory_space=pl.ANY)],
        out_specs=blk(d),
        out_shape=jax.ShapeDtypeStruct((m, d), F32),
        scratch_shapes=[pltpu.VMEM((2, 2, t, d), F32), pltpu.SemaphoreType.DMA((2,))],
        compiler_params=_cparams(("arbitrary",), VMEM_LIMIT), name="moe_combine",
    )(x, wab, pos2, pos2, ys)


def _inproj_odd_kernel(x_ref, g_ref, w_ref, bias_ref, q_ref, k_ref, v_ref, og_ref, gc_ref, gr_ref, *, hq, hv, nh):
    xn = _rms(x_ref[...], g_ref[...]).astype(BF16)

    def mm(lo, hi):
        return _dot(xn, w_ref[:, lo:hi])

    dk = hq // nh
    q_ref[...] = mm(0, hq).astype(BF16)
    k_ref[...] = (mm(hq, 2 * hq) * (dk ** -0.5)).astype(BF16)
    v_ref[...] = mm(2 * hq, 2 * hq + hv).astype(BF16)
    og_ref[...] = jax.nn.sigmoid(mm(2 * hq + hv, 2 * hq + 2 * hv))
    gi = mm(2 * hq + 2 * hv, 2 * hq + 2 * hv + LANES) + bias_ref[...]
    lane = lax.broadcasted_iota(jnp.int32, gi.shape, 1)
    gates = jnp.where(lane < nh, gi, jax.nn.log_sigmoid(gi))
    gc_ref[...] = gates
    gr_ref[...] = gates.T[0:SUBLANES, :]


def inproj_odd(x, g, w_pad, gate_bias):
    m, d = x.shape
    nh = MLSTM_HEADS
    hq = hv = d
    tm = _row_tile(m, 256)
    row = lambda n: pl.BlockSpec((tm, n), lambda i: (i, 0))
    outs = [(hq, BF16), (hq, BF16), (hv, BF16), (hv, F32), (LANES, F32)]
    if tm % LANES:
        gr_spec = _full((SUBLANES, m))
    else:
        gr_spec = pl.BlockSpec((SUBLANES, tm), lambda i: (0, i))
    return pl.pallas_call(
        functools.partial(_inproj_odd_kernel, hq=hq, hv=hv, nh=nh), grid=(m // tm,),
        in_specs=[row(d), _full((1, d)), _full(w_pad.shape), _full((1, LANES))],
        out_specs=[row(n) for n, _ in outs] + [gr_spec],
        out_shape=[jax.ShapeDtypeStruct((m, n), t) for n, t in outs] + [jax.ShapeDtypeStruct((SUBLANES, m), F32)],
        compiler_params=_cparams(("parallel",), VMEM_LIMIT), name="inproj_odd",
    )(x, g.reshape(1, d), w_pad, gate_bias)


def prep_odd(w_in, b_i, b_f):
    d, n = w_in.shape
    n_pad = _round_up(n - 2 * MLSTM_HEADS, LANES) + LANES
    w_pad = jnp.zeros((d, n_pad), BF16).at[:, :n].set(w_in.astype(BF16))
    bias = jnp.zeros((1, LANES), F32).at[0, :2 * MLSTM_HEADS].set(jnp.concatenate([b_i, b_f]).astype(F32))
    return dict(w_in=w_pad, bias=bias)


def _mlstm_prompt_kernel(q_ref, k_ref, v_ref, og_ref, gc_ref, gr_ref, gain_ref, hn_ref, c_ref, n_ref, m_ref,
                         *, nh, dk, dv, ln):
    ci = pl.program_id(1)

    @pl.when(ci == 0)
    def _():
        c_ref[...] = jnp.zeros_like(c_ref)
        n_ref[...] = jnp.zeros_like(n_ref)
        m_ref[...] = jnp.full(m_ref.shape, NEG, F32)

    row = lax.broadcasted_iota(jnp.int32, (ln, ln), 0)
    col = lax.broadcasted_iota(jnp.int32, (ln, ln), 1)
    tri = row >= col
    gc = gc_ref[...]
    gr = gr_ref[...]
    for h in range(nh):
        q = q_ref[:, h * dk:(h + 1) * dk]
        k = k_ref[:, h * dk:(h + 1) * dk]
        v = v_ref[:, h * dv:(h + 1) * dv]
        ig_c, lf_c = gc[:, h:h + 1], gc[:, nh + h:nh + h + 1]
        ig_r, lf_r = gr[h:h + 1, :], gr[nh + h:nh + h + 1, :]
        b_c = jnp.sum(jnp.where(tri, lf_r, 0.0), axis=1, keepdims=True)
        b_r = jnp.sum(jnp.where(row <= col, lf_c, 0.0), axis=0, keepdims=True)
        m_prev = m_ref[0, h:h + 1, 0:1]
        c_prev = c_ref[0, h]
        n_prev = n_ref[0, h:h + 1, :]
        dmat = jnp.where(tri, b_c - b_r + ig_r, NEG)
        inter = b_c + m_prev
        mt = jnp.maximum(inter, jnp.max(dmat, -1, keepdims=True))
        wm = jnp.exp(dmat - mt)
        a = jnp.exp(inter - mt)
        wqk = wm * _dot_nt(q, k)
        num = a * _dot_nt(q, c_prev.astype(BF16)) + _dot(wqk.astype(BF16), v)
        den = a * jnp.sum(q.astype(F32) * n_prev, -1, keepdims=True) + jnp.sum(wqk, -1, keepdims=True)
        hh = num / jnp.maximum(jnp.abs(den), jnp.exp(-mt))
        b_end = b_c[ln - 1:ln, :]
        m_new = mt[ln - 1:ln, :]
        a_end = jnp.exp(b_end + m_prev - m_new)
        w_s = jnp.exp(b_end - b_c + ig_c - m_new)
        c_ref[0, h] = a_end * c_prev + _dot_tn((v.astype(F32) * w_s).astype(BF16), k)
        n_ref[0, h:h + 1, :] = a_end * n_prev + jnp.sum(w_s * k.astype(F32), axis=0, keepdims=True)
        m_ref[0, h:h + 1, :] = jnp.broadcast_to(m_new, (1, m_ref.shape[2]))
        hn = hh * lax.rsqrt(jnp.mean(hh * hh, -1, keepdims=True) + RMS_EPS)
        hn = hn * gain_ref[:, h * dv:(h + 1) * dv] * og_ref[:, h * dv:(h + 1) * dv]
        hn_ref[:, h * dv:(h + 1) * dv] = hn.astype(hn_ref.dtype)


def mlstm_prompt(q, k, v, og, gc, gr, gain, b, s):
    m, d = q.shape
    nh = MLSTM_HEADS
    dk = dv = d // nh
    ln = _row_tile(s, MLSTM_CHUNK)
    nch = s // ln
    row = lambda n: pl.BlockSpec((ln, n), lambda bi, ci: (bi * nch + ci, 0))
    return pl.pallas_call(
        functools.partial(_mlstm_prompt_kernel, nh=nh, dk=dk, dv=dv, ln=ln), grid=(b, nch),
        in_specs=[row(d), row(d), row(d), row(d), row(LANES),
                  pl.BlockSpec((SUBLANES, ln), lambda bi, ci: (0, bi * nch + ci)), _full((1, d))],
        out_specs=[row(d), pl.BlockSpec((1, nh, dv, dk), lambda bi, ci: (bi, 0, 0, 0)),
                   pl.BlockSpec((1, nh, dk), lambda bi, ci: (bi, 0, 0)),
                   pl.BlockSpec((1, nh, LANES), lambda bi, ci: (bi, 0, 0))],
        out_shape=[jax.ShapeDtypeStruct((m, d), BF16), jax.ShapeDtypeStruct((b, nh, dv, dk), F32),
                   jax.ShapeDtypeStruct((b, nh, dk), F32), jax.ShapeDtypeStruct((b, nh, LANES), F32)],
        compiler_params=_cparams(("parallel", "arbitrary"), VMEM_LIMIT), name="mlstm_prompt",
    )(q, k, v, og, gc, gr, gain.reshape(1, d))


def _mlstm_sample_kernel(q_ref, k_ref, v_ref, og_ref, g_ref, gain_ref, c_ref, n_ref, m_ref,
                         hn_ref, co_ref, no_ref, mo_ref, *, nh):
    row = lax.broadcasted_iota(jnp.int32, (SUBLANES, 1), 0)
    for h in range(nh):
        q = q_ref[0, h:h + 1, :]
        k = k_ref[0, h:h + 1, :]
        v = v_ref[0, h:h + 1, :].astype(F32)
        ig = g_ref[0, h:h + 1, 0:1]
        lf = g_ref[0, h:h + 1, 1:2]
        m_prev = m_ref[0, h:h + 1, :]
        c_prev = c_ref[0, h]
        n_prev = n_ref[0, h:h + 1, :]
        inter = lf + m_prev
        mt = jnp.maximum(inter, ig)
        wm = jnp.exp(ig - mt)
        a = jnp.exp(inter - mt)
        q8 = jnp.broadcast_to(q, (SUBLANES, q.shape[1]))
        cq = _dot_nt(q8, c_prev.astype(BF16))[0:1, :]
        wqk = wm * jnp.sum(q.astype(F32) * k.astype(F32), -1, keepdims=True)
        num = a * cq + wqk * v
        den = a * jnp.sum(n_prev * q.astype(F32), -1, keepdims=True) + wqk
        hh = num / jnp.maximum(jnp.abs(den), jnp.exp(-mt))
        v8 = jnp.where(row == 0, jnp.broadcast_to(v * wm, (SUBLANES, v.shape[1])), 0.0).astype(BF16)
        k8 = jnp.broadcast_to(k, (SUBLANES, k.shape[1]))
        co_ref[0, h] = a * c_prev + _dot_tn(v8, k8)
        no_ref[0, h:h + 1, :] = a * n_prev + wm * k.astype(F32)
        mo_ref[0, h:h + 1, :] = mt
        hn = hh * lax.rsqrt(jnp.mean(hh * hh, -1, keepdims=True) + RMS_EPS)
        hn_ref[0, h:h + 1, :] = (hn * gain_ref[h:h + 1, :] * og_ref[0, h:h + 1, :]).astype(hn_ref.dtype)


def mlstm_sample(q, k, v, og, gc, gain, c, n, m):
    bd, d = q.shape
    nh = MLSTM_HEADS
    dk = d // nh
    heads = lambda a: a.reshape(bd, nh, dk)
    g2 = jnp.transpose(gc[:, :2 * nh].reshape(bd, 2, nh), (0, 2, 1))
    blk3 = lambda n2: pl.BlockSpec((1, nh, n2), lambda bi: (bi, 0, 0))
    cspec = pl.BlockSpec((1, nh, dk, dk), lambda bi: (bi, 0, 0, 0))
    hn, co, no, mo = pl.pallas_call(
        functools.partial(_mlstm_sample_kernel, nh=nh), grid=(bd,),
        in_specs=[blk3(dk), blk3(dk), blk3(dk), blk3(dk), blk3(2), _full((nh, dk)), cspec, blk3(dk), blk3(1)],
        out_specs=[blk3(dk), cspec, blk3(dk), blk3(1)],
        out_shape=[jax.ShapeDtypeStruct((bd, nh, dk), BF16), jax.ShapeDtypeStruct(c.shape, F32),
                   jax.ShapeDtypeStruct(n.shape, F32), jax.ShapeDtypeStruct((bd, nh, 1), F32)],
        compiler_params=_cparams(("parallel",)), name="mlstm_sample",
    )(heads(q), heads(k), heads(v), heads(og), g2, gain.reshape(nh, dk), c, n, m.reshape(bd, nh, 1))
    return hn.reshape(bd, d), co, no, mo.reshape(bd, nh)


def _head_pad(q, keep):
    q2 = jnp.concatenate([q] * NSA_KV_HEADS, axis=1)
    row = lax.broadcasted_iota(jnp.int32, q2.shape, 0)
    lane = lax.broadcasted_iota(jnp.int32, q2.shape, 1)
    return jnp.where((lane // NSA_HD == row // NSA_GROUP) & keep(row), q2, jnp.zeros_like(q2))


def _nsa_sample_cmp_kernel(pt_ref, q_ref, pages_ref, pe_ref, w_ref, biasc_ref, oc_ref, idx_ref, xbuf, sem,
                           *, n_pages, nc, ncp, nsb, n_sel, past):
    b = pl.program_id(0)
    nb = pl.num_programs(0)
    kvd = NSA_KV_HEADS * NSA_HD

    def page_copy(bb, slot, p, sl):
        return pltpu.make_async_copy(pages_ref.at[pt_ref[bb * n_pages + p], :, pl.ds(sl * kvd, kvd)],
                                     xbuf.at[slot, sl, pl.ds(p * PAGE_SIZE, PAGE_SIZE), :], sem.at[slot])

    def start_all(bb, slot):
        def body(p, c):
            page_copy(bb, slot, p, 0).start()
            page_copy(bb, slot, p, 1).start()
            return c
        lax.fori_loop(0, n_pages, body, 0)

    @pl.when(b == 0)
    def _():
        start_all(0, 0)

    @pl.when(b + 1 < nb)
    def _():
        start_all(b + 1, (b + 1) % 2)

    slot = b % 2

    def wait_body(p, c):
        page_copy(b, slot, p, 0).wait()
        page_copy(b, slot, p, 1).wait()
        return c
    lax.fori_loop(0, n_pages, wait_body, 0)

    acc = _compress_accumulate(lambda sl, j: xbuf[slot, sl, pl.ds(j, nc, stride=CMP_BLOCK), :], pe_ref, w_ref, nc)
    kc = acc[:, 0:kvd].astype(BF16)
    vc = acc[:, kvd:2 * kvd].astype(BF16)
    q = q_ref[0]
    nh = q.shape[0]
    qpad = _head_pad(q, lambda r: r >= 0)
    s = _dot_nt(qpad, kc)
    s = s + biasc_ref[:, 0:nc]
    e = jnp.exp(s - jnp.max(s, -1, keepdims=True))
    p_c = e / jnp.sum(e, -1, keepdims=True)
    o = _dot(p_c.astype(BF16), vc)
    row = lax.broadcasted_iota(jnp.int32, (nh, NSA_HD), 0)
    o_h = o[:, 0:NSA_HD]
    for k in range(1, NSA_KV_HEADS):
        o_h = jnp.where(row // NSA_GROUP == k, o[:, k * NSA_HD:(k + 1) * NSA_HD], o_h)
    oc_ref[0] = o_h
    prow = lax.broadcasted_iota(jnp.int32, p_c.shape, 0)
    lane = lax.broadcasted_iota(jnp.int32, (1, ncp), 1)
    blk = lane // 2
    cur = past // SEL_BLOCK
    forced = (blk == 0) | (blk == cur) | (blk == cur - 1)
    is_cand = ((lane % 2) == 0) & (lane < 2 * nsb)
    nselp = idx_ref.shape[1]
    rsel = lax.broadcasted_iota(jnp.int32, (nselp, 1), 0).astype(F32)
    out_lane = lax.broadcasted_iota(jnp.int32, (nselp, LANES), 1)
    result = jnp.full((nselp, LANES), -1, jnp.int32)
    for k in range(NSA_KV_HEADS):
        imp = jnp.sum(jnp.where(prow // NSA_GROUP == k, p_c, 0.0), axis=0, keepdims=True)
        imp = jnp.concatenate([imp, jnp.zeros((1, ncp - nc), F32)], axis=1)
        imp = _pair_sum(imp)
        score = jnp.where(forced, FORCE_SCORE, jnp.where(blk <= cur, imp, -1.0))
        score = jnp.where(is_cand, score, -2.0)
        sel, rank = _select_blocks(score, n_sel, nsb)
        hit = (rank == rsel) & (sel > 0.5)
        idx = jnp.sum(jnp.where(hit, (blk + 1).astype(F32), 0.0), axis=1, keepdims=True) - 1.0
        result = jnp.where(out_lane == k, idx.astype(jnp.int32), result)
    idx_ref[0] = result


def nsa_sample_cmp(q8, pages, page_table, pe2, wbd, rel_bias):
    bd, nh, hd = q8.shape
    n_pages = page_table.shape[1]
    past = n_pages * PAGE_SIZE
    nc = past // CMP_BLOCK
    nsb = -(-(past + 1) // SEL_BLOCK)
    n_sel = min(TOP_N, nsb)
    ncp = _round_up(max(nc, SEL_RATIO * nsb), LANES)
    nselp = _round_up(n_sel, SUBLANES)
    kvd = wbd.shape[-1]
    cend = jnp.arange(nc) * CMP_BLOCK + (CMP_BLOCK - 1)
    biasc = jnp.zeros((nh, ncp), F32).at[:, :nc].set(_bias_of(rel_bias, past - cend).T)
    grid_spec = pltpu.PrefetchScalarGridSpec(
        num_scalar_prefetch=1, grid=(bd,),
        in_specs=[pl.BlockSpec((1, nh, hd), lambda bi, pt: (bi, 0, 0)),
                  pl.BlockSpec(memory_space=pl.ANY),
                  pl.BlockSpec(pe2.shape, lambda bi, pt: (0, 0, 0)),
                  pl.BlockSpec(wbd.shape, lambda bi, pt: (0, 0, 0, 0)),
                  pl.BlockSpec((nh, ncp), lambda bi, pt: (0, 0))],
        out_specs=[pl.BlockSpec((1, nh, hd), lambda bi, pt: (bi, 0, 0)),
                   pl.BlockSpec((1, nselp, LANES), lambda bi, pt: (bi, 0, 0))],
        scratch_shapes=[pltpu.VMEM((2, 2, past, kvd), F32), pltpu.SemaphoreType.DMA((2,))])
    oc, idx = pl.pallas_call(
        functools.partial(_nsa_sample_cmp_kernel, n_pages=n_pages, nc=nc, ncp=ncp, nsb=nsb, n_sel=n_sel, past=past),
        grid_spec=grid_spec,
        out_shape=[jax.ShapeDtypeStruct((bd, nh, hd), F32), jax.ShapeDtypeStruct((bd, nselp, LANES), jnp.int32)],
        compiler_params=_cparams(("arbitrary",), VMEM_LIMIT), name="nsa_sample_cmp",
    )(page_table.reshape(-1), q8, pages, pe2, wbd, biasc)
    sel_idx = jnp.transpose(idx[:, :n_sel, :NSA_KV_HEADS], (0, 2, 1))
    return oc, sel_idx


def _nsa_sample_att_kernel(pt_ref, si_ref, q_ref, g_ref, oc_ref, kvn_ref, wn_ref, wc_ref, pages_ref,
                           bsel_ref, bwin_ref, ob_ref, win_ref, selbuf, wall, sem,
                           *, n_pages, n_sel, past, wb):
    b = pl.program_id(0)
    nb = pl.num_programs(0)
    kvd = NSA_KV_HEADS * NSA_HD
    hd = NSA_HD
    n_blk_pages = past // SEL_BLOCK
    per_page = PAGE_SIZE // SEL_BLOCK
    n_slots = NSA_KV_HEADS * n_sel

    def blk_of(bb, j):
        return si_ref[bb * n_slots + j]

    def blk_copy(bb, slot, j):
        blk = jnp.clip(blk_of(bb, j), 0, n_blk_pages - 1)
        page = pt_ref[bb * n_pages + blk // per_page]
        r0 = pl.multiple_of((blk % per_page) * SEL_BLOCK, SEL_BLOCK)
        return pltpu.make_async_copy(pages_ref.at[page, pl.ds(r0, SEL_BLOCK), pl.ds(2 * kvd, 2 * kvd)],
                                     selbuf.at[slot, j], sem.at[slot])

    def in_pages(bb, j):
        blk = blk_of(bb, j)
        return (blk >= 0) & (blk < n_blk_pages)

    def start_all(bb, slot):
        def body(j, c):
            @pl.when(in_pages(bb, j))
            def _():
                blk_copy(bb, slot, j).start()
            return c
        lax.fori_loop(0, n_slots, body, 0)

    @pl.when(b == 0)
    def _():
        start_all(0, 0)

    @pl.when(b + 1 < nb)
    def _():
        start_all(b + 1, (b + 1) % 2)

    slot = b % 2
    new_sel = kvn_ref[0][:, 2 * kvd:4 * kvd]
    sub = lax.broadcasted_iota(jnp.int32, (SEL_BLOCK, 1), 0)

    def wait_body(j, c):
        @pl.when(in_pages(b, j))
        def _():
            blk_copy(b, slot, j).wait()

        @pl.when(jnp.logical_not(in_pages(b, j)))
        def _():
            is_new = blk_of(b, j) == n_blk_pages
            selbuf[slot, j] = jnp.where((sub == 0) & is_new, new_sel, 0.0)
        return c
    lax.fori_loop(0, n_slots, wait_body, 0)

    q = q_ref[0]
    nh = q.shape[0]
    gates = g_ref[0]
    o_c = oc_ref[0]

    def q_rows(keep):
        qp = _head_pad(q, keep)
        return jnp.concatenate([qp, jnp.zeros((LANES - nh, qp.shape[1]), qp.dtype)], axis=0)

    def attend(qrows, keys_b, vals, bias, valid):
        s = _dot_nt(keys_b, qrows) + bias
        s = jnp.where(valid, s, NEG)
        e = jnp.where(valid, jnp.exp(s - jnp.max(s, 0, keepdims=True)), 0.0)
        den = jnp.sum(e, 0, keepdims=True)
        p = e / jnp.where(den > 0, den, 1.0)
        return [jnp.sum(p[:, h:h + 1] * vals, axis=0, keepdims=True) for h in range(nh)]

    o_s = [None] * nh
    for k in range(NSA_KV_HEADS):
        qrows = q_rows(lambda r: r // NSA_GROUP == k)
        blks = [blk_of(b, k * n_sel + r) for r in range(n_sel)]
        keys = selbuf[slot, k * n_sel:(k + 1) * n_sel].reshape(n_sel * SEL_BLOCK, 2 * kvd)
        bias = jnp.concatenate([bsel_ref[jnp.maximum(bl, 0)] for bl in blks], axis=0)
        valid = jnp.concatenate([(bl * SEL_BLOCK + sub <= past) & (bl >= 0) for bl in blks], axis=0)
        rows = attend(qrows, keys[:, 0:kvd].astype(BF16), keys[:, kvd:2 * kvd], bias, valid)
        for h in range(k * NSA_GROUP, (k + 1) * NSA_GROUP):
            o_s[h] = rows[h][:, k * hd:(k + 1) * hd]
    wbp = wall.shape[0]
    wall[0:wb, :] = wc_ref[0]
    wall[wb:wb + 1, :] = wn_ref[0]
    wall[wb + 1:wbp, :] = jnp.zeros((wbp - wb - 1, 2 * kvd), F32)
    win_ref[0] = wall[1:wb + 1, :]
    w_pos = lax.broadcasted_iota(jnp.int32, (wbp, 1), 0)
    valid_w = (w_pos <= wb) & (wb - w_pos < WINDOW) & (past - wb + w_pos >= 0)
    qrows = q_rows(lambda r: r >= 0)
    rows = attend(qrows, wall[:, 0:kvd].astype(BF16), wall[:, kvd:2 * kvd], bwin_ref[...], valid_w)
    for h in range(nh):
        k = h // NSA_GROUP
        o_w = rows[h][:, k * hd:(k + 1) * hd]
        mix = gates[h:h + 1, 0:1] * o_c[h:h + 1, :] + gates[h:h + 1, 1:2] * o_s[h] + gates[h:h + 1, 2:3] * o_w
        ob_ref[0, h:h + 1, :] = mix.astype(ob_ref.dtype)


def nsa_sample_att(q8, gates, oc, kv03, kv45, wcache, pages, page_table, sel_idx, rel_bias):
    bd, nh, hd = q8.shape
    n_pages = page_table.shape[1]
    past = n_pages * PAGE_SIZE
    wb = wcache.shape[1]
    kvd = NSA_KV_HEADS * hd
    n_sel = sel_idx.shape[2]
    nsb = -(-(past + 1) // SEL_BLOCK)
    wbp = _round_up(wb + 1, SUBLANES)
    g3 = gates[:, :3 * nh].reshape(bd, nh, 3)
    dist = past - (jnp.arange(nsb)[:, None] * SEL_BLOCK + jnp.arange(SEL_BLOCK)[None, :])
    bsel = jnp.zeros((nsb, SEL_BLOCK, LANES), F32).at[:, :, :nh].set(_bias_of(rel_bias, dist))
    bwin = jnp.zeros((wbp, LANES), F32).at[:, :nh].set(_bias_of(rel_bias, wb - jnp.arange(wbp)))
    blk = lambda n2, n3: pl.BlockSpec((1, n2, n3), lambda bi, pt, si: (bi, 0, 0))
    grid_spec = pltpu.PrefetchScalarGridSpec(
        num_scalar_prefetch=2, grid=(bd,),
        in_specs=[blk(nh, hd), blk(nh, 3), blk(nh, hd), blk(1, 4 * kvd), blk(1, 2 * kvd), blk(wb, 2 * kvd),
                  pl.BlockSpec(memory_space=pl.ANY),
                  pl.BlockSpec(bsel.shape, lambda bi, pt, si: (0, 0, 0)),
                  pl.BlockSpec(bwin.shape, lambda bi, pt, si: (0, 0))],
        out_specs=[blk(nh, hd), blk(wb, 2 * kvd)],
        scratch_shapes=[pltpu.VMEM((2, NSA_KV_HEADS * n_sel, SEL_BLOCK, 2 * kvd), F32),
                        pltpu.VMEM((wbp, 2 * kvd), F32), pltpu.SemaphoreType.DMA((2,))])
    ob, win = pl.pallas_call(
        functools.partial(_nsa_sample_att_kernel, n_pages=n_pages, n_sel=n_sel, past=past, wb=wb),
        grid_spec=grid_spec,
        out_shape=[jax.ShapeDtypeStruct((bd, nh, hd), BF16), jax.ShapeDtypeStruct((bd, wb, 2 * kvd), F32)],
        compiler_params=_cparams(("arbitrary",), VMEM_LIMIT), name="nsa_sample_att",
    )(page_table.reshape(-1), sel_idx.reshape(-1), q8, g3, oc, kv03.reshape(bd, 1, -1), kv45.reshape(bd, 1, -1),
      wcache, pages, bsel, bwin)
    return ob.reshape(bd, nh * hd), win


def kernel(x_prompt, x_sample, mem_prompt, cache_conv, cache_nsa_pages, cache_nsa_window, state_mlstm_c,
           state_mlstm_n, state_mlstm_m, cache_mem_kv, page_table, rel_bias, norm_mix, norm_xattn, norm_mem,
           norm_ffn, norm_final, w_in_even, w_out_even, conv_w, conv_b, conv_ln_g, conv_ln_b, nsa_cmp_pe,
           nsa_cmp_w, w_in_odd, mlstm_b_i, mlstm_b_f, mlstm_norm, w_out_odd, xattn_wq, xattn_wkv, xattn_wo,
           ffn_w_gu, ffn_w_dn, router_w, router_b, expert_w_gu, expert_w_dn):
    b, s, d = x_prompt.shape
    bd, td, _ = x_sample.shape
    assert td == 1, "the sample group decodes one token per sequence"
    depth = norm_mix.shape[0]
    mt = mem_prompt.shape[1]
    cc = conv_w.shape[2]
    hist = conv_w.shape[1] - 1
    wb = cache_nsa_window.shape[2]
    kvh, hd = NSA_KV_HEADS, NSA_HD
    n_pool = cache_nsa_pages.shape[1]
    assert s >= hist and s >= wb and s % Q_BLOCK == 0
    xp = x_prompt.reshape(b * s, d)
    xs = x_sample.reshape(bd, d)
    mem = mem_prompt.reshape(b * mt, d)
    pages_all = cache_nsa_pages.reshape(-1, PAGE_SIZE, 4 * kvh * hd)
    memkv_all = cache_mem_kv.reshape(depth * bd, mt, -1)
    bf = lambda a: a.astype(BF16)
    conv_p, conv_s, nsa_p, nsa_s, win_p, win_s = [], [], [], [], [], []
    mc_p, mc_s, mn_p, mn_s, mm_p, mm_s, memkv_p = [], [], [], [], [], [], []
    for l in range(depth):
        li = l // 2
        if l % 2 == 0:
            prm = prep_even(w_in_even[li], nsa_cmp_pe[li], nsa_cmp_w[li])
            w_out = bf(w_out_even[li])
            w_parts = [w_out[:cc], w_out[cc:]]
            conv_args = (conv_w[li], conv_b[li], conv_ln_g[li], conv_ln_b[li])
            glu, kv03, kv45, keys, qt, vt, gt = inproj_even(xp, norm_mix[l], prm['w_in'], cc, True)
            a_out = conv_prompt(glu, *conv_args, b, s)
            kc, vct = compress_prompt(kv03, prm['pe2'], prm['wbd'], b, s, _round_up(s // CMP_BLOCK, LANES))
            b_out = nsa_prompt(qt, gt, kc, vct, keys, vt, rel_bias, b, s)
            xp = outproj(xp, [a_out, b_out], w_parts)
            conv_p.append(glu.reshape(b, s, cc)[:, s - hist:])
            nsa_p.append(kv03.reshape(b, s, 4, kvh, hd))
            win_p.append(kv45.reshape(b, s, 2, kvh, hd)[:, s - wb:])
            glu, kv03, kv45, q, gates = inproj_even(xs, norm_mix[l], prm['w_in'], cc, False)
            a_out, conv_state = conv_sample(cache_conv[li], glu, *conv_args)
            q8 = q.reshape(bd, NSA_HEADS, hd)
            pt = page_table + li * n_pool
            o_c, sel_idx = nsa_sample_cmp(q8, pages_all, pt, prm['pe2'], prm['wbd'], rel_bias)
            b_out, win = nsa_sample_att(q8, gates, o_c, kv03, kv45, cache_nsa_window[li].reshape(bd, wb, -1),
                                        pages_all, pt, sel_idx, rel_bias)
            xs = outproj(xs, [a_out, b_out], w_parts)
            conv_s.append(conv_state)
            nsa_s.append(kv03.reshape(bd, 1, 4, kvh, hd))
            win_s.append(win.reshape(bd, wb, 2, kvh, hd))
        else:
            prm = prep_odd(w_in_odd[li], mlstm_b_i[li], mlstm_b_f[li])
            w_out = bf(w_out_odd[li])
            q, k, v, og, gc, gr = inproj_odd(xp, norm_mix[l], prm['w_in'], prm['bias'])
            hn, c_new, n_new, m_new = mlstm_prompt(q, k, v, og, gc, gr, mlstm_norm[li], b, s)
            xp = outproj(xp, [hn], [w_out])
            mc_p.append(c_new)
            mn_p.append(n_new)
            mm_p.append(m_new[:, :, 0])
            q, k, v, og, gc, gr = inproj_odd(xs, norm_mix[l], prm['w_in'], prm['bias'])
            hn, c_new, n_new, m_new = mlstm_sample(q, k, v, og, gc, mlstm_norm[li], state_mlstm_c[li],
                                                   state_mlstm_n[li], state_mlstm_m[li])
            xs = outproj(xs, [hn], [w_out])
            mc_s.append(c_new)
            mn_s.append(n_new)
            mm_s.append(m_new)
        wq, wo = bf(xattn_wq[l]), bf(xattn_wo[l])
        mkv, mkv_b = memkv(mem, norm_mem[l], bf(xattn_wkv[l]))
        memkv_p.append(mkv.reshape(b, mt, 2, X_HEADS, d // X_HEADS))
        xp = xattn_prompt(xp, norm_xattn[l], wq, mkv_b, wo, b, s)
        xs = xattn_sample(xs, norm_xattn[l], wq, memkv_all, wo, l)
        if l % 2 == 0:
            w_gu, w_dn = bf(ffn_w_gu[li]), bf(ffn_w_dn[li])
            xp = ffn(xp, norm_ffn[l], w_gu, w_dn)
            xs = ffn(xs, norm_ffn[l], w_gu, w_dn)
        else:
            e_gu, e_dn = bf(expert_w_gu[li]), bf(expert_w_dn[li])
            comb, h, mask, counts = router(xp, norm_ffn[l], router_w[li], router_b[li])
            xp = moe_grouped(xp, h, comb, mask, counts, e_gu, e_dn)
            comb, h, _, _ = router(xs, norm_ffn[l], router_w[li], router_b[li])
            xs = moe(xs, h, comb, e_gu, e_dn)
    y_prompt = rmsnorm(xp, norm_final).reshape(b, s, d)
    y_sample = rmsnorm(xs, norm_final).reshape(bd, 1, d)
    return (y_prompt, y_sample, jnp.stack(conv_p), jnp.stack(conv_s), jnp.stack(nsa_p), jnp.stack(nsa_s),
            jnp.stack(win_p), jnp.stack(win_s), jnp.stack(mc_p), jnp.stack(mc_s), jnp.stack(mn_p),
            jnp.stack(mn_s), jnp.stack(mm_p), jnp.stack(mm_s), jnp.stack(memkv_p))
```

```python
import functools
import math

import jax
import jax.numpy as jnp
import numpy as np
from jax import lax
from jax.experimental import pallas as pl
from jax.experimental.pallas import tpu as pltpu

F32 = jnp.float32
BF16 = jnp.bfloat16

PAGE_SIZE = 128
CONV_WIDTH = 31
NSA_HEADS = 8
NSA_KV_HEADS = 2
NSA_GROUP = NSA_HEADS // NSA_KV_HEADS
NSA_HD = 64
CMP_BLOCK = 32
SEL_BLOCK = 64
SEL_RATIO = SEL_BLOCK // CMP_BLOCK
TOP_N = 16
WINDOW = 512
Q_BLOCK = 128
FORCE_SCORE = 1.0e4
NUM_BUCKETS = 32
MAX_DISTANCE = 1024
MLSTM_HEADS = 4
X_HEADS = 4
N_EXPERTS = 8
TOP_K = 2
RMS_EPS = 1e-6
LN_EPS = 1e-5
NEG = -1e30

LANES = 128
SUBLANES = 8
VMEM_LIMIT = 56 * 1024 * 1024
MLSTM_CHUNK = 256


def _cparams(sem, vmem=None):
    return pltpu.CompilerParams(dimension_semantics=sem, vmem_limit_bytes=vmem)


def _rms(x, g):
    return x * lax.rsqrt(jnp.mean(x * x, -1, keepdims=True) + RMS_EPS) * g


def _dot(a, b):
    return jnp.dot(a, b, preferred_element_type=F32)


def _dot_nt(a, b):
    return lax.dot_general(a, b, (((1,), (1,)), ((), ())), preferred_element_type=F32)


def _dot_tn(a, b):
    return lax.dot_general(a, b, (((0,), (0,)), ((), ())), preferred_element_type=F32)


def _full(shape):
    n = len(shape)
    return pl.BlockSpec(shape, lambda *_: (0,) * n)


def _row_tile(m, pref):
    t = min(pref, m)
    while m % t:
        t //= 2
    return t


def _rmsnorm_kernel(x_ref, g_ref, o_ref):
    o_ref[...] = _rms(x_ref[...], g_ref[...])


def rmsnorm(x, g):
    m, d = x.shape
    tm = _row_tile(m, 1024)
    return pl.pallas_call(
        _rmsnorm_kernel, grid=(m // tm,),
        in_specs=[pl.BlockSpec((tm, d), lambda i: (i, 0)), _full((1, d))],
        out_specs=pl.BlockSpec((tm, d), lambda i: (i, 0)),
        out_shape=jax.ShapeDtypeStruct((m, d), F32),
        compiler_params=_cparams(("parallel",)), name="rmsnorm",
    )(x, g.reshape(1, d))


def _outproj_kernel(*refs, n_in):
    x_ref = refs[0]
    a_refs = refs[1:1 + n_in]
    w_refs = refs[1 + n_in:1 + 2 * n_in]
    o_ref = refs[1 + 2 * n_in]
    acc = x_ref[...]
    for a_ref, w_ref in zip(a_refs, w_refs):
        acc = acc + _dot(a_ref[...], w_ref[...])
    o_ref[...] = acc


def outproj(x, acts, ws):
    m, d = x.shape
    tm = _row_tile(m, 512)
    n_in = len(acts)
    in_specs = [pl.BlockSpec((tm, d), lambda i: (i, 0))]
    in_specs += [pl.BlockSpec((tm, a.shape[1]), lambda i: (i, 0)) for a in acts]
    in_specs += [_full(w.shape) for w in ws]
    return pl.pallas_call(
        functools.partial(_outproj_kernel, n_in=n_in), grid=(m // tm,),
        in_specs=in_specs, out_specs=pl.BlockSpec((tm, d), lambda i: (i, 0)),
        out_shape=jax.ShapeDtypeStruct((m, d), F32),
        compiler_params=_cparams(("parallel",)), name="outproj",
    )(x, *acts, *ws)


def _inproj_even_kernel(x_ref, g_ref, w_ref, glu_ref, kv03_ref, kv45_ref, *rest, cc, qd, kvd, tiles):
    xn = _rms(x_ref[...], g_ref[...]).astype(BF16)

    def mm(lo, hi):
        return _dot(xn, w_ref[:, lo:hi])

    o = 0
    a = mm(o, o + cc)
    b = mm(o + cc, o + 2 * cc)
    glu_ref[...] = a * jax.nn.sigmoid(b)
    o += 2 * cc
    q = mm(o, o + qd) * (NSA_HD ** -0.5)
    o += qd
    kv03 = mm(o, o + 4 * kvd)
    kv03_ref[...] = kv03
    o += 4 * kvd
    kv45 = mm(o, o + 2 * kvd)
    kv45_ref[...] = kv45
    o += 2 * kvd
    gates = jax.nn.sigmoid(mm(o, o + LANES))
    if tiles == 0:
        q_ref, gate_ref = rest
        q_ref[...] = q.astype(BF16)
        gate_ref[...] = gates
        return
    keys_ref, qt_ref, vt_ref, gt_ref = rest
    keys_ref[...] = jnp.concatenate([kv03[:, 2 * kvd:3 * kvd], kv45[:, 0:kvd]], axis=1).astype(BF16)
    vals = jnp.concatenate([kv03[:, 3 * kvd:4 * kvd], kv45[:, kvd:2 * kvd]], axis=1)
    for j in range(tiles):
        rows = slice(j * Q_BLOCK, (j + 1) * Q_BLOCK)
        qt_ref[j] = q[rows, :].T.astype(BF16)
        vt_ref[j] = vals[rows, :].T.astype(BF16)
        gt_ref[j] = gates[rows, :].T


def inproj_even(x, g, w_pad, cc, transposed):
    m, d = x.shape
    qd = NSA_HEADS * NSA_HD
    kvd = NSA_KV_HEADS * NSA_HD
    tm = _row_tile(m, 256)
    row = lambda n: pl.BlockSpec((tm, n), lambda i: (i, 0))
    out_specs = [row(cc), row(4 * kvd), row(2 * kvd)]
    out_shape = [jax.ShapeDtypeStruct((m, n), F32) for n in (cc, 4 * kvd, 2 * kvd)]
    tiles = tm // Q_BLOCK if transposed else 0
    if transposed:
        assert tm % Q_BLOCK == 0
        tile = lambda n: pl.BlockSpec((tiles, n, Q_BLOCK), lambda i: (i, 0, 0))
        out_specs += [row(2 * kvd), tile(qd), tile(2 * kvd), tile(LANES)]
        out_shape += [jax.ShapeDtypeStruct((m, 2 * kvd), BF16),
                      jax.ShapeDtypeStruct((m // Q_BLOCK, qd, Q_BLOCK), BF16),
                      jax.ShapeDtypeStruct((m // Q_BLOCK, 2 * kvd, Q_BLOCK), BF16),
                      jax.ShapeDtypeStruct((m // Q_BLOCK, LANES, Q_BLOCK), F32)]
    else:
        out_specs += [row(qd), row(LANES)]
        out_shape += [jax.ShapeDtypeStruct((m, qd), BF16), jax.ShapeDtypeStruct((m, LANES), F32)]
    return pl.pallas_call(
        functools.partial(_inproj_even_kernel, cc=cc, qd=qd, kvd=kvd, tiles=tiles), grid=(m // tm,),
        in_specs=[row(d), _full((1, d)), _full(w_pad.shape)],
        out_specs=out_specs, out_shape=out_shape,
        compiler_params=_cparams(("parallel",)), name="inproj_even",
    )(x, g.reshape(1, d), w_pad)


def _conv_post(y, lg, lb):
    mu = jnp.mean(y, -1, keepdims=True)
    var = jnp.mean(jnp.square(y - mu), -1, keepdims=True)
    yn = (y - mu) * lax.rsqrt(var + LN_EPS) * lg + lb
    return yn * jax.nn.sigmoid(yn)


CONV_SUB = 64
CONV_PAD = 32


def _conv_prompt_kernel(glu_ref, cw_ref, cb_ref, lg_ref, lb_ref, o_ref, ext_ref, y_ref, *, ts, s):
    i = pl.program_id(1)
    c = glu_ref.shape[-1]

    @pl.when(i == 0)
    def _():
        ext_ref[0:CONV_PAD, :] = jnp.zeros((CONV_PAD, c), F32)
        ext_ref[CONV_PAD:CONV_PAD + s, :] = glu_ref[0]
        ext_ref[CONV_PAD + s:CONV_PAD + s + SUBLANES, :] = jnp.zeros((SUBLANES, c), F32)

    lead = CONV_PAD - (CONV_WIDTH - 1)
    span = CONV_SUB + CONV_PAD

    def sub(j, carry):
        r0 = pl.multiple_of(i * ts + j * CONV_SUB, CONV_SUB)
        for c0 in range(0, c, LANES):
            xw = ext_ref[pl.ds(r0, span + SUBLANES), c0:c0 + LANES]
            acc = jnp.zeros((CONV_SUB, LANES), F32) + cb_ref[:, c0:c0 + LANES]
            for r in range(SUBLANES):
                xr = xw[r:r + span, :]
                for a in range(span // SUBLANES):
                    w = SUBLANES * a + r - lead
                    if 0 <= w < CONV_WIDTH:
                        acc = acc + xr[SUBLANES * a:SUBLANES * a + CONV_SUB, :] * cw_ref[w:w + 1, c0:c0 + LANES]
            y_ref[:, c0:c0 + LANES] = acc
        o_ref[0, pl.ds(pl.multiple_of(j * CONV_SUB, CONV_SUB), CONV_SUB), :] = _conv_post(
            y_ref[...], lg_ref[...], lb_ref[...]).astype(o_ref.dtype)
        return carry

    lax.fori_loop(0, ts // CONV_SUB, sub, 0)


def conv_prompt(glu, cw, cb, lg, lb, b, s):
    c = glu.shape[-1]
    ts = _row_tile(s, 256)
    vec = lambda a: a.reshape(1, c)
    out = pl.pallas_call(
        functools.partial(_conv_prompt_kernel, ts=ts, s=s), grid=(b, s // ts),
        in_specs=[pl.BlockSpec((1, s, c), lambda bi, i: (bi, 0, 0)), _full((CONV_WIDTH, c)),
                  _full((1, c)), _full((1, c)), _full((1, c))],
        out_specs=pl.BlockSpec((1, ts, c), lambda bi, i: (bi, i, 0)),
        out_shape=jax.ShapeDtypeStruct((b, s, c), BF16),
        scratch_shapes=[pltpu.VMEM((CONV_PAD + s + SUBLANES, c), F32), pltpu.VMEM((CONV_SUB, c), F32)],
        compiler_params=_cparams(("parallel", "arbitrary")), name="conv_prompt",
    )(glu.reshape(b, s, c), cw, vec(cb), vec(lg), vec(lb))
    return out.reshape(b * s, c)


def _conv_sample_kernel(cache_ref, glu_ref, cw_ref, cb_ref, lg_ref, lb_ref, o_ref, st_ref):
    hist = CONV_WIDTH - 1
    cache = cache_ref[...]
    glu = glu_ref[...]
    y = jnp.sum(cache * cw_ref[0:hist, :][None], axis=1) + glu * cw_ref[hist:hist + 1, :] + cb_ref[...]
    o_ref[...] = _conv_post(y, lg_ref[...], lb_ref[...]).astype(o_ref.dtype)
    st_ref[:, 0:hist - 1, :] = cache[:, 1:hist, :]
    st_ref[:, hist - 1:hist, :] = glu[:, None, :]


def conv_sample(cache, glu, cw, cb, lg, lb):
    bd, hist, c = cache.shape
    vec = lambda a: a.reshape(1, c)
    return pl.pallas_call(
        _conv_sample_kernel,
        out_shape=[jax.ShapeDtypeStruct((bd, c), BF16), jax.ShapeDtypeStruct((bd, hist, c), F32)],
        name="conv_sample",
    )(cache, glu, cw, vec(cb), vec(lg), vec(lb))


def _rel_bucket(dist):
    n = jnp.maximum(dist, 0)
    max_exact = NUM_BUCKETS // 2
    nf = jnp.maximum(n, 1).astype(F32)
    large = max_exact + (jnp.log(nf / max_exact) / math.log(MAX_DISTANCE / max_exact)
                         * (NUM_BUCKETS - max_exact)).astype(jnp.int32)
    large = jnp.minimum(large, NUM_BUCKETS - 1)
    return jnp.where(n < max_exact, n, large)


def _bias_of(rel_bias, dist):
    bucket = _rel_bucket(dist)[..., None]
    out = jnp.zeros(bucket.shape[:-1] + (rel_bias.shape[1],), F32)
    for k in range(NUM_BUCKETS):
        out = jnp.where(bucket == k, rel_bias[k].astype(F32), out)
    return out


def _compress_accumulate(load_rows, pe_ref, w_ref, nc):
    accs = []
    for slot in range(2):
        acc = jnp.zeros((nc, w_ref.shape[-1]), F32)
        for j in range(CMP_BLOCK):
            xj = load_rows(slot, j) + pe_ref[slot, j:j + 1, :]
            acc = acc + _dot(xj.astype(BF16), w_ref[slot, j])
        accs.append(acc)
    return jnp.concatenate(accs, axis=1)


def _compress_prompt_kernel(x_ref, pe_ref, w_ref, kc_ref, vct_ref, *, nc, ncp, n_slots):
    acc = _compress_accumulate(
        lambda sl, j: x_ref[0, pl.ds(j * n_slots + sl, nc, stride=CMP_BLOCK * n_slots), :], pe_ref, w_ref, nc)
    kvd = acc.shape[1] // 2
    if ncp > nc:
        acc = jnp.concatenate([acc, jnp.zeros((ncp - nc, 2 * kvd), F32)], axis=0)
    kc_ref[0] = acc[:, 0:kvd].astype(BF16)
    vct_ref[0] = acc[:, kvd:2 * kvd].T.astype(BF16)


def compress_prompt(kv03, pe2, wbd, b, s, ncp):
    nc = s // CMP_BLOCK
    kvd = wbd.shape[-1]
    n_slots = kv03.shape[1] // kvd
    return pl.pallas_call(
        functools.partial(_compress_prompt_kernel, nc=nc, ncp=ncp, n_slots=n_slots), grid=(b,),
        in_specs=[pl.BlockSpec((1, s * n_slots, kvd), lambda bi: (bi, 0, 0)), _full(pe2.shape), _full(wbd.shape)],
        out_specs=[pl.BlockSpec((1, ncp, kvd), lambda bi: (bi, 0, 0)), pl.BlockSpec((1, kvd, ncp), lambda bi: (bi, 0, 0))],
        out_shape=[jax.ShapeDtypeStruct((b, ncp, kvd), BF16), jax.ShapeDtypeStruct((b, kvd, ncp), BF16)],
        compiler_params=_cparams(("parallel",)), name="compress_prompt",
    )(kv03.reshape(b, s * n_slots, kvd), pe2, wbd)


def _select_blocks(score, n_sel, n_cand):
    lane = lax.broadcasted_iota(jnp.int32, score.shape, 1)
    rank = jnp.zeros(score.shape, F32)
    for i in range(n_cand):
        col = score[:, 2 * i:2 * i + 1]
        beats = (col > score) | ((col == score) & (lane > 2 * i))
        rank = rank + beats.astype(F32)
    is_cand = ((lane % 2) == 0) & (lane < 2 * n_cand)
    return (is_cand & (rank < n_sel) & (score >= 0)).astype(F32), rank


def _pair_sum(imp):
    n = imp.shape[1]
    return imp + pltpu.roll(imp, n - 1, 1)


def _rank_rows(score, n_sel, n_cand):
    blk = lax.broadcasted_iota(jnp.int32, score.shape, 0)
    rank = jnp.zeros(score.shape, F32)
    for i in range(n_cand):
        row = score[i:i + 1, :]
        beats = (row > score) | ((row == score) & (blk > i))
        rank = rank + beats.astype(F32)
    return ((rank < n_sel) & (score >= 0)).astype(F32)


def _nsa_prompt_kernel(qt_ref, gt_ref, kc_ref, vct_ref, keys_ref, vt_ref, biasc_ref, btile_ref, o_ref,
                       qt_scr, oc_scr, acc_s, acc_w, imp_scr, sel_scr, out_scr, *, nc, nsb, n_sel):
    qi = pl.program_id(1)
    g, hd, kvh, qb = NSA_GROUP, NSA_HD, NSA_KV_HEADS, Q_BLOCK
    kvd = kvh * hd
    ncp = kc_ref.shape[1]
    nsbp = sel_scr.shape[1]
    q_pos = qi * qb + lax.broadcasted_iota(jnp.int32, (1, qb), 1)
    key_row = lax.broadcasted_iota(jnp.int32, (qb, 1), 0)
    c_row = lax.broadcasted_iota(jnp.int32, (ncp, 1), 0)
    mask_c = (q_pos >= c_row * CMP_BLOCK + (CMP_BLOCK - 1)) & (c_row < nc)
    blk = lax.broadcasted_iota(jnp.int32, (nsbp, 1), 0)
    cur = q_pos // SEL_BLOCK
    forced = (blk == 0) | (blk == cur) | (blk == cur - 1)
    zeros = jnp.zeros((hd, qb), BF16)
    for k in range(kvh):
        for gi in range(g):
            h = k * g + gi
            parts = [zeros] * kvh
            parts[k] = qt_ref[0, h * hd:(h + 1) * hd, :]
            qt_scr[k, :, gi * qb:(gi + 1) * qb] = jnp.concatenate(parts, axis=0)
        s_c = _dot(kc_ref[0], qt_scr[k])
        imp = jnp.zeros((ncp, qb), F32)
        probs = []
        for gi in range(g):
            s = jnp.where(mask_c, s_c[:, gi * qb:(gi + 1) * qb] + biasc_ref[k * g + gi], NEG)
            e = jnp.where(mask_c, jnp.exp(s - jnp.max(s, 0, keepdims=True)), 0.0)
            den = jnp.sum(e, 0, keepdims=True)
            p = e / jnp.where(den > 0, den, 1.0)
            imp = imp + p
            probs.append(p.astype(BF16))
        oc_scr[k] = _dot(vct_ref[0, k * hd:(k + 1) * hd, :], jnp.concatenate(probs, axis=1))
        imp_scr[...] = imp + pltpu.roll(imp, ncp - 1, 0)
        cand = imp_scr[pl.ds(0, nsbp, stride=SEL_RATIO), :]
        score = jnp.where(forced, FORCE_SCORE, jnp.where(blk <= cur, cand, -1.0))
        sel_scr[k] = _rank_rows(jnp.where(blk < nsb, score, -2.0), n_sel, nsb)

    per_tile = qb // SEL_BLOCK
    n_tiles = keys_ref.shape[1] // qb
    first = ([jnp.full((1, qb), NEG, F32)] * g, [jnp.zeros((1, qb), F32)] * g)

    def tile_step(tiles, carry, key_col, val_row, acc_ref, window):
        kts = [jnp.clip(kt, 0, n_tiles - 1) for kt, _ in tiles]
        starts = [pl.multiple_of(kt * qb, qb) for kt in kts]
        k_t = jnp.concatenate([keys_ref[0, pl.ds(r0, qb), key_col:key_col + kvd] for r0 in starts], axis=0)
        dist = jnp.concatenate([jnp.where(active, q_pos - (r0 + key_row), -1)
                                for r0, (_, active) in zip(starts, tiles)], axis=0)
        in_range = dist >= 0
        scores = [_dot(k_t, qt_scr[k]) for k in range(kvh)]
        new, updates = [], []
        for k in range(kvh):
            if window:
                valid = in_range & (dist < WINDOW)
            else:
                pieces = []
                for kt in kts:
                    chosen = jnp.zeros((qb, qb), F32)
                    for j in range(per_tile):
                        row = sel_scr[k, pl.ds(per_tile * kt + j, 1), :]
                        chosen = jnp.where(key_row // SEL_BLOCK == j, row, chosen)
                    pieces.append(chosen)
                valid = in_range & (jnp.concatenate(pieces, axis=0) > 0.5)
            ms, ls = carry[k]
            ms2, ls2, alphas, probs = [], [], [], []
            for gi in range(g):
                bias = jnp.concatenate([btile_ref[jnp.maximum(qi - kt, 0), k * g + gi] for kt in kts], axis=0)
                s = jnp.where(valid, scores[k][:, gi * qb:(gi + 1) * qb] + bias, NEG)
                m_new = jnp.maximum(ms[gi], jnp.max(s, 0, keepdims=True))
                alpha = jnp.exp(ms[gi] - m_new)
                p = jnp.exp(s - jnp.where(m_new == NEG, 0.0, m_new))
                ms2.append(m_new)
                ls2.append(alpha * ls[gi] + jnp.sum(p, 0, keepdims=True))
                alphas.append(alpha)
                probs.append(p.astype(BF16))
            new.append((ms2, ls2))
            updates.append((jnp.concatenate(alphas, axis=1), jnp.concatenate(probs, axis=1)))
        for k, (alpha, prob) in enumerate(updates):
            v_t = jnp.concatenate([vt_ref[kt, val_row + k * hd:val_row + (k + 1) * hd, :] for kt in kts], axis=1)
            acc_ref[k] = acc_ref[k] * alpha + _dot(v_t, prob)
        return tuple(new)

    acc_s[...] = jnp.zeros_like(acc_s)
    acc_w[...] = jnp.zeros_like(acc_w)
    sel_args = dict(key_col=0, val_row=0, acc_ref=acc_s, window=False)
    win_args = dict(key_col=kvd, val_row=kvd, acc_ref=acc_w, window=True)

    def sel_pair(i, carry):
        return tile_step([(2 * i, 2 * i <= qi), (2 * i + 1, 2 * i + 1 <= qi)], carry, **sel_args)

    stat_s = lax.fori_loop(0, qi // 2 + 1, sel_pair, (first,) * kvh)
    stat_w = (first,) * kvh
    win_tiles = [(qi - j, qi - j >= 0) for j in range(WINDOW // qb, -1, -1)]
    for j in range(0, len(win_tiles), 2):
        stat_w = tile_step(win_tiles[j:j + 2], stat_w, **win_args)
    gt = gt_ref[0]
    for k in range(kvh):
        for gi in range(g):
            h = k * g + gi
            cols = slice(gi * qb, (gi + 1) * qb)
            l_s, l_w = stat_s[k][1][gi], stat_w[k][1][gi]
            o_s = acc_s[k, :, cols] / jnp.where(l_s > 0, l_s, 1.0)
            o_w = acc_w[k, :, cols] / jnp.where(l_w > 0, l_w, 1.0)
            out_scr[h * hd:(h + 1) * hd, :] = (gt[3 * h:3 * h + 1, :] * oc_scr[k, :, cols]
                                               + gt[3 * h + 1:3 * h + 2, :] * o_s + gt[3 * h + 2:3 * h + 3, :] * o_w)
    o_ref[0] = out_scr[...].T.astype(o_ref.dtype)


def nsa_prompt(qt, gt, kc, vct, keys, vt, rel_bias, b, s):
    qb = Q_BLOCK
    nq = s // qb
    nc = s // CMP_BLOCK
    ncp = kc.shape[1]
    nsb = s // SEL_BLOCK
    nsbp = _round_up(nsb, SUBLANES)
    assert SEL_RATIO * nsbp <= ncp
    n_sel = min(TOP_N, nsb)
    nh = NSA_HEADS
    hq = qt.shape[1]
    kvd = kc.shape[2]
    glanes = NSA_GROUP * qb
    cend = jnp.arange(ncp)[:, None] * CMP_BLOCK + (CMP_BLOCK - 1)
    biasc = jnp.transpose(_bias_of(rel_bias, jnp.arange(s)[None, :] - cend), (2, 0, 1))
    r = jnp.arange(qb)
    dist = jnp.arange(nq)[:, None, None] * qb + r[None, None, :] - r[None, :, None]
    btile = jnp.transpose(_bias_of(rel_bias, dist), (0, 3, 1, 2))
    out = pl.pallas_call(
        functools.partial(_nsa_prompt_kernel, nc=nc, nsb=nsb, n_sel=n_sel), grid=(b, nq),
        in_specs=[pl.BlockSpec((1, hq, qb), lambda bi, i: (bi * nq + i, 0, 0)),
                  pl.BlockSpec((1, LANES, qb), lambda bi, i: (bi * nq + i, 0, 0)),
                  pl.BlockSpec((1, ncp, kvd), lambda bi, i: (bi, 0, 0)),
                  pl.BlockSpec((1, kvd, ncp), lambda bi, i: (bi, 0, 0)),
                  pl.BlockSpec((1, s, 2 * kvd), lambda bi, i: (bi, 0, 0)),
                  pl.BlockSpec((nq, 2 * kvd, qb), lambda bi, i: (bi, 0, 0)),
                  pl.BlockSpec((nh, ncp, qb), lambda bi, i: (0, 0, i)),
                  _full(btile.shape)],
        out_specs=pl.BlockSpec((1, qb, hq), lambda bi, i: (bi, i, 0)),
        out_shape=jax.ShapeDtypeStruct((b, s, hq), BF16),
        scratch_shapes=[pltpu.VMEM((NSA_KV_HEADS, kvd, glanes), BF16), pltpu.VMEM((NSA_KV_HEADS, NSA_HD, glanes), F32),
                        pltpu.VMEM((NSA_KV_HEADS, NSA_HD, glanes), F32), pltpu.VMEM((NSA_KV_HEADS, NSA_HD, glanes), F32),
                        pltpu.VMEM((ncp, qb), F32), pltpu.VMEM((NSA_KV_HEADS, nsbp, qb), F32),
                        pltpu.VMEM((hq, qb), F32)],
        compiler_params=_cparams(("parallel", "arbitrary"), VMEM_LIMIT), name="nsa_prompt",
    )(qt, gt, kc, vct, keys.reshape(b, s, 2 * kvd), vt, biasc, btile)
    return out.reshape(b * s, hq)


def _round_up(x, m):
    return (x + m - 1) // m * m


def prep_even(w_in, pe, wc):
    d, n = w_in.shape
    n_pad = _round_up(n - 3 * NSA_HEADS, LANES) + LANES
    w_pad = jnp.zeros((d, n_pad), BF16).at[:, :n].set(w_in.astype(BF16))
    hd = NSA_HD
    pe2 = jnp.tile(pe, (1, 1, NSA_KV_HEADS))
    wbd = jnp.zeros((2, CMP_BLOCK, NSA_KV_HEADS * hd, NSA_KV_HEADS * hd), F32)
    for i in range(NSA_KV_HEADS):
        wbd = wbd.at[:, :, i * hd:(i + 1) * hd, i * hd:(i + 1) * hd].set(wc)
    return dict(w_in=w_pad, pe2=pe2, wbd=wbd.astype(BF16))


def _memkv_kernel(x_ref, g_ref, w_ref, o_ref, ob_ref):
    y = _dot(_rms(x_ref[...], g_ref[...]).astype(BF16), w_ref[...])
    o_ref[...] = y
    ob_ref[...] = y.astype(BF16)


def memkv(mem, g, w):
    m, d = mem.shape
    n = w.shape[1]
    tm = _row_tile(m, 256)
    return pl.pallas_call(
        _memkv_kernel, grid=(m // tm,),
        in_specs=[pl.BlockSpec((tm, d), lambda i: (i, 0)), _full((1, d)), _full(w.shape)],
        out_specs=[pl.BlockSpec((tm, n), lambda i: (i, 0))] * 2,
        out_shape=[jax.ShapeDtypeStruct((m, n), F32), jax.ShapeDtypeStruct((m, n), BF16)],
        compiler_params=_cparams(("parallel",)), name="memkv",
    )(mem, g.reshape(1, d), w)


def _xattn_core(q, kv, hd):
    nh = q.shape[1] // hd
    outs = []
    for h in range(nh):
        s = _dot_nt(q[:, h * hd:(h + 1) * hd], kv[:, h * hd:(h + 1) * hd])
        e = jnp.exp(s - jnp.max(s, -1, keepdims=True))
        p = e / jnp.sum(e, -1, keepdims=True)
        outs.append(_dot(p.astype(BF16), kv[:, (nh + h) * hd:(nh + h + 1) * hd]))
    return jnp.concatenate(outs, axis=1).astype(BF16)


def _xattn_prompt_kernel(x_ref, g_ref, wq_ref, kv_ref, wo_ref, o_ref, *, hd):
    x = x_ref[0]
    q = (_dot(_rms(x, g_ref[...]).astype(BF16), wq_ref[...]) * (hd ** -0.5)).astype(BF16)
    o = _xattn_core(q, kv_ref[0], hd)
    o_ref[0] = x + _dot(o, wo_ref[...])


def xattn_prompt(x, g, wq, kvb, wo, b, s):
    d = x.shape[1]
    mt = kvb.shape[0] // b
    tm = _row_tile(s, 512)
    out = pl.pallas_call(
        functools.partial(_xattn_prompt_kernel, hd=d // X_HEADS), grid=(b, s // tm),
        in_specs=[pl.BlockSpec((1, tm, d), lambda bi, i: (bi, i, 0)), _full((1, d)), _full(wq.shape),
                  pl.BlockSpec((1, mt, kvb.shape[1]), lambda bi, i: (bi, 0, 0)), _full(wo.shape)],
        out_specs=pl.BlockSpec((1, tm, d), lambda bi, i: (bi, i, 0)),
        out_shape=jax.ShapeDtypeStruct((b, s, d), F32),
        compiler_params=_cparams(("parallel", "parallel"), VMEM_LIMIT), name="xattn_prompt",
    )(x.reshape(b, s, d), g.reshape(1, d), wq, kvb.reshape(b, mt, -1), wo)
    return out.reshape(b * s, d)


def _xattn_sample_kernel(x_ref, g_ref, wq_ref, kv_ref, wo_ref, o_ref, q_scr, a_scr, *, hd):
    bi = pl.program_id(0)
    nb = pl.num_programs(0)
    nh = wq_ref.shape[1] // hd

    @pl.when(bi == 0)
    def _():
        q_scr[...] = _dot(_rms(x_ref[...], g_ref[...]).astype(BF16), wq_ref[...]) * (hd ** -0.5)

    q = jnp.broadcast_to(q_scr[pl.ds(bi, 1), :], (SUBLANES, q_scr.shape[1])).astype(BF16)
    kv = kv_ref[0].astype(BF16)
    a_scr[pl.ds(bi, 1), :] = _xattn_core(q, kv, hd)[0:1, :].astype(F32)

    @pl.when(bi == nb - 1)
    def _():
        o_ref[...] = x_ref[...] + _dot(a_scr[...].astype(BF16), wo_ref[...])


def xattn_sample(x, g, wq, kv_cache, wo, layer):
    bd, d = x.shape
    mt, n = kv_cache.shape[1:]
    return pl.pallas_call(
        functools.partial(_xattn_sample_kernel, hd=d // X_HEADS), grid=(bd,),
        in_specs=[_full((bd, d)), _full((1, d)), _full(wq.shape),
                  pl.BlockSpec((1, mt, n), lambda bi: (layer * bd + bi, 0, 0)), _full(wo.shape)],
        out_specs=_full((bd, d)),
        out_shape=jax.ShapeDtypeStruct((bd, d), F32),
        scratch_shapes=[pltpu.VMEM((bd, wq.shape[1]), F32), pltpu.VMEM((bd, wq.shape[1]), F32)],
        compiler_params=_cparams(("arbitrary",), VMEM_LIMIT), name="xattn_sample",
    )(x, g.reshape(1, d), wq, kv_cache, wo)


def _ffn_kernel(x_ref, g_ref, wg_ref, wu_ref, wd_ref, o_ref, h_scr, acc_scr):
    c = pl.program_id(1)

    @pl.when(c == 0)
    def _():
        h_scr[...] = _rms(x_ref[...], g_ref[...]).astype(BF16)
        acc_scr[...] = x_ref[...]

    h = h_scr[...]
    gate = _dot(h, wg_ref[...])
    up = _dot(h, wu_ref[...])
    act = (gate * jax.nn.sigmoid(gate) * up).astype(BF16)
    acc_scr[...] += _dot(act, wd_ref[...])

    @pl.when(c == pl.num_programs(1) - 1)
    def _():
        o_ref[...] = acc_scr[...]


def _ff_chunk(dff, pref):
    c = dff
    for n in range(1, dff // LANES + 1):
        if dff % n == 0 and (dff // n) % LANES == 0 and dff // n <= pref:
            c = dff // n
            break
    return c


def ffn(x, g, w_gu, w_dn):
    m, d = x.shape
    dff = w_dn.shape[0]
    tm = _row_tile(m, 512)
    fc = _ff_chunk(dff, 1408)
    nch = dff // fc
    return pl.pallas_call(
        _ffn_kernel, grid=(m // tm, nch),
        in_specs=[pl.BlockSpec((tm, d), lambda i, c: (i, 0)), _full((1, d)),
                  pl.BlockSpec((d, fc), lambda i, c: (0, c)),
                  pl.BlockSpec((d, fc), lambda i, c: (0, nch + c)),
                  pl.BlockSpec((fc, d), lambda i, c: (c, 0))],
        out_specs=pl.BlockSpec((tm, d), lambda i, c: (i, 0)),
        out_shape=jax.ShapeDtypeStruct((m, d), F32),
        scratch_shapes=[pltpu.VMEM((tm, d), BF16), pltpu.VMEM((tm, d), F32)],
        compiler_params=_cparams(("parallel", "arbitrary"), VMEM_LIMIT), name="ffn",
    )(x, g.reshape(1, d), w_gu, w_gu, w_dn)


def _router_kernel(x_ref, g_ref, w_ref, b_ref, comb_ref, h_ref, mask_ref, cnt_ref, *, ne):
    h = _rms(x_ref[...], g_ref[...]).astype(BF16)
    h_ref[...] = h
    logits = _dot(h, w_ref[...]) + b_ref[...]
    lane = lax.broadcasted_iota(jnp.int32, logits.shape, 1)
    logits = jnp.where(lane < ne, logits, -jnp.inf)
    v1 = jnp.max(logits, -1, keepdims=True)
    i1 = jnp.min(jnp.where(logits == v1, lane, LANES), -1, keepdims=True)
    rest = jnp.where(lane == i1, -jnp.inf, logits)
    v2 = jnp.max(rest, -1, keepdims=True)
    i2 = jnp.min(jnp.where(rest == v2, lane, LANES), -1, keepdims=True)
    e2 = jnp.exp(v2 - v1)
    den = 1.0 + e2
    comb_ref[...] = jnp.where(lane == i1, 1.0 / den, 0.0) + jnp.where(lane == i2, e2 / den, 0.0)
    chosen = jnp.where((lane == i1) | (lane == i2), 1.0, 0.0)
    mask_ref[...] = chosen.astype(BF16)

    @pl.when(pl.program_id(0) == 0)
    def _():
        cnt_ref[...] = jnp.zeros_like(cnt_ref)

    cnt_ref[0:1, :] += jnp.sum(chosen, axis=0, keepdims=True)


def router(x, g, w_r, b_r):
    m, d = x.shape
    ne = w_r.shape[1]
    w_pad = jnp.zeros((d, LANES), BF16).at[:, :ne].set(w_r.astype(BF16))
    b_pad = jnp.zeros((1, LANES), F32).at[0, :ne].set(b_r.astype(F32))
    tm = _row_tile(m, 512)
    row = lambda n: pl.BlockSpec((tm, n), lambda i: (i, 0))
    return pl.pallas_call(
        functools.partial(_router_kernel, ne=ne), grid=(m // tm,),
        in_specs=[row(d), _full((1, d)), _full((d, LANES)), _full((1, LANES))],
        out_specs=[row(LANES), row(d), row(LANES), _full((SUBLANES, LANES))],
        out_shape=[jax.ShapeDtypeStruct((m, LANES), F32), jax.ShapeDtypeStruct((m, d), BF16),
                   jax.ShapeDtypeStruct((m, LANES), BF16), jax.ShapeDtypeStruct((SUBLANES, LANES), F32)],
        compiler_params=_cparams(("arbitrary",)), name="router",
    )(x, g.reshape(1, d), w_pad, b_pad)


def _residual_out(y, gain_ref, norm):
    return _rms(y, gain_ref[...]) if norm else y


def _moe_kernel(x_ref, h_ref, comb_ref, wg_ref, wu_ref, wd_ref, gain_ref, o_ref, acc_scr, *, norm):
    e = pl.program_id(1)

    @pl.when(e == 0)
    def _():
        acc_scr[...] = jnp.zeros_like(acc_scr)

    h = h_ref[...]
    gate = _dot(h, wg_ref[0])
    up = _dot(h, wu_ref[0])
    act = (gate * jax.nn.sigmoid(gate) * up).astype(BF16)
    y = _dot(act, wd_ref[0])
    comb = comb_ref[...]
    lane = lax.broadcasted_iota(jnp.int32, comb.shape, 1)
    acc_scr[...] += jnp.sum(jnp.where(lane == e, comb, 0.0), -1, keepdims=True) * y

    @pl.when(e == pl.num_programs(1) - 1)
    def _():
        o_ref[...] = _residual_out(x_ref[...] + acc_scr[...], gain_ref, norm)


def moe(x, h, comb, w_gu, w_dn, final_g=None):
    m, d = x.shape
    ne, dfe = w_dn.shape[:2]
    tm = _row_tile(m, 512)
    gain = jnp.ones((1, d), F32) if final_g is None else final_g.reshape(1, d)
    return pl.pallas_call(
        functools.partial(_moe_kernel, norm=final_g is not None), grid=(m // tm, ne),
        in_specs=[pl.BlockSpec((tm, d), lambda i, e: (i, 0)), pl.BlockSpec((tm, d), lambda i, e: (i, 0)),
                  pl.BlockSpec((tm, LANES), lambda i, e: (i, 0)),
                  pl.BlockSpec((1, d, dfe), lambda i, e: (e, 0, 0)),
                  pl.BlockSpec((1, d, dfe), lambda i, e: (e, 0, 1)),
                  pl.BlockSpec((1, dfe, d), lambda i, e: (e, 0, 0)), _full((1, d))],
        out_specs=pl.BlockSpec((tm, d), lambda i, e: (i, 0)),
        out_shape=jax.ShapeDtypeStruct((m, d), F32),
        scratch_shapes=[pltpu.VMEM((tm, d), F32)],
        compiler_params=_cparams(("parallel", "arbitrary"), VMEM_LIMIT), name="moe",
    )(x, h, comb, w_gu, w_gu, w_dn, gain)


MOE_TILE = 256


def _moe_pos_kernel(mask_ref, comb_ref, tri_ref, base_ref, post_ref, pos2_ref, wab_ref, stab_ref, run_scr, *, nep):
    sb = pl.program_id(0)
    nb = pl.num_programs(0)

    @pl.when(sb == 0)
    def _():
        run_scr[...] = jnp.zeros_like(run_scr)
        stab_ref[...] = jnp.zeros_like(stab_ref)

    a = mask_ref[...]
    af = a.astype(F32)
    start = base_ref[...] + run_scr[...]
    stab_ref[pl.ds(sb, 1), :] = start.astype(jnp.int32)
    rank = _dot(tri_ref[...], a)
    pos = jnp.where(af > 0, start + rank, -1.0)
    lane = lax.broadcasted_iota(jnp.int32, pos.shape, 1)
    first_e = jnp.min(jnp.where(af > 0, lane, LANES), -1, keepdims=True)
    last_e = jnp.max(jnp.where(af > 0, lane, -1), -1, keepdims=True)
    comb = comb_ref[...]
    w_a = jnp.sum(jnp.where(lane == first_e, comb, 0.0), -1, keepdims=True)
    w_b = jnp.sum(jnp.where(lane == last_e, comb, 0.0), -1, keepdims=True)
    wab_ref[...] = jnp.where(lane == 0, w_a, jnp.where(lane == 1, w_b, 0.0))
    pos_t = pos.T[0:nep, :]
    post_ref[0] = pos_t.astype(jnp.int32)
    row = lax.broadcasted_iota(jnp.int32, pos_t.shape, 0)
    first_r = jnp.min(jnp.where(pos_t >= 0, row, nep), 0, keepdims=True)
    last_r = jnp.max(jnp.where(pos_t >= 0, row, -1), 0, keepdims=True)
    pos_a = jnp.sum(jnp.where(row == first_r, pos_t, 0.0), 0, keepdims=True)
    pos_b = jnp.sum(jnp.where(row == last_r, pos_t, 0.0), 0, keepdims=True)
    pos2_ref[0] = jnp.where(row == 0, pos_a, jnp.where(row == 1, pos_b, 0.0)).astype(jnp.int32)
    run_scr[...] += jnp.sum(af, axis=0, keepdims=True)

    @pl.when(sb == nb - 1)
    def _():
        stab_ref[pl.ds(nb, 1), :] = (base_ref[...] + run_scr[...]).astype(jnp.int32)


def _moe_expert_kernel(te_ref, lo_ref, cnt_ref, ring_ref, nt_ref, h_ref, post_ref, wg_ref, wu_ref, wd_ref, y_ref,
                       hbuf, xg_scr, sem, *, t):
    i = pl.program_id(0)

    @pl.when(i >= nt_ref[0])
    def _():
        y_ref[...] = jnp.zeros_like(y_ref)

    nbuf = hbuf.shape[0]
    ahead = nbuf - 1

    def copy(sb, slot):
        return pltpu.make_async_copy(h_ref.at[pl.ds(pl.multiple_of(sb * t, t), t), :], hbuf.at[slot], sem.at[slot])

    def start_first(tile):
        for j in range(ahead):
            @pl.when(j < cnt_ref[tile])
            def _():
                copy(lo_ref[tile] + j, (ring_ref[tile] + j) % nbuf).start()

    @pl.when(i == 0)
    def _():
        start_first(0)

    @pl.when(i < nt_ref[0])
    def _():
        e = te_ref[i]
        lo = lo_ref[i]
        n = cnt_ref[i]
        ring = ring_ref[i]
        xg_scr[...] = jnp.zeros_like(xg_scr)
        row = i * t + lax.broadcasted_iota(jnp.int32, (t, 1), 0)

        def body(j, c):
            slot = (ring + j) % nbuf
            copy(lo + j, slot).wait()

            @pl.when(j + ahead < n)
            def _():
                copy(lo + j + ahead, (ring + j + ahead) % nbuf).start()

            src_pos = post_ref[lo + j, pl.ds(e, 1), :]
            onehot = jnp.where(src_pos == row, 1.0, 0.0).astype(BF16)
            xg_scr[...] += _dot(onehot, hbuf[slot])
            return c

        lax.fori_loop(0, n, body, 0)

        @pl.when(i + 1 < nt_ref[0])
        def _():
            start_first(i + 1)

        x = xg_scr[...].astype(BF16)
        gate = _dot(x, wg_ref[0])
        up = _dot(x, wu_ref[0])
        act = (gate * jax.nn.sigmoid(gate) * up).astype(BF16)
        y_ref[...] = _dot(act, wd_ref[0])


def _moe_combine_kernel(x_ref, wab_ref, pos_ref, nxt_ref, gain_ref, ys_ref, o_ref, ybuf, sem, *, t, norm):
    sb = pl.program_id(0)
    nb = pl.num_programs(0)

    def row_copy(p_ref, which, slot, tok):
        return pltpu.make_async_copy(ys_ref.at[pl.ds(p_ref[0, which, tok], 1), :],
                                     ybuf.at[slot, which, pl.ds(tok, 1), :], sem.at[slot])

    def start_all(p_ref, slot):
        def body(tok, c):
            row_copy(p_ref, 0, slot, tok).start()
            row_copy(p_ref, 1, slot, tok).start()
            return c
        lax.fori_loop(0, t, body, 0, unroll=8)

    @pl.when(sb == 0)
    def _():
        start_all(pos_ref, 0)

    @pl.when(sb + 1 < nb)
    def _():
        start_all(nxt_ref, (sb + 1) % 2)

    slot = sb % 2

    def wait_body(tok, c):
        row_copy(pos_ref, 0, slot, tok).wait()
        row_copy(pos_ref, 1, slot, tok).wait()
        return c
    lax.fori_loop(0, t, wait_body, 0, unroll=8)
    w = wab_ref[...]
    y = x_ref[...] + w[:, 0:1] * ybuf[slot, 0] + w[:, 1:2] * ybuf[slot, 1]
    o_ref[...] = _residual_out(y, gain_ref, norm)


def moe_grouped(x, h, comb, mask, counts, w_gu, w_dn, final_g=None):
    m, d = x.shape
    ne, dfe = w_dn.shape[:2]
    t = MOE_TILE
    assert m % t == 0
    nb = m // t
    nep = _round_up(ne, SUBLANES)
    nbp = _round_up(nb + 1, SUBLANES)
    k_top = TOP_K
    nt_max = k_top * m // t + ne
    cnt = counts[0, :ne].astype(jnp.int32)
    cnt_pad = (cnt + t - 1) // t * t
    ends = jnp.cumsum(cnt_pad)
    base = ends - cnt_pad
    base_row = jnp.zeros((1, LANES), F32).at[0, :ne].set(base.astype(F32))
    idx = lax.broadcasted_iota(jnp.int32, (t, t), 0)
    tri = jnp.where(lax.broadcasted_iota(jnp.int32, (t, t), 1) < idx, 1.0, 0.0).astype(BF16)
    blk = lambda n2: pl.BlockSpec((t, n2), lambda i: (i, 0))
    post, pos2, wab, stab = pl.pallas_call(
        functools.partial(_moe_pos_kernel, nep=nep), grid=(nb,),
        in_specs=[blk(LANES), blk(LANES), _full((t, t)), _full((1, LANES))],
        out_specs=[pl.BlockSpec((1, nep, t), lambda i: (i, 0, 0)), pl.BlockSpec((1, nep, t), lambda i: (i, 0, 0)),
                   blk(LANES), _full((nbp, LANES))],
        out_shape=[jax.ShapeDtypeStruct((nb, nep, t), jnp.int32), jax.ShapeDtypeStruct((nb, nep, t), jnp.int32),
                   jax.ShapeDtypeStruct((m, LANES), F32), jax.ShapeDtypeStruct((nbp, LANES), jnp.int32)],
        scratch_shapes=[pltpu.VMEM((1, LANES), F32)],
        compiler_params=_cparams(("arbitrary",)), name="moe_positions",
    )(mask, comb, tri, base_row)
    r0 = jnp.arange(nt_max, dtype=jnp.int32) * t
    tile_e = jnp.minimum(jnp.sum(ends[None, :] <= r0[:, None], axis=1), ne - 1).astype(jnp.int32)
    n_tiles = (ends[-1] // t).astype(jnp.int32).reshape(1)
    s_e = stab[:nb + 1, :ne][:, tile_e]
    lo = jnp.sum(s_e[1:] <= r0[None, :], axis=0)
    hi = jnp.sum(s_e[:nb] < r0[None, :] + t, axis=0) - 1
    lo = jnp.clip(lo, 0, nb - 1).astype(jnp.int32)
    hi = jnp.clip(hi, lo, nb - 1).astype(jnp.int32)
    n_src = jnp.where(jnp.arange(nt_max) < n_tiles[0], hi - lo + 1, 0).astype(jnp.int32)
    n_ring = 3
    ring = ((jnp.cumsum(n_src) - n_src) % n_ring).astype(jnp.int32)
    w_spec = lambda shape, col: pl.BlockSpec(shape, lambda i, te, *_: (te[i], 0, col))
    grid_spec = pltpu.PrefetchScalarGridSpec(
        num_scalar_prefetch=5, grid=(nt_max,),
        in_specs=[pl.BlockSpec(memory_space=pl.ANY),
                  pl.BlockSpec((nb, nep, t), lambda i, *_: (0, 0, 0)),
                  w_spec((1, d, dfe), 0), w_spec((1, d, dfe), 1), w_spec((1, dfe, d), 0)],
        out_specs=pl.BlockSpec((t, d), lambda i, *_: (i, 0)),
        scratch_shapes=[pltpu.VMEM((n_ring, t, d), BF16), pltpu.VMEM((t, d), F32),
                        pltpu.SemaphoreType.DMA((n_ring,))])
    ys = pl.pallas_call(
        functools.partial(_moe_expert_kernel, t=t), grid_spec=grid_spec,
        out_shape=jax.ShapeDtypeStruct((nt_max * t, d), F32),
        compiler_params=_cparams(("arbitrary",), VMEM_LIMIT), name="moe_experts",
    )(tile_e, lo, n_src, ring, n_tiles, h, post, w_gu, w_gu, w_dn)
    smem_blk = lambda f: pl.BlockSpec((1, nep, t), f, memory_space=pltpu.SMEM)
    gain = jnp.ones((1, d), F32) if final_g is None else final_g.reshape(1, d)
    return pl.pallas_call(
        functools.partial(_moe_combine_kernel, t=t, norm=final_g is not None), grid=(nb,),
        in_specs=[blk(d), blk(LANES), smem_blk(lambda i: (i, 0, 0)),
                  smem_blk(lambda i: (jnp.minimum(i + 1, nb - 1), 0, 0)), _full((1, d)),
                  pl.BlockSpec(memory_space=pl.ANY)],
        out_specs=blk(d),
        out_shape=jax.ShapeDtypeStruct((m, d), F32),
        scratch_shapes=[pltpu.VMEM((2, 2, t, d), F32), pltpu.SemaphoreType.DMA((2,))],
        compiler_params=_cparams(("arbitrary",), VMEM_LIMIT), name="moe_combine",
    )(x, wab, pos2, pos2, gain, ys)


def _inproj_odd_kernel(x_ref, g_ref, w_ref, bias_ref, q_ref, k_ref, v_ref, og_ref, gc_ref, gr_ref, *, hq, hv, nh):
    xn = _rms(x_ref[...], g_ref[...]).astype(BF16)

    def mm(lo, hi):
        return _dot(xn, w_ref[:, lo:hi])

    dk = hq // nh
    q_ref[...] = mm(0, hq).astype(BF16)
    k_ref[...] = (mm(hq, 2 * hq) * (dk ** -0.5)).astype(BF16)
    v_ref[...] = mm(2 * hq, 2 * hq + hv).astype(BF16)
    og_ref[...] = jax.nn.sigmoid(mm(2 * hq + hv, 2 * hq + 2 * hv))
    gi = mm(2 * hq + 2 * hv, 2 * hq + 2 * hv + LANES) + bias_ref[...]
    lane = lax.broadcasted_iota(jnp.int32, gi.shape, 1)
    gates = jnp.where(lane < nh, gi, jax.nn.log_sigmoid(gi))
    gc_ref[...] = gates
    gr_ref[...] = gates.T[0:SUBLANES, :]


def inproj_odd(x, g, w_pad, gate_bias):
    m, d = x.shape
    nh = MLSTM_HEADS
    hq = hv = d
    tm = _row_tile(m, 256)
    row = lambda n: pl.BlockSpec((tm, n), lambda i: (i, 0))
    outs = [(hq, BF16), (hq, BF16), (hv, BF16), (hv, F32), (LANES, F32)]
    if tm % LANES:
        gr_spec = _full((SUBLANES, m))
    else:
        gr_spec = pl.BlockSpec((SUBLANES, tm), lambda i: (0, i))
    return pl.pallas_call(
        functools.partial(_inproj_odd_kernel, hq=hq, hv=hv, nh=nh), grid=(m // tm,),
        in_specs=[row(d), _full((1, d)), _full(w_pad.shape), _full((1, LANES))],
        out_specs=[row(n) for n, _ in outs] + [gr_spec],
        out_shape=[jax.ShapeDtypeStruct((m, n), t) for n, t in outs] + [jax.ShapeDtypeStruct((SUBLANES, m), F32)],
        compiler_params=_cparams(("parallel",), VMEM_LIMIT), name="inproj_odd",
    )(x, g.reshape(1, d), w_pad, gate_bias)


def prep_odd(w_in, b_i, b_f):
    d, n = w_in.shape
    n_pad = _round_up(n - 2 * MLSTM_HEADS, LANES) + LANES
    w_pad = jnp.zeros((d, n_pad), BF16).at[:, :n].set(w_in.astype(BF16))
    bias = jnp.zeros((1, LANES), F32).at[0, :2 * MLSTM_HEADS].set(jnp.concatenate([b_i, b_f]).astype(F32))
    return dict(w_in=w_pad, bias=bias)


def _mlstm_prompt_kernel(q_ref, k_ref, v_ref, og_ref, gc_ref, gr_ref, gain_ref, hn_ref, c_ref, n_ref, m_ref,
                         *, nh, dk, dv, ln):
    ci = pl.program_id(1)

    @pl.when(ci == 0)
    def _():
        c_ref[...] = jnp.zeros_like(c_ref)
        n_ref[...] = jnp.zeros_like(n_ref)
        m_ref[...] = jnp.full(m_ref.shape, NEG, F32)

    row = lax.broadcasted_iota(jnp.int32, (ln, ln), 0)
    col = lax.broadcasted_iota(jnp.int32, (ln, ln), 1)
    tri = row >= col
    gc = gc_ref[...]
    gr = gr_ref[...]
    for h in range(nh):
        q = q_ref[:, h * dk:(h + 1) * dk]
        k = k_ref[:, h * dk:(h + 1) * dk]
        v = v_ref[:, h * dv:(h + 1) * dv]
        ig_c, lf_c = gc[:, h:h + 1], gc[:, nh + h:nh + h + 1]
        ig_r, lf_r = gr[h:h + 1, :], gr[nh + h:nh + h + 1, :]
        b_c = jnp.sum(jnp.where(tri, lf_r, 0.0), axis=1, keepdims=True)
        b_r = jnp.sum(jnp.where(row <= col, lf_c, 0.0), axis=0, keepdims=True)
        m_prev = m_ref[0, h:h + 1, 0:1]
        c_prev = c_ref[0, h]
        n_prev = n_ref[0, h:h + 1, :]
        dmat = jnp.where(tri, b_c - b_r + ig_r, NEG)
        inter = b_c + m_prev
        mt = jnp.maximum(inter, jnp.max(dmat, -1, keepdims=True))
        wm = jnp.exp(dmat - mt)
        a = jnp.exp(inter - mt)
        wqk = wm * _dot_nt(q, k)
        num = a * _dot_nt(q, c_prev.astype(BF16)) + _dot(wqk.astype(BF16), v)
        den = a * jnp.sum(q.astype(F32) * n_prev, -1, keepdims=True) + jnp.sum(wqk, -1, keepdims=True)
        hh = num / jnp.maximum(jnp.abs(den), jnp.exp(-mt))
        b_end = b_c[ln - 1:ln, :]
        m_new = mt[ln - 1:ln, :]
        a_end = jnp.exp(b_end + m_prev - m_new)
        w_s = jnp.exp(b_end - b_c + ig_c - m_new)
        c_ref[0, h] = a_end * c_prev + _dot_tn((v.astype(F32) * w_s).astype(BF16), k)
        n_ref[0, h:h + 1, :] = a_end * n_prev + jnp.sum(w_s * k.astype(F32), axis=0, keepdims=True)
        m_ref[0, h:h + 1, :] = jnp.broadcast_to(m_new, (1, m_ref.shape[2]))
        hn = hh * lax.rsqrt(jnp.mean(hh * hh, -1, keepdims=True) + RMS_EPS)
        hn = hn * gain_ref[:, h * dv:(h + 1) * dv] * og_ref[:, h * dv:(h + 1) * dv]
        hn_ref[:, h * dv:(h + 1) * dv] = hn.astype(hn_ref.dtype)


def mlstm_prompt(q, k, v, og, gc, gr, gain, b, s):
    m, d = q.shape
    nh = MLSTM_HEADS
    dk = dv = d // nh
    ln = _row_tile(s, MLSTM_CHUNK)
    nch = s // ln
    row = lambda n: pl.BlockSpec((ln, n), lambda bi, ci: (bi * nch + ci, 0))
    return pl.pallas_call(
        functools.partial(_mlstm_prompt_kernel, nh=nh, dk=dk, dv=dv, ln=ln), grid=(b, nch),
        in_specs=[row(d), row(d), row(d), row(d), row(LANES),
                  pl.BlockSpec((SUBLANES, ln), lambda bi, ci: (0, bi * nch + ci)), _full((1, d))],
        out_specs=[row(d), pl.BlockSpec((1, nh, dv, dk), lambda bi, ci: (bi, 0, 0, 0)),
                   pl.BlockSpec((1, nh, dk), lambda bi, ci: (bi, 0, 0)),
                   pl.BlockSpec((1, nh, LANES), lambda bi, ci: (bi, 0, 0))],
        out_shape=[jax.ShapeDtypeStruct((m, d), BF16), jax.ShapeDtypeStruct((b, nh, dv, dk), F32),
                   jax.ShapeDtypeStruct((b, nh, dk), F32), jax.ShapeDtypeStruct((b, nh, LANES), F32)],
        compiler_params=_cparams(("parallel", "arbitrary"), VMEM_LIMIT), name="mlstm_prompt",
    )(q, k, v, og, gc, gr, gain.reshape(1, d))


def _mlstm_sample_kernel(q_ref, k_ref, v_ref, og_ref, g_ref, gain_ref, c_ref, n_ref, m_ref,
                         hn_ref, co_ref, no_ref, mo_ref, *, nh):
    row = lax.broadcasted_iota(jnp.int32, (SUBLANES, 1), 0)
    for h in range(nh):
        q = q_ref[0, h:h + 1, :]
        k = k_ref[0, h:h + 1, :]
        v = v_ref[0, h:h + 1, :].astype(F32)
        ig = g_ref[0, h:h + 1, 0:1]
        lf = g_ref[0, h:h + 1, 1:2]
        m_prev = m_ref[0, h:h + 1, :]
        c_prev = c_ref[0, h]
        n_prev = n_ref[0, h:h + 1, :]
        inter = lf + m_prev
        mt = jnp.maximum(inter, ig)
        wm = jnp.exp(ig - mt)
        a = jnp.exp(inter - mt)
        q8 = jnp.broadcast_to(q, (SUBLANES, q.shape[1]))
        cq = _dot_nt(q8, c_prev.astype(BF16))[0:1, :]
        wqk = wm * jnp.sum(q.astype(F32) * k.astype(F32), -1, keepdims=True)
        num = a * cq + wqk * v
        den = a * jnp.sum(n_prev * q.astype(F32), -1, keepdims=True) + wqk
        hh = num / jnp.maximum(jnp.abs(den), jnp.exp(-mt))
        v8 = jnp.where(row == 0, jnp.broadcast_to(v * wm, (SUBLANES, v.shape[1])), 0.0).astype(BF16)
        k8 = jnp.broadcast_to(k, (SUBLANES, k.shape[1]))
        co_ref[0, h] = a * c_prev + _dot_tn(v8, k8)
        no_ref[0, h:h + 1, :] = a * n_prev + wm * k.astype(F32)
        mo_ref[0, h:h + 1, :] = mt
        hn = hh * lax.rsqrt(jnp.mean(hh * hh, -1, keepdims=True) + RMS_EPS)
        hn_ref[0, h:h + 1, :] = (hn * gain_ref[h:h + 1, :] * og_ref[0, h:h + 1, :]).astype(hn_ref.dtype)


def mlstm_sample(q, k, v, og, gc, gain, c, n, m):
    bd, d = q.shape
    nh = MLSTM_HEADS
    dk = d // nh
    heads = lambda a: a.reshape(bd, nh, dk)
    g2 = jnp.transpose(gc[:, :2 * nh].reshape(bd, 2, nh), (0, 2, 1))
    blk3 = lambda n2: pl.BlockSpec((1, nh, n2), lambda bi: (bi, 0, 0))
    cspec = pl.BlockSpec((1, nh, dk, dk), lambda bi: (bi, 0, 0, 0))
    hn, co, no, mo = pl.pallas_call(
        functools.partial(_mlstm_sample_kernel, nh=nh), grid=(bd,),
        in_specs=[blk3(dk), blk3(dk), blk3(dk), blk3(dk), blk3(2), _full((nh, dk)), cspec, blk3(dk), blk3(1)],
        out_specs=[blk3(dk), cspec, blk3(dk), blk3(1)],
        out_shape=[jax.ShapeDtypeStruct((bd, nh, dk), BF16), jax.ShapeDtypeStruct(c.shape, F32),
                   jax.ShapeDtypeStruct(n.shape, F32), jax.ShapeDtypeStruct((bd, nh, 1), F32)],
        compiler_params=_cparams(("parallel",)), name="mlstm_sample",
    )(heads(q), heads(k), heads(v), heads(og), g2, gain.reshape(nh, dk), c, n, m.reshape(bd, nh, 1))
    return hn.reshape(bd, d), co, no, mo.reshape(bd, nh)


def _head_pad(q, keep):
    q2 = jnp.concatenate([q] * NSA_KV_HEADS, axis=1)
    row = lax.broadcasted_iota(jnp.int32, q2.shape, 0)
    lane = lax.broadcasted_iota(jnp.int32, q2.shape, 1)
    return jnp.where((lane // NSA_HD == row // NSA_GROUP) & keep(row), q2, jnp.zeros_like(q2))


def _nsa_sample_cmp_kernel(pt_ref, q_ref, pages_ref, pe_ref, w_ref, biasc_ref, oc_ref, idx_ref, xbuf, sem,
                           *, n_pages, nc, ncp, nsb, n_sel, past):
    b = pl.program_id(0)
    nb = pl.num_programs(0)
    kvd = NSA_KV_HEADS * NSA_HD

    def page_copy(bb, slot, p, sl):
        return pltpu.make_async_copy(pages_ref.at[pt_ref[bb * n_pages + p], :, pl.ds(sl * kvd, kvd)],
                                     xbuf.at[slot, sl, pl.ds(p * PAGE_SIZE, PAGE_SIZE), :], sem.at[slot])

    def start_all(bb, slot):
        def body(p, c):
            page_copy(bb, slot, p, 0).start()
            page_copy(bb, slot, p, 1).start()
            return c
        lax.fori_loop(0, n_pages, body, 0)

    @pl.when(b == 0)
    def _():
        start_all(0, 0)

    @pl.when(b + 1 < nb)
    def _():
        start_all(b + 1, (b + 1) % 2)

    slot = b % 2

    def wait_body(p, c):
        page_copy(b, slot, p, 0).wait()
        page_copy(b, slot, p, 1).wait()
        return c
    lax.fori_loop(0, n_pages, wait_body, 0)

    acc = _compress_accumulate(lambda sl, j: xbuf[slot, sl, pl.ds(j, nc, stride=CMP_BLOCK), :], pe_ref, w_ref, nc)
    kc = acc[:, 0:kvd].astype(BF16)
    vc = acc[:, kvd:2 * kvd].astype(BF16)
    q = q_ref[0]
    nh = q.shape[0]
    qpad = _head_pad(q, lambda r: r >= 0)
    s = _dot_nt(qpad, kc)
    s = s + biasc_ref[:, 0:nc]
    e = jnp.exp(s - jnp.max(s, -1, keepdims=True))
    p_c = e / jnp.sum(e, -1, keepdims=True)
    o = _dot(p_c.astype(BF16), vc)
    row = lax.broadcasted_iota(jnp.int32, (nh, NSA_HD), 0)
    o_h = o[:, 0:NSA_HD]
    for k in range(1, NSA_KV_HEADS):
        o_h = jnp.where(row // NSA_GROUP == k, o[:, k * NSA_HD:(k + 1) * NSA_HD], o_h)
    oc_ref[0] = o_h
    prow = lax.broadcasted_iota(jnp.int32, p_c.shape, 0)
    lane = lax.broadcasted_iota(jnp.int32, (1, ncp), 1)
    blk = lane // 2
    cur = past // SEL_BLOCK
    forced = (blk == 0) | (blk == cur) | (blk == cur - 1)
    is_cand = ((lane % 2) == 0) & (lane < 2 * nsb)
    nselp = idx_ref.shape[1]
    rsel = lax.broadcasted_iota(jnp.int32, (nselp, 1), 0).astype(F32)
    out_lane = lax.broadcasted_iota(jnp.int32, (nselp, LANES), 1)
    result = jnp.full((nselp, LANES), -1, jnp.int32)
    for k in range(NSA_KV_HEADS):
        imp = jnp.sum(jnp.where(prow // NSA_GROUP == k, p_c, 0.0), axis=0, keepdims=True)
        imp = jnp.concatenate([imp, jnp.zeros((1, ncp - nc), F32)], axis=1)
        imp = _pair_sum(imp)
        score = jnp.where(forced, FORCE_SCORE, jnp.where(blk <= cur, imp, -1.0))
        score = jnp.where(is_cand, score, -2.0)
        sel, rank = _select_blocks(score, n_sel, nsb)
        hit = (rank == rsel) & (sel > 0.5)
        idx = jnp.sum(jnp.where(hit, (blk + 1).astype(F32), 0.0), axis=1, keepdims=True) - 1.0
        result = jnp.where(out_lane == k, idx.astype(jnp.int32), result)
    idx_ref[0] = result


def nsa_sample_cmp(q8, pages, page_table, pe2, wbd, rel_bias):
    bd, nh, hd = q8.shape
    n_pages = page_table.shape[1]
    past = n_pages * PAGE_SIZE
    nc = past // CMP_BLOCK
    nsb = -(-(past + 1) // SEL_BLOCK)
    n_sel = min(TOP_N, nsb)
    ncp = _round_up(max(nc, SEL_RATIO * nsb), LANES)
    nselp = _round_up(n_sel, SUBLANES)
    kvd = wbd.shape[-1]
    cend = jnp.arange(nc) * CMP_BLOCK + (CMP_BLOCK - 1)
    biasc = jnp.zeros((nh, ncp), F32).at[:, :nc].set(_bias_of(rel_bias, past - cend).T)
    grid_spec = pltpu.PrefetchScalarGridSpec(
        num_scalar_prefetch=1, grid=(bd,),
        in_specs=[pl.BlockSpec((1, nh, hd), lambda bi, pt: (bi, 0, 0)),
                  pl.BlockSpec(memory_space=pl.ANY),
                  pl.BlockSpec(pe2.shape, lambda bi, pt: (0, 0, 0)),
                  pl.BlockSpec(wbd.shape, lambda bi, pt: (0, 0, 0, 0)),
                  pl.BlockSpec((nh, ncp), lambda bi, pt: (0, 0))],
        out_specs=[pl.BlockSpec((1, nh, hd), lambda bi, pt: (bi, 0, 0)),
                   pl.BlockSpec((1, nselp, LANES), lambda bi, pt: (bi, 0, 0))],
        scratch_shapes=[pltpu.VMEM((2, 2, past, kvd), F32), pltpu.SemaphoreType.DMA((2,))])
    oc, idx = pl.pallas_call(
        functools.partial(_nsa_sample_cmp_kernel, n_pages=n_pages, nc=nc, ncp=ncp, nsb=nsb, n_sel=n_sel, past=past),
        grid_spec=grid_spec,
        out_shape=[jax.ShapeDtypeStruct((bd, nh, hd), F32), jax.ShapeDtypeStruct((bd, nselp, LANES), jnp.int32)],
        compiler_params=_cparams(("arbitrary",), VMEM_LIMIT), name="nsa_sample_cmp",
    )(page_table.reshape(-1), q8, pages, pe2, wbd, biasc)
    sel_idx = jnp.transpose(idx[:, :n_sel, :NSA_KV_HEADS], (0, 2, 1))
    return oc, sel_idx


def _nsa_sample_att_kernel(pt_ref, si_ref, q_ref, g_ref, oc_ref, kvn_ref, wn_ref, wc_ref, pages_ref,
                           bsel_ref, bwin_ref, ob_ref, win_ref, selbuf, wall, sem,
                           *, n_pages, n_sel, past, wb):
    b = pl.program_id(0)
    nb = pl.num_programs(0)
    kvd = NSA_KV_HEADS * NSA_HD
    hd = NSA_HD
    n_blk_pages = past // SEL_BLOCK
    per_page = PAGE_SIZE // SEL_BLOCK
    n_slots = NSA_KV_HEADS * n_sel

    def blk_of(bb, j):
        return si_ref[bb * n_slots + j]

    def blk_copy(bb, slot, j):
        blk = jnp.clip(blk_of(bb, j), 0, n_blk_pages - 1)
        page = pt_ref[bb * n_pages + blk // per_page]
        r0 = pl.multiple_of((blk % per_page) * SEL_BLOCK, SEL_BLOCK)
        return pltpu.make_async_copy(pages_ref.at[page, pl.ds(r0, SEL_BLOCK), pl.ds(2 * kvd, 2 * kvd)],
                                     selbuf.at[slot, j], sem.at[slot])

    def in_pages(bb, j):
        blk = blk_of(bb, j)
        return (blk >= 0) & (blk < n_blk_pages)

    def start_all(bb, slot):
        def body(j, c):
            @pl.when(in_pages(bb, j))
            def _():
                blk_copy(bb, slot, j).start()
            return c
        lax.fori_loop(0, n_slots, body, 0)

    @pl.when(b == 0)
    def _():
        start_all(0, 0)

    @pl.when(b + 1 < nb)
    def _():
        start_all(b + 1, (b + 1) % 2)

    slot = b % 2
    new_sel = kvn_ref[0][:, 2 * kvd:4 * kvd]
    sub = lax.broadcasted_iota(jnp.int32, (SEL_BLOCK, 1), 0)

    def wait_body(j, c):
        @pl.when(in_pages(b, j))
        def _():
            blk_copy(b, slot, j).wait()

        @pl.when(jnp.logical_not(in_pages(b, j)))
        def _():
            is_new = blk_of(b, j) == n_blk_pages
            selbuf[slot, j] = jnp.where((sub == 0) & is_new, new_sel, 0.0)
        return c
    lax.fori_loop(0, n_slots, wait_body, 0)

    q = q_ref[0]
    nh = q.shape[0]
    gates = g_ref[0]
    o_c = oc_ref[0]

    def q_rows(keep):
        qp = _head_pad(q, keep)
        return jnp.concatenate([qp, jnp.zeros((LANES - nh, qp.shape[1]), qp.dtype)], axis=0)

    def attend(qrows, keys_b, vals, bias, valid):
        s = _dot_nt(keys_b, qrows) + bias
        s = jnp.where(valid, s, NEG)
        e = jnp.where(valid, jnp.exp(s - jnp.max(s, 0, keepdims=True)), 0.0)
        den = jnp.sum(e, 0, keepdims=True)
        p = e / jnp.where(den > 0, den, 1.0)
        return [jnp.sum(p[:, h:h + 1] * vals, axis=0, keepdims=True) for h in range(nh)]

    o_s = [None] * nh
    for k in range(NSA_KV_HEADS):
        qrows = q_rows(lambda r: r // NSA_GROUP == k)
        blks = [blk_of(b, k * n_sel + r) for r in range(n_sel)]
        keys = selbuf[slot, k * n_sel:(k + 1) * n_sel].reshape(n_sel * SEL_BLOCK, 2 * kvd)
        bias = jnp.concatenate([bsel_ref[jnp.maximum(bl, 0)] for bl in blks], axis=0)
        valid = jnp.concatenate([(bl * SEL_BLOCK + sub <= past) & (bl >= 0) for bl in blks], axis=0)
        rows = attend(qrows, keys[:, 0:kvd].astype(BF16), keys[:, kvd:2 * kvd], bias, valid)
        for h in range(k * NSA_GROUP, (k + 1) * NSA_GROUP):
            o_s[h] = rows[h][:, k * hd:(k + 1) * hd]
    wbp = wall.shape[0]
    wall[0:wb, :] = wc_ref[0]
    wall[wb:wb + 1, :] = wn_ref[0]
    wall[wb + 1:wbp, :] = jnp.zeros((wbp - wb - 1, 2 * kvd), F32)
    win_ref[0] = wall[1:wb + 1, :]
    w_pos = lax.broadcasted_iota(jnp.int32, (wbp, 1), 0)
    valid_w = (w_pos <= wb) & (wb - w_pos < WINDOW) & (past - wb + w_pos >= 0)
    qrows = q_rows(lambda r: r >= 0)
    rows = attend(qrows, wall[:, 0:kvd].astype(BF16), wall[:, kvd:2 * kvd], bwin_ref[...], valid_w)
    for h in range(nh):
        k = h // NSA_GROUP
        o_w = rows[h][:, k * hd:(k + 1) * hd]
        mix = gates[h:h + 1, 0:1] * o_c[h:h + 1, :] + gates[h:h + 1, 1:2] * o_s[h] + gates[h:h + 1, 2:3] * o_w
        ob_ref[0, h:h + 1, :] = mix.astype(ob_ref.dtype)


def nsa_sample_att(q8, gates, oc, kv03, kv45, wcache, pages, page_table, sel_idx, rel_bias):
    bd, nh, hd = q8.shape
    n_pages = page_table.shape[1]
    past = n_pages * PAGE_SIZE
    wb = wcache.shape[1]
    kvd = NSA_KV_HEADS * hd
    n_sel = sel_idx.shape[2]
    nsb = -(-(past + 1) // SEL_BLOCK)
    wbp = _round_up(wb + 1, SUBLANES)
    g3 = gates[:, :3 * nh].reshape(bd, nh, 3)
    dist = past - (jnp.arange(nsb)[:, None] * SEL_BLOCK + jnp.arange(SEL_BLOCK)[None, :])
    bsel = jnp.zeros((nsb, SEL_BLOCK, LANES), F32).at[:, :, :nh].set(_bias_of(rel_bias, dist))
    bwin = jnp.zeros((wbp, LANES), F32).at[:, :nh].set(_bias_of(rel_bias, wb - jnp.arange(wbp)))
    blk = lambda n2, n3: pl.BlockSpec((1, n2, n3), lambda bi, pt, si: (bi, 0, 0))
    grid_spec = pltpu.PrefetchScalarGridSpec(
        num_scalar_prefetch=2, grid=(bd,),
        in_specs=[blk(nh, hd), blk(nh, 3), blk(nh, hd), blk(1, 4 * kvd), blk(1, 2 * kvd), blk(wb, 2 * kvd),
                  pl.BlockSpec(memory_space=pl.ANY),
                  pl.BlockSpec(bsel.shape, lambda bi, pt, si: (0, 0, 0)),
                  pl.BlockSpec(bwin.shape, lambda bi, pt, si: (0, 0))],
        out_specs=[blk(nh, hd), blk(wb, 2 * kvd)],
        scratch_shapes=[pltpu.VMEM((2, NSA_KV_HEADS * n_sel, SEL_BLOCK, 2 * kvd), F32),
                        pltpu.VMEM((wbp, 2 * kvd), F32), pltpu.SemaphoreType.DMA((2,))])
    ob, win = pl.pallas_call(
        functools.partial(_nsa_sample_att_kernel, n_pages=n_pages, n_sel=n_sel, past=past, wb=wb),
        grid_spec=grid_spec,
        out_shape=[jax.ShapeDtypeStruct((bd, nh, hd), BF16), jax.ShapeDtypeStruct((bd, wb, 2 * kvd), F32)],
        compiler_params=_cparams(("arbitrary",), VMEM_LIMIT), name="nsa_sample_att",
    )(page_table.reshape(-1), sel_idx.reshape(-1), q8, g3, oc, kv03.reshape(bd, 1, -1), kv45.reshape(bd, 1, -1),
      wcache, pages, bsel, bwin)
    return ob.reshape(bd, nh * hd), win


def kernel(x_prompt, x_sample, mem_prompt, cache_conv, cache_nsa_pages, cache_nsa_window, state_mlstm_c,
           state_mlstm_n, state_mlstm_m, cache_mem_kv, page_table, rel_bias, norm_mix, norm_xattn, norm_mem,
           norm_ffn, norm_final, w_in_even, w_out_even, conv_w, conv_b, conv_ln_g, conv_ln_b, nsa_cmp_pe,
           nsa_cmp_w, w_in_odd, mlstm_b_i, mlstm_b_f, mlstm_norm, w_out_odd, xattn_wq, xattn_wkv, xattn_wo,
           ffn_w_gu, ffn_w_dn, router_w, router_b, expert_w_gu, expert_w_dn):
    b, s, d = x_prompt.shape
    bd, td, _ = x_sample.shape
    assert td == 1, "the sample group decodes one token per sequence"
    depth = norm_mix.shape[0]
    mt = mem_prompt.shape[1]
    cc = conv_w.shape[2]
    hist = conv_w.shape[1] - 1
    wb = cache_nsa_window.shape[2]
    kvh, hd = NSA_KV_HEADS, NSA_HD
    n_pool = cache_nsa_pages.shape[1]
    assert s >= hist and s >= wb and s % Q_BLOCK == 0
    xp = x_prompt.reshape(b * s, d)
    xs = x_sample.reshape(bd, d)
    mem = mem_prompt.reshape(b * mt, d)
    pages_all = cache_nsa_pages.reshape(-1, PAGE_SIZE, 4 * kvh * hd)
    memkv_all = cache_mem_kv.reshape(depth * bd, mt, -1)
    bf = lambda a: a.astype(BF16)
    conv_p, conv_s, nsa_p, nsa_s, win_p, win_s = [], [], [], [], [], []
    mc_p, mc_s, mn_p, mn_s, mm_p, mm_s, memkv_p = [], [], [], [], [], [], []
    for l in range(depth):
        li = l // 2
        if l % 2 == 0:
            prm = prep_even(w_in_even[li], nsa_cmp_pe[li], nsa_cmp_w[li])
            w_out = bf(w_out_even[li])
            w_parts = [w_out[:cc], w_out[cc:]]
            conv_args = (conv_w[li], conv_b[li], conv_ln_g[li], conv_ln_b[li])
            glu, kv03, kv45, keys, qt, vt, gt = inproj_even(xp, norm_mix[l], prm['w_in'], cc, True)
            a_out = conv_prompt(glu, *conv_args, b, s)
            kc, vct = compress_prompt(kv03, prm['pe2'], prm['wbd'], b, s, _round_up(s // CMP_BLOCK, LANES))
            b_out = nsa_prompt(qt, gt, kc, vct, keys, vt, rel_bias, b, s)
            xp = outproj(xp, [a_out, b_out], w_parts)
            conv_p.append(glu.reshape(b, s, cc)[:, s - hist:])
            nsa_p.append(kv03.reshape(b, s, 4, kvh, hd))
            win_p.append(kv45.reshape(b, s, 2, kvh, hd)[:, s - wb:])
            glu, kv03, kv45, q, gates = inproj_even(xs, norm_mix[l], prm['w_in'], cc, False)
            a_out, conv_state = conv_sample(cache_conv[li], glu, *conv_args)
            q8 = q.reshape(bd, NSA_HEADS, hd)
            pt = page_table + li * n_pool
            o_c, sel_idx = nsa_sample_cmp(q8, pages_all, pt, prm['pe2'], prm['wbd'], rel_bias)
            b_out, win = nsa_sample_att(q8, gates, o_c, kv03, kv45, cache_nsa_window[li].reshape(bd, wb, -1),
                                        pages_all, pt, sel_idx, rel_bias)
            xs = outproj(xs, [a_out, b_out], w_parts)
            conv_s.append(conv_state)
            nsa_s.append(kv03.reshape(bd, 1, 4, kvh, hd))
            win_s.append(win.reshape(bd, wb, 2, kvh, hd))
        else:
            prm = prep_odd(w_in_odd[li], mlstm_b_i[li], mlstm_b_f[li])
            w_out = bf(w_out_odd[li])
            q, k, v, og, gc, gr = inproj_odd(xp, norm_mix[l], prm['w_in'], prm['bias'])
            hn, c_new, n_new, m_new = mlstm_prompt(q, k, v, og, gc, gr, mlstm_norm[li], b, s)
            xp = outproj(xp, [hn], [w_out])
            mc_p.append(c_new)
            mn_p.append(n_new)
            mm_p.append(m_new[:, :, 0])
            q, k, v, og, gc, gr = inproj_odd(xs, norm_mix[l], prm['w_in'], prm['bias'])
            hn, c_new, n_new, m_new = mlstm_sample(q, k, v, og, gc, mlstm_norm[li], state_mlstm_c[li],
                                                   state_mlstm_n[li], state_mlstm_m[li])
            xs = outproj(xs, [hn], [w_out])
            mc_s.append(c_new)
            mn_s.append(n_new)
            mm_s.append(m_new)
        wq, wo = bf(xattn_wq[l]), bf(xattn_wo[l])
        mkv, mkv_b = memkv(mem, norm_mem[l], bf(xattn_wkv[l]))
        memkv_p.append(mkv.reshape(b, mt, 2, X_HEADS, d // X_HEADS))
        xp = xattn_prompt(xp, norm_xattn[l], wq, mkv_b, wo, b, s)
        xs = xattn_sample(xs, norm_xattn[l], wq, memkv_all, wo, l)
        if l % 2 == 0:
            w_gu, w_dn = bf(ffn_w_gu[li]), bf(ffn_w_dn[li])
            xp = ffn(xp, norm_ffn[l], w_gu, w_dn)
            xs = ffn(xs, norm_ffn[l], w_gu, w_dn)
        else:
            e_gu, e_dn = bf(expert_w_gu[li]), bf(expert_w_dn[li])
            final_g = norm_final if l == depth - 1 else None
            comb, h, mask, counts = router(xp, norm_ffn[l], router_w[li], router_b[li])
            xp = moe_grouped(xp, h, comb, mask, counts, e_gu, e_dn, final_g)
            comb, h, _, _ = router(xs, norm_ffn[l], router_w[li], router_b[li])
            xs = moe(xs, h, comb, e_gu, e_dn, final_g)
    if depth % 2:
        xp, xs = rmsnorm(xp, norm_final), rmsnorm(xs, norm_final)
    y_prompt = xp.reshape(b, s, d)
    y_sample = xs.reshape(bd, 1, d)
    return (y_prompt, y_sample, jnp.stack(conv_p), jnp.stack(conv_s), jnp.stack(nsa_p), jnp.stack(nsa_s),
            jnp.stack(win_p), jnp.stack(win_s), jnp.stack(mc_p), jnp.stack(mc_s), jnp.stack(mn_p),
            jnp.stack(mn_s), jnp.stack(mm_p), jnp.stack(mm_s), jnp.stack(memkv_p))
```

```python
import functools
import math

import jax
import jax.numpy as jnp
import numpy as np
from jax import lax
from jax.experimental import pallas as pl
from jax.experimental.pallas import tpu as pltpu

F32 = jnp.float32
BF16 = jnp.bfloat16

PAGE_SIZE = 128
CONV_WIDTH = 31
NSA_HEADS = 8
NSA_KV_HEADS = 2
NSA_GROUP = NSA_HEADS // NSA_KV_HEADS
NSA_HD = 64
CMP_BLOCK = 32
SEL_BLOCK = 64
SEL_RATIO = SEL_BLOCK // CMP_BLOCK
TOP_N = 16
WINDOW = 512
Q_BLOCK = 128
FORCE_SCORE = 1.0e4
NUM_BUCKETS = 32
MAX_DISTANCE = 1024
MLSTM_HEADS = 4
X_HEADS = 4
N_EXPERTS = 8
TOP_K = 2
RMS_EPS = 1e-6
LN_EPS = 1e-5
NEG = -1e30

LANES = 128
SUBLANES = 8
VMEM_LIMIT = 56 * 1024 * 1024
MLSTM_CHUNK = 256


def _cparams(sem, vmem=None):
    return pltpu.CompilerParams(dimension_semantics=sem, vmem_limit_bytes=vmem)


def _rms(x, g):
    return x * lax.rsqrt(jnp.mean(x * x, -1, keepdims=True) + RMS_EPS) * g


def _dot(a, b):
    return jnp.dot(a, b, preferred_element_type=F32)


def _dot_nt(a, b):
    return lax.dot_general(a, b, (((1,), (1,)), ((), ())), preferred_element_type=F32)


def _dot_tn(a, b):
    return lax.dot_general(a, b, (((0,), (0,)), ((), ())), preferred_element_type=F32)


def _full(shape):
    n = len(shape)
    return pl.BlockSpec(shape, lambda *_: (0,) * n)


def _row_tile(m, pref):
    t = min(pref, m)
    while m % t:
        t //= 2
    return t


def _rmsnorm_kernel(x_ref, g_ref, o_ref):
    o_ref[...] = _rms(x_ref[...], g_ref[...])


def rmsnorm(x, g):
    m, d = x.shape
    tm = _row_tile(m, 1024)
    return pl.pallas_call(
        _rmsnorm_kernel, grid=(m // tm,),
        in_specs=[pl.BlockSpec((tm, d), lambda i: (i, 0)), _full((1, d))],
        out_specs=pl.BlockSpec((tm, d), lambda i: (i, 0)),
        out_shape=jax.ShapeDtypeStruct((m, d), F32),
        compiler_params=_cparams(("parallel",)), name="rmsnorm",
    )(x, g.reshape(1, d))


def _outproj_kernel(*refs, n_in):
    x_ref = refs[0]
    a_refs = refs[1:1 + n_in]
    w_refs = refs[1 + n_in:1 + 2 * n_in]
    o_ref = refs[1 + 2 * n_in]
    acc = x_ref[...]
    for a_ref, w_ref in zip(a_refs, w_refs):
        acc = acc + _dot(a_ref[...], w_ref[...])
    o_ref[...] = acc


def outproj(x, acts, ws):
    m, d = x.shape
    tm = _row_tile(m, 512)
    n_in = len(acts)
    in_specs = [pl.BlockSpec((tm, d), lambda i: (i, 0))]
    in_specs += [pl.BlockSpec((tm, a.shape[1]), lambda i: (i, 0)) for a in acts]
    in_specs += [_full(w.shape) for w in ws]
    return pl.pallas_call(
        functools.partial(_outproj_kernel, n_in=n_in), grid=(m // tm,),
        in_specs=in_specs, out_specs=pl.BlockSpec((tm, d), lambda i: (i, 0)),
        out_shape=jax.ShapeDtypeStruct((m, d), F32),
        compiler_params=_cparams(("parallel",)), name="outproj",
    )(x, *acts, *ws)


def _inproj_even_kernel(x_ref, g_ref, w_ref, glu_ref, *rest, cc, qd, kvd, tiles):
    xn = _rms(x_ref[...], g_ref[...]).astype(BF16)

    def mm(lo, hi):
        return _dot(xn, w_ref[:, lo:hi])

    o = 0
    a = mm(o, o + cc)
    b = mm(o + cc, o + 2 * cc)
    glu_ref[...] = a * jax.nn.sigmoid(b)
    o += 2 * cc
    q = mm(o, o + qd) * (NSA_HD ** -0.5)
    o += qd
    kv03 = mm(o, o + 4 * kvd)
    o += 4 * kvd
    kv45 = mm(o, o + 2 * kvd)
    o += 2 * kvd
    gates = jax.nn.sigmoid(mm(o, o + LANES))
    if tiles == 0:
        kv03_ref, kv45_ref, q_ref, gate_ref = rest
        kv03_ref[...] = kv03
        kv45_ref[...] = kv45
        q_ref[...] = q.astype(BF16)
        gate_ref[...] = gates
        return
    keys_ref, qt_ref, vt_ref, gt_ref, kvt_ref = rest
    kvt_ref[0] = jnp.concatenate([kv03, kv45], axis=1).T
    keys_ref[...] = jnp.concatenate([kv03[:, 2 * kvd:3 * kvd], kv45[:, 0:kvd]], axis=1).astype(BF16)
    vals = jnp.concatenate([kv03[:, 3 * kvd:4 * kvd], kv45[:, kvd:2 * kvd]], axis=1)
    for j in range(tiles):
        rows = slice(j * Q_BLOCK, (j + 1) * Q_BLOCK)
        qt_ref[j] = q[rows, :].T.astype(BF16)
        vt_ref[j] = vals[rows, :].T.astype(BF16)
        gt_ref[j] = gates[rows, :].T


def inproj_even(x, g, w_pad, cc, seq=None):
    m, d = x.shape
    qd = NSA_HEADS * NSA_HD
    kvd = NSA_KV_HEADS * NSA_HD
    tm = _row_tile(m, 256)
    row = lambda n: pl.BlockSpec((tm, n), lambda i: (i, 0))
    out_specs = [row(cc)]
    out_shape = [jax.ShapeDtypeStruct((m, cc), F32)]
    transposed = seq is not None
    tiles = tm // Q_BLOCK if transposed else 0
    if transposed:
        assert tm % Q_BLOCK == 0 and seq % tm == 0
        per_seq = seq // tm
        tile = lambda n: pl.BlockSpec((tiles, n, Q_BLOCK), lambda i: (i, 0, 0))
        out_specs += [row(2 * kvd), tile(qd), tile(2 * kvd), tile(LANES),
                      pl.BlockSpec((1, 6 * kvd, tm), lambda i: (i // per_seq, 0, i % per_seq))]
        out_shape += [jax.ShapeDtypeStruct((m, 2 * kvd), BF16),
                      jax.ShapeDtypeStruct((m // Q_BLOCK, qd, Q_BLOCK), BF16),
                      jax.ShapeDtypeStruct((m // Q_BLOCK, 2 * kvd, Q_BLOCK), BF16),
                      jax.ShapeDtypeStruct((m // Q_BLOCK, LANES, Q_BLOCK), F32),
                      jax.ShapeDtypeStruct((m // seq, 6 * kvd, seq), F32)]
    else:
        out_specs += [row(4 * kvd), row(2 * kvd), row(qd), row(LANES)]
        out_shape += [jax.ShapeDtypeStruct((m, 4 * kvd), F32), jax.ShapeDtypeStruct((m, 2 * kvd), F32),
                      jax.ShapeDtypeStruct((m, qd), BF16), jax.ShapeDtypeStruct((m, LANES), F32)]
    return pl.pallas_call(
        functools.partial(_inproj_even_kernel, cc=cc, qd=qd, kvd=kvd, tiles=tiles), grid=(m // tm,),
        in_specs=[row(d), _full((1, d)), _full(w_pad.shape)],
        out_specs=out_specs, out_shape=out_shape,
        compiler_params=_cparams(("parallel",)), name="inproj_even",
    )(x, g.reshape(1, d), w_pad)


def _conv_post(y, lg, lb):
    mu = jnp.mean(y, -1, keepdims=True)
    var = jnp.mean(jnp.square(y - mu), -1, keepdims=True)
    yn = (y - mu) * lax.rsqrt(var + LN_EPS) * lg + lb
    return yn * jax.nn.sigmoid(yn)


CONV_SUB = 64
CONV_PAD = 32


def _conv_prompt_kernel(glu_ref, cw_ref, cb_ref, lg_ref, lb_ref, o_ref, ext_ref, y_ref, *, ts, s):
    i = pl.program_id(1)
    c = glu_ref.shape[-1]

    @pl.when(i == 0)
    def _():
        ext_ref[0:CONV_PAD, :] = jnp.zeros((CONV_PAD, c), F32)
        ext_ref[CONV_PAD:CONV_PAD + s, :] = glu_ref[0]
        ext_ref[CONV_PAD + s:CONV_PAD + s + SUBLANES, :] = jnp.zeros((SUBLANES, c), F32)

    lead = CONV_PAD - (CONV_WIDTH - 1)
    span = CONV_SUB + CONV_PAD

    def sub(j, carry):
        r0 = pl.multiple_of(i * ts + j * CONV_SUB, CONV_SUB)
        for c0 in range(0, c, LANES):
            xw = ext_ref[pl.ds(r0, span + SUBLANES), c0:c0 + LANES]
            acc = jnp.zeros((CONV_SUB, LANES), F32) + cb_ref[:, c0:c0 + LANES]
            for r in range(SUBLANES):
                xr = xw[r:r + span, :]
                for a in range(span // SUBLANES):
                    w = SUBLANES * a + r - lead
                    if 0 <= w < CONV_WIDTH:
                        acc = acc + xr[SUBLANES * a:SUBLANES * a + CONV_SUB, :] * cw_ref[w:w + 1, c0:c0 + LANES]
            y_ref[:, c0:c0 + LANES] = acc
        o_ref[0, pl.ds(pl.multiple_of(j * CONV_SUB, CONV_SUB), CONV_SUB), :] = _conv_post(
            y_ref[...], lg_ref[...], lb_ref[...]).astype(o_ref.dtype)
        return carry

    lax.fori_loop(0, ts // CONV_SUB, sub, 0)


def conv_prompt(glu, cw, cb, lg, lb, b, s):
    c = glu.shape[-1]
    ts = _row_tile(s, 256)
    vec = lambda a: a.reshape(1, c)
    out = pl.pallas_call(
        functools.partial(_conv_prompt_kernel, ts=ts, s=s), grid=(b, s // ts),
        in_specs=[pl.BlockSpec((1, s, c), lambda bi, i: (bi, 0, 0)), _full((CONV_WIDTH, c)),
                  _full((1, c)), _full((1, c)), _full((1, c))],
        out_specs=pl.BlockSpec((1, ts, c), lambda bi, i: (bi, i, 0)),
        out_shape=jax.ShapeDtypeStruct((b, s, c), BF16),
        scratch_shapes=[pltpu.VMEM((CONV_PAD + s + SUBLANES, c), F32), pltpu.VMEM((CONV_SUB, c), F32)],
        compiler_params=_cparams(("parallel", "arbitrary")), name="conv_prompt",
    )(glu.reshape(b, s, c), cw, vec(cb), vec(lg), vec(lb))
    return out.reshape(b * s, c)


def _conv_sample_kernel(cache_ref, glu_ref, cw_ref, cb_ref, lg_ref, lb_ref, o_ref, st_ref):
    hist = CONV_WIDTH - 1
    cache = cache_ref[...]
    glu = glu_ref[...]
    y = jnp.sum(cache * cw_ref[0:hist, :][None], axis=1) + glu * cw_ref[hist:hist + 1, :] + cb_ref[...]
    o_ref[...] = _conv_post(y, lg_ref[...], lb_ref[...]).astype(o_ref.dtype)
    st_ref[:, 0:hist - 1, :] = cache[:, 1:hist, :]
    st_ref[:, hist - 1:hist, :] = glu[:, None, :]


def conv_sample(cache, glu, cw, cb, lg, lb):
    bd, hist, c = cache.shape
    vec = lambda a: a.reshape(1, c)
    return pl.pallas_call(
        _conv_sample_kernel,
        out_shape=[jax.ShapeDtypeStruct((bd, c), BF16), jax.ShapeDtypeStruct((bd, hist, c), F32)],
        name="conv_sample",
    )(cache, glu, cw, vec(cb), vec(lg), vec(lb))


def _rel_bucket(dist):
    n = jnp.maximum(dist, 0)
    max_exact = NUM_BUCKETS // 2
    nf = jnp.maximum(n, 1).astype(F32)
    large = max_exact + (jnp.log(nf / max_exact) / math.log(MAX_DISTANCE / max_exact)
                         * (NUM_BUCKETS - max_exact)).astype(jnp.int32)
    large = jnp.minimum(large, NUM_BUCKETS - 1)
    return jnp.where(n < max_exact, n, large)


def _bias_of(rel_bias, dist):
    bucket = _rel_bucket(dist)[..., None]
    out = jnp.zeros(bucket.shape[:-1] + (rel_bias.shape[1],), F32)
    for k in range(NUM_BUCKETS):
        out = jnp.where(bucket == k, rel_bias[k].astype(F32), out)
    return out


def _compress_accumulate(load_rows, pe_ref, w_ref, nc):
    accs = []
    for slot in range(2):
        acc = jnp.zeros((nc, w_ref.shape[-1]), F32)
        for j in range(CMP_BLOCK):
            xj = load_rows(slot, j) + pe_ref[slot, j:j + 1, :]
            acc = acc + _dot(xj.astype(BF16), w_ref[slot, j])
        accs.append(acc)
    return jnp.concatenate(accs, axis=1)


def _file_rows(tile_t, xrow, sl, page):
    rows = tile_t.T
    per_page = PAGE_SIZE // CMP_BLOCK
    for cl in range(per_page):
        for a in range(CMP_BLOCK // SUBLANES):
            r0 = cl * CMP_BLOCK + a * SUBLANES
            dst = pl.multiple_of((page * per_page + cl) * SUBLANES, SUBLANES)
            xrow[sl, a, pl.ds(dst, SUBLANES), :] = rows[r0:r0 + SUBLANES, :]


def _compress_filed(xrow, pe_ref, w_ref, nc):
    return _compress_accumulate(
        lambda sl, j: xrow[sl, j // SUBLANES, pl.ds(j % SUBLANES, nc, stride=SUBLANES), :], pe_ref, w_ref, nc)


def _compress_prompt_kernel(x_ref, pe_ref, w_ref, kc_ref, vct_ref, xrow, *, nc, ncp, kvd):
    for p in range(x_ref.shape[2] // PAGE_SIZE):
        for sl in range(2):
            _file_rows(x_ref[0, sl * kvd:(sl + 1) * kvd, p * PAGE_SIZE:(p + 1) * PAGE_SIZE], xrow, sl, p)
    acc = _compress_filed(xrow, pe_ref, w_ref, nc)
    if ncp > nc:
        acc = jnp.concatenate([acc, jnp.zeros((ncp - nc, 2 * kvd), F32)], axis=0)
    kc_ref[0] = acc[:, 0:kvd].astype(BF16)
    vct_ref[0] = acc[:, kvd:2 * kvd].T.astype(BF16)


def compress_prompt(kv_t, pe2, wbd, ncp):
    b, _, s = kv_t.shape
    assert s % PAGE_SIZE == 0
    nc = s // CMP_BLOCK
    kvd = wbd.shape[-1]
    return pl.pallas_call(
        functools.partial(_compress_prompt_kernel, nc=nc, ncp=ncp, kvd=kvd), grid=(b,),
        in_specs=[pl.BlockSpec((1, 2 * kvd, s), lambda bi: (bi, 0, 0)), _full(pe2.shape), _full(wbd.shape)],
        out_specs=[pl.BlockSpec((1, ncp, kvd), lambda bi: (bi, 0, 0)), pl.BlockSpec((1, kvd, ncp), lambda bi: (bi, 0, 0))],
        out_shape=[jax.ShapeDtypeStruct((b, ncp, kvd), BF16), jax.ShapeDtypeStruct((b, kvd, ncp), BF16)],
        scratch_shapes=[pltpu.VMEM((2, CMP_BLOCK // SUBLANES, nc * SUBLANES, kvd), F32)],
        compiler_params=_cparams(("parallel",)), name="compress_prompt",
    )(kv_t, pe2, wbd)


def _select_blocks(score, n_sel, n_cand):
    lane = lax.broadcasted_iota(jnp.int32, score.shape, 1)
    rank = jnp.zeros(score.shape, F32)
    for i in range(n_cand):
        col = score[:, 2 * i:2 * i + 1]
        beats = (col > score) | ((col == score) & (lane > 2 * i))
        rank = rank + beats.astype(F32)
    is_cand = ((lane % 2) == 0) & (lane < 2 * n_cand)
    return (is_cand & (rank < n_sel) & (score >= 0)).astype(F32), rank


def _pair_sum(imp):
    n = imp.shape[1]
    return imp + pltpu.roll(imp, n - 1, 1)


def _rank_rows(score, n_sel, n_cand):
    blk = lax.broadcasted_iota(jnp.int32, score.shape, 0)
    rank = jnp.zeros(score.shape, F32)
    for i in range(n_cand):
        row = score[i:i + 1, :]
        beats = (row > score) | ((row == score) & (blk > i))
        rank = rank + beats.astype(F32)
    return ((rank < n_sel) & (score >= 0)).astype(F32)


def _nsa_prompt_kernel(qt_ref, gt_ref, kc_ref, vct_ref, keys_ref, vt_ref, biasc_ref, btile_ref, o_ref,
                       qt_scr, oc_scr, acc_s, acc_w, imp_scr, sel_scr, out_scr, *, nc, nsb, n_sel):
    qi = pl.program_id(1)
    g, hd, kvh, qb = NSA_GROUP, NSA_HD, NSA_KV_HEADS, Q_BLOCK
    kvd = kvh * hd
    ncp = kc_ref.shape[1]
    nsbp = sel_scr.shape[1]
    q_pos = qi * qb + lax.broadcasted_iota(jnp.int32, (1, qb), 1)
    key_row = lax.broadcasted_iota(jnp.int32, (qb, 1), 0)
    c_row = lax.broadcasted_iota(jnp.int32, (ncp, 1), 0)
    mask_c = (q_pos >= c_row * CMP_BLOCK + (CMP_BLOCK - 1)) & (c_row < nc)
    blk = lax.broadcasted_iota(jnp.int32, (nsbp, 1), 0)
    cur = q_pos // SEL_BLOCK
    forced = (blk == 0) | (blk == cur) | (blk == cur - 1)
    zeros = jnp.zeros((hd, qb), BF16)
    for k in range(kvh):
        for gi in range(g):
            h = k * g + gi
            parts = [zeros] * kvh
            parts[k] = qt_ref[0, h * hd:(h + 1) * hd, :]
            qt_scr[k, :, gi * qb:(gi + 1) * qb] = jnp.concatenate(parts, axis=0)
        s_c = _dot(kc_ref[0], qt_scr[k])
        imp = jnp.zeros((ncp, qb), F32)
        probs = []
        for gi in range(g):
            s = jnp.where(mask_c, s_c[:, gi * qb:(gi + 1) * qb] + biasc_ref[k * g + gi], NEG)
            e = jnp.where(mask_c, jnp.exp(s - jnp.max(s, 0, keepdims=True)), 0.0)
            den = jnp.sum(e, 0, keepdims=True)
            p = e / jnp.where(den > 0, den, 1.0)
            imp = imp + p
            probs.append(p.astype(BF16))
        oc_scr[k] = _dot(vct_ref[0, k * hd:(k + 1) * hd, :], jnp.concatenate(probs, axis=1))
        imp_scr[...] = imp + pltpu.roll(imp, ncp - 1, 0)
        cand = imp_scr[pl.ds(0, nsbp, stride=SEL_RATIO), :]
        score = jnp.where(forced, FORCE_SCORE, jnp.where(blk <= cur, cand, -1.0))
        sel_scr[k] = _rank_rows(jnp.where(blk < nsb, score, -2.0), n_sel, nsb)

    per_tile = qb // SEL_BLOCK
    n_tiles = keys_ref.shape[1] // qb
    first = ([jnp.full((1, qb), NEG, F32)] * g, [jnp.zeros((1, qb), F32)] * g)

    def tile_step(tiles, carry, key_col, val_row, acc_ref, window):
        kts = [jnp.clip(kt, 0, n_tiles - 1) for kt, _ in tiles]
        starts = [pl.multiple_of(kt * qb, qb) for kt in kts]
        k_t = jnp.concatenate([keys_ref[0, pl.ds(r0, qb), key_col:key_col + kvd] for r0 in starts], axis=0)
        dist = jnp.concatenate([jnp.where(active, q_pos - (r0 + key_row), -1)
                                for r0, (_, active) in zip(starts, tiles)], axis=0)
        in_range = dist >= 0
        scores = [_dot(k_t, qt_scr[k]) for k in range(kvh)]
        new, updates = [], []
        for k in range(kvh):
            if window:
                valid = in_range & (dist < WINDOW)
            else:
                pieces = []
                for kt in kts:
                    chosen = jnp.zeros((qb, qb), F32)
                    for j in range(per_tile):
                        row = sel_scr[k, pl.ds(per_tile * kt + j, 1), :]
                        chosen = jnp.where(key_row // SEL_BLOCK == j, row, chosen)
                    pieces.append(chosen)
                valid = in_range & (jnp.concatenate(pieces, axis=0) > 0.5)
            ms, ls = carry[k]
            ms2, ls2, alphas, probs = [], [], [], []
            for gi in range(g):
                bias = jnp.concatenate([btile_ref[jnp.maximum(qi - kt, 0), k * g + gi] for kt in kts], axis=0)
                s = jnp.where(valid, scores[k][:, gi * qb:(gi + 1) * qb] + bias, NEG)
                m_new = jnp.maximum(ms[gi], jnp.max(s, 0, keepdims=True))
                alpha = jnp.exp(ms[gi] - m_new)
                p = jnp.exp(s - jnp.where(m_new == NEG, 0.0, m_new))
                ms2.append(m_new)
                ls2.append(alpha * ls[gi] + jnp.sum(p, 0, keepdims=True))
                alphas.append(alpha)
                probs.append(p.astype(BF16))
            new.append((ms2, ls2))
            updates.append((jnp.concatenate(alphas, axis=1), jnp.concatenate(probs, axis=1)))
        for k, (alpha, prob) in enumerate(updates):
            v_t = jnp.concatenate([vt_ref[kt, val_row + k * hd:val_row + (k + 1) * hd, :] for kt in kts], axis=1)
            acc_ref[k] = acc_ref[k] * alpha + _dot(v_t, prob)
        return tuple(new)

    acc_s[...] = jnp.zeros_like(acc_s)
    acc_w[...] = jnp.zeros_like(acc_w)
    sel_args = dict(key_col=0, val_row=0, acc_ref=acc_s, window=False)
    win_args = dict(key_col=kvd, val_row=kvd, acc_ref=acc_w, window=True)

    def sel_pair(i, carry):
        return tile_step([(2 * i, 2 * i <= qi), (2 * i + 1, 2 * i + 1 <= qi)], carry, **sel_args)

    stat_s = lax.fori_loop(0, qi // 2 + 1, sel_pair, (first,) * kvh)
    stat_w = (first,) * kvh
    win_tiles = [(qi - j, qi - j >= 0) for j in range(WINDOW // qb, -1, -1)]
    for j in range(0, len(win_tiles), 2):
        stat_w = tile_step(win_tiles[j:j + 2], stat_w, **win_args)
    gt = gt_ref[0]
    for k in range(kvh):
        for gi in range(g):
            h = k * g + gi
            cols = slice(gi * qb, (gi + 1) * qb)
            l_s, l_w = stat_s[k][1][gi], stat_w[k][1][gi]
            o_s = acc_s[k, :, cols] / jnp.where(l_s > 0, l_s, 1.0)
            o_w = acc_w[k, :, cols] / jnp.where(l_w > 0, l_w, 1.0)
            out_scr[h * hd:(h + 1) * hd, :] = (gt[3 * h:3 * h + 1, :] * oc_scr[k, :, cols]
                                               + gt[3 * h + 1:3 * h + 2, :] * o_s + gt[3 * h + 2:3 * h + 3, :] * o_w)
    o_ref[0] = out_scr[...].T.astype(o_ref.dtype)


def nsa_prompt(qt, gt, kc, vct, keys, vt, rel_bias, b, s):
    qb = Q_BLOCK
    nq = s // qb
    nc = s // CMP_BLOCK
    ncp = kc.shape[1]
    nsb = s // SEL_BLOCK
    nsbp = _round_up(nsb, SUBLANES)
    assert SEL_RATIO * nsbp <= ncp
    n_sel = min(TOP_N, nsb)
    nh = NSA_HEADS
    hq = qt.shape[1]
    kvd = kc.shape[2]
    glanes = NSA_GROUP * qb
    cend = jnp.arange(ncp)[:, None] * CMP_BLOCK + (CMP_BLOCK - 1)
    biasc = jnp.transpose(_bias_of(rel_bias, jnp.arange(s)[None, :] - cend), (2, 0, 1))
    r = jnp.arange(qb)
    dist = jnp.arange(nq)[:, None, None] * qb + r[None, None, :] - r[None, :, None]
    btile = jnp.transpose(_bias_of(rel_bias, dist), (0, 3, 1, 2))
    out = pl.pallas_call(
        functools.partial(_nsa_prompt_kernel, nc=nc, nsb=nsb, n_sel=n_sel), grid=(b, nq),
        in_specs=[pl.BlockSpec((1, hq, qb), lambda bi, i: (bi * nq + i, 0, 0)),
                  pl.BlockSpec((1, LANES, qb), lambda bi, i: (bi * nq + i, 0, 0)),
                  pl.BlockSpec((1, ncp, kvd), lambda bi, i: (bi, 0, 0)),
                  pl.BlockSpec((1, kvd, ncp), lambda bi, i: (bi, 0, 0)),
                  pl.BlockSpec((1, s, 2 * kvd), lambda bi, i: (bi, 0, 0)),
                  pl.BlockSpec((nq, 2 * kvd, qb), lambda bi, i: (bi, 0, 0)),
                  pl.BlockSpec((nh, ncp, qb), lambda bi, i: (0, 0, i)),
                  _full(btile.shape)],
        out_specs=pl.BlockSpec((1, qb, hq), lambda bi, i: (bi, i, 0)),
        out_shape=jax.ShapeDtypeStruct((b, s, hq), BF16),
        scratch_shapes=[pltpu.VMEM((NSA_KV_HEADS, kvd, glanes), BF16), pltpu.VMEM((NSA_KV_HEADS, NSA_HD, glanes), F32),
                        pltpu.VMEM((NSA_KV_HEADS, NSA_HD, glanes), F32), pltpu.VMEM((NSA_KV_HEADS, NSA_HD, glanes), F32),
                        pltpu.VMEM((ncp, qb), F32), pltpu.VMEM((NSA_KV_HEADS, nsbp, qb), F32),
                        pltpu.VMEM((hq, qb), F32)],
        compiler_params=_cparams(("parallel", "arbitrary"), VMEM_LIMIT), name="nsa_prompt",
    )(qt, gt, kc, vct, keys.reshape(b, s, 2 * kvd), vt, biasc, btile)
    return out.reshape(b * s, hq)


def _round_up(x, m):
    return (x + m - 1) // m * m


def prep_even(w_in, pe, wc):
    d, n = w_in.shape
    n_pad = _round_up(n - 3 * NSA_HEADS, LANES) + LANES
    w_pad = jnp.zeros((d, n_pad), BF16).at[:, :n].set(w_in.astype(BF16))
    hd = NSA_HD
    pe2 = jnp.tile(pe, (1, 1, NSA_KV_HEADS))
    wbd = jnp.zeros((2, CMP_BLOCK, NSA_KV_HEADS * hd, NSA_KV_HEADS * hd), F32)
    for i in range(NSA_KV_HEADS):
        wbd = wbd.at[:, :, i * hd:(i + 1) * hd, i * hd:(i + 1) * hd].set(wc)
    return dict(w_in=w_pad, pe2=pe2, wbd=wbd.astype(BF16))


def _memkv_kernel(x_ref, g_ref, w_ref, o_ref, ob_ref):
    y = _dot(_rms(x_ref[...], g_ref[...]).astype(BF16), w_ref[...])
    o_ref[...] = y
    ob_ref[...] = y.astype(BF16)


def memkv(mem, g, w):
    m, d = mem.shape
    n = w.shape[1]
    tm = _row_tile(m, 256)
    return pl.pallas_call(
        _memkv_kernel, grid=(m // tm,),
        in_specs=[pl.BlockSpec((tm, d), lambda i: (i, 0)), _full((1, d)), _full(w.shape)],
        out_specs=[pl.BlockSpec((tm, n), lambda i: (i, 0))] * 2,
        out_shape=[jax.ShapeDtypeStruct((m, n), F32), jax.ShapeDtypeStruct((m, n), BF16)],
        compiler_params=_cparams(("parallel",)), name="memkv",
    )(mem, g.reshape(1, d), w)


def _xattn_core(q, kv, hd):
    nh = q.shape[1] // hd
    outs = []
    for h in range(nh):
        s = _dot_nt(q[:, h * hd:(h + 1) * hd], kv[:, h * hd:(h + 1) * hd])
        e = jnp.exp(s - jnp.max(s, -1, keepdims=True))
        p = e / jnp.sum(e, -1, keepdims=True)
        outs.append(_dot(p.astype(BF16), kv[:, (nh + h) * hd:(nh + h + 1) * hd]))
    return jnp.concatenate(outs, axis=1).astype(BF16)


def _xattn_prompt_kernel(x_ref, g_ref, wq_ref, kv_ref, wo_ref, o_ref, *, hd):
    x = x_ref[0]
    q = (_dot(_rms(x, g_ref[...]).astype(BF16), wq_ref[...]) * (hd ** -0.5)).astype(BF16)
    o = _xattn_core(q, kv_ref[0], hd)
    o_ref[0] = x + _dot(o, wo_ref[...])


def xattn_prompt(x, g, wq, kvb, wo, b, s):
    d = x.shape[1]
    mt = kvb.shape[0] // b
    tm = _row_tile(s, 512)
    out = pl.pallas_call(
        functools.partial(_xattn_prompt_kernel, hd=d // X_HEADS), grid=(b, s // tm),
        in_specs=[pl.BlockSpec((1, tm, d), lambda bi, i: (bi, i, 0)), _full((1, d)), _full(wq.shape),
                  pl.BlockSpec((1, mt, kvb.shape[1]), lambda bi, i: (bi, 0, 0)), _full(wo.shape)],
        out_specs=pl.BlockSpec((1, tm, d), lambda bi, i: (bi, i, 0)),
        out_shape=jax.ShapeDtypeStruct((b, s, d), F32),
        compiler_params=_cparams(("parallel", "parallel"), VMEM_LIMIT), name="xattn_prompt",
    )(x.reshape(b, s, d), g.reshape(1, d), wq, kvb.reshape(b, mt, -1), wo)
    return out.reshape(b * s, d)


def _xattn_sample_kernel(x_ref, g_ref, wq_ref, kv_ref, wo_ref, o_ref, q_scr, a_scr, *, hd):
    bi = pl.program_id(0)
    nb = pl.num_programs(0)
    nh = wq_ref.shape[1] // hd

    @pl.when(bi == 0)
    def _():
        q_scr[...] = _dot(_rms(x_ref[...], g_ref[...]).astype(BF16), wq_ref[...]) * (hd ** -0.5)

    q = jnp.broadcast_to(q_scr[pl.ds(bi, 1), :], (SUBLANES, q_scr.shape[1])).astype(BF16)
    kv = kv_ref[0].astype(BF16)
    a_scr[pl.ds(bi, 1), :] = _xattn_core(q, kv, hd)[0:1, :].astype(F32)

    @pl.when(bi == nb - 1)
    def _():
        o_ref[...] = x_ref[...] + _dot(a_scr[...].astype(BF16), wo_ref[...])


def xattn_sample(x, g, wq, kv_cache, wo, layer):
    bd, d = x.shape
    mt, n = kv_cache.shape[1:]
    return pl.pallas_call(
        functools.partial(_xattn_sample_kernel, hd=d // X_HEADS), grid=(bd,),
        in_specs=[_full((bd, d)), _full((1, d)), _full(wq.shape),
                  pl.BlockSpec((1, mt, n), lambda bi: (layer * bd + bi, 0, 0)), _full(wo.shape)],
        out_specs=_full((bd, d)),
        out_shape=jax.ShapeDtypeStruct((bd, d), F32),
        scratch_shapes=[pltpu.VMEM((bd, wq.shape[1]), F32), pltpu.VMEM((bd, wq.shape[1]), F32)],
        compiler_params=_cparams(("arbitrary",), VMEM_LIMIT), name="xattn_sample",
    )(x, g.reshape(1, d), wq, kv_cache, wo)


def _ffn_kernel(x_ref, g_ref, wg_ref, wu_ref, wd_ref, o_ref, h_scr, acc_scr):
    c = pl.program_id(1)

    @pl.when(c == 0)
    def _():
        h_scr[...] = _rms(x_ref[...], g_ref[...]).astype(BF16)
        acc_scr[...] = x_ref[...]

    h = h_scr[...]
    gate = _dot(h, wg_ref[...])
    up = _dot(h, wu_ref[...])
    act = (gate * jax.nn.sigmoid(gate) * up).astype(BF16)
    acc_scr[...] += _dot(act, wd_ref[...])

    @pl.when(c == pl.num_programs(1) - 1)
    def _():
        o_ref[...] = acc_scr[...]


def _ff_chunk(dff, pref):
    c = dff
    for n in range(1, dff // LANES + 1):
        if dff % n == 0 and (dff // n) % LANES == 0 and dff // n <= pref:
            c = dff // n
            break
    return c


def ffn(x, g, w_gu, w_dn):
    m, d = x.shape
    dff = w_dn.shape[0]
    tm = _row_tile(m, 512)
    fc = _ff_chunk(dff, 1408)
    nch = dff // fc
    return pl.pallas_call(
        _ffn_kernel, grid=(m // tm, nch),
        in_specs=[pl.BlockSpec((tm, d), lambda i, c: (i, 0)), _full((1, d)),
                  pl.BlockSpec((d, fc), lambda i, c: (0, c)),
                  pl.BlockSpec((d, fc), lambda i, c: (0, nch + c)),
                  pl.BlockSpec((fc, d), lambda i, c: (c, 0))],
        out_specs=pl.BlockSpec((tm, d), lambda i, c: (i, 0)),
        out_shape=jax.ShapeDtypeStruct((m, d), F32),
        scratch_shapes=[pltpu.VMEM((tm, d), BF16), pltpu.VMEM((tm, d), F32)],
        compiler_params=_cparams(("parallel", "arbitrary"), VMEM_LIMIT), name="ffn",
    )(x, g.reshape(1, d), w_gu, w_gu, w_dn)


def _router_kernel(x_ref, g_ref, w_ref, b_ref, comb_ref, h_ref, mask_ref, cnt_ref, *, ne):
    h = _rms(x_ref[...], g_ref[...]).astype(BF16)
    h_ref[...] = h
    logits = _dot(h, w_ref[...]) + b_ref[...]
    lane = lax.broadcasted_iota(jnp.int32, logits.shape, 1)
    logits = jnp.where(lane < ne, logits, -jnp.inf)
    v1 = jnp.max(logits, -1, keepdims=True)
    i1 = jnp.min(jnp.where(logits == v1, lane, LANES), -1, keepdims=True)
    rest = jnp.where(lane == i1, -jnp.inf, logits)
    v2 = jnp.max(rest, -1, keepdims=True)
    i2 = jnp.min(jnp.where(rest == v2, lane, LANES), -1, keepdims=True)
    e2 = jnp.exp(v2 - v1)
    den = 1.0 + e2
    comb_ref[...] = jnp.where(lane == i1, 1.0 / den, 0.0) + jnp.where(lane == i2, e2 / den, 0.0)
    chosen = jnp.where((lane == i1) | (lane == i2), 1.0, 0.0)
    mask_ref[...] = chosen.astype(BF16)

    @pl.when(pl.program_id(0) == 0)
    def _():
        cnt_ref[...] = jnp.zeros_like(cnt_ref)

    cnt_ref[0:1, :] += jnp.sum(chosen, axis=0, keepdims=True)


def router(x, g, w_r, b_r):
    m, d = x.shape
    ne = w_r.shape[1]
    w_pad = jnp.zeros((d, LANES), BF16).at[:, :ne].set(w_r.astype(BF16))
    b_pad = jnp.zeros((1, LANES), F32).at[0, :ne].set(b_r.astype(F32))
    tm = _row_tile(m, 512)
    row = lambda n: pl.BlockSpec((tm, n), lambda i: (i, 0))
    return pl.pallas_call(
        functools.partial(_router_kernel, ne=ne), grid=(m // tm,),
        in_specs=[row(d), _full((1, d)), _full((d, LANES)), _full((1, LANES))],
        out_specs=[row(LANES), row(d), row(LANES), _full((SUBLANES, LANES))],
        out_shape=[jax.ShapeDtypeStruct((m, LANES), F32), jax.ShapeDtypeStruct((m, d), BF16),
                   jax.ShapeDtypeStruct((m, LANES), BF16), jax.ShapeDtypeStruct((SUBLANES, LANES), F32)],
        compiler_params=_cparams(("arbitrary",)), name="router",
    )(x, g.reshape(1, d), w_pad, b_pad)


def _residual_out(y, gain_ref, norm):
    return _rms(y, gain_ref[...]) if norm else y


def _moe_kernel(x_ref, h_ref, comb_ref, wg_ref, wu_ref, wd_ref, gain_ref, o_ref, acc_scr, *, norm):
    e = pl.program_id(1)

    @pl.when(e == 0)
    def _():
        acc_scr[...] = jnp.zeros_like(acc_scr)

    h = h_ref[...]
    gate = _dot(h, wg_ref[0])
    up = _dot(h, wu_ref[0])
    act = (gate * jax.nn.sigmoid(gate) * up).astype(BF16)
    y = _dot(act, wd_ref[0])
    comb = comb_ref[...]
    lane = lax.broadcasted_iota(jnp.int32, comb.shape, 1)
    acc_scr[...] += jnp.sum(jnp.where(lane == e, comb, 0.0), -1, keepdims=True) * y

    @pl.when(e == pl.num_programs(1) - 1)
    def _():
        o_ref[...] = _residual_out(x_ref[...] + acc_scr[...], gain_ref, norm)


def moe(x, h, comb, w_gu, w_dn, final_g=None):
    m, d = x.shape
    ne, dfe = w_dn.shape[:2]
    tm = _row_tile(m, 512)
    gain = jnp.ones((1, d), F32) if final_g is None else final_g.reshape(1, d)
    return pl.pallas_call(
        functools.partial(_moe_kernel, norm=final_g is not None), grid=(m // tm, ne),
        in_specs=[pl.BlockSpec((tm, d), lambda i, e: (i, 0)), pl.BlockSpec((tm, d), lambda i, e: (i, 0)),
                  pl.BlockSpec((tm, LANES), lambda i, e: (i, 0)),
                  pl.BlockSpec((1, d, dfe), lambda i, e: (e, 0, 0)),
                  pl.BlockSpec((1, d, dfe), lambda i, e: (e, 0, 1)),
                  pl.BlockSpec((1, dfe, d), lambda i, e: (e, 0, 0)), _full((1, d))],
        out_specs=pl.BlockSpec((tm, d), lambda i, e: (i, 0)),
        out_shape=jax.ShapeDtypeStruct((m, d), F32),
        scratch_shapes=[pltpu.VMEM((tm, d), F32)],
        compiler_params=_cparams(("parallel", "arbitrary"), VMEM_LIMIT), name="moe",
    )(x, h, comb, w_gu, w_gu, w_dn, gain)


MOE_TILE = 256


def _moe_pos_kernel(mask_ref, comb_ref, tri_ref, base_ref, post_ref, pos2_ref, wab_ref, stab_ref, run_scr, *, nep):
    sb = pl.program_id(0)
    nb = pl.num_programs(0)

    @pl.when(sb == 0)
    def _():
        run_scr[...] = jnp.zeros_like(run_scr)
        stab_ref[...] = jnp.zeros_like(stab_ref)

    a = mask_ref[...]
    af = a.astype(F32)
    start = base_ref[...] + run_scr[...]
    stab_ref[pl.ds(sb, 1), :] = start.astype(jnp.int32)
    rank = _dot(tri_ref[...], a)
    pos = jnp.where(af > 0, start + rank, -1.0)
    lane = lax.broadcasted_iota(jnp.int32, pos.shape, 1)
    first_e = jnp.min(jnp.where(af > 0, lane, LANES), -1, keepdims=True)
    last_e = jnp.max(jnp.where(af > 0, lane, -1), -1, keepdims=True)
    comb = comb_ref[...]
    w_a = jnp.sum(jnp.where(lane == first_e, comb, 0.0), -1, keepdims=True)
    w_b = jnp.sum(jnp.where(lane == last_e, comb, 0.0), -1, keepdims=True)
    wab_ref[...] = jnp.where(lane == 0, w_a, jnp.where(lane == 1, w_b, 0.0))
    pos_t = pos.T[0:nep, :]
    post_ref[0] = pos_t.astype(jnp.int32)
    row = lax.broadcasted_iota(jnp.int32, pos_t.shape, 0)
    first_r = jnp.min(jnp.where(pos_t >= 0, row, nep), 0, keepdims=True)
    last_r = jnp.max(jnp.where(pos_t >= 0, row, -1), 0, keepdims=True)
    pos_a = jnp.sum(jnp.where(row == first_r, pos_t, 0.0), 0, keepdims=True)
    pos_b = jnp.sum(jnp.where(row == last_r, pos_t, 0.0), 0, keepdims=True)
    pos2_ref[0] = jnp.where(row == 0, pos_a, jnp.where(row == 1, pos_b, 0.0)).astype(jnp.int32)
    run_scr[...] += jnp.sum(af, axis=0, keepdims=True)

    @pl.when(sb == nb - 1)
    def _():
        stab_ref[pl.ds(nb, 1), :] = (base_ref[...] + run_scr[...]).astype(jnp.int32)


def _moe_expert_kernel(te_ref, lo_ref, cnt_ref, ring_ref, nt_ref, h_ref, post_ref, wg_ref, wu_ref, wd_ref, y_ref,
                       hbuf, xg_scr, sem, *, t):
    i = pl.program_id(0)

    @pl.when(i >= nt_ref[0])
    def _():
        y_ref[...] = jnp.zeros_like(y_ref)

    nbuf = hbuf.shape[0]
    ahead = nbuf - 1

    def copy(sb, slot):
        return pltpu.make_async_copy(h_ref.at[pl.ds(pl.multiple_of(sb * t, t), t), :], hbuf.at[slot], sem.at[slot])

    def start_first(tile):
        for j in range(ahead):
            @pl.when(j < cnt_ref[tile])
            def _():
                copy(lo_ref[tile] + j, (ring_ref[tile] + j) % nbuf).start()

    @pl.when(i == 0)
    def _():
        start_first(0)

    @pl.when(i < nt_ref[0])
    def _():
        e = te_ref[i]
        lo = lo_ref[i]
        n = cnt_ref[i]
        ring = ring_ref[i]
        xg_scr[...] = jnp.zeros_like(xg_scr)
        row = i * t + lax.broadcasted_iota(jnp.int32, (t, 1), 0)

        def body(j, c):
            slot = (ring + j) % nbuf
            copy(lo + j, slot).wait()

            @pl.when(j + ahead < n)
            def _():
                copy(lo + j + ahead, (ring + j + ahead) % nbuf).start()

            src_pos = post_ref[lo + j, pl.ds(e, 1), :]
            onehot = jnp.where(src_pos == row, 1.0, 0.0).astype(BF16)
            xg_scr[...] += _dot(onehot, hbuf[slot])
            return c

        lax.fori_loop(0, n, body, 0)

        @pl.when(i + 1 < nt_ref[0])
        def _():
            start_first(i + 1)

        x = xg_scr[...].astype(BF16)
        gate = _dot(x, wg_ref[0])
        up = _dot(x, wu_ref[0])
        act = (gate * jax.nn.sigmoid(gate) * up).astype(BF16)
        y_ref[...] = _dot(act, wd_ref[0])


def _moe_combine_kernel(x_ref, wab_ref, pos_ref, nxt_ref, gain_ref, ys_ref, o_ref, ybuf, sem, *, t, norm):
    sb = pl.program_id(0)
    nb = pl.num_programs(0)

    def row_copy(p_ref, which, slot, tok):
        return pltpu.make_async_copy(ys_ref.at[pl.ds(p_ref[0, which, tok], 1), :],
                                     ybuf.at[slot, which, pl.ds(tok, 1), :], sem.at[slot])

    def start_all(p_ref, slot):
        def body(tok, c):
            row_copy(p_ref, 0, slot, tok).start()
            row_copy(p_ref, 1, slot, tok).start()
            return c
        lax.fori_loop(0, t, body, 0, unroll=8)

    @pl.when(sb == 0)
    def _():
        start_all(pos_ref, 0)

    @pl.when(sb + 1 < nb)
    def _():
        start_all(nxt_ref, (sb + 1) % 2)

    slot = sb % 2

    def wait_body(tok, c):
        row_copy(pos_ref, 0, slot, tok).wait()
        row_copy(pos_ref, 1, slot, tok).wait()
        return c
    lax.fori_loop(0, t, wait_body, 0, unroll=8)
    w = wab_ref[...]
    y = x_ref[...] + w[:, 0:1] * ybuf[slot, 0] + w[:, 1:2] * ybuf[slot, 1]
    o_ref[...] = _residual_out(y, gain_ref, norm)


def moe_grouped(x, h, comb, mask, counts, w_gu, w_dn, final_g=None):
    m, d = x.shape
    ne, dfe = w_dn.shape[:2]
    t = MOE_TILE
    assert m % t == 0
    nb = m // t
    nep = _round_up(ne, SUBLANES)
    nbp = _round_up(nb + 1, SUBLANES)
    k_top = TOP_K
    nt_max = k_top * m // t + ne
    cnt = counts[0, :ne].astype(jnp.int32)
    cnt_pad = (cnt + t - 1) // t * t
    ends = jnp.cumsum(cnt_pad)
    base = ends - cnt_pad
    base_row = jnp.zeros((1, LANES), F32).at[0, :ne].set(base.astype(F32))
    idx = lax.broadcasted_iota(jnp.int32, (t, t), 0)
    tri = jnp.where(lax.broadcasted_iota(jnp.int32, (t, t), 1) < idx, 1.0, 0.0).astype(BF16)
    blk = lambda n2: pl.BlockSpec((t, n2), lambda i: (i, 0))
    post, pos2, wab, stab = pl.pallas_call(
        functools.partial(_moe_pos_kernel, nep=nep), grid=(nb,),
        in_specs=[blk(LANES), blk(LANES), _full((t, t)), _full((1, LANES))],
        out_specs=[pl.BlockSpec((1, nep, t), lambda i: (i, 0, 0)), pl.BlockSpec((1, nep, t), lambda i: (i, 0, 0)),
                   blk(LANES), _full((nbp, LANES))],
        out_shape=[jax.ShapeDtypeStruct((nb, nep, t), jnp.int32), jax.ShapeDtypeStruct((nb, nep, t), jnp.int32),
                   jax.ShapeDtypeStruct((m, LANES), F32), jax.ShapeDtypeStruct((nbp, LANES), jnp.int32)],
        scratch_shapes=[pltpu.VMEM((1, LANES), F32)],
        compiler_params=_cparams(("arbitrary",)), name="moe_positions",
    )(mask, comb, tri, base_row)
    r0 = jnp.arange(nt_max, dtype=jnp.int32) * t
    tile_e = jnp.minimum(jnp.sum(ends[None, :] <= r0[:, None], axis=1), ne - 1).astype(jnp.int32)
    n_tiles = (ends[-1] // t).astype(jnp.int32).reshape(1)
    s_e = stab[:nb + 1, :ne][:, tile_e]
    lo = jnp.sum(s_e[1:] <= r0[None, :], axis=0)
    hi = jnp.sum(s_e[:nb] < r0[None, :] + t, axis=0) - 1
    lo = jnp.clip(lo, 0, nb - 1).astype(jnp.int32)
    hi = jnp.clip(hi, lo, nb - 1).astype(jnp.int32)
    n_src = jnp.where(jnp.arange(nt_max) < n_tiles[0], hi - lo + 1, 0).astype(jnp.int32)
    n_ring = 3
    ring = ((jnp.cumsum(n_src) - n_src) % n_ring).astype(jnp.int32)
    w_spec = lambda shape, col: pl.BlockSpec(shape, lambda i, te, *_: (te[i], 0, col))
    grid_spec = pltpu.PrefetchScalarGridSpec(
        num_scalar_prefetch=5, grid=(nt_max,),
        in_specs=[pl.BlockSpec(memory_space=pl.ANY),
                  pl.BlockSpec((nb, nep, t), lambda i, *_: (0, 0, 0)),
                  w_spec((1, d, dfe), 0), w_spec((1, d, dfe), 1), w_spec((1, dfe, d), 0)],
        out_specs=pl.BlockSpec((t, d), lambda i, *_: (i, 0)),
        scratch_shapes=[pltpu.VMEM((n_ring, t, d), BF16), pltpu.VMEM((t, d), F32),
                        pltpu.SemaphoreType.DMA((n_ring,))])
    ys = pl.pallas_call(
        functools.partial(_moe_expert_kernel, t=t), grid_spec=grid_spec,
        out_shape=jax.ShapeDtypeStruct((nt_max * t, d), F32),
        compiler_params=_cparams(("arbitrary",), VMEM_LIMIT), name="moe_experts",
    )(tile_e, lo, n_src, ring, n_tiles, h, post, w_gu, w_gu, w_dn)
    smem_blk = lambda f: pl.BlockSpec((1, nep, t), f, memory_space=pltpu.SMEM)
    gain = jnp.ones((1, d), F32) if final_g is None else final_g.reshape(1, d)
    return pl.pallas_call(
        functools.partial(_moe_combine_kernel, t=t, norm=final_g is not None), grid=(nb,),
        in_specs=[blk(d), blk(LANES), smem_blk(lambda i: (i, 0, 0)),
                  smem_blk(lambda i: (jnp.minimum(i + 1, nb - 1), 0, 0)), _full((1, d)),
                  pl.BlockSpec(memory_space=pl.ANY)],
        out_specs=blk(d),
        out_shape=jax.ShapeDtypeStruct((m, d), F32),
        scratch_shapes=[pltpu.VMEM((2, 2, t, d), F32), pltpu.SemaphoreType.DMA((2,))],
        compiler_params=_cparams(("arbitrary",), VMEM_LIMIT), name="moe_combine",
    )(x, wab, pos2, pos2, gain, ys)


def _inproj_odd_kernel(x_ref, g_ref, w_ref, bias_ref, q_ref, k_ref, v_ref, og_ref, gc_ref, gr_ref, *, hq, hv, nh):
    xn = _rms(x_ref[...], g_ref[...]).astype(BF16)

    def mm(lo, hi):
        return _dot(xn, w_ref[:, lo:hi])

    dk = hq // nh
    q_ref[...] = mm(0, hq).astype(BF16)
    k_ref[...] = (mm(hq, 2 * hq) * (dk ** -0.5)).astype(BF16)
    v_ref[...] = mm(2 * hq, 2 * hq + hv).astype(BF16)
    og_ref[...] = jax.nn.sigmoid(mm(2 * hq + hv, 2 * hq + 2 * hv))
    gi = mm(2 * hq + 2 * hv, 2 * hq + 2 * hv + LANES) + bias_ref[...]
    lane = lax.broadcasted_iota(jnp.int32, gi.shape, 1)
    gates = jnp.where(lane < nh, gi, jax.nn.log_sigmoid(gi))
    gc_ref[...] = gates
    gr_ref[...] = gates.T[0:SUBLANES, :]


def inproj_odd(x, g, w_pad, gate_bias):
    m, d = x.shape
    nh = MLSTM_HEADS
    hq = hv = d
    tm = _row_tile(m, 256)
    row = lambda n: pl.BlockSpec((tm, n), lambda i: (i, 0))
    outs = [(hq, BF16), (hq, BF16), (hv, BF16), (hv, F32), (LANES, F32)]
    if tm % LANES:
        gr_spec = _full((SUBLANES, m))
    else:
        gr_spec = pl.BlockSpec((SUBLANES, tm), lambda i: (0, i))
    return pl.pallas_call(
        functools.partial(_inproj_odd_kernel, hq=hq, hv=hv, nh=nh), grid=(m // tm,),
        in_specs=[row(d), _full((1, d)), _full(w_pad.shape), _full((1, LANES))],
        out_specs=[row(n) for n, _ in outs] + [gr_spec],
        out_shape=[jax.ShapeDtypeStruct((m, n), t) for n, t in outs] + [jax.ShapeDtypeStruct((SUBLANES, m), F32)],
        compiler_params=_cparams(("parallel",), VMEM_LIMIT), name="inproj_odd",
    )(x, g.reshape(1, d), w_pad, gate_bias)


def prep_odd(w_in, b_i, b_f):
    d, n = w_in.shape
    n_pad = _round_up(n - 2 * MLSTM_HEADS, LANES) + LANES
    w_pad = jnp.zeros((d, n_pad), BF16).at[:, :n].set(w_in.astype(BF16))
    bias = jnp.zeros((1, LANES), F32).at[0, :2 * MLSTM_HEADS].set(jnp.concatenate([b_i, b_f]).astype(F32))
    return dict(w_in=w_pad, bias=bias)


def _mlstm_prompt_kernel(q_ref, k_ref, v_ref, og_ref, gc_ref, gr_ref, gain_ref, hn_ref, c_ref, n_ref, m_ref,
                         *, nh, dk, dv, ln):
    ci = pl.program_id(1)

    @pl.when(ci == 0)
    def _():
        c_ref[...] = jnp.zeros_like(c_ref)
        n_ref[...] = jnp.zeros_like(n_ref)
        m_ref[...] = jnp.full(m_ref.shape, NEG, F32)

    row = lax.broadcasted_iota(jnp.int32, (ln, ln), 0)
    col = lax.broadcasted_iota(jnp.int32, (ln, ln), 1)
    tri = row >= col
    gc = gc_ref[...]
    gr = gr_ref[...]
    for h in range(nh):
        q = q_ref[:, h * dk:(h + 1) * dk]
        k = k_ref[:, h * dk:(h + 1) * dk]
        v = v_ref[:, h * dv:(h + 1) * dv]
        ig_c, lf_c = gc[:, h:h + 1], gc[:, nh + h:nh + h + 1]
        ig_r, lf_r = gr[h:h + 1, :], gr[nh + h:nh + h + 1, :]
        b_c = jnp.sum(jnp.where(tri, lf_r, 0.0), axis=1, keepdims=True)
        b_r = jnp.sum(jnp.where(row <= col, lf_c, 0.0), axis=0, keepdims=True)
        m_prev = m_ref[0, h:h + 1, 0:1]
        c_prev = c_ref[0, h]
        n_prev = n_ref[0, h:h + 1, :]
        dmat = jnp.where(tri, b_c - b_r + ig_r, NEG)
        inter = b_c + m_prev
        mt = jnp.maximum(inter, jnp.max(dmat, -1, keepdims=True))
        wm = jnp.exp(dmat - mt)
        a = jnp.exp(inter - mt)
        wqk = wm * _dot_nt(q, k)
        num = a * _dot_nt(q, c_prev.astype(BF16)) + _dot(wqk.astype(BF16), v)
        den = a * jnp.sum(q.astype(F32) * n_prev, -1, keepdims=True) + jnp.sum(wqk, -1, keepdims=True)
        hh = num / jnp.maximum(jnp.abs(den), jnp.exp(-mt))
        b_end = b_c[ln - 1:ln, :]
        m_new = mt[ln - 1:ln, :]
        a_end = jnp.exp(b_end + m_prev - m_new)
        w_s = jnp.exp(b_end - b_c + ig_c - m_new)
        c_ref[0, h] = a_end * c_prev + _dot_tn((v.astype(F32) * w_s).astype(BF16), k)
        n_ref[0, h:h + 1, :] = a_end * n_prev + jnp.sum(w_s * k.astype(F32), axis=0, keepdims=True)
        m_ref[0, h:h + 1, :] = jnp.broadcast_to(m_new, (1, m_ref.shape[2]))
        hn = hh * lax.rsqrt(jnp.mean(hh * hh, -1, keepdims=True) + RMS_EPS)
        hn = hn * gain_ref[:, h * dv:(h + 1) * dv] * og_ref[:, h * dv:(h + 1) * dv]
        hn_ref[:, h * dv:(h + 1) * dv] = hn.astype(hn_ref.dtype)


def mlstm_prompt(q, k, v, og, gc, gr, gain, b, s):
    m, d = q.shape
    nh = MLSTM_HEADS
    dk = dv = d // nh
    ln = _row_tile(s, MLSTM_CHUNK)
    nch = s // ln
    row = lambda n: pl.BlockSpec((ln, n), lambda bi, ci: (bi * nch + ci, 0))
    return pl.pallas_call(
        functools.partial(_mlstm_prompt_kernel, nh=nh, dk=dk, dv=dv, ln=ln), grid=(b, nch),
        in_specs=[row(d), row(d), row(d), row(d), row(LANES),
                  pl.BlockSpec((SUBLANES, ln), lambda bi, ci: (0, bi * nch + ci)), _full((1, d))],
        out_specs=[row(d), pl.BlockSpec((1, nh, dv, dk), lambda bi, ci: (bi, 0, 0, 0)),
                   pl.BlockSpec((1, nh, dk), lambda bi, ci: (bi, 0, 0)),
                   pl.BlockSpec((1, nh, LANES), lambda bi, ci: (bi, 0, 0))],
        out_shape=[jax.ShapeDtypeStruct((m, d), BF16), jax.ShapeDtypeStruct((b, nh, dv, dk), F32),
                   jax.ShapeDtypeStruct((b, nh, dk), F32), jax.ShapeDtypeStruct((b, nh, LANES), F32)],
        compiler_params=_cparams(("parallel", "arbitrary"), VMEM_LIMIT), name="mlstm_prompt",
    )(q, k, v, og, gc, gr, gain.reshape(1, d))


def _mlstm_sample_kernel(q_ref, k_ref, v_ref, og_ref, g_ref, gain_ref, c_ref, n_ref, m_ref,
                         hn_ref, co_ref, no_ref, mo_ref, *, nh):
    row = lax.broadcasted_iota(jnp.int32, (SUBLANES, 1), 0)
    for h in range(nh):
        q = q_ref[0, h:h + 1, :]
        k = k_ref[0, h:h + 1, :]
        v = v_ref[0, h:h + 1, :].astype(F32)
        ig = g_ref[0, h:h + 1, 0:1]
        lf = g_ref[0, h:h + 1, 1:2]
        m_prev = m_ref[0, h:h + 1, :]
        c_prev = c_ref[0, h]
        n_prev = n_ref[0, h:h + 1, :]
        inter = lf + m_prev
        mt = jnp.maximum(inter, ig)
        wm = jnp.exp(ig - mt)
        a = jnp.exp(inter - mt)
        q8 = jnp.broadcast_to(q, (SUBLANES, q.shape[1]))
        cq = _dot_nt(q8, c_prev.astype(BF16))[0:1, :]
        wqk = wm * jnp.sum(q.astype(F32) * k.astype(F32), -1, keepdims=True)
        num = a * cq + wqk * v
        den = a * jnp.sum(n_prev * q.astype(F32), -1, keepdims=True) + wqk
        hh = num / jnp.maximum(jnp.abs(den), jnp.exp(-mt))
        v8 = jnp.where(row == 0, jnp.broadcast_to(v * wm, (SUBLANES, v.shape[1])), 0.0).astype(BF16)
        k8 = jnp.broadcast_to(k, (SUBLANES, k.shape[1]))
        co_ref[0, h] = a * c_prev + _dot_tn(v8, k8)
        no_ref[0, h:h + 1, :] = a * n_prev + wm * k.astype(F32)
        mo_ref[0, h:h + 1, :] = mt
        hn = hh * lax.rsqrt(jnp.mean(hh * hh, -1, keepdims=True) + RMS_EPS)
        hn_ref[0, h:h + 1, :] = (hn * gain_ref[h:h + 1, :] * og_ref[0, h:h + 1, :]).astype(hn_ref.dtype)


def mlstm_sample(q, k, v, og, gc, gain, c, n, m):
    bd, d = q.shape
    nh = MLSTM_HEADS
    dk = d // nh
    heads = lambda a: a.reshape(bd, nh, dk)
    g2 = jnp.transpose(gc[:, :2 * nh].reshape(bd, 2, nh), (0, 2, 1))
    blk3 = lambda n2: pl.BlockSpec((1, nh, n2), lambda bi: (bi, 0, 0))
    cspec = pl.BlockSpec((1, nh, dk, dk), lambda bi: (bi, 0, 0, 0))
    hn, co, no, mo = pl.pallas_call(
        functools.partial(_mlstm_sample_kernel, nh=nh), grid=(bd,),
        in_specs=[blk3(dk), blk3(dk), blk3(dk), blk3(dk), blk3(2), _full((nh, dk)), cspec, blk3(dk), blk3(1)],
        out_specs=[blk3(dk), cspec, blk3(dk), blk3(1)],
        out_shape=[jax.ShapeDtypeStruct((bd, nh, dk), BF16), jax.ShapeDtypeStruct(c.shape, F32),
                   jax.ShapeDtypeStruct(n.shape, F32), jax.ShapeDtypeStruct((bd, nh, 1), F32)],
        compiler_params=_cparams(("parallel",)), name="mlstm_sample",
    )(heads(q), heads(k), heads(v), heads(og), g2, gain.reshape(nh, dk), c, n, m.reshape(bd, nh, 1))
    return hn.reshape(bd, d), co, no, mo.reshape(bd, nh)


def _row_to_col(row):
    n = row.shape[1]
    eye = lax.broadcasted_iota(jnp.int32, (n, n), 0) == lax.broadcasted_iota(jnp.int32, (n, n), 1)
    return jnp.sum(jnp.where(eye, row, 0.0), axis=1, keepdims=True)


def _head_pad(q, keep):
    q2 = jnp.concatenate([q] * NSA_KV_HEADS, axis=1)
    row = lax.broadcasted_iota(jnp.int32, q2.shape, 0)
    lane = lax.broadcasted_iota(jnp.int32, q2.shape, 1)
    return jnp.where((lane // NSA_HD == row // NSA_GROUP) & keep(row), q2, jnp.zeros_like(q2))


def _nsa_sample_cmp_kernel(pt_ref, q_ref, pages_ref, pe_ref, w_ref, biasc_ref, oc_ref, idx_ref, xbuf, xrow, sem,
                           *, n_pages, nc, ncp, nsb, n_sel, past):
    b = pl.program_id(0)
    nb = pl.num_programs(0)
    kvd = NSA_KV_HEADS * NSA_HD

    def page_copy(bb, slot, p, sl):
        return pltpu.make_async_copy(pages_ref.at[pt_ref[bb * n_pages + p], pl.ds(sl * kvd, kvd), :],
                                     xbuf.at[slot, sl, p], sem.at[slot])

    def start_all(bb, slot):
        def body(p, c):
            page_copy(bb, slot, p, 0).start()
            page_copy(bb, slot, p, 1).start()
            return c
        lax.fori_loop(0, n_pages, body, 0)

    @pl.when(b == 0)
    def _():
        start_all(0, 0)

    @pl.when(b + 1 < nb)
    def _():
        start_all(b + 1, (b + 1) % 2)

    slot = b % 2

    def wait_body(p, c):
        page_copy(b, slot, p, 0).wait()
        page_copy(b, slot, p, 1).wait()
        return c
    lax.fori_loop(0, n_pages, wait_body, 0)

    def file_page(p, c):
        for sl in range(2):
            _file_rows(xbuf[slot, sl, p], xrow, sl, p)
        return c
    lax.fori_loop(0, n_pages, file_page, 0)

    acc = _compress_filed(xrow, pe_ref, w_ref, nc)
    kc = acc[:, 0:kvd].astype(BF16)
    vc = acc[:, kvd:2 * kvd].astype(BF16)
    q = q_ref[0]
    nh = q.shape[0]
    qpad = _head_pad(q, lambda r: r >= 0)
    s = _dot_nt(qpad, kc)
    s = s + biasc_ref[:, 0:nc]
    e = jnp.exp(s - jnp.max(s, -1, keepdims=True))
    p_c = e / jnp.sum(e, -1, keepdims=True)
    o = _dot(p_c.astype(BF16), vc)
    row = lax.broadcasted_iota(jnp.int32, (nh, NSA_HD), 0)
    o_h = o[:, 0:NSA_HD]
    for k in range(1, NSA_KV_HEADS):
        o_h = jnp.where(row // NSA_GROUP == k, o[:, k * NSA_HD:(k + 1) * NSA_HD], o_h)
    oc_ref[0] = o_h
    prow = lax.broadcasted_iota(jnp.int32, p_c.shape, 0)
    lane = lax.broadcasted_iota(jnp.int32, (1, ncp), 1)
    blk = lane // 2
    cur = past // SEL_BLOCK
    forced = (blk == 0) | (blk == cur) | (blk == cur - 1)
    is_cand = ((lane % 2) == 0) & (lane < 2 * nsb)
    nselp = idx_ref.shape[1]
    rsel = lax.broadcasted_iota(jnp.int32, (nselp, 1), 0).astype(F32)
    out_lane = lax.broadcasted_iota(jnp.int32, (nselp, LANES), 1)
    result = jnp.full((nselp, LANES), -1, jnp.int32)
    for k in range(NSA_KV_HEADS):
        imp = jnp.sum(jnp.where(prow // NSA_GROUP == k, p_c, 0.0), axis=0, keepdims=True)
        imp = jnp.concatenate([imp, jnp.zeros((1, ncp - nc), F32)], axis=1)
        imp = _pair_sum(imp)
        score = jnp.where(forced, FORCE_SCORE, jnp.where(blk <= cur, imp, -1.0))
        score = jnp.where(is_cand, score, -2.0)
        sel, rank = _select_blocks(score, n_sel, nsb)
        hit = (rank == rsel) & (sel > 0.5)
        idx = jnp.sum(jnp.where(hit, (blk + 1).astype(F32), 0.0), axis=1, keepdims=True) - 1.0
        result = jnp.where(out_lane == k, idx.astype(jnp.int32), result)
    idx_ref[0] = result


def nsa_sample_cmp(q8, pages, page_table, pe2, wbd, rel_bias):
    bd, nh, hd = q8.shape
    n_pages = page_table.shape[1]
    past = n_pages * PAGE_SIZE
    nc = past // CMP_BLOCK
    nsb = -(-(past + 1) // SEL_BLOCK)
    n_sel = min(TOP_N, nsb)
    ncp = _round_up(max(nc, SEL_RATIO * nsb), LANES)
    nselp = _round_up(n_sel, SUBLANES)
    kvd = wbd.shape[-1]
    cend = jnp.arange(nc) * CMP_BLOCK + (CMP_BLOCK - 1)
    biasc = jnp.zeros((nh, ncp), F32).at[:, :nc].set(_bias_of(rel_bias, past - cend).T)
    grid_spec = pltpu.PrefetchScalarGridSpec(
        num_scalar_prefetch=1, grid=(bd,),
        in_specs=[pl.BlockSpec((1, nh, hd), lambda bi, pt: (bi, 0, 0)),
                  pl.BlockSpec(memory_space=pl.ANY),
                  pl.BlockSpec(pe2.shape, lambda bi, pt: (0, 0, 0)),
                  pl.BlockSpec(wbd.shape, lambda bi, pt: (0, 0, 0, 0)),
                  pl.BlockSpec((nh, ncp), lambda bi, pt: (0, 0))],
        out_specs=[pl.BlockSpec((1, nh, hd), lambda bi, pt: (bi, 0, 0)),
                   pl.BlockSpec((1, nselp, LANES), lambda bi, pt: (bi, 0, 0))],
        scratch_shapes=[pltpu.VMEM((2, 2, n_pages, kvd, PAGE_SIZE), F32),
                        pltpu.VMEM((2, CMP_BLOCK // SUBLANES, nc * SUBLANES, kvd), F32),
                        pltpu.SemaphoreType.DMA((2,))])
    oc, idx = pl.pallas_call(
        functools.partial(_nsa_sample_cmp_kernel, n_pages=n_pages, nc=nc, ncp=ncp, nsb=nsb, n_sel=n_sel, past=past),
        grid_spec=grid_spec,
        out_shape=[jax.ShapeDtypeStruct((bd, nh, hd), F32), jax.ShapeDtypeStruct((bd, nselp, LANES), jnp.int32)],
        compiler_params=_cparams(("arbitrary",), VMEM_LIMIT), name="nsa_sample_cmp",
    )(page_table.reshape(-1), q8, pages, pe2, wbd, biasc)
    sel_idx = jnp.transpose(idx[:, :n_sel, :NSA_KV_HEADS], (0, 2, 1))
    return oc, sel_idx


def _nsa_sample_att_kernel(pt_ref, si_ref, q_ref, g_ref, oc_ref, kvn_ref, wn_ref, wc_ref, pages_ref,
                           bsel_ref, bwin_ref, ob_ref, win_ref, selbuf, wall, sem,
                           *, n_pages, n_sel, past, wb):
    b = pl.program_id(0)
    nb = pl.num_programs(0)
    kvd = NSA_KV_HEADS * NSA_HD
    hd = NSA_HD
    n_blk_pages = past // SEL_BLOCK
    per_page = PAGE_SIZE // SEL_BLOCK
    n_slots = NSA_KV_HEADS * n_sel

    def blk_of(bb, j):
        return si_ref[bb * n_slots + j]

    def blk_copy(bb, slot, j):
        blk = jnp.clip(blk_of(bb, j), 0, n_blk_pages - 1)
        page = pt_ref[bb * n_pages + blk // per_page]
        return pltpu.make_async_copy(pages_ref.at[page, pl.ds(2 * kvd, 2 * kvd), :], selbuf.at[slot, j], sem.at[slot])

    def in_pages(bb, j):
        blk = blk_of(bb, j)
        return (blk >= 0) & (blk < n_blk_pages)

    def start_all(bb, slot):
        def body(j, c):
            @pl.when(in_pages(bb, j))
            def _():
                blk_copy(bb, slot, j).start()
            return c
        lax.fori_loop(0, n_slots, body, 0)

    @pl.when(b == 0)
    def _():
        start_all(0, 0)

    @pl.when(b + 1 < nb)
    def _():
        start_all(b + 1, (b + 1) % 2)

    slot = b % 2
    new_sel = _row_to_col(kvn_ref[0][:, 2 * kvd:4 * kvd])
    lane = lax.broadcasted_iota(jnp.int32, (1, PAGE_SIZE), 1)

    def wait_body(j, c):
        @pl.when(in_pages(b, j))
        def _():
            blk_copy(b, slot, j).wait()

        @pl.when(jnp.logical_not(in_pages(b, j)))
        def _():
            is_new = blk_of(b, j) == n_blk_pages
            selbuf[slot, j] = jnp.where((lane == 0) & is_new, new_sel, 0.0)
        return c
    lax.fori_loop(0, n_slots, wait_body, 0)

    q = q_ref[0]
    nh = q.shape[0]
    gates = g_ref[0]
    head = lax.broadcasted_iota(jnp.int32, (nh, 1), 0)

    def attend(qp, keys_t, vals_t, bias, valid):
        s = jnp.where(valid, _dot(qp, keys_t) + bias, NEG)
        e = jnp.where(valid, jnp.exp(s - jnp.max(s, -1, keepdims=True)), 0.0)
        den = jnp.sum(e, -1, keepdims=True)
        p = e / jnp.where(den > 0, den, 1.0)
        return _dot_nt(p.astype(BF16), vals_t)

    o_s = jnp.zeros((nh, kvd), F32)
    for k in range(NSA_KV_HEADS):
        blks = [blk_of(b, k * n_sel + r) for r in range(n_sel)]
        tiles = [selbuf[slot, k * n_sel + r] for r in range(n_sel)]
        keys_t = jnp.concatenate([t_[0:kvd, :] for t_ in tiles], axis=1).astype(BF16)
        vals_t = jnp.concatenate([t_[kvd:2 * kvd, :] for t_ in tiles], axis=1).astype(BF16)
        bias = jnp.concatenate([bsel_ref[jnp.clip(bl // per_page, 0, n_pages)] for bl in blks], axis=1)
        valid = jnp.concatenate(
            [(lane // SEL_BLOCK == bl % per_page) & ((bl // per_page) * PAGE_SIZE + lane <= past) & (bl >= 0)
             for bl in blks], axis=1)
        o_k = attend(_head_pad(q, lambda r: r // NSA_GROUP == k), keys_t, vals_t, bias, valid)
        o_s = jnp.where(head // NSA_GROUP == k, o_k, o_s)
    wlanes = wall.shape[1]
    wall[:, 0:wb] = wc_ref[0]
    tail = lax.broadcasted_iota(jnp.int32, (1, wlanes - wb), 1)
    wall[:, wb:wlanes] = jnp.where(tail == 0, _row_to_col(wn_ref[0]), 0.0)
    win_ref[0] = pltpu.roll(wall[...], wlanes - 1, 1)[:, 0:wb]
    w_pos = lax.broadcasted_iota(jnp.int32, (1, wlanes), 1)
    valid_w = (w_pos <= wb) & (wb - w_pos < WINDOW) & (past - wb + w_pos >= 0)
    o_w = attend(_head_pad(q, lambda r: r >= 0), wall[0:kvd, :].astype(BF16), wall[kvd:2 * kvd, :].astype(BF16),
                 bwin_ref[...], valid_w)
    o_c = jnp.concatenate([oc_ref[0]] * NSA_KV_HEADS, axis=1)
    mix = gates[:, 0:1] * o_c + gates[:, 1:2] * o_s + gates[:, 2:3] * o_w
    out = mix[:, 0:hd]
    for k in range(1, NSA_KV_HEADS):
        out = jnp.where(head // NSA_GROUP == k, mix[:, k * hd:(k + 1) * hd], out)
    ob_ref[0] = out.astype(ob_ref.dtype)


def nsa_sample_att(q8, gates, oc, kv03, kv45, wcache_t, layer, pages_t, page_table, sel_idx, rel_bias):
    bd, nh, hd = q8.shape
    n_pages = page_table.shape[1]
    past = n_pages * PAGE_SIZE
    wb = wcache_t.shape[2]
    kvd = NSA_KV_HEADS * hd
    n_sel = sel_idx.shape[2]
    wlanes = _round_up(wb + 1, LANES)
    g3 = gates[:, :3 * nh].reshape(bd, nh, 3)
    dist = past - (jnp.arange(n_pages + 1)[:, None] * PAGE_SIZE + jnp.arange(PAGE_SIZE)[None, :])
    bsel = jnp.transpose(_bias_of(rel_bias, dist), (0, 2, 1))
    bwin = _bias_of(rel_bias, wb - jnp.arange(wlanes)).T
    blk = lambda n2, n3: pl.BlockSpec((1, n2, n3), lambda bi, pt, si: (bi, 0, 0))
    grid_spec = pltpu.PrefetchScalarGridSpec(
        num_scalar_prefetch=2, grid=(bd,),
        in_specs=[blk(nh, hd), blk(nh, 3), blk(nh, hd), blk(1, 4 * kvd), blk(1, 2 * kvd),
                  pl.BlockSpec((1, 2 * kvd, wb), lambda bi, pt, si: (layer * bd + bi, 0, 0)),
                  pl.BlockSpec(memory_space=pl.ANY),
                  pl.BlockSpec(bsel.shape, lambda bi, pt, si: (0, 0, 0)),
                  pl.BlockSpec(bwin.shape, lambda bi, pt, si: (0, 0))],
        out_specs=[blk(nh, hd), blk(2 * kvd, wb)],
        scratch_shapes=[pltpu.VMEM((2, NSA_KV_HEADS * n_sel, 2 * kvd, PAGE_SIZE), F32),
                        pltpu.VMEM((2 * kvd, wlanes), F32), pltpu.SemaphoreType.DMA((2,))])
    ob, win = pl.pallas_call(
        functools.partial(_nsa_sample_att_kernel, n_pages=n_pages, n_sel=n_sel, past=past, wb=wb),
        grid_spec=grid_spec,
        out_shape=[jax.ShapeDtypeStruct((bd, nh, hd), BF16), jax.ShapeDtypeStruct((bd, 2 * kvd, wb), F32)],
        compiler_params=_cparams(("arbitrary",), VMEM_LIMIT), name="nsa_sample_att",
    )(page_table.reshape(-1), sel_idx.reshape(-1), q8, g3, oc, kv03.reshape(bd, 1, -1), kv45.reshape(bd, 1, -1),
      wcache_t, pages_t, bsel, bwin)
    return ob.reshape(bd, nh * hd), win


def kernel(x_prompt, x_sample, mem_prompt, cache_conv, cache_nsa_pages, cache_nsa_window, state_mlstm_c,
           state_mlstm_n, state_mlstm_m, cache_mem_kv, page_table, rel_bias, norm_mix, norm_xattn, norm_mem,
           norm_ffn, norm_final, w_in_even, w_out_even, conv_w, conv_b, conv_ln_g, conv_ln_b, nsa_cmp_pe,
           nsa_cmp_w, w_in_odd, mlstm_b_i, mlstm_b_f, mlstm_norm, w_out_odd, xattn_wq, xattn_wkv, xattn_wo,
           ffn_w_gu, ffn_w_dn, router_w, router_b, expert_w_gu, expert_w_dn):
    b, s, d = x_prompt.shape
    bd, td, _ = x_sample.shape
    assert td == 1, "the sample group decodes one token per sequence"
    depth = norm_mix.shape[0]
    mt = mem_prompt.shape[1]
    cc = conv_w.shape[2]
    hist = conv_w.shape[1] - 1
    wb = cache_nsa_window.shape[2]
    kvh, hd = NSA_KV_HEADS, NSA_HD
    n_pool = cache_nsa_pages.shape[1]
    assert s >= hist and s >= wb and s % Q_BLOCK == 0
    xp = x_prompt.reshape(b * s, d)
    xs = x_sample.reshape(bd, d)
    mem = mem_prompt.reshape(b * mt, d)
    pages_t = jnp.swapaxes(cache_nsa_pages.reshape(-1, PAGE_SIZE, 4 * kvh * hd), 1, 2)
    window_t = jnp.swapaxes(cache_nsa_window.reshape(-1, wb, 2 * kvh * hd), 1, 2)
    memkv_all = cache_mem_kv.reshape(depth * bd, mt, -1)
    bf = lambda a: a.astype(BF16)
    conv_p, conv_s, nsa_p, nsa_s, win_p, win_s = [], [], [], [], [], []
    mc_p, mc_s, mn_p, mn_s, mm_p, mm_s, memkv_p = [], [], [], [], [], [], []
    for l in range(depth):
        li = l // 2
        if l % 2 == 0:
            prm = prep_even(w_in_even[li], nsa_cmp_pe[li], nsa_cmp_w[li])
            w_out = bf(w_out_even[li])
            w_parts = [w_out[:cc], w_out[cc:]]
            conv_args = (conv_w[li], conv_b[li], conv_ln_g[li], conv_ln_b[li])
            glu, keys, qt, vt, gt, kv_t = inproj_even(xp, norm_mix[l], prm['w_in'], cc, s)
            a_out = conv_prompt(glu, *conv_args, b, s)
            kc, vct = compress_prompt(kv_t, prm['pe2'], prm['wbd'], _round_up(s // CMP_BLOCK, LANES))
            b_out = nsa_prompt(qt, gt, kc, vct, keys, vt, rel_bias, b, s)
            xp = outproj(xp, [a_out, b_out], w_parts)
            conv_p.append(glu.reshape(b, s, cc)[:, s - hist:])
            rows_t = kv_t.reshape(b, 6, kvh, hd, s)
            nsa_p.append(jnp.transpose(rows_t[:, :4], (0, 4, 1, 2, 3)))
            win_p.append(jnp.transpose(rows_t[:, 4:, :, :, s - wb:], (0, 4, 1, 2, 3)))
            glu, kv03, kv45, q, gates = inproj_even(xs, norm_mix[l], prm['w_in'], cc)
            a_out, conv_state = conv_sample(cache_conv[li], glu, *conv_args)
            q8 = q.reshape(bd, NSA_HEADS, hd)
            pt = page_table + li * n_pool
            o_c, sel_idx = nsa_sample_cmp(q8, pages_t, pt, prm['pe2'], prm['wbd'], rel_bias)
            b_out, win = nsa_sample_att(q8, gates, o_c, kv03, kv45, window_t, li, pages_t, pt, sel_idx, rel_bias)
            xs = outproj(xs, [a_out, b_out], w_parts)
            conv_s.append(conv_state)
            nsa_s.append(kv03.reshape(bd, 1, 4, kvh, hd))
            win_s.append(jnp.transpose(win.reshape(bd, 2, kvh, hd, wb), (0, 4, 1, 2, 3)))
        else:
            prm = prep_odd(w_in_odd[li], mlstm_b_i[li], mlstm_b_f[li])
            w_out = bf(w_out_odd[li])
            q, k, v, og, gc, gr = inproj_odd(xp, norm_mix[l], prm['w_in'], prm['bias'])
            hn, c_new, n_new, m_new = mlstm_prompt(q, k, v, og, gc, gr, mlstm_norm[li], b, s)
            xp = outproj(xp, [hn], [w_out])
            mc_p.append(c_new)
            mn_p.append(n_new)
            mm_p.append(m_new[:, :, 0])
            q, k, v, og, gc, gr = inproj_odd(xs, norm_mix[l], prm['w_in'], prm['bias'])
            hn, c_new, n_new, m_new = mlstm_sample(q, k, v, og, gc, mlstm_norm[li], state_mlstm_c[li],
                                                   state_mlstm_n[li], state_mlstm_m[li])
            xs = outproj(xs, [hn], [w_out])
            mc_s.append(c_new)
            mn_s.append(n_new)
            mm_s.append(m_new)
        wq, wo = bf(xattn_wq[l]), bf(xattn_wo[l])
        mkv, mkv_b = memkv(mem, norm_mem[l], bf(xattn_wkv[l]))
        memkv_p.append(mkv.reshape(b, mt, 2, X_HEADS, d // X_HEADS))
        xp = xattn_prompt(xp, norm_xattn[l], wq, mkv_b, wo, b, s)
        xs = xattn_sample(xs, norm_xattn[l], wq, memkv_all, wo, l)
        if l % 2 == 0:
            w_gu, w_dn = bf(ffn_w_gu[li]), bf(ffn_w_dn[li])
            xp = ffn(xp, norm_ffn[l], w_gu, w_dn)
            xs = ffn(xs, norm_ffn[l], w_gu, w_dn)
        else:
            e_gu, e_dn = bf(expert_w_gu[li]), bf(expert_w_dn[li])
            final_g = norm_final if l == depth - 1 else None
            comb, h, mask, counts = router(xp, norm_ffn[l], router_w[li], router_b[li])
            xp = moe_grouped(xp, h, comb, mask, counts, e_gu, e_dn, final_g)
            comb, h, _, _ = router(xs, norm_ffn[l], router_w[li], router_b[li])
            xs = moe(xs, h, comb, e_gu, e_dn, final_g)
    if depth % 2:
        xp, xs = rmsnorm(xp, norm_final), rmsnorm(xs, norm_final)
    y_prompt = xp.reshape(b, s, d)
    y_sample = xs.reshape(bd, 1, d)
    return (y_prompt, y_sample, jnp.stack(conv_p), jnp.stack(conv_s), jnp.stack(nsa_p), jnp.stack(nsa_s),
            jnp.stack(win_p), jnp.stack(win_s), jnp.stack(mc_p), jnp.stack(mc_s), jnp.stack(mn_p),
            jnp.stack(mn_s), jnp.stack(mm_p), jnp.stack(mm_s), jnp.stack(memkv_p))
```

```python
import functools
import math

import jax
import jax.numpy as jnp
import numpy as np
from jax import lax
from jax.experimental import pallas as pl
from jax.experimental.pallas import tpu as pltpu

F32 = jnp.float32
BF16 = jnp.bfloat16

PAGE_SIZE = 128
CONV_WIDTH = 31
NSA_HEADS = 8
NSA_KV_HEADS = 2
NSA_GROUP = NSA_HEADS // NSA_KV_HEADS
NSA_HD = 64
CMP_BLOCK = 32
SEL_BLOCK = 64
SEL_RATIO = SEL_BLOCK // CMP_BLOCK
TOP_N = 16
WINDOW = 512
Q_BLOCK = 128
FORCE_SCORE = 1.0e4
NUM_BUCKETS = 32
MAX_DISTANCE = 1024
MLSTM_HEADS = 4
X_HEADS = 4
N_EXPERTS = 8
TOP_K = 2
RMS_EPS = 1e-6
LN_EPS = 1e-5
NEG = -1e30

LANES = 128
SUBLANES = 8
VMEM_LIMIT = 56 * 1024 * 1024
MLSTM_CHUNK = 256


def _cparams(sem, vmem=None):
    return pltpu.CompilerParams(dimension_semantics=sem, vmem_limit_bytes=vmem)


def _rms(x, g):
    return x * lax.rsqrt(jnp.mean(x * x, -1, keepdims=True) + RMS_EPS) * g


def _dot(a, b):
    return jnp.dot(a, b, preferred_element_type=F32)


def _dot_nt(a, b):
    return lax.dot_general(a, b, (((1,), (1,)), ((), ())), preferred_element_type=F32)


def _dot_tn(a, b):
    return lax.dot_general(a, b, (((0,), (0,)), ((), ())), preferred_element_type=F32)


def _full(shape):
    n = len(shape)
    return pl.BlockSpec(shape, lambda *_: (0,) * n)


def _row_tile(m, pref):
    t = min(pref, m)
    while m % t:
        t //= 2
    return t


def _rmsnorm_kernel(x_ref, g_ref, o_ref):
    o_ref[...] = _rms(x_ref[...], g_ref[...])


def rmsnorm(x, g):
    m, d = x.shape
    tm = _row_tile(m, 1024)
    return pl.pallas_call(
        _rmsnorm_kernel, grid=(m // tm,),
        in_specs=[pl.BlockSpec((tm, d), lambda i: (i, 0)), _full((1, d))],
        out_specs=pl.BlockSpec((tm, d), lambda i: (i, 0)),
        out_shape=jax.ShapeDtypeStruct((m, d), F32),
        compiler_params=_cparams(("parallel",)), name="rmsnorm",
    )(x, g.reshape(1, d))


def _outproj_kernel(*refs, n_in):
    x_ref = refs[0]
    a_refs = refs[1:1 + n_in]
    w_refs = refs[1 + n_in:1 + 2 * n_in]
    o_ref = refs[1 + 2 * n_in]
    acc = x_ref[...]
    for a_ref, w_ref in zip(a_refs, w_refs):
        acc = acc + _dot(a_ref[...], w_ref[...])
    o_ref[...] = acc


def outproj(x, acts, ws):
    m, d = x.shape
    tm = _row_tile(m, 512)
    n_in = len(acts)
    in_specs = [pl.BlockSpec((tm, d), lambda i: (i, 0))]
    in_specs += [pl.BlockSpec((tm, a.shape[1]), lambda i: (i, 0)) for a in acts]
    in_specs += [_full(w.shape) for w in ws]
    return pl.pallas_call(
        functools.partial(_outproj_kernel, n_in=n_in), grid=(m // tm,),
        in_specs=in_specs, out_specs=pl.BlockSpec((tm, d), lambda i: (i, 0)),
        out_shape=jax.ShapeDtypeStruct((m, d), F32),
        compiler_params=_cparams(("parallel",)), name="outproj",
    )(x, *acts, *ws)


def _inproj_even_kernel(x_ref, g_ref, w_ref, glu_ref, *rest, cc, qd, kvd, tiles):
    xn = _rms(x_ref[...], g_ref[...]).astype(BF16)

    def mm(lo, hi):
        return _dot(xn, w_ref[:, lo:hi])

    o = 0
    a = mm(o, o + cc)
    b = mm(o + cc, o + 2 * cc)
    glu_ref[...] = a * jax.nn.sigmoid(b)
    o += 2 * cc
    q = mm(o, o + qd) * (NSA_HD ** -0.5)
    o += qd
    kv03 = mm(o, o + 4 * kvd)
    o += 4 * kvd
    kv45 = mm(o, o + 2 * kvd)
    o += 2 * kvd
    gates = jax.nn.sigmoid(mm(o, o + LANES))
    if tiles == 0:
        kv03_ref, kv45_ref, q_ref, gate_ref = rest
        kv03_ref[...] = kv03
        kv45_ref[...] = kv45
        q_ref[...] = q.astype(BF16)
        gate_ref[...] = gates
        return
    keys_ref, qt_ref, vt_ref, gt_ref, kvt_ref = rest
    kvt_ref[0] = jnp.concatenate([kv03, kv45], axis=1).T
    keys_ref[...] = jnp.concatenate([kv03[:, 2 * kvd:3 * kvd], kv45[:, 0:kvd]], axis=1).astype(BF16)
    vals = jnp.concatenate([kv03[:, 3 * kvd:4 * kvd], kv45[:, kvd:2 * kvd]], axis=1)
    for j in range(tiles):
        rows = slice(j * Q_BLOCK, (j + 1) * Q_BLOCK)
        qt_ref[j] = q[rows, :].T.astype(BF16)
        vt_ref[j] = vals[rows, :].T.astype(BF16)
        gt_ref[j] = gates[rows, :].T


def inproj_even(x, g, w_pad, cc, seq=None):
    m, d = x.shape
    qd = NSA_HEADS * NSA_HD
    kvd = NSA_KV_HEADS * NSA_HD
    tm = _row_tile(m, 256)
    row = lambda n: pl.BlockSpec((tm, n), lambda i: (i, 0))
    out_specs = [row(cc)]
    out_shape = [jax.ShapeDtypeStruct((m, cc), F32)]
    transposed = seq is not None
    tiles = tm // Q_BLOCK if transposed else 0
    if transposed:
        assert tm % Q_BLOCK == 0 and seq % tm == 0
        per_seq = seq // tm
        tile = lambda n: pl.BlockSpec((tiles, n, Q_BLOCK), lambda i: (i, 0, 0))
        out_specs += [row(2 * kvd), tile(qd), tile(2 * kvd), tile(LANES),
                      pl.BlockSpec((1, 6 * kvd, tm), lambda i: (i // per_seq, 0, i % per_seq))]
        out_shape += [jax.ShapeDtypeStruct((m, 2 * kvd), BF16),
                      jax.ShapeDtypeStruct((m // Q_BLOCK, qd, Q_BLOCK), BF16),
                      jax.ShapeDtypeStruct((m // Q_BLOCK, 2 * kvd, Q_BLOCK), BF16),
                      jax.ShapeDtypeStruct((m // Q_BLOCK, LANES, Q_BLOCK), F32),
                      jax.ShapeDtypeStruct((m // seq, 6 * kvd, seq), F32)]
    else:
        out_specs += [row(4 * kvd), row(2 * kvd), row(qd), row(LANES)]
        out_shape += [jax.ShapeDtypeStruct((m, 4 * kvd), F32), jax.ShapeDtypeStruct((m, 2 * kvd), F32),
                      jax.ShapeDtypeStruct((m, qd), BF16), jax.ShapeDtypeStruct((m, LANES), F32)]
    return pl.pallas_call(
        functools.partial(_inproj_even_kernel, cc=cc, qd=qd, kvd=kvd, tiles=tiles), grid=(m // tm,),
        in_specs=[row(d), _full((1, d)), _full(w_pad.shape)],
        out_specs=out_specs, out_shape=out_shape,
        compiler_params=_cparams(("parallel",)), name="inproj_even",
    )(x, g.reshape(1, d), w_pad)


def _conv_post(y, lg, lb):
    mu = jnp.mean(y, -1, keepdims=True)
    var = jnp.mean(jnp.square(y - mu), -1, keepdims=True)
    yn = (y - mu) * lax.rsqrt(var + LN_EPS) * lg + lb
    return yn * jax.nn.sigmoid(yn)


CONV_SUB = 64
CONV_PAD = 32


def _conv_prompt_kernel(glu_ref, cw_ref, cb_ref, lg_ref, lb_ref, o_ref, ext_ref, y_ref, *, ts, s):
    i = pl.program_id(1)
    c = glu_ref.shape[-1]

    @pl.when(i == 0)
    def _():
        ext_ref[0:CONV_PAD, :] = jnp.zeros((CONV_PAD, c), F32)
        ext_ref[CONV_PAD:CONV_PAD + s, :] = glu_ref[0]
        ext_ref[CONV_PAD + s:CONV_PAD + s + SUBLANES, :] = jnp.zeros((SUBLANES, c), F32)

    lead = CONV_PAD - (CONV_WIDTH - 1)
    span = CONV_SUB + CONV_PAD

    def sub(j, carry):
        r0 = pl.multiple_of(i * ts + j * CONV_SUB, CONV_SUB)
        for c0 in range(0, c, LANES):
            xw = ext_ref[pl.ds(r0, span + SUBLANES), c0:c0 + LANES]
            acc = jnp.zeros((CONV_SUB, LANES), F32) + cb_ref[:, c0:c0 + LANES]
            for r in range(SUBLANES):
                xr = xw if r == 0 else pltpu.roll(xw, span + SUBLANES - r, 0)
                for a in range(span // SUBLANES):
                    w = SUBLANES * a + r - lead
                    if 0 <= w < CONV_WIDTH:
                        acc = acc + xr[SUBLANES * a:SUBLANES * a + CONV_SUB, :] * cw_ref[w:w + 1, c0:c0 + LANES]
            y_ref[:, c0:c0 + LANES] = acc
        o_ref[0, pl.ds(pl.multiple_of(j * CONV_SUB, CONV_SUB), CONV_SUB), :] = _conv_post(
            y_ref[...], lg_ref[...], lb_ref[...]).astype(o_ref.dtype)
        return carry

    lax.fori_loop(0, ts // CONV_SUB, sub, 0)


def conv_prompt(glu, cw, cb, lg, lb, b, s):
    c = glu.shape[-1]
    ts = _row_tile(s, 256)
    vec = lambda a: a.reshape(1, c)
    out = pl.pallas_call(
        functools.partial(_conv_prompt_kernel, ts=ts, s=s), grid=(b, s // ts),
        in_specs=[pl.BlockSpec((1, s, c), lambda bi, i: (bi, 0, 0)), _full((CONV_WIDTH, c)),
                  _full((1, c)), _full((1, c)), _full((1, c))],
        out_specs=pl.BlockSpec((1, ts, c), lambda bi, i: (bi, i, 0)),
        out_shape=jax.ShapeDtypeStruct((b, s, c), BF16),
        scratch_shapes=[pltpu.VMEM((CONV_PAD + s + SUBLANES, c), F32), pltpu.VMEM((CONV_SUB, c), F32)],
        compiler_params=_cparams(("parallel", "arbitrary")), name="conv_prompt",
    )(glu.reshape(b, s, c), cw, vec(cb), vec(lg), vec(lb))
    return out.reshape(b * s, c)


def _conv_sample_kernel(cache_ref, glu_ref, cw_ref, cb_ref, lg_ref, lb_ref, o_ref, st_ref):
    hist = CONV_WIDTH - 1
    cache = cache_ref[...]
    glu = glu_ref[...]
    y = jnp.sum(cache * cw_ref[0:hist, :][None], axis=1) + glu * cw_ref[hist:hist + 1, :] + cb_ref[...]
    o_ref[...] = _conv_post(y, lg_ref[...], lb_ref[...]).astype(o_ref.dtype)
    st_ref[:, 0:hist - 1, :] = cache[:, 1:hist, :]
    st_ref[:, hist - 1:hist, :] = glu[:, None, :]


def conv_sample(cache, glu, cw, cb, lg, lb):
    bd, hist, c = cache.shape
    vec = lambda a: a.reshape(1, c)
    return pl.pallas_call(
        _conv_sample_kernel,
        out_shape=[jax.ShapeDtypeStruct((bd, c), BF16), jax.ShapeDtypeStruct((bd, hist, c), F32)],
        name="conv_sample",
    )(cache, glu, cw, vec(cb), vec(lg), vec(lb))


def _rel_bucket(dist):
    n = jnp.maximum(dist, 0)
    max_exact = NUM_BUCKETS // 2
    nf = jnp.maximum(n, 1).astype(F32)
    large = max_exact + (jnp.log(nf / max_exact) / math.log(MAX_DISTANCE / max_exact)
                         * (NUM_BUCKETS - max_exact)).astype(jnp.int32)
    large = jnp.minimum(large, NUM_BUCKETS - 1)
    return jnp.where(n < max_exact, n, large)


def _bias_of(rel_bias, dist):
    bucket = _rel_bucket(dist)[..., None]
    out = jnp.zeros(bucket.shape[:-1] + (rel_bias.shape[1],), F32)
    for k in range(NUM_BUCKETS):
        out = jnp.where(bucket == k, rel_bias[k].astype(F32), out)
    return out


def _compress_accumulate(load_rows, pe_ref, w_ref, nc):
    accs = []
    for slot in range(2):
        acc = jnp.zeros((nc, w_ref.shape[-1]), F32)
        for j in range(CMP_BLOCK):
            xj = load_rows(slot, j) + pe_ref[slot, j:j + 1, :]
            acc = acc + _dot(xj.astype(BF16), w_ref[slot, j])
        accs.append(acc)
    return jnp.concatenate(accs, axis=1)


def _file_rows(tile_t, xrow, sl, page):
    rows = tile_t.T
    per_page = PAGE_SIZE // CMP_BLOCK
    for cl in range(per_page):
        for a in range(CMP_BLOCK // SUBLANES):
            r0 = cl * CMP_BLOCK + a * SUBLANES
            dst = pl.multiple_of((page * per_page + cl) * SUBLANES, SUBLANES)
            xrow[sl, a, pl.ds(dst, SUBLANES), :] = rows[r0:r0 + SUBLANES, :]


def _compress_filed(xrow, pe_ref, w_ref, nc):
    return _compress_accumulate(
        lambda sl, j: xrow[sl, j // SUBLANES, pl.ds(j % SUBLANES, nc, stride=SUBLANES), :], pe_ref, w_ref, nc)


def _compress_prompt_kernel(x_ref, pe_ref, w_ref, kc_ref, vct_ref, xrow, *, nc, ncp, kvd):
    for p in range(x_ref.shape[2] // PAGE_SIZE):
        for sl in range(2):
            _file_rows(x_ref[0, sl * kvd:(sl + 1) * kvd, p * PAGE_SIZE:(p + 1) * PAGE_SIZE], xrow, sl, p)
    acc = _compress_filed(xrow, pe_ref, w_ref, nc)
    if ncp > nc:
        acc = jnp.concatenate([acc, jnp.zeros((ncp - nc, 2 * kvd), F32)], axis=0)
    kc_ref[0] = acc[:, 0:kvd].astype(BF16)
    vct_ref[0] = acc[:, kvd:2 * kvd].T.astype(BF16)


def compress_prompt(kv_t, pe2, wbd, ncp):
    b, _, s = kv_t.shape
    assert s % PAGE_SIZE == 0
    nc = s // CMP_BLOCK
    kvd = wbd.shape[-1]
    return pl.pallas_call(
        functools.partial(_compress_prompt_kernel, nc=nc, ncp=ncp, kvd=kvd), grid=(b,),
        in_specs=[pl.BlockSpec((1, 2 * kvd, s), lambda bi: (bi, 0, 0)), _full(pe2.shape), _full(wbd.shape)],
        out_specs=[pl.BlockSpec((1, ncp, kvd), lambda bi: (bi, 0, 0)), pl.BlockSpec((1, kvd, ncp), lambda bi: (bi, 0, 0))],
        out_shape=[jax.ShapeDtypeStruct((b, ncp, kvd), BF16), jax.ShapeDtypeStruct((b, kvd, ncp), BF16)],
        scratch_shapes=[pltpu.VMEM((2, CMP_BLOCK // SUBLANES, nc * SUBLANES, kvd), F32)],
        compiler_params=_cparams(("parallel",)), name="compress_prompt",
    )(kv_t, pe2, wbd)


def _select_blocks(score, n_sel, n_cand):
    lane = lax.broadcasted_iota(jnp.int32, score.shape, 1)
    rank = jnp.zeros(score.shape, F32)
    for i in range(n_cand):
        col = score[:, 2 * i:2 * i + 1]
        beats = (col > score) | ((col == score) & (lane > 2 * i))
        rank = rank + beats.astype(F32)
    is_cand = ((lane % 2) == 0) & (lane < 2 * n_cand)
    return (is_cand & (rank < n_sel) & (score >= 0)).astype(F32), rank


def _pair_sum(imp):
    n = imp.shape[1]
    return imp + pltpu.roll(imp, n - 1, 1)


def _rank_rows(score, n_sel, n_cand):
    blk = lax.broadcasted_iota(jnp.int32, score.shape, 0)
    rank = jnp.zeros(score.shape, F32)
    for i in range(n_cand):
        row = score[i:i + 1, :]
        beats = (row > score) | ((row == score) & (blk > i))
        rank = rank + beats.astype(F32)
    return ((rank < n_sel) & (score >= 0)).astype(F32)


def _nsa_prompt_kernel(qt_ref, gt_ref, kc_ref, vct_ref, keys_ref, vt_ref, biasc_ref, btile_ref, o_ref,
                       qt_scr, oc_scr, acc_s, acc_w, imp_scr, sel_scr, out_scr, *, nc, nsb, n_sel):
    qi = pl.program_id(1)
    g, hd, kvh, qb = NSA_GROUP, NSA_HD, NSA_KV_HEADS, Q_BLOCK
    kvd = kvh * hd
    ncp = kc_ref.shape[1]
    nsbp = sel_scr.shape[1]
    q_pos = qi * qb + lax.broadcasted_iota(jnp.int32, (1, qb), 1)
    key_row = lax.broadcasted_iota(jnp.int32, (qb, 1), 0)
    c_row = lax.broadcasted_iota(jnp.int32, (ncp, 1), 0)
    mask_c = (q_pos >= c_row * CMP_BLOCK + (CMP_BLOCK - 1)) & (c_row < nc)
    blk = lax.broadcasted_iota(jnp.int32, (nsbp, 1), 0)
    cur = q_pos // SEL_BLOCK
    forced = (blk == 0) | (blk == cur) | (blk == cur - 1)
    zeros = jnp.zeros((hd, qb), BF16)
    for k in range(kvh):
        for gi in range(g):
            h = k * g + gi
            parts = [zeros] * kvh
            parts[k] = qt_ref[0, h * hd:(h + 1) * hd, :]
            qt_scr[k, :, gi * qb:(gi + 1) * qb] = jnp.concatenate(parts, axis=0)
        s_c = _dot(kc_ref[0], qt_scr[k])
        imp = jnp.zeros((ncp, qb), F32)
        probs = []
        for gi in range(g):
            s = jnp.where(mask_c, s_c[:, gi * qb:(gi + 1) * qb] + biasc_ref[k * g + gi], NEG)
            e = jnp.where(mask_c, jnp.exp(s - jnp.max(s, 0, keepdims=True)), 0.0)
            den = jnp.sum(e, 0, keepdims=True)
            p = e / jnp.where(den > 0, den, 1.0)
            imp = imp + p
            probs.append(p.astype(BF16))
        oc_scr[k] = _dot(vct_ref[0, k * hd:(k + 1) * hd, :], jnp.concatenate(probs, axis=1))
        imp_scr[...] = imp + pltpu.roll(imp, ncp - 1, 0)
        cand = imp_scr[pl.ds(0, nsbp, stride=SEL_RATIO), :]
        score = jnp.where(forced, FORCE_SCORE, jnp.where(blk <= cur, cand, -1.0))
        sel_scr[k] = _rank_rows(jnp.where(blk < nsb, score, -2.0), n_sel, nsb)

    per_tile = qb // SEL_BLOCK
    n_tiles = keys_ref.shape[1] // qb
    first = ([jnp.full((1, qb), NEG, F32)] * g, [jnp.zeros((1, qb), F32)] * g)

    def tile_step(tiles, carry, key_col, val_row, acc_ref, window):
        kts = [jnp.clip(kt, 0, n_tiles - 1) for kt, _ in tiles]
        starts = [pl.multiple_of(kt * qb, qb) for kt in kts]
        k_t = jnp.concatenate([keys_ref[0, pl.ds(r0, qb), key_col:key_col + kvd] for r0 in starts], axis=0)
        dist = jnp.concatenate([jnp.where(active, q_pos - (r0 + key_row), -1)
                                for r0, (_, active) in zip(starts, tiles)], axis=0)
        in_range = dist >= 0
        scores = [_dot(k_t, qt_scr[k]) for k in range(kvh)]
        new, updates = [], []
        for k in range(kvh):
            if window:
                valid = in_range & (dist < WINDOW)
            else:
                pieces = []
                for kt in kts:
                    chosen = jnp.zeros((qb, qb), F32)
                    for j in range(per_tile):
                        row = sel_scr[k, pl.ds(per_tile * kt + j, 1), :]
                        chosen = jnp.where(key_row // SEL_BLOCK == j, row, chosen)
                    pieces.append(chosen)
                valid = in_range & (jnp.concatenate(pieces, axis=0) > 0.5)
            ms, ls = carry[k]
            ms2, ls2, alphas, probs = [], [], [], []
            for gi in range(g):
                bias = jnp.concatenate([btile_ref[jnp.maximum(qi - kt, 0), k * g + gi] for kt in kts], axis=0)
                s = jnp.where(valid, scores[k][:, gi * qb:(gi + 1) * qb] + bias, NEG)
                m_new = jnp.maximum(ms[gi], jnp.max(s, 0, keepdims=True))
                alpha = jnp.exp(ms[gi] - m_new)
                p = jnp.exp(s - jnp.where(m_new == NEG, 0.0, m_new))
                ms2.append(m_new)
                ls2.append(alpha * ls[gi] + jnp.sum(p, 0, keepdims=True))
                alphas.append(alpha)
                probs.append(p.astype(BF16))
            new.append((ms2, ls2))
            updates.append((jnp.concatenate(alphas, axis=1), jnp.concatenate(probs, axis=1)))
        for k, (alpha, prob) in enumerate(updates):
            v_t = jnp.concatenate([vt_ref[kt, val_row + k * hd:val_row + (k + 1) * hd, :] for kt in kts], axis=1)
            acc_ref[k] = acc_ref[k] * alpha + _dot(v_t, prob)
        return tuple(new)

    acc_s[...] = jnp.zeros_like(acc_s)
    acc_w[...] = jnp.zeros_like(acc_w)
    sel_args = dict(key_col=0, val_row=0, acc_ref=acc_s, window=False)
    win_args = dict(key_col=kvd, val_row=kvd, acc_ref=acc_w, window=True)

    def sel_pair(i, carry):
        return tile_step([(2 * i, 2 * i <= qi), (2 * i + 1, 2 * i + 1 <= qi)], carry, **sel_args)

    stat_s = lax.fori_loop(0, qi // 2 + 1, sel_pair, (first,) * kvh)
    stat_w = (first,) * kvh
    win_tiles = [(qi - j, qi - j >= 0) for j in range(WINDOW // qb, -1, -1)]
    for j in range(0, len(win_tiles), 2):
        stat_w = tile_step(win_tiles[j:j + 2], stat_w, **win_args)
    gt = gt_ref[0]
    for k in range(kvh):
        for gi in range(g):
            h = k * g + gi
            cols = slice(gi * qb, (gi + 1) * qb)
            l_s, l_w = stat_s[k][1][gi], stat_w[k][1][gi]
            o_s = acc_s[k, :, cols] / jnp.where(l_s > 0, l_s, 1.0)
            o_w = acc_w[k, :, cols] / jnp.where(l_w > 0, l_w, 1.0)
            out_scr[h * hd:(h + 1) * hd, :] = (gt[3 * h:3 * h + 1, :] * oc_scr[k, :, cols]
                                               + gt[3 * h + 1:3 * h + 2, :] * o_s + gt[3 * h + 2:3 * h + 3, :] * o_w)
    o_ref[0] = out_scr[...].T.astype(o_ref.dtype)


def nsa_prompt(qt, gt, kc, vct, keys, vt, rel_bias, b, s):
    qb = Q_BLOCK
    nq = s // qb
    nc = s // CMP_BLOCK
    ncp = kc.shape[1]
    nsb = s // SEL_BLOCK
    nsbp = _round_up(nsb, SUBLANES)
    assert SEL_RATIO * nsbp <= ncp
    n_sel = min(TOP_N, nsb)
    nh = NSA_HEADS
    hq = qt.shape[1]
    kvd = kc.shape[2]
    glanes = NSA_GROUP * qb
    cend = jnp.arange(ncp)[:, None] * CMP_BLOCK + (CMP_BLOCK - 1)
    biasc = jnp.transpose(_bias_of(rel_bias, jnp.arange(s)[None, :] - cend), (2, 0, 1))
    r = jnp.arange(qb)
    dist = jnp.arange(nq)[:, None, None] * qb + r[None, None, :] - r[None, :, None]
    btile = jnp.transpose(_bias_of(rel_bias, dist), (0, 3, 1, 2))
    out = pl.pallas_call(
        functools.partial(_nsa_prompt_kernel, nc=nc, nsb=nsb, n_sel=n_sel), grid=(b, nq),
        in_specs=[pl.BlockSpec((1, hq, qb), lambda bi, i: (bi * nq + i, 0, 0)),
                  pl.BlockSpec((1, LANES, qb), lambda bi, i: (bi * nq + i, 0, 0)),
                  pl.BlockSpec((1, ncp, kvd), lambda bi, i: (bi, 0, 0)),
                  pl.BlockSpec((1, kvd, ncp), lambda bi, i: (bi, 0, 0)),
                  pl.BlockSpec((1, s, 2 * kvd), lambda bi, i: (bi, 0, 0)),
                  pl.BlockSpec((nq, 2 * kvd, qb), lambda bi, i: (bi, 0, 0)),
                  pl.BlockSpec((nh, ncp, qb), lambda bi, i: (0, 0, i)),
                  _full(btile.shape)],
        out_specs=pl.BlockSpec((1, qb, hq), lambda bi, i: (bi, i, 0)),
        out_shape=jax.ShapeDtypeStruct((b, s, hq), BF16),
        scratch_shapes=[pltpu.VMEM((NSA_KV_HEADS, kvd, glanes), BF16), pltpu.VMEM((NSA_KV_HEADS, NSA_HD, glanes), F32),
                        pltpu.VMEM((NSA_KV_HEADS, NSA_HD, glanes), F32), pltpu.VMEM((NSA_KV_HEADS, NSA_HD, glanes), F32),
                        pltpu.VMEM((ncp, qb), F32), pltpu.VMEM((NSA_KV_HEADS, nsbp, qb), F32),
                        pltpu.VMEM((hq, qb), F32)],
        compiler_params=_cparams(("parallel", "arbitrary"), VMEM_LIMIT), name="nsa_prompt",
    )(qt, gt, kc, vct, keys.reshape(b, s, 2 * kvd), vt, biasc, btile)
    return out.reshape(b * s, hq)


def _round_up(x, m):
    return (x + m - 1) // m * m


def prep_even(w_in, pe, wc):
    d, n = w_in.shape
    n_pad = _round_up(n - 3 * NSA_HEADS, LANES) + LANES
    w_pad = jnp.zeros((d, n_pad), BF16).at[:, :n].set(w_in.astype(BF16))
    hd = NSA_HD
    pe2 = jnp.tile(pe, (1, 1, NSA_KV_HEADS))
    zero = jnp.zeros_like(wc)
    wbd = jnp.concatenate([jnp.concatenate([wc if i == j else zero for j in range(NSA_KV_HEADS)], axis=-1)
                           for i in range(NSA_KV_HEADS)], axis=-2)
    return dict(w_in=w_pad, pe2=pe2, wbd=wbd.astype(BF16))


def _memkv_kernel(x_ref, g_ref, w_ref, o_ref, ob_ref):
    y = _dot(_rms(x_ref[...], g_ref[...]).astype(BF16), w_ref[...])
    o_ref[...] = y
    ob_ref[...] = y.astype(BF16)


def memkv(mem, g, w):
    m, d = mem.shape
    n = w.shape[1]
    tm = _row_tile(m, 256)
    return pl.pallas_call(
        _memkv_kernel, grid=(m // tm,),
        in_specs=[pl.BlockSpec((tm, d), lambda i: (i, 0)), _full((1, d)), _full(w.shape)],
        out_specs=[pl.BlockSpec((tm, n), lambda i: (i, 0))] * 2,
        out_shape=[jax.ShapeDtypeStruct((m, n), F32), jax.ShapeDtypeStruct((m, n), BF16)],
        compiler_params=_cparams(("parallel",)), name="memkv",
    )(mem, g.reshape(1, d), w)


def _xattn_core(q, kv, hd):
    nh = q.shape[1] // hd
    outs = []
    for h in range(nh):
        s = _dot_nt(q[:, h * hd:(h + 1) * hd], kv[:, h * hd:(h + 1) * hd])
        e = jnp.exp(s - jnp.max(s, -1, keepdims=True))
        p = e / jnp.sum(e, -1, keepdims=True)
        outs.append(_dot(p.astype(BF16), kv[:, (nh + h) * hd:(nh + h + 1) * hd]))
    return jnp.concatenate(outs, axis=1).astype(BF16)


def _xattn_prompt_kernel(x_ref, g_ref, wq_ref, kv_ref, wo_ref, o_ref, *, hd):
    x = x_ref[0]
    q = (_dot(_rms(x, g_ref[...]).astype(BF16), wq_ref[...]) * (hd ** -0.5)).astype(BF16)
    o = _xattn_core(q, kv_ref[0], hd)
    o_ref[0] = x + _dot(o, wo_ref[...])


def xattn_prompt(x, g, wq, kvb, wo, b, s):
    d = x.shape[1]
    mt = kvb.shape[0] // b
    tm = _row_tile(s, 512)
    out = pl.pallas_call(
        functools.partial(_xattn_prompt_kernel, hd=d // X_HEADS), grid=(b, s // tm),
        in_specs=[pl.BlockSpec((1, tm, d), lambda bi, i: (bi, i, 0)), _full((1, d)), _full(wq.shape),
                  pl.BlockSpec((1, mt, kvb.shape[1]), lambda bi, i: (bi, 0, 0)), _full(wo.shape)],
        out_specs=pl.BlockSpec((1, tm, d), lambda bi, i: (bi, i, 0)),
        out_shape=jax.ShapeDtypeStruct((b, s, d), F32),
        compiler_params=_cparams(("parallel", "parallel"), VMEM_LIMIT), name="xattn_prompt",
    )(x.reshape(b, s, d), g.reshape(1, d), wq, kvb.reshape(b, mt, -1), wo)
    return out.reshape(b * s, d)


def _xattn_sample_kernel(x_ref, g_ref, wq_ref, kv_ref, wo_ref, o_ref, q_scr, a_scr, *, hd):
    bi = pl.program_id(0)
    nb = pl.num_programs(0)
    nh = wq_ref.shape[1] // hd

    @pl.when(bi == 0)
    def _():
        q_scr[...] = _dot(_rms(x_ref[...], g_ref[...]).astype(BF16), wq_ref[...]) * (hd ** -0.5)

    q = jnp.broadcast_to(q_scr[pl.ds(bi, 1), :], (SUBLANES, q_scr.shape[1])).astype(BF16)
    kv = kv_ref[0].astype(BF16)
    a_scr[pl.ds(bi, 1), :] = _xattn_core(q, kv, hd)[0:1, :].astype(F32)

    @pl.when(bi == nb - 1)
    def _():
        o_ref[...] = x_ref[...] + _dot(a_scr[...].astype(BF16), wo_ref[...])


def xattn_sample(x, g, wq, kv_cache, wo, layer):
    bd, d = x.shape
    mt, n = kv_cache.shape[1:]
    return pl.pallas_call(
        functools.partial(_xattn_sample_kernel, hd=d // X_HEADS), grid=(bd,),
        in_specs=[_full((bd, d)), _full((1, d)), _full(wq.shape),
                  pl.BlockSpec((1, mt, n), lambda bi: (layer * bd + bi, 0, 0)), _full(wo.shape)],
        out_specs=_full((bd, d)),
        out_shape=jax.ShapeDtypeStruct((bd, d), F32),
        scratch_shapes=[pltpu.VMEM((bd, wq.shape[1]), F32), pltpu.VMEM((bd, wq.shape[1]), F32)],
        compiler_params=_cparams(("arbitrary",), VMEM_LIMIT), name="xattn_sample",
    )(x, g.reshape(1, d), wq, kv_cache, wo)


def _ffn_kernel(x_ref, g_ref, wg_ref, wu_ref, wd_ref, o_ref, h_scr, acc_scr):
    c = pl.program_id(1)

    @pl.when(c == 0)
    def _():
        h_scr[...] = _rms(x_ref[...], g_ref[...]).astype(BF16)
        acc_scr[...] = x_ref[...]

    h = h_scr[...]
    gate = _dot(h, wg_ref[...])
    up = _dot(h, wu_ref[...])
    act = (gate * jax.nn.sigmoid(gate) * up).astype(BF16)
    acc_scr[...] += _dot(act, wd_ref[...])

    @pl.when(c == pl.num_programs(1) - 1)
    def _():
        o_ref[...] = acc_scr[...]


def _ff_chunk(dff, pref):
    c = dff
    for n in range(1, dff // LANES + 1):
        if dff % n == 0 and (dff // n) % LANES == 0 and dff // n <= pref:
            c = dff // n
            break
    return c


def ffn(x, g, w_gu, w_dn):
    m, d = x.shape
    dff = w_dn.shape[0]
    tm = _row_tile(m, 512)
    fc = _ff_chunk(dff, 1408)
    nch = dff // fc
    return pl.pallas_call(
        _ffn_kernel, grid=(m // tm, nch),
        in_specs=[pl.BlockSpec((tm, d), lambda i, c: (i, 0)), _full((1, d)),
                  pl.BlockSpec((d, fc), lambda i, c: (0, c)),
                  pl.BlockSpec((d, fc), lambda i, c: (0, nch + c)),
                  pl.BlockSpec((fc, d), lambda i, c: (c, 0))],
        out_specs=pl.BlockSpec((tm, d), lambda i, c: (i, 0)),
        out_shape=jax.ShapeDtypeStruct((m, d), F32),
        scratch_shapes=[pltpu.VMEM((tm, d), BF16), pltpu.VMEM((tm, d), F32)],
        compiler_params=_cparams(("parallel", "arbitrary"), VMEM_LIMIT), name="ffn",
    )(x, g.reshape(1, d), w_gu, w_gu, w_dn)


def _router_kernel(x_ref, g_ref, w_ref, b_ref, comb_ref, h_ref, mask_ref, cnt_ref, *, ne):
    h = _rms(x_ref[...], g_ref[...]).astype(BF16)
    h_ref[...] = h
    logits = _dot(h, w_ref[...]) + b_ref[...]
    lane = lax.broadcasted_iota(jnp.int32, logits.shape, 1)
    logits = jnp.where(lane < ne, logits, -jnp.inf)
    v1 = jnp.max(logits, -1, keepdims=True)
    i1 = jnp.min(jnp.where(logits == v1, lane, LANES), -1, keepdims=True)
    rest = jnp.where(lane == i1, -jnp.inf, logits)
    v2 = jnp.max(rest, -1, keepdims=True)
    i2 = jnp.min(jnp.where(rest == v2, lane, LANES), -1, keepdims=True)
    e2 = jnp.exp(v2 - v1)
    den = 1.0 + e2
    comb_ref[...] = jnp.where(lane == i1, 1.0 / den, 0.0) + jnp.where(lane == i2, e2 / den, 0.0)
    chosen = jnp.where((lane == i1) | (lane == i2), 1.0, 0.0)
    mask_ref[...] = chosen.astype(BF16)

    @pl.when(pl.program_id(0) == 0)
    def _():
        cnt_ref[...] = jnp.zeros_like(cnt_ref)

    cnt_ref[0:1, :] += jnp.sum(chosen, axis=0, keepdims=True)


def router(x, g, w_r, b_r):
    m, d = x.shape
    ne = w_r.shape[1]
    w_pad = jnp.zeros((d, LANES), BF16).at[:, :ne].set(w_r.astype(BF16))
    b_pad = jnp.zeros((1, LANES), F32).at[0, :ne].set(b_r.astype(F32))
    tm = _row_tile(m, 512)
    row = lambda n: pl.BlockSpec((tm, n), lambda i: (i, 0))
    return pl.pallas_call(
        functools.partial(_router_kernel, ne=ne), grid=(m // tm,),
        in_specs=[row(d), _full((1, d)), _full((d, LANES)), _full((1, LANES))],
        out_specs=[row(LANES), row(d), row(LANES), _full((SUBLANES, LANES))],
        out_shape=[jax.ShapeDtypeStruct((m, LANES), F32), jax.ShapeDtypeStruct((m, d), BF16),
                   jax.ShapeDtypeStruct((m, LANES), BF16), jax.ShapeDtypeStruct((SUBLANES, LANES), F32)],
        compiler_params=_cparams(("arbitrary",)), name="router",
    )(x, g.reshape(1, d), w_pad, b_pad)


def _residual_out(y, gain_ref, norm):
    return _rms(y, gain_ref[...]) if norm else y


def _moe_kernel(x_ref, h_ref, comb_ref, wg_ref, wu_ref, wd_ref, gain_ref, o_ref, acc_scr, *, norm):
    e = pl.program_id(1)

    @pl.when(e == 0)
    def _():
        acc_scr[...] = jnp.zeros_like(acc_scr)

    h = h_ref[...]
    gate = _dot(h, wg_ref[0])
    up = _dot(h, wu_ref[0])
    act = (gate * jax.nn.sigmoid(gate) * up).astype(BF16)
    y = _dot(act, wd_ref[0])
    comb = comb_ref[...]
    lane = lax.broadcasted_iota(jnp.int32, comb.shape, 1)
    acc_scr[...] += jnp.sum(jnp.where(lane == e, comb, 0.0), -1, keepdims=True) * y

    @pl.when(e == pl.num_programs(1) - 1)
    def _():
        o_ref[...] = _residual_out(x_ref[...] + acc_scr[...], gain_ref, norm)


def moe(x, h, comb, w_gu, w_dn, final_g=None):
    m, d = x.shape
    ne, dfe = w_dn.shape[:2]
    tm = _row_tile(m, 512)
    gain = jnp.ones((1, d), F32) if final_g is None else final_g.reshape(1, d)
    return pl.pallas_call(
        functools.partial(_moe_kernel, norm=final_g is not None), grid=(m // tm, ne),
        in_specs=[pl.BlockSpec((tm, d), lambda i, e: (i, 0)), pl.BlockSpec((tm, d), lambda i, e: (i, 0)),
                  pl.BlockSpec((tm, LANES), lambda i, e: (i, 0)),
                  pl.BlockSpec((1, d, dfe), lambda i, e: (e, 0, 0)),
                  pl.BlockSpec((1, d, dfe), lambda i, e: (e, 0, 1)),
                  pl.BlockSpec((1, dfe, d), lambda i, e: (e, 0, 0)), _full((1, d))],
        out_specs=pl.BlockSpec((tm, d), lambda i, e: (i, 0)),
        out_shape=jax.ShapeDtypeStruct((m, d), F32),
        scratch_shapes=[pltpu.VMEM((tm, d), F32)],
        compiler_params=_cparams(("parallel", "arbitrary"), VMEM_LIMIT), name="moe",
    )(x, h, comb, w_gu, w_gu, w_dn, gain)


MOE_TILE = 256


def _moe_pos_kernel(mask_ref, comb_ref, tri_ref, base_ref, post_ref, pos2_ref, wab_ref, stab_ref, run_scr, *, nep):
    sb = pl.program_id(0)
    nb = pl.num_programs(0)

    @pl.when(sb == 0)
    def _():
        run_scr[...] = jnp.zeros_like(run_scr)
        stab_ref[...] = jnp.zeros_like(stab_ref)

    a = mask_ref[...]
    af = a.astype(F32)
    start = base_ref[...] + run_scr[...]
    stab_ref[pl.ds(sb, 1), :] = start.astype(jnp.int32)
    rank = _dot(tri_ref[...], a)
    pos = jnp.where(af > 0, start + rank, -1.0)
    lane = lax.broadcasted_iota(jnp.int32, pos.shape, 1)
    first_e = jnp.min(jnp.where(af > 0, lane, LANES), -1, keepdims=True)
    last_e = jnp.max(jnp.where(af > 0, lane, -1), -1, keepdims=True)
    comb = comb_ref[...]
    w_a = jnp.sum(jnp.where(lane == first_e, comb, 0.0), -1, keepdims=True)
    w_b = jnp.sum(jnp.where(lane == last_e, comb, 0.0), -1, keepdims=True)
    wab_ref[...] = jnp.where(lane == 0, w_a, jnp.where(lane == 1, w_b, 0.0))
    pos_t = pos.T[0:nep, :]
    post_ref[0] = pos_t.astype(jnp.int32)
    row = lax.broadcasted_iota(jnp.int32, pos_t.shape, 0)
    first_r = jnp.min(jnp.where(pos_t >= 0, row, nep), 0, keepdims=True)
    last_r = jnp.max(jnp.where(pos_t >= 0, row, -1), 0, keepdims=True)
    pos_a = jnp.sum(jnp.where(row == first_r, pos_t, 0.0), 0, keepdims=True)
    pos_b = jnp.sum(jnp.where(row == last_r, pos_t, 0.0), 0, keepdims=True)
    pos2_ref[0] = jnp.where(row == 0, pos_a, jnp.where(row == 1, pos_b, 0.0)).astype(jnp.int32)
    run_scr[...] += jnp.sum(af, axis=0, keepdims=True)

    @pl.when(sb == nb - 1)
    def _():
        stab_ref[pl.ds(nb, 1), :] = (base_ref[...] + run_scr[...]).astype(jnp.int32)


def _moe_expert_kernel(te_ref, lo_ref, cnt_ref, ring_ref, nt_ref, h_ref, post_ref, wg_ref, wu_ref, wd_ref, y_ref,
                       hbuf, xg_scr, sem, *, t):
    i = pl.program_id(0)

    @pl.when(i >= nt_ref[0])
    def _():
        y_ref[...] = jnp.zeros_like(y_ref)

    nbuf = hbuf.shape[0]
    ahead = nbuf - 1

    def copy(sb, slot):
        return pltpu.make_async_copy(h_ref.at[pl.ds(pl.multiple_of(sb * t, t), t), :], hbuf.at[slot], sem.at[slot])

    def start_first(tile):
        for j in range(ahead):
            @pl.when(j < cnt_ref[tile])
            def _():
                copy(lo_ref[tile] + j, (ring_ref[tile] + j) % nbuf).start()

    @pl.when(i == 0)
    def _():
        start_first(0)

    @pl.when(i < nt_ref[0])
    def _():
        e = te_ref[i]
        lo = lo_ref[i]
        n = cnt_ref[i]
        ring = ring_ref[i]
        xg_scr[...] = jnp.zeros_like(xg_scr)
        row = i * t + lax.broadcasted_iota(jnp.int32, (t, 1), 0)

        def body(j, c):
            slot = (ring + j) % nbuf
            copy(lo + j, slot).wait()

            @pl.when(j + ahead < n)
            def _():
                copy(lo + j + ahead, (ring + j + ahead) % nbuf).start()

            src_pos = post_ref[lo + j, pl.ds(e, 1), :]
            onehot = jnp.where(src_pos == row, 1.0, 0.0).astype(BF16)
            xg_scr[...] += _dot(onehot, hbuf[slot])
            return c

        lax.fori_loop(0, n, body, 0)

        @pl.when(i + 1 < nt_ref[0])
        def _():
            start_first(i + 1)

        x = xg_scr[...].astype(BF16)
        gate = _dot(x, wg_ref[0])
        up = _dot(x, wu_ref[0])
        act = (gate * jax.nn.sigmoid(gate) * up).astype(BF16)
        y_ref[...] = _dot(act, wd_ref[0])


def _moe_combine_kernel(x_ref, wab_ref, pos_ref, nxt_ref, gain_ref, ys_ref, o_ref, ybuf, sem, *, t, norm):
    sb = pl.program_id(0)
    nb = pl.num_programs(0)

    def row_copy(p_ref, which, slot, tok):
        return pltpu.make_async_copy(ys_ref.at[pl.ds(p_ref[0, which, tok], 1), :],
                                     ybuf.at[slot, which, pl.ds(tok, 1), :], sem.at[slot])

    def start_all(p_ref, slot):
        def body(tok, c):
            row_copy(p_ref, 0, slot, tok).start()
            row_copy(p_ref, 1, slot, tok).start()
            return c
        lax.fori_loop(0, t, body, 0, unroll=8)

    @pl.when(sb == 0)
    def _():
        start_all(pos_ref, 0)

    @pl.when(sb + 1 < nb)
    def _():
        start_all(nxt_ref, (sb + 1) % 2)

    slot = sb % 2

    def wait_body(tok, c):
        row_copy(pos_ref, 0, slot, tok).wait()
        row_copy(pos_ref, 1, slot, tok).wait()
        return c
    lax.fori_loop(0, t, wait_body, 0, unroll=8)
    w = wab_ref[...]
    y = x_ref[...] + w[:, 0:1] * ybuf[slot, 0] + w[:, 1:2] * ybuf[slot, 1]
    o_ref[...] = _residual_out(y, gain_ref, norm)


def moe_grouped(x, h, comb, mask, counts, w_gu, w_dn, final_g=None):
    m, d = x.shape
    ne, dfe = w_dn.shape[:2]
    t = MOE_TILE
    assert m % t == 0
    nb = m // t
    nep = _round_up(ne, SUBLANES)
    nbp = _round_up(nb + 1, SUBLANES)
    k_top = TOP_K
    nt_max = k_top * m // t + ne
    cnt = counts[0, :ne].astype(jnp.int32)
    cnt_pad = (cnt + t - 1) // t * t
    ends = jnp.cumsum(cnt_pad)
    base = ends - cnt_pad
    base_row = jnp.zeros((1, LANES), F32).at[0, :ne].set(base.astype(F32))
    idx = lax.broadcasted_iota(jnp.int32, (t, t), 0)
    tri = jnp.where(lax.broadcasted_iota(jnp.int32, (t, t), 1) < idx, 1.0, 0.0).astype(BF16)
    blk = lambda n2: pl.BlockSpec((t, n2), lambda i: (i, 0))
    post, pos2, wab, stab = pl.pallas_call(
        functools.partial(_moe_pos_kernel, nep=nep), grid=(nb,),
        in_specs=[blk(LANES), blk(LANES), _full((t, t)), _full((1, LANES))],
        out_specs=[pl.BlockSpec((1, nep, t), lambda i: (i, 0, 0)), pl.BlockSpec((1, nep, t), lambda i: (i, 0, 0)),
                   blk(LANES), _full((nbp, LANES))],
        out_shape=[jax.ShapeDtypeStruct((nb, nep, t), jnp.int32), jax.ShapeDtypeStruct((nb, nep, t), jnp.int32),
                   jax.ShapeDtypeStruct((m, LANES), F32), jax.ShapeDtypeStruct((nbp, LANES), jnp.int32)],
        scratch_shapes=[pltpu.VMEM((1, LANES), F32)],
        compiler_params=_cparams(("arbitrary",)), name="moe_positions",
    )(mask, comb, tri, base_row)
    r0 = jnp.arange(nt_max, dtype=jnp.int32) * t
    tile_e = jnp.minimum(jnp.sum(ends[None, :] <= r0[:, None], axis=1), ne - 1).astype(jnp.int32)
    n_tiles = (ends[-1] // t).astype(jnp.int32).reshape(1)
    s_e = stab[:nb + 1, :ne][:, tile_e]
    lo = jnp.sum(s_e[1:] <= r0[None, :], axis=0)
    hi = jnp.sum(s_e[:nb] < r0[None, :] + t, axis=0) - 1
    lo = jnp.clip(lo, 0, nb - 1).astype(jnp.int32)
    hi = jnp.clip(hi, lo, nb - 1).astype(jnp.int32)
    n_src = jnp.where(jnp.arange(nt_max) < n_tiles[0], hi - lo + 1, 0).astype(jnp.int32)
    n_ring = 3
    ring = ((jnp.cumsum(n_src) - n_src) % n_ring).astype(jnp.int32)
    w_spec = lambda shape, col: pl.BlockSpec(shape, lambda i, te, *_: (te[i], 0, col))
    grid_spec = pltpu.PrefetchScalarGridSpec(
        num_scalar_prefetch=5, grid=(nt_max,),
        in_specs=[pl.BlockSpec(memory_space=pl.ANY),
                  pl.BlockSpec((nb, nep, t), lambda i, *_: (0, 0, 0)),
                  w_spec((1, d, dfe), 0), w_spec((1, d, dfe), 1), w_spec((1, dfe, d), 0)],
        out_specs=pl.BlockSpec((t, d), lambda i, *_: (i, 0)),
        scratch_shapes=[pltpu.VMEM((n_ring, t, d), BF16), pltpu.VMEM((t, d), F32),
                        pltpu.SemaphoreType.DMA((n_ring,))])
    ys = pl.pallas_call(
        functools.partial(_moe_expert_kernel, t=t), grid_spec=grid_spec,
        out_shape=jax.ShapeDtypeStruct((nt_max * t, d), F32),
        compiler_params=_cparams(("arbitrary",), VMEM_LIMIT), name="moe_experts",
    )(tile_e, lo, n_src, ring, n_tiles, h, post, w_gu, w_gu, w_dn)
    smem_blk = lambda f: pl.BlockSpec((1, nep, t), f, memory_space=pltpu.SMEM)
    gain = jnp.ones((1, d), F32) if final_g is None else final_g.reshape(1, d)
    return pl.pallas_call(
        functools.partial(_moe_combine_kernel, t=t, norm=final_g is not None), grid=(nb,),
        in_specs=[blk(d), blk(LANES), smem_blk(lambda i: (i, 0, 0)),
                  smem_blk(lambda i: (jnp.minimum(i + 1, nb - 1), 0, 0)), _full((1, d)),
                  pl.BlockSpec(memory_space=pl.ANY)],
        out_specs=blk(d),
        out_shape=jax.ShapeDtypeStruct((m, d), F32),
        scratch_shapes=[pltpu.VMEM((2, 2, t, d), F32), pltpu.SemaphoreType.DMA((2,))],
        compiler_params=_cparams(("arbitrary",), VMEM_LIMIT), name="moe_combine",
    )(x, wab, pos2, pos2, gain, ys)


def _inproj_odd_kernel(x_ref, g_ref, w_ref, bias_ref, q_ref, k_ref, v_ref, og_ref, gc_ref, gr_ref, *, hq, hv, nh):
    xn = _rms(x_ref[...], g_ref[...]).astype(BF16)

    def mm(lo, hi):
        return _dot(xn, w_ref[:, lo:hi])

    dk = hq // nh
    q_ref[...] = mm(0, hq).astype(BF16)
    k_ref[...] = (mm(hq, 2 * hq) * (dk ** -0.5)).astype(BF16)
    v_ref[...] = mm(2 * hq, 2 * hq + hv).astype(BF16)
    og_ref[...] = jax.nn.sigmoid(mm(2 * hq + hv, 2 * hq + 2 * hv))
    gi = mm(2 * hq + 2 * hv, 2 * hq + 2 * hv + LANES) + bias_ref[...]
    lane = lax.broadcasted_iota(jnp.int32, gi.shape, 1)
    gates = jnp.where(lane < nh, gi, jax.nn.log_sigmoid(gi))
    gc_ref[...] = gates
    gr_ref[...] = gates.T[0:SUBLANES, :]


def inproj_odd(x, g, w_pad, gate_bias):
    m, d = x.shape
    nh = MLSTM_HEADS
    hq = hv = d
    tm = _row_tile(m, 256)
    row = lambda n: pl.BlockSpec((tm, n), lambda i: (i, 0))
    outs = [(hq, BF16), (hq, BF16), (hv, BF16), (hv, F32), (LANES, F32)]
    if tm % LANES:
        gr_spec = _full((SUBLANES, m))
    else:
        gr_spec = pl.BlockSpec((SUBLANES, tm), lambda i: (0, i))
    return pl.pallas_call(
        functools.partial(_inproj_odd_kernel, hq=hq, hv=hv, nh=nh), grid=(m // tm,),
        in_specs=[row(d), _full((1, d)), _full(w_pad.shape), _full((1, LANES))],
        out_specs=[row(n) for n, _ in outs] + [gr_spec],
        out_shape=[jax.ShapeDtypeStruct((m, n), t) for n, t in outs] + [jax.ShapeDtypeStruct((SUBLANES, m), F32)],
        compiler_params=_cparams(("parallel",), VMEM_LIMIT), name="inproj_odd",
    )(x, g.reshape(1, d), w_pad, gate_bias)


def prep_odd(w_in, b_i, b_f):
    d, n = w_in.shape
    n_pad = _round_up(n - 2 * MLSTM_HEADS, LANES) + LANES
    w_pad = jnp.zeros((d, n_pad), BF16).at[:, :n].set(w_in.astype(BF16))
    bias = jnp.zeros((1, LANES), F32).at[0, :2 * MLSTM_HEADS].set(jnp.concatenate([b_i, b_f]).astype(F32))
    return dict(w_in=w_pad, bias=bias)


def _mlstm_prompt_kernel(q_ref, k_ref, v_ref, og_ref, gc_ref, gr_ref, gain_ref, hn_ref, c_ref, n_ref, m_ref,
                         *, nh, dk, dv, ln):
    ci = pl.program_id(1)

    @pl.when(ci == 0)
    def _():
        c_ref[...] = jnp.zeros_like(c_ref)
        n_ref[...] = jnp.zeros_like(n_ref)
        m_ref[...] = jnp.full(m_ref.shape, NEG, F32)

    row = lax.broadcasted_iota(jnp.int32, (ln, ln), 0)
    col = lax.broadcasted_iota(jnp.int32, (ln, ln), 1)
    tri = row >= col
    gc = gc_ref[...]
    gr = gr_ref[...]
    for h in range(nh):
        q = q_ref[:, h * dk:(h + 1) * dk]
        k = k_ref[:, h * dk:(h + 1) * dk]
        v = v_ref[:, h * dv:(h + 1) * dv]
        ig_c, lf_c = gc[:, h:h + 1], gc[:, nh + h:nh + h + 1]
        ig_r, lf_r = gr[h:h + 1, :], gr[nh + h:nh + h + 1, :]
        b_c = jnp.sum(jnp.where(tri, lf_r, 0.0), axis=1, keepdims=True)
        b_r = jnp.sum(jnp.where(row <= col, lf_c, 0.0), axis=0, keepdims=True)
        m_prev = m_ref[0, h:h + 1, 0:1]
        c_prev = c_ref[0, h]
        n_prev = n_ref[0, h:h + 1, :]
        dmat = jnp.where(tri, b_c - b_r + ig_r, NEG)
        inter = b_c + m_prev
        mt = jnp.maximum(inter, jnp.max(dmat, -1, keepdims=True))
        wm = jnp.exp(dmat - mt)
        a = jnp.exp(inter - mt)
        wqk = wm * _dot_nt(q, k)
        num = a * _dot_nt(q, c_prev.astype(BF16)) + _dot(wqk.astype(BF16), v)
        den = a * jnp.sum(q.astype(F32) * n_prev, -1, keepdims=True) + jnp.sum(wqk, -1, keepdims=True)
        hh = num / jnp.maximum(jnp.abs(den), jnp.exp(-mt))
        b_end = b_c[ln - 1:ln, :]
        m_new = mt[ln - 1:ln, :]
        a_end = jnp.exp(b_end + m_prev - m_new)
        w_s = jnp.exp(b_end - b_c + ig_c - m_new)
        c_ref[0, h] = a_end * c_prev + _dot_tn((v.astype(F32) * w_s).astype(BF16), k)
        n_ref[0, h:h + 1, :] = a_end * n_prev + jnp.sum(w_s * k.astype(F32), axis=0, keepdims=True)
        m_ref[0, h:h + 1, :] = jnp.broadcast_to(m_new, (1, m_ref.shape[2]))
        hn = hh * lax.rsqrt(jnp.mean(hh * hh, -1, keepdims=True) + RMS_EPS)
        hn = hn * gain_ref[:, h * dv:(h + 1) * dv] * og_ref[:, h * dv:(h + 1) * dv]
        hn_ref[:, h * dv:(h + 1) * dv] = hn.astype(hn_ref.dtype)


def mlstm_prompt(q, k, v, og, gc, gr, gain, b, s):
    m, d = q.shape
    nh = MLSTM_HEADS
    dk = dv = d // nh
    ln = _row_tile(s, MLSTM_CHUNK)
    nch = s // ln
    row = lambda n: pl.BlockSpec((ln, n), lambda bi, ci: (bi * nch + ci, 0))
    return pl.pallas_call(
        functools.partial(_mlstm_prompt_kernel, nh=nh, dk=dk, dv=dv, ln=ln), grid=(b, nch),
        in_specs=[row(d), row(d), row(d), row(d), row(LANES),
                  pl.BlockSpec((SUBLANES, ln), lambda bi, ci: (0, bi * nch + ci)), _full((1, d))],
        out_specs=[row(d), pl.BlockSpec((1, nh, dv, dk), lambda bi, ci: (bi, 0, 0, 0)),
                   pl.BlockSpec((1, nh, dk), lambda bi, ci: (bi, 0, 0)),
                   pl.BlockSpec((1, nh, LANES), lambda bi, ci: (bi, 0, 0))],
        out_shape=[jax.ShapeDtypeStruct((m, d), BF16), jax.ShapeDtypeStruct((b, nh, dv, dk), F32),
                   jax.ShapeDtypeStruct((b, nh, dk), F32), jax.ShapeDtypeStruct((b, nh, LANES), F32)],
        compiler_params=_cparams(("parallel", "arbitrary"), VMEM_LIMIT), name="mlstm_prompt",
    )(q, k, v, og, gc, gr, gain.reshape(1, d))


def _mlstm_sample_kernel(q_ref, k_ref, v_ref, og_ref, g_ref, gain_ref, c_ref, n_ref, m_ref,
                         hn_ref, co_ref, no_ref, mo_ref, *, nh):
    row = lax.broadcasted_iota(jnp.int32, (SUBLANES, 1), 0)
    for h in range(nh):
        q = q_ref[0, h:h + 1, :]
        k = k_ref[0, h:h + 1, :]
        v = v_ref[0, h:h + 1, :].astype(F32)
        ig = g_ref[0, h:h + 1, 0:1]
        lf = g_ref[0, h:h + 1, 1:2]
        m_prev = m_ref[0, h:h + 1, :]
        c_prev = c_ref[0, h]
        n_prev = n_ref[0, h:h + 1, :]
        inter = lf + m_prev
        mt = jnp.maximum(inter, ig)
        wm = jnp.exp(ig - mt)
        a = jnp.exp(inter - mt)
        q8 = jnp.broadcast_to(q, (SUBLANES, q.shape[1]))
        cq = _dot_nt(q8, c_prev.astype(BF16))[0:1, :]
        wqk = wm * jnp.sum(q.astype(F32) * k.astype(F32), -1, keepdims=True)
        num = a * cq + wqk * v
        den = a * jnp.sum(n_prev * q.astype(F32), -1, keepdims=True) + wqk
        hh = num / jnp.maximum(jnp.abs(den), jnp.exp(-mt))
        v8 = jnp.where(row == 0, jnp.broadcast_to(v * wm, (SUBLANES, v.shape[1])), 0.0).astype(BF16)
        k8 = jnp.broadcast_to(k, (SUBLANES, k.shape[1]))
        co_ref[0, h] = a * c_prev + _dot_tn(v8, k8)
        no_ref[0, h:h + 1, :] = a * n_prev + wm * k.astype(F32)
        mo_ref[0, h:h + 1, :] = mt
        hn = hh * lax.rsqrt(jnp.mean(hh * hh, -1, keepdims=True) + RMS_EPS)
        hn_ref[0, h:h + 1, :] = (hn * gain_ref[h:h + 1, :] * og_ref[0, h:h + 1, :]).astype(hn_ref.dtype)


def mlstm_sample(q, k, v, og, gc, gain, c, n, m):
    bd, d = q.shape
    nh = MLSTM_HEADS
    dk = d // nh
    heads = lambda a: a.reshape(bd, nh, dk)
    g2 = jnp.transpose(gc[:, :2 * nh].reshape(bd, 2, nh), (0, 2, 1))
    blk3 = lambda n2: pl.BlockSpec((1, nh, n2), lambda bi: (bi, 0, 0))
    cspec = pl.BlockSpec((1, nh, dk, dk), lambda bi: (bi, 0, 0, 0))
    hn, co, no, mo = pl.pallas_call(
        functools.partial(_mlstm_sample_kernel, nh=nh), grid=(bd,),
        in_specs=[blk3(dk), blk3(dk), blk3(dk), blk3(dk), blk3(2), _full((nh, dk)), cspec, blk3(dk), blk3(1)],
        out_specs=[blk3(dk), cspec, blk3(dk), blk3(1)],
        out_shape=[jax.ShapeDtypeStruct((bd, nh, dk), BF16), jax.ShapeDtypeStruct(c.shape, F32),
                   jax.ShapeDtypeStruct(n.shape, F32), jax.ShapeDtypeStruct((bd, nh, 1), F32)],
        compiler_params=_cparams(("parallel",)), name="mlstm_sample",
    )(heads(q), heads(k), heads(v), heads(og), g2, gain.reshape(nh, dk), c, n, m.reshape(bd, nh, 1))
    return hn.reshape(bd, d), co, no, mo.reshape(bd, nh)


def _row_to_col(row):
    n = row.shape[1]
    eye = lax.broadcasted_iota(jnp.int32, (n, n), 0) == lax.broadcasted_iota(jnp.int32, (n, n), 1)
    return jnp.sum(jnp.where(eye, row, 0.0), axis=1, keepdims=True)


def _head_pad(q, keep):
    q2 = jnp.concatenate([q] * NSA_KV_HEADS, axis=1)
    row = lax.broadcasted_iota(jnp.int32, q2.shape, 0)
    lane = lax.broadcasted_iota(jnp.int32, q2.shape, 1)
    return jnp.where((lane // NSA_HD == row // NSA_GROUP) & keep(row), q2, jnp.zeros_like(q2))


def _nsa_sample_cmp_kernel(pt_ref, q_ref, pages_ref, pe_ref, w_ref, biasc_ref, oc_ref, idx_ref, xbuf, xrow, sem,
                           *, n_pages, nc, ncp, nsb, n_sel, past):
    b = pl.program_id(0)
    nb = pl.num_programs(0)
    kvd = NSA_KV_HEADS * NSA_HD

    def page_copy(bb, slot, p, sl):
        return pltpu.make_async_copy(pages_ref.at[pt_ref[bb * n_pages + p], pl.ds(sl * kvd, kvd), :],
                                     xbuf.at[slot, sl, p], sem.at[slot])

    def start_all(bb, slot):
        def body(p, c):
            page_copy(bb, slot, p, 0).start()
            page_copy(bb, slot, p, 1).start()
            return c
        lax.fori_loop(0, n_pages, body, 0)

    @pl.when(b == 0)
    def _():
        start_all(0, 0)

    @pl.when(b + 1 < nb)
    def _():
        start_all(b + 1, (b + 1) % 2)

    slot = b % 2

    def wait_body(p, c):
        page_copy(b, slot, p, 0).wait()
        page_copy(b, slot, p, 1).wait()
        return c
    lax.fori_loop(0, n_pages, wait_body, 0)

    def file_page(p, c):
        for sl in range(2):
            _file_rows(xbuf[slot, sl, p], xrow, sl, p)
        return c
    lax.fori_loop(0, n_pages, file_page, 0, unroll=8)

    acc = _compress_filed(xrow, pe_ref, w_ref, nc)
    kc = acc[:, 0:kvd].astype(BF16)
    vc = acc[:, kvd:2 * kvd].astype(BF16)
    q = q_ref[0]
    nh = q.shape[0]
    qpad = _head_pad(q, lambda r: r >= 0)
    s = _dot_nt(qpad, kc)
    s = s + biasc_ref[:, 0:nc]
    e = jnp.exp(s - jnp.max(s, -1, keepdims=True))
    p_c = e / jnp.sum(e, -1, keepdims=True)
    o = _dot(p_c.astype(BF16), vc)
    row = lax.broadcasted_iota(jnp.int32, (nh, NSA_HD), 0)
    o_h = o[:, 0:NSA_HD]
    for k in range(1, NSA_KV_HEADS):
        o_h = jnp.where(row // NSA_GROUP == k, o[:, k * NSA_HD:(k + 1) * NSA_HD], o_h)
    oc_ref[0] = o_h
    prow = lax.broadcasted_iota(jnp.int32, p_c.shape, 0)
    lane = lax.broadcasted_iota(jnp.int32, (1, ncp), 1)
    blk = lane // 2
    cur = past // SEL_BLOCK
    forced = (blk == 0) | (blk == cur) | (blk == cur - 1)
    is_cand = ((lane % 2) == 0) & (lane < 2 * nsb)
    nselp = idx_ref.shape[1]
    rsel = lax.broadcasted_iota(jnp.int32, (nselp, 1), 0).astype(F32)
    out_lane = lax.broadcasted_iota(jnp.int32, (nselp, LANES), 1)
    result = jnp.full((nselp, LANES), -1, jnp.int32)
    for k in range(NSA_KV_HEADS):
        imp = jnp.sum(jnp.where(prow // NSA_GROUP == k, p_c, 0.0), axis=0, keepdims=True)
        imp = jnp.concatenate([imp, jnp.zeros((1, ncp - nc), F32)], axis=1)
        imp = _pair_sum(imp)
        score = jnp.where(forced, FORCE_SCORE, jnp.where(blk <= cur, imp, -1.0))
        score = jnp.where(is_cand, score, -2.0)
        sel, rank = _select_blocks(score, n_sel, nsb)
        hit = (rank == rsel) & (sel > 0.5)
        idx = jnp.sum(jnp.where(hit, (blk + 1).astype(F32), 0.0), axis=1, keepdims=True) - 1.0
        result = jnp.where(out_lane == k, idx.astype(jnp.int32), result)
    idx_ref[0] = result


def nsa_sample_cmp(q8, pages, page_table, pe2, wbd, rel_bias):
    bd, nh, hd = q8.shape
    n_pages = page_table.shape[1]
    past = n_pages * PAGE_SIZE
    nc = past // CMP_BLOCK
    nsb = -(-(past + 1) // SEL_BLOCK)
    n_sel = min(TOP_N, nsb)
    ncp = _round_up(max(nc, SEL_RATIO * nsb), LANES)
    nselp = _round_up(n_sel, SUBLANES)
    kvd = wbd.shape[-1]
    cend = jnp.arange(nc) * CMP_BLOCK + (CMP_BLOCK - 1)
    biasc = jnp.zeros((nh, ncp), F32).at[:, :nc].set(_bias_of(rel_bias, past - cend).T)
    grid_spec = pltpu.PrefetchScalarGridSpec(
        num_scalar_prefetch=1, grid=(bd,),
        in_specs=[pl.BlockSpec((1, nh, hd), lambda bi, pt: (bi, 0, 0)),
                  pl.BlockSpec(memory_space=pl.ANY),
                  pl.BlockSpec(pe2.shape, lambda bi, pt: (0, 0, 0)),
                  pl.BlockSpec(wbd.shape, lambda bi, pt: (0, 0, 0, 0)),
                  pl.BlockSpec((nh, ncp), lambda bi, pt: (0, 0))],
        out_specs=[pl.BlockSpec((1, nh, hd), lambda bi, pt: (bi, 0, 0)),
                   pl.BlockSpec((1, nselp, LANES), lambda bi, pt: (bi, 0, 0))],
        scratch_shapes=[pltpu.VMEM((2, 2, n_pages, kvd, PAGE_SIZE), F32),
                        pltpu.VMEM((2, CMP_BLOCK // SUBLANES, nc * SUBLANES, kvd), F32),
                        pltpu.SemaphoreType.DMA((2,))])
    oc, idx = pl.pallas_call(
        functools.partial(_nsa_sample_cmp_kernel, n_pages=n_pages, nc=nc, ncp=ncp, nsb=nsb, n_sel=n_sel, past=past),
        grid_spec=grid_spec,
        out_shape=[jax.ShapeDtypeStruct((bd, nh, hd), F32), jax.ShapeDtypeStruct((bd, nselp, LANES), jnp.int32)],
        compiler_params=_cparams(("arbitrary",), VMEM_LIMIT), name="nsa_sample_cmp",
    )(page_table.reshape(-1), q8, pages, pe2, wbd, biasc)
    sel_idx = jnp.transpose(idx[:, :n_sel, :NSA_KV_HEADS], (0, 2, 1))
    return oc, sel_idx


def _nsa_sample_att_kernel(pt_ref, si_ref, q_ref, g_ref, oc_ref, kvn_ref, wn_ref, wc_ref, pages_ref,
                           bsel_ref, bwin_ref, ob_ref, win_ref, selbuf, wall, sem,
                           *, n_pages, n_sel, past, wb):
    b = pl.program_id(0)
    nb = pl.num_programs(0)
    kvd = NSA_KV_HEADS * NSA_HD
    hd = NSA_HD
    n_blk_pages = past // SEL_BLOCK
    per_page = PAGE_SIZE // SEL_BLOCK
    n_slots = NSA_KV_HEADS * n_sel

    def blk_of(bb, j):
        return si_ref[bb * n_slots + j]

    def blk_copy(bb, slot, j):
        blk = jnp.clip(blk_of(bb, j), 0, n_blk_pages - 1)
        page = pt_ref[bb * n_pages + blk // per_page]
        return pltpu.make_async_copy(pages_ref.at[page, pl.ds(2 * kvd, 2 * kvd), :], selbuf.at[slot, j], sem.at[slot])

    def in_pages(bb, j):
        blk = blk_of(bb, j)
        return (blk >= 0) & (blk < n_blk_pages)

    def start_all(bb, slot):
        def body(j, c):
            @pl.when(in_pages(bb, j))
            def _():
                blk_copy(bb, slot, j).start()
            return c
        lax.fori_loop(0, n_slots, body, 0)

    @pl.when(b == 0)
    def _():
        start_all(0, 0)

    @pl.when(b + 1 < nb)
    def _():
        start_all(b + 1, (b + 1) % 2)

    slot = b % 2
    new_sel = _row_to_col(kvn_ref[0][:, 2 * kvd:4 * kvd])
    lane = lax.broadcasted_iota(jnp.int32, (1, PAGE_SIZE), 1)

    def wait_body(j, c):
        @pl.when(in_pages(b, j))
        def _():
            blk_copy(b, slot, j).wait()

        @pl.when(jnp.logical_not(in_pages(b, j)))
        def _():
            is_new = blk_of(b, j) == n_blk_pages
            selbuf[slot, j] = jnp.where((lane == 0) & is_new, new_sel, 0.0)
        return c
    lax.fori_loop(0, n_slots, wait_body, 0)

    q = q_ref[0]
    nh = q.shape[0]
    gates = g_ref[0]
    head = lax.broadcasted_iota(jnp.int32, (nh, 1), 0)

    def attend(qp, keys_t, vals_t, bias, valid):
        s = jnp.where(valid, _dot(qp, keys_t) + bias, NEG)
        e = jnp.where(valid, jnp.exp(s - jnp.max(s, -1, keepdims=True)), 0.0)
        den = jnp.sum(e, -1, keepdims=True)
        p = e / jnp.where(den > 0, den, 1.0)
        return _dot_nt(p.astype(BF16), vals_t)

    o_s = jnp.zeros((nh, kvd), F32)
    for k in range(NSA_KV_HEADS):
        blks = [blk_of(b, k * n_sel + r) for r in range(n_sel)]
        tiles = [selbuf[slot, k * n_sel + r] for r in range(n_sel)]
        keys_t = jnp.concatenate([t_[0:kvd, :] for t_ in tiles], axis=1).astype(BF16)
        vals_t = jnp.concatenate([t_[kvd:2 * kvd, :] for t_ in tiles], axis=1).astype(BF16)
        bias = jnp.concatenate([bsel_ref[jnp.clip(bl // per_page, 0, n_pages)] for bl in blks], axis=1)
        valid = jnp.concatenate(
            [(lane // SEL_BLOCK == bl % per_page) & ((bl // per_page) * PAGE_SIZE + lane <= past) & (bl >= 0)
             for bl in blks], axis=1)
        o_k = attend(_head_pad(q, lambda r: r // NSA_GROUP == k), keys_t, vals_t, bias, valid)
        o_s = jnp.where(head // NSA_GROUP == k, o_k, o_s)
    wlanes = wall.shape[1]
    wall[:, 0:wb] = wc_ref[0]
    tail = lax.broadcasted_iota(jnp.int32, (1, wlanes - wb), 1)
    wall[:, wb:wlanes] = jnp.where(tail == 0, _row_to_col(wn_ref[0]), 0.0)
    win_ref[0] = pltpu.roll(wall[...], wlanes - 1, 1)[:, 0:wb]
    w_pos = lax.broadcasted_iota(jnp.int32, (1, wlanes), 1)
    valid_w = (w_pos <= wb) & (wb - w_pos < WINDOW) & (past - wb + w_pos >= 0)
    o_w = attend(_head_pad(q, lambda r: r >= 0), wall[0:kvd, :].astype(BF16), wall[kvd:2 * kvd, :].astype(BF16),
                 bwin_ref[...], valid_w)
    o_c = jnp.concatenate([oc_ref[0]] * NSA_KV_HEADS, axis=1)
    mix = gates[:, 0:1] * o_c + gates[:, 1:2] * o_s + gates[:, 2:3] * o_w
    out = mix[:, 0:hd]
    for k in range(1, NSA_KV_HEADS):
        out = jnp.where(head // NSA_GROUP == k, mix[:, k * hd:(k + 1) * hd], out)
    ob_ref[0] = out.astype(ob_ref.dtype)


def nsa_sample_att(q8, gates, oc, kv03, kv45, wcache_t, layer, pages_t, page_table, sel_idx, rel_bias):
    bd, nh, hd = q8.shape
    n_pages = page_table.shape[1]
    past = n_pages * PAGE_SIZE
    wb = wcache_t.shape[2]
    kvd = NSA_KV_HEADS * hd
    n_sel = sel_idx.shape[2]
    wlanes = _round_up(wb + 1, LANES)
    g3 = gates[:, :3 * nh].reshape(bd, nh, 3)
    dist = past - (jnp.arange(n_pages + 1)[:, None] * PAGE_SIZE + jnp.arange(PAGE_SIZE)[None, :])
    bsel = jnp.transpose(_bias_of(rel_bias, dist), (0, 2, 1))
    bwin = _bias_of(rel_bias, wb - jnp.arange(wlanes)).T
    blk = lambda n2, n3: pl.BlockSpec((1, n2, n3), lambda bi, pt, si: (bi, 0, 0))
    grid_spec = pltpu.PrefetchScalarGridSpec(
        num_scalar_prefetch=2, grid=(bd,),
        in_specs=[blk(nh, hd), blk(nh, 3), blk(nh, hd), blk(1, 4 * kvd), blk(1, 2 * kvd),
                  pl.BlockSpec((1, 2 * kvd, wb), lambda bi, pt, si: (layer * bd + bi, 0, 0)),
                  pl.BlockSpec(memory_space=pl.ANY),
                  pl.BlockSpec(bsel.shape, lambda bi, pt, si: (0, 0, 0)),
                  pl.BlockSpec(bwin.shape, lambda bi, pt, si: (0, 0))],
        out_specs=[blk(nh, hd), blk(2 * kvd, wb)],
        scratch_shapes=[pltpu.VMEM((2, NSA_KV_HEADS * n_sel, 2 * kvd, PAGE_SIZE), F32),
                        pltpu.VMEM((2 * kvd, wlanes), F32), pltpu.SemaphoreType.DMA((2,))])
    ob, win = pl.pallas_call(
        functools.partial(_nsa_sample_att_kernel, n_pages=n_pages, n_sel=n_sel, past=past, wb=wb),
        grid_spec=grid_spec,
        out_shape=[jax.ShapeDtypeStruct((bd, nh, hd), BF16), jax.ShapeDtypeStruct((bd, 2 * kvd, wb), F32)],
        compiler_params=_cparams(("arbitrary",), VMEM_LIMIT), name="nsa_sample_att",
    )(page_table.reshape(-1), sel_idx.reshape(-1), q8, g3, oc, kv03.reshape(bd, 1, -1), kv45.reshape(bd, 1, -1),
      wcache_t, pages_t, bsel, bwin)
    return ob.reshape(bd, nh * hd), win


def kernel(x_prompt, x_sample, mem_prompt, cache_conv, cache_nsa_pages, cache_nsa_window, state_mlstm_c,
           state_mlstm_n, state_mlstm_m, cache_mem_kv, page_table, rel_bias, norm_mix, norm_xattn, norm_mem,
           norm_ffn, norm_final, w_in_even, w_out_even, conv_w, conv_b, conv_ln_g, conv_ln_b, nsa_cmp_pe,
           nsa_cmp_w, w_in_odd, mlstm_b_i, mlstm_b_f, mlstm_norm, w_out_odd, xattn_wq, xattn_wkv, xattn_wo,
           ffn_w_gu, ffn_w_dn, router_w, router_b, expert_w_gu, expert_w_dn):
    b, s, d = x_prompt.shape
    bd, td, _ = x_sample.shape
    assert td == 1, "the sample group decodes one token per sequence"
    depth = norm_mix.shape[0]
    mt = mem_prompt.shape[1]
    cc = conv_w.shape[2]
    hist = conv_w.shape[1] - 1
    wb = cache_nsa_window.shape[2]
    kvh, hd = NSA_KV_HEADS, NSA_HD
    n_pool = cache_nsa_pages.shape[1]
    assert s >= hist and s >= wb and s % Q_BLOCK == 0
    xp = x_prompt.reshape(b * s, d)
    xs = x_sample.reshape(bd, d)
    mem = mem_prompt.reshape(b * mt, d)
    pages_t = jnp.swapaxes(cache_nsa_pages.reshape(-1, PAGE_SIZE, 4 * kvh * hd), 1, 2)
    window_t = jnp.swapaxes(cache_nsa_window.reshape(-1, wb, 2 * kvh * hd), 1, 2)
    memkv_all = cache_mem_kv.reshape(depth * bd, mt, -1)
    bf = lambda a: a.astype(BF16)
    conv_p, conv_s, nsa_p, nsa_s, win_p, win_s = [], [], [], [], [], []
    mc_p, mc_s, mn_p, mn_s, mm_p, mm_s, memkv_p = [], [], [], [], [], [], []
    for l in range(depth):
        li = l // 2
        if l % 2 == 0:
            prm = prep_even(w_in_even[li], nsa_cmp_pe[li], nsa_cmp_w[li])
            w_out = bf(w_out_even[li])
            w_parts = [w_out[:cc], w_out[cc:]]
            conv_args = (conv_w[li], conv_b[li], conv_ln_g[li], conv_ln_b[li])
            glu, keys, qt, vt, gt, kv_t = inproj_even(xp, norm_mix[l], prm['w_in'], cc, s)
            a_out = conv_prompt(glu, *conv_args, b, s)
            kc, vct = compress_prompt(kv_t, prm['pe2'], prm['wbd'], _round_up(s // CMP_BLOCK, LANES))
            b_out = nsa_prompt(qt, gt, kc, vct, keys, vt, rel_bias, b, s)
            xp = outproj(xp, [a_out, b_out], w_parts)
            conv_p.append(glu.reshape(b, s, cc)[:, s - hist:])
            rows_t = kv_t.reshape(b, 6, kvh, hd, s)
            nsa_p.append(jnp.transpose(rows_t[:, :4], (0, 4, 1, 2, 3)))
            win_p.append(jnp.transpose(rows_t[:, 4:, :, :, s - wb:], (0, 4, 1, 2, 3)))
            glu, kv03, kv45, q, gates = inproj_even(xs, norm_mix[l], prm['w_in'], cc)
            a_out, conv_state = conv_sample(cache_conv[li], glu, *conv_args)
            q8 = q.reshape(bd, NSA_HEADS, hd)
            pt = page_table + li * n_pool
            o_c, sel_idx = nsa_sample_cmp(q8, pages_t, pt, prm['pe2'], prm['wbd'], rel_bias)
            b_out, win = nsa_sample_att(q8, gates, o_c, kv03, kv45, window_t, li, pages_t, pt, sel_idx, rel_bias)
            xs = outproj(xs, [a_out, b_out], w_parts)
            conv_s.append(conv_state)
            nsa_s.append(kv03.reshape(bd, 1, 4, kvh, hd))
            win_s.append(jnp.transpose(win.reshape(bd, 2, kvh, hd, wb), (0, 4, 1, 2, 3)))
        else:
            prm = prep_odd(w_in_odd[li], mlstm_b_i[li], mlstm_b_f[li])
            w_out = bf(w_out_odd[li])
            q, k, v, og, gc, gr = inproj_odd(xp, norm_mix[l], prm['w_in'], prm['bias'])
            hn, c_new, n_new, m_new = mlstm_prompt(q, k, v, og, gc, gr, mlstm_norm[li], b, s)
            xp = outproj(xp, [hn], [w_out])
            mc_p.append(c_new)
            mn_p.append(n_new)
            mm_p.append(m_new[:, :, 0])
            q, k, v, og, gc, gr = inproj_odd(xs, norm_mix[l], prm['w_in'], prm['bias'])
            hn, c_new, n_new, m_new = mlstm_sample(q, k, v, og, gc, mlstm_norm[li], state_mlstm_c[li],
                                                   state_mlstm_n[li], state_mlstm_m[li])
            xs = outproj(xs, [hn], [w_out])
            mc_s.append(c_new)
            mn_s.append(n_new)
            mm_s.append(m_new)
        wq, wo = bf(xattn_wq[l]), bf(xattn_wo[l])
        mkv, mkv_b = memkv(mem, norm_mem[l], bf(xattn_wkv[l]))
        memkv_p.append(mkv.reshape(b, mt, 2, X_HEADS, d // X_HEADS))
        xp = xattn_prompt(xp, norm_xattn[l], wq, mkv_b, wo, b, s)
        xs = xattn_sample(xs, norm_xattn[l], wq, memkv_all, wo, l)
        if l % 2 == 0:
            w_gu, w_dn = bf(ffn_w_gu[li]), bf(ffn_w_dn[li])
            xp = ffn(xp, norm_ffn[l], w_gu, w_dn)
            xs = ffn(xs, norm_ffn[l], w_gu, w_dn)
        else:
            e_gu, e_dn = bf(expert_w_gu[li]), bf(expert_w_dn[li])
            final_g = norm_final if l == depth - 1 else None
            comb, h, mask, counts = router(xp, norm_ffn[l], router_w[li], router_b[li])
            xp = moe_grouped(xp, h, comb, mask, counts, e_gu, e_dn, final_g)
            comb, h, _, _ = router(xs, norm_ffn[l], router_w[li], router_b[li])
            xs = moe(xs, h, comb, e_gu, e_dn, final_g)
    if depth % 2:
        xp, xs = rmsnorm(xp, norm_final), rmsnorm(xs, norm_final)
    y_prompt = xp.reshape(b, s, d)
    y_sample = xs.reshape(bd, 1, d)
    return (y_prompt, y_sample, jnp.stack(conv_p), jnp.stack(conv_s), jnp.stack(nsa_p), jnp.stack(nsa_s),
            jnp.stack(win_p), jnp.stack(win_s), jnp.stack(mc_p), jnp.stack(mc_s), jnp.stack(mn_p),
            jnp.stack(mn_s), jnp.stack(mm_p), jnp.stack(mm_s), jnp.stack(memkv_p))
```

```python
import functools
import math

import jax
import jax.numpy as jnp
import numpy as np
from jax import lax
from jax.experimental import pallas as pl
from jax.experimental.pallas import tpu as pltpu

F32 = jnp.float32
BF16 = jnp.bfloat16

PAGE_SIZE = 128
CONV_WIDTH = 31
NSA_HEADS = 8
NSA_KV_HEADS = 2
NSA_GROUP = NSA_HEADS // NSA_KV_HEADS
NSA_HD = 64
CMP_BLOCK = 32
SEL_BLOCK = 64
SEL_RATIO = SEL_BLOCK // CMP_BLOCK
TOP_N = 16
WINDOW = 512
Q_BLOCK = 128
FORCE_SCORE = 1.0e4
NUM_BUCKETS = 32
MAX_DISTANCE = 1024
MLSTM_HEADS = 4
X_HEADS = 4
N_EXPERTS = 8
TOP_K = 2
RMS_EPS = 1e-6
LN_EPS = 1e-5
NEG = -1e30

LANES = 128
SUBLANES = 8
VMEM_LIMIT = 56 * 1024 * 1024
MLSTM_CHUNK = 256


def _cparams(sem, vmem=None):
    return pltpu.CompilerParams(dimension_semantics=sem, vmem_limit_bytes=vmem)


def _rms(x, g):
    return x * lax.rsqrt(jnp.mean(x * x, -1, keepdims=True) + RMS_EPS) * g


def _dot(a, b):
    return jnp.dot(a, b, preferred_element_type=F32)


def _dot_nt(a, b):
    return lax.dot_general(a, b, (((1,), (1,)), ((), ())), preferred_element_type=F32)


def _dot_tn(a, b):
    return lax.dot_general(a, b, (((0,), (0,)), ((), ())), preferred_element_type=F32)


def _full(shape):
    n = len(shape)
    return pl.BlockSpec(shape, lambda *_: (0,) * n)


def _row_tile(m, pref):
    t = min(pref, m)
    while m % t:
        t //= 2
    return t


def _rmsnorm_kernel(x_ref, g_ref, o_ref):
    o_ref[...] = _rms(x_ref[...], g_ref[...])


def rmsnorm(x, g):
    m, d = x.shape
    tm = _row_tile(m, 1024)
    return pl.pallas_call(
        _rmsnorm_kernel, grid=(m // tm,),
        in_specs=[pl.BlockSpec((tm, d), lambda i: (i, 0)), _full((1, d))],
        out_specs=pl.BlockSpec((tm, d), lambda i: (i, 0)),
        out_shape=jax.ShapeDtypeStruct((m, d), F32),
        compiler_params=_cparams(("parallel",)), name="rmsnorm",
    )(x, g.reshape(1, d))


def _outproj_kernel(*refs, n_in):
    x_ref = refs[0]
    a_refs = refs[1:1 + n_in]
    w_refs = refs[1 + n_in:1 + 2 * n_in]
    o_ref = refs[1 + 2 * n_in]
    acc = x_ref[...]
    for a_ref, w_ref in zip(a_refs, w_refs):
        acc = acc + _dot(a_ref[...], w_ref[...])
    o_ref[...] = acc


def outproj(x, acts, ws):
    m, d = x.shape
    tm = _row_tile(m, 512)
    n_in = len(acts)
    in_specs = [pl.BlockSpec((tm, d), lambda i: (i, 0))]
    in_specs += [pl.BlockSpec((tm, a.shape[1]), lambda i: (i, 0)) for a in acts]
    in_specs += [_full(w.shape) for w in ws]
    return pl.pallas_call(
        functools.partial(_outproj_kernel, n_in=n_in), grid=(m // tm,),
        in_specs=in_specs, out_specs=pl.BlockSpec((tm, d), lambda i: (i, 0)),
        out_shape=jax.ShapeDtypeStruct((m, d), F32),
        compiler_params=_cparams(("parallel",)), name="outproj",
    )(x, *acts, *ws)


def _inproj_even_kernel(x_ref, g_ref, w_ref, glu_ref, *rest, cc, qd, kvd, tiles):
    xn = _rms(x_ref[...], g_ref[...]).astype(BF16)

    def mm(lo, hi):
        return _dot(xn, w_ref[:, lo:hi])

    o = 0
    a = mm(o, o + cc)
    b = mm(o + cc, o + 2 * cc)
    glu_ref[...] = a * jax.nn.sigmoid(b)
    o += 2 * cc
    q = mm(o, o + qd) * (NSA_HD ** -0.5)
    o += qd
    kv03 = mm(o, o + 4 * kvd)
    o += 4 * kvd
    kv45 = mm(o, o + 2 * kvd)
    o += 2 * kvd
    gates = jax.nn.sigmoid(mm(o, o + LANES))
    if tiles == 0:
        kv03_ref, kv45_ref, q_ref, gate_ref = rest
        kv03_ref[...] = kv03
        kv45_ref[...] = kv45
        q_ref[...] = q.astype(BF16)
        gate_ref[...] = gates
        return
    keys_ref, qt_ref, vt_ref, gt_ref, kvt_ref = rest
    kvt_ref[0] = jnp.concatenate([kv03, kv45], axis=1).T
    keys_ref[...] = jnp.concatenate([kv03[:, 2 * kvd:3 * kvd], kv45[:, 0:kvd]], axis=1).astype(BF16)
    vals = jnp.concatenate([kv03[:, 3 * kvd:4 * kvd], kv45[:, kvd:2 * kvd]], axis=1)
    for j in range(tiles):
        rows = slice(j * Q_BLOCK, (j + 1) * Q_BLOCK)
        qt_ref[j] = q[rows, :].T.astype(BF16)
        vt_ref[j] = vals[rows, :].T.astype(BF16)
        gt_ref[j] = gates[rows, :].T


def inproj_even(x, g, w_pad, cc, seq=None):
    m, d = x.shape
    qd = NSA_HEADS * NSA_HD
    kvd = NSA_KV_HEADS * NSA_HD
    tm = _row_tile(m, 256)
    row = lambda n: pl.BlockSpec((tm, n), lambda i: (i, 0))
    out_specs = [row(cc)]
    out_shape = [jax.ShapeDtypeStruct((m, cc), F32)]
    transposed = seq is not None
    tiles = tm // Q_BLOCK if transposed else 0
    if transposed:
        assert tm % Q_BLOCK == 0 and seq % tm == 0
        per_seq = seq // tm
        tile = lambda n: pl.BlockSpec((tiles, n, Q_BLOCK), lambda i: (i, 0, 0))
        out_specs += [row(2 * kvd), tile(qd), tile(2 * kvd), tile(LANES),
                      pl.BlockSpec((1, 6 * kvd, tm), lambda i: (i // per_seq, 0, i % per_seq))]
        out_shape += [jax.ShapeDtypeStruct((m, 2 * kvd), BF16),
                      jax.ShapeDtypeStruct((m // Q_BLOCK, qd, Q_BLOCK), BF16),
                      jax.ShapeDtypeStruct((m // Q_BLOCK, 2 * kvd, Q_BLOCK), BF16),
                      jax.ShapeDtypeStruct((m // Q_BLOCK, LANES, Q_BLOCK), F32),
                      jax.ShapeDtypeStruct((m // seq, 6 * kvd, seq), F32)]
    else:
        out_specs += [row(4 * kvd), row(2 * kvd), row(qd), row(LANES)]
        out_shape += [jax.ShapeDtypeStruct((m, 4 * kvd), F32), jax.ShapeDtypeStruct((m, 2 * kvd), F32),
                      jax.ShapeDtypeStruct((m, qd), BF16), jax.ShapeDtypeStruct((m, LANES), F32)]
    return pl.pallas_call(
        functools.partial(_inproj_even_kernel, cc=cc, qd=qd, kvd=kvd, tiles=tiles), grid=(m // tm,),
        in_specs=[row(d), _full((1, d)), _full(w_pad.shape)],
        out_specs=out_specs, out_shape=out_shape,
        compiler_params=_cparams(("parallel",)), name="inproj_even",
    )(x, g.reshape(1, d), w_pad)


def _conv_post(y, lg, lb):
    mu = jnp.mean(y, -1, keepdims=True)
    var = jnp.mean(jnp.square(y - mu), -1, keepdims=True)
    yn = (y - mu) * lax.rsqrt(var + LN_EPS) * lg + lb
    return yn * jax.nn.sigmoid(yn)


CONV_SUB = 64
CONV_PAD = 32


def _conv_prompt_kernel(glu_ref, cw_ref, cb_ref, lg_ref, lb_ref, o_ref, ext_ref, y_ref, *, ts, s):
    i = pl.program_id(1)
    c = glu_ref.shape[-1]

    @pl.when(i == 0)
    def _():
        ext_ref[0:CONV_PAD, :] = jnp.zeros((CONV_PAD, c), F32)
        ext_ref[CONV_PAD:CONV_PAD + s, :] = glu_ref[0]
        ext_ref[CONV_PAD + s:CONV_PAD + s + SUBLANES, :] = jnp.zeros((SUBLANES, c), F32)

    lead = CONV_PAD - (CONV_WIDTH - 1)
    span = CONV_SUB + CONV_PAD

    def sub(j, carry):
        r0 = pl.multiple_of(i * ts + j * CONV_SUB, CONV_SUB)
        for c0 in range(0, c, LANES):
            xw = ext_ref[pl.ds(r0, span + SUBLANES), c0:c0 + LANES]
            acc = jnp.zeros((CONV_SUB, LANES), F32) + cb_ref[:, c0:c0 + LANES]
            for r in range(SUBLANES):
                xr = xw if r == 0 else pltpu.roll(xw, span + SUBLANES - r, 0)
                for a in range(span // SUBLANES):
                    w = SUBLANES * a + r - lead
                    if 0 <= w < CONV_WIDTH:
                        acc = acc + xr[SUBLANES * a:SUBLANES * a + CONV_SUB, :] * cw_ref[w:w + 1, c0:c0 + LANES]
            y_ref[:, c0:c0 + LANES] = acc
        o_ref[0, pl.ds(pl.multiple_of(j * CONV_SUB, CONV_SUB), CONV_SUB), :] = _conv_post(
            y_ref[...], lg_ref[...], lb_ref[...]).astype(o_ref.dtype)
        return carry

    lax.fori_loop(0, ts // CONV_SUB, sub, 0)


def conv_prompt(glu, cw, cb, lg, lb, b, s):
    c = glu.shape[-1]
    ts = _row_tile(s, 256)
    vec = lambda a: a.reshape(1, c)
    out = pl.pallas_call(
        functools.partial(_conv_prompt_kernel, ts=ts, s=s), grid=(b, s // ts),
        in_specs=[pl.BlockSpec((1, s, c), lambda bi, i: (bi, 0, 0)), _full((CONV_WIDTH, c)),
                  _full((1, c)), _full((1, c)), _full((1, c))],
        out_specs=pl.BlockSpec((1, ts, c), lambda bi, i: (bi, i, 0)),
        out_shape=jax.ShapeDtypeStruct((b, s, c), BF16),
        scratch_shapes=[pltpu.VMEM((CONV_PAD + s + SUBLANES, c), F32), pltpu.VMEM((CONV_SUB, c), F32)],
        compiler_params=_cparams(("parallel", "arbitrary")), name="conv_prompt",
    )(glu.reshape(b, s, c), cw, vec(cb), vec(lg), vec(lb))
    return out.reshape(b * s, c)


def _conv_sample_kernel(cache_ref, glu_ref, cw_ref, cb_ref, lg_ref, lb_ref, o_ref, st_ref):
    hist = CONV_WIDTH - 1
    cache = cache_ref[...]
    glu = glu_ref[...]
    y = jnp.sum(cache * cw_ref[0:hist, :][None], axis=1) + glu * cw_ref[hist:hist + 1, :] + cb_ref[...]
    o_ref[...] = _conv_post(y, lg_ref[...], lb_ref[...]).astype(o_ref.dtype)
    st_ref[:, 0:hist - 1, :] = cache[:, 1:hist, :]
    st_ref[:, hist - 1:hist, :] = glu[:, None, :]


def conv_sample(cache, glu, cw, cb, lg, lb):
    bd, hist, c = cache.shape
    vec = lambda a: a.reshape(1, c)
    return pl.pallas_call(
        _conv_sample_kernel,
        out_shape=[jax.ShapeDtypeStruct((bd, c), BF16), jax.ShapeDtypeStruct((bd, hist, c), F32)],
        name="conv_sample",
    )(cache, glu, cw, vec(cb), vec(lg), vec(lb))


def _rel_bucket(dist):
    n = jnp.maximum(dist, 0)
    max_exact = NUM_BUCKETS // 2
    nf = jnp.maximum(n, 1).astype(F32)
    large = max_exact + (jnp.log(nf / max_exact) / math.log(MAX_DISTANCE / max_exact)
                         * (NUM_BUCKETS - max_exact)).astype(jnp.int32)
    large = jnp.minimum(large, NUM_BUCKETS - 1)
    return jnp.where(n < max_exact, n, large)


def _bias_of(rel_bias, dist):
    bucket = _rel_bucket(dist)[..., None]
    out = jnp.zeros(bucket.shape[:-1] + (rel_bias.shape[1],), F32)
    for k in range(NUM_BUCKETS):
        out = jnp.where(bucket == k, rel_bias[k].astype(F32), out)
    return out


def _compress_accumulate(load_rows, pe_ref, w_ref, nc):
    accs = []
    for slot in range(2):
        acc = jnp.zeros((nc, w_ref.shape[-1]), F32)
        for j in range(CMP_BLOCK):
            xj = load_rows(slot, j) + pe_ref[slot, j:j + 1, :]
            acc = acc + _dot(xj.astype(BF16), w_ref[slot, j])
        accs.append(acc)
    return jnp.concatenate(accs, axis=1)


def _file_rows(tile_t, xrow, sl, page):
    rows = tile_t.T
    per_page = PAGE_SIZE // CMP_BLOCK
    for cl in range(per_page):
        for a in range(CMP_BLOCK // SUBLANES):
            r0 = cl * CMP_BLOCK + a * SUBLANES
            dst = pl.multiple_of((page * per_page + cl) * SUBLANES, SUBLANES)
            xrow[sl, a, pl.ds(dst, SUBLANES), :] = rows[r0:r0 + SUBLANES, :]


def _compress_filed(xrow, pe_ref, w_ref, nc):
    return _compress_accumulate(
        lambda sl, j: xrow[sl, j // SUBLANES, pl.ds(j % SUBLANES, nc, stride=SUBLANES), :], pe_ref, w_ref, nc)


def _compress_prompt_kernel(x_ref, pe_ref, w_ref, kc_ref, vct_ref, xrow, *, nc, ncp, kvd):
    for p in range(x_ref.shape[2] // PAGE_SIZE):
        for sl in range(2):
            _file_rows(x_ref[0, sl * kvd:(sl + 1) * kvd, p * PAGE_SIZE:(p + 1) * PAGE_SIZE], xrow, sl, p)
    acc = _compress_filed(xrow, pe_ref, w_ref, nc)
    if ncp > nc:
        acc = jnp.concatenate([acc, jnp.zeros((ncp - nc, 2 * kvd), F32)], axis=0)
    kc_ref[0] = acc[:, 0:kvd].astype(BF16)
    vct_ref[0] = acc[:, kvd:2 * kvd].T.astype(BF16)


def compress_prompt(kv_t, pe2, wbd, ncp):
    b, _, s = kv_t.shape
    assert s % PAGE_SIZE == 0
    nc = s // CMP_BLOCK
    kvd = wbd.shape[-1]
    return pl.pallas_call(
        functools.partial(_compress_prompt_kernel, nc=nc, ncp=ncp, kvd=kvd), grid=(b,),
        in_specs=[pl.BlockSpec((1, 2 * kvd, s), lambda bi: (bi, 0, 0)), _full(pe2.shape), _full(wbd.shape)],
        out_specs=[pl.BlockSpec((1, ncp, kvd), lambda bi: (bi, 0, 0)), pl.BlockSpec((1, kvd, ncp), lambda bi: (bi, 0, 0))],
        out_shape=[jax.ShapeDtypeStruct((b, ncp, kvd), BF16), jax.ShapeDtypeStruct((b, kvd, ncp), BF16)],
        scratch_shapes=[pltpu.VMEM((2, CMP_BLOCK // SUBLANES, nc * SUBLANES, kvd), F32)],
        compiler_params=_cparams(("parallel",)), name="compress_prompt",
    )(kv_t, pe2, wbd)


def _select_blocks(score, n_sel, n_cand):
    lane = lax.broadcasted_iota(jnp.int32, score.shape, 1)
    rank = jnp.zeros(score.shape, F32)
    for i in range(n_cand):
        col = score[:, 2 * i:2 * i + 1]
        beats = (col > score) | ((col == score) & (lane > 2 * i))
        rank = rank + beats.astype(F32)
    is_cand = ((lane % 2) == 0) & (lane < 2 * n_cand)
    return (is_cand & (rank < n_sel) & (score >= 0)).astype(F32), rank


def _pair_sum(imp):
    n = imp.shape[1]
    return imp + pltpu.roll(imp, n - 1, 1)


def _rank_rows(score, n_sel, n_cand):
    blk = lax.broadcasted_iota(jnp.int32, score.shape, 0)
    rank = jnp.zeros(score.shape, F32)
    for i in range(n_cand):
        row = score[i:i + 1, :]
        beats = (row > score) | ((row == score) & (blk > i))
        rank = rank + beats.astype(F32)
    return ((rank < n_sel) & (score >= 0)).astype(F32)


def _nsa_prompt_kernel(qt_ref, gt_ref, kc_ref, vct_ref, keys_ref, vt_ref, biasc_ref, btile_ref, o_ref,
                       qt_scr, oc_scr, acc_s, acc_w, imp_scr, sel_scr, out_scr, *, nc, nsb, n_sel):
    qi = pl.program_id(1)
    g, hd, kvh, qb = NSA_GROUP, NSA_HD, NSA_KV_HEADS, Q_BLOCK
    kvd = kvh * hd
    ncp = kc_ref.shape[1]
    nsbp = sel_scr.shape[1]
    q_pos = qi * qb + lax.broadcasted_iota(jnp.int32, (1, qb), 1)
    key_row = lax.broadcasted_iota(jnp.int32, (qb, 1), 0)
    c_row = lax.broadcasted_iota(jnp.int32, (ncp, 1), 0)
    mask_c = (q_pos >= c_row * CMP_BLOCK + (CMP_BLOCK - 1)) & (c_row < nc)
    blk = lax.broadcasted_iota(jnp.int32, (nsbp, 1), 0)
    cur = q_pos // SEL_BLOCK
    forced = (blk == 0) | (blk == cur) | (blk == cur - 1)
    zeros = jnp.zeros((hd, qb), BF16)
    for k in range(kvh):
        for gi in range(g):
            h = k * g + gi
            parts = [zeros] * kvh
            parts[k] = qt_ref[0, h * hd:(h + 1) * hd, :]
            qt_scr[k, :, gi * qb:(gi + 1) * qb] = jnp.concatenate(parts, axis=0)
        s_c = _dot(kc_ref[0], qt_scr[k])
        imp = jnp.zeros((ncp, qb), F32)
        probs = []
        for gi in range(g):
            s = jnp.where(mask_c, s_c[:, gi * qb:(gi + 1) * qb] + biasc_ref[k * g + gi], NEG)
            e = jnp.where(mask_c, jnp.exp(s - jnp.max(s, 0, keepdims=True)), 0.0)
            den = jnp.sum(e, 0, keepdims=True)
            p = e / jnp.where(den > 0, den, 1.0)
            imp = imp + p
            probs.append(p.astype(BF16))
        oc_scr[k] = _dot(vct_ref[0, k * hd:(k + 1) * hd, :], jnp.concatenate(probs, axis=1))
        imp_scr[...] = imp + pltpu.roll(imp, ncp - 1, 0)
        cand = imp_scr[pl.ds(0, nsbp, stride=SEL_RATIO), :]
        score = jnp.where(forced, FORCE_SCORE, jnp.where(blk <= cur, cand, -1.0))
        sel_scr[k] = _rank_rows(jnp.where(blk < nsb, score, -2.0), n_sel, nsb)

    per_tile = qb // SEL_BLOCK
    n_tiles = keys_ref.shape[1] // qb
    first = ([jnp.full((1, qb), NEG, F32)] * g, [jnp.zeros((1, qb), F32)] * g)

    def tile_step(tiles, carry, key_col, val_row, acc_ref, window):
        kts = [jnp.clip(kt, 0, n_tiles - 1) for kt, _ in tiles]
        starts = [pl.multiple_of(kt * qb, qb) for kt in kts]
        k_t = jnp.concatenate([keys_ref[0, pl.ds(r0, qb), key_col:key_col + kvd] for r0 in starts], axis=0)
        dist = jnp.concatenate([jnp.where(active, q_pos - (r0 + key_row), -1)
                                for r0, (_, active) in zip(starts, tiles)], axis=0)
        in_range = dist >= 0
        scores = [_dot(k_t, qt_scr[k]) for k in range(kvh)]
        new, updates = [], []
        for k in range(kvh):
            if window:
                valid = in_range & (dist < WINDOW)
            else:
                pieces = []
                for kt in kts:
                    chosen = jnp.zeros((qb, qb), F32)
                    for j in range(per_tile):
                        row = sel_scr[k, pl.ds(per_tile * kt + j, 1), :]
                        chosen = jnp.where(key_row // SEL_BLOCK == j, row, chosen)
                    pieces.append(chosen)
                valid = in_range & (jnp.concatenate(pieces, axis=0) > 0.5)
            ms, ls = carry[k]
            ms2, ls2, alphas, probs = [], [], [], []
            for gi in range(g):
                bias = jnp.concatenate([btile_ref[jnp.maximum(qi - kt, 0), k * g + gi] for kt in kts], axis=0)
                s = jnp.where(valid, scores[k][:, gi * qb:(gi + 1) * qb] + bias, NEG)
                m_new = jnp.maximum(ms[gi], jnp.max(s, 0, keepdims=True))
                alpha = jnp.exp(ms[gi] - m_new)
                p = jnp.exp(s - jnp.where(m_new == NEG, 0.0, m_new))
                ms2.append(m_new)
                ls2.append(alpha * ls[gi] + jnp.sum(p, 0, keepdims=True))
                alphas.append(alpha)
                probs.append(p.astype(BF16))
            new.append((ms2, ls2))
            updates.append((jnp.concatenate(alphas, axis=1), jnp.concatenate(probs, axis=1)))
        for k, (alpha, prob) in enumerate(updates):
            v_t = jnp.concatenate([vt_ref[kt, val_row + k * hd:val_row + (k + 1) * hd, :] for kt in kts], axis=1)
            acc_ref[k] = acc_ref[k] * alpha + _dot(v_t, prob)
        return tuple(new)

    acc_s[...] = jnp.zeros_like(acc_s)
    acc_w[...] = jnp.zeros_like(acc_w)
    sel_args = dict(key_col=0, val_row=0, acc_ref=acc_s, window=False)
    win_args = dict(key_col=kvd, val_row=kvd, acc_ref=acc_w, window=True)

    def sel_pair(i, carry):
        return tile_step([(2 * i, 2 * i <= qi), (2 * i + 1, 2 * i + 1 <= qi)], carry, **sel_args)

    stat_s = lax.fori_loop(0, qi // 2 + 1, sel_pair, (first,) * kvh)
    stat_w = (first,) * kvh
    win_tiles = [(qi - j, qi - j >= 0) for j in range(WINDOW // qb, -1, -1)]
    for j in range(0, len(win_tiles), 2):
        stat_w = tile_step(win_tiles[j:j + 2], stat_w, **win_args)
    gt = gt_ref[0]
    for k in range(kvh):
        for gi in range(g):
            h = k * g + gi
            cols = slice(gi * qb, (gi + 1) * qb)
            l_s, l_w = stat_s[k][1][gi], stat_w[k][1][gi]
            o_s = acc_s[k, :, cols] / jnp.where(l_s > 0, l_s, 1.0)
            o_w = acc_w[k, :, cols] / jnp.where(l_w > 0, l_w, 1.0)
            out_scr[h * hd:(h + 1) * hd, :] = (gt[3 * h:3 * h + 1, :] * oc_scr[k, :, cols]
                                               + gt[3 * h + 1:3 * h + 2, :] * o_s + gt[3 * h + 2:3 * h + 3, :] * o_w)
    o_ref[0] = out_scr[...].T.astype(o_ref.dtype)


def nsa_prompt(qt, gt, kc, vct, keys, vt, rel_bias, b, s):
    qb = Q_BLOCK
    nq = s // qb
    nc = s // CMP_BLOCK
    ncp = kc.shape[1]
    nsb = s // SEL_BLOCK
    nsbp = _round_up(nsb, SUBLANES)
    assert SEL_RATIO * nsbp <= ncp
    n_sel = min(TOP_N, nsb)
    nh = NSA_HEADS
    hq = qt.shape[1]
    kvd = kc.shape[2]
    glanes = NSA_GROUP * qb
    cend = jnp.arange(ncp)[:, None] * CMP_BLOCK + (CMP_BLOCK - 1)
    biasc = jnp.transpose(_bias_of(rel_bias, jnp.arange(s)[None, :] - cend), (2, 0, 1))
    r = jnp.arange(qb)
    dist = jnp.arange(nq)[:, None, None] * qb + r[None, None, :] - r[None, :, None]
    btile = jnp.transpose(_bias_of(rel_bias, dist), (0, 3, 1, 2))
    out = pl.pallas_call(
        functools.partial(_nsa_prompt_kernel, nc=nc, nsb=nsb, n_sel=n_sel), grid=(b, nq),
        in_specs=[pl.BlockSpec((1, hq, qb), lambda bi, i: (bi * nq + i, 0, 0)),
                  pl.BlockSpec((1, LANES, qb), lambda bi, i: (bi * nq + i, 0, 0)),
                  pl.BlockSpec((1, ncp, kvd), lambda bi, i: (bi, 0, 0)),
                  pl.BlockSpec((1, kvd, ncp), lambda bi, i: (bi, 0, 0)),
                  pl.BlockSpec((1, s, 2 * kvd), lambda bi, i: (bi, 0, 0)),
                  pl.BlockSpec((nq, 2 * kvd, qb), lambda bi, i: (bi, 0, 0)),
                  pl.BlockSpec((nh, ncp, qb), lambda bi, i: (0, 0, i)),
                  _full(btile.shape)],
        out_specs=pl.BlockSpec((1, qb, hq), lambda bi, i: (bi, i, 0)),
        out_shape=jax.ShapeDtypeStruct((b, s, hq), BF16),
        scratch_shapes=[pltpu.VMEM((NSA_KV_HEADS, kvd, glanes), BF16), pltpu.VMEM((NSA_KV_HEADS, NSA_HD, glanes), F32),
                        pltpu.VMEM((NSA_KV_HEADS, NSA_HD, glanes), F32), pltpu.VMEM((NSA_KV_HEADS, NSA_HD, glanes), F32),
                        pltpu.VMEM((ncp, qb), F32), pltpu.VMEM((NSA_KV_HEADS, nsbp, qb), F32),
                        pltpu.VMEM((hq, qb), F32)],
        compiler_params=_cparams(("parallel", "arbitrary"), VMEM_LIMIT), name="nsa_prompt",
    )(qt, gt, kc, vct, keys.reshape(b, s, 2 * kvd), vt, biasc, btile)
    return out.reshape(b * s, hq)


def _round_up(x, m):
    return (x + m - 1) // m * m


def prep_even(w_in, pe, wc):
    d, n = w_in.shape
    n_pad = _round_up(n - 3 * NSA_HEADS, LANES) + LANES
    w_pad = jnp.zeros((d, n_pad), BF16).at[:, :n].set(w_in.astype(BF16))
    hd = NSA_HD
    pe2 = jnp.tile(pe, (1, 1, NSA_KV_HEADS))
    zero = jnp.zeros_like(wc)
    wbd = jnp.concatenate([jnp.concatenate([wc if i == j else zero for j in range(NSA_KV_HEADS)], axis=-1)
                           for i in range(NSA_KV_HEADS)], axis=-2)
    return dict(w_in=w_pad, pe2=pe2, wbd=wbd.astype(BF16))


def _memkv_kernel(x_ref, g_ref, w_ref, o_ref, ob_ref, *, hd, nh):
    y = _dot(_rms(x_ref[...], g_ref[...]).astype(BF16), w_ref[...])
    ob_ref[...] = y.astype(BF16)
    tm = y.shape[0]
    chunks = hd // LANES
    period = 2 * chunks * nh
    for kv in range(2):
        for h in range(nh):
            for c in range(chunks):
                col = (kv * nh + h) * hd + c * LANES
                o_ref[pl.ds((kv * chunks + c) * nh + h, tm, stride=period), :] = y[:, col:col + LANES]


def memkv(mem, g, w, nh):
    m, d = mem.shape
    n = w.shape[1]
    hd = n // (2 * nh)
    per_tok = n // LANES
    tm = _row_tile(m, 256)
    return pl.pallas_call(
        functools.partial(_memkv_kernel, hd=hd, nh=nh), grid=(m // tm,),
        in_specs=[pl.BlockSpec((tm, d), lambda i: (i, 0)), _full((1, d)), _full(w.shape)],
        out_specs=[pl.BlockSpec((tm * per_tok, LANES), lambda i: (i, 0)), pl.BlockSpec((tm, n), lambda i: (i, 0))],
        out_shape=[jax.ShapeDtypeStruct((m * per_tok, LANES), F32), jax.ShapeDtypeStruct((m, n), BF16)],
        compiler_params=_cparams(("parallel",)), name="memkv",
    )(mem, g.reshape(1, d), w)


def _xattn_core(q, kv, hd):
    nh = q.shape[1] // hd
    outs = []
    for h in range(nh):
        s = _dot_nt(q[:, h * hd:(h + 1) * hd], kv[:, h * hd:(h + 1) * hd])
        e = jnp.exp(s - jnp.max(s, -1, keepdims=True))
        p = e / jnp.sum(e, -1, keepdims=True)
        outs.append(_dot(p.astype(BF16), kv[:, (nh + h) * hd:(nh + h + 1) * hd]))
    return jnp.concatenate(outs, axis=1).astype(BF16)


def _xattn_prompt_kernel(x_ref, g_ref, wq_ref, kv_ref, wo_ref, o_ref, *, hd):
    x = x_ref[0]
    q = (_dot(_rms(x, g_ref[...]).astype(BF16), wq_ref[...]) * (hd ** -0.5)).astype(BF16)
    o = _xattn_core(q, kv_ref[0], hd)
    o_ref[0] = x + _dot(o, wo_ref[...])


def xattn_prompt(x, g, wq, kvb, wo, b, s):
    d = x.shape[1]
    mt = kvb.shape[0] // b
    tm = _row_tile(s, 512)
    out = pl.pallas_call(
        functools.partial(_xattn_prompt_kernel, hd=d // X_HEADS), grid=(b, s // tm),
        in_specs=[pl.BlockSpec((1, tm, d), lambda bi, i: (bi, i, 0)), _full((1, d)), _full(wq.shape),
                  pl.BlockSpec((1, mt, kvb.shape[1]), lambda bi, i: (bi, 0, 0)), _full(wo.shape)],
        out_specs=pl.BlockSpec((1, tm, d), lambda bi, i: (bi, i, 0)),
        out_shape=jax.ShapeDtypeStruct((b, s, d), F32),
        compiler_params=_cparams(("parallel", "parallel"), VMEM_LIMIT), name="xattn_prompt",
    )(x.reshape(b, s, d), g.reshape(1, d), wq, kvb.reshape(b, mt, -1), wo)
    return out.reshape(b * s, d)


def _xattn_sample_kernel(x_ref, g_ref, wq_ref, kv_ref, wo_ref, o_ref, q_scr, a_scr, *, hd, mt):
    bi = pl.program_id(0)
    nb = pl.num_programs(0)
    nh = wq_ref.shape[1] // hd

    @pl.when(bi == 0)
    def _():
        q_scr[...] = _dot(_rms(x_ref[...], g_ref[...]).astype(BF16), wq_ref[...]) * (hd ** -0.5)

    q = jnp.broadcast_to(q_scr[pl.ds(bi, 1), :], (SUBLANES, q_scr.shape[1])).astype(BF16)
    chunks = hd // LANES
    period = 2 * chunks * nh

    def head_rows(kv, h):
        return jnp.concatenate([kv_ref[0, pl.ds((kv * chunks + c) * nh + h, mt, stride=period), :]
                                for c in range(chunks)], axis=1).astype(BF16)

    outs = []
    for h in range(nh):
        s = _dot_nt(q[:, h * hd:(h + 1) * hd], head_rows(0, h))
        e = jnp.exp(s - jnp.max(s, -1, keepdims=True))
        p = e / jnp.sum(e, -1, keepdims=True)
        outs.append(_dot(p.astype(BF16), head_rows(1, h)))
    a_scr[pl.ds(bi, 1), :] = jnp.concatenate(outs, axis=1)[0:1, :]

    @pl.when(bi == nb - 1)
    def _():
        o_ref[...] = x_ref[...] + _dot(a_scr[...].astype(BF16), wo_ref[...])


def xattn_sample(x, g, wq, kv_rows, wo, layer, mt):
    bd, d = x.shape
    rows = kv_rows.shape[1]
    return pl.pallas_call(
        functools.partial(_xattn_sample_kernel, hd=d // X_HEADS, mt=mt), grid=(bd,),
        in_specs=[_full((bd, d)), _full((1, d)), _full(wq.shape),
                  pl.BlockSpec((1, rows, LANES), lambda bi: (layer * bd + bi, 0, 0)), _full(wo.shape)],
        out_specs=_full((bd, d)),
        out_shape=jax.ShapeDtypeStruct((bd, d), F32),
        scratch_shapes=[pltpu.VMEM((bd, wq.shape[1]), F32), pltpu.VMEM((bd, wq.shape[1]), F32)],
        compiler_params=_cparams(("arbitrary",), VMEM_LIMIT), name="xattn_sample",
    )(x, g.reshape(1, d), wq, kv_rows, wo)


def _ffn_kernel(x_ref, g_ref, wg_ref, wu_ref, wd_ref, o_ref, h_scr, acc_scr):
    c = pl.program_id(1)

    @pl.when(c == 0)
    def _():
        h_scr[...] = _rms(x_ref[...], g_ref[...]).astype(BF16)
        acc_scr[...] = x_ref[...]

    h = h_scr[...]
    gate = _dot(h, wg_ref[...])
    up = _dot(h, wu_ref[...])
    act = (gate * jax.nn.sigmoid(gate) * up).astype(BF16)
    acc_scr[...] += _dot(act, wd_ref[...])

    @pl.when(c == pl.num_programs(1) - 1)
    def _():
        o_ref[...] = acc_scr[...]


def _ff_chunk(dff, pref):
    c = dff
    for n in range(1, dff // LANES + 1):
        if dff % n == 0 and (dff // n) % LANES == 0 and dff // n <= pref:
            c = dff // n
            break
    return c


def ffn(x, g, w_gu, w_dn):
    m, d = x.shape
    dff = w_dn.shape[0]
    tm = _row_tile(m, 512)
    fc = _ff_chunk(dff, 1408)
    nch = dff // fc
    return pl.pallas_call(
        _ffn_kernel, grid=(m // tm, nch),
        in_specs=[pl.BlockSpec((tm, d), lambda i, c: (i, 0)), _full((1, d)),
                  pl.BlockSpec((d, fc), lambda i, c: (0, c)),
                  pl.BlockSpec((d, fc), lambda i, c: (0, nch + c)),
                  pl.BlockSpec((fc, d), lambda i, c: (c, 0))],
        out_specs=pl.BlockSpec((tm, d), lambda i, c: (i, 0)),
        out_shape=jax.ShapeDtypeStruct((m, d), F32),
        scratch_shapes=[pltpu.VMEM((tm, d), BF16), pltpu.VMEM((tm, d), F32)],
        compiler_params=_cparams(("parallel", "arbitrary"), VMEM_LIMIT), name="ffn",
    )(x, g.reshape(1, d), w_gu, w_gu, w_dn)


def _router_kernel(x_ref, g_ref, w_ref, b_ref, comb_ref, h_ref, mask_ref, cnt_ref, *, ne):
    h = _rms(x_ref[...], g_ref[...]).astype(BF16)
    h_ref[...] = h
    logits = _dot(h, w_ref[...]) + b_ref[...]
    lane = lax.broadcasted_iota(jnp.int32, logits.shape, 1)
    logits = jnp.where(lane < ne, logits, -jnp.inf)
    v1 = jnp.max(logits, -1, keepdims=True)
    i1 = jnp.min(jnp.where(logits == v1, lane, LANES), -1, keepdims=True)
    rest = jnp.where(lane == i1, -jnp.inf, logits)
    v2 = jnp.max(rest, -1, keepdims=True)
    i2 = jnp.min(jnp.where(rest == v2, lane, LANES), -1, keepdims=True)
    e2 = jnp.exp(v2 - v1)
    den = 1.0 + e2
    comb_ref[...] = jnp.where(lane == i1, 1.0 / den, 0.0) + jnp.where(lane == i2, e2 / den, 0.0)
    chosen = jnp.where((lane == i1) | (lane == i2), 1.0, 0.0)
    mask_ref[...] = chosen.astype(BF16)

    @pl.when(pl.program_id(0) == 0)
    def _():
        cnt_ref[...] = jnp.zeros_like(cnt_ref)

    cnt_ref[0:1, :] += jnp.sum(chosen, axis=0, keepdims=True)


def router(x, g, w_r, b_r):
    m, d = x.shape
    ne = w_r.shape[1]
    w_pad = jnp.zeros((d, LANES), BF16).at[:, :ne].set(w_r.astype(BF16))
    b_pad = jnp.zeros((1, LANES), F32).at[0, :ne].set(b_r.astype(F32))
    tm = _row_tile(m, 512)
    row = lambda n: pl.BlockSpec((tm, n), lambda i: (i, 0))
    return pl.pallas_call(
        functools.partial(_router_kernel, ne=ne), grid=(m // tm,),
        in_specs=[row(d), _full((1, d)), _full((d, LANES)), _full((1, LANES))],
        out_specs=[row(LANES), row(d), row(LANES), _full((SUBLANES, LANES))],
        out_shape=[jax.ShapeDtypeStruct((m, LANES), F32), jax.ShapeDtypeStruct((m, d), BF16),
                   jax.ShapeDtypeStruct((m, LANES), BF16), jax.ShapeDtypeStruct((SUBLANES, LANES), F32)],
        compiler_params=_cparams(("arbitrary",)), name="router",
    )(x, g.reshape(1, d), w_pad, b_pad)


def _residual_out(y, gain_ref, norm):
    return _rms(y, gain_ref[...]) if norm else y


def _moe_kernel(x_ref, h_ref, comb_ref, wg_ref, wu_ref, wd_ref, gain_ref, o_ref, acc_scr, *, norm):
    e = pl.program_id(1)

    @pl.when(e == 0)
    def _():
        acc_scr[...] = jnp.zeros_like(acc_scr)

    h = h_ref[...]
    gate = _dot(h, wg_ref[0])
    up = _dot(h, wu_ref[0])
    act = (gate * jax.nn.sigmoid(gate) * up).astype(BF16)
    y = _dot(act, wd_ref[0])
    comb = comb_ref[...]
    lane = lax.broadcasted_iota(jnp.int32, comb.shape, 1)
    acc_scr[...] += jnp.sum(jnp.where(lane == e, comb, 0.0), -1, keepdims=True) * y

    @pl.when(e == pl.num_programs(1) - 1)
    def _():
        o_ref[...] = _residual_out(x_ref[...] + acc_scr[...], gain_ref, norm)


def moe(x, h, comb, w_gu, w_dn, final_g=None):
    m, d = x.shape
    ne, dfe = w_dn.shape[:2]
    tm = _row_tile(m, 512)
    gain = jnp.ones((1, d), F32) if final_g is None else final_g.reshape(1, d)
    return pl.pallas_call(
        functools.partial(_moe_kernel, norm=final_g is not None), grid=(m // tm, ne),
        in_specs=[pl.BlockSpec((tm, d), lambda i, e: (i, 0)), pl.BlockSpec((tm, d), lambda i, e: (i, 0)),
                  pl.BlockSpec((tm, LANES), lambda i, e: (i, 0)),
                  pl.BlockSpec((1, d, dfe), lambda i, e: (e, 0, 0)),
                  pl.BlockSpec((1, d, dfe), lambda i, e: (e, 0, 1)),
                  pl.BlockSpec((1, dfe, d), lambda i, e: (e, 0, 0)), _full((1, d))],
        out_specs=pl.BlockSpec((tm, d), lambda i, e: (i, 0)),
        out_shape=jax.ShapeDtypeStruct((m, d), F32),
        scratch_shapes=[pltpu.VMEM((tm, d), F32)],
        compiler_params=_cparams(("parallel", "arbitrary"), VMEM_LIMIT), name="moe",
    )(x, h, comb, w_gu, w_gu, w_dn, gain)


MOE_TILE = 256


def _moe_pos_kernel(mask_ref, comb_ref, tri_ref, base_ref, post_ref, pos2_ref, wab_ref, stab_ref, run_scr, *, nep):
    sb = pl.program_id(0)
    nb = pl.num_programs(0)

    @pl.when(sb == 0)
    def _():
        run_scr[...] = jnp.zeros_like(run_scr)
        stab_ref[...] = jnp.zeros_like(stab_ref)

    a = mask_ref[...]
    af = a.astype(F32)
    start = base_ref[...] + run_scr[...]
    stab_ref[pl.ds(sb, 1), :] = start.astype(jnp.int32)
    rank = _dot(tri_ref[...], a)
    pos = jnp.where(af > 0, start + rank, -1.0)
    lane = lax.broadcasted_iota(jnp.int32, pos.shape, 1)
    first_e = jnp.min(jnp.where(af > 0, lane, LANES), -1, keepdims=True)
    last_e = jnp.max(jnp.where(af > 0, lane, -1), -1, keepdims=True)
    comb = comb_ref[...]
    w_a = jnp.sum(jnp.where(lane == first_e, comb, 0.0), -1, keepdims=True)
    w_b = jnp.sum(jnp.where(lane == last_e, comb, 0.0), -1, keepdims=True)
    wab_ref[...] = jnp.where(lane == 0, w_a, jnp.where(lane == 1, w_b, 0.0))
    pos_t = pos.T[0:nep, :]
    post_ref[0] = pos_t.astype(jnp.int32)
    row = lax.broadcasted_iota(jnp.int32, pos_t.shape, 0)
    first_r = jnp.min(jnp.where(pos_t >= 0, row, nep), 0, keepdims=True)
    last_r = jnp.max(jnp.where(pos_t >= 0, row, -1), 0, keepdims=True)
    pos_a = jnp.sum(jnp.where(row == first_r, pos_t, 0.0), 0, keepdims=True)
    pos_b = jnp.sum(jnp.where(row == last_r, pos_t, 0.0), 0, keepdims=True)
    pos2_ref[0] = jnp.where(row == 0, pos_a, jnp.where(row == 1, pos_b, 0.0)).astype(jnp.int32)
    run_scr[...] += jnp.sum(af, axis=0, keepdims=True)

    @pl.when(sb == nb - 1)
    def _():
        stab_ref[pl.ds(nb, 1), :] = (base_ref[...] + run_scr[...]).astype(jnp.int32)


def _moe_expert_kernel(te_ref, lo_ref, cnt_ref, ring_ref, nt_ref, h_ref, post_ref, wg_ref, wu_ref, wd_ref, y_ref,
                       hbuf, xg_scr, sem, *, t):
    i = pl.program_id(0)

    @pl.when(i >= nt_ref[0])
    def _():
        y_ref[...] = jnp.zeros_like(y_ref)

    nbuf = hbuf.shape[0]
    ahead = nbuf - 1

    def copy(sb, slot):
        return pltpu.make_async_copy(h_ref.at[pl.ds(pl.multiple_of(sb * t, t), t), :], hbuf.at[slot], sem.at[slot])

    def start_first(tile):
        for j in range(ahead):
            @pl.when(j < cnt_ref[tile])
            def _():
                copy(lo_ref[tile] + j, (ring_ref[tile] + j) % nbuf).start()

    @pl.when(i == 0)
    def _():
        start_first(0)

    @pl.when(i < nt_ref[0])
    def _():
        e = te_ref[i]
        lo = lo_ref[i]
        n = cnt_ref[i]
        ring = ring_ref[i]
        xg_scr[...] = jnp.zeros_like(xg_scr)
        row = i * t + lax.broadcasted_iota(jnp.int32, (t, 1), 0)

        def body(j, c):
            slot = (ring + j) % nbuf
            copy(lo + j, slot).wait()

            @pl.when(j + ahead < n)
            def _():
                copy(lo + j + ahead, (ring + j + ahead) % nbuf).start()

            src_pos = post_ref[lo + j, pl.ds(e, 1), :]
            onehot = jnp.where(src_pos == row, 1.0, 0.0).astype(BF16)
            xg_scr[...] += _dot(onehot, hbuf[slot])
            return c

        lax.fori_loop(0, n, body, 0)

        @pl.when(i + 1 < nt_ref[0])
        def _():
            start_first(i + 1)

        x = xg_scr[...].astype(BF16)
        gate = _dot(x, wg_ref[0])
        up = _dot(x, wu_ref[0])
        act = (gate * jax.nn.sigmoid(gate) * up).astype(BF16)
        y_ref[...] = _dot(act, wd_ref[0])


def _moe_combine_kernel(x_ref, wab_ref, pos_ref, nxt_ref, gain_ref, ys_ref, o_ref, ybuf, sem, *, t, norm):
    sb = pl.program_id(0)
    nb = pl.num_programs(0)

    def row_copy(p_ref, which, slot, tok):
        return pltpu.make_async_copy(ys_ref.at[pl.ds(p_ref[0, which, tok], 1), :],
                                     ybuf.at[slot, which, pl.ds(tok, 1), :], sem.at[slot])

    def start_all(p_ref, slot):
        def body(tok, c):
            row_copy(p_ref, 0, slot, tok).start()
            row_copy(p_ref, 1, slot, tok).start()
            return c
        lax.fori_loop(0, t, body, 0, unroll=8)

    @pl.when(sb == 0)
    def _():
        start_all(pos_ref, 0)

    @pl.when(sb + 1 < nb)
    def _():
        start_all(nxt_ref, (sb + 1) % 2)

    slot = sb % 2

    def wait_body(tok, c):
        row_copy(pos_ref, 0, slot, tok).wait()
        row_copy(pos_ref, 1, slot, tok).wait()
        return c
    lax.fori_loop(0, t, wait_body, 0, unroll=8)
    w = wab_ref[...]
    y = x_ref[...] + w[:, 0:1] * ybuf[slot, 0] + w[:, 1:2] * ybuf[slot, 1]
    o_ref[...] = _residual_out(y, gain_ref, norm)


def moe_grouped(x, h, comb, mask, counts, w_gu, w_dn, final_g=None):
    m, d = x.shape
    ne, dfe = w_dn.shape[:2]
    t = MOE_TILE
    assert m % t == 0
    nb = m // t
    nep = _round_up(ne, SUBLANES)
    nbp = _round_up(nb + 1, SUBLANES)
    k_top = TOP_K
    nt_max = k_top * m // t + ne
    cnt = counts[0, :ne].astype(jnp.int32)
    cnt_pad = (cnt + t - 1) // t * t
    ends = jnp.cumsum(cnt_pad)
    base = ends - cnt_pad
    base_row = jnp.zeros((1, LANES), F32).at[0, :ne].set(base.astype(F32))
    idx = lax.broadcasted_iota(jnp.int32, (t, t), 0)
    tri = jnp.where(lax.broadcasted_iota(jnp.int32, (t, t), 1) < idx, 1.0, 0.0).astype(BF16)
    blk = lambda n2: pl.BlockSpec((t, n2), lambda i: (i, 0))
    post, pos2, wab, stab = pl.pallas_call(
        functools.partial(_moe_pos_kernel, nep=nep), grid=(nb,),
        in_specs=[blk(LANES), blk(LANES), _full((t, t)), _full((1, LANES))],
        out_specs=[pl.BlockSpec((1, nep, t), lambda i: (i, 0, 0)), pl.BlockSpec((1, nep, t), lambda i: (i, 0, 0)),
                   blk(LANES), _full((nbp, LANES))],
        out_shape=[jax.ShapeDtypeStruct((nb, nep, t), jnp.int32), jax.ShapeDtypeStruct((nb, nep, t), jnp.int32),
                   jax.ShapeDtypeStruct((m, LANES), F32), jax.ShapeDtypeStruct((nbp, LANES), jnp.int32)],
        scratch_shapes=[pltpu.VMEM((1, LANES), F32)],
        compiler_params=_cparams(("arbitrary",)), name="moe_positions",
    )(mask, comb, tri, base_row)
    r0 = jnp.arange(nt_max, dtype=jnp.int32) * t
    tile_e = jnp.minimum(jnp.sum(ends[None, :] <= r0[:, None], axis=1), ne - 1).astype(jnp.int32)
    n_tiles = (ends[-1] // t).astype(jnp.int32).reshape(1)
    s_e = stab[:nb + 1, :ne][:, tile_e]
    lo = jnp.sum(s_e[1:] <= r0[None, :], axis=0)
    hi = jnp.sum(s_e[:nb] < r0[None, :] + t, axis=0) - 1
    lo = jnp.clip(lo, 0, nb - 1).astype(jnp.int32)
    hi = jnp.clip(hi, lo, nb - 1).astype(jnp.int32)
    n_src = jnp.where(jnp.arange(nt_max) < n_tiles[0], hi - lo + 1, 0).astype(jnp.int32)
    n_ring = 6
    ring = ((jnp.cumsum(n_src) - n_src) % n_ring).astype(jnp.int32)
    w_spec = lambda shape, col: pl.BlockSpec(shape, lambda i, te, *_: (te[i], 0, col))
    grid_spec = pltpu.PrefetchScalarGridSpec(
        num_scalar_prefetch=5, grid=(nt_max,),
        in_specs=[pl.BlockSpec(memory_space=pl.ANY),
                  pl.BlockSpec((nb, nep, t), lambda i, *_: (0, 0, 0)),
                  w_spec((1, d, dfe), 0), w_spec((1, d, dfe), 1), w_spec((1, dfe, d), 0)],
        out_specs=pl.BlockSpec((t, d), lambda i, *_: (i, 0)),
        scratch_shapes=[pltpu.VMEM((n_ring, t, d), BF16), pltpu.VMEM((t, d), F32),
                        pltpu.SemaphoreType.DMA((n_ring,))])
    ys = pl.pallas_call(
        functools.partial(_moe_expert_kernel, t=t), grid_spec=grid_spec,
        out_shape=jax.ShapeDtypeStruct((nt_max * t, d), F32),
        compiler_params=_cparams(("arbitrary",), VMEM_LIMIT), name="moe_experts",
    )(tile_e, lo, n_src, ring, n_tiles, h, post, w_gu, w_gu, w_dn)
    smem_blk = lambda f: pl.BlockSpec((1, nep, t), f, memory_space=pltpu.SMEM)
    gain = jnp.ones((1, d), F32) if final_g is None else final_g.reshape(1, d)
    return pl.pallas_call(
        functools.partial(_moe_combine_kernel, t=t, norm=final_g is not None), grid=(nb,),
        in_specs=[blk(d), blk(LANES), smem_blk(lambda i: (i, 0, 0)),
                  smem_blk(lambda i: (jnp.minimum(i + 1, nb - 1), 0, 0)), _full((1, d)),
                  pl.BlockSpec(memory_space=pl.ANY)],
        out_specs=blk(d),
        out_shape=jax.ShapeDtypeStruct((m, d), F32),
        scratch_shapes=[pltpu.VMEM((2, 2, t, d), F32), pltpu.SemaphoreType.DMA((2,))],
        compiler_params=_cparams(("arbitrary",), VMEM_LIMIT), name="moe_combine",
    )(x, wab, pos2, pos2, gain, ys)


def _inproj_odd_kernel(x_ref, g_ref, w_ref, bias_ref, q_ref, k_ref, v_ref, og_ref, gc_ref, gr_ref, *, hq, hv, nh):
    xn = _rms(x_ref[...], g_ref[...]).astype(BF16)

    def mm(lo, hi):
        return _dot(xn, w_ref[:, lo:hi])

    dk = hq // nh
    q_ref[...] = mm(0, hq).astype(BF16)
    k_ref[...] = (mm(hq, 2 * hq) * (dk ** -0.5)).astype(BF16)
    v_ref[...] = mm(2 * hq, 2 * hq + hv).astype(BF16)
    og_ref[...] = jax.nn.sigmoid(mm(2 * hq + hv, 2 * hq + 2 * hv))
    gi = mm(2 * hq + 2 * hv, 2 * hq + 2 * hv + LANES) + bias_ref[...]
    lane = lax.broadcasted_iota(jnp.int32, gi.shape, 1)
    gates = jnp.where(lane < nh, gi, jax.nn.log_sigmoid(gi))
    gc_ref[...] = gates
    gr_ref[...] = gates.T[0:SUBLANES, :]


def inproj_odd(x, g, w_pad, gate_bias):
    m, d = x.shape
    nh = MLSTM_HEADS
    hq = hv = d
    tm = _row_tile(m, 256)
    row = lambda n: pl.BlockSpec((tm, n), lambda i: (i, 0))
    outs = [(hq, BF16), (hq, BF16), (hv, BF16), (hv, F32), (LANES, F32)]
    if tm % LANES:
        gr_spec = _full((SUBLANES, m))
    else:
        gr_spec = pl.BlockSpec((SUBLANES, tm), lambda i: (0, i))
    return pl.pallas_call(
        functools.partial(_inproj_odd_kernel, hq=hq, hv=hv, nh=nh), grid=(m // tm,),
        in_specs=[row(d), _full((1, d)), _full(w_pad.shape), _full((1, LANES))],
        out_specs=[row(n) for n, _ in outs] + [gr_spec],
        out_shape=[jax.ShapeDtypeStruct((m, n), t) for n, t in outs] + [jax.ShapeDtypeStruct((SUBLANES, m), F32)],
        compiler_params=_cparams(("parallel",), VMEM_LIMIT), name="inproj_odd",
    )(x, g.reshape(1, d), w_pad, gate_bias)


def prep_odd(w_in, b_i, b_f):
    d, n = w_in.shape
    n_pad = _round_up(n - 2 * MLSTM_HEADS, LANES) + LANES
    w_pad = jnp.zeros((d, n_pad), BF16).at[:, :n].set(w_in.astype(BF16))
    bias = jnp.zeros((1, LANES), F32).at[0, :2 * MLSTM_HEADS].set(jnp.concatenate([b_i, b_f]).astype(F32))
    return dict(w_in=w_pad, bias=bias)


def _mlstm_prompt_kernel(q_ref, k_ref, v_ref, og_ref, gc_ref, gr_ref, gain_ref, hn_ref, c_ref, n_ref, m_ref,
                         *, nh, dk, dv, ln):
    ci = pl.program_id(1)

    @pl.when(ci == 0)
    def _():
        c_ref[...] = jnp.zeros_like(c_ref)
        n_ref[...] = jnp.zeros_like(n_ref)
        m_ref[...] = jnp.full(m_ref.shape, NEG, F32)

    row = lax.broadcasted_iota(jnp.int32, (ln, ln), 0)
    col = lax.broadcasted_iota(jnp.int32, (ln, ln), 1)
    tri = row >= col
    gc = gc_ref[...]
    gr = gr_ref[...]
    for h in range(nh):
        q = q_ref[:, h * dk:(h + 1) * dk]
        k = k_ref[:, h * dk:(h + 1) * dk]
        v = v_ref[:, h * dv:(h + 1) * dv]
        ig_c, lf_c = gc[:, h:h + 1], gc[:, nh + h:nh + h + 1]
        ig_r, lf_r = gr[h:h + 1, :], gr[nh + h:nh + h + 1, :]
        b_c = jnp.sum(jnp.where(tri, lf_r, 0.0), axis=1, keepdims=True)
        b_r = jnp.sum(jnp.where(row <= col, lf_c, 0.0), axis=0, keepdims=True)
        m_prev = m_ref[0, h:h + 1, 0:1]
        c_prev = c_ref[0, h]
        n_prev = n_ref[0, h:h + 1, :]
        dmat = jnp.where(tri, b_c - b_r + ig_r, NEG)
        inter = b_c + m_prev
        mt = jnp.maximum(inter, jnp.max(dmat, -1, keepdims=True))
        wm = jnp.exp(dmat - mt)
        a = jnp.exp(inter - mt)
        wqk = wm * _dot_nt(q, k)
        num = a * _dot_nt(q, c_prev.astype(BF16)) + _dot(wqk.astype(BF16), v)
        den = a * jnp.sum(q.astype(F32) * n_prev, -1, keepdims=True) + jnp.sum(wqk, -1, keepdims=True)
        hh = num / jnp.maximum(jnp.abs(den), jnp.exp(-mt))
        b_end = b_c[ln - 1:ln, :]
        m_new = mt[ln - 1:ln, :]
        a_end = jnp.exp(b_end + m_prev - m_new)
        w_s = jnp.exp(b_end - b_c + ig_c - m_new)
        c_ref[0, h] = a_end * c_prev + _dot_tn((v.astype(F32) * w_s).astype(BF16), k)
        n_ref[0, h:h + 1, :] = a_end * n_prev + jnp.sum(w_s * k.astype(F32), axis=0, keepdims=True)
        m_ref[0, h:h + 1, :] = jnp.broadcast_to(m_new, (1, m_ref.shape[2]))
        hn = hh * lax.rsqrt(jnp.mean(hh * hh, -1, keepdims=True) + RMS_EPS)
        hn = hn * gain_ref[:, h * dv:(h + 1) * dv] * og_ref[:, h * dv:(h + 1) * dv]
        hn_ref[:, h * dv:(h + 1) * dv] = hn.astype(hn_ref.dtype)


def mlstm_prompt(q, k, v, og, gc, gr, gain, b, s):
    m, d = q.shape
    nh = MLSTM_HEADS
    dk = dv = d // nh
    ln = _row_tile(s, MLSTM_CHUNK)
    nch = s // ln
    row = lambda n: pl.BlockSpec((ln, n), lambda bi, ci: (bi * nch + ci, 0))
    return pl.pallas_call(
        functools.partial(_mlstm_prompt_kernel, nh=nh, dk=dk, dv=dv, ln=ln), grid=(b, nch),
        in_specs=[row(d), row(d), row(d), row(d), row(LANES),
                  pl.BlockSpec((SUBLANES, ln), lambda bi, ci: (0, bi * nch + ci)), _full((1, d))],
        out_specs=[row(d), pl.BlockSpec((1, nh, dv, dk), lambda bi, ci: (bi, 0, 0, 0)),
                   pl.BlockSpec((1, nh, dk), lambda bi, ci: (bi, 0, 0)),
                   pl.BlockSpec((1, nh, LANES), lambda bi, ci: (bi, 0, 0))],
        out_shape=[jax.ShapeDtypeStruct((m, d), BF16), jax.ShapeDtypeStruct((b, nh, dv, dk), F32),
                   jax.ShapeDtypeStruct((b, nh, dk), F32), jax.ShapeDtypeStruct((b, nh, LANES), F32)],
        compiler_params=_cparams(("parallel", "arbitrary"), VMEM_LIMIT), name="mlstm_prompt",
    )(q, k, v, og, gc, gr, gain.reshape(1, d))


def _mlstm_sample_kernel(q_ref, k_ref, v_ref, og_ref, g_ref, gain_ref, c_ref, n_ref, m_ref,
                         hn_ref, co_ref, no_ref, mo_ref, *, nh):
    row = lax.broadcasted_iota(jnp.int32, (SUBLANES, 1), 0)
    for h in range(nh):
        q = q_ref[0, h:h + 1, :]
        k = k_ref[0, h:h + 1, :]
        v = v_ref[0, h:h + 1, :].astype(F32)
        ig = g_ref[0, h:h + 1, 0:1]
        lf = g_ref[0, h:h + 1, 1:2]
        m_prev = m_ref[0, h:h + 1, :]
        c_prev = c_ref[0, h]
        n_prev = n_ref[0, h:h + 1, :]
        inter = lf + m_prev
        mt = jnp.maximum(inter, ig)
        wm = jnp.exp(ig - mt)
        a = jnp.exp(inter - mt)
        q8 = jnp.broadcast_to(q, (SUBLANES, q.shape[1]))
        cq = _dot_nt(q8, c_prev.astype(BF16))[0:1, :]
        wqk = wm * jnp.sum(q.astype(F32) * k.astype(F32), -1, keepdims=True)
        num = a * cq + wqk * v
        den = a * jnp.sum(n_prev * q.astype(F32), -1, keepdims=True) + wqk
        hh = num / jnp.maximum(jnp.abs(den), jnp.exp(-mt))
        v8 = jnp.where(row == 0, jnp.broadcast_to(v * wm, (SUBLANES, v.shape[1])), 0.0).astype(BF16)
        k8 = jnp.broadcast_to(k, (SUBLANES, k.shape[1]))
        co_ref[0, h] = a * c_prev + _dot_tn(v8, k8)
        no_ref[0, h:h + 1, :] = a * n_prev + wm * k.astype(F32)
        mo_ref[0, h:h + 1, :] = mt
        hn = hh * lax.rsqrt(jnp.mean(hh * hh, -1, keepdims=True) + RMS_EPS)
        hn_ref[0, h:h + 1, :] = (hn * gain_ref[h:h + 1, :] * og_ref[0, h:h + 1, :]).astype(hn_ref.dtype)


def mlstm_sample(q, k, v, og, gc, gain, c, n, m):
    bd, d = q.shape
    nh = MLSTM_HEADS
    dk = d // nh
    heads = lambda a: a.reshape(bd, nh, dk)
    g2 = jnp.transpose(gc[:, :2 * nh].reshape(bd, 2, nh), (0, 2, 1))
    blk3 = lambda n2: pl.BlockSpec((1, nh, n2), lambda bi: (bi, 0, 0))
    cspec = pl.BlockSpec((1, nh, dk, dk), lambda bi: (bi, 0, 0, 0))
    hn, co, no, mo = pl.pallas_call(
        functools.partial(_mlstm_sample_kernel, nh=nh), grid=(bd,),
        in_specs=[blk3(dk), blk3(dk), blk3(dk), blk3(dk), blk3(2), _full((nh, dk)), cspec, blk3(dk), blk3(1)],
        out_specs=[blk3(dk), cspec, blk3(dk), blk3(1)],
        out_shape=[jax.ShapeDtypeStruct((bd, nh, dk), BF16), jax.ShapeDtypeStruct(c.shape, F32),
                   jax.ShapeDtypeStruct(n.shape, F32), jax.ShapeDtypeStruct((bd, nh, 1), F32)],
        compiler_params=_cparams(("parallel",)), name="mlstm_sample",
    )(heads(q), heads(k), heads(v), heads(og), g2, gain.reshape(nh, dk), c, n, m.reshape(bd, nh, 1))
    return hn.reshape(bd, d), co, no, mo.reshape(bd, nh)


def _row_to_col(row):
    n = row.shape[1]
    eye = lax.broadcasted_iota(jnp.int32, (n, n), 0) == lax.broadcasted_iota(jnp.int32, (n, n), 1)
    return jnp.sum(jnp.where(eye, row, 0.0), axis=1, keepdims=True)


def _head_pad(q, keep):
    q2 = jnp.concatenate([q] * NSA_KV_HEADS, axis=1)
    row = lax.broadcasted_iota(jnp.int32, q2.shape, 0)
    lane = lax.broadcasted_iota(jnp.int32, q2.shape, 1)
    return jnp.where((lane // NSA_HD == row // NSA_GROUP) & keep(row), q2, jnp.zeros_like(q2))


def _nsa_sample_cmp_kernel(pt_ref, q_ref, pages_ref, pe_ref, w_ref, biasc_ref, oc_ref, idx_ref, xbuf, xrow, sem,
                           *, n_pages, nc, ncp, nsb, n_sel, past):
    b = pl.program_id(0)
    nb = pl.num_programs(0)
    kvd = NSA_KV_HEADS * NSA_HD

    def page_copy(bb, slot, p, sl):
        return pltpu.make_async_copy(pages_ref.at[pt_ref[bb * n_pages + p], pl.ds(sl * kvd, kvd), :],
                                     xbuf.at[slot, sl, p], sem.at[slot])

    def start_all(bb, slot):
        def body(p, c):
            page_copy(bb, slot, p, 0).start()
            page_copy(bb, slot, p, 1).start()
            return c
        lax.fori_loop(0, n_pages, body, 0)

    @pl.when(b == 0)
    def _():
        start_all(0, 0)

    @pl.when(b + 1 < nb)
    def _():
        start_all(b + 1, (b + 1) % 2)

    slot = b % 2

    def wait_body(p, c):
        page_copy(b, slot, p, 0).wait()
        page_copy(b, slot, p, 1).wait()
        return c
    lax.fori_loop(0, n_pages, wait_body, 0)

    def file_page(p, c):
        for sl in range(2):
            _file_rows(xbuf[slot, sl, p], xrow, sl, p)
        return c
    lax.fori_loop(0, n_pages, file_page, 0, unroll=8)

    acc = _compress_filed(xrow, pe_ref, w_ref, nc)
    kc = acc[:, 0:kvd].astype(BF16)
    vc = acc[:, kvd:2 * kvd].astype(BF16)
    q = q_ref[0]
    nh = q.shape[0]
    qpad = _head_pad(q, lambda r: r >= 0)
    s = _dot_nt(qpad, kc)
    s = s + biasc_ref[:, 0:nc]
    e = jnp.exp(s - jnp.max(s, -1, keepdims=True))
    p_c = e / jnp.sum(e, -1, keepdims=True)
    o = _dot(p_c.astype(BF16), vc)
    row = lax.broadcasted_iota(jnp.int32, (nh, NSA_HD), 0)
    o_h = o[:, 0:NSA_HD]
    for k in range(1, NSA_KV_HEADS):
        o_h = jnp.where(row // NSA_GROUP == k, o[:, k * NSA_HD:(k + 1) * NSA_HD], o_h)
    oc_ref[0] = o_h
    prow = lax.broadcasted_iota(jnp.int32, p_c.shape, 0)
    lane = lax.broadcasted_iota(jnp.int32, (1, ncp), 1)
    blk = lane // 2
    cur = past // SEL_BLOCK
    forced = (blk == 0) | (blk == cur) | (blk == cur - 1)
    is_cand = ((lane % 2) == 0) & (lane < 2 * nsb)
    nselp = idx_ref.shape[1]
    rsel = lax.broadcasted_iota(jnp.int32, (nselp, 1), 0).astype(F32)
    out_lane = lax.broadcasted_iota(jnp.int32, (nselp, LANES), 1)
    result = jnp.full((nselp, LANES), -1, jnp.int32)
    for k in range(NSA_KV_HEADS):
        imp = jnp.sum(jnp.where(prow // NSA_GROUP == k, p_c, 0.0), axis=0, keepdims=True)
        imp = jnp.concatenate([imp, jnp.zeros((1, ncp - nc), F32)], axis=1)
        imp = _pair_sum(imp)
        score = jnp.where(forced, FORCE_SCORE, jnp.where(blk <= cur, imp, -1.0))
        score = jnp.where(is_cand, score, -2.0)
        sel, rank = _select_blocks(score, n_sel, nsb)
        hit = (rank == rsel) & (sel > 0.5)
        idx = jnp.sum(jnp.where(hit, (blk + 1).astype(F32), 0.0), axis=1, keepdims=True) - 1.0
        result = jnp.where(out_lane == k, idx.astype(jnp.int32), result)
    idx_ref[0] = result


def nsa_sample_cmp(q8, pages, page_table, pe2, wbd, rel_bias):
    bd, nh, hd = q8.shape
    n_pages = page_table.shape[1]
    past = n_pages * PAGE_SIZE
    nc = past // CMP_BLOCK
    nsb = -(-(past + 1) // SEL_BLOCK)
    n_sel = min(TOP_N, nsb)
    ncp = _round_up(max(nc, SEL_RATIO * nsb), LANES)
    nselp = _round_up(n_sel, SUBLANES)
    kvd = wbd.shape[-1]
    cend = jnp.arange(nc) * CMP_BLOCK + (CMP_BLOCK - 1)
    biasc = jnp.zeros((nh, ncp), F32).at[:, :nc].set(_bias_of(rel_bias, past - cend).T)
    grid_spec = pltpu.PrefetchScalarGridSpec(
        num_scalar_prefetch=1, grid=(bd,),
        in_specs=[pl.BlockSpec((1, nh, hd), lambda bi, pt: (bi, 0, 0)),
                  pl.BlockSpec(memory_space=pl.ANY),
                  pl.BlockSpec(pe2.shape, lambda bi, pt: (0, 0, 0)),
                  pl.BlockSpec(wbd.shape, lambda bi, pt: (0, 0, 0, 0)),
                  pl.BlockSpec((nh, ncp), lambda bi, pt: (0, 0))],
        out_specs=[pl.BlockSpec((1, nh, hd), lambda bi, pt: (bi, 0, 0)),
                   pl.BlockSpec((1, nselp, LANES), lambda bi, pt: (bi, 0, 0))],
        scratch_shapes=[pltpu.VMEM((2, 2, n_pages, kvd, PAGE_SIZE), F32),
                        pltpu.VMEM((2, CMP_BLOCK // SUBLANES, nc * SUBLANES, kvd), F32),
                        pltpu.SemaphoreType.DMA((2,))])
    oc, idx = pl.pallas_call(
        functools.partial(_nsa_sample_cmp_kernel, n_pages=n_pages, nc=nc, ncp=ncp, nsb=nsb, n_sel=n_sel, past=past),
        grid_spec=grid_spec,
        out_shape=[jax.ShapeDtypeStruct((bd, nh, hd), F32), jax.ShapeDtypeStruct((bd, nselp, LANES), jnp.int32)],
        compiler_params=_cparams(("arbitrary",), VMEM_LIMIT), name="nsa_sample_cmp",
    )(page_table.reshape(-1), q8, pages, pe2, wbd, biasc)
    sel_idx = jnp.transpose(idx[:, :n_sel, :NSA_KV_HEADS], (0, 2, 1))
    return oc, sel_idx


def _nsa_sample_att_kernel(pt_ref, si_ref, q_ref, g_ref, oc_ref, kvn_ref, wn_ref, wc_ref, pages_ref,
                           bsel_ref, bwin_ref, ob_ref, win_ref, selbuf, wall, sem,
                           *, n_pages, n_sel, past, wb):
    b = pl.program_id(0)
    nb = pl.num_programs(0)
    kvd = NSA_KV_HEADS * NSA_HD
    hd = NSA_HD
    n_blk_pages = past // SEL_BLOCK
    per_page = PAGE_SIZE // SEL_BLOCK
    n_slots = NSA_KV_HEADS * n_sel

    def blk_of(bb, j):
        return si_ref[bb * n_slots + j]

    def blk_copy(bb, slot, j):
        blk = jnp.clip(blk_of(bb, j), 0, n_blk_pages - 1)
        page = pt_ref[bb * n_pages + blk // per_page]
        return pltpu.make_async_copy(pages_ref.at[page, pl.ds(2 * kvd, 2 * kvd), :], selbuf.at[slot, j], sem.at[slot])

    def in_pages(bb, j):
        blk = blk_of(bb, j)
        return (blk >= 0) & (blk < n_blk_pages)

    def start_all(bb, slot):
        def body(j, c):
            @pl.when(in_pages(bb, j))
            def _():
                blk_copy(bb, slot, j).start()
            return c
        lax.fori_loop(0, n_slots, body, 0)

    @pl.when(b == 0)
    def _():
        start_all(0, 0)

    @pl.when(b + 1 < nb)
    def _():
        start_all(b + 1, (b + 1) % 2)

    slot = b % 2
    new_sel = _row_to_col(kvn_ref[0][:, 2 * kvd:4 * kvd])
    lane = lax.broadcasted_iota(jnp.int32, (1, PAGE_SIZE), 1)

    def wait_body(j, c):
        @pl.when(in_pages(b, j))
        def _():
            blk_copy(b, slot, j).wait()

        @pl.when(jnp.logical_not(in_pages(b, j)))
        def _():
            is_new = blk_of(b, j) == n_blk_pages
            selbuf[slot, j] = jnp.where((lane == 0) & is_new, new_sel, 0.0)
        return c
    lax.fori_loop(0, n_slots, wait_body, 0)

    q = q_ref[0]
    nh = q.shape[0]
    gates = g_ref[0]
    head = lax.broadcasted_iota(jnp.int32, (nh, 1), 0)

    def attend(qp, keys_t, vals_t, bias, valid):
        s = jnp.where(valid, _dot(qp, keys_t) + bias, NEG)
        e = jnp.where(valid, jnp.exp(s - jnp.max(s, -1, keepdims=True)), 0.0)
        den = jnp.sum(e, -1, keepdims=True)
        p = e / jnp.where(den > 0, den, 1.0)
        return _dot_nt(p.astype(BF16), vals_t)

    o_s = jnp.zeros((nh, kvd), F32)
    for k in range(NSA_KV_HEADS):
        blks = [blk_of(b, k * n_sel + r) for r in range(n_sel)]
        tiles = [selbuf[slot, k * n_sel + r] for r in range(n_sel)]
        keys_t = jnp.concatenate([t_[0:kvd, :] for t_ in tiles], axis=1).astype(BF16)
        vals_t = jnp.concatenate([t_[kvd:2 * kvd, :] for t_ in tiles], axis=1).astype(BF16)
        bias = jnp.concatenate([bsel_ref[jnp.clip(bl // per_page, 0, n_pages)] for bl in blks], axis=1)
        valid = jnp.concatenate(
            [(lane // SEL_BLOCK == bl % per_page) & ((bl // per_page) * PAGE_SIZE + lane <= past) & (bl >= 0)
             for bl in blks], axis=1)
        o_k = attend(_head_pad(q, lambda r: r // NSA_GROUP == k), keys_t, vals_t, bias, valid)
        o_s = jnp.where(head // NSA_GROUP == k, o_k, o_s)
    wlanes = wall.shape[1]
    wall[:, 0:wb] = wc_ref[0]
    tail = lax.broadcasted_iota(jnp.int32, (1, wlanes - wb), 1)
    wall[:, wb:wlanes] = jnp.where(tail == 0, _row_to_col(wn_ref[0]), 0.0)
    win_ref[0] = pltpu.roll(wall[...], wlanes - 1, 1)[:, 0:wb]
    w_pos = lax.broadcasted_iota(jnp.int32, (1, wlanes), 1)
    valid_w = (w_pos <= wb) & (wb - w_pos < WINDOW) & (past - wb + w_pos >= 0)
    o_w = attend(_head_pad(q, lambda r: r >= 0), wall[0:kvd, :].astype(BF16), wall[kvd:2 * kvd, :].astype(BF16),
                 bwin_ref[...], valid_w)
    o_c = jnp.concatenate([oc_ref[0]] * NSA_KV_HEADS, axis=1)
    mix = gates[:, 0:1] * o_c + gates[:, 1:2] * o_s + gates[:, 2:3] * o_w
    out = mix[:, 0:hd]
    for k in range(1, NSA_KV_HEADS):
        out = jnp.where(head // NSA_GROUP == k, mix[:, k * hd:(k + 1) * hd], out)
    ob_ref[0] = out.astype(ob_ref.dtype)


def nsa_sample_att(q8, gates, oc, kv03, kv45, wcache_t, layer, pages_t, page_table, sel_idx, rel_bias):
    bd, nh, hd = q8.shape
    n_pages = page_table.shape[1]
    past = n_pages * PAGE_SIZE
    wb = wcache_t.shape[2]
    kvd = NSA_KV_HEADS * hd
    n_sel = sel_idx.shape[2]
    wlanes = _round_up(wb + 1, LANES)
    g3 = gates[:, :3 * nh].reshape(bd, nh, 3)
    dist = past - (jnp.arange(n_pages + 1)[:, None] * PAGE_SIZE + jnp.arange(PAGE_SIZE)[None, :])
    bsel = jnp.transpose(_bias_of(rel_bias, dist), (0, 2, 1))
    bwin = _bias_of(rel_bias, wb - jnp.arange(wlanes)).T
    blk = lambda n2, n3: pl.BlockSpec((1, n2, n3), lambda bi, pt, si: (bi, 0, 0))
    grid_spec = pltpu.PrefetchScalarGridSpec(
        num_scalar_prefetch=2, grid=(bd,),
        in_specs=[blk(nh, hd), blk(nh, 3), blk(nh, hd), blk(1, 4 * kvd), blk(1, 2 * kvd),
                  pl.BlockSpec((1, 2 * kvd, wb), lambda bi, pt, si: (layer * bd + bi, 0, 0)),
                  pl.BlockSpec(memory_space=pl.ANY),
                  pl.BlockSpec(bsel.shape, lambda bi, pt, si: (0, 0, 0)),
                  pl.BlockSpec(bwin.shape, lambda bi, pt, si: (0, 0))],
        out_specs=[blk(nh, hd), blk(2 * kvd, wb)],
        scratch_shapes=[pltpu.VMEM((2, NSA_KV_HEADS * n_sel, 2 * kvd, PAGE_SIZE), F32),
                        pltpu.VMEM((2 * kvd, wlanes), F32), pltpu.SemaphoreType.DMA((2,))])
    ob, win = pl.pallas_call(
        functools.partial(_nsa_sample_att_kernel, n_pages=n_pages, n_sel=n_sel, past=past, wb=wb),
        grid_spec=grid_spec,
        out_shape=[jax.ShapeDtypeStruct((bd, nh, hd), BF16), jax.ShapeDtypeStruct((bd, 2 * kvd, wb), F32)],
        compiler_params=_cparams(("arbitrary",), VMEM_LIMIT), name="nsa_sample_att",
    )(page_table.reshape(-1), sel_idx.reshape(-1), q8, g3, oc, kv03.reshape(bd, 1, -1), kv45.reshape(bd, 1, -1),
      wcache_t, pages_t, bsel, bwin)
    return ob.reshape(bd, nh * hd), win


def kernel(x_prompt, x_sample, mem_prompt, cache_conv, cache_nsa_pages, cache_nsa_window, state_mlstm_c,
           state_mlstm_n, state_mlstm_m, cache_mem_kv, page_table, rel_bias, norm_mix, norm_xattn, norm_mem,
           norm_ffn, norm_final, w_in_even, w_out_even, conv_w, conv_b, conv_ln_g, conv_ln_b, nsa_cmp_pe,
           nsa_cmp_w, w_in_odd, mlstm_b_i, mlstm_b_f, mlstm_norm, w_out_odd, xattn_wq, xattn_wkv, xattn_wo,
           ffn_w_gu, ffn_w_dn, router_w, router_b, expert_w_gu, expert_w_dn):
    b, s, d = x_prompt.shape
    bd, td, _ = x_sample.shape
    assert td == 1, "the sample group decodes one token per sequence"
    depth = norm_mix.shape[0]
    mt = mem_prompt.shape[1]
    cc = conv_w.shape[2]
    hist = conv_w.shape[1] - 1
    wb = cache_nsa_window.shape[2]
    kvh, hd = NSA_KV_HEADS, NSA_HD
    n_pool = cache_nsa_pages.shape[1]
    assert s >= hist and s >= wb and s % Q_BLOCK == 0
    xp = x_prompt.reshape(b * s, d)
    xs = x_sample.reshape(bd, d)
    mem = mem_prompt.reshape(b * mt, d)
    pages_t = jnp.swapaxes(cache_nsa_pages.reshape(-1, PAGE_SIZE, 4 * kvh * hd), 1, 2)
    window_t = jnp.swapaxes(cache_nsa_window.reshape(-1, wb, 2 * kvh * hd), 1, 2)
    xhd = d // X_HEADS
    memkv_rows = jnp.swapaxes(cache_mem_kv.reshape(depth * bd, mt, 2, X_HEADS, xhd // LANES, LANES), 3, 4)
    memkv_rows = memkv_rows.reshape(depth * bd, -1, LANES)
    bf = lambda a: a.astype(BF16)
    conv_p, conv_s, nsa_p, nsa_s, win_p, win_s = [], [], [], [], [], []
    mc_p, mc_s, mn_p, mn_s, mm_p, mm_s, memkv_p = [], [], [], [], [], [], []
    for l in range(depth):
        li = l // 2
        if l % 2 == 0:
            prm = prep_even(w_in_even[li], nsa_cmp_pe[li], nsa_cmp_w[li])
            w_out = bf(w_out_even[li])
            w_parts = [w_out[:cc], w_out[cc:]]
            conv_args = (conv_w[li], conv_b[li], conv_ln_g[li], conv_ln_b[li])
            glu, keys, qt, vt, gt, kv_t = inproj_even(xp, norm_mix[l], prm['w_in'], cc, s)
            a_out = conv_prompt(glu, *conv_args, b, s)
            kc, vct = compress_prompt(kv_t, prm['pe2'], prm['wbd'], _round_up(s // CMP_BLOCK, LANES))
            b_out = nsa_prompt(qt, gt, kc, vct, keys, vt, rel_bias, b, s)
            xp = outproj(xp, [a_out, b_out], w_parts)
            conv_p.append(glu.reshape(b, s, cc)[:, s - hist:])
            rows_t = kv_t.reshape(b, 6, kvh, hd, s)
            nsa_p.append(jnp.transpose(rows_t[:, :4], (0, 4, 1, 2, 3)))
            win_p.append(jnp.transpose(rows_t[:, 4:, :, :, s - wb:], (0, 4, 1, 2, 3)))
            glu, kv03, kv45, q, gates = inproj_even(xs, norm_mix[l], prm['w_in'], cc)
            a_out, conv_state = conv_sample(cache_conv[li], glu, *conv_args)
            q8 = q.reshape(bd, NSA_HEADS, hd)
            pt = page_table + li * n_pool
            o_c, sel_idx = nsa_sample_cmp(q8, pages_t, pt, prm['pe2'], prm['wbd'], rel_bias)
            b_out, win = nsa_sample_att(q8, gates, o_c, kv03, kv45, window_t, li, pages_t, pt, sel_idx, rel_bias)
            xs = outproj(xs, [a_out, b_out], w_parts)
            conv_s.append(conv_state)
            nsa_s.append(kv03.reshape(bd, 1, 4, kvh, hd))
            win_s.append(jnp.transpose(win.reshape(bd, 2, kvh, hd, wb), (0, 4, 1, 2, 3)))
        else:
            prm = prep_odd(w_in_odd[li], mlstm_b_i[li], mlstm_b_f[li])
            w_out = bf(w_out_odd[li])
            q, k, v, og, gc, gr = inproj_odd(xp, norm_mix[l], prm['w_in'], prm['bias'])
            hn, c_new, n_new, m_new = mlstm_prompt(q, k, v, og, gc, gr, mlstm_norm[li], b, s)
            xp = outproj(xp, [hn], [w_out])
            mc_p.append(c_new)
            mn_p.append(n_new)
            mm_p.append(m_new[:, :, 0])
            q, k, v, og, gc, gr = inproj_odd(xs, norm_mix[l], prm['w_in'], prm['bias'])
            hn, c_new, n_new, m_new = mlstm_sample(q, k, v, og, gc, mlstm_norm[li], state_mlstm_c[li],
                                                   state_mlstm_n[li], state_mlstm_m[li])
            xs = outproj(xs, [hn], [w_out])
            mc_s.append(c_new)
            mn_s.append(n_new)
            mm_s.append(m_new)
        wq, wo = bf(xattn_wq[l]), bf(xattn_wo[l])
        mkv_rows, mkv_b = memkv(mem, norm_mem[l], bf(xattn_wkv[l]), X_HEADS)
        mkv = jnp.swapaxes(mkv_rows.reshape(b, mt, 2, xhd // LANES, X_HEADS, LANES), 3, 4)
        memkv_p.append(mkv.reshape(b, mt, 2, X_HEADS, xhd))
        xp = xattn_prompt(xp, norm_xattn[l], wq, mkv_b, wo, b, s)
        xs = xattn_sample(xs, norm_xattn[l], wq, memkv_rows, wo, l, mt)
        if l % 2 == 0:
            w_gu, w_dn = bf(ffn_w_gu[li]), bf(ffn_w_dn[li])
            xp = ffn(xp, norm_ffn[l], w_gu, w_dn)
            xs = ffn(xs, norm_ffn[l], w_gu, w_dn)
        else:
            e_gu, e_dn = bf(expert_w_gu[li]), bf(expert_w_dn[li])
            final_g = norm_final if l == depth - 1 else None
            comb, h, mask, counts = router(xp, norm_ffn[l], router_w[li], router_b[li])
            xp = moe_grouped(xp, h, comb, mask, counts, e_gu, e_dn, final_g)
            comb, h, _, _ = router(xs, norm_ffn[l], router_w[li], router_b[li])
            xs = moe(xs, h, comb, e_gu, e_dn, final_g)
    if depth % 2:
        xp, xs = rmsnorm(xp, norm_final), rmsnorm(xs, norm_final)
    y_prompt = xp.reshape(b, s, d)
    y_sample = xs.reshape(bd, 1, d)
    return (y_prompt, y_sample, jnp.stack(conv_p), jnp.stack(conv_s), jnp.stack(nsa_p), jnp.stack(nsa_s),
            jnp.stack(win_p), jnp.stack(win_s), jnp.stack(mc_p), jnp.stack(mc_s), jnp.stack(mn_p),
            jnp.stack(mn_s), jnp.stack(mm_p), jnp.stack(mm_s), jnp.stack(memkv_p))
```

```python
import functools
import math

import jax
import jax.numpy as jnp
import numpy as np
from jax import lax
from jax.experimental import pallas as pl
from jax.experimental.pallas import tpu as pltpu

F32 = jnp.float32
BF16 = jnp.bfloat16

PAGE_SIZE = 128
CONV_WIDTH = 31
NSA_HEADS = 8
NSA_KV_HEADS = 2
NSA_GROUP = NSA_HEADS // NSA_KV_HEADS
NSA_HD = 64
CMP_BLOCK = 32
SEL_BLOCK = 64
SEL_RATIO = SEL_BLOCK // CMP_BLOCK
TOP_N = 16
WINDOW = 512
Q_BLOCK = 128
FORCE_SCORE = 1.0e4
NUM_BUCKETS = 32
MAX_DISTANCE = 1024
MLSTM_HEADS = 4
X_HEADS = 4
N_EXPERTS = 8
TOP_K = 2
RMS_EPS = 1e-6
LN_EPS = 1e-5
NEG = -1e30

LANES = 128
SUBLANES = 8
VMEM_LIMIT = 56 * 1024 * 1024
MLSTM_CHUNK = 256


def _cparams(sem, vmem=None):
    return pltpu.CompilerParams(dimension_semantics=sem, vmem_limit_bytes=vmem)


def _rms(x, g):
    return x * lax.rsqrt(jnp.mean(x * x, -1, keepdims=True) + RMS_EPS) * g


def _dot(a, b):
    return jnp.dot(a, b, preferred_element_type=F32)


def _dot_nt(a, b):
    return lax.dot_general(a, b, (((1,), (1,)), ((), ())), preferred_element_type=F32)


def _dot_tn(a, b):
    return lax.dot_general(a, b, (((0,), (0,)), ((), ())), preferred_element_type=F32)


def _full(shape):
    n = len(shape)
    return pl.BlockSpec(shape, lambda *_: (0,) * n)


def _row_tile(m, pref):
    t = min(pref, m)
    while m % t:
        t //= 2
    return t


def _rmsnorm_kernel(x_ref, g_ref, o_ref):
    o_ref[...] = _rms(x_ref[...], g_ref[...])


def rmsnorm(x, g):
    m, d = x.shape
    tm = _row_tile(m, 1024)
    return pl.pallas_call(
        _rmsnorm_kernel, grid=(m // tm,),
        in_specs=[pl.BlockSpec((tm, d), lambda i: (i, 0)), _full((1, d))],
        out_specs=pl.BlockSpec((tm, d), lambda i: (i, 0)),
        out_shape=jax.ShapeDtypeStruct((m, d), F32),
        compiler_params=_cparams(("parallel",)), name="rmsnorm",
    )(x, g.reshape(1, d))


def _outproj_kernel(*refs, n_in):
    x_ref = refs[0]
    a_refs = refs[1:1 + n_in]
    w_refs = refs[1 + n_in:1 + 2 * n_in]
    o_ref = refs[1 + 2 * n_in]
    acc = x_ref[...]
    for a_ref, w_ref in zip(a_refs, w_refs):
        acc = acc + _dot(a_ref[...], w_ref[...])
    o_ref[...] = acc


def outproj(x, acts, ws):
    m, d = x.shape
    tm = _row_tile(m, 512)
    n_in = len(acts)
    in_specs = [pl.BlockSpec((tm, d), lambda i: (i, 0))]
    in_specs += [pl.BlockSpec((tm, a.shape[1]), lambda i: (i, 0)) for a in acts]
    in_specs += [_full(w.shape) for w in ws]
    return pl.pallas_call(
        functools.partial(_outproj_kernel, n_in=n_in), grid=(m // tm,),
        in_specs=in_specs, out_specs=pl.BlockSpec((tm, d), lambda i: (i, 0)),
        out_shape=jax.ShapeDtypeStruct((m, d), F32),
        compiler_params=_cparams(("parallel",)), name="outproj",
    )(x, *acts, *ws)


def _inproj_even_kernel(x_ref, g_ref, w_ref, glu_ref, *rest, cc, qd, kvd, tiles):
    xn = _rms(x_ref[...], g_ref[...]).astype(BF16)

    def mm(lo, hi):
        return _dot(xn, w_ref[:, lo:hi])

    o = 0
    a = mm(o, o + cc)
    b = mm(o + cc, o + 2 * cc)
    glu_ref[...] = a * jax.nn.sigmoid(b)
    o += 2 * cc
    q = mm(o, o + qd) * (NSA_HD ** -0.5)
    o += qd
    kv03 = mm(o, o + 4 * kvd)
    o += 4 * kvd
    kv45 = mm(o, o + 2 * kvd)
    o += 2 * kvd
    gates = jax.nn.sigmoid(mm(o, o + LANES))
    if tiles == 0:
        kv03_ref, kv45_ref, q_ref, gate_ref = rest
        kv03_ref[...] = kv03
        kv45_ref[...] = kv45
        q_ref[...] = q.astype(BF16)
        gate_ref[...] = gates
        return
    keys_ref, qt_ref, vt_ref, gt_ref, kvt_ref = rest
    kvt_ref[0] = jnp.concatenate([kv03, kv45], axis=1).T
    keys_ref[...] = jnp.concatenate([kv03[:, 2 * kvd:3 * kvd], kv45[:, 0:kvd]], axis=1).astype(BF16)
    vals = jnp.concatenate([kv03[:, 3 * kvd:4 * kvd], kv45[:, kvd:2 * kvd]], axis=1)
    for j in range(tiles):
        rows = slice(j * Q_BLOCK, (j + 1) * Q_BLOCK)
        qt_ref[j] = q[rows, :].T.astype(BF16)
        vt_ref[j] = vals[rows, :].T.astype(BF16)
        gt_ref[j] = gates[rows, :].T


def inproj_even(x, g, w_pad, cc, seq=None):
    m, d = x.shape
    qd = NSA_HEADS * NSA_HD
    kvd = NSA_KV_HEADS * NSA_HD
    tm = _row_tile(m, 256)
    row = lambda n: pl.BlockSpec((tm, n), lambda i: (i, 0))
    out_specs = [row(cc)]
    out_shape = [jax.ShapeDtypeStruct((m, cc), F32)]
    transposed = seq is not None
    tiles = tm // Q_BLOCK if transposed else 0
    if transposed:
        assert tm % Q_BLOCK == 0 and seq % tm == 0
        per_seq = seq // tm
        tile = lambda n: pl.BlockSpec((tiles, n, Q_BLOCK), lambda i: (i, 0, 0))
        out_specs += [row(2 * kvd), tile(qd), tile(2 * kvd), tile(LANES),
                      pl.BlockSpec((1, 6 * kvd, tm), lambda i: (i // per_seq, 0, i % per_seq))]
        out_shape += [jax.ShapeDtypeStruct((m, 2 * kvd), BF16),
                      jax.ShapeDtypeStruct((m // Q_BLOCK, qd, Q_BLOCK), BF16),
                      jax.ShapeDtypeStruct((m // Q_BLOCK, 2 * kvd, Q_BLOCK), BF16),
                      jax.ShapeDtypeStruct((m // Q_BLOCK, LANES, Q_BLOCK), F32),
                      jax.ShapeDtypeStruct((m // seq, 6 * kvd, seq), F32)]
    else:
        out_specs += [row(4 * kvd), row(2 * kvd), row(qd), row(LANES)]
        out_shape += [jax.ShapeDtypeStruct((m, 4 * kvd), F32), jax.ShapeDtypeStruct((m, 2 * kvd), F32),
                      jax.ShapeDtypeStruct((m, qd), BF16), jax.ShapeDtypeStruct((m, LANES), F32)]
    return pl.pallas_call(
        functools.partial(_inproj_even_kernel, cc=cc, qd=qd, kvd=kvd, tiles=tiles), grid=(m // tm,),
        in_specs=[row(d), _full((1, d)), _full(w_pad.shape)],
        out_specs=out_specs, out_shape=out_shape,
        compiler_params=_cparams(("parallel",)), name="inproj_even",
    )(x, g.reshape(1, d), w_pad)


def _conv_post(y, lg, lb):
    mu = jnp.mean(y, -1, keepdims=True)
    var = jnp.mean(jnp.square(y - mu), -1, keepdims=True)
    yn = (y - mu) * lax.rsqrt(var + LN_EPS) * lg + lb
    return yn * jax.nn.sigmoid(yn)


CONV_SUB = 64
CONV_PAD = 32


def _conv_prompt_kernel(glu_ref, cw_ref, cb_ref, lg_ref, lb_ref, o_ref, ext_ref, y_ref, *, ts, s):
    i = pl.program_id(1)
    c = glu_ref.shape[-1]

    @pl.when(i == 0)
    def _():
        ext_ref[0:CONV_PAD, :] = jnp.zeros((CONV_PAD, c), F32)
        ext_ref[CONV_PAD:CONV_PAD + s, :] = glu_ref[0]
        ext_ref[CONV_PAD + s:CONV_PAD + s + SUBLANES, :] = jnp.zeros((SUBLANES, c), F32)

    lead = CONV_PAD - (CONV_WIDTH - 1)
    span = CONV_SUB + CONV_PAD

    def sub(j, carry):
        r0 = pl.multiple_of(i * ts + j * CONV_SUB, CONV_SUB)
        for c0 in range(0, c, LANES):
            xw = ext_ref[pl.ds(r0, span + SUBLANES), c0:c0 + LANES]
            acc = jnp.zeros((CONV_SUB, LANES), F32) + cb_ref[:, c0:c0 + LANES]
            for r in range(SUBLANES):
                xr = xw if r == 0 else pltpu.roll(xw, span + SUBLANES - r, 0)
                for a in range(span // SUBLANES):
                    w = SUBLANES * a + r - lead
                    if 0 <= w < CONV_WIDTH:
                        acc = acc + xr[SUBLANES * a:SUBLANES * a + CONV_SUB, :] * cw_ref[w:w + 1, c0:c0 + LANES]
            y_ref[:, c0:c0 + LANES] = acc
        o_ref[0, pl.ds(pl.multiple_of(j * CONV_SUB, CONV_SUB), CONV_SUB), :] = _conv_post(
            y_ref[...], lg_ref[...], lb_ref[...]).astype(o_ref.dtype)
        return carry

    lax.fori_loop(0, ts // CONV_SUB, sub, 0)


def conv_prompt(glu, cw, cb, lg, lb, b, s):
    c = glu.shape[-1]
    ts = _row_tile(s, 256)
    vec = lambda a: a.reshape(1, c)
    out = pl.pallas_call(
        functools.partial(_conv_prompt_kernel, ts=ts, s=s), grid=(b, s // ts),
        in_specs=[pl.BlockSpec((1, s, c), lambda bi, i: (bi, 0, 0)), _full((CONV_WIDTH, c)),
                  _full((1, c)), _full((1, c)), _full((1, c))],
        out_specs=pl.BlockSpec((1, ts, c), lambda bi, i: (bi, i, 0)),
        out_shape=jax.ShapeDtypeStruct((b, s, c), BF16),
        scratch_shapes=[pltpu.VMEM((CONV_PAD + s + SUBLANES, c), F32), pltpu.VMEM((CONV_SUB, c), F32)],
        compiler_params=_cparams(("parallel", "arbitrary")), name="conv_prompt",
    )(glu.reshape(b, s, c), cw, vec(cb), vec(lg), vec(lb))
    return out.reshape(b * s, c)


def _conv_sample_kernel(cache_ref, glu_ref, cw_ref, cb_ref, lg_ref, lb_ref, o_ref, st_ref):
    hist = CONV_WIDTH - 1
    cache = cache_ref[...]
    glu = glu_ref[...]
    y = jnp.sum(cache * cw_ref[0:hist, :][None], axis=1) + glu * cw_ref[hist:hist + 1, :] + cb_ref[...]
    o_ref[...] = _conv_post(y, lg_ref[...], lb_ref[...]).astype(o_ref.dtype)
    st_ref[:, 0:hist - 1, :] = cache[:, 1:hist, :]
    st_ref[:, hist - 1:hist, :] = glu[:, None, :]


def conv_sample(cache, glu, cw, cb, lg, lb):
    bd, hist, c = cache.shape
    vec = lambda a: a.reshape(1, c)
    return pl.pallas_call(
        _conv_sample_kernel,
        out_shape=[jax.ShapeDtypeStruct((bd, c), BF16), jax.ShapeDtypeStruct((bd, hist, c), F32)],
        name="conv_sample",
    )(cache, glu, cw, vec(cb), vec(lg), vec(lb))


def _rel_bucket(dist):
    n = jnp.maximum(dist, 0)
    max_exact = NUM_BUCKETS // 2
    nf = jnp.maximum(n, 1).astype(F32)
    large = max_exact + (jnp.log(nf / max_exact) / math.log(MAX_DISTANCE / max_exact)
                         * (NUM_BUCKETS - max_exact)).astype(jnp.int32)
    large = jnp.minimum(large, NUM_BUCKETS - 1)
    return jnp.where(n < max_exact, n, large)


def _bias_of(rel_bias, dist):
    bucket = _rel_bucket(dist)[..., None]
    out = jnp.zeros(bucket.shape[:-1] + (rel_bias.shape[1],), F32)
    for k in range(NUM_BUCKETS):
        out = jnp.where(bucket == k, rel_bias[k].astype(F32), out)
    return out


def _compress_accumulate(load_rows, pe_ref, w_ref, nc):
    accs = []
    for slot in range(2):
        acc = jnp.zeros((nc, w_ref.shape[-1]), F32)
        for j in range(CMP_BLOCK):
            xj = load_rows(slot, j) + pe_ref[slot, j:j + 1, :]
            acc = acc + _dot(xj.astype(BF16), w_ref[slot, j])
        accs.append(acc)
    return jnp.concatenate(accs, axis=1)


def _file_rows(tile_t, xrow, sl, page):
    rows = tile_t.T
    per_page = PAGE_SIZE // CMP_BLOCK
    for cl in range(per_page):
        for a in range(CMP_BLOCK // SUBLANES):
            r0 = cl * CMP_BLOCK + a * SUBLANES
            dst = pl.multiple_of((page * per_page + cl) * SUBLANES, SUBLANES)
            xrow[sl, a, pl.ds(dst, SUBLANES), :] = rows[r0:r0 + SUBLANES, :]


def _compress_filed(xrow, pe_ref, w_ref, nc):
    return _compress_accumulate(
        lambda sl, j: xrow[sl, j // SUBLANES, pl.ds(j % SUBLANES, nc, stride=SUBLANES), :], pe_ref, w_ref, nc)


def _compress_prompt_kernel(x_ref, pe_ref, w_ref, kc_ref, vct_ref, xrow, *, nc, ncp, kvd):
    for p in range(x_ref.shape[2] // PAGE_SIZE):
        for sl in range(2):
            _file_rows(x_ref[0, sl * kvd:(sl + 1) * kvd, p * PAGE_SIZE:(p + 1) * PAGE_SIZE], xrow, sl, p)
    acc = _compress_filed(xrow, pe_ref, w_ref, nc)
    if ncp > nc:
        acc = jnp.concatenate([acc, jnp.zeros((ncp - nc, 2 * kvd), F32)], axis=0)
    kc_ref[0] = acc[:, 0:kvd].astype(BF16)
    vct_ref[0] = acc[:, kvd:2 * kvd].T.astype(BF16)


def compress_prompt(kv_t, pe2, wbd, ncp):
    b, _, s = kv_t.shape
    assert s % PAGE_SIZE == 0
    nc = s // CMP_BLOCK
    kvd = wbd.shape[-1]
    return pl.pallas_call(
        functools.partial(_compress_prompt_kernel, nc=nc, ncp=ncp, kvd=kvd), grid=(b,),
        in_specs=[pl.BlockSpec((1, 2 * kvd, s), lambda bi: (bi, 0, 0)), _full(pe2.shape), _full(wbd.shape)],
        out_specs=[pl.BlockSpec((1, ncp, kvd), lambda bi: (bi, 0, 0)), pl.BlockSpec((1, kvd, ncp), lambda bi: (bi, 0, 0))],
        out_shape=[jax.ShapeDtypeStruct((b, ncp, kvd), BF16), jax.ShapeDtypeStruct((b, kvd, ncp), BF16)],
        scratch_shapes=[pltpu.VMEM((2, CMP_BLOCK // SUBLANES, nc * SUBLANES, kvd), F32)],
        compiler_params=_cparams(("parallel",)), name="compress_prompt",
    )(kv_t, pe2, wbd)


def _select_blocks(score, n_sel, n_cand):
    lane = lax.broadcasted_iota(jnp.int32, score.shape, 1)
    rank = jnp.zeros(score.shape, F32)
    for i in range(n_cand):
        col = score[:, 2 * i:2 * i + 1]
        beats = (col > score) | ((col == score) & (lane > 2 * i))
        rank = rank + beats.astype(F32)
    is_cand = ((lane % 2) == 0) & (lane < 2 * n_cand)
    return (is_cand & (rank < n_sel) & (score >= 0)).astype(F32), rank


def _pair_sum(imp):
    n = imp.shape[1]
    return imp + pltpu.roll(imp, n - 1, 1)


def _rank_rows(score, n_sel, n_cand):
    blk = lax.broadcasted_iota(jnp.int32, score.shape, 0)
    rank = jnp.zeros(score.shape, F32)
    for i in range(n_cand):
        row = score[i:i + 1, :]
        beats = (row > score) | ((row == score) & (blk > i))
        rank = rank + beats.astype(F32)
    return ((rank < n_sel) & (score >= 0)).astype(F32)


def _nsa_prompt_kernel(qt_ref, gt_ref, kc_ref, vct_ref, keys_ref, vt_ref, biasc_ref, btile_ref, o_ref,
                       qt_scr, oc_scr, acc_s, acc_w, imp_scr, sel_scr, out_scr, *, nc, nsb, n_sel):
    qi = pl.program_id(1)
    g, hd, kvh, qb = NSA_GROUP, NSA_HD, NSA_KV_HEADS, Q_BLOCK
    kvd = kvh * hd
    ncp = kc_ref.shape[1]
    nsbp = sel_scr.shape[1]
    q_pos = qi * qb + lax.broadcasted_iota(jnp.int32, (1, qb), 1)
    key_row = lax.broadcasted_iota(jnp.int32, (qb, 1), 0)
    c_row = lax.broadcasted_iota(jnp.int32, (ncp, 1), 0)
    mask_c = (q_pos >= c_row * CMP_BLOCK + (CMP_BLOCK - 1)) & (c_row < nc)
    blk = lax.broadcasted_iota(jnp.int32, (nsbp, 1), 0)
    cur = q_pos // SEL_BLOCK
    forced = (blk == 0) | (blk == cur) | (blk == cur - 1)
    zeros = jnp.zeros((hd, qb), BF16)
    for k in range(kvh):
        for gi in range(g):
            h = k * g + gi
            parts = [zeros] * kvh
            parts[k] = qt_ref[0, h * hd:(h + 1) * hd, :]
            qt_scr[k, :, gi * qb:(gi + 1) * qb] = jnp.concatenate(parts, axis=0)
        s_c = _dot(kc_ref[0], qt_scr[k])
        imp = jnp.zeros((ncp, qb), F32)
        probs = []
        for gi in range(g):
            s = jnp.where(mask_c, s_c[:, gi * qb:(gi + 1) * qb] + biasc_ref[k * g + gi], NEG)
            e = jnp.where(mask_c, jnp.exp(s - jnp.max(s, 0, keepdims=True)), 0.0)
            den = jnp.sum(e, 0, keepdims=True)
            p = e / jnp.where(den > 0, den, 1.0)
            imp = imp + p
            probs.append(p.astype(BF16))
        oc_scr[k] = _dot(vct_ref[0, k * hd:(k + 1) * hd, :], jnp.concatenate(probs, axis=1))
        imp_scr[...] = imp + pltpu.roll(imp, ncp - 1, 0)
        cand = imp_scr[pl.ds(0, nsbp, stride=SEL_RATIO), :]
        score = jnp.where(forced, FORCE_SCORE, jnp.where(blk <= cur, cand, -1.0))
        sel_scr[k] = _rank_rows(jnp.where(blk < nsb, score, -2.0), n_sel, nsb)

    per_tile = qb // SEL_BLOCK
    n_tiles = keys_ref.shape[1] // qb
    first = ([jnp.full((1, qb), NEG, F32)] * g, [jnp.zeros((1, qb), F32)] * g)

    def tile_step(tiles, carry, key_col, val_row, acc_ref, window):
        kts = [jnp.clip(kt, 0, n_tiles - 1) for kt, _ in tiles]
        starts = [pl.multiple_of(kt * qb, qb) for kt in kts]
        k_t = jnp.concatenate([keys_ref[0, pl.ds(r0, qb), key_col:key_col + kvd] for r0 in starts], axis=0)
        dist = jnp.concatenate([jnp.where(active, q_pos - (r0 + key_row), -1)
                                for r0, (_, active) in zip(starts, tiles)], axis=0)
        in_range = dist >= 0
        scores = [_dot(k_t, qt_scr[k]) for k in range(kvh)]
        new, updates = [], []
        for k in range(kvh):
            if window:
                valid = in_range & (dist < WINDOW)
            else:
                pieces = []
                for kt in kts:
                    chosen = jnp.zeros((qb, qb), F32)
                    for j in range(per_tile):
                        row = sel_scr[k, pl.ds(per_tile * kt + j, 1), :]
                        chosen = jnp.where(key_row // SEL_BLOCK == j, row, chosen)
                    pieces.append(chosen)
                valid = in_range & (jnp.concatenate(pieces, axis=0) > 0.5)
            ms, ls = carry[k]
            ms2, ls2, alphas, probs = [], [], [], []
            for gi in range(g):
                bias = jnp.concatenate([btile_ref[jnp.maximum(qi - kt, 0), k * g + gi] for kt in kts], axis=0)
                s = jnp.where(valid, scores[k][:, gi * qb:(gi + 1) * qb] + bias, NEG)
                m_new = jnp.maximum(ms[gi], jnp.max(s, 0, keepdims=True))
                alpha = jnp.exp(ms[gi] - m_new)
                p = jnp.exp(s - jnp.where(m_new == NEG, 0.0, m_new))
                ms2.append(m_new)
                ls2.append(alpha * ls[gi] + jnp.sum(p, 0, keepdims=True))
                alphas.append(alpha)
                probs.append(p.astype(BF16))
            new.append((ms2, ls2))
            updates.append((jnp.concatenate(alphas, axis=1), jnp.concatenate(probs, axis=1)))
        for k, (alpha, prob) in enumerate(updates):
            v_t = jnp.concatenate([vt_ref[kt, val_row + k * hd:val_row + (k + 1) * hd, :] for kt in kts], axis=1)
            acc_ref[k] = acc_ref[k] * alpha + _dot(v_t, prob)
        return tuple(new)

    acc_s[...] = jnp.zeros_like(acc_s)
    acc_w[...] = jnp.zeros_like(acc_w)
    sel_args = dict(key_col=0, val_row=0, acc_ref=acc_s, window=False)
    win_args = dict(key_col=kvd, val_row=kvd, acc_ref=acc_w, window=True)

    def sel_pair(i, carry):
        return tile_step([(2 * i, 2 * i <= qi), (2 * i + 1, 2 * i + 1 <= qi)], carry, **sel_args)

    stat_s = lax.fori_loop(0, qi // 2 + 1, sel_pair, (first,) * kvh)
    stat_w = (first,) * kvh
    win_tiles = [(qi - j, qi - j >= 0) for j in range(WINDOW // qb, -1, -1)]
    for j in range(0, len(win_tiles), 2):
        stat_w = tile_step(win_tiles[j:j + 2], stat_w, **win_args)
    gt = gt_ref[0]
    for k in range(kvh):
        for gi in range(g):
            h = k * g + gi
            cols = slice(gi * qb, (gi + 1) * qb)
            l_s, l_w = stat_s[k][1][gi], stat_w[k][1][gi]
            o_s = acc_s[k, :, cols] / jnp.where(l_s > 0, l_s, 1.0)
            o_w = acc_w[k, :, cols] / jnp.where(l_w > 0, l_w, 1.0)
            out_scr[h * hd:(h + 1) * hd, :] = (gt[3 * h:3 * h + 1, :] * oc_scr[k, :, cols]
                                               + gt[3 * h + 1:3 * h + 2, :] * o_s + gt[3 * h + 2:3 * h + 3, :] * o_w)
    o_ref[0] = out_scr[...].T.astype(o_ref.dtype)


def nsa_prompt(qt, gt, kc, vct, keys, vt, rel_bias, b, s):
    qb = Q_BLOCK
    nq = s // qb
    nc = s // CMP_BLOCK
    ncp = kc.shape[1]
    nsb = s // SEL_BLOCK
    nsbp = _round_up(nsb, SUBLANES)
    assert SEL_RATIO * nsbp <= ncp
    n_sel = min(TOP_N, nsb)
    nh = NSA_HEADS
    hq = qt.shape[1]
    kvd = kc.shape[2]
    glanes = NSA_GROUP * qb
    cend = jnp.arange(ncp)[:, None] * CMP_BLOCK + (CMP_BLOCK - 1)
    biasc = jnp.transpose(_bias_of(rel_bias, jnp.arange(s)[None, :] - cend), (2, 0, 1))
    r = jnp.arange(qb)
    dist = jnp.arange(nq)[:, None, None] * qb + r[None, None, :] - r[None, :, None]
    btile = jnp.transpose(_bias_of(rel_bias, dist), (0, 3, 1, 2))
    out = pl.pallas_call(
        functools.partial(_nsa_prompt_kernel, nc=nc, nsb=nsb, n_sel=n_sel), grid=(b, nq),
        in_specs=[pl.BlockSpec((1, hq, qb), lambda bi, i: (bi * nq + i, 0, 0)),
                  pl.BlockSpec((1, LANES, qb), lambda bi, i: (bi * nq + i, 0, 0)),
                  pl.BlockSpec((1, ncp, kvd), lambda bi, i: (bi, 0, 0)),
                  pl.BlockSpec((1, kvd, ncp), lambda bi, i: (bi, 0, 0)),
                  pl.BlockSpec((1, s, 2 * kvd), lambda bi, i: (bi, 0, 0)),
                  pl.BlockSpec((nq, 2 * kvd, qb), lambda bi, i: (bi, 0, 0)),
                  pl.BlockSpec((nh, ncp, qb), lambda bi, i: (0, 0, i)),
                  _full(btile.shape)],
        out_specs=pl.BlockSpec((1, qb, hq), lambda bi, i: (bi, i, 0)),
        out_shape=jax.ShapeDtypeStruct((b, s, hq), BF16),
        scratch_shapes=[pltpu.VMEM((NSA_KV_HEADS, kvd, glanes), BF16), pltpu.VMEM((NSA_KV_HEADS, NSA_HD, glanes), F32),
                        pltpu.VMEM((NSA_KV_HEADS, NSA_HD, glanes), F32), pltpu.VMEM((NSA_KV_HEADS, NSA_HD, glanes), F32),
                        pltpu.VMEM((ncp, qb), F32), pltpu.VMEM((NSA_KV_HEADS, nsbp, qb), F32),
                        pltpu.VMEM((hq, qb), F32)],
        compiler_params=_cparams(("parallel", "arbitrary"), VMEM_LIMIT), name="nsa_prompt",
    )(qt, gt, kc, vct, keys.reshape(b, s, 2 * kvd), vt, biasc, btile)
    return out.reshape(b * s, hq)


def _round_up(x, m):
    return (x + m - 1) // m * m


def prep_even(w_in, pe, wc):
    d, n = w_in.shape
    n_pad = _round_up(n - 3 * NSA_HEADS, LANES) + LANES
    w_pad = jnp.zeros((d, n_pad), BF16).at[:, :n].set(w_in.astype(BF16))
    hd = NSA_HD
    pe2 = jnp.tile(pe, (1, 1, NSA_KV_HEADS))
    zero = jnp.zeros_like(wc)
    wbd = jnp.concatenate([jnp.concatenate([wc if i == j else zero for j in range(NSA_KV_HEADS)], axis=-1)
                           for i in range(NSA_KV_HEADS)], axis=-2)
    return dict(w_in=w_pad, pe2=pe2, wbd=wbd.astype(BF16))


def _memkv_kernel(x_ref, g_ref, w_ref, o_ref, ob_ref, *, hd, nh):
    y = _dot(_rms(x_ref[...], g_ref[...]).astype(BF16), w_ref[...])
    ob_ref[...] = y.astype(BF16)
    tm = y.shape[0]
    chunks = hd // LANES
    period = 2 * chunks * nh
    for kv in range(2):
        for h in range(nh):
            for c in range(chunks):
                col = (kv * nh + h) * hd + c * LANES
                o_ref[pl.ds((kv * chunks + c) * nh + h, tm, stride=period), :] = y[:, col:col + LANES]


def memkv(mem, g, w, nh):
    m, d = mem.shape
    n = w.shape[1]
    hd = n // (2 * nh)
    per_tok = n // LANES
    tm = _row_tile(m, 256)
    return pl.pallas_call(
        functools.partial(_memkv_kernel, hd=hd, nh=nh), grid=(m // tm,),
        in_specs=[pl.BlockSpec((tm, d), lambda i: (i, 0)), _full((1, d)), _full(w.shape)],
        out_specs=[pl.BlockSpec((tm * per_tok, LANES), lambda i: (i, 0)), pl.BlockSpec((tm, n), lambda i: (i, 0))],
        out_shape=[jax.ShapeDtypeStruct((m * per_tok, LANES), F32), jax.ShapeDtypeStruct((m, n), BF16)],
        compiler_params=_cparams(("parallel",)), name="memkv",
    )(mem, g.reshape(1, d), w)


def _xattn_core(q, kv, hd):
    nh = q.shape[1] // hd
    outs = []
    for h in range(nh):
        s = _dot_nt(q[:, h * hd:(h + 1) * hd], kv[:, h * hd:(h + 1) * hd])
        e = jnp.exp(s - jnp.max(s, -1, keepdims=True))
        p = e / jnp.sum(e, -1, keepdims=True)
        outs.append(_dot(p.astype(BF16), kv[:, (nh + h) * hd:(nh + h + 1) * hd]))
    return jnp.concatenate(outs, axis=1).astype(BF16)


def _xattn_prompt_kernel(*refs, hd, n_in):
    x_ref, g_ref, wq_ref, kv_ref, wo_ref = refs[:5]
    a_refs = refs[5:5 + n_in]
    w_refs = refs[5 + n_in:5 + 2 * n_in]
    o_ref = refs[5 + 2 * n_in]
    x = x_ref[0]
    for a_ref, w_ref in zip(a_refs, w_refs):
        x = x + _dot(a_ref[0], w_ref[...])
    q = (_dot(_rms(x, g_ref[...]).astype(BF16), wq_ref[...]) * (hd ** -0.5)).astype(BF16)
    o = _xattn_core(q, kv_ref[0], hd)
    o_ref[0] = x + _dot(o, wo_ref[...])


def xattn_prompt(x, g, wq, kvb, wo, b, s, acts=(), ws=()):
    d = x.shape[1]
    mt = kvb.shape[0] // b
    tm = _row_tile(s, 512)
    tile = lambda n: pl.BlockSpec((1, tm, n), lambda bi, i: (bi, i, 0))
    out = pl.pallas_call(
        functools.partial(_xattn_prompt_kernel, hd=d // X_HEADS, n_in=len(acts)), grid=(b, s // tm),
        in_specs=[tile(d), _full((1, d)), _full(wq.shape),
                  pl.BlockSpec((1, mt, kvb.shape[1]), lambda bi, i: (bi, 0, 0)), _full(wo.shape)]
        + [tile(a.shape[1]) for a in acts] + [_full(w.shape) for w in ws],
        out_specs=tile(d),
        out_shape=jax.ShapeDtypeStruct((b, s, d), F32),
        compiler_params=_cparams(("parallel", "parallel"), VMEM_LIMIT), name="xattn_prompt",
    )(x.reshape(b, s, d), g.reshape(1, d), wq, kvb.reshape(b, mt, -1), wo,
      *[a.reshape(b, s, -1) for a in acts], *ws)
    return out.reshape(b * s, d)


def _xattn_sample_kernel(x_ref, g_ref, wq_ref, kv_ref, wo_ref, o_ref, q_scr, a_scr, *, hd, mt):
    bi = pl.program_id(0)
    nb = pl.num_programs(0)
    nh = wq_ref.shape[1] // hd

    @pl.when(bi == 0)
    def _():
        q_scr[...] = _dot(_rms(x_ref[...], g_ref[...]).astype(BF16), wq_ref[...]) * (hd ** -0.5)

    q = jnp.broadcast_to(q_scr[pl.ds(bi, 1), :], (SUBLANES, q_scr.shape[1])).astype(BF16)
    chunks = hd // LANES
    period = 2 * chunks * nh

    def head_rows(kv, h):
        return jnp.concatenate([kv_ref[0, pl.ds((kv * chunks + c) * nh + h, mt, stride=period), :]
                                for c in range(chunks)], axis=1).astype(BF16)

    outs = []
    for h in range(nh):
        s = _dot_nt(q[:, h * hd:(h + 1) * hd], head_rows(0, h))
        e = jnp.exp(s - jnp.max(s, -1, keepdims=True))
        p = e / jnp.sum(e, -1, keepdims=True)
        outs.append(_dot(p.astype(BF16), head_rows(1, h)))
    a_scr[pl.ds(bi, 1), :] = jnp.concatenate(outs, axis=1)[0:1, :]

    @pl.when(bi == nb - 1)
    def _():
        o_ref[...] = x_ref[...] + _dot(a_scr[...].astype(BF16), wo_ref[...])


def xattn_sample(x, g, wq, kv_rows, wo, layer, mt):
    bd, d = x.shape
    rows = kv_rows.shape[1]
    return pl.pallas_call(
        functools.partial(_xattn_sample_kernel, hd=d // X_HEADS, mt=mt), grid=(bd,),
        in_specs=[_full((bd, d)), _full((1, d)), _full(wq.shape),
                  pl.BlockSpec((1, rows, LANES), lambda bi: (layer * bd + bi, 0, 0)), _full(wo.shape)],
        out_specs=_full((bd, d)),
        out_shape=jax.ShapeDtypeStruct((bd, d), F32),
        scratch_shapes=[pltpu.VMEM((bd, wq.shape[1]), F32), pltpu.VMEM((bd, wq.shape[1]), F32)],
        compiler_params=_cparams(("arbitrary",), VMEM_LIMIT), name="xattn_sample",
    )(x, g.reshape(1, d), wq, kv_rows, wo)


def _ffn_kernel(x_ref, g_ref, wg_ref, wu_ref, wd_ref, o_ref, h_scr, acc_scr):
    c = pl.program_id(1)

    @pl.when(c == 0)
    def _():
        h_scr[...] = _rms(x_ref[...], g_ref[...]).astype(BF16)
        acc_scr[...] = x_ref[...]

    h = h_scr[...]
    gate = _dot(h, wg_ref[...])
    up = _dot(h, wu_ref[...])
    act = (gate * jax.nn.sigmoid(gate) * up).astype(BF16)
    acc_scr[...] += _dot(act, wd_ref[...])

    @pl.when(c == pl.num_programs(1) - 1)
    def _():
        o_ref[...] = acc_scr[...]


def _ff_chunk(dff, pref):
    c = dff
    for n in range(1, dff // LANES + 1):
        if dff % n == 0 and (dff // n) % LANES == 0 and dff // n <= pref:
            c = dff // n
            break
    return c


def ffn(x, g, w_gu, w_dn):
    m, d = x.shape
    dff = w_dn.shape[0]
    tm = _row_tile(m, 512)
    fc = _ff_chunk(dff, 1408)
    nch = dff // fc
    return pl.pallas_call(
        _ffn_kernel, grid=(m // tm, nch),
        in_specs=[pl.BlockSpec((tm, d), lambda i, c: (i, 0)), _full((1, d)),
                  pl.BlockSpec((d, fc), lambda i, c: (0, c)),
                  pl.BlockSpec((d, fc), lambda i, c: (0, nch + c)),
                  pl.BlockSpec((fc, d), lambda i, c: (c, 0))],
        out_specs=pl.BlockSpec((tm, d), lambda i, c: (i, 0)),
        out_shape=jax.ShapeDtypeStruct((m, d), F32),
        scratch_shapes=[pltpu.VMEM((tm, d), BF16), pltpu.VMEM((tm, d), F32)],
        compiler_params=_cparams(("parallel", "arbitrary"), VMEM_LIMIT), name="ffn",
    )(x, g.reshape(1, d), w_gu, w_gu, w_dn)


def _router_kernel(x_ref, g_ref, w_ref, b_ref, comb_ref, h_ref, mask_ref, cnt_ref, *, ne):
    h = _rms(x_ref[...], g_ref[...]).astype(BF16)
    h_ref[...] = h
    logits = _dot(h, w_ref[...]) + b_ref[...]
    lane = lax.broadcasted_iota(jnp.int32, logits.shape, 1)
    logits = jnp.where(lane < ne, logits, -jnp.inf)
    v1 = jnp.max(logits, -1, keepdims=True)
    i1 = jnp.min(jnp.where(logits == v1, lane, LANES), -1, keepdims=True)
    rest = jnp.where(lane == i1, -jnp.inf, logits)
    v2 = jnp.max(rest, -1, keepdims=True)
    i2 = jnp.min(jnp.where(rest == v2, lane, LANES), -1, keepdims=True)
    e2 = jnp.exp(v2 - v1)
    den = 1.0 + e2
    comb_ref[...] = jnp.where(lane == i1, 1.0 / den, 0.0) + jnp.where(lane == i2, e2 / den, 0.0)
    chosen = jnp.where((lane == i1) | (lane == i2), 1.0, 0.0)
    mask_ref[...] = chosen.astype(BF16)

    @pl.when(pl.program_id(0) == 0)
    def _():
        cnt_ref[...] = jnp.zeros_like(cnt_ref)

    cnt_ref[0:1, :] += jnp.sum(chosen, axis=0, keepdims=True)


def router(x, g, w_r, b_r):
    m, d = x.shape
    ne = w_r.shape[1]
    w_pad = jnp.zeros((d, LANES), BF16).at[:, :ne].set(w_r.astype(BF16))
    b_pad = jnp.zeros((1, LANES), F32).at[0, :ne].set(b_r.astype(F32))
    tm = _row_tile(m, 512)
    row = lambda n: pl.BlockSpec((tm, n), lambda i: (i, 0))
    return pl.pallas_call(
        functools.partial(_router_kernel, ne=ne), grid=(m // tm,),
        in_specs=[row(d), _full((1, d)), _full((d, LANES)), _full((1, LANES))],
        out_specs=[row(LANES), row(d), row(LANES), _full((SUBLANES, LANES))],
        out_shape=[jax.ShapeDtypeStruct((m, LANES), F32), jax.ShapeDtypeStruct((m, d), BF16),
                   jax.ShapeDtypeStruct((m, LANES), BF16), jax.ShapeDtypeStruct((SUBLANES, LANES), F32)],
        compiler_params=_cparams(("arbitrary",)), name="router",
    )(x, g.reshape(1, d), w_pad, b_pad)


def _residual_out(y, gain_ref, norm):
    return _rms(y, gain_ref[...]) if norm else y


def _moe_kernel(x_ref, h_ref, comb_ref, wg_ref, wu_ref, wd_ref, gain_ref, o_ref, acc_scr, *, norm):
    e = pl.program_id(1)

    @pl.when(e == 0)
    def _():
        acc_scr[...] = jnp.zeros_like(acc_scr)

    h = h_ref[...]
    gate = _dot(h, wg_ref[0])
    up = _dot(h, wu_ref[0])
    act = (gate * jax.nn.sigmoid(gate) * up).astype(BF16)
    y = _dot(act, wd_ref[0])
    comb = comb_ref[...]
    lane = lax.broadcasted_iota(jnp.int32, comb.shape, 1)
    acc_scr[...] += jnp.sum(jnp.where(lane == e, comb, 0.0), -1, keepdims=True) * y

    @pl.when(e == pl.num_programs(1) - 1)
    def _():
        o_ref[...] = _residual_out(x_ref[...] + acc_scr[...], gain_ref, norm)


def moe(x, h, comb, w_gu, w_dn, final_g=None):
    m, d = x.shape
    ne, dfe = w_dn.shape[:2]
    tm = _row_tile(m, 512)
    gain = jnp.ones((1, d), F32) if final_g is None else final_g.reshape(1, d)
    return pl.pallas_call(
        functools.partial(_moe_kernel, norm=final_g is not None), grid=(m // tm, ne),
        in_specs=[pl.BlockSpec((tm, d), lambda i, e: (i, 0)), pl.BlockSpec((tm, d), lambda i, e: (i, 0)),
                  pl.BlockSpec((tm, LANES), lambda i, e: (i, 0)),
                  pl.BlockSpec((1, d, dfe), lambda i, e: (e, 0, 0)),
                  pl.BlockSpec((1, d, dfe), lambda i, e: (e, 0, 1)),
                  pl.BlockSpec((1, dfe, d), lambda i, e: (e, 0, 0)), _full((1, d))],
        out_specs=pl.BlockSpec((tm, d), lambda i, e: (i, 0)),
        out_shape=jax.ShapeDtypeStruct((m, d), F32),
        scratch_shapes=[pltpu.VMEM((tm, d), F32)],
        compiler_params=_cparams(("parallel", "arbitrary"), VMEM_LIMIT), name="moe",
    )(x, h, comb, w_gu, w_gu, w_dn, gain)


MOE_TILE = 256


def _moe_pos_kernel(mask_ref, comb_ref, tri_ref, base_ref, post_ref, pos2_ref, wab_ref, stab_ref, run_scr, *, nep):
    sb = pl.program_id(0)
    nb = pl.num_programs(0)

    @pl.when(sb == 0)
    def _():
        run_scr[...] = jnp.zeros_like(run_scr)
        stab_ref[...] = jnp.zeros_like(stab_ref)

    a = mask_ref[...]
    af = a.astype(F32)
    start = base_ref[...] + run_scr[...]
    stab_ref[pl.ds(sb, 1), :] = start.astype(jnp.int32)
    rank = _dot(tri_ref[...], a)
    pos = jnp.where(af > 0, start + rank, -1.0)
    lane = lax.broadcasted_iota(jnp.int32, pos.shape, 1)
    first_e = jnp.min(jnp.where(af > 0, lane, LANES), -1, keepdims=True)
    last_e = jnp.max(jnp.where(af > 0, lane, -1), -1, keepdims=True)
    comb = comb_ref[...]
    w_a = jnp.sum(jnp.where(lane == first_e, comb, 0.0), -1, keepdims=True)
    w_b = jnp.sum(jnp.where(lane == last_e, comb, 0.0), -1, keepdims=True)
    wab_ref[...] = jnp.where(lane == 0, w_a, jnp.where(lane == 1, w_b, 0.0))
    pos_t = pos.T[0:nep, :]
    post_ref[0] = pos_t.astype(jnp.int32)
    row = lax.broadcasted_iota(jnp.int32, pos_t.shape, 0)
    first_r = jnp.min(jnp.where(pos_t >= 0, row, nep), 0, keepdims=True)
    last_r = jnp.max(jnp.where(pos_t >= 0, row, -1), 0, keepdims=True)
    pos_a = jnp.sum(jnp.where(row == first_r, pos_t, 0.0), 0, keepdims=True)
    pos_b = jnp.sum(jnp.where(row == last_r, pos_t, 0.0), 0, keepdims=True)
    pos2_ref[0] = jnp.where(row == 0, pos_a, jnp.where(row == 1, pos_b, 0.0)).astype(jnp.int32)
    run_scr[...] += jnp.sum(af, axis=0, keepdims=True)

    @pl.when(sb == nb - 1)
    def _():
        stab_ref[pl.ds(nb, 1), :] = (base_ref[...] + run_scr[...]).astype(jnp.int32)


def _moe_expert_kernel(te_ref, lo_ref, cnt_ref, ring_ref, nt_ref, h_ref, post_ref, wg_ref, wu_ref, wd_ref, y_ref,
                       hbuf, xg_scr, sem, *, t):
    i = pl.program_id(0)

    @pl.when(i >= nt_ref[0])
    def _():
        y_ref[...] = jnp.zeros_like(y_ref)

    nbuf = hbuf.shape[0]
    ahead = nbuf - 1

    def copy(sb, slot):
        return pltpu.make_async_copy(h_ref.at[pl.ds(pl.multiple_of(sb * t, t), t), :], hbuf.at[slot], sem.at[slot])

    def start_first(tile):
        for j in range(ahead):
            @pl.when(j < cnt_ref[tile])
            def _():
                copy(lo_ref[tile] + j, (ring_ref[tile] + j) % nbuf).start()

    @pl.when(i == 0)
    def _():
        start_first(0)

    @pl.when(i < nt_ref[0])
    def _():
        e = te_ref[i]
        lo = lo_ref[i]
        n = cnt_ref[i]
        ring = ring_ref[i]
        xg_scr[...] = jnp.zeros_like(xg_scr)
        row = i * t + lax.broadcasted_iota(jnp.int32, (t, 1), 0)

        def body(j, c):
            slot = (ring + j) % nbuf
            copy(lo + j, slot).wait()

            @pl.when(j + ahead < n)
            def _():
                copy(lo + j + ahead, (ring + j + ahead) % nbuf).start()

            src_pos = post_ref[lo + j, pl.ds(e, 1), :]
            onehot = jnp.where(src_pos == row, 1.0, 0.0).astype(BF16)
            xg_scr[...] += _dot(onehot, hbuf[slot])
            return c

        lax.fori_loop(0, n, body, 0)

        @pl.when(i + 1 < nt_ref[0])
        def _():
            start_first(i + 1)

        x = xg_scr[...].astype(BF16)
        gate = _dot(x, wg_ref[0])
        up = _dot(x, wu_ref[0])
        act = (gate * jax.nn.sigmoid(gate) * up).astype(BF16)
        y_ref[...] = _dot(act, wd_ref[0])


def _moe_combine_kernel(x_ref, wab_ref, pos_ref, nxt_ref, gain_ref, ys_ref, o_ref, ybuf, sem, *, t, norm):
    sb = pl.program_id(0)
    nb = pl.num_programs(0)

    def row_copy(p_ref, which, slot, tok):
        return pltpu.make_async_copy(ys_ref.at[pl.ds(p_ref[0, which, tok], 1), :],
                                     ybuf.at[slot, which, pl.ds(tok, 1), :], sem.at[slot])

    def start_all(p_ref, slot):
        def body(tok, c):
            row_copy(p_ref, 0, slot, tok).start()
            row_copy(p_ref, 1, slot, tok).start()
            return c
        lax.fori_loop(0, t, body, 0, unroll=8)

    @pl.when(sb == 0)
    def _():
        start_all(pos_ref, 0)

    @pl.when(sb + 1 < nb)
    def _():
        start_all(nxt_ref, (sb + 1) % 2)

    slot = sb % 2

    def wait_body(tok, c):
        row_copy(pos_ref, 0, slot, tok).wait()
        row_copy(pos_ref, 1, slot, tok).wait()
        return c
    lax.fori_loop(0, t, wait_body, 0, unroll=8)
    w = wab_ref[...]
    y = x_ref[...] + w[:, 0:1] * ybuf[slot, 0] + w[:, 1:2] * ybuf[slot, 1]
    o_ref[...] = _residual_out(y, gain_ref, norm)


def moe_grouped(x, h, comb, mask, counts, w_gu, w_dn, final_g=None):
    m, d = x.shape
    ne, dfe = w_dn.shape[:2]
    t = MOE_TILE
    assert m % t == 0
    nb = m // t
    nep = _round_up(ne, SUBLANES)
    nbp = _round_up(nb + 1, SUBLANES)
    k_top = TOP_K
    nt_max = k_top * m // t + ne
    cnt = counts[0, :ne].astype(jnp.int32)
    cnt_pad = (cnt + t - 1) // t * t
    ends = jnp.cumsum(cnt_pad)
    base = ends - cnt_pad
    base_row = jnp.zeros((1, LANES), F32).at[0, :ne].set(base.astype(F32))
    idx = lax.broadcasted_iota(jnp.int32, (t, t), 0)
    tri = jnp.where(lax.broadcasted_iota(jnp.int32, (t, t), 1) < idx, 1.0, 0.0).astype(BF16)
    blk = lambda n2: pl.BlockSpec((t, n2), lambda i: (i, 0))
    post, pos2, wab, stab = pl.pallas_call(
        functools.partial(_moe_pos_kernel, nep=nep), grid=(nb,),
        in_specs=[blk(LANES), blk(LANES), _full((t, t)), _full((1, LANES))],
        out_specs=[pl.BlockSpec((1, nep, t), lambda i: (i, 0, 0)), pl.BlockSpec((1, nep, t), lambda i: (i, 0, 0)),
                   blk(LANES), _full((nbp, LANES))],
        out_shape=[jax.ShapeDtypeStruct((nb, nep, t), jnp.int32), jax.ShapeDtypeStruct((nb, nep, t), jnp.int32),
                   jax.ShapeDtypeStruct((m, LANES), F32), jax.ShapeDtypeStruct((nbp, LANES), jnp.int32)],
        scratch_shapes=[pltpu.VMEM((1, LANES), F32)],
        compiler_params=_cparams(("arbitrary",)), name="moe_positions",
    )(mask, comb, tri, base_row)
    r0 = jnp.arange(nt_max, dtype=jnp.int32) * t
    tile_e = jnp.minimum(jnp.sum(ends[None, :] <= r0[:, None], axis=1), ne - 1).astype(jnp.int32)
    n_tiles = (ends[-1] // t).astype(jnp.int32).reshape(1)
    s_e = stab[:nb + 1, :ne][:, tile_e]
    lo = jnp.sum(s_e[1:] <= r0[None, :], axis=0)
    hi = jnp.sum(s_e[:nb] < r0[None, :] + t, axis=0) - 1
    lo = jnp.clip(lo, 0, nb - 1).astype(jnp.int32)
    hi = jnp.clip(hi, lo, nb - 1).astype(jnp.int32)
    n_src = jnp.where(jnp.arange(nt_max) < n_tiles[0], hi - lo + 1, 0).astype(jnp.int32)
    n_ring = 6
    ring = ((jnp.cumsum(n_src) - n_src) % n_ring).astype(jnp.int32)
    w_spec = lambda shape, col: pl.BlockSpec(shape, lambda i, te, *_: (te[i], 0, col))
    grid_spec = pltpu.PrefetchScalarGridSpec(
        num_scalar_prefetch=5, grid=(nt_max,),
        in_specs=[pl.BlockSpec(memory_space=pl.ANY),
                  pl.BlockSpec((nb, nep, t), lambda i, *_: (0, 0, 0)),
                  w_spec((1, d, dfe), 0), w_spec((1, d, dfe), 1), w_spec((1, dfe, d), 0)],
        out_specs=pl.BlockSpec((t, d), lambda i, *_: (i, 0)),
        scratch_shapes=[pltpu.VMEM((n_ring, t, d), BF16), pltpu.VMEM((t, d), F32),
                        pltpu.SemaphoreType.DMA((n_ring,))])
    ys = pl.pallas_call(
        functools.partial(_moe_expert_kernel, t=t), grid_spec=grid_spec,
        out_shape=jax.ShapeDtypeStruct((nt_max * t, d), F32),
        compiler_params=_cparams(("arbitrary",), VMEM_LIMIT), name="moe_experts",
    )(tile_e, lo, n_src, ring, n_tiles, h, post, w_gu, w_gu, w_dn)
    smem_blk = lambda f: pl.BlockSpec((1, nep, t), f, memory_space=pltpu.SMEM)
    gain = jnp.ones((1, d), F32) if final_g is None else final_g.reshape(1, d)
    return pl.pallas_call(
        functools.partial(_moe_combine_kernel, t=t, norm=final_g is not None), grid=(nb,),
        in_specs=[blk(d), blk(LANES), smem_blk(lambda i: (i, 0, 0)),
                  smem_blk(lambda i: (jnp.minimum(i + 1, nb - 1), 0, 0)), _full((1, d)),
                  pl.BlockSpec(memory_space=pl.ANY)],
        out_specs=blk(d),
        out_shape=jax.ShapeDtypeStruct((m, d), F32),
        scratch_shapes=[pltpu.VMEM((2, 2, t, d), F32), pltpu.SemaphoreType.DMA((2,))],
        compiler_params=_cparams(("arbitrary",), VMEM_LIMIT), name="moe_combine",
    )(x, wab, pos2, pos2, gain, ys)


def _inproj_odd_kernel(x_ref, g_ref, w_ref, bias_ref, q_ref, k_ref, v_ref, og_ref, gc_ref, gr_ref, *, hq, hv, nh):
    xn = _rms(x_ref[...], g_ref[...]).astype(BF16)

    def mm(lo, hi):
        return _dot(xn, w_ref[:, lo:hi])

    dk = hq // nh
    q_ref[...] = mm(0, hq).astype(BF16)
    k_ref[...] = (mm(hq, 2 * hq) * (dk ** -0.5)).astype(BF16)
    v_ref[...] = mm(2 * hq, 2 * hq + hv).astype(BF16)
    og_ref[...] = jax.nn.sigmoid(mm(2 * hq + hv, 2 * hq + 2 * hv))
    gi = mm(2 * hq + 2 * hv, 2 * hq + 2 * hv + LANES) + bias_ref[...]
    lane = lax.broadcasted_iota(jnp.int32, gi.shape, 1)
    gates = jnp.where(lane < nh, gi, jax.nn.log_sigmoid(gi))
    gc_ref[...] = gates
    gr_ref[...] = gates.T[0:SUBLANES, :]


def inproj_odd(x, g, w_pad, gate_bias):
    m, d = x.shape
    nh = MLSTM_HEADS
    hq = hv = d
    tm = _row_tile(m, 256)
    row = lambda n: pl.BlockSpec((tm, n), lambda i: (i, 0))
    outs = [(hq, BF16), (hq, BF16), (hv, BF16), (hv, F32), (LANES, F32)]
    if tm % LANES:
        gr_spec = _full((SUBLANES, m))
    else:
        gr_spec = pl.BlockSpec((SUBLANES, tm), lambda i: (0, i))
    return pl.pallas_call(
        functools.partial(_inproj_odd_kernel, hq=hq, hv=hv, nh=nh), grid=(m // tm,),
        in_specs=[row(d), _full((1, d)), _full(w_pad.shape), _full((1, LANES))],
        out_specs=[row(n) for n, _ in outs] + [gr_spec],
        out_shape=[jax.ShapeDtypeStruct((m, n), t) for n, t in outs] + [jax.ShapeDtypeStruct((SUBLANES, m), F32)],
        compiler_params=_cparams(("parallel",), VMEM_LIMIT), name="inproj_odd",
    )(x, g.reshape(1, d), w_pad, gate_bias)


def prep_odd(w_in, b_i, b_f):
    d, n = w_in.shape
    n_pad = _round_up(n - 2 * MLSTM_HEADS, LANES) + LANES
    w_pad = jnp.zeros((d, n_pad), BF16).at[:, :n].set(w_in.astype(BF16))
    bias = jnp.zeros((1, LANES), F32).at[0, :2 * MLSTM_HEADS].set(jnp.concatenate([b_i, b_f]).astype(F32))
    return dict(w_in=w_pad, bias=bias)


def _mlstm_prompt_kernel(q_ref, k_ref, v_ref, og_ref, gc_ref, gr_ref, gain_ref, hn_ref, c_ref, n_ref, m_ref,
                         *, nh, dk, dv, ln):
    ci = pl.program_id(1)

    @pl.when(ci == 0)
    def _():
        c_ref[...] = jnp.zeros_like(c_ref)
        n_ref[...] = jnp.zeros_like(n_ref)
        m_ref[...] = jnp.full(m_ref.shape, NEG, F32)

    row = lax.broadcasted_iota(jnp.int32, (ln, ln), 0)
    col = lax.broadcasted_iota(jnp.int32, (ln, ln), 1)
    tri = row >= col
    gc = gc_ref[...]
    gr = gr_ref[...]
    for h in range(nh):
        q = q_ref[:, h * dk:(h + 1) * dk]
        k = k_ref[:, h * dk:(h + 1) * dk]
        v = v_ref[:, h * dv:(h + 1) * dv]
        ig_c, lf_c = gc[:, h:h + 1], gc[:, nh + h:nh + h + 1]
        ig_r, lf_r = gr[h:h + 1, :], gr[nh + h:nh + h + 1, :]
        b_c = jnp.sum(jnp.where(tri, lf_r, 0.0), axis=1, keepdims=True)
        b_r = jnp.sum(jnp.where(row <= col, lf_c, 0.0), axis=0, keepdims=True)
        m_prev = m_ref[0, h:h + 1, 0:1]
        c_prev = c_ref[0, h]
        n_prev = n_ref[0, h:h + 1, :]
        dmat = jnp.where(tri, b_c - b_r + ig_r, NEG)
        inter = b_c + m_prev
        mt = jnp.maximum(inter, jnp.max(dmat, -1, keepdims=True))
        wm = jnp.exp(dmat - mt)
        a = jnp.exp(inter - mt)
        wqk = wm * _dot_nt(q, k)
        num = a * _dot_nt(q, c_prev.astype(BF16)) + _dot(wqk.astype(BF16), v)
        den = a * jnp.sum(q.astype(F32) * n_prev, -1, keepdims=True) + jnp.sum(wqk, -1, keepdims=True)
        hh = num / jnp.maximum(jnp.abs(den), jnp.exp(-mt))
        b_end = b_c[ln - 1:ln, :]
        m_new = mt[ln - 1:ln, :]
        a_end = jnp.exp(b_end + m_prev - m_new)
        w_s = jnp.exp(b_end - b_c + ig_c - m_new)
        c_ref[0, h] = a_end * c_prev + _dot_tn((v.astype(F32) * w_s).astype(BF16), k)
        n_ref[0, h:h + 1, :] = a_end * n_prev + jnp.sum(w_s * k.astype(F32), axis=0, keepdims=True)
        m_ref[0, h:h + 1, :] = jnp.broadcast_to(m_new, (1, m_ref.shape[2]))
        hn = hh * lax.rsqrt(jnp.mean(hh * hh, -1, keepdims=True) + RMS_EPS)
        hn = hn * gain_ref[:, h * dv:(h + 1) * dv] * og_ref[:, h * dv:(h + 1) * dv]
        hn_ref[:, h * dv:(h + 1) * dv] = hn.astype(hn_ref.dtype)


def mlstm_prompt(q, k, v, og, gc, gr, gain, b, s):
    m, d = q.shape
    nh = MLSTM_HEADS
    dk = dv = d // nh
    ln = _row_tile(s, MLSTM_CHUNK)
    nch = s // ln
    row = lambda n: pl.BlockSpec((ln, n), lambda bi, ci: (bi * nch + ci, 0))
    return pl.pallas_call(
        functools.partial(_mlstm_prompt_kernel, nh=nh, dk=dk, dv=dv, ln=ln), grid=(b, nch),
        in_specs=[row(d), row(d), row(d), row(d), row(LANES),
                  pl.BlockSpec((SUBLANES, ln), lambda bi, ci: (0, bi * nch + ci)), _full((1, d))],
        out_specs=[row(d), pl.BlockSpec((1, nh, dv, dk), lambda bi, ci: (bi, 0, 0, 0)),
                   pl.BlockSpec((1, nh, dk), lambda bi, ci: (bi, 0, 0)),
                   pl.BlockSpec((1, nh, LANES), lambda bi, ci: (bi, 0, 0))],
        out_shape=[jax.ShapeDtypeStruct((m, d), BF16), jax.ShapeDtypeStruct((b, nh, dv, dk), F32),
                   jax.ShapeDtypeStruct((b, nh, dk), F32), jax.ShapeDtypeStruct((b, nh, LANES), F32)],
        compiler_params=_cparams(("parallel", "arbitrary"), VMEM_LIMIT), name="mlstm_prompt",
    )(q, k, v, og, gc, gr, gain.reshape(1, d))


def _mlstm_sample_kernel(q_ref, k_ref, v_ref, og_ref, g_ref, gain_ref, c_ref, n_ref, m_ref,
                         hn_ref, co_ref, no_ref, mo_ref, *, nh):
    row = lax.broadcasted_iota(jnp.int32, (SUBLANES, 1), 0)
    for h in range(nh):
        q = q_ref[0, h:h + 1, :]
        k = k_ref[0, h:h + 1, :]
        v = v_ref[0, h:h + 1, :].astype(F32)
        ig = g_ref[0, h:h + 1, 0:1]
        lf = g_ref[0, h:h + 1, 1:2]
        m_prev = m_ref[0, h:h + 1, :]
        c_prev = c_ref[0, h]
        n_prev = n_ref[0, h:h + 1, :]
        inter = lf + m_prev
        mt = jnp.maximum(inter, ig)
        wm = jnp.exp(ig - mt)
        a = jnp.exp(inter - mt)
        q8 = jnp.broadcast_to(q, (SUBLANES, q.shape[1]))
        cq = _dot_nt(q8, c_prev.astype(BF16))[0:1, :]
        wqk = wm * jnp.sum(q.astype(F32) * k.astype(F32), -1, keepdims=True)
        num = a * cq + wqk * v
        den = a * jnp.sum(n_prev * q.astype(F32), -1, keepdims=True) + wqk
        hh = num / jnp.maximum(jnp.abs(den), jnp.exp(-mt))
        v8 = jnp.where(row == 0, jnp.broadcast_to(v * wm, (SUBLANES, v.shape[1])), 0.0).astype(BF16)
        k8 = jnp.broadcast_to(k, (SUBLANES, k.shape[1]))
        co_ref[0, h] = a * c_prev + _dot_tn(v8, k8)
        no_ref[0, h:h + 1, :] = a * n_prev + wm * k.astype(F32)
        mo_ref[0, h:h + 1, :] = mt
        hn = hh * lax.rsqrt(jnp.mean(hh * hh, -1, keepdims=True) + RMS_EPS)
        hn_ref[0, h:h + 1, :] = (hn * gain_ref[h:h + 1, :] * og_ref[0, h:h + 1, :]).astype(hn_ref.dtype)


def mlstm_sample(q, k, v, og, gc, gain, c, n, m):
    bd, d = q.shape
    nh = MLSTM_HEADS
    dk = d // nh
    heads = lambda a: a.reshape(bd, nh, dk)
    g2 = jnp.transpose(gc[:, :2 * nh].reshape(bd, 2, nh), (0, 2, 1))
    blk3 = lambda n2: pl.BlockSpec((1, nh, n2), lambda bi: (bi, 0, 0))
    cspec = pl.BlockSpec((1, nh, dk, dk), lambda bi: (bi, 0, 0, 0))
    hn, co, no, mo = pl.pallas_call(
        functools.partial(_mlstm_sample_kernel, nh=nh), grid=(bd,),
        in_specs=[blk3(dk), blk3(dk), blk3(dk), blk3(dk), blk3(2), _full((nh, dk)), cspec, blk3(dk), blk3(1)],
        out_specs=[blk3(dk), cspec, blk3(dk), blk3(1)],
        out_shape=[jax.ShapeDtypeStruct((bd, nh, dk), BF16), jax.ShapeDtypeStruct(c.shape, F32),
                   jax.ShapeDtypeStruct(n.shape, F32), jax.ShapeDtypeStruct((bd, nh, 1), F32)],
        compiler_params=_cparams(("parallel",)), name="mlstm_sample",
    )(heads(q), heads(k), heads(v), heads(og), g2, gain.reshape(nh, dk), c, n, m.reshape(bd, nh, 1))
    return hn.reshape(bd, d), co, no, mo.reshape(bd, nh)


def _row_to_col(row):
    n = row.shape[1]
    eye = lax.broadcasted_iota(jnp.int32, (n, n), 0) == lax.broadcasted_iota(jnp.int32, (n, n), 1)
    return jnp.sum(jnp.where(eye, row, 0.0), axis=1, keepdims=True)


def _head_pad(q, keep):
    q2 = jnp.concatenate([q] * NSA_KV_HEADS, axis=1)
    row = lax.broadcasted_iota(jnp.int32, q2.shape, 0)
    lane = lax.broadcasted_iota(jnp.int32, q2.shape, 1)
    return jnp.where((lane // NSA_HD == row // NSA_GROUP) & keep(row), q2, jnp.zeros_like(q2))


def _nsa_sample_cmp_kernel(pt_ref, q_ref, pages_ref, pe_ref, w_ref, biasc_ref, oc_ref, idx_ref, xbuf, xrow, sem,
                           *, n_pages, nc, ncp, nsb, n_sel, past):
    b = pl.program_id(0)
    nb = pl.num_programs(0)
    kvd = NSA_KV_HEADS * NSA_HD

    def page_copy(bb, slot, p, sl):
        return pltpu.make_async_copy(pages_ref.at[pt_ref[bb * n_pages + p], pl.ds(sl * kvd, kvd), :],
                                     xbuf.at[slot, sl, p], sem.at[slot])

    def start_all(bb, slot):
        def body(p, c):
            page_copy(bb, slot, p, 0).start()
            page_copy(bb, slot, p, 1).start()
            return c
        lax.fori_loop(0, n_pages, body, 0)

    @pl.when(b == 0)
    def _():
        start_all(0, 0)

    @pl.when(b + 1 < nb)
    def _():
        start_all(b + 1, (b + 1) % 2)

    slot = b % 2

    def wait_body(p, c):
        page_copy(b, slot, p, 0).wait()
        page_copy(b, slot, p, 1).wait()
        return c
    lax.fori_loop(0, n_pages, wait_body, 0)

    def file_page(p, c):
        for sl in range(2):
            _file_rows(xbuf[slot, sl, p], xrow, sl, p)
        return c
    lax.fori_loop(0, n_pages, file_page, 0, unroll=8)

    acc = _compress_filed(xrow, pe_ref, w_ref, nc)
    kc = acc[:, 0:kvd].astype(BF16)
    vc = acc[:, kvd:2 * kvd].astype(BF16)
    q = q_ref[0]
    nh = q.shape[0]
    qpad = _head_pad(q, lambda r: r >= 0)
    s = _dot_nt(qpad, kc)
    s = s + biasc_ref[:, 0:nc]
    e = jnp.exp(s - jnp.max(s, -1, keepdims=True))
    p_c = e / jnp.sum(e, -1, keepdims=True)
    o = _dot(p_c.astype(BF16), vc)
    row = lax.broadcasted_iota(jnp.int32, (nh, NSA_HD), 0)
    o_h = o[:, 0:NSA_HD]
    for k in range(1, NSA_KV_HEADS):
        o_h = jnp.where(row // NSA_GROUP == k, o[:, k * NSA_HD:(k + 1) * NSA_HD], o_h)
    oc_ref[0] = o_h
    prow = lax.broadcasted_iota(jnp.int32, p_c.shape, 0)
    lane = lax.broadcasted_iota(jnp.int32, (1, ncp), 1)
    blk = lane // 2
    cur = past // SEL_BLOCK
    forced = (blk == 0) | (blk == cur) | (blk == cur - 1)
    is_cand = ((lane % 2) == 0) & (lane < 2 * nsb)
    nselp = idx_ref.shape[1]
    rsel = lax.broadcasted_iota(jnp.int32, (nselp, 1), 0).astype(F32)
    out_lane = lax.broadcasted_iota(jnp.int32, (nselp, LANES), 1)
    result = jnp.full((nselp, LANES), -1, jnp.int32)
    for k in range(NSA_KV_HEADS):
        imp = jnp.sum(jnp.where(prow // NSA_GROUP == k, p_c, 0.0), axis=0, keepdims=True)
        imp = jnp.concatenate([imp, jnp.zeros((1, ncp - nc), F32)], axis=1)
        imp = _pair_sum(imp)
        score = jnp.where(forced, FORCE_SCORE, jnp.where(blk <= cur, imp, -1.0))
        score = jnp.where(is_cand, score, -2.0)
        sel, rank = _select_blocks(score, n_sel, nsb)
        hit = (rank == rsel) & (sel > 0.5)
        idx = jnp.sum(jnp.where(hit, (blk + 1).astype(F32), 0.0), axis=1, keepdims=True) - 1.0
        result = jnp.where(out_lane == k, idx.astype(jnp.int32), result)
    idx_ref[0] = result


def nsa_sample_cmp(q8, pages, page_table, pe2, wbd, rel_bias):
    bd, nh, hd = q8.shape
    n_pages = page_table.shape[1]
    past = n_pages * PAGE_SIZE
    nc = past // CMP_BLOCK
    nsb = -(-(past + 1) // SEL_BLOCK)
    n_sel = min(TOP_N, nsb)
    ncp = _round_up(max(nc, SEL_RATIO * nsb), LANES)
    nselp = _round_up(n_sel, SUBLANES)
    kvd = wbd.shape[-1]
    cend = jnp.arange(nc) * CMP_BLOCK + (CMP_BLOCK - 1)
    biasc = jnp.zeros((nh, ncp), F32).at[:, :nc].set(_bias_of(rel_bias, past - cend).T)
    grid_spec = pltpu.PrefetchScalarGridSpec(
        num_scalar_prefetch=1, grid=(bd,),
        in_specs=[pl.BlockSpec((1, nh, hd), lambda bi, pt: (bi, 0, 0)),
                  pl.BlockSpec(memory_space=pl.ANY),
                  pl.BlockSpec(pe2.shape, lambda bi, pt: (0, 0, 0)),
                  pl.BlockSpec(wbd.shape, lambda bi, pt: (0, 0, 0, 0)),
                  pl.BlockSpec((nh, ncp), lambda bi, pt: (0, 0))],
        out_specs=[pl.BlockSpec((1, nh, hd), lambda bi, pt: (bi, 0, 0)),
                   pl.BlockSpec((1, nselp, LANES), lambda bi, pt: (bi, 0, 0))],
        scratch_shapes=[pltpu.VMEM((2, 2, n_pages, kvd, PAGE_SIZE), F32),
                        pltpu.VMEM((2, CMP_BLOCK // SUBLANES, nc * SUBLANES, kvd), F32),
                        pltpu.SemaphoreType.DMA((2,))])
    oc, idx = pl.pallas_call(
        functools.partial(_nsa_sample_cmp_kernel, n_pages=n_pages, nc=nc, ncp=ncp, nsb=nsb, n_sel=n_sel, past=past),
        grid_spec=grid_spec,
        out_shape=[jax.ShapeDtypeStruct((bd, nh, hd), F32), jax.ShapeDtypeStruct((bd, nselp, LANES), jnp.int32)],
        compiler_params=_cparams(("arbitrary",), VMEM_LIMIT), name="nsa_sample_cmp",
    )(page_table.reshape(-1), q8, pages, pe2, wbd, biasc)
    sel_idx = jnp.transpose(idx[:, :n_sel, :NSA_KV_HEADS], (0, 2, 1))
    return oc, sel_idx


def _nsa_sample_att_kernel(pt_ref, si_ref, q_ref, g_ref, oc_ref, kvn_ref, wn_ref, wc_ref, pages_ref,
                           bsel_ref, bwin_ref, ob_ref, win_ref, selbuf, wall, sem,
                           *, n_pages, n_sel, past, wb):
    b = pl.program_id(0)
    nb = pl.num_programs(0)
    kvd = NSA_KV_HEADS * NSA_HD
    hd = NSA_HD
    n_blk_pages = past // SEL_BLOCK
    per_page = PAGE_SIZE // SEL_BLOCK
    n_slots = NSA_KV_HEADS * n_sel

    def blk_of(bb, j):
        return si_ref[bb * n_slots + j]

    def blk_copy(bb, slot, j):
        blk = jnp.clip(blk_of(bb, j), 0, n_blk_pages - 1)
        page = pt_ref[bb * n_pages + blk // per_page]
        return pltpu.make_async_copy(pages_ref.at[page, pl.ds(2 * kvd, 2 * kvd), :], selbuf.at[slot, j], sem.at[slot])

    def in_pages(bb, j):
        blk = blk_of(bb, j)
        return (blk >= 0) & (blk < n_blk_pages)

    def start_all(bb, slot):
        def body(j, c):
            @pl.when(in_pages(bb, j))
            def _():
                blk_copy(bb, slot, j).start()
            return c
        lax.fori_loop(0, n_slots, body, 0)

    @pl.when(b == 0)
    def _():
        start_all(0, 0)

    @pl.when(b + 1 < nb)
    def _():
        start_all(b + 1, (b + 1) % 2)

    slot = b % 2
    new_sel = _row_to_col(kvn_ref[0][:, 2 * kvd:4 * kvd])
    lane = lax.broadcasted_iota(jnp.int32, (1, PAGE_SIZE), 1)

    def wait_body(j, c):
        @pl.when(in_pages(b, j))
        def _():
            blk_copy(b, slot, j).wait()

        @pl.when(jnp.logical_not(in_pages(b, j)))
        def _():
            is_new = blk_of(b, j) == n_blk_pages
            selbuf[slot, j] = jnp.where((lane == 0) & is_new, new_sel, 0.0)
        return c
    lax.fori_loop(0, n_slots, wait_body, 0)

    q = q_ref[0]
    nh = q.shape[0]
    gates = g_ref[0]
    head = lax.broadcasted_iota(jnp.int32, (nh, 1), 0)

    def attend(qp, keys_t, vals_t, bias, valid):
        s = jnp.where(valid, _dot(qp, keys_t) + bias, NEG)
        e = jnp.where(valid, jnp.exp(s - jnp.max(s, -1, keepdims=True)), 0.0)
        den = jnp.sum(e, -1, keepdims=True)
        p = e / jnp.where(den > 0, den, 1.0)
        return _dot_nt(p.astype(BF16), vals_t)

    o_s = jnp.zeros((nh, kvd), F32)
    for k in range(NSA_KV_HEADS):
        blks = [blk_of(b, k * n_sel + r) for r in range(n_sel)]
        tiles = [selbuf[slot, k * n_sel + r] for r in range(n_sel)]
        keys_t = jnp.concatenate([t_[0:kvd, :] for t_ in tiles], axis=1).astype(BF16)
        vals_t = jnp.concatenate([t_[kvd:2 * kvd, :] for t_ in tiles], axis=1).astype(BF16)
        bias = jnp.concatenate([bsel_ref[jnp.clip(bl // per_page, 0, n_pages)] for bl in blks], axis=1)
        valid = jnp.concatenate(
            [(lane // SEL_BLOCK == bl % per_page) & ((bl // per_page) * PAGE_SIZE + lane <= past) & (bl >= 0)
             for bl in blks], axis=1)
        o_k = attend(_head_pad(q, lambda r: r // NSA_GROUP == k), keys_t, vals_t, bias, valid)
        o_s = jnp.where(head // NSA_GROUP == k, o_k, o_s)
    wlanes = wall.shape[1]
    wall[:, 0:wb] = wc_ref[0]
    tail = lax.broadcasted_iota(jnp.int32, (1, wlanes - wb), 1)
    wall[:, wb:wlanes] = jnp.where(tail == 0, _row_to_col(wn_ref[0]), 0.0)
    win_ref[0] = pltpu.roll(wall[...], wlanes - 1, 1)[:, 0:wb]
    w_pos = lax.broadcasted_iota(jnp.int32, (1, wlanes), 1)
    valid_w = (w_pos <= wb) & (wb - w_pos < WINDOW) & (past - wb + w_pos >= 0)
    o_w = attend(_head_pad(q, lambda r: r >= 0), wall[0:kvd, :].astype(BF16), wall[kvd:2 * kvd, :].astype(BF16),
                 bwin_ref[...], valid_w)
    o_c = jnp.concatenate([oc_ref[0]] * NSA_KV_HEADS, axis=1)
    mix = gates[:, 0:1] * o_c + gates[:, 1:2] * o_s + gates[:, 2:3] * o_w
    out = mix[:, 0:hd]
    for k in range(1, NSA_KV_HEADS):
        out = jnp.where(head // NSA_GROUP == k, mix[:, k * hd:(k + 1) * hd], out)
    ob_ref[0] = out.astype(ob_ref.dtype)


def nsa_sample_att(q8, gates, oc, kv03, kv45, wcache_t, layer, pages_t, page_table, sel_idx, rel_bias):
    bd, nh, hd = q8.shape
    n_pages = page_table.shape[1]
    past = n_pages * PAGE_SIZE
    wb = wcache_t.shape[2]
    kvd = NSA_KV_HEADS * hd
    n_sel = sel_idx.shape[2]
    wlanes = _round_up(wb + 1, LANES)
    g3 = gates[:, :3 * nh].reshape(bd, nh, 3)
    dist = past - (jnp.arange(n_pages + 1)[:, None] * PAGE_SIZE + jnp.arange(PAGE_SIZE)[None, :])
    bsel = jnp.transpose(_bias_of(rel_bias, dist), (0, 2, 1))
    bwin = _bias_of(rel_bias, wb - jnp.arange(wlanes)).T
    blk = lambda n2, n3: pl.BlockSpec((1, n2, n3), lambda bi, pt, si: (bi, 0, 0))
    grid_spec = pltpu.PrefetchScalarGridSpec(
        num_scalar_prefetch=2, grid=(bd,),
        in_specs=[blk(nh, hd), blk(nh, 3), blk(nh, hd), blk(1, 4 * kvd), blk(1, 2 * kvd),
                  pl.BlockSpec((1, 2 * kvd, wb), lambda bi, pt, si: (layer * bd + bi, 0, 0)),
                  pl.BlockSpec(memory_space=pl.ANY),
                  pl.BlockSpec(bsel.shape, lambda bi, pt, si: (0, 0, 0)),
                  pl.BlockSpec(bwin.shape, lambda bi, pt, si: (0, 0))],
        out_specs=[blk(nh, hd), blk(2 * kvd, wb)],
        scratch_shapes=[pltpu.VMEM((2, NSA_KV_HEADS * n_sel, 2 * kvd, PAGE_SIZE), F32),
                        pltpu.VMEM((2 * kvd, wlanes), F32), pltpu.SemaphoreType.DMA((2,))])
    ob, win = pl.pallas_call(
        functools.partial(_nsa_sample_att_kernel, n_pages=n_pages, n_sel=n_sel, past=past, wb=wb),
        grid_spec=grid_spec,
        out_shape=[jax.ShapeDtypeStruct((bd, nh, hd), BF16), jax.ShapeDtypeStruct((bd, 2 * kvd, wb), F32)],
        compiler_params=_cparams(("arbitrary",), VMEM_LIMIT), name="nsa_sample_att",
    )(page_table.reshape(-1), sel_idx.reshape(-1), q8, g3, oc, kv03.reshape(bd, 1, -1), kv45.reshape(bd, 1, -1),
      wcache_t, pages_t, bsel, bwin)
    return ob.reshape(bd, nh * hd), win


def kernel(x_prompt, x_sample, mem_prompt, cache_conv, cache_nsa_pages, cache_nsa_window, state_mlstm_c,
           state_mlstm_n, state_mlstm_m, cache_mem_kv, page_table, rel_bias, norm_mix, norm_xattn, norm_mem,
           norm_ffn, norm_final, w_in_even, w_out_even, conv_w, conv_b, conv_ln_g, conv_ln_b, nsa_cmp_pe,
           nsa_cmp_w, w_in_odd, mlstm_b_i, mlstm_b_f, mlstm_norm, w_out_odd, xattn_wq, xattn_wkv, xattn_wo,
           ffn_w_gu, ffn_w_dn, router_w, router_b, expert_w_gu, expert_w_dn):
    b, s, d = x_prompt.shape
    bd, td, _ = x_sample.shape
    assert td == 1, "the sample group decodes one token per sequence"
    depth = norm_mix.shape[0]
    mt = mem_prompt.shape[1]
    cc = conv_w.shape[2]
    hist = conv_w.shape[1] - 1
    wb = cache_nsa_window.shape[2]
    kvh, hd = NSA_KV_HEADS, NSA_HD
    n_pool = cache_nsa_pages.shape[1]
    assert s >= hist and s >= wb and s % Q_BLOCK == 0
    xp = x_prompt.reshape(b * s, d)
    xs = x_sample.reshape(bd, d)
    mem = mem_prompt.reshape(b * mt, d)
    pages_t = jnp.swapaxes(cache_nsa_pages.reshape(-1, PAGE_SIZE, 4 * kvh * hd), 1, 2)
    window_t = jnp.swapaxes(cache_nsa_window.reshape(-1, wb, 2 * kvh * hd), 1, 2)
    xhd = d // X_HEADS
    memkv_rows = jnp.swapaxes(cache_mem_kv.reshape(depth * bd, mt, 2, X_HEADS, xhd // LANES, LANES), 3, 4)
    memkv_rows = memkv_rows.reshape(depth * bd, -1, LANES)
    bf = lambda a: a.astype(BF16)
    conv_p, conv_s, nsa_p, nsa_s, win_p, win_s = [], [], [], [], [], []
    mc_p, mc_s, mn_p, mn_s, mm_p, mm_s, memkv_p = [], [], [], [], [], [], []
    for l in range(depth):
        li = l // 2
        if l % 2 == 0:
            prm = prep_even(w_in_even[li], nsa_cmp_pe[li], nsa_cmp_w[li])
            w_out = bf(w_out_even[li])
            w_parts = [w_out[:cc], w_out[cc:]]
            conv_args = (conv_w[li], conv_b[li], conv_ln_g[li], conv_ln_b[li])
            glu, keys, qt, vt, gt, kv_t = inproj_even(xp, norm_mix[l], prm['w_in'], cc, s)
            a_out = conv_prompt(glu, *conv_args, b, s)
            kc, vct = compress_prompt(kv_t, prm['pe2'], prm['wbd'], _round_up(s // CMP_BLOCK, LANES))
            b_out = nsa_prompt(qt, gt, kc, vct, keys, vt, rel_bias, b, s)
            mix_p = ([a_out, b_out], w_parts)
            conv_p.append(glu.reshape(b, s, cc)[:, s - hist:])
            rows_t = kv_t.reshape(b, 6, kvh, hd, s)
            nsa_p.append(jnp.transpose(rows_t[:, :4], (0, 4, 1, 2, 3)))
            win_p.append(jnp.transpose(rows_t[:, 4:, :, :, s - wb:], (0, 4, 1, 2, 3)))
            glu, kv03, kv45, q, gates = inproj_even(xs, norm_mix[l], prm['w_in'], cc)
            a_out, conv_state = conv_sample(cache_conv[li], glu, *conv_args)
            q8 = q.reshape(bd, NSA_HEADS, hd)
            pt = page_table + li * n_pool
            o_c, sel_idx = nsa_sample_cmp(q8, pages_t, pt, prm['pe2'], prm['wbd'], rel_bias)
            b_out, win = nsa_sample_att(q8, gates, o_c, kv03, kv45, window_t, li, pages_t, pt, sel_idx, rel_bias)
            xs = outproj(xs, [a_out, b_out], w_parts)
            conv_s.append(conv_state)
            nsa_s.append(kv03.reshape(bd, 1, 4, kvh, hd))
            win_s.append(jnp.transpose(win.reshape(bd, 2, kvh, hd, wb), (0, 4, 1, 2, 3)))
        else:
            prm = prep_odd(w_in_odd[li], mlstm_b_i[li], mlstm_b_f[li])
            w_out = bf(w_out_odd[li])
            q, k, v, og, gc, gr = inproj_odd(xp, norm_mix[l], prm['w_in'], prm['bias'])
            hn, c_new, n_new, m_new = mlstm_prompt(q, k, v, og, gc, gr, mlstm_norm[li], b, s)
            mix_p = ([hn], [w_out])
            mc_p.append(c_new)
            mn_p.append(n_new)
            mm_p.append(m_new[:, :, 0])
            q, k, v, og, gc, gr = inproj_odd(xs, norm_mix[l], prm['w_in'], prm['bias'])
            hn, c_new, n_new, m_new = mlstm_sample(q, k, v, og, gc, mlstm_norm[li], state_mlstm_c[li],
                                                   state_mlstm_n[li], state_mlstm_m[li])
            xs = outproj(xs, [hn], [w_out])
            mc_s.append(c_new)
            mn_s.append(n_new)
            mm_s.append(m_new)
        wq, wo = bf(xattn_wq[l]), bf(xattn_wo[l])
        mkv_rows, mkv_b = memkv(mem, norm_mem[l], bf(xattn_wkv[l]), X_HEADS)
        mkv = jnp.swapaxes(mkv_rows.reshape(b, mt, 2, xhd // LANES, X_HEADS, LANES), 3, 4)
        memkv_p.append(mkv.reshape(b, mt, 2, X_HEADS, xhd))
        xp = xattn_prompt(xp, norm_xattn[l], wq, mkv_b, wo, b, s, *mix_p)
        xs = xattn_sample(xs, norm_xattn[l], wq, memkv_rows, wo, l, mt)
        if l % 2 == 0:
            w_gu, w_dn = bf(ffn_w_gu[li]), bf(ffn_w_dn[li])
            xp = ffn(xp, norm_ffn[l], w_gu, w_dn)
            xs = ffn(xs, norm_ffn[l], w_gu, w_dn)
        else:
            e_gu, e_dn = bf(expert_w_gu[li]), bf(expert_w_dn[li])
            final_g = norm_final if l == depth - 1 else None
            comb, h, mask, counts = router(xp, norm_ffn[l], router_w[li], router_b[li])
            xp = moe_grouped(xp, h, comb, mask, counts, e_gu, e_dn, final_g)
            comb, h, _, _ = router(xs, norm_ffn[l], router_w[li], router_b[li])
            xs = moe(xs, h, comb, e_gu, e_dn, final_g)
    if depth % 2:
        xp, xs = rmsnorm(xp, norm_final), rmsnorm(xs, norm_final)
    y_prompt = xp.reshape(b, s, d)
    y_sample = xs.reshape(bd, 1, d)
    return (y_prompt, y_sample, jnp.stack(conv_p), jnp.stack(conv_s), jnp.stack(nsa_p), jnp.stack(nsa_s),
            jnp.stack(win_p), jnp.stack(win_s), jnp.stack(mc_p), jnp.stack(mc_s), jnp.stack(mn_p),
            jnp.stack(mn_s), jnp.stack(mm_p), jnp.stack(mm_s), jnp.stack(memkv_p))
```

```python
import functools
import math

import jax
import jax.numpy as jnp
import numpy as np
from jax import lax
from jax.experimental import pallas as pl
from jax.experimental.pallas import tpu as pltpu

F32 = jnp.float32
BF16 = jnp.bfloat16

PAGE_SIZE = 128
CONV_WIDTH = 31
NSA_HEADS = 8
NSA_KV_HEADS = 2
NSA_GROUP = NSA_HEADS // NSA_KV_HEADS
NSA_HD = 64
CMP_BLOCK = 32
SEL_BLOCK = 64
SEL_RATIO = SEL_BLOCK // CMP_BLOCK
TOP_N = 16
WINDOW = 512
Q_BLOCK = 128
FORCE_SCORE = 1.0e4
NUM_BUCKETS = 32
MAX_DISTANCE = 1024
MLSTM_HEADS = 4
X_HEADS = 4
N_EXPERTS = 8
TOP_K = 2
RMS_EPS = 1e-6
LN_EPS = 1e-5
NEG = -1e30

LANES = 128
SUBLANES = 8
VMEM_LIMIT = 56 * 1024 * 1024
MLSTM_CHUNK = 256


def _cparams(sem, vmem=None):
    return pltpu.CompilerParams(dimension_semantics=sem, vmem_limit_bytes=vmem)


def _rms(x, g):
    return x * lax.rsqrt(jnp.mean(x * x, -1, keepdims=True) + RMS_EPS) * g


def _dot(a, b):
    return jnp.dot(a, b, preferred_element_type=F32)


def _dot_nt(a, b):
    return lax.dot_general(a, b, (((1,), (1,)), ((), ())), preferred_element_type=F32)


def _dot_tn(a, b):
    return lax.dot_general(a, b, (((0,), (0,)), ((), ())), preferred_element_type=F32)


def _full(shape):
    n = len(shape)
    return pl.BlockSpec(shape, lambda *_: (0,) * n)


def _row_tile(m, pref):
    t = min(pref, m)
    while m % t:
        t //= 2
    return t


def _rmsnorm_kernel(x_ref, g_ref, o_ref):
    o_ref[...] = _rms(x_ref[...], g_ref[...])


def rmsnorm(x, g):
    m, d = x.shape
    tm = _row_tile(m, 1024)
    return pl.pallas_call(
        _rmsnorm_kernel, grid=(m // tm,),
        in_specs=[pl.BlockSpec((tm, d), lambda i: (i, 0)), _full((1, d))],
        out_specs=pl.BlockSpec((tm, d), lambda i: (i, 0)),
        out_shape=jax.ShapeDtypeStruct((m, d), F32),
        compiler_params=_cparams(("parallel",)), name="rmsnorm",
    )(x, g.reshape(1, d))


def _outproj_kernel(*refs, n_in):
    x_ref = refs[0]
    a_refs = refs[1:1 + n_in]
    w_refs = refs[1 + n_in:1 + 2 * n_in]
    o_ref = refs[1 + 2 * n_in]
    acc = x_ref[...]
    for a_ref, w_ref in zip(a_refs, w_refs):
        acc = acc + _dot(a_ref[...], w_ref[...])
    o_ref[...] = acc


def outproj(x, acts, ws):
    m, d = x.shape
    tm = _row_tile(m, 512)
    n_in = len(acts)
    in_specs = [pl.BlockSpec((tm, d), lambda i: (i, 0))]
    in_specs += [pl.BlockSpec((tm, a.shape[1]), lambda i: (i, 0)) for a in acts]
    in_specs += [_full(w.shape) for w in ws]
    return pl.pallas_call(
        functools.partial(_outproj_kernel, n_in=n_in), grid=(m // tm,),
        in_specs=in_specs, out_specs=pl.BlockSpec((tm, d), lambda i: (i, 0)),
        out_shape=jax.ShapeDtypeStruct((m, d), F32),
        compiler_params=_cparams(("parallel",)), name="outproj",
    )(x, *acts, *ws)


def _inproj_even_kernel(x_ref, g_ref, w_ref, glu_ref, *rest, cc, qd, kvd, tiles):
    xn = _rms(x_ref[...], g_ref[...]).astype(BF16)

    def mm(lo, hi):
        return _dot(xn, w_ref[:, lo:hi])

    o = 0
    a = mm(o, o + cc)
    b = mm(o + cc, o + 2 * cc)
    glu_ref[...] = a * jax.nn.sigmoid(b)
    o += 2 * cc
    q = mm(o, o + qd) * (NSA_HD ** -0.5)
    o += qd
    kv03 = mm(o, o + 4 * kvd)
    o += 4 * kvd
    kv45 = mm(o, o + 2 * kvd)
    o += 2 * kvd
    gates = jax.nn.sigmoid(mm(o, o + LANES))
    if tiles == 0:
        kv03_ref, kv45_ref, q_ref, gate_ref = rest
        kv03_ref[...] = kv03
        kv45_ref[...] = kv45
        q_ref[...] = q.astype(BF16)
        gate_ref[...] = gates
        return
    keys_ref, qt_ref, vt_ref, gt_ref, kvt_ref = rest
    kvt_ref[0] = jnp.concatenate([kv03, kv45], axis=1).T
    keys_ref[...] = jnp.concatenate([kv03[:, 2 * kvd:3 * kvd], kv45[:, 0:kvd]], axis=1).astype(BF16)
    vals = jnp.concatenate([kv03[:, 3 * kvd:4 * kvd], kv45[:, kvd:2 * kvd]], axis=1)
    for j in range(tiles):
        rows = slice(j * Q_BLOCK, (j + 1) * Q_BLOCK)
        qt_ref[j] = q[rows, :].T.astype(BF16)
        vt_ref[j] = vals[rows, :].T.astype(BF16)
        gt_ref[j] = gates[rows, :].T


def inproj_even(x, g, w_pad, cc, seq=None):
    m, d = x.shape
    qd = NSA_HEADS * NSA_HD
    kvd = NSA_KV_HEADS * NSA_HD
    tm = _row_tile(m, 256)
    row = lambda n: pl.BlockSpec((tm, n), lambda i: (i, 0))
    out_specs = [row(cc)]
    out_shape = [jax.ShapeDtypeStruct((m, cc), F32)]
    transposed = seq is not None
    tiles = tm // Q_BLOCK if transposed else 0
    if transposed:
        assert tm % Q_BLOCK == 0 and seq % tm == 0
        per_seq = seq // tm
        tile = lambda n: pl.BlockSpec((tiles, n, Q_BLOCK), lambda i: (i, 0, 0))
        out_specs += [row(2 * kvd), tile(qd), tile(2 * kvd), tile(LANES),
                      pl.BlockSpec((1, 6 * kvd, tm), lambda i: (i // per_seq, 0, i % per_seq))]
        out_shape += [jax.ShapeDtypeStruct((m, 2 * kvd), BF16),
                      jax.ShapeDtypeStruct((m // Q_BLOCK, qd, Q_BLOCK), BF16),
                      jax.ShapeDtypeStruct((m // Q_BLOCK, 2 * kvd, Q_BLOCK), BF16),
                      jax.ShapeDtypeStruct((m // Q_BLOCK, LANES, Q_BLOCK), F32),
                      jax.ShapeDtypeStruct((m // seq, 6 * kvd, seq), F32)]
    else:
        out_specs += [row(4 * kvd), row(2 * kvd), row(qd), row(LANES)]
        out_shape += [jax.ShapeDtypeStruct((m, 4 * kvd), F32), jax.ShapeDtypeStruct((m, 2 * kvd), F32),
                      jax.ShapeDtypeStruct((m, qd), BF16), jax.ShapeDtypeStruct((m, LANES), F32)]
    return pl.pallas_call(
        functools.partial(_inproj_even_kernel, cc=cc, qd=qd, kvd=kvd, tiles=tiles), grid=(m // tm,),
        in_specs=[row(d), _full((1, d)), _full(w_pad.shape)],
        out_specs=out_specs, out_shape=out_shape,
        compiler_params=_cparams(("parallel",)), name="inproj_even",
    )(x, g.reshape(1, d), w_pad)


def _conv_post(y, lg, lb):
    mu = jnp.mean(y, -1, keepdims=True)
    var = jnp.mean(jnp.square(y - mu), -1, keepdims=True)
    yn = (y - mu) * lax.rsqrt(var + LN_EPS) * lg + lb
    return yn * jax.nn.sigmoid(yn)


CONV_SUB = 64
CONV_PAD = 32


def _conv_prompt_kernel(glu_ref, cw_ref, cb_ref, lg_ref, lb_ref, o_ref, ext_ref, y_ref, *, ts, s):
    i = pl.program_id(1)
    c = glu_ref.shape[-1]

    @pl.when(i == 0)
    def _():
        ext_ref[0:CONV_PAD, :] = jnp.zeros((CONV_PAD, c), F32)
        ext_ref[CONV_PAD:CONV_PAD + s, :] = glu_ref[0]
        ext_ref[CONV_PAD + s:CONV_PAD + s + SUBLANES, :] = jnp.zeros((SUBLANES, c), F32)

    lead = CONV_PAD - (CONV_WIDTH - 1)
    span = CONV_SUB + CONV_PAD

    def sub(j, carry):
        r0 = pl.multiple_of(i * ts + j * CONV_SUB, CONV_SUB)
        for c0 in range(0, c, LANES):
            xw = ext_ref[pl.ds(r0, span + SUBLANES), c0:c0 + LANES]
            acc = jnp.zeros((CONV_SUB, LANES), F32) + cb_ref[:, c0:c0 + LANES]
            for r in range(SUBLANES):
                xr = xw if r == 0 else pltpu.roll(xw, span + SUBLANES - r, 0)
                for a in range(span // SUBLANES):
                    w = SUBLANES * a + r - lead
                    if 0 <= w < CONV_WIDTH:
                        acc = acc + xr[SUBLANES * a:SUBLANES * a + CONV_SUB, :] * cw_ref[w:w + 1, c0:c0 + LANES]
            y_ref[:, c0:c0 + LANES] = acc
        o_ref[0, pl.ds(pl.multiple_of(j * CONV_SUB, CONV_SUB), CONV_SUB), :] = _conv_post(
            y_ref[...], lg_ref[...], lb_ref[...]).astype(o_ref.dtype)
        return carry

    lax.fori_loop(0, ts // CONV_SUB, sub, 0)


def conv_prompt(glu, cw, cb, lg, lb, b, s):
    c = glu.shape[-1]
    ts = _row_tile(s, 256)
    vec = lambda a: a.reshape(1, c)
    out = pl.pallas_call(
        functools.partial(_conv_prompt_kernel, ts=ts, s=s), grid=(b, s // ts),
        in_specs=[pl.BlockSpec((1, s, c), lambda bi, i: (bi, 0, 0)), _full((CONV_WIDTH, c)),
                  _full((1, c)), _full((1, c)), _full((1, c))],
        out_specs=pl.BlockSpec((1, ts, c), lambda bi, i: (bi, i, 0)),
        out_shape=jax.ShapeDtypeStruct((b, s, c), BF16),
        scratch_shapes=[pltpu.VMEM((CONV_PAD + s + SUBLANES, c), F32), pltpu.VMEM((CONV_SUB, c), F32)],
        compiler_params=_cparams(("parallel", "arbitrary")), name="conv_prompt",
    )(glu.reshape(b, s, c), cw, vec(cb), vec(lg), vec(lb))
    return out.reshape(b * s, c)


def _conv_sample_kernel(cache_ref, glu_ref, cw_ref, cb_ref, lg_ref, lb_ref, o_ref, st_ref):
    hist = CONV_WIDTH - 1
    cache = cache_ref[...]
    glu = glu_ref[...]
    y = jnp.sum(cache * cw_ref[0:hist, :][None], axis=1) + glu * cw_ref[hist:hist + 1, :] + cb_ref[...]
    o_ref[...] = _conv_post(y, lg_ref[...], lb_ref[...]).astype(o_ref.dtype)
    st_ref[:, 0:hist - 1, :] = cache[:, 1:hist, :]
    st_ref[:, hist - 1:hist, :] = glu[:, None, :]


def conv_sample(cache, glu, cw, cb, lg, lb):
    bd, hist, c = cache.shape
    vec = lambda a: a.reshape(1, c)
    return pl.pallas_call(
        _conv_sample_kernel,
        out_shape=[jax.ShapeDtypeStruct((bd, c), BF16), jax.ShapeDtypeStruct((bd, hist, c), F32)],
        name="conv_sample",
    )(cache, glu, cw, vec(cb), vec(lg), vec(lb))


def _rel_bucket(dist):
    n = jnp.maximum(dist, 0)
    max_exact = NUM_BUCKETS // 2
    nf = jnp.maximum(n, 1).astype(F32)
    large = max_exact + (jnp.log(nf / max_exact) / math.log(MAX_DISTANCE / max_exact)
                         * (NUM_BUCKETS - max_exact)).astype(jnp.int32)
    large = jnp.minimum(large, NUM_BUCKETS - 1)
    return jnp.where(n < max_exact, n, large)


def _bias_of(rel_bias, dist):
    bucket = _rel_bucket(dist)[..., None]
    out = jnp.zeros(bucket.shape[:-1] + (rel_bias.shape[1],), F32)
    for k in range(NUM_BUCKETS):
        out = jnp.where(bucket == k, rel_bias[k].astype(F32), out)
    return out


def _compress_accumulate(load_rows, pe_ref, w_ref, nc):
    accs = []
    half = CMP_BLOCK // 2
    for slot in range(2):
        acc = jnp.zeros((nc, w_ref.shape[-1]), F32)
        for j in range(half):
            pair = [load_rows(slot, jj) + pe_ref[slot, jj:jj + 1, :] for jj in (j, j + half)]
            acc = acc + _dot(jnp.concatenate(pair, axis=1).astype(BF16), w_ref[slot, j])
        accs.append(acc)
    return jnp.concatenate(accs, axis=1)


def _file_rows(tile_t, xrow, sl, page):
    rows = tile_t.T
    per_page = PAGE_SIZE // CMP_BLOCK
    for cl in range(per_page):
        for a in range(CMP_BLOCK // SUBLANES):
            r0 = cl * CMP_BLOCK + a * SUBLANES
            dst = pl.multiple_of((page * per_page + cl) * SUBLANES, SUBLANES)
            xrow[sl, a, pl.ds(dst, SUBLANES), :] = rows[r0:r0 + SUBLANES, :]


def _compress_filed(xrow, pe_ref, w_ref, nc):
    return _compress_accumulate(
        lambda sl, j: xrow[sl, j // SUBLANES, pl.ds(j % SUBLANES, nc, stride=SUBLANES), :], pe_ref, w_ref, nc)


def _compress_prompt_kernel(x_ref, pe_ref, w_ref, kc_ref, vct_ref, xrow, *, nc, ncp, kvd):
    for p in range(x_ref.shape[2] // PAGE_SIZE):
        for sl in range(2):
            _file_rows(x_ref[0, sl * kvd:(sl + 1) * kvd, p * PAGE_SIZE:(p + 1) * PAGE_SIZE], xrow, sl, p)
    acc = _compress_filed(xrow, pe_ref, w_ref, nc)
    if ncp > nc:
        acc = jnp.concatenate([acc, jnp.zeros((ncp - nc, 2 * kvd), F32)], axis=0)
    kc_ref[0] = acc[:, 0:kvd].astype(BF16)
    vct_ref[0] = acc[:, kvd:2 * kvd].T.astype(BF16)


def compress_prompt(kv_t, pe2, wbd, ncp):
    b, _, s = kv_t.shape
    assert s % PAGE_SIZE == 0
    nc = s // CMP_BLOCK
    kvd = wbd.shape[-1]
    return pl.pallas_call(
        functools.partial(_compress_prompt_kernel, nc=nc, ncp=ncp, kvd=kvd), grid=(b,),
        in_specs=[pl.BlockSpec((1, 2 * kvd, s), lambda bi: (bi, 0, 0)), _full(pe2.shape), _full(wbd.shape)],
        out_specs=[pl.BlockSpec((1, ncp, kvd), lambda bi: (bi, 0, 0)), pl.BlockSpec((1, kvd, ncp), lambda bi: (bi, 0, 0))],
        out_shape=[jax.ShapeDtypeStruct((b, ncp, kvd), BF16), jax.ShapeDtypeStruct((b, kvd, ncp), BF16)],
        scratch_shapes=[pltpu.VMEM((2, CMP_BLOCK // SUBLANES, nc * SUBLANES, kvd), F32)],
        compiler_params=_cparams(("parallel",)), name="compress_prompt",
    )(kv_t, pe2, wbd)


def _select_blocks(score, n_sel, n_cand):
    lane = lax.broadcasted_iota(jnp.int32, score.shape, 1)
    rank = jnp.zeros(score.shape, F32)
    for i in range(n_cand):
        col = score[:, 2 * i:2 * i + 1]
        beats = (col > score) | ((col == score) & (lane > 2 * i))
        rank = rank + beats.astype(F32)
    is_cand = ((lane % 2) == 0) & (lane < 2 * n_cand)
    return (is_cand & (rank < n_sel) & (score >= 0)).astype(F32), rank


def _pair_sum(imp):
    n = imp.shape[1]
    return imp + pltpu.roll(imp, n - 1, 1)


def _rank_rows(score, n_sel, n_cand):
    blk = lax.broadcasted_iota(jnp.int32, score.shape, 0)
    rank = jnp.zeros(score.shape, F32)
    for i in range(n_cand):
        row = score[i:i + 1, :]
        beats = (row > score) | ((row == score) & (blk > i))
        rank = rank + beats.astype(F32)
    return ((rank < n_sel) & (score >= 0)).astype(F32)


def _nsa_prompt_kernel(qt_ref, gt_ref, kc_ref, vct_ref, keys_ref, vt_ref, biasc_ref, btile_ref, o_ref,
                       qt_scr, oc_scr, acc_s, acc_w, imp_scr, sel_scr, out_scr, *, nc, nsb, n_sel):
    qi = pl.program_id(1)
    g, hd, kvh, qb = NSA_GROUP, NSA_HD, NSA_KV_HEADS, Q_BLOCK
    kvd = kvh * hd
    ncp = kc_ref.shape[1]
    nsbp = sel_scr.shape[1]
    q_pos = qi * qb + lax.broadcasted_iota(jnp.int32, (1, qb), 1)
    key_row = lax.broadcasted_iota(jnp.int32, (qb, 1), 0)
    c_row = lax.broadcasted_iota(jnp.int32, (ncp, 1), 0)
    mask_c = (q_pos >= c_row * CMP_BLOCK + (CMP_BLOCK - 1)) & (c_row < nc)
    blk = lax.broadcasted_iota(jnp.int32, (nsbp, 1), 0)
    cur = q_pos // SEL_BLOCK
    forced = (blk == 0) | (blk == cur) | (blk == cur - 1)
    zeros = jnp.zeros((hd, qb), BF16)
    for k in range(kvh):
        for gi in range(g):
            h = k * g + gi
            parts = [zeros] * kvh
            parts[k] = qt_ref[0, h * hd:(h + 1) * hd, :]
            qt_scr[k, :, gi * qb:(gi + 1) * qb] = jnp.concatenate(parts, axis=0)
        s_c = _dot(kc_ref[0], qt_scr[k])
        imp = jnp.zeros((ncp, qb), F32)
        probs = []
        for gi in range(g):
            s = jnp.where(mask_c, s_c[:, gi * qb:(gi + 1) * qb] + biasc_ref[k * g + gi], NEG)
            e = jnp.where(mask_c, jnp.exp(s - jnp.max(s, 0, keepdims=True)), 0.0)
            den = jnp.sum(e, 0, keepdims=True)
            p = e / jnp.where(den > 0, den, 1.0)
            imp = imp + p
            probs.append(p.astype(BF16))
        oc_scr[k] = _dot(vct_ref[0, k * hd:(k + 1) * hd, :], jnp.concatenate(probs, axis=1))
        imp_scr[...] = imp + pltpu.roll(imp, ncp - 1, 0)
        cand = imp_scr[pl.ds(0, nsbp, stride=SEL_RATIO), :]
        score = jnp.where(forced, FORCE_SCORE, jnp.where(blk <= cur, cand, -1.0))
        sel_scr[k] = _rank_rows(jnp.where(blk < nsb, score, -2.0), n_sel, nsb)

    per_tile = qb // SEL_BLOCK
    n_tiles = keys_ref.shape[1] // qb
    first = ([jnp.full((1, qb), NEG, F32)] * g, [jnp.zeros((1, qb), F32)] * g)

    def tile_step(tiles, carry, key_col, val_row, acc_ref, window):
        kts = [jnp.clip(kt, 0, n_tiles - 1) for kt, _ in tiles]
        starts = [pl.multiple_of(kt * qb, qb) for kt in kts]
        k_t = jnp.concatenate([keys_ref[0, pl.ds(r0, qb), key_col:key_col + kvd] for r0 in starts], axis=0)
        dist = jnp.concatenate([jnp.where(active, q_pos - (r0 + key_row), -1)
                                for r0, (_, active) in zip(starts, tiles)], axis=0)
        in_range = dist >= 0
        scores = [_dot(k_t, qt_scr[k]) for k in range(kvh)]
        new, updates = [], []
        for k in range(kvh):
            if window:
                valid = in_range & (dist < WINDOW)
            else:
                pieces = []
                for kt in kts:
                    chosen = jnp.zeros((qb, qb), F32)
                    for j in range(per_tile):
                        row = sel_scr[k, pl.ds(per_tile * kt + j, 1), :]
                        chosen = jnp.where(key_row // SEL_BLOCK == j, row, chosen)
                    pieces.append(chosen)
                valid = in_range & (jnp.concatenate(pieces, axis=0) > 0.5)
            ms, ls = carry[k]
            ms2, ls2, alphas, probs = [], [], [], []
            for gi in range(g):
                bias = jnp.concatenate([btile_ref[jnp.maximum(qi - kt, 0), k * g + gi] for kt in kts], axis=0)
                s = jnp.where(valid, scores[k][:, gi * qb:(gi + 1) * qb] + bias, NEG)
                m_new = jnp.maximum(ms[gi], jnp.max(s, 0, keepdims=True))
                alpha = jnp.exp(ms[gi] - m_new)
                p = jnp.exp(s - jnp.where(m_new == NEG, 0.0, m_new))
                ms2.append(m_new)
                ls2.append(alpha * ls[gi] + jnp.sum(p, 0, keepdims=True))
                alphas.append(alpha)
                probs.append(p.astype(BF16))
            new.append((ms2, ls2))
            updates.append((jnp.concatenate(alphas, axis=1), jnp.concatenate(probs, axis=1)))
        for k, (alpha, prob) in enumerate(updates):
            v_t = jnp.concatenate([vt_ref[kt, val_row + k * hd:val_row + (k + 1) * hd, :] for kt in kts], axis=1)
            acc_ref[k] = acc_ref[k] * alpha + _dot(v_t, prob)
        return tuple(new)

    acc_s[...] = jnp.zeros_like(acc_s)
    acc_w[...] = jnp.zeros_like(acc_w)
    sel_args = dict(key_col=0, val_row=0, acc_ref=acc_s, window=False)
    win_args = dict(key_col=kvd, val_row=kvd, acc_ref=acc_w, window=True)

    def sel_pair(i, carry):
        return tile_step([(2 * i, 2 * i <= qi), (2 * i + 1, 2 * i + 1 <= qi)], carry, **sel_args)

    stat_s = lax.fori_loop(0, qi // 2 + 1, sel_pair, (first,) * kvh)
    stat_w = (first,) * kvh
    win_tiles = [(qi - j, qi - j >= 0) for j in range(WINDOW // qb, -1, -1)]
    for j in range(0, len(win_tiles), 2):
        stat_w = tile_step(win_tiles[j:j + 2], stat_w, **win_args)
    gt = gt_ref[0]
    for k in range(kvh):
        for gi in range(g):
            h = k * g + gi
            cols = slice(gi * qb, (gi + 1) * qb)
            l_s, l_w = stat_s[k][1][gi], stat_w[k][1][gi]
            o_s = acc_s[k, :, cols] / jnp.where(l_s > 0, l_s, 1.0)
            o_w = acc_w[k, :, cols] / jnp.where(l_w > 0, l_w, 1.0)
            out_scr[h * hd:(h + 1) * hd, :] = (gt[3 * h:3 * h + 1, :] * oc_scr[k, :, cols]
                                               + gt[3 * h + 1:3 * h + 2, :] * o_s + gt[3 * h + 2:3 * h + 3, :] * o_w)
    o_ref[0] = out_scr[...].T.astype(o_ref.dtype)


def nsa_prompt(qt, gt, kc, vct, keys, vt, rel_bias, b, s):
    qb = Q_BLOCK
    nq = s // qb
    nc = s // CMP_BLOCK
    ncp = kc.shape[1]
    nsb = s // SEL_BLOCK
    nsbp = _round_up(nsb, SUBLANES)
    assert SEL_RATIO * nsbp <= ncp
    n_sel = min(TOP_N, nsb)
    nh = NSA_HEADS
    hq = qt.shape[1]
    kvd = kc.shape[2]
    glanes = NSA_GROUP * qb
    cend = jnp.arange(ncp)[:, None] * CMP_BLOCK + (CMP_BLOCK - 1)
    biasc = jnp.transpose(_bias_of(rel_bias, jnp.arange(s)[None, :] - cend), (2, 0, 1))
    r = jnp.arange(qb)
    dist = jnp.arange(nq)[:, None, None] * qb + r[None, None, :] - r[None, :, None]
    btile = jnp.transpose(_bias_of(rel_bias, dist), (0, 3, 1, 2))
    out = pl.pallas_call(
        functools.partial(_nsa_prompt_kernel, nc=nc, nsb=nsb, n_sel=n_sel), grid=(b, nq),
        in_specs=[pl.BlockSpec((1, hq, qb), lambda bi, i: (bi * nq + i, 0, 0)),
                  pl.BlockSpec((1, LANES, qb), lambda bi, i: (bi * nq + i, 0, 0)),
                  pl.BlockSpec((1, ncp, kvd), lambda bi, i: (bi, 0, 0)),
                  pl.BlockSpec((1, kvd, ncp), lambda bi, i: (bi, 0, 0)),
                  pl.BlockSpec((1, s, 2 * kvd), lambda bi, i: (bi, 0, 0)),
                  pl.BlockSpec((nq, 2 * kvd, qb), lambda bi, i: (bi, 0, 0)),
                  pl.BlockSpec((nh, ncp, qb), lambda bi, i: (0, 0, i)),
                  _full(btile.shape)],
        out_specs=pl.BlockSpec((1, qb, hq), lambda bi, i: (bi, i, 0)),
        out_shape=jax.ShapeDtypeStruct((b, s, hq), BF16),
        scratch_shapes=[pltpu.VMEM((NSA_KV_HEADS, kvd, glanes), BF16), pltpu.VMEM((NSA_KV_HEADS, NSA_HD, glanes), F32),
                        pltpu.VMEM((NSA_KV_HEADS, NSA_HD, glanes), F32), pltpu.VMEM((NSA_KV_HEADS, NSA_HD, glanes), F32),
                        pltpu.VMEM((ncp, qb), F32), pltpu.VMEM((NSA_KV_HEADS, nsbp, qb), F32),
                        pltpu.VMEM((hq, qb), F32)],
        compiler_params=_cparams(("parallel", "arbitrary"), VMEM_LIMIT), name="nsa_prompt",
    )(qt, gt, kc, vct, keys.reshape(b, s, 2 * kvd), vt, biasc, btile)
    return out.reshape(b * s, hq)


def _round_up(x, m):
    return (x + m - 1) // m * m


def prep_even(w_in, pe, wc):
    d, n = w_in.shape
    n_pad = _round_up(n - 3 * NSA_HEADS, LANES) + LANES
    w_pad = jnp.zeros((d, n_pad), BF16).at[:, :n].set(w_in.astype(BF16))
    hd = NSA_HD
    pe2 = jnp.tile(pe, (1, 1, NSA_KV_HEADS))
    zero = jnp.zeros_like(wc)
    wbd = jnp.concatenate([jnp.concatenate([wc if i == j else zero for j in range(NSA_KV_HEADS)], axis=-1)
                           for i in range(NSA_KV_HEADS)], axis=-2)
    half = CMP_BLOCK // 2
    wbd = jnp.concatenate([wbd[:, :half], wbd[:, half:]], axis=2)
    return dict(w_in=w_pad, pe2=pe2, wbd=wbd.astype(BF16))


def _memkv_kernel(x_ref, g_ref, w_ref, o_ref, ob_ref, *, hd, nh):
    y = _dot(_rms(x_ref[...], g_ref[...]).astype(BF16), w_ref[...])
    ob_ref[...] = y.astype(BF16)
    tm = y.shape[0]
    chunks = hd // LANES
    period = 2 * chunks * nh
    for kv in range(2):
        for h in range(nh):
            for c in range(chunks):
                col = (kv * nh + h) * hd + c * LANES
                o_ref[pl.ds((kv * chunks + c) * nh + h, tm, stride=period), :] = y[:, col:col + LANES]


def memkv(mem, g, w, nh):
    m, d = mem.shape
    n = w.shape[1]
    hd = n // (2 * nh)
    per_tok = n // LANES
    tm = _row_tile(m, 256)
    return pl.pallas_call(
        functools.partial(_memkv_kernel, hd=hd, nh=nh), grid=(m // tm,),
        in_specs=[pl.BlockSpec((tm, d), lambda i: (i, 0)), _full((1, d)), _full(w.shape)],
        out_specs=[pl.BlockSpec((tm * per_tok, LANES), lambda i: (i, 0)), pl.BlockSpec((tm, n), lambda i: (i, 0))],
        out_shape=[jax.ShapeDtypeStruct((m * per_tok, LANES), F32), jax.ShapeDtypeStruct((m, n), BF16)],
        compiler_params=_cparams(("parallel",)), name="memkv",
    )(mem, g.reshape(1, d), w)


def _xattn_core(q, kv, hd):
    nh = q.shape[1] // hd
    outs = []
    for h in range(nh):
        s = _dot_nt(q[:, h * hd:(h + 1) * hd], kv[:, h * hd:(h + 1) * hd])
        e = jnp.exp(s - jnp.max(s, -1, keepdims=True))
        p = e / jnp.sum(e, -1, keepdims=True)
        outs.append(_dot(p.astype(BF16), kv[:, (nh + h) * hd:(nh + h + 1) * hd]))
    return jnp.concatenate(outs, axis=1).astype(BF16)


def _xattn_prompt_kernel(*refs, hd, n_in):
    x_ref, g_ref, wq_ref, kv_ref, wo_ref = refs[:5]
    a_refs = refs[5:5 + n_in]
    w_refs = refs[5 + n_in:5 + 2 * n_in]
    o_ref = refs[5 + 2 * n_in]
    x = x_ref[0]
    for a_ref, w_ref in zip(a_refs, w_refs):
        x = x + _dot(a_ref[0], w_ref[...])
    q = (_dot(_rms(x, g_ref[...]).astype(BF16), wq_ref[...]) * (hd ** -0.5)).astype(BF16)
    o = _xattn_core(q, kv_ref[0], hd)
    o_ref[0] = x + _dot(o, wo_ref[...])


def xattn_prompt(x, g, wq, kvb, wo, b, s, acts=(), ws=()):
    d = x.shape[1]
    mt = kvb.shape[0] // b
    tm = _row_tile(s, 512)
    tile = lambda n: pl.BlockSpec((1, tm, n), lambda bi, i: (bi, i, 0))
    out = pl.pallas_call(
        functools.partial(_xattn_prompt_kernel, hd=d // X_HEADS, n_in=len(acts)), grid=(b, s // tm),
        in_specs=[tile(d), _full((1, d)), _full(wq.shape),
                  pl.BlockSpec((1, mt, kvb.shape[1]), lambda bi, i: (bi, 0, 0)), _full(wo.shape)]
        + [tile(a.shape[1]) for a in acts] + [_full(w.shape) for w in ws],
        out_specs=tile(d),
        out_shape=jax.ShapeDtypeStruct((b, s, d), F32),
        compiler_params=_cparams(("parallel", "parallel"), VMEM_LIMIT), name="xattn_prompt",
    )(x.reshape(b, s, d), g.reshape(1, d), wq, kvb.reshape(b, mt, -1), wo,
      *[a.reshape(b, s, -1) for a in acts], *ws)
    return out.reshape(b * s, d)


def _xattn_sample_kernel(x_ref, g_ref, wq_ref, kv_ref, wo_ref, o_ref, q_scr, a_scr, *, hd, mt):
    bi = pl.program_id(0)
    nb = pl.num_programs(0)
    nh = wq_ref.shape[1] // hd

    @pl.when(bi == 0)
    def _():
        q_scr[...] = _dot(_rms(x_ref[...], g_ref[...]).astype(BF16), wq_ref[...]) * (hd ** -0.5)

    q = jnp.broadcast_to(q_scr[pl.ds(bi, 1), :], (SUBLANES, q_scr.shape[1])).astype(BF16)
    chunks = hd // LANES
    period = 2 * chunks * nh

    def head_rows(kv, h):
        return jnp.concatenate([kv_ref[0, pl.ds((kv * chunks + c) * nh + h, mt, stride=period), :]
                                for c in range(chunks)], axis=1).astype(BF16)

    outs = []
    for h in range(nh):
        s = _dot_nt(q[:, h * hd:(h + 1) * hd], head_rows(0, h))
        e = jnp.exp(s - jnp.max(s, -1, keepdims=True))
        p = e / jnp.sum(e, -1, keepdims=True)
        outs.append(_dot(p.astype(BF16), head_rows(1, h)))
    a_scr[pl.ds(bi, 1), :] = jnp.concatenate(outs, axis=1)[0:1, :]

    @pl.when(bi == nb - 1)
    def _():
        o_ref[...] = x_ref[...] + _dot(a_scr[...].astype(BF16), wo_ref[...])


def xattn_sample(x, g, wq, kv_rows, wo, layer, mt):
    bd, d = x.shape
    rows = kv_rows.shape[1]
    return pl.pallas_call(
        functools.partial(_xattn_sample_kernel, hd=d // X_HEADS, mt=mt), grid=(bd,),
        in_specs=[_full((bd, d)), _full((1, d)), _full(wq.shape),
                  pl.BlockSpec((1, rows, LANES), lambda bi: (layer * bd + bi, 0, 0)), _full(wo.shape)],
        out_specs=_full((bd, d)),
        out_shape=jax.ShapeDtypeStruct((bd, d), F32),
        scratch_shapes=[pltpu.VMEM((bd, wq.shape[1]), F32), pltpu.VMEM((bd, wq.shape[1]), F32)],
        compiler_params=_cparams(("arbitrary",), VMEM_LIMIT), name="xattn_sample",
    )(x, g.reshape(1, d), wq, kv_rows, wo)


def _ffn_kernel(x_ref, g_ref, wg_ref, wu_ref, wd_ref, o_ref, h_scr, acc_scr):
    c = pl.program_id(1)

    @pl.when(c == 0)
    def _():
        h_scr[...] = _rms(x_ref[...], g_ref[...]).astype(BF16)
        acc_scr[...] = x_ref[...]

    h = h_scr[...]
    gate = _dot(h, wg_ref[...])
    up = _dot(h, wu_ref[...])
    act = (gate * jax.nn.sigmoid(gate) * up).astype(BF16)
    acc_scr[...] += _dot(act, wd_ref[...])

    @pl.when(c == pl.num_programs(1) - 1)
    def _():
        o_ref[...] = acc_scr[...]


def _xattn_ffn_kernel(*refs, hd, n_in):
    x_ref, gx_ref, wq_ref, kv_ref, wo_ref, gf_ref, wg_ref, wu_ref, wd_ref = refs[:9]
    a_refs = refs[9:9 + n_in]
    w_refs = refs[9 + n_in:9 + 2 * n_in]
    o_ref, h_scr, acc_scr = refs[9 + 2 * n_in:]
    c = pl.program_id(1)

    @pl.when(c == 0)
    def _():
        x = x_ref[...]
        for a_ref, w_ref in zip(a_refs, w_refs):
            x = x + _dot(a_ref[...], w_ref[...])
        q = (_dot(_rms(x, gx_ref[...]).astype(BF16), wq_ref[...]) * (hd ** -0.5)).astype(BF16)
        x = x + _dot(_xattn_core(q, kv_ref[0], hd), wo_ref[...])
        h_scr[...] = _rms(x, gf_ref[...]).astype(BF16)
        acc_scr[...] = x

    h = h_scr[...]
    gate = _dot(h, wg_ref[...])
    up = _dot(h, wu_ref[...])
    act = (gate * jax.nn.sigmoid(gate) * up).astype(BF16)
    acc_scr[...] += _dot(act, wd_ref[...])

    @pl.when(c == pl.num_programs(1) - 1)
    def _():
        o_ref[...] = acc_scr[...]


def xattn_ffn(x, gx, wq, kvb, wo, acts, ws, gf, w_gu, w_dn, b, s):
    m, d = x.shape
    dff = w_dn.shape[0]
    mt = kvb.shape[0] // b
    tm = _row_tile(s, 512)
    per_seq = s // tm
    fc = _ff_chunk(dff, 1408)
    nch = dff // fc
    row = lambda n: pl.BlockSpec((tm, n), lambda i, c: (i, 0))
    return pl.pallas_call(
        functools.partial(_xattn_ffn_kernel, hd=d // X_HEADS, n_in=len(acts)), grid=(m // tm, nch),
        in_specs=[row(d), _full((1, d)), _full(wq.shape),
                  pl.BlockSpec((1, mt, kvb.shape[1]), lambda i, c: (i // per_seq, 0, 0)), _full(wo.shape),
                  _full((1, d)),
                  pl.BlockSpec((d, fc), lambda i, c: (0, c)),
                  pl.BlockSpec((d, fc), lambda i, c: (0, nch + c)),
                  pl.BlockSpec((fc, d), lambda i, c: (c, 0))]
        + [row(a.shape[1]) for a in acts] + [_full(w.shape) for w in ws],
        out_specs=row(d),
        out_shape=jax.ShapeDtypeStruct((m, d), F32),
        scratch_shapes=[pltpu.VMEM((tm, d), BF16), pltpu.VMEM((tm, d), F32)],
        compiler_params=_cparams(("parallel", "arbitrary"), VMEM_LIMIT), name="xattn_ffn",
    )(x, gx.reshape(1, d), wq, kvb.reshape(b, mt, -1), wo, gf.reshape(1, d), w_gu, w_gu, w_dn, *acts, *ws)


def _ff_chunk(dff, pref):
    c = dff
    for n in range(1, dff // LANES + 1):
        if dff % n == 0 and (dff // n) % LANES == 0 and dff // n <= pref:
            c = dff // n
            break
    return c


def ffn(x, g, w_gu, w_dn):
    m, d = x.shape
    dff = w_dn.shape[0]
    tm = _row_tile(m, 512)
    fc = _ff_chunk(dff, 1408)
    nch = dff // fc
    return pl.pallas_call(
        _ffn_kernel, grid=(m // tm, nch),
        in_specs=[pl.BlockSpec((tm, d), lambda i, c: (i, 0)), _full((1, d)),
                  pl.BlockSpec((d, fc), lambda i, c: (0, c)),
                  pl.BlockSpec((d, fc), lambda i, c: (0, nch + c)),
                  pl.BlockSpec((fc, d), lambda i, c: (c, 0))],
        out_specs=pl.BlockSpec((tm, d), lambda i, c: (i, 0)),
        out_shape=jax.ShapeDtypeStruct((m, d), F32),
        scratch_shapes=[pltpu.VMEM((tm, d), BF16), pltpu.VMEM((tm, d), F32)],
        compiler_params=_cparams(("parallel", "arbitrary"), VMEM_LIMIT), name="ffn",
    )(x, g.reshape(1, d), w_gu, w_gu, w_dn)


def _router_kernel(x_ref, g_ref, w_ref, b_ref, comb_ref, h_ref, mask_ref, cnt_ref, *, ne):
    h = _rms(x_ref[...], g_ref[...]).astype(BF16)
    h_ref[...] = h
    logits = _dot(h, w_ref[...]) + b_ref[...]
    lane = lax.broadcasted_iota(jnp.int32, logits.shape, 1)
    logits = jnp.where(lane < ne, logits, -jnp.inf)
    v1 = jnp.max(logits, -1, keepdims=True)
    i1 = jnp.min(jnp.where(logits == v1, lane, LANES), -1, keepdims=True)
    rest = jnp.where(lane == i1, -jnp.inf, logits)
    v2 = jnp.max(rest, -1, keepdims=True)
    i2 = jnp.min(jnp.where(rest == v2, lane, LANES), -1, keepdims=True)
    e2 = jnp.exp(v2 - v1)
    den = 1.0 + e2
    comb_ref[...] = jnp.where(lane == i1, 1.0 / den, 0.0) + jnp.where(lane == i2, e2 / den, 0.0)
    chosen = jnp.where((lane == i1) | (lane == i2), 1.0, 0.0)
    mask_ref[...] = chosen.astype(BF16)

    @pl.when(pl.program_id(0) == 0)
    def _():
        cnt_ref[...] = jnp.zeros_like(cnt_ref)

    cnt_ref[0:1, :] += jnp.sum(chosen, axis=0, keepdims=True)


def router(x, g, w_r, b_r):
    m, d = x.shape
    ne = w_r.shape[1]
    w_pad = jnp.zeros((d, LANES), BF16).at[:, :ne].set(w_r.astype(BF16))
    b_pad = jnp.zeros((1, LANES), F32).at[0, :ne].set(b_r.astype(F32))
    tm = _row_tile(m, 512)
    row = lambda n: pl.BlockSpec((tm, n), lambda i: (i, 0))
    return pl.pallas_call(
        functools.partial(_router_kernel, ne=ne), grid=(m // tm,),
        in_specs=[row(d), _full((1, d)), _full((d, LANES)), _full((1, LANES))],
        out_specs=[row(LANES), row(d), row(LANES), _full((SUBLANES, LANES))],
        out_shape=[jax.ShapeDtypeStruct((m, LANES), F32), jax.ShapeDtypeStruct((m, d), BF16),
                   jax.ShapeDtypeStruct((m, LANES), BF16), jax.ShapeDtypeStruct((SUBLANES, LANES), F32)],
        compiler_params=_cparams(("arbitrary",)), name="router",
    )(x, g.reshape(1, d), w_pad, b_pad)


def _residual_out(y, gain_ref, norm):
    return _rms(y, gain_ref[...]) if norm else y


def _moe_kernel(x_ref, h_ref, comb_ref, wg_ref, wu_ref, wd_ref, gain_ref, o_ref, acc_scr, *, norm):
    e = pl.program_id(1)

    @pl.when(e == 0)
    def _():
        acc_scr[...] = jnp.zeros_like(acc_scr)

    h = h_ref[...]
    gate = _dot(h, wg_ref[0])
    up = _dot(h, wu_ref[0])
    act = (gate * jax.nn.sigmoid(gate) * up).astype(BF16)
    y = _dot(act, wd_ref[0])
    comb = comb_ref[...]
    lane = lax.broadcasted_iota(jnp.int32, comb.shape, 1)
    acc_scr[...] += jnp.sum(jnp.where(lane == e, comb, 0.0), -1, keepdims=True) * y

    @pl.when(e == pl.num_programs(1) - 1)
    def _():
        o_ref[...] = _residual_out(x_ref[...] + acc_scr[...], gain_ref, norm)


def moe(x, h, comb, w_gu, w_dn, final_g=None):
    m, d = x.shape
    ne, dfe = w_dn.shape[:2]
    tm = _row_tile(m, 512)
    gain = jnp.ones((1, d), F32) if final_g is None else final_g.reshape(1, d)
    return pl.pallas_call(
        functools.partial(_moe_kernel, norm=final_g is not None), grid=(m // tm, ne),
        in_specs=[pl.BlockSpec((tm, d), lambda i, e: (i, 0)), pl.BlockSpec((tm, d), lambda i, e: (i, 0)),
                  pl.BlockSpec((tm, LANES), lambda i, e: (i, 0)),
                  pl.BlockSpec((1, d, dfe), lambda i, e: (e, 0, 0)),
                  pl.BlockSpec((1, d, dfe), lambda i, e: (e, 0, 1)),
                  pl.BlockSpec((1, dfe, d), lambda i, e: (e, 0, 0)), _full((1, d))],
        out_specs=pl.BlockSpec((tm, d), lambda i, e: (i, 0)),
        out_shape=jax.ShapeDtypeStruct((m, d), F32),
        scratch_shapes=[pltpu.VMEM((tm, d), F32)],
        compiler_params=_cparams(("parallel", "arbitrary"), VMEM_LIMIT), name="moe",
    )(x, h, comb, w_gu, w_gu, w_dn, gain)


MOE_TILE = 256


def _moe_pos_kernel(mask_ref, comb_ref, tri_ref, base_ref, post_ref, pos2_ref, wab_ref, stab_ref, run_scr, *, nep):
    sb = pl.program_id(0)
    nb = pl.num_programs(0)

    @pl.when(sb == 0)
    def _():
        run_scr[...] = jnp.zeros_like(run_scr)
        stab_ref[...] = jnp.zeros_like(stab_ref)

    a = mask_ref[...]
    af = a.astype(F32)
    start = base_ref[...] + run_scr[...]
    stab_ref[pl.ds(sb, 1), :] = start.astype(jnp.int32)
    rank = _dot(tri_ref[...], a)
    pos = jnp.where(af > 0, start + rank, -1.0)
    lane = lax.broadcasted_iota(jnp.int32, pos.shape, 1)
    first_e = jnp.min(jnp.where(af > 0, lane, LANES), -1, keepdims=True)
    last_e = jnp.max(jnp.where(af > 0, lane, -1), -1, keepdims=True)
    comb = comb_ref[...]
    w_a = jnp.sum(jnp.where(lane == first_e, comb, 0.0), -1, keepdims=True)
    w_b = jnp.sum(jnp.where(lane == last_e, comb, 0.0), -1, keepdims=True)
    wab_ref[...] = jnp.where(lane == 0, w_a, jnp.where(lane == 1, w_b, 0.0))
    pos_t = pos.T[0:nep, :]
    post_ref[0] = pos_t.astype(jnp.int32)
    row = lax.broadcasted_iota(jnp.int32, pos_t.shape, 0)
    first_r = jnp.min(jnp.where(pos_t >= 0, row, nep), 0, keepdims=True)
    last_r = jnp.max(jnp.where(pos_t >= 0, row, -1), 0, keepdims=True)
    pos_a = jnp.sum(jnp.where(row == first_r, pos_t, 0.0), 0, keepdims=True)
    pos_b = jnp.sum(jnp.where(row == last_r, pos_t, 0.0), 0, keepdims=True)
    pos2_ref[0] = jnp.where(row == 0, pos_a, jnp.where(row == 1, pos_b, 0.0)).astype(jnp.int32)
    run_scr[...] += jnp.sum(af, axis=0, keepdims=True)

    @pl.when(sb == nb - 1)
    def _():
        stab_ref[pl.ds(nb, 1), :] = (base_ref[...] + run_scr[...]).astype(jnp.int32)


def _moe_expert_kernel(te_ref, lo_ref, cnt_ref, ring_ref, nt_ref, h_ref, post_ref, wg_ref, wu_ref, wd_ref, y_ref,
                       hbuf, xg_scr, sem, *, t):
    i = pl.program_id(0)

    @pl.when(i >= nt_ref[0])
    def _():
        y_ref[...] = jnp.zeros_like(y_ref)

    nbuf = hbuf.shape[0]
    ahead = nbuf - 1

    def copy(sb, slot):
        return pltpu.make_async_copy(h_ref.at[pl.ds(pl.multiple_of(sb * t, t), t), :], hbuf.at[slot], sem.at[slot])

    def start_first(tile):
        for j in range(ahead):
            @pl.when(j < cnt_ref[tile])
            def _():
                copy(lo_ref[tile] + j, (ring_ref[tile] + j) % nbuf).start()

    @pl.when(i == 0)
    def _():
        start_first(0)

    @pl.when(i < nt_ref[0])
    def _():
        e = te_ref[i]
        lo = lo_ref[i]
        n = cnt_ref[i]
        ring = ring_ref[i]
        xg_scr[...] = jnp.zeros_like(xg_scr)
        row = i * t + lax.broadcasted_iota(jnp.int32, (t, 1), 0)

        def body(j, c):
            slot = (ring + j) % nbuf
            copy(lo + j, slot).wait()

            @pl.when(j + ahead < n)
            def _():
                copy(lo + j + ahead, (ring + j + ahead) % nbuf).start()

            src_pos = post_ref[lo + j, pl.ds(e, 1), :]
            onehot = jnp.where(src_pos == row, 1.0, 0.0).astype(BF16)
            xg_scr[...] += _dot(onehot, hbuf[slot])
            return c

        lax.fori_loop(0, n, body, 0)

        @pl.when(i + 1 < nt_ref[0])
        def _():
            start_first(i + 1)

        x = xg_scr[...].astype(BF16)
        gate = _dot(x, wg_ref[0])
        up = _dot(x, wu_ref[0])
        act = (gate * jax.nn.sigmoid(gate) * up).astype(BF16)
        y_ref[...] = _dot(act, wd_ref[0])


def _moe_combine_kernel(x_ref, wab_ref, pos_ref, nxt_ref, gain_ref, ys_ref, o_ref, ybuf, sem, *, t, norm):
    sb = pl.program_id(0)
    nb = pl.num_programs(0)

    def row_copy(p_ref, which, slot, tok):
        return pltpu.make_async_copy(ys_ref.at[pl.ds(p_ref[0, which, tok], 1), :],
                                     ybuf.at[slot, which, pl.ds(tok, 1), :], sem.at[slot])

    def start_all(p_ref, slot):
        def body(tok, c):
            row_copy(p_ref, 0, slot, tok).start()
            row_copy(p_ref, 1, slot, tok).start()
            return c
        lax.fori_loop(0, t, body, 0, unroll=8)

    @pl.when(sb == 0)
    def _():
        start_all(pos_ref, 0)

    @pl.when(sb + 1 < nb)
    def _():
        start_all(nxt_ref, (sb + 1) % 2)

    slot = sb % 2

    def wait_body(tok, c):
        row_copy(pos_ref, 0, slot, tok).wait()
        row_copy(pos_ref, 1, slot, tok).wait()
        return c
    lax.fori_loop(0, t, wait_body, 0, unroll=8)
    w = wab_ref[...]
    y = x_ref[...] + w[:, 0:1] * ybuf[slot, 0] + w[:, 1:2] * ybuf[slot, 1]
    o_ref[...] = _residual_out(y, gain_ref, norm)


def moe_grouped(x, h, comb, mask, counts, w_gu, w_dn, final_g=None):
    m, d = x.shape
    ne, dfe = w_dn.shape[:2]
    t = MOE_TILE
    assert m % t == 0
    nb = m // t
    nep = _round_up(ne, SUBLANES)
    nbp = _round_up(nb + 1, SUBLANES)
    k_top = TOP_K
    nt_max = k_top * m // t + ne
    cnt = counts[0, :ne].astype(jnp.int32)
    cnt_pad = (cnt + t - 1) // t * t
    ends = jnp.cumsum(cnt_pad)
    base = ends - cnt_pad
    base_row = jnp.zeros((1, LANES), F32).at[0, :ne].set(base.astype(F32))
    idx = lax.broadcasted_iota(jnp.int32, (t, t), 0)
    tri = jnp.where(lax.broadcasted_iota(jnp.int32, (t, t), 1) < idx, 1.0, 0.0).astype(BF16)
    blk = lambda n2: pl.BlockSpec((t, n2), lambda i: (i, 0))
    post, pos2, wab, stab = pl.pallas_call(
        functools.partial(_moe_pos_kernel, nep=nep), grid=(nb,),
        in_specs=[blk(LANES), blk(LANES), _full((t, t)), _full((1, LANES))],
        out_specs=[pl.BlockSpec((1, nep, t), lambda i: (i, 0, 0)), pl.BlockSpec((1, nep, t), lambda i: (i, 0, 0)),
                   blk(LANES), _full((nbp, LANES))],
        out_shape=[jax.ShapeDtypeStruct((nb, nep, t), jnp.int32), jax.ShapeDtypeStruct((nb, nep, t), jnp.int32),
                   jax.ShapeDtypeStruct((m, LANES), F32), jax.ShapeDtypeStruct((nbp, LANES), jnp.int32)],
        scratch_shapes=[pltpu.VMEM((1, LANES), F32)],
        compiler_params=_cparams(("arbitrary",)), name="moe_positions",
    )(mask, comb, tri, base_row)
    r0 = jnp.arange(nt_max, dtype=jnp.int32) * t
    tile_e = jnp.minimum(jnp.sum(ends[None, :] <= r0[:, None], axis=1), ne - 1).astype(jnp.int32)
    n_tiles = (ends[-1] // t).astype(jnp.int32).reshape(1)
    s_e = stab[:nb + 1, :ne][:, tile_e]
    lo = jnp.sum(s_e[1:] <= r0[None, :], axis=0)
    hi = jnp.sum(s_e[:nb] < r0[None, :] + t, axis=0) - 1
    lo = jnp.clip(lo, 0, nb - 1).astype(jnp.int32)
    hi = jnp.clip(hi, lo, nb - 1).astype(jnp.int32)
    n_src = jnp.where(jnp.arange(nt_max) < n_tiles[0], hi - lo + 1, 0).astype(jnp.int32)
    n_ring = 6
    ring = ((jnp.cumsum(n_src) - n_src) % n_ring).astype(jnp.int32)
    w_spec = lambda shape, col: pl.BlockSpec(shape, lambda i, te, *_: (te[i], 0, col))
    grid_spec = pltpu.PrefetchScalarGridSpec(
        num_scalar_prefetch=5, grid=(nt_max,),
        in_specs=[pl.BlockSpec(memory_space=pl.ANY),
                  pl.BlockSpec((nb, nep, t), lambda i, *_: (0, 0, 0)),
                  w_spec((1, d, dfe), 0), w_spec((1, d, dfe), 1), w_spec((1, dfe, d), 0)],
        out_specs=pl.BlockSpec((t, d), lambda i, *_: (i, 0)),
        scratch_shapes=[pltpu.VMEM((n_ring, t, d), BF16), pltpu.VMEM((t, d), F32),
                        pltpu.SemaphoreType.DMA((n_ring,))])
    ys = pl.pallas_call(
        functools.partial(_moe_expert_kernel, t=t), grid_spec=grid_spec,
        out_shape=jax.ShapeDtypeStruct((nt_max * t, d), F32),
        compiler_params=_cparams(("arbitrary",), VMEM_LIMIT), name="moe_experts",
    )(tile_e, lo, n_src, ring, n_tiles, h, post, w_gu, w_gu, w_dn)
    smem_blk = lambda f: pl.BlockSpec((1, nep, t), f, memory_space=pltpu.SMEM)
    gain = jnp.ones((1, d), F32) if final_g is None else final_g.reshape(1, d)
    return pl.pallas_call(
        functools.partial(_moe_combine_kernel, t=t, norm=final_g is not None), grid=(nb,),
        in_specs=[blk(d), blk(LANES), smem_blk(lambda i: (i, 0, 0)),
                  smem_blk(lambda i: (jnp.minimum(i + 1, nb - 1), 0, 0)), _full((1, d)),
                  pl.BlockSpec(memory_space=pl.ANY)],
        out_specs=blk(d),
        out_shape=jax.ShapeDtypeStruct((m, d), F32),
        scratch_shapes=[pltpu.VMEM((2, 2, t, d), F32), pltpu.SemaphoreType.DMA((2,))],
        compiler_params=_cparams(("arbitrary",), VMEM_LIMIT), name="moe_combine",
    )(x, wab, pos2, pos2, gain, ys)


def _inproj_odd_kernel(x_ref, g_ref, w_ref, bias_ref, q_ref, k_ref, v_ref, og_ref, gc_ref, gr_ref, *, hq, hv, nh):
    xn = _rms(x_ref[...], g_ref[...]).astype(BF16)

    def mm(lo, hi):
        return _dot(xn, w_ref[:, lo:hi])

    dk = hq // nh
    q_ref[...] = mm(0, hq).astype(BF16)
    k_ref[...] = (mm(hq, 2 * hq) * (dk ** -0.5)).astype(BF16)
    v_ref[...] = mm(2 * hq, 2 * hq + hv).astype(BF16)
    og_ref[...] = jax.nn.sigmoid(mm(2 * hq + hv, 2 * hq + 2 * hv))
    gi = mm(2 * hq + 2 * hv, 2 * hq + 2 * hv + LANES) + bias_ref[...]
    lane = lax.broadcasted_iota(jnp.int32, gi.shape, 1)
    gates = jnp.where(lane < nh, gi, jax.nn.log_sigmoid(gi))
    gc_ref[...] = gates
    gr_ref[...] = gates.T[0:SUBLANES, :]


def inproj_odd(x, g, w_pad, gate_bias):
    m, d = x.shape
    nh = MLSTM_HEADS
    hq = hv = d
    tm = _row_tile(m, 256)
    row = lambda n: pl.BlockSpec((tm, n), lambda i: (i, 0))
    outs = [(hq, BF16), (hq, BF16), (hv, BF16), (hv, F32), (LANES, F32)]
    if tm % LANES:
        gr_spec = _full((SUBLANES, m))
    else:
        gr_spec = pl.BlockSpec((SUBLANES, tm), lambda i: (0, i))
    return pl.pallas_call(
        functools.partial(_inproj_odd_kernel, hq=hq, hv=hv, nh=nh), grid=(m // tm,),
        in_specs=[row(d), _full((1, d)), _full(w_pad.shape), _full((1, LANES))],
        out_specs=[row(n) for n, _ in outs] + [gr_spec],
        out_shape=[jax.ShapeDtypeStruct((m, n), t) for n, t in outs] + [jax.ShapeDtypeStruct((SUBLANES, m), F32)],
        compiler_params=_cparams(("parallel",), VMEM_LIMIT), name="inproj_odd",
    )(x, g.reshape(1, d), w_pad, gate_bias)


def prep_odd(w_in, b_i, b_f):
    d, n = w_in.shape
    n_pad = _round_up(n - 2 * MLSTM_HEADS, LANES) + LANES
    w_pad = jnp.zeros((d, n_pad), BF16).at[:, :n].set(w_in.astype(BF16))
    bias = jnp.zeros((1, LANES), F32).at[0, :2 * MLSTM_HEADS].set(jnp.concatenate([b_i, b_f]).astype(F32))
    return dict(w_in=w_pad, bias=bias)


def _mlstm_prompt_kernel(q_ref, k_ref, v_ref, og_ref, gc_ref, gr_ref, gain_ref, hn_ref, c_ref, n_ref, m_ref,
                         *, nh, dk, dv, ln):
    ci = pl.program_id(1)

    @pl.when(ci == 0)
    def _():
        c_ref[...] = jnp.zeros_like(c_ref)
        n_ref[...] = jnp.zeros_like(n_ref)
        m_ref[...] = jnp.full(m_ref.shape, NEG, F32)

    row = lax.broadcasted_iota(jnp.int32, (ln, ln), 0)
    col = lax.broadcasted_iota(jnp.int32, (ln, ln), 1)
    tri = row >= col
    gc = gc_ref[...]
    gr = gr_ref[...]
    for h in range(nh):
        q = q_ref[:, h * dk:(h + 1) * dk]
        k = k_ref[:, h * dk:(h + 1) * dk]
        v = v_ref[:, h * dv:(h + 1) * dv]
        ig_c, lf_c = gc[:, h:h + 1], gc[:, nh + h:nh + h + 1]
        ig_r, lf_r = gr[h:h + 1, :], gr[nh + h:nh + h + 1, :]
        b_c = jnp.sum(jnp.where(tri, lf_r, 0.0), axis=1, keepdims=True)
        b_r = jnp.sum(jnp.where(row <= col, lf_c, 0.0), axis=0, keepdims=True)
        m_prev = m_ref[0, h:h + 1, 0:1]
        c_prev = c_ref[0, h]
        n_prev = n_ref[0, h:h + 1, :]
        dmat = jnp.where(tri, b_c - b_r + ig_r, NEG)
        inter = b_c + m_prev
        mt = jnp.maximum(inter, jnp.max(dmat, -1, keepdims=True))
        wm = jnp.exp(dmat - mt)
        a = jnp.exp(inter - mt)
        wqk = wm * _dot_nt(q, k)
        num = a * _dot_nt(q, c_prev.astype(BF16)) + _dot(wqk.astype(BF16), v)
        den = a * jnp.sum(q.astype(F32) * n_prev, -1, keepdims=True) + jnp.sum(wqk, -1, keepdims=True)
        hh = num / jnp.maximum(jnp.abs(den), jnp.exp(-mt))
        b_end = b_c[ln - 1:ln, :]
        m_new = mt[ln - 1:ln, :]
        a_end = jnp.exp(b_end + m_prev - m_new)
        w_s = jnp.exp(b_end - b_c + ig_c - m_new)
        c_ref[0, h] = a_end * c_prev + _dot_tn((v.astype(F32) * w_s).astype(BF16), k)
        n_ref[0, h:h + 1, :] = a_end * n_prev + jnp.sum(w_s * k.astype(F32), axis=0, keepdims=True)
        m_ref[0, h:h + 1, :] = jnp.broadcast_to(m_new, (1, m_ref.shape[2]))
        hn = hh * lax.rsqrt(jnp.mean(hh * hh, -1, keepdims=True) + RMS_EPS)
        hn = hn * gain_ref[:, h * dv:(h + 1) * dv] * og_ref[:, h * dv:(h + 1) * dv]
        hn_ref[:, h * dv:(h + 1) * dv] = hn.astype(hn_ref.dtype)


def mlstm_prompt(q, k, v, og, gc, gr, gain, b, s):
    m, d = q.shape
    nh = MLSTM_HEADS
    dk = dv = d // nh
    ln = _row_tile(s, MLSTM_CHUNK)
    nch = s // ln
    row = lambda n: pl.BlockSpec((ln, n), lambda bi, ci: (bi * nch + ci, 0))
    return pl.pallas_call(
        functools.partial(_mlstm_prompt_kernel, nh=nh, dk=dk, dv=dv, ln=ln), grid=(b, nch),
        in_specs=[row(d), row(d), row(d), row(d), row(LANES),
                  pl.BlockSpec((SUBLANES, ln), lambda bi, ci: (0, bi * nch + ci)), _full((1, d))],
        out_specs=[row(d), pl.BlockSpec((1, nh, dv, dk), lambda bi, ci: (bi, 0, 0, 0)),
                   pl.BlockSpec((1, nh, dk), lambda bi, ci: (bi, 0, 0)),
                   pl.BlockSpec((1, nh, LANES), lambda bi, ci: (bi, 0, 0))],
        out_shape=[jax.ShapeDtypeStruct((m, d), BF16), jax.ShapeDtypeStruct((b, nh, dv, dk), F32),
                   jax.ShapeDtypeStruct((b, nh, dk), F32), jax.ShapeDtypeStruct((b, nh, LANES), F32)],
        compiler_params=_cparams(("parallel", "arbitrary"), VMEM_LIMIT), name="mlstm_prompt",
    )(q, k, v, og, gc, gr, gain.reshape(1, d))


def _mlstm_sample_kernel(q_ref, k_ref, v_ref, og_ref, g_ref, gain_ref, c_ref, n_ref, m_ref,
                         hn_ref, co_ref, no_ref, mo_ref, *, nh):
    row = lax.broadcasted_iota(jnp.int32, (SUBLANES, 1), 0)
    for h in range(nh):
        q = q_ref[0, h:h + 1, :]
        k = k_ref[0, h:h + 1, :]
        v = v_ref[0, h:h + 1, :].astype(F32)
        ig = g_ref[0, h:h + 1, 0:1]
        lf = g_ref[0, h:h + 1, 1:2]
        m_prev = m_ref[0, h:h + 1, :]
        c_prev = c_ref[0, h]
        n_prev = n_ref[0, h:h + 1, :]
        inter = lf + m_prev
        mt = jnp.maximum(inter, ig)
        wm = jnp.exp(ig - mt)
        a = jnp.exp(inter - mt)
        q8 = jnp.broadcast_to(q, (SUBLANES, q.shape[1]))
        cq = _dot_nt(q8, c_prev.astype(BF16))[0:1, :]
        wqk = wm * jnp.sum(q.astype(F32) * k.astype(F32), -1, keepdims=True)
        num = a * cq + wqk * v
        den = a * jnp.sum(n_prev * q.astype(F32), -1, keepdims=True) + wqk
        hh = num / jnp.maximum(jnp.abs(den), jnp.exp(-mt))
        v8 = jnp.where(row == 0, jnp.broadcast_to(v * wm, (SUBLANES, v.shape[1])), 0.0).astype(BF16)
        k8 = jnp.broadcast_to(k, (SUBLANES, k.shape[1]))
        co_ref[0, h] = a * c_prev + _dot_tn(v8, k8)
        no_ref[0, h:h + 1, :] = a * n_prev + wm * k.astype(F32)
        mo_ref[0, h:h + 1, :] = mt
        hn = hh * lax.rsqrt(jnp.mean(hh * hh, -1, keepdims=True) + RMS_EPS)
        hn_ref[0, h:h + 1, :] = (hn * gain_ref[h:h + 1, :] * og_ref[0, h:h + 1, :]).astype(hn_ref.dtype)


def mlstm_sample(q, k, v, og, gc, gain, c, n, m):
    bd, d = q.shape
    nh = MLSTM_HEADS
    dk = d // nh
    heads = lambda a: a.reshape(bd, nh, dk)
    g2 = jnp.transpose(gc[:, :2 * nh].reshape(bd, 2, nh), (0, 2, 1))
    blk3 = lambda n2: pl.BlockSpec((1, nh, n2), lambda bi: (bi, 0, 0))
    cspec = pl.BlockSpec((1, nh, dk, dk), lambda bi: (bi, 0, 0, 0))
    hn, co, no, mo = pl.pallas_call(
        functools.partial(_mlstm_sample_kernel, nh=nh), grid=(bd,),
        in_specs=[blk3(dk), blk3(dk), blk3(dk), blk3(dk), blk3(2), _full((nh, dk)), cspec, blk3(dk), blk3(1)],
        out_specs=[blk3(dk), cspec, blk3(dk), blk3(1)],
        out_shape=[jax.ShapeDtypeStruct((bd, nh, dk), BF16), jax.ShapeDtypeStruct(c.shape, F32),
                   jax.ShapeDtypeStruct(n.shape, F32), jax.ShapeDtypeStruct((bd, nh, 1), F32)],
        compiler_params=_cparams(("parallel",)), name="mlstm_sample",
    )(heads(q), heads(k), heads(v), heads(og), g2, gain.reshape(nh, dk), c, n, m.reshape(bd, nh, 1))
    return hn.reshape(bd, d), co, no, mo.reshape(bd, nh)


def _row_to_col(row):
    n = row.shape[1]
    eye = lax.broadcasted_iota(jnp.int32, (n, n), 0) == lax.broadcasted_iota(jnp.int32, (n, n), 1)
    return jnp.sum(jnp.where(eye, row, 0.0), axis=1, keepdims=True)


def _head_pad(q, keep):
    q2 = jnp.concatenate([q] * NSA_KV_HEADS, axis=1)
    row = lax.broadcasted_iota(jnp.int32, q2.shape, 0)
    lane = lax.broadcasted_iota(jnp.int32, q2.shape, 1)
    return jnp.where((lane // NSA_HD == row // NSA_GROUP) & keep(row), q2, jnp.zeros_like(q2))


def _nsa_sample_cmp_kernel(pt_ref, q_ref, pages_ref, pe_ref, w_ref, biasc_ref, oc_ref, idx_ref, xbuf, xrow, sem,
                           *, n_pages, nc, ncp, nsb, n_sel, past):
    b = pl.program_id(0)
    nb = pl.num_programs(0)
    kvd = NSA_KV_HEADS * NSA_HD

    def page_copy(bb, slot, p, sl):
        return pltpu.make_async_copy(pages_ref.at[pt_ref[bb * n_pages + p], pl.ds(sl * kvd, kvd), :],
                                     xbuf.at[slot, sl, p], sem.at[slot])

    def start_all(bb, slot):
        def body(p, c):
            page_copy(bb, slot, p, 0).start()
            page_copy(bb, slot, p, 1).start()
            return c
        lax.fori_loop(0, n_pages, body, 0)

    @pl.when(b == 0)
    def _():
        start_all(0, 0)

    @pl.when(b + 1 < nb)
    def _():
        start_all(b + 1, (b + 1) % 2)

    slot = b % 2

    def wait_body(p, c):
        page_copy(b, slot, p, 0).wait()
        page_copy(b, slot, p, 1).wait()
        return c
    lax.fori_loop(0, n_pages, wait_body, 0)

    def file_page(p, c):
        for sl in range(2):
            _file_rows(xbuf[slot, sl, p], xrow, sl, p)
        return c
    lax.fori_loop(0, n_pages, file_page, 0, unroll=8)

    acc = _compress_filed(xrow, pe_ref, w_ref, nc)
    kc = acc[:, 0:kvd].astype(BF16)
    vc = acc[:, kvd:2 * kvd].astype(BF16)
    q = q_ref[0]
    nh = q.shape[0]
    qpad = _head_pad(q, lambda r: r >= 0)
    s = _dot_nt(qpad, kc)
    s = s + biasc_ref[:, 0:nc]
    e = jnp.exp(s - jnp.max(s, -1, keepdims=True))
    p_c = e / jnp.sum(e, -1, keepdims=True)
    o = _dot(p_c.astype(BF16), vc)
    row = lax.broadcasted_iota(jnp.int32, (nh, NSA_HD), 0)
    o_h = o[:, 0:NSA_HD]
    for k in range(1, NSA_KV_HEADS):
        o_h = jnp.where(row // NSA_GROUP == k, o[:, k * NSA_HD:(k + 1) * NSA_HD], o_h)
    oc_ref[0] = o_h
    prow = lax.broadcasted_iota(jnp.int32, p_c.shape, 0)
    lane = lax.broadcasted_iota(jnp.int32, (1, ncp), 1)
    blk = lane // 2
    cur = past // SEL_BLOCK
    forced = (blk == 0) | (blk == cur) | (blk == cur - 1)
    is_cand = ((lane % 2) == 0) & (lane < 2 * nsb)
    nselp = idx_ref.shape[1]
    rsel = lax.broadcasted_iota(jnp.int32, (nselp, 1), 0).astype(F32)
    out_lane = lax.broadcasted_iota(jnp.int32, (nselp, LANES), 1)
    result = jnp.full((nselp, LANES), -1, jnp.int32)
    for k in range(NSA_KV_HEADS):
        imp = jnp.sum(jnp.where(prow // NSA_GROUP == k, p_c, 0.0), axis=0, keepdims=True)
        imp = jnp.concatenate([imp, jnp.zeros((1, ncp - nc), F32)], axis=1)
        imp = _pair_sum(imp)
        score = jnp.where(forced, FORCE_SCORE, jnp.where(blk <= cur, imp, -1.0))
        score = jnp.where(is_cand, score, -2.0)
        sel, rank = _select_blocks(score, n_sel, nsb)
        hit = (rank == rsel) & (sel > 0.5)
        idx = jnp.sum(jnp.where(hit, (blk + 1).astype(F32), 0.0), axis=1, keepdims=True) - 1.0
        result = jnp.where(out_lane == k, idx.astype(jnp.int32), result)
    idx_ref[0] = result


def nsa_sample_cmp(q8, pages, page_table, pe2, wbd, rel_bias):
    bd, nh, hd = q8.shape
    n_pages = page_table.shape[1]
    past = n_pages * PAGE_SIZE
    nc = past // CMP_BLOCK
    nsb = -(-(past + 1) // SEL_BLOCK)
    n_sel = min(TOP_N, nsb)
    ncp = _round_up(max(nc, SEL_RATIO * nsb), LANES)
    nselp = _round_up(n_sel, SUBLANES)
    kvd = wbd.shape[-1]
    cend = jnp.arange(nc) * CMP_BLOCK + (CMP_BLOCK - 1)
    biasc = jnp.zeros((nh, ncp), F32).at[:, :nc].set(_bias_of(rel_bias, past - cend).T)
    grid_spec = pltpu.PrefetchScalarGridSpec(
        num_scalar_prefetch=1, grid=(bd,),
        in_specs=[pl.BlockSpec((1, nh, hd), lambda bi, pt: (bi, 0, 0)),
                  pl.BlockSpec(memory_space=pl.ANY),
                  pl.BlockSpec(pe2.shape, lambda bi, pt: (0, 0, 0)),
                  pl.BlockSpec(wbd.shape, lambda bi, pt: (0, 0, 0, 0)),
                  pl.BlockSpec((nh, ncp), lambda bi, pt: (0, 0))],
        out_specs=[pl.BlockSpec((1, nh, hd), lambda bi, pt: (bi, 0, 0)),
                   pl.BlockSpec((1, nselp, LANES), lambda bi, pt: (bi, 0, 0))],
        scratch_shapes=[pltpu.VMEM((2, 2, n_pages, kvd, PAGE_SIZE), F32),
                        pltpu.VMEM((2, CMP_BLOCK // SUBLANES, nc * SUBLANES, kvd), F32),
                        pltpu.SemaphoreType.DMA((2,))])
    oc, idx = pl.pallas_call(
        functools.partial(_nsa_sample_cmp_kernel, n_pages=n_pages, nc=nc, ncp=ncp, nsb=nsb, n_sel=n_sel, past=past),
        grid_spec=grid_spec,
        out_shape=[jax.ShapeDtypeStruct((bd, nh, hd), F32), jax.ShapeDtypeStruct((bd, nselp, LANES), jnp.int32)],
        compiler_params=_cparams(("arbitrary",), VMEM_LIMIT), name="nsa_sample_cmp",
    )(page_table.reshape(-1), q8, pages, pe2, wbd, biasc)
    sel_idx = jnp.transpose(idx[:, :n_sel, :NSA_KV_HEADS], (0, 2, 1))
    return oc, sel_idx


def _nsa_sample_att_kernel(pt_ref, si_ref, q_ref, g_ref, oc_ref, kvn_ref, wn_ref, wc_ref, pages_ref,
                           bsel_ref, bwin_ref, ob_ref, win_ref, selbuf, wall, sem,
                           *, n_pages, n_sel, past, wb):
    b = pl.program_id(0)
    nb = pl.num_programs(0)
    kvd = NSA_KV_HEADS * NSA_HD
    hd = NSA_HD
    n_blk_pages = past // SEL_BLOCK
    per_page = PAGE_SIZE // SEL_BLOCK
    n_slots = NSA_KV_HEADS * n_sel

    def blk_of(bb, j):
        return si_ref[bb * n_slots + j]

    def blk_copy(bb, slot, j):
        blk = jnp.clip(blk_of(bb, j), 0, n_blk_pages - 1)
        page = pt_ref[bb * n_pages + blk // per_page]
        return pltpu.make_async_copy(pages_ref.at[page, pl.ds(2 * kvd, 2 * kvd), :], selbuf.at[slot, j], sem.at[slot])

    def in_pages(bb, j):
        blk = blk_of(bb, j)
        return (blk >= 0) & (blk < n_blk_pages)

    def start_all(bb, slot):
        def body(j, c):
            @pl.when(in_pages(bb, j))
            def _():
                blk_copy(bb, slot, j).start()
            return c
        lax.fori_loop(0, n_slots, body, 0)

    @pl.when(b == 0)
    def _():
        start_all(0, 0)

    @pl.when(b + 1 < nb)
    def _():
        start_all(b + 1, (b + 1) % 2)

    slot = b % 2
    new_sel = _row_to_col(kvn_ref[0][:, 2 * kvd:4 * kvd])
    lane = lax.broadcasted_iota(jnp.int32, (1, PAGE_SIZE), 1)

    def wait_body(j, c):
        @pl.when(in_pages(b, j))
        def _():
            blk_copy(b, slot, j).wait()

        @pl.when(jnp.logical_not(in_pages(b, j)))
        def _():
            is_new = blk_of(b, j) == n_blk_pages
            selbuf[slot, j] = jnp.where((lane == 0) & is_new, new_sel, 0.0)
        return c
    lax.fori_loop(0, n_slots, wait_body, 0)

    q = q_ref[0]
    nh = q.shape[0]
    gates = g_ref[0]
    head = lax.broadcasted_iota(jnp.int32, (nh, 1), 0)

    def attend(qp, keys_t, vals_t, bias, valid):
        s = jnp.where(valid, _dot(qp, keys_t) + bias, NEG)
        e = jnp.where(valid, jnp.exp(s - jnp.max(s, -1, keepdims=True)), 0.0)
        den = jnp.sum(e, -1, keepdims=True)
        p = e / jnp.where(den > 0, den, 1.0)
        return _dot_nt(p.astype(BF16), vals_t)

    o_s = jnp.zeros((nh, kvd), F32)
    for k in range(NSA_KV_HEADS):
        blks = [blk_of(b, k * n_sel + r) for r in range(n_sel)]
        tiles = [selbuf[slot, k * n_sel + r] for r in range(n_sel)]
        keys_t = jnp.concatenate([t_[0:kvd, :] for t_ in tiles], axis=1).astype(BF16)
        vals_t = jnp.concatenate([t_[kvd:2 * kvd, :] for t_ in tiles], axis=1).astype(BF16)
        bias = jnp.concatenate([bsel_ref[jnp.clip(bl // per_page, 0, n_pages)] for bl in blks], axis=1)
        valid = jnp.concatenate(
            [(lane // SEL_BLOCK == bl % per_page) & ((bl // per_page) * PAGE_SIZE + lane <= past) & (bl >= 0)
             for bl in blks], axis=1)
        o_k = attend(_head_pad(q, lambda r: r // NSA_GROUP == k), keys_t, vals_t, bias, valid)
        o_s = jnp.where(head // NSA_GROUP == k, o_k, o_s)
    wlanes = wall.shape[1]
    wall[:, 0:wb] = wc_ref[0]
    tail = lax.broadcasted_iota(jnp.int32, (1, wlanes - wb), 1)
    wall[:, wb:wlanes] = jnp.where(tail == 0, _row_to_col(wn_ref[0]), 0.0)
    win_ref[0] = pltpu.roll(wall[...], wlanes - 1, 1)[:, 0:wb]
    w_pos = lax.broadcasted_iota(jnp.int32, (1, wlanes), 1)
    valid_w = (w_pos <= wb) & (wb - w_pos < WINDOW) & (past - wb + w_pos >= 0)
    o_w = attend(_head_pad(q, lambda r: r >= 0), wall[0:kvd, :].astype(BF16), wall[kvd:2 * kvd, :].astype(BF16),
                 bwin_ref[...], valid_w)
    o_c = jnp.concatenate([oc_ref[0]] * NSA_KV_HEADS, axis=1)
    mix = gates[:, 0:1] * o_c + gates[:, 1:2] * o_s + gates[:, 2:3] * o_w
    out = mix[:, 0:hd]
    for k in range(1, NSA_KV_HEADS):
        out = jnp.where(head // NSA_GROUP == k, mix[:, k * hd:(k + 1) * hd], out)
    ob_ref[0] = out.astype(ob_ref.dtype)


def nsa_sample_att(q8, gates, oc, kv03, kv45, wcache_t, layer, pages_t, page_table, sel_idx, rel_bias):
    bd, nh, hd = q8.shape
    n_pages = page_table.shape[1]
    past = n_pages * PAGE_SIZE
    wb = wcache_t.shape[2]
    kvd = NSA_KV_HEADS * hd
    n_sel = sel_idx.shape[2]
    wlanes = _round_up(wb + 1, LANES)
    g3 = gates[:, :3 * nh].reshape(bd, nh, 3)
    dist = past - (jnp.arange(n_pages + 1)[:, None] * PAGE_SIZE + jnp.arange(PAGE_SIZE)[None, :])
    bsel = jnp.transpose(_bias_of(rel_bias, dist), (0, 2, 1))
    bwin = _bias_of(rel_bias, wb - jnp.arange(wlanes)).T
    blk = lambda n2, n3: pl.BlockSpec((1, n2, n3), lambda bi, pt, si: (bi, 0, 0))
    grid_spec = pltpu.PrefetchScalarGridSpec(
        num_scalar_prefetch=2, grid=(bd,),
        in_specs=[blk(nh, hd), blk(nh, 3), blk(nh, hd), blk(1, 4 * kvd), blk(1, 2 * kvd),
                  pl.BlockSpec((1, 2 * kvd, wb), lambda bi, pt, si: (layer * bd + bi, 0, 0)),
                  pl.BlockSpec(memory_space=pl.ANY),
                  pl.BlockSpec(bsel.shape, lambda bi, pt, si: (0, 0, 0)),
                  pl.BlockSpec(bwin.shape, lambda bi, pt, si: (0, 0))],
        out_specs=[blk(nh, hd), blk(2 * kvd, wb)],
        scratch_shapes=[pltpu.VMEM((2, NSA_KV_HEADS * n_sel, 2 * kvd, PAGE_SIZE), F32),
                        pltpu.VMEM((2 * kvd, wlanes), F32), pltpu.SemaphoreType.DMA((2,))])
    ob, win = pl.pallas_call(
        functools.partial(_nsa_sample_att_kernel, n_pages=n_pages, n_sel=n_sel, past=past, wb=wb),
        grid_spec=grid_spec,
        out_shape=[jax.ShapeDtypeStruct((bd, nh, hd), BF16), jax.ShapeDtypeStruct((bd, 2 * kvd, wb), F32)],
        compiler_params=_cparams(("arbitrary",), VMEM_LIMIT), name="nsa_sample_att",
    )(page_table.reshape(-1), sel_idx.reshape(-1), q8, g3, oc, kv03.reshape(bd, 1, -1), kv45.reshape(bd, 1, -1),
      wcache_t, pages_t, bsel, bwin)
    return ob.reshape(bd, nh * hd), win


def kernel(x_prompt, x_sample, mem_prompt, cache_conv, cache_nsa_pages, cache_nsa_window, state_mlstm_c,
           state_mlstm_n, state_mlstm_m, cache_mem_kv, page_table, rel_bias, norm_mix, norm_xattn, norm_mem,
           norm_ffn, norm_final, w_in_even, w_out_even, conv_w, conv_b, conv_ln_g, conv_ln_b, nsa_cmp_pe,
           nsa_cmp_w, w_in_odd, mlstm_b_i, mlstm_b_f, mlstm_norm, w_out_odd, xattn_wq, xattn_wkv, xattn_wo,
           ffn_w_gu, ffn_w_dn, router_w, router_b, expert_w_gu, expert_w_dn):
    b, s, d = x_prompt.shape
    bd, td, _ = x_sample.shape
    assert td == 1, "the sample group decodes one token per sequence"
    depth = norm_mix.shape[0]
    mt = mem_prompt.shape[1]
    cc = conv_w.shape[2]
    hist = conv_w.shape[1] - 1
    wb = cache_nsa_window.shape[2]
    kvh, hd = NSA_KV_HEADS, NSA_HD
    n_pool = cache_nsa_pages.shape[1]
    assert s >= hist and s >= wb and s % Q_BLOCK == 0
    xp = x_prompt.reshape(b * s, d)
    xs = x_sample.reshape(bd, d)
    mem = mem_prompt.reshape(b * mt, d)
    pages_t = jnp.swapaxes(cache_nsa_pages.reshape(-1, PAGE_SIZE, 4 * kvh * hd), 1, 2)
    window_t = jnp.swapaxes(cache_nsa_window.reshape(-1, wb, 2 * kvh * hd), 1, 2)
    xhd = d // X_HEADS
    memkv_rows = jnp.swapaxes(cache_mem_kv.reshape(depth * bd, mt, 2, X_HEADS, xhd // LANES, LANES), 3, 4)
    memkv_rows = memkv_rows.reshape(depth * bd, -1, LANES)
    bf = lambda a: a.astype(BF16)
    conv_p, conv_s, nsa_p, nsa_s, win_p, win_s = [], [], [], [], [], []
    mc_p, mc_s, mn_p, mn_s, mm_p, mm_s, memkv_p = [], [], [], [], [], [], []
    for l in range(depth):
        li = l // 2
        if l % 2 == 0:
            prm = prep_even(w_in_even[li], nsa_cmp_pe[li], nsa_cmp_w[li])
            w_out = bf(w_out_even[li])
            w_parts = [w_out[:cc], w_out[cc:]]
            conv_args = (conv_w[li], conv_b[li], conv_ln_g[li], conv_ln_b[li])
            glu, keys, qt, vt, gt, kv_t = inproj_even(xp, norm_mix[l], prm['w_in'], cc, s)
            a_out = conv_prompt(glu, *conv_args, b, s)
            kc, vct = compress_prompt(kv_t, prm['pe2'], prm['wbd'], _round_up(s // CMP_BLOCK, LANES))
            b_out = nsa_prompt(qt, gt, kc, vct, keys, vt, rel_bias, b, s)
            mix_p = ([a_out, b_out], w_parts)
            conv_p.append(glu.reshape(b, s, cc)[:, s - hist:])
            rows_t = kv_t.reshape(b, 6, kvh, hd, s)
            nsa_p.append(jnp.transpose(rows_t[:, :4], (0, 4, 1, 2, 3)))
            win_p.append(jnp.transpose(rows_t[:, 4:, :, :, s - wb:], (0, 4, 1, 2, 3)))
            glu, kv03, kv45, q, gates = inproj_even(xs, norm_mix[l], prm['w_in'], cc)
            a_out, conv_state = conv_sample(cache_conv[li], glu, *conv_args)
            q8 = q.reshape(bd, NSA_HEADS, hd)
            pt = page_table + li * n_pool
            o_c, sel_idx = nsa_sample_cmp(q8, pages_t, pt, prm['pe2'], prm['wbd'], rel_bias)
            b_out, win = nsa_sample_att(q8, gates, o_c, kv03, kv45, window_t, li, pages_t, pt, sel_idx, rel_bias)
            xs = outproj(xs, [a_out, b_out], w_parts)
            conv_s.append(conv_state)
            nsa_s.append(kv03.reshape(bd, 1, 4, kvh, hd))
            win_s.append(jnp.transpose(win.reshape(bd, 2, kvh, hd, wb), (0, 4, 1, 2, 3)))
        else:
            prm = prep_odd(w_in_odd[li], mlstm_b_i[li], mlstm_b_f[li])
            w_out = bf(w_out_odd[li])
            q, k, v, og, gc, gr = inproj_odd(xp, norm_mix[l], prm['w_in'], prm['bias'])
            hn, c_new, n_new, m_new = mlstm_prompt(q, k, v, og, gc, gr, mlstm_norm[li], b, s)
            mix_p = ([hn], [w_out])
            mc_p.append(c_new)
            mn_p.append(n_new)
            mm_p.append(m_new[:, :, 0])
            q, k, v, og, gc, gr = inproj_odd(xs, norm_mix[l], prm['w_in'], prm['bias'])
            hn, c_new, n_new, m_new = mlstm_sample(q, k, v, og, gc, mlstm_norm[li], state_mlstm_c[li],
                                                   state_mlstm_n[li], state_mlstm_m[li])
            xs = outproj(xs, [hn], [w_out])
            mc_s.append(c_new)
            mn_s.append(n_new)
            mm_s.append(m_new)
        wq, wo = bf(xattn_wq[l]), bf(xattn_wo[l])
        mkv_rows, mkv_b = memkv(mem, norm_mem[l], bf(xattn_wkv[l]), X_HEADS)
        mkv = jnp.swapaxes(mkv_rows.reshape(b, mt, 2, xhd // LANES, X_HEADS, LANES), 3, 4)
        memkv_p.append(mkv.reshape(b, mt, 2, X_HEADS, xhd))
        xs = xattn_sample(xs, norm_xattn[l], wq, memkv_rows, wo, l, mt)
        if l % 2 == 0:
            w_gu, w_dn = bf(ffn_w_gu[li]), bf(ffn_w_dn[li])
            xp = xattn_ffn(xp, norm_xattn[l], wq, mkv_b, wo, *mix_p, norm_ffn[l], w_gu, w_dn, b, s)
            xs = ffn(xs, norm_ffn[l], w_gu, w_dn)
        else:
            xp = xattn_prompt(xp, norm_xattn[l], wq, mkv_b, wo, b, s, *mix_p)
            e_gu, e_dn = bf(expert_w_gu[li]), bf(expert_w_dn[li])
            final_g = norm_final if l == depth - 1 else None
            comb, h, mask, counts = router(xp, norm_ffn[l], router_w[li], router_b[li])
            xp = moe_grouped(xp, h, comb, mask, counts, e_gu, e_dn, final_g)
            comb, h, _, _ = router(xs, norm_ffn[l], router_w[li], router_b[li])
            xs = moe(xs, h, comb, e_gu, e_dn, final_g)
    if depth % 2:
        xp, xs = rmsnorm(xp, norm_final), rmsnorm(xs, norm_final)
    y_prompt = xp.reshape(b, s, d)
    y_sample = xs.reshape(bd, 1, d)
    return (y_prompt, y_sample, jnp.stack(conv_p), jnp.stack(conv_s), jnp.stack(nsa_p), jnp.stack(nsa_s),
            jnp.stack(win_p), jnp.stack(win_s), jnp.stack(mc_p), jnp.stack(mc_s), jnp.stack(mn_p),
            jnp.stack(mn_s), jnp.stack(mm_p), jnp.stack(mm_s), jnp.stack(memkv_p))
```

```python
import functools
import math

import jax
import jax.numpy as jnp
import numpy as np
from jax import lax
from jax.experimental import pallas as pl
from jax.experimental.pallas import tpu as pltpu

F32 = jnp.float32
BF16 = jnp.bfloat16

PAGE_SIZE = 128
CONV_WIDTH = 31
NSA_HEADS = 8
NSA_KV_HEADS = 2
NSA_GROUP = NSA_HEADS // NSA_KV_HEADS
NSA_HD = 64
CMP_BLOCK = 32
SEL_BLOCK = 64
SEL_RATIO = SEL_BLOCK // CMP_BLOCK
TOP_N = 16
WINDOW = 512
Q_BLOCK = 128
FORCE_SCORE = 1.0e4
NUM_BUCKETS = 32
MAX_DISTANCE = 1024
MLSTM_HEADS = 4
X_HEADS = 4
N_EXPERTS = 8
TOP_K = 2
RMS_EPS = 1e-6
LN_EPS = 1e-5
NEG = -1e30

LANES = 128
SUBLANES = 8
VMEM_LIMIT = 56 * 1024 * 1024
MLSTM_CHUNK = 256
SEL_TILES = 2
WIN_TILES = 2


def _cparams(sem, vmem=None):
    return pltpu.CompilerParams(dimension_semantics=sem, vmem_limit_bytes=vmem)


def _rms(x, g):
    return x * lax.rsqrt(jnp.mean(x * x, -1, keepdims=True) + RMS_EPS) * g


def _dot(a, b):
    return jnp.dot(a, b, preferred_element_type=F32)


def _dot_nt(a, b):
    return lax.dot_general(a, b, (((1,), (1,)), ((), ())), preferred_element_type=F32)


def _dot_tn(a, b):
    return lax.dot_general(a, b, (((0,), (0,)), ((), ())), preferred_element_type=F32)


def _full(shape):
    n = len(shape)
    return pl.BlockSpec(shape, lambda *_: (0,) * n)


def _row_tile(m, pref):
    t = min(pref, m)
    while m % t:
        t //= 2
    return t


def _rmsnorm_kernel(x_ref, g_ref, o_ref):
    o_ref[...] = _rms(x_ref[...], g_ref[...])


def rmsnorm(x, g):
    m, d = x.shape
    tm = _row_tile(m, 1024)
    return pl.pallas_call(
        _rmsnorm_kernel, grid=(m // tm,),
        in_specs=[pl.BlockSpec((tm, d), lambda i: (i, 0)), _full((1, d))],
        out_specs=pl.BlockSpec((tm, d), lambda i: (i, 0)),
        out_shape=jax.ShapeDtypeStruct((m, d), F32),
        compiler_params=_cparams(("parallel",)), name="rmsnorm",
    )(x, g.reshape(1, d))


def _outproj_kernel(*refs, n_in):
    x_ref = refs[0]
    a_refs = refs[1:1 + n_in]
    w_refs = refs[1 + n_in:1 + 2 * n_in]
    o_ref = refs[1 + 2 * n_in]
    acc = x_ref[...]
    for a_ref, w_ref in zip(a_refs, w_refs):
        acc = acc + _dot(a_ref[...], w_ref[...])
    o_ref[...] = acc


def outproj(x, acts, ws):
    m, d = x.shape
    tm = _row_tile(m, 512)
    n_in = len(acts)
    in_specs = [pl.BlockSpec((tm, d), lambda i: (i, 0))]
    in_specs += [pl.BlockSpec((tm, a.shape[1]), lambda i: (i, 0)) for a in acts]
    in_specs += [_full(w.shape) for w in ws]
    return pl.pallas_call(
        functools.partial(_outproj_kernel, n_in=n_in), grid=(m // tm,),
        in_specs=in_specs, out_specs=pl.BlockSpec((tm, d), lambda i: (i, 0)),
        out_shape=jax.ShapeDtypeStruct((m, d), F32),
        compiler_params=_cparams(("parallel",)), name="outproj",
    )(x, *acts, *ws)


def _inproj_even_kernel(x_ref, g_ref, w_ref, glu_ref, *rest, cc, qd, kvd, tiles):
    xn = _rms(x_ref[...], g_ref[...]).astype(BF16)

    def mm(lo, hi):
        return _dot(xn, w_ref[:, lo:hi])

    o = 0
    a = mm(o, o + cc)
    b = mm(o + cc, o + 2 * cc)
    glu_ref[...] = a * jax.nn.sigmoid(b)
    o += 2 * cc
    q = mm(o, o + qd) * (NSA_HD ** -0.5)
    o += qd
    kv03 = mm(o, o + 4 * kvd)
    o += 4 * kvd
    kv45 = mm(o, o + 2 * kvd)
    o += 2 * kvd
    gates = jax.nn.sigmoid(mm(o, o + LANES))
    if tiles == 0:
        kv03_ref, kv45_ref, q_ref, gate_ref = rest
        kv03_ref[...] = kv03
        kv45_ref[...] = kv45
        q_ref[...] = q.astype(BF16)
        gate_ref[...] = gates
        return
    keys_ref, qt_ref, vt_ref, gt_ref, kvt_ref = rest
    kvt_ref[0] = jnp.concatenate([kv03, kv45], axis=1).T
    keys_ref[...] = jnp.concatenate([kv03[:, 2 * kvd:3 * kvd], kv45[:, 0:kvd]], axis=1).astype(BF16)
    vals = jnp.concatenate([kv03[:, 3 * kvd:4 * kvd], kv45[:, kvd:2 * kvd]], axis=1)
    for j in range(tiles):
        rows = slice(j * Q_BLOCK, (j + 1) * Q_BLOCK)
        qt_ref[j] = q[rows, :].T.astype(BF16)
        vt_ref[j] = vals[rows, :].T.astype(BF16)
        gt_ref[j] = gates[rows, :].T


def inproj_even(x, g, w_pad, cc, seq=None):
    m, d = x.shape
    qd = NSA_HEADS * NSA_HD
    kvd = NSA_KV_HEADS * NSA_HD
    tm = _row_tile(m, 256)
    row = lambda n: pl.BlockSpec((tm, n), lambda i: (i, 0))
    out_specs = [row(cc)]
    out_shape = [jax.ShapeDtypeStruct((m, cc), F32)]
    transposed = seq is not None
    tiles = tm // Q_BLOCK if transposed else 0
    if transposed:
        assert tm % Q_BLOCK == 0 and seq % tm == 0
        per_seq = seq // tm
        tile = lambda n: pl.BlockSpec((tiles, n, Q_BLOCK), lambda i: (i, 0, 0))
        out_specs += [row(2 * kvd), tile(qd), tile(2 * kvd), tile(LANES),
                      pl.BlockSpec((1, 6 * kvd, tm), lambda i: (i // per_seq, 0, i % per_seq))]
        out_shape += [jax.ShapeDtypeStruct((m, 2 * kvd), BF16),
                      jax.ShapeDtypeStruct((m // Q_BLOCK, qd, Q_BLOCK), BF16),
                      jax.ShapeDtypeStruct((m // Q_BLOCK, 2 * kvd, Q_BLOCK), BF16),
                      jax.ShapeDtypeStruct((m // Q_BLOCK, LANES, Q_BLOCK), F32),
                      jax.ShapeDtypeStruct((m // seq, 6 * kvd, seq), F32)]
    else:
        out_specs += [row(4 * kvd), row(2 * kvd), row(qd), row(LANES)]
        out_shape += [jax.ShapeDtypeStruct((m, 4 * kvd), F32), jax.ShapeDtypeStruct((m, 2 * kvd), F32),
                      jax.ShapeDtypeStruct((m, qd), BF16), jax.ShapeDtypeStruct((m, LANES), F32)]
    return pl.pallas_call(
        functools.partial(_inproj_even_kernel, cc=cc, qd=qd, kvd=kvd, tiles=tiles), grid=(m // tm,),
        in_specs=[row(d), _full((1, d)), _full(w_pad.shape)],
        out_specs=out_specs, out_shape=out_shape,
        compiler_params=_cparams(("parallel",)), name="inproj_even",
    )(x, g.reshape(1, d), w_pad)


def _conv_post(y, lg, lb):
    mu = jnp.mean(y, -1, keepdims=True)
    var = jnp.mean(jnp.square(y - mu), -1, keepdims=True)
    yn = (y - mu) * lax.rsqrt(var + LN_EPS) * lg + lb
    return yn * jax.nn.sigmoid(yn)


CONV_SUB = 64
CONV_PAD = 32


def _conv_prompt_kernel(glu_ref, cw_ref, cb_ref, lg_ref, lb_ref, o_ref, ext_ref, y_ref, *, ts, s):
    i = pl.program_id(1)
    c = glu_ref.shape[-1]

    @pl.when(i == 0)
    def _():
        ext_ref[0:CONV_PAD, :] = jnp.zeros((CONV_PAD, c), F32)
        ext_ref[CONV_PAD:CONV_PAD + s, :] = glu_ref[0]
        ext_ref[CONV_PAD + s:CONV_PAD + s + SUBLANES, :] = jnp.zeros((SUBLANES, c), F32)

    lead = CONV_PAD - (CONV_WIDTH - 1)
    span = CONV_SUB + CONV_PAD

    def sub(j, carry):
        r0 = pl.multiple_of(i * ts + j * CONV_SUB, CONV_SUB)
        for c0 in range(0, c, LANES):
            xw = ext_ref[pl.ds(r0, span + SUBLANES), c0:c0 + LANES]
            acc = jnp.zeros((CONV_SUB, LANES), F32) + cb_ref[:, c0:c0 + LANES]
            for r in range(SUBLANES):
                xr = xw if r == 0 else pltpu.roll(xw, span + SUBLANES - r, 0)
                for a in range(span // SUBLANES):
                    w = SUBLANES * a + r - lead
                    if 0 <= w < CONV_WIDTH:
                        acc = acc + xr[SUBLANES * a:SUBLANES * a + CONV_SUB, :] * cw_ref[w:w + 1, c0:c0 + LANES]
            y_ref[:, c0:c0 + LANES] = acc
        o_ref[0, pl.ds(pl.multiple_of(j * CONV_SUB, CONV_SUB), CONV_SUB), :] = _conv_post(
            y_ref[...], lg_ref[...], lb_ref[...]).astype(o_ref.dtype)
        return carry

    lax.fori_loop(0, ts // CONV_SUB, sub, 0)


def conv_prompt(glu, cw, cb, lg, lb, b, s):
    c = glu.shape[-1]
    ts = _row_tile(s, 256)
    vec = lambda a: a.reshape(1, c)
    out = pl.pallas_call(
        functools.partial(_conv_prompt_kernel, ts=ts, s=s), grid=(b, s // ts),
        in_specs=[pl.BlockSpec((1, s, c), lambda bi, i: (bi, 0, 0)), _full((CONV_WIDTH, c)),
                  _full((1, c)), _full((1, c)), _full((1, c))],
        out_specs=pl.BlockSpec((1, ts, c), lambda bi, i: (bi, i, 0)),
        out_shape=jax.ShapeDtypeStruct((b, s, c), BF16),
        scratch_shapes=[pltpu.VMEM((CONV_PAD + s + SUBLANES, c), F32), pltpu.VMEM((CONV_SUB, c), F32)],
        compiler_params=_cparams(("parallel", "arbitrary")), name="conv_prompt",
    )(glu.reshape(b, s, c), cw, vec(cb), vec(lg), vec(lb))
    return out.reshape(b * s, c)


def _conv_sample_kernel(cache_ref, glu_ref, cw_ref, cb_ref, lg_ref, lb_ref, o_ref, st_ref):
    hist = CONV_WIDTH - 1
    cache = cache_ref[...]
    glu = glu_ref[...]
    y = jnp.sum(cache * cw_ref[0:hist, :][None], axis=1) + glu * cw_ref[hist:hist + 1, :] + cb_ref[...]
    o_ref[...] = _conv_post(y, lg_ref[...], lb_ref[...]).astype(o_ref.dtype)
    st_ref[:, 0:hist - 1, :] = cache[:, 1:hist, :]
    st_ref[:, hist - 1:hist, :] = glu[:, None, :]


def conv_sample(cache, glu, cw, cb, lg, lb):
    bd, hist, c = cache.shape
    vec = lambda a: a.reshape(1, c)
    return pl.pallas_call(
        _conv_sample_kernel,
        out_shape=[jax.ShapeDtypeStruct((bd, c), BF16), jax.ShapeDtypeStruct((bd, hist, c), F32)],
        name="conv_sample",
    )(cache, glu, cw, vec(cb), vec(lg), vec(lb))


def _rel_bucket(dist):
    n = jnp.maximum(dist, 0)
    max_exact = NUM_BUCKETS // 2
    nf = jnp.maximum(n, 1).astype(F32)
    large = max_exact + (jnp.log(nf / max_exact) / math.log(MAX_DISTANCE / max_exact)
                         * (NUM_BUCKETS - max_exact)).astype(jnp.int32)
    large = jnp.minimum(large, NUM_BUCKETS - 1)
    return jnp.where(n < max_exact, n, large)


def _bias_of(rel_bias, dist):
    bucket = _rel_bucket(dist)[..., None]
    out = jnp.zeros(bucket.shape[:-1] + (rel_bias.shape[1],), F32)
    for k in range(NUM_BUCKETS):
        out = jnp.where(bucket == k, rel_bias[k].astype(F32), out)
    return out


def _compress_accumulate(load_rows, pe_ref, w_ref, nc):
    accs = []
    half = CMP_BLOCK // 2
    for slot in range(2):
        acc = jnp.zeros((nc, w_ref.shape[-1]), F32)
        for j in range(half):
            pair = [load_rows(slot, jj) + pe_ref[slot, jj:jj + 1, :] for jj in (j, j + half)]
            acc = acc + _dot(jnp.concatenate(pair, axis=1).astype(BF16), w_ref[slot, j])
        accs.append(acc)
    return jnp.concatenate(accs, axis=1)


def _file_rows(tile_t, xrow, sl, page):
    rows = tile_t.T
    per_page = PAGE_SIZE // CMP_BLOCK
    for cl in range(per_page):
        for a in range(CMP_BLOCK // SUBLANES):
            r0 = cl * CMP_BLOCK + a * SUBLANES
            dst = pl.multiple_of((page * per_page + cl) * SUBLANES, SUBLANES)
            xrow[sl, a, pl.ds(dst, SUBLANES), :] = rows[r0:r0 + SUBLANES, :]


def _compress_filed(xrow, pe_ref, w_ref, nc):
    return _compress_accumulate(
        lambda sl, j: xrow[sl, j // SUBLANES, pl.ds(j % SUBLANES, nc, stride=SUBLANES), :], pe_ref, w_ref, nc)


def _compress_prompt_kernel(x_ref, pe_ref, w_ref, kc_ref, vct_ref, xrow, *, nc, ncp, kvd):
    for p in range(x_ref.shape[2] // PAGE_SIZE):
        for sl in range(2):
            _file_rows(x_ref[0, sl * kvd:(sl + 1) * kvd, p * PAGE_SIZE:(p + 1) * PAGE_SIZE], xrow, sl, p)
    acc = _compress_filed(xrow, pe_ref, w_ref, nc)
    if ncp > nc:
        acc = jnp.concatenate([acc, jnp.zeros((ncp - nc, 2 * kvd), F32)], axis=0)
    kc_ref[0] = acc[:, 0:kvd].astype(BF16)
    vct_ref[0] = acc[:, kvd:2 * kvd].T.astype(BF16)


def compress_prompt(kv_t, pe2, wbd, ncp):
    b, _, s = kv_t.shape
    assert s % PAGE_SIZE == 0
    nc = s // CMP_BLOCK
    kvd = wbd.shape[-1]
    return pl.pallas_call(
        functools.partial(_compress_prompt_kernel, nc=nc, ncp=ncp, kvd=kvd), grid=(b,),
        in_specs=[pl.BlockSpec((1, 2 * kvd, s), lambda bi: (bi, 0, 0)), _full(pe2.shape), _full(wbd.shape)],
        out_specs=[pl.BlockSpec((1, ncp, kvd), lambda bi: (bi, 0, 0)), pl.BlockSpec((1, kvd, ncp), lambda bi: (bi, 0, 0))],
        out_shape=[jax.ShapeDtypeStruct((b, ncp, kvd), BF16), jax.ShapeDtypeStruct((b, kvd, ncp), BF16)],
        scratch_shapes=[pltpu.VMEM((2, CMP_BLOCK // SUBLANES, nc * SUBLANES, kvd), F32)],
        compiler_params=_cparams(("parallel",)), name="compress_prompt",
    )(kv_t, pe2, wbd)


def _select_blocks(score, n_sel, n_cand):
    lane = lax.broadcasted_iota(jnp.int32, score.shape, 1)
    rank = jnp.zeros(score.shape, F32)
    for i in range(n_cand):
        col = score[:, 2 * i:2 * i + 1]
        beats = (col > score) | ((col == score) & (lane > 2 * i))
        rank = rank + beats.astype(F32)
    is_cand = ((lane % 2) == 0) & (lane < 2 * n_cand)
    return (is_cand & (rank < n_sel) & (score >= 0)).astype(F32), rank


def _pair_sum(imp):
    n = imp.shape[1]
    return imp + pltpu.roll(imp, n - 1, 1)


def _rank_rows(score, n_sel, n_cand):
    blk = lax.broadcasted_iota(jnp.int32, score.shape, 0)
    rank = jnp.zeros(score.shape, F32)
    for i in range(n_cand):
        row = score[i:i + 1, :]
        beats = (row > score) | ((row == score) & (blk > i))
        rank = rank + beats.astype(F32)
    return ((rank < n_sel) & (score >= 0)).astype(F32)


def _nsa_prompt_kernel(qt_ref, gt_ref, kc_ref, vct_ref, keys_ref, vt_ref, biasc_ref, btile_ref, o_ref,
                       qt_scr, oc_scr, acc_s, acc_w, imp_scr, sel_scr, out_scr, *, nc, nsb, n_sel):
    qi = pl.program_id(1)
    g, hd, kvh, qb = NSA_GROUP, NSA_HD, NSA_KV_HEADS, Q_BLOCK
    kvd = kvh * hd
    ncp = kc_ref.shape[1]
    nsbp = sel_scr.shape[1]
    q_pos = qi * qb + lax.broadcasted_iota(jnp.int32, (1, qb), 1)
    key_row = lax.broadcasted_iota(jnp.int32, (qb, 1), 0)
    c_row = lax.broadcasted_iota(jnp.int32, (ncp, 1), 0)
    mask_c = (q_pos >= c_row * CMP_BLOCK + (CMP_BLOCK - 1)) & (c_row < nc)
    blk = lax.broadcasted_iota(jnp.int32, (nsbp, 1), 0)
    cur = q_pos // SEL_BLOCK
    forced = (blk == 0) | (blk == cur) | (blk == cur - 1)
    zeros = jnp.zeros((hd, qb), BF16)
    for k in range(kvh):
        for gi in range(g):
            h = k * g + gi
            parts = [zeros] * kvh
            parts[k] = qt_ref[0, h * hd:(h + 1) * hd, :]
            qt_scr[k, :, gi * qb:(gi + 1) * qb] = jnp.concatenate(parts, axis=0)
        s_c = _dot(kc_ref[0], qt_scr[k])
        imp = jnp.zeros((ncp, qb), F32)
        probs = []
        for gi in range(g):
            s = jnp.where(mask_c, s_c[:, gi * qb:(gi + 1) * qb] + biasc_ref[k * g + gi], NEG)
            e = jnp.where(mask_c, jnp.exp(s - jnp.max(s, 0, keepdims=True)), 0.0)
            den = jnp.sum(e, 0, keepdims=True)
            p = e / jnp.where(den > 0, den, 1.0)
            imp = imp + p
            probs.append(p.astype(BF16))
        oc_scr[k] = _dot(vct_ref[0, k * hd:(k + 1) * hd, :], jnp.concatenate(probs, axis=1))
        imp_scr[...] = imp + pltpu.roll(imp, ncp - 1, 0)
        cand = imp_scr[pl.ds(0, nsbp, stride=SEL_RATIO), :]
        score = jnp.where(forced, FORCE_SCORE, jnp.where(blk <= cur, cand, -1.0))
        sel_scr[k] = _rank_rows(jnp.where(blk < nsb, score, -2.0), n_sel, nsb)

    per_tile = qb // SEL_BLOCK
    n_tiles = keys_ref.shape[1] // qb
    first = ([jnp.full((1, qb), NEG, F32)] * g, [jnp.zeros((1, qb), F32)] * g)

    def tile_step(tiles, carry, key_col, val_row, acc_ref, window):
        kts = [jnp.clip(kt, 0, n_tiles - 1) for kt, _ in tiles]
        starts = [pl.multiple_of(kt * qb, qb) for kt in kts]
        k_t = jnp.concatenate([keys_ref[0, pl.ds(r0, qb), key_col:key_col + kvd] for r0 in starts], axis=0)
        dist = jnp.concatenate([jnp.where(active, q_pos - (r0 + key_row), -1)
                                for r0, (_, active) in zip(starts, tiles)], axis=0)
        in_range = dist >= 0
        scores = [_dot(k_t, qt_scr[k]) for k in range(kvh)]
        new, updates = [], []
        for k in range(kvh):
            if window:
                valid = in_range & (dist < WINDOW)
            else:
                pieces = []
                for kt in kts:
                    chosen = jnp.zeros((qb, qb), F32)
                    for j in range(per_tile):
                        row = sel_scr[k, pl.ds(per_tile * kt + j, 1), :]
                        chosen = jnp.where(key_row // SEL_BLOCK == j, row, chosen)
                    pieces.append(chosen)
                valid = in_range & (jnp.concatenate(pieces, axis=0) > 0.5)
            ms, ls = carry[k]
            ms2, ls2, alphas, probs = [], [], [], []
            for gi in range(g):
                bias = jnp.concatenate([btile_ref[jnp.maximum(qi - kt, 0), k * g + gi] for kt in kts], axis=0)
                s = jnp.where(valid, scores[k][:, gi * qb:(gi + 1) * qb] + bias, NEG)
                m_new = jnp.maximum(ms[gi], jnp.max(s, 0, keepdims=True))
                alpha = jnp.exp(ms[gi] - m_new)
                p = jnp.exp(s - jnp.where(m_new == NEG, 0.0, m_new))
                ms2.append(m_new)
                ls2.append(alpha * ls[gi] + jnp.sum(p, 0, keepdims=True))
                alphas.append(alpha)
                probs.append(p.astype(BF16))
            new.append((ms2, ls2))
            updates.append((jnp.concatenate(alphas, axis=1), jnp.concatenate(probs, axis=1)))
        for k, (alpha, prob) in enumerate(updates):
            v_t = jnp.concatenate([vt_ref[kt, val_row + k * hd:val_row + (k + 1) * hd, :] for kt in kts], axis=1)
            acc_ref[k] = acc_ref[k] * alpha + _dot(v_t, prob)
        return tuple(new)

    acc_s[...] = jnp.zeros_like(acc_s)
    acc_w[...] = jnp.zeros_like(acc_w)
    sel_args = dict(key_col=0, val_row=0, acc_ref=acc_s, window=False)
    win_args = dict(key_col=kvd, val_row=kvd, acc_ref=acc_w, window=True)

    def sel_group(i, carry):
        return tile_step([(SEL_TILES * i + j, SEL_TILES * i + j <= qi) for j in range(SEL_TILES)], carry, **sel_args)

    stat_s = lax.fori_loop(0, qi // SEL_TILES + 1, sel_group, (first,) * kvh)
    stat_w = (first,) * kvh
    win_tiles = [(qi - j, qi - j >= 0) for j in range(WINDOW // qb, -1, -1)]
    for j in range(0, len(win_tiles), WIN_TILES):
        stat_w = tile_step(win_tiles[j:j + WIN_TILES], stat_w, **win_args)
    gt = gt_ref[0]
    for k in range(kvh):
        for gi in range(g):
            h = k * g + gi
            cols = slice(gi * qb, (gi + 1) * qb)
            l_s, l_w = stat_s[k][1][gi], stat_w[k][1][gi]
            o_s = acc_s[k, :, cols] / jnp.where(l_s > 0, l_s, 1.0)
            o_w = acc_w[k, :, cols] / jnp.where(l_w > 0, l_w, 1.0)
            out_scr[h * hd:(h + 1) * hd, :] = (gt[3 * h:3 * h + 1, :] * oc_scr[k, :, cols]
                                               + gt[3 * h + 1:3 * h + 2, :] * o_s + gt[3 * h + 2:3 * h + 3, :] * o_w)
    o_ref[0] = out_scr[...].T.astype(o_ref.dtype)


def nsa_prompt(qt, gt, kc, vct, keys, vt, rel_bias, b, s):
    qb = Q_BLOCK
    nq = s // qb
    nc = s // CMP_BLOCK
    ncp = kc.shape[1]
    nsb = s // SEL_BLOCK
    nsbp = _round_up(nsb, SUBLANES)
    assert SEL_RATIO * nsbp <= ncp
    n_sel = min(TOP_N, nsb)
    nh = NSA_HEADS
    hq = qt.shape[1]
    kvd = kc.shape[2]
    glanes = NSA_GROUP * qb
    cend = jnp.arange(ncp)[:, None] * CMP_BLOCK + (CMP_BLOCK - 1)
    biasc = jnp.transpose(_bias_of(rel_bias, jnp.arange(s)[None, :] - cend), (2, 0, 1))
    r = jnp.arange(qb)
    dist = jnp.arange(nq)[:, None, None] * qb + r[None, None, :] - r[None, :, None]
    btile = jnp.transpose(_bias_of(rel_bias, dist), (0, 3, 1, 2))
    out = pl.pallas_call(
        functools.partial(_nsa_prompt_kernel, nc=nc, nsb=nsb, n_sel=n_sel), grid=(b, nq),
        in_specs=[pl.BlockSpec((1, hq, qb), lambda bi, i: (bi * nq + i, 0, 0)),
                  pl.BlockSpec((1, LANES, qb), lambda bi, i: (bi * nq + i, 0, 0)),
                  pl.BlockSpec((1, ncp, kvd), lambda bi, i: (bi, 0, 0)),
                  pl.BlockSpec((1, kvd, ncp), lambda bi, i: (bi, 0, 0)),
                  pl.BlockSpec((1, s, 2 * kvd), lambda bi, i: (bi, 0, 0)),
                  pl.BlockSpec((nq, 2 * kvd, qb), lambda bi, i: (bi, 0, 0)),
                  pl.BlockSpec((nh, ncp, qb), lambda bi, i: (0, 0, i)),
                  _full(btile.shape)],
        out_specs=pl.BlockSpec((1, qb, hq), lambda bi, i: (bi, i, 0)),
        out_shape=jax.ShapeDtypeStruct((b, s, hq), BF16),
        scratch_shapes=[pltpu.VMEM((NSA_KV_HEADS, kvd, glanes), BF16), pltpu.VMEM((NSA_KV_HEADS, NSA_HD, glanes), F32),
                        pltpu.VMEM((NSA_KV_HEADS, NSA_HD, glanes), F32), pltpu.VMEM((NSA_KV_HEADS, NSA_HD, glanes), F32),
                        pltpu.VMEM((ncp, qb), F32), pltpu.VMEM((NSA_KV_HEADS, nsbp, qb), F32),
                        pltpu.VMEM((hq, qb), F32)],
        compiler_params=_cparams(("parallel", "arbitrary"), VMEM_LIMIT), name="nsa_prompt",
    )(qt, gt, kc, vct, keys.reshape(b, s, 2 * kvd), vt, biasc, btile)
    return out.reshape(b * s, hq)


def _round_up(x, m):
    return (x + m - 1) // m * m


def prep_even(w_in, pe, wc):
    d, n = w_in.shape
    n_pad = _round_up(n - 3 * NSA_HEADS, LANES) + LANES
    w_pad = jnp.zeros((d, n_pad), BF16).at[:, :n].set(w_in.astype(BF16))
    hd = NSA_HD
    pe2 = jnp.tile(pe, (1, 1, NSA_KV_HEADS))
    zero = jnp.zeros_like(wc)
    wbd = jnp.concatenate([jnp.concatenate([wc if i == j else zero for j in range(NSA_KV_HEADS)], axis=-1)
                           for i in range(NSA_KV_HEADS)], axis=-2)
    half = CMP_BLOCK // 2
    wbd = jnp.concatenate([wbd[:, :half], wbd[:, half:]], axis=2)
    return dict(w_in=w_pad, pe2=pe2, wbd=wbd.astype(BF16))


def _memkv_kernel(x_ref, g_ref, w_ref, o_ref, ob_ref, *, hd, nh):
    y = _dot(_rms(x_ref[...], g_ref[...]).astype(BF16), w_ref[...])
    ob_ref[...] = y.astype(BF16)
    tm = y.shape[0]
    chunks = hd // LANES
    period = 2 * chunks * nh
    for kv in range(2):
        for h in range(nh):
            for c in range(chunks):
                col = (kv * nh + h) * hd + c * LANES
                o_ref[pl.ds((kv * chunks + c) * nh + h, tm, stride=period), :] = y[:, col:col + LANES]


def memkv(mem, g, w, nh):
    m, d = mem.shape
    n = w.shape[1]
    hd = n // (2 * nh)
    per_tok = n // LANES
    tm = _row_tile(m, 256)
    return pl.pallas_call(
        functools.partial(_memkv_kernel, hd=hd, nh=nh), grid=(m // tm,),
        in_specs=[pl.BlockSpec((tm, d), lambda i: (i, 0)), _full((1, d)), _full(w.shape)],
        out_specs=[pl.BlockSpec((tm * per_tok, LANES), lambda i: (i, 0)), pl.BlockSpec((tm, n), lambda i: (i, 0))],
        out_shape=[jax.ShapeDtypeStruct((m * per_tok, LANES), F32), jax.ShapeDtypeStruct((m, n), BF16)],
        compiler_params=_cparams(("parallel",)), name="memkv",
    )(mem, g.reshape(1, d), w)


def _xattn_core(q, kv, hd):
    nh = q.shape[1] // hd
    outs = []
    for h in range(nh):
        s = _dot_nt(q[:, h * hd:(h + 1) * hd], kv[:, h * hd:(h + 1) * hd])
        e = jnp.exp(s - jnp.max(s, -1, keepdims=True))
        p = e / jnp.sum(e, -1, keepdims=True)
        outs.append(_dot(p.astype(BF16), kv[:, (nh + h) * hd:(nh + h + 1) * hd]))
    return jnp.concatenate(outs, axis=1).astype(BF16)


def _xattn_prompt_kernel(*refs, hd, n_in):
    x_ref, g_ref, wq_ref, kv_ref, wo_ref = refs[:5]
    a_refs = refs[5:5 + n_in]
    w_refs = refs[5 + n_in:5 + 2 * n_in]
    o_ref = refs[5 + 2 * n_in]
    x = x_ref[0]
    for a_ref, w_ref in zip(a_refs, w_refs):
        x = x + _dot(a_ref[0], w_ref[...])
    q = (_dot(_rms(x, g_ref[...]).astype(BF16), wq_ref[...]) * (hd ** -0.5)).astype(BF16)
    o = _xattn_core(q, kv_ref[0], hd)
    o_ref[0] = x + _dot(o, wo_ref[...])


def xattn_prompt(x, g, wq, kvb, wo, b, s, acts=(), ws=()):
    d = x.shape[1]
    mt = kvb.shape[0] // b
    tm = _row_tile(s, 512)
    tile = lambda n: pl.BlockSpec((1, tm, n), lambda bi, i: (bi, i, 0))
    out = pl.pallas_call(
        functools.partial(_xattn_prompt_kernel, hd=d // X_HEADS, n_in=len(acts)), grid=(b, s // tm),
        in_specs=[tile(d), _full((1, d)), _full(wq.shape),
                  pl.BlockSpec((1, mt, kvb.shape[1]), lambda bi, i: (bi, 0, 0)), _full(wo.shape)]
        + [tile(a.shape[1]) for a in acts] + [_full(w.shape) for w in ws],
        out_specs=tile(d),
        out_shape=jax.ShapeDtypeStruct((b, s, d), F32),
        compiler_params=_cparams(("parallel", "parallel"), VMEM_LIMIT), name="xattn_prompt",
    )(x.reshape(b, s, d), g.reshape(1, d), wq, kvb.reshape(b, mt, -1), wo,
      *[a.reshape(b, s, -1) for a in acts], *ws)
    return out.reshape(b * s, d)


def _xattn_sample_kernel(x_ref, g_ref, wq_ref, kv_ref, wo_ref, o_ref, q_scr, a_scr, *, hd, mt):
    bi = pl.program_id(0)
    nb = pl.num_programs(0)
    nh = wq_ref.shape[1] // hd

    @pl.when(bi == 0)
    def _():
        q_scr[...] = _dot(_rms(x_ref[...], g_ref[...]).astype(BF16), wq_ref[...]) * (hd ** -0.5)

    q = jnp.broadcast_to(q_scr[pl.ds(bi, 1), :], (SUBLANES, q_scr.shape[1])).astype(BF16)
    chunks = hd // LANES
    period = 2 * chunks * nh

    def head_rows(kv, h):
        return jnp.concatenate([kv_ref[0, pl.ds((kv * chunks + c) * nh + h, mt, stride=period), :]
                                for c in range(chunks)], axis=1).astype(BF16)

    outs = []
    for h in range(nh):
        s = _dot_nt(q[:, h * hd:(h + 1) * hd], head_rows(0, h))
        e = jnp.exp(s - jnp.max(s, -1, keepdims=True))
        p = e / jnp.sum(e, -1, keepdims=True)
        outs.append(_dot(p.astype(BF16), head_rows(1, h)))
    a_scr[pl.ds(bi, 1), :] = jnp.concatenate(outs, axis=1)[0:1, :]

    @pl.when(bi == nb - 1)
    def _():
        o_ref[...] = x_ref[...] + _dot(a_scr[...].astype(BF16), wo_ref[...])


def xattn_sample(x, g, wq, kv_rows, wo, layer, mt):
    bd, d = x.shape
    rows = kv_rows.shape[1]
    return pl.pallas_call(
        functools.partial(_xattn_sample_kernel, hd=d // X_HEADS, mt=mt), grid=(bd,),
        in_specs=[_full((bd, d)), _full((1, d)), _full(wq.shape),
                  pl.BlockSpec((1, rows, LANES), lambda bi: (layer * bd + bi, 0, 0)), _full(wo.shape)],
        out_specs=_full((bd, d)),
        out_shape=jax.ShapeDtypeStruct((bd, d), F32),
        scratch_shapes=[pltpu.VMEM((bd, wq.shape[1]), F32), pltpu.VMEM((bd, wq.shape[1]), F32)],
        compiler_params=_cparams(("arbitrary",), VMEM_LIMIT), name="xattn_sample",
    )(x, g.reshape(1, d), wq, kv_rows, wo)


def _ffn_kernel(x_ref, g_ref, wg_ref, wu_ref, wd_ref, o_ref, h_scr, acc_scr):
    c = pl.program_id(1)

    @pl.when(c == 0)
    def _():
        h_scr[...] = _rms(x_ref[...], g_ref[...]).astype(BF16)
        acc_scr[...] = x_ref[...]

    h = h_scr[...]
    gate = _dot(h, wg_ref[...])
    up = _dot(h, wu_ref[...])
    act = (gate * jax.nn.sigmoid(gate) * up).astype(BF16)
    acc_scr[...] += _dot(act, wd_ref[...])

    @pl.when(c == pl.num_programs(1) - 1)
    def _():
        o_ref[...] = acc_scr[...]


def _xattn_ffn_kernel(*refs, hd, n_in):
    x_ref, gx_ref, wq_ref, kv_ref, wo_ref, gf_ref, wg_ref, wu_ref, wd_ref = refs[:9]
    a_refs = refs[9:9 + n_in]
    w_refs = refs[9 + n_in:9 + 2 * n_in]
    o_ref, h_scr, acc_scr = refs[9 + 2 * n_in:]
    c = pl.program_id(1)

    @pl.when(c == 0)
    def _():
        x = x_ref[...]
        for a_ref, w_ref in zip(a_refs, w_refs):
            x = x + _dot(a_ref[...], w_ref[...])
        q = (_dot(_rms(x, gx_ref[...]).astype(BF16), wq_ref[...]) * (hd ** -0.5)).astype(BF16)
        x = x + _dot(_xattn_core(q, kv_ref[0], hd), wo_ref[...])
        h_scr[...] = _rms(x, gf_ref[...]).astype(BF16)
        acc_scr[...] = x

    h = h_scr[...]
    gate = _dot(h, wg_ref[...])
    up = _dot(h, wu_ref[...])
    act = (gate * jax.nn.sigmoid(gate) * up).astype(BF16)
    acc_scr[...] += _dot(act, wd_ref[...])

    @pl.when(c == pl.num_programs(1) - 1)
    def _():
        o_ref[...] = acc_scr[...]


def xattn_ffn(x, gx, wq, kvb, wo, acts, ws, gf, w_gu, w_dn, b, s):
    m, d = x.shape
    dff = w_dn.shape[0]
    mt = kvb.shape[0] // b
    tm = _row_tile(s, 512)
    per_seq = s // tm
    fc = _ff_chunk(dff, 1408)
    nch = dff // fc
    row = lambda n: pl.BlockSpec((tm, n), lambda i, c: (i, 0))
    return pl.pallas_call(
        functools.partial(_xattn_ffn_kernel, hd=d // X_HEADS, n_in=len(acts)), grid=(m // tm, nch),
        in_specs=[row(d), _full((1, d)), _full(wq.shape),
                  pl.BlockSpec((1, mt, kvb.shape[1]), lambda i, c: (i // per_seq, 0, 0)), _full(wo.shape),
                  _full((1, d)),
                  pl.BlockSpec((d, fc), lambda i, c: (0, c)),
                  pl.BlockSpec((d, fc), lambda i, c: (0, nch + c)),
                  pl.BlockSpec((fc, d), lambda i, c: (c, 0))]
        + [row(a.shape[1]) for a in acts] + [_full(w.shape) for w in ws],
        out_specs=row(d),
        out_shape=jax.ShapeDtypeStruct((m, d), F32),
        scratch_shapes=[pltpu.VMEM((tm, d), BF16), pltpu.VMEM((tm, d), F32)],
        compiler_params=_cparams(("parallel", "arbitrary"), VMEM_LIMIT), name="xattn_ffn",
    )(x, gx.reshape(1, d), wq, kvb.reshape(b, mt, -1), wo, gf.reshape(1, d), w_gu, w_gu, w_dn, *acts, *ws)


def _ff_chunk(dff, pref):
    c = dff
    for n in range(1, dff // LANES + 1):
        if dff % n == 0 and (dff // n) % LANES == 0 and dff // n <= pref:
            c = dff // n
            break
    return c


def ffn(x, g, w_gu, w_dn):
    m, d = x.shape
    dff = w_dn.shape[0]
    tm = _row_tile(m, 512)
    fc = _ff_chunk(dff, 1408)
    nch = dff // fc
    return pl.pallas_call(
        _ffn_kernel, grid=(m // tm, nch),
        in_specs=[pl.BlockSpec((tm, d), lambda i, c: (i, 0)), _full((1, d)),
                  pl.BlockSpec((d, fc), lambda i, c: (0, c)),
                  pl.BlockSpec((d, fc), lambda i, c: (0, nch + c)),
                  pl.BlockSpec((fc, d), lambda i, c: (c, 0))],
        out_specs=pl.BlockSpec((tm, d), lambda i, c: (i, 0)),
        out_shape=jax.ShapeDtypeStruct((m, d), F32),
        scratch_shapes=[pltpu.VMEM((tm, d), BF16), pltpu.VMEM((tm, d), F32)],
        compiler_params=_cparams(("parallel", "arbitrary"), VMEM_LIMIT), name="ffn",
    )(x, g.reshape(1, d), w_gu, w_gu, w_dn)


def _router_kernel(x_ref, g_ref, w_ref, b_ref, comb_ref, h_ref, mask_ref, cnt_ref, *, ne):
    h = _rms(x_ref[...], g_ref[...]).astype(BF16)
    h_ref[...] = h
    logits = _dot(h, w_ref[...]) + b_ref[...]
    lane = lax.broadcasted_iota(jnp.int32, logits.shape, 1)
    logits = jnp.where(lane < ne, logits, -jnp.inf)
    v1 = jnp.max(logits, -1, keepdims=True)
    i1 = jnp.min(jnp.where(logits == v1, lane, LANES), -1, keepdims=True)
    rest = jnp.where(lane == i1, -jnp.inf, logits)
    v2 = jnp.max(rest, -1, keepdims=True)
    i2 = jnp.min(jnp.where(rest == v2, lane, LANES), -1, keepdims=True)
    e2 = jnp.exp(v2 - v1)
    den = 1.0 + e2
    comb_ref[...] = jnp.where(lane == i1, 1.0 / den, 0.0) + jnp.where(lane == i2, e2 / den, 0.0)
    chosen = jnp.where((lane == i1) | (lane == i2), 1.0, 0.0)
    mask_ref[...] = chosen.astype(BF16)

    @pl.when(pl.program_id(0) == 0)
    def _():
        cnt_ref[...] = jnp.zeros_like(cnt_ref)

    cnt_ref[0:1, :] += jnp.sum(chosen, axis=0, keepdims=True)


def router(x, g, w_r, b_r):
    m, d = x.shape
    ne = w_r.shape[1]
    w_pad = jnp.zeros((d, LANES), BF16).at[:, :ne].set(w_r.astype(BF16))
    b_pad = jnp.zeros((1, LANES), F32).at[0, :ne].set(b_r.astype(F32))
    tm = _row_tile(m, 512)
    row = lambda n: pl.BlockSpec((tm, n), lambda i: (i, 0))
    return pl.pallas_call(
        functools.partial(_router_kernel, ne=ne), grid=(m // tm,),
        in_specs=[row(d), _full((1, d)), _full((d, LANES)), _full((1, LANES))],
        out_specs=[row(LANES), row(d), row(LANES), _full((SUBLANES, LANES))],
        out_shape=[jax.ShapeDtypeStruct((m, LANES), F32), jax.ShapeDtypeStruct((m, d), BF16),
                   jax.ShapeDtypeStruct((m, LANES), BF16), jax.ShapeDtypeStruct((SUBLANES, LANES), F32)],
        compiler_params=_cparams(("arbitrary",)), name="router",
    )(x, g.reshape(1, d), w_pad, b_pad)


def _residual_out(y, gain_ref, norm):
    return _rms(y, gain_ref[...]) if norm else y


def _moe_kernel(x_ref, h_ref, comb_ref, wg_ref, wu_ref, wd_ref, gain_ref, o_ref, acc_scr, *, norm):
    e = pl.program_id(1)

    @pl.when(e == 0)
    def _():
        acc_scr[...] = jnp.zeros_like(acc_scr)

    h = h_ref[...]
    gate = _dot(h, wg_ref[0])
    up = _dot(h, wu_ref[0])
    act = (gate * jax.nn.sigmoid(gate) * up).astype(BF16)
    y = _dot(act, wd_ref[0])
    comb = comb_ref[...]
    lane = lax.broadcasted_iota(jnp.int32, comb.shape, 1)
    acc_scr[...] += jnp.sum(jnp.where(lane == e, comb, 0.0), -1, keepdims=True) * y

    @pl.when(e == pl.num_programs(1) - 1)
    def _():
        o_ref[...] = _residual_out(x_ref[...] + acc_scr[...], gain_ref, norm)


def moe(x, h, comb, w_gu, w_dn, final_g=None):
    m, d = x.shape
    ne, dfe = w_dn.shape[:2]
    tm = _row_tile(m, 512)
    gain = jnp.ones((1, d), F32) if final_g is None else final_g.reshape(1, d)
    return pl.pallas_call(
        functools.partial(_moe_kernel, norm=final_g is not None), grid=(m // tm, ne),
        in_specs=[pl.BlockSpec((tm, d), lambda i, e: (i, 0)), pl.BlockSpec((tm, d), lambda i, e: (i, 0)),
                  pl.BlockSpec((tm, LANES), lambda i, e: (i, 0)),
                  pl.BlockSpec((1, d, dfe), lambda i, e: (e, 0, 0)),
                  pl.BlockSpec((1, d, dfe), lambda i, e: (e, 0, 1)),
                  pl.BlockSpec((1, dfe, d), lambda i, e: (e, 0, 0)), _full((1, d))],
        out_specs=pl.BlockSpec((tm, d), lambda i, e: (i, 0)),
        out_shape=jax.ShapeDtypeStruct((m, d), F32),
        scratch_shapes=[pltpu.VMEM((tm, d), F32)],
        compiler_params=_cparams(("parallel", "arbitrary"), VMEM_LIMIT), name="moe",
    )(x, h, comb, w_gu, w_gu, w_dn, gain)


MOE_ROWS = 1024
MOE_CAP = 288


def _moe_compact_kernel(x_ref, h_ref, comb_ref, mask_ref, upper_ref, wg_ref, wu_ref, wd_ref, gain_ref, o_ref,
                        acc_scr, rank_scr, mask_scr, *, cap, norm):
    e = pl.program_id(1)
    tm = x_ref.shape[0]

    @pl.when(e == 0)
    def _():
        acc_scr[...] = jnp.zeros_like(acc_scr)
        mask_t = mask_ref[...].astype(F32).T
        mask_scr[...] = mask_t
        rank_scr[...] = _dot(mask_t.astype(BF16), upper_ref[...])

    rank = rank_scr[pl.ds(e, 1), :]
    routed = mask_scr[pl.ds(e, 1), :] > 0.5
    count = jnp.sum(jnp.where(routed, 1.0, 0.0), -1, keepdims=True)[0, 0].astype(jnp.int32)
    comb = comb_ref[...]
    lane = lax.broadcasted_iota(jnp.int32, comb.shape, 1)
    weight = jnp.sum(jnp.where(lane == e, comb, 0.0), -1, keepdims=True)
    row = lax.broadcasted_iota(jnp.int32, (cap, 1), 0).astype(F32)

    def chunk(c, carry):
        onehot = jnp.where(routed & (rank == row + (c * cap).astype(F32)), 1.0, 0.0).astype(BF16)
        xg = _dot(onehot, h_ref[...]).astype(BF16)
        gate = _dot(xg, wg_ref[0])
        up = _dot(xg, wu_ref[0])
        act = (gate * jax.nn.sigmoid(gate) * up).astype(BF16)
        y = _dot(act, wd_ref[0])
        y_hi = y.astype(BF16)
        y_lo = (y - y_hi.astype(F32)).astype(BF16)
        back = _dot_tn(onehot, jnp.concatenate([y_hi, y_lo], axis=1))
        d = y.shape[1]
        acc_scr[...] += weight * (back[:, 0:d] + back[:, d:2 * d])
        return carry

    lax.fori_loop(0, (count + cap - 1) // cap, chunk, 0)

    @pl.when(e == pl.num_programs(1) - 1)
    def _():
        o_ref[...] = _residual_out(x_ref[...] + acc_scr[...], gain_ref, norm)


def moe_compact(x, h, comb, mask, w_gu, w_dn, final_g=None):
    m, d = x.shape
    ne, dfe = w_dn.shape[:2]
    tm = _row_tile(m, MOE_ROWS)
    cap = min(MOE_CAP, tm)
    idx = lax.broadcasted_iota(jnp.int32, (tm, tm), 0)
    upper = jnp.where(idx < lax.broadcasted_iota(jnp.int32, (tm, tm), 1), 1.0, 0.0).astype(BF16)
    gain = jnp.ones((1, d), F32) if final_g is None else final_g.reshape(1, d)
    row = lambda n: pl.BlockSpec((tm, n), lambda i, e: (i, 0))
    return pl.pallas_call(
        functools.partial(_moe_compact_kernel, cap=cap, norm=final_g is not None), grid=(m // tm, ne),
        in_specs=[row(d), row(d), row(LANES), row(LANES), _full((tm, tm)),
                  pl.BlockSpec((1, d, dfe), lambda i, e: (e, 0, 0)),
                  pl.BlockSpec((1, d, dfe), lambda i, e: (e, 0, 1)),
                  pl.BlockSpec((1, dfe, d), lambda i, e: (e, 0, 0)), _full((1, d))],
        out_specs=row(d),
        out_shape=jax.ShapeDtypeStruct((m, d), F32),
        scratch_shapes=[pltpu.VMEM((tm, d), F32), pltpu.VMEM((LANES, tm), F32), pltpu.VMEM((LANES, tm), F32)],
        compiler_params=_cparams(("parallel", "arbitrary"), VMEM_LIMIT), name="moe_compact",
    )(x, h, comb, mask, upper, w_gu, w_gu, w_dn, gain)


MOE_TILE = 256


def _moe_pos_kernel(mask_ref, comb_ref, tri_ref, base_ref, post_ref, pos2_ref, wab_ref, stab_ref, run_scr, *, nep):
    sb = pl.program_id(0)
    nb = pl.num_programs(0)

    @pl.when(sb == 0)
    def _():
        run_scr[...] = jnp.zeros_like(run_scr)
        stab_ref[...] = jnp.zeros_like(stab_ref)

    a = mask_ref[...]
    af = a.astype(F32)
    start = base_ref[...] + run_scr[...]
    stab_ref[pl.ds(sb, 1), :] = start.astype(jnp.int32)
    rank = _dot(tri_ref[...], a)
    pos = jnp.where(af > 0, start + rank, -1.0)
    lane = lax.broadcasted_iota(jnp.int32, pos.shape, 1)
    first_e = jnp.min(jnp.where(af > 0, lane, LANES), -1, keepdims=True)
    last_e = jnp.max(jnp.where(af > 0, lane, -1), -1, keepdims=True)
    comb = comb_ref[...]
    w_a = jnp.sum(jnp.where(lane == first_e, comb, 0.0), -1, keepdims=True)
    w_b = jnp.sum(jnp.where(lane == last_e, comb, 0.0), -1, keepdims=True)
    wab_ref[...] = jnp.where(lane == 0, w_a, jnp.where(lane == 1, w_b, 0.0))
    pos_t = pos.T[0:nep, :]
    post_ref[0] = pos_t.astype(jnp.int32)
    row = lax.broadcasted_iota(jnp.int32, pos_t.shape, 0)
    first_r = jnp.min(jnp.where(pos_t >= 0, row, nep), 0, keepdims=True)
    last_r = jnp.max(jnp.where(pos_t >= 0, row, -1), 0, keepdims=True)
    pos_a = jnp.sum(jnp.where(row == first_r, pos_t, 0.0), 0, keepdims=True)
    pos_b = jnp.sum(jnp.where(row == last_r, pos_t, 0.0), 0, keepdims=True)
    pos2_ref[0] = jnp.where(row == 0, pos_a, jnp.where(row == 1, pos_b, 0.0)).astype(jnp.int32)
    run_scr[...] += jnp.sum(af, axis=0, keepdims=True)

    @pl.when(sb == nb - 1)
    def _():
        stab_ref[pl.ds(nb, 1), :] = (base_ref[...] + run_scr[...]).astype(jnp.int32)


def _moe_expert_kernel(te_ref, lo_ref, cnt_ref, ring_ref, nt_ref, h_ref, post_ref, wg_ref, wu_ref, wd_ref, y_ref,
                       hbuf, xg_scr, sem, *, t):
    i = pl.program_id(0)

    @pl.when(i >= nt_ref[0])
    def _():
        y_ref[...] = jnp.zeros_like(y_ref)

    nbuf = hbuf.shape[0]
    ahead = nbuf - 1

    def copy(sb, slot):
        return pltpu.make_async_copy(h_ref.at[pl.ds(pl.multiple_of(sb * t, t), t), :], hbuf.at[slot], sem.at[slot])

    def start_first(tile):
        for j in range(ahead):
            @pl.when(j < cnt_ref[tile])
            def _():
                copy(lo_ref[tile] + j, (ring_ref[tile] + j) % nbuf).start()

    @pl.when(i == 0)
    def _():
        start_first(0)

    @pl.when(i < nt_ref[0])
    def _():
        e = te_ref[i]
        lo = lo_ref[i]
        n = cnt_ref[i]
        ring = ring_ref[i]
        xg_scr[...] = jnp.zeros_like(xg_scr)
        row = i * t + lax.broadcasted_iota(jnp.int32, (t, 1), 0)

        def body(j, c):
            slot = (ring + j) % nbuf
            copy(lo + j, slot).wait()

            @pl.when(j + ahead < n)
            def _():
                copy(lo + j + ahead, (ring + j + ahead) % nbuf).start()

            src_pos = post_ref[lo + j, pl.ds(e, 1), :]
            onehot = jnp.where(src_pos == row, 1.0, 0.0).astype(BF16)
            xg_scr[...] += _dot(onehot, hbuf[slot])
            return c

        lax.fori_loop(0, n, body, 0)

        @pl.when(i + 1 < nt_ref[0])
        def _():
            start_first(i + 1)

        x = xg_scr[...].astype(BF16)
        gate = _dot(x, wg_ref[0])
        up = _dot(x, wu_ref[0])
        act = (gate * jax.nn.sigmoid(gate) * up).astype(BF16)
        y_ref[...] = _dot(act, wd_ref[0])


def _moe_combine_kernel(x_ref, wab_ref, pos_ref, nxt_ref, gain_ref, ys_ref, o_ref, ybuf, sem, *, t, norm):
    sb = pl.program_id(0)
    nb = pl.num_programs(0)

    def row_copy(p_ref, which, slot, tok):
        return pltpu.make_async_copy(ys_ref.at[pl.ds(p_ref[0, which, tok], 1), :],
                                     ybuf.at[slot, which, pl.ds(tok, 1), :], sem.at[slot])

    def start_all(p_ref, slot):
        def body(tok, c):
            row_copy(p_ref, 0, slot, tok).start()
            row_copy(p_ref, 1, slot, tok).start()
            return c
        lax.fori_loop(0, t, body, 0, unroll=8)

    @pl.when(sb == 0)
    def _():
        start_all(pos_ref, 0)

    @pl.when(sb + 1 < nb)
    def _():
        start_all(nxt_ref, (sb + 1) % 2)

    slot = sb % 2

    def wait_body(tok, c):
        row_copy(pos_ref, 0, slot, tok).wait()
        row_copy(pos_ref, 1, slot, tok).wait()
        return c
    lax.fori_loop(0, t, wait_body, 0, unroll=8)
    w = wab_ref[...]
    y = x_ref[...] + w[:, 0:1] * ybuf[slot, 0] + w[:, 1:2] * ybuf[slot, 1]
    o_ref[...] = _residual_out(y, gain_ref, norm)


def moe_grouped(x, h, comb, mask, counts, w_gu, w_dn, final_g=None):
    m, d = x.shape
    ne, dfe = w_dn.shape[:2]
    t = MOE_TILE
    assert m % t == 0
    nb = m // t
    nep = _round_up(ne, SUBLANES)
    nbp = _round_up(nb + 1, SUBLANES)
    k_top = TOP_K
    nt_max = k_top * m // t + ne
    cnt = counts[0, :ne].astype(jnp.int32)
    cnt_pad = (cnt + t - 1) // t * t
    ends = jnp.cumsum(cnt_pad)
    base = ends - cnt_pad
    base_row = jnp.zeros((1, LANES), F32).at[0, :ne].set(base.astype(F32))
    idx = lax.broadcasted_iota(jnp.int32, (t, t), 0)
    tri = jnp.where(lax.broadcasted_iota(jnp.int32, (t, t), 1) < idx, 1.0, 0.0).astype(BF16)
    blk = lambda n2: pl.BlockSpec((t, n2), lambda i: (i, 0))
    post, pos2, wab, stab = pl.pallas_call(
        functools.partial(_moe_pos_kernel, nep=nep), grid=(nb,),
        in_specs=[blk(LANES), blk(LANES), _full((t, t)), _full((1, LANES))],
        out_specs=[pl.BlockSpec((1, nep, t), lambda i: (i, 0, 0)), pl.BlockSpec((1, nep, t), lambda i: (i, 0, 0)),
                   blk(LANES), _full((nbp, LANES))],
        out_shape=[jax.ShapeDtypeStruct((nb, nep, t), jnp.int32), jax.ShapeDtypeStruct((nb, nep, t), jnp.int32),
                   jax.ShapeDtypeStruct((m, LANES), F32), jax.ShapeDtypeStruct((nbp, LANES), jnp.int32)],
        scratch_shapes=[pltpu.VMEM((1, LANES), F32)],
        compiler_params=_cparams(("arbitrary",)), name="moe_positions",
    )(mask, comb, tri, base_row)
    r0 = jnp.arange(nt_max, dtype=jnp.int32) * t
    tile_e = jnp.minimum(jnp.sum(ends[None, :] <= r0[:, None], axis=1), ne - 1).astype(jnp.int32)
    n_tiles = (ends[-1] // t).astype(jnp.int32).reshape(1)
    s_e = stab[:nb + 1, :ne][:, tile_e]
    lo = jnp.sum(s_e[1:] <= r0[None, :], axis=0)
    hi = jnp.sum(s_e[:nb] < r0[None, :] + t, axis=0) - 1
    lo = jnp.clip(lo, 0, nb - 1).astype(jnp.int32)
    hi = jnp.clip(hi, lo, nb - 1).astype(jnp.int32)
    n_src = jnp.where(jnp.arange(nt_max) < n_tiles[0], hi - lo + 1, 0).astype(jnp.int32)
    n_ring = 6
    ring = ((jnp.cumsum(n_src) - n_src) % n_ring).astype(jnp.int32)
    w_spec = lambda shape, col: pl.BlockSpec(shape, lambda i, te, *_: (te[i], 0, col))
    grid_spec = pltpu.PrefetchScalarGridSpec(
        num_scalar_prefetch=5, grid=(nt_max,),
        in_specs=[pl.BlockSpec(memory_space=pl.ANY),
                  pl.BlockSpec((nb, nep, t), lambda i, *_: (0, 0, 0)),
                  w_spec((1, d, dfe), 0), w_spec((1, d, dfe), 1), w_spec((1, dfe, d), 0)],
        out_specs=pl.BlockSpec((t, d), lambda i, *_: (i, 0)),
        scratch_shapes=[pltpu.VMEM((n_ring, t, d), BF16), pltpu.VMEM((t, d), F32),
                        pltpu.SemaphoreType.DMA((n_ring,))])
    ys = pl.pallas_call(
        functools.partial(_moe_expert_kernel, t=t), grid_spec=grid_spec,
        out_shape=jax.ShapeDtypeStruct((nt_max * t, d), F32),
        compiler_params=_cparams(("arbitrary",), VMEM_LIMIT), name="moe_experts",
    )(tile_e, lo, n_src, ring, n_tiles, h, post, w_gu, w_gu, w_dn)
    smem_blk = lambda f: pl.BlockSpec((1, nep, t), f, memory_space=pltpu.SMEM)
    gain = jnp.ones((1, d), F32) if final_g is None else final_g.reshape(1, d)
    return pl.pallas_call(
        functools.partial(_moe_combine_kernel, t=t, norm=final_g is not None), grid=(nb,),
        in_specs=[blk(d), blk(LANES), smem_blk(lambda i: (i, 0, 0)),
                  smem_blk(lambda i: (jnp.minimum(i + 1, nb - 1), 0, 0)), _full((1, d)),
                  pl.BlockSpec(memory_space=pl.ANY)],
        out_specs=blk(d),
        out_shape=jax.ShapeDtypeStruct((m, d), F32),
        scratch_shapes=[pltpu.VMEM((2, 2, t, d), F32), pltpu.SemaphoreType.DMA((2,))],
        compiler_params=_cparams(("arbitrary",), VMEM_LIMIT), name="moe_combine",
    )(x, wab, pos2, pos2, gain, ys)


def _inproj_odd_kernel(x_ref, g_ref, w_ref, bias_ref, q_ref, k_ref, v_ref, og_ref, gc_ref, gr_ref, *, hq, hv, nh):
    xn = _rms(x_ref[...], g_ref[...]).astype(BF16)

    def mm(lo, hi):
        return _dot(xn, w_ref[:, lo:hi])

    dk = hq // nh
    q_ref[...] = mm(0, hq).astype(BF16)
    k_ref[...] = (mm(hq, 2 * hq) * (dk ** -0.5)).astype(BF16)
    v_ref[...] = mm(2 * hq, 2 * hq + hv).astype(BF16)
    og_ref[...] = jax.nn.sigmoid(mm(2 * hq + hv, 2 * hq + 2 * hv))
    gi = mm(2 * hq + 2 * hv, 2 * hq + 2 * hv + LANES) + bias_ref[...]
    lane = lax.broadcasted_iota(jnp.int32, gi.shape, 1)
    gates = jnp.where(lane < nh, gi, jax.nn.log_sigmoid(gi))
    gc_ref[...] = gates
    gr_ref[...] = gates.T[0:SUBLANES, :]


def inproj_odd(x, g, w_pad, gate_bias):
    m, d = x.shape
    nh = MLSTM_HEADS
    hq = hv = d
    tm = _row_tile(m, 256)
    row = lambda n: pl.BlockSpec((tm, n), lambda i: (i, 0))
    outs = [(hq, BF16), (hq, BF16), (hv, BF16), (hv, F32), (LANES, F32)]
    if tm % LANES:
        gr_spec = _full((SUBLANES, m))
    else:
        gr_spec = pl.BlockSpec((SUBLANES, tm), lambda i: (0, i))
    return pl.pallas_call(
        functools.partial(_inproj_odd_kernel, hq=hq, hv=hv, nh=nh), grid=(m // tm,),
        in_specs=[row(d), _full((1, d)), _full(w_pad.shape), _full((1, LANES))],
        out_specs=[row(n) for n, _ in outs] + [gr_spec],
        out_shape=[jax.ShapeDtypeStruct((m, n), t) for n, t in outs] + [jax.ShapeDtypeStruct((SUBLANES, m), F32)],
        compiler_params=_cparams(("parallel",), VMEM_LIMIT), name="inproj_odd",
    )(x, g.reshape(1, d), w_pad, gate_bias)


def prep_odd(w_in, b_i, b_f):
    d, n = w_in.shape
    n_pad = _round_up(n - 2 * MLSTM_HEADS, LANES) + LANES
    w_pad = jnp.zeros((d, n_pad), BF16).at[:, :n].set(w_in.astype(BF16))
    bias = jnp.zeros((1, LANES), F32).at[0, :2 * MLSTM_HEADS].set(jnp.concatenate([b_i, b_f]).astype(F32))
    return dict(w_in=w_pad, bias=bias)


def _mlstm_prompt_kernel(q_ref, k_ref, v_ref, og_ref, gc_ref, gr_ref, gain_ref, hn_ref, c_ref, n_ref, m_ref,
                         *, nh, dk, dv, ln):
    ci = pl.program_id(1)

    @pl.when(ci == 0)
    def _():
        c_ref[...] = jnp.zeros_like(c_ref)
        n_ref[...] = jnp.zeros_like(n_ref)
        m_ref[...] = jnp.full(m_ref.shape, NEG, F32)

    row = lax.broadcasted_iota(jnp.int32, (ln, ln), 0)
    col = lax.broadcasted_iota(jnp.int32, (ln, ln), 1)
    tri = row >= col
    gc = gc_ref[...]
    gr = gr_ref[...]
    for h in range(nh):
        q = q_ref[:, h * dk:(h + 1) * dk]
        k = k_ref[:, h * dk:(h + 1) * dk]
        v = v_ref[:, h * dv:(h + 1) * dv]
        ig_c, lf_c = gc[:, h:h + 1], gc[:, nh + h:nh + h + 1]
        ig_r, lf_r = gr[h:h + 1, :], gr[nh + h:nh + h + 1, :]
        b_c = jnp.sum(jnp.where(tri, lf_r, 0.0), axis=1, keepdims=True)
        b_r = jnp.sum(jnp.where(row <= col, lf_c, 0.0), axis=0, keepdims=True)
        m_prev = m_ref[0, h:h + 1, 0:1]
        c_prev = c_ref[0, h]
        n_prev = n_ref[0, h:h + 1, :]
        dmat = jnp.where(tri, b_c - b_r + ig_r, NEG)
        inter = b_c + m_prev
        mt = jnp.maximum(inter, jnp.max(dmat, -1, keepdims=True))
        wm = jnp.exp(dmat - mt)
        a = jnp.exp(inter - mt)
        wqk = wm * _dot_nt(q, k)
        num = a * _dot_nt(q, c_prev.astype(BF16)) + _dot(wqk.astype(BF16), v)
        den = a * jnp.sum(q.astype(F32) * n_prev, -1, keepdims=True) + jnp.sum(wqk, -1, keepdims=True)
        hh = num / jnp.maximum(jnp.abs(den), jnp.exp(-mt))
        b_end = b_c[ln - 1:ln, :]
        m_new = mt[ln - 1:ln, :]
        a_end = jnp.exp(b_end + m_prev - m_new)
        w_s = jnp.exp(b_end - b_c + ig_c - m_new)
        c_ref[0, h] = a_end * c_prev + _dot_tn((v.astype(F32) * w_s).astype(BF16), k)
        n_ref[0, h:h + 1, :] = a_end * n_prev + jnp.sum(w_s * k.astype(F32), axis=0, keepdims=True)
        m_ref[0, h:h + 1, :] = jnp.broadcast_to(m_new, (1, m_ref.shape[2]))
        hn = hh * lax.rsqrt(jnp.mean(hh * hh, -1, keepdims=True) + RMS_EPS)
        hn = hn * gain_ref[:, h * dv:(h + 1) * dv] * og_ref[:, h * dv:(h + 1) * dv]
        hn_ref[:, h * dv:(h + 1) * dv] = hn.astype(hn_ref.dtype)


def mlstm_prompt(q, k, v, og, gc, gr, gain, b, s):
    m, d = q.shape
    nh = MLSTM_HEADS
    dk = dv = d // nh
    ln = _row_tile(s, MLSTM_CHUNK)
    nch = s // ln
    row = lambda n: pl.BlockSpec((ln, n), lambda bi, ci: (bi * nch + ci, 0))
    return pl.pallas_call(
        functools.partial(_mlstm_prompt_kernel, nh=nh, dk=dk, dv=dv, ln=ln), grid=(b, nch),
        in_specs=[row(d), row(d), row(d), row(d), row(LANES),
                  pl.BlockSpec((SUBLANES, ln), lambda bi, ci: (0, bi * nch + ci)), _full((1, d))],
        out_specs=[row(d), pl.BlockSpec((1, nh, dv, dk), lambda bi, ci: (bi, 0, 0, 0)),
                   pl.BlockSpec((1, nh, dk), lambda bi, ci: (bi, 0, 0)),
                   pl.BlockSpec((1, nh, LANES), lambda bi, ci: (bi, 0, 0))],
        out_shape=[jax.ShapeDtypeStruct((m, d), BF16), jax.ShapeDtypeStruct((b, nh, dv, dk), F32),
                   jax.ShapeDtypeStruct((b, nh, dk), F32), jax.ShapeDtypeStruct((b, nh, LANES), F32)],
        compiler_params=_cparams(("parallel", "arbitrary"), VMEM_LIMIT), name="mlstm_prompt",
    )(q, k, v, og, gc, gr, gain.reshape(1, d))


def _mlstm_sample_kernel(q_ref, k_ref, v_ref, og_ref, g_ref, gain_ref, c_ref, n_ref, m_ref,
                         hn_ref, co_ref, no_ref, mo_ref, *, nh):
    row = lax.broadcasted_iota(jnp.int32, (SUBLANES, 1), 0)
    for h in range(nh):
        q = q_ref[0, h:h + 1, :]
        k = k_ref[0, h:h + 1, :]
        v = v_ref[0, h:h + 1, :].astype(F32)
        ig = g_ref[0, h:h + 1, 0:1]
        lf = g_ref[0, h:h + 1, 1:2]
        m_prev = m_ref[0, h:h + 1, :]
        c_prev = c_ref[0, h]
        n_prev = n_ref[0, h:h + 1, :]
        inter = lf + m_prev
        mt = jnp.maximum(inter, ig)
        wm = jnp.exp(ig - mt)
        a = jnp.exp(inter - mt)
        q8 = jnp.broadcast_to(q, (SUBLANES, q.shape[1]))
        cq = _dot_nt(q8, c_prev.astype(BF16))[0:1, :]
        wqk = wm * jnp.sum(q.astype(F32) * k.astype(F32), -1, keepdims=True)
        num = a * cq + wqk * v
        den = a * jnp.sum(n_prev * q.astype(F32), -1, keepdims=True) + wqk
        hh = num / jnp.maximum(jnp.abs(den), jnp.exp(-mt))
        v8 = jnp.where(row == 0, jnp.broadcast_to(v * wm, (SUBLANES, v.shape[1])), 0.0).astype(BF16)
        k8 = jnp.broadcast_to(k, (SUBLANES, k.shape[1]))
        co_ref[0, h] = a * c_prev + _dot_tn(v8, k8)
        no_ref[0, h:h + 1, :] = a * n_prev + wm * k.astype(F32)
        mo_ref[0, h:h + 1, :] = mt
        hn = hh * lax.rsqrt(jnp.mean(hh * hh, -1, keepdims=True) + RMS_EPS)
        hn_ref[0, h:h + 1, :] = (hn * gain_ref[h:h + 1, :] * og_ref[0, h:h + 1, :]).astype(hn_ref.dtype)


def mlstm_sample(q, k, v, og, gc, gain, c, n, m):
    bd, d = q.shape
    nh = MLSTM_HEADS
    dk = d // nh
    heads = lambda a: a.reshape(bd, nh, dk)
    g2 = jnp.transpose(gc[:, :2 * nh].reshape(bd, 2, nh), (0, 2, 1))
    blk3 = lambda n2: pl.BlockSpec((1, nh, n2), lambda bi: (bi, 0, 0))
    cspec = pl.BlockSpec((1, nh, dk, dk), lambda bi: (bi, 0, 0, 0))
    hn, co, no, mo = pl.pallas_call(
        functools.partial(_mlstm_sample_kernel, nh=nh), grid=(bd,),
        in_specs=[blk3(dk), blk3(dk), blk3(dk), blk3(dk), blk3(2), _full((nh, dk)), cspec, blk3(dk), blk3(1)],
        out_specs=[blk3(dk), cspec, blk3(dk), blk3(1)],
        out_shape=[jax.ShapeDtypeStruct((bd, nh, dk), BF16), jax.ShapeDtypeStruct(c.shape, F32),
                   jax.ShapeDtypeStruct(n.shape, F32), jax.ShapeDtypeStruct((bd, nh, 1), F32)],
        compiler_params=_cparams(("parallel",)), name="mlstm_sample",
    )(heads(q), heads(k), heads(v), heads(og), g2, gain.reshape(nh, dk), c, n, m.reshape(bd, nh, 1))
    return hn.reshape(bd, d), co, no, mo.reshape(bd, nh)


def _row_to_col(row):
    n = row.shape[1]
    eye = lax.broadcasted_iota(jnp.int32, (n, n), 0) == lax.broadcasted_iota(jnp.int32, (n, n), 1)
    return jnp.sum(jnp.where(eye, row, 0.0), axis=1, keepdims=True)


def _head_pad(q, keep):
    q2 = jnp.concatenate([q] * NSA_KV_HEADS, axis=1)
    row = lax.broadcasted_iota(jnp.int32, q2.shape, 0)
    lane = lax.broadcasted_iota(jnp.int32, q2.shape, 1)
    return jnp.where((lane // NSA_HD == row // NSA_GROUP) & keep(row), q2, jnp.zeros_like(q2))


def _nsa_sample_cmp_kernel(pt_ref, q_ref, pages_ref, pe_ref, w_ref, biasc_ref, oc_ref, idx_ref, xbuf, xrow, sem,
                           *, n_pages, nc, ncp, nsb, n_sel, past):
    b = pl.program_id(0)
    nb = pl.num_programs(0)
    kvd = NSA_KV_HEADS * NSA_HD

    def page_copy(bb, slot, p, sl):
        return pltpu.make_async_copy(pages_ref.at[pt_ref[bb * n_pages + p], pl.ds(sl * kvd, kvd), :],
                                     xbuf.at[slot, sl, p], sem.at[slot])

    def start_all(bb, slot):
        def body(p, c):
            page_copy(bb, slot, p, 0).start()
            page_copy(bb, slot, p, 1).start()
            return c
        lax.fori_loop(0, n_pages, body, 0)

    @pl.when(b == 0)
    def _():
        start_all(0, 0)

    @pl.when(b + 1 < nb)
    def _():
        start_all(b + 1, (b + 1) % 2)

    slot = b % 2

    def wait_body(p, c):
        page_copy(b, slot, p, 0).wait()
        page_copy(b, slot, p, 1).wait()
        return c
    lax.fori_loop(0, n_pages, wait_body, 0)

    def file_page(p, c):
        for sl in range(2):
            _file_rows(xbuf[slot, sl, p], xrow, sl, p)
        return c
    lax.fori_loop(0, n_pages, file_page, 0, unroll=8)

    acc = _compress_filed(xrow, pe_ref, w_ref, nc)
    kc = acc[:, 0:kvd].astype(BF16)
    vc = acc[:, kvd:2 * kvd].astype(BF16)
    q = q_ref[0]
    nh = q.shape[0]
    qpad = _head_pad(q, lambda r: r >= 0)
    s = _dot_nt(qpad, kc)
    s = s + biasc_ref[:, 0:nc]
    e = jnp.exp(s - jnp.max(s, -1, keepdims=True))
    p_c = e / jnp.sum(e, -1, keepdims=True)
    o = _dot(p_c.astype(BF16), vc)
    row = lax.broadcasted_iota(jnp.int32, (nh, NSA_HD), 0)
    o_h = o[:, 0:NSA_HD]
    for k in range(1, NSA_KV_HEADS):
        o_h = jnp.where(row // NSA_GROUP == k, o[:, k * NSA_HD:(k + 1) * NSA_HD], o_h)
    oc_ref[0] = o_h
    prow = lax.broadcasted_iota(jnp.int32, p_c.shape, 0)
    lane = lax.broadcasted_iota(jnp.int32, (1, ncp), 1)
    blk = lane // 2
    cur = past // SEL_BLOCK
    forced = (blk == 0) | (blk == cur) | (blk == cur - 1)
    is_cand = ((lane % 2) == 0) & (lane < 2 * nsb)
    nselp = idx_ref.shape[1]
    rsel = lax.broadcasted_iota(jnp.int32, (nselp, 1), 0).astype(F32)
    out_lane = lax.broadcasted_iota(jnp.int32, (nselp, LANES), 1)
    result = jnp.full((nselp, LANES), -1, jnp.int32)
    for k in range(NSA_KV_HEADS):
        imp = jnp.sum(jnp.where(prow // NSA_GROUP == k, p_c, 0.0), axis=0, keepdims=True)
        imp = jnp.concatenate([imp, jnp.zeros((1, ncp - nc), F32)], axis=1)
        imp = _pair_sum(imp)
        score = jnp.where(forced, FORCE_SCORE, jnp.where(blk <= cur, imp, -1.0))
        score = jnp.where(is_cand, score, -2.0)
        sel, rank = _select_blocks(score, n_sel, nsb)
        hit = (rank == rsel) & (sel > 0.5)
        idx = jnp.sum(jnp.where(hit, (blk + 1).astype(F32), 0.0), axis=1, keepdims=True) - 1.0
        result = jnp.where(out_lane == k, idx.astype(jnp.int32), result)
    idx_ref[0] = result


def nsa_sample_cmp(q8, pages, page_table, pe2, wbd, rel_bias):
    bd, nh, hd = q8.shape
    n_pages = page_table.shape[1]
    past = n_pages * PAGE_SIZE
    nc = past // CMP_BLOCK
    nsb = -(-(past + 1) // SEL_BLOCK)
    n_sel = min(TOP_N, nsb)
    ncp = _round_up(max(nc, SEL_RATIO * nsb), LANES)
    nselp = _round_up(n_sel, SUBLANES)
    kvd = wbd.shape[-1]
    cend = jnp.arange(nc) * CMP_BLOCK + (CMP_BLOCK - 1)
    biasc = jnp.zeros((nh, ncp), F32).at[:, :nc].set(_bias_of(rel_bias, past - cend).T)
    grid_spec = pltpu.PrefetchScalarGridSpec(
        num_scalar_prefetch=1, grid=(bd,),
        in_specs=[pl.BlockSpec((1, nh, hd), lambda bi, pt: (bi, 0, 0)),
                  pl.BlockSpec(memory_space=pl.ANY),
                  pl.BlockSpec(pe2.shape, lambda bi, pt: (0, 0, 0)),
                  pl.BlockSpec(wbd.shape, lambda bi, pt: (0, 0, 0, 0)),
                  pl.BlockSpec((nh, ncp), lambda bi, pt: (0, 0))],
        out_specs=[pl.BlockSpec((1, nh, hd), lambda bi, pt: (bi, 0, 0)),
                   pl.BlockSpec((1, nselp, LANES), lambda bi, pt: (bi, 0, 0))],
        scratch_shapes=[pltpu.VMEM((2, 2, n_pages, kvd, PAGE_SIZE), F32),
                        pltpu.VMEM((2, CMP_BLOCK // SUBLANES, nc * SUBLANES, kvd), F32),
                        pltpu.SemaphoreType.DMA((2,))])
    oc, idx = pl.pallas_call(
        functools.partial(_nsa_sample_cmp_kernel, n_pages=n_pages, nc=nc, ncp=ncp, nsb=nsb, n_sel=n_sel, past=past),
        grid_spec=grid_spec,
        out_shape=[jax.ShapeDtypeStruct((bd, nh, hd), F32), jax.ShapeDtypeStruct((bd, nselp, LANES), jnp.int32)],
        compiler_params=_cparams(("arbitrary",), VMEM_LIMIT), name="nsa_sample_cmp",
    )(page_table.reshape(-1), q8, pages, pe2, wbd, biasc)
    sel_idx = jnp.transpose(idx[:, :n_sel, :NSA_KV_HEADS], (0, 2, 1))
    return oc, sel_idx


def _nsa_sample_att_kernel(pt_ref, si_ref, q_ref, g_ref, oc_ref, kvn_ref, wn_ref, wc_ref, pages_ref,
                           bsel_ref, bwin_ref, ob_ref, win_ref, selbuf, wall, sem,
                           *, n_pages, n_sel, past, wb):
    b = pl.program_id(0)
    nb = pl.num_programs(0)
    kvd = NSA_KV_HEADS * NSA_HD
    hd = NSA_HD
    n_blk_pages = past // SEL_BLOCK
    per_page = PAGE_SIZE // SEL_BLOCK
    n_slots = NSA_KV_HEADS * n_sel

    def blk_of(bb, j):
        return si_ref[bb * n_slots + j]

    def blk_copy(bb, slot, j):
        blk = jnp.clip(blk_of(bb, j), 0, n_blk_pages - 1)
        page = pt_ref[bb * n_pages + blk // per_page]
        return pltpu.make_async_copy(pages_ref.at[page, pl.ds(2 * kvd, 2 * kvd), :], selbuf.at[slot, j], sem.at[slot])

    def in_pages(bb, j):
        blk = blk_of(bb, j)
        return (blk >= 0) & (blk < n_blk_pages)

    def start_all(bb, slot):
        def body(j, c):
            @pl.when(in_pages(bb, j))
            def _():
                blk_copy(bb, slot, j).start()
            return c
        lax.fori_loop(0, n_slots, body, 0)

    @pl.when(b == 0)
    def _():
        start_all(0, 0)

    @pl.when(b + 1 < nb)
    def _():
        start_all(b + 1, (b + 1) % 2)

    slot = b % 2
    new_sel = _row_to_col(kvn_ref[0][:, 2 * kvd:4 * kvd])
    lane = lax.broadcasted_iota(jnp.int32, (1, PAGE_SIZE), 1)

    def wait_body(j, c):
        @pl.when(in_pages(b, j))
        def _():
            blk_copy(b, slot, j).wait()

        @pl.when(jnp.logical_not(in_pages(b, j)))
        def _():
            is_new = blk_of(b, j) == n_blk_pages
            selbuf[slot, j] = jnp.where((lane == 0) & is_new, new_sel, 0.0)
        return c
    lax.fori_loop(0, n_slots, wait_body, 0)

    q = q_ref[0]
    nh = q.shape[0]
    gates = g_ref[0]
    head = lax.broadcasted_iota(jnp.int32, (nh, 1), 0)

    def attend(qp, keys_t, vals_t, bias, valid):
        s = jnp.where(valid, _dot(qp, keys_t) + bias, NEG)
        e = jnp.where(valid, jnp.exp(s - jnp.max(s, -1, keepdims=True)), 0.0)
        den = jnp.sum(e, -1, keepdims=True)
        p = e / jnp.where(den > 0, den, 1.0)
        return _dot_nt(p.astype(BF16), vals_t)

    o_s = jnp.zeros((nh, kvd), F32)
    for k in range(NSA_KV_HEADS):
        blks = [blk_of(b, k * n_sel + r) for r in range(n_sel)]
        tiles = [selbuf[slot, k * n_sel + r] for r in range(n_sel)]
        keys_t = jnp.concatenate([t_[0:kvd, :] for t_ in tiles], axis=1).astype(BF16)
        vals_t = jnp.concatenate([t_[kvd:2 * kvd, :] for t_ in tiles], axis=1).astype(BF16)
        bias = jnp.concatenate([bsel_ref[jnp.clip(bl // per_page, 0, n_pages)] for bl in blks], axis=1)
        valid = jnp.concatenate(
            [(lane // SEL_BLOCK == bl % per_page) & ((bl // per_page) * PAGE_SIZE + lane <= past) & (bl >= 0)
             for bl in blks], axis=1)
        o_k = attend(_head_pad(q, lambda r: r // NSA_GROUP == k), keys_t, vals_t, bias, valid)
        o_s = jnp.where(head // NSA_GROUP == k, o_k, o_s)
    wlanes = wall.shape[1]
    wall[:, 0:wb] = wc_ref[0]
    tail = lax.broadcasted_iota(jnp.int32, (1, wlanes - wb), 1)
    wall[:, wb:wlanes] = jnp.where(tail == 0, _row_to_col(wn_ref[0]), 0.0)
    win_ref[0] = pltpu.roll(wall[...], wlanes - 1, 1)[:, 0:wb]
    w_pos = lax.broadcasted_iota(jnp.int32, (1, wlanes), 1)
    valid_w = (w_pos <= wb) & (wb - w_pos < WINDOW) & (past - wb + w_pos >= 0)
    o_w = attend(_head_pad(q, lambda r: r >= 0), wall[0:kvd, :].astype(BF16), wall[kvd:2 * kvd, :].astype(BF16),
                 bwin_ref[...], valid_w)
    o_c = jnp.concatenate([oc_ref[0]] * NSA_KV_HEADS, axis=1)
    mix = gates[:, 0:1] * o_c + gates[:, 1:2] * o_s + gates[:, 2:3] * o_w
    out = mix[:, 0:hd]
    for k in range(1, NSA_KV_HEADS):
        out = jnp.where(head // NSA_GROUP == k, mix[:, k * hd:(k + 1) * hd], out)
    ob_ref[0] = out.astype(ob_ref.dtype)


def nsa_sample_att(q8, gates, oc, kv03, kv45, wcache_t, layer, pages_t, page_table, sel_idx, rel_bias):
    bd, nh, hd = q8.shape
    n_pages = page_table.shape[1]
    past = n_pages * PAGE_SIZE
    wb = wcache_t.shape[2]
    kvd = NSA_KV_HEADS * hd
    n_sel = sel_idx.shape[2]
    wlanes = _round_up(wb + 1, LANES)
    g3 = gates[:, :3 * nh].reshape(bd, nh, 3)
    dist = past - (jnp.arange(n_pages + 1)[:, None] * PAGE_SIZE + jnp.arange(PAGE_SIZE)[None, :])
    bsel = jnp.transpose(_bias_of(rel_bias, dist), (0, 2, 1))
    bwin = _bias_of(rel_bias, wb - jnp.arange(wlanes)).T
    blk = lambda n2, n3: pl.BlockSpec((1, n2, n3), lambda bi, pt, si: (bi, 0, 0))
    grid_spec = pltpu.PrefetchScalarGridSpec(
        num_scalar_prefetch=2, grid=(bd,),
        in_specs=[blk(nh, hd), blk(nh, 3), blk(nh, hd), blk(1, 4 * kvd), blk(1, 2 * kvd),
                  pl.BlockSpec((1, 2 * kvd, wb), lambda bi, pt, si: (layer * bd + bi, 0, 0)),
                  pl.BlockSpec(memory_space=pl.ANY),
                  pl.BlockSpec(bsel.shape, lambda bi, pt, si: (0, 0, 0)),
                  pl.BlockSpec(bwin.shape, lambda bi, pt, si: (0, 0))],
        out_specs=[blk(nh, hd), blk(2 * kvd, wb)],
        scratch_shapes=[pltpu.VMEM((2, NSA_KV_HEADS * n_sel, 2 * kvd, PAGE_SIZE), F32),
                        pltpu.VMEM((2 * kvd, wlanes), F32), pltpu.SemaphoreType.DMA((2,))])
    ob, win = pl.pallas_call(
        functools.partial(_nsa_sample_att_kernel, n_pages=n_pages, n_sel=n_sel, past=past, wb=wb),
        grid_spec=grid_spec,
        out_shape=[jax.ShapeDtypeStruct((bd, nh, hd), BF16), jax.ShapeDtypeStruct((bd, 2 * kvd, wb), F32)],
        compiler_params=_cparams(("arbitrary",), VMEM_LIMIT), name="nsa_sample_att",
    )(page_table.reshape(-1), sel_idx.reshape(-1), q8, g3, oc, kv03.reshape(bd, 1, -1), kv45.reshape(bd, 1, -1),
      wcache_t, pages_t, bsel, bwin)
    return ob.reshape(bd, nh * hd), win


def kernel(x_prompt, x_sample, mem_prompt, cache_conv, cache_nsa_pages, cache_nsa_window, state_mlstm_c,
           state_mlstm_n, state_mlstm_m, cache_mem_kv, page_table, rel_bias, norm_mix, norm_xattn, norm_mem,
           norm_ffn, norm_final, w_in_even, w_out_even, conv_w, conv_b, conv_ln_g, conv_ln_b, nsa_cmp_pe,
           nsa_cmp_w, w_in_odd, mlstm_b_i, mlstm_b_f, mlstm_norm, w_out_odd, xattn_wq, xattn_wkv, xattn_wo,
           ffn_w_gu, ffn_w_dn, router_w, router_b, expert_w_gu, expert_w_dn):
    b, s, d = x_prompt.shape
    bd, td, _ = x_sample.shape
    assert td == 1, "the sample group decodes one token per sequence"
    depth = norm_mix.shape[0]
    mt = mem_prompt.shape[1]
    cc = conv_w.shape[2]
    hist = conv_w.shape[1] - 1
    wb = cache_nsa_window.shape[2]
    kvh, hd = NSA_KV_HEADS, NSA_HD
    n_pool = cache_nsa_pages.shape[1]
    assert s >= hist and s >= wb and s % Q_BLOCK == 0
    xp = x_prompt.reshape(b * s, d)
    xs = x_sample.reshape(bd, d)
    mem = mem_prompt.reshape(b * mt, d)
    pages_t = jnp.swapaxes(cache_nsa_pages.reshape(-1, PAGE_SIZE, 4 * kvh * hd), 1, 2)
    window_t = jnp.swapaxes(cache_nsa_window.reshape(-1, wb, 2 * kvh * hd), 1, 2)
    xhd = d // X_HEADS
    memkv_rows = jnp.swapaxes(cache_mem_kv.reshape(depth * bd, mt, 2, X_HEADS, xhd // LANES, LANES), 3, 4)
    memkv_rows = memkv_rows.reshape(depth * bd, -1, LANES)
    bf = lambda a: a.astype(BF16)
    conv_p, conv_s, nsa_p, nsa_s, win_p, win_s = [], [], [], [], [], []
    mc_p, mc_s, mn_p, mn_s, mm_p, mm_s, memkv_p = [], [], [], [], [], [], []
    for l in range(depth):
        li = l // 2
        if l % 2 == 0:
            prm = prep_even(w_in_even[li], nsa_cmp_pe[li], nsa_cmp_w[li])
            w_out = bf(w_out_even[li])
            w_parts = [w_out[:cc], w_out[cc:]]
            conv_args = (conv_w[li], conv_b[li], conv_ln_g[li], conv_ln_b[li])
            glu, keys, qt, vt, gt, kv_t = inproj_even(xp, norm_mix[l], prm['w_in'], cc, s)
            a_out = conv_prompt(glu, *conv_args, b, s)
            kc, vct = compress_prompt(kv_t, prm['pe2'], prm['wbd'], _round_up(s // CMP_BLOCK, LANES))
            b_out = nsa_prompt(qt, gt, kc, vct, keys, vt, rel_bias, b, s)
            mix_p = ([a_out, b_out], w_parts)
            conv_p.append(glu.reshape(b, s, cc)[:, s - hist:])
            rows_t = kv_t.reshape(b, 6, kvh, hd, s)
            nsa_p.append(jnp.transpose(rows_t[:, :4], (0, 4, 1, 2, 3)))
            win_p.append(jnp.transpose(rows_t[:, 4:, :, :, s - wb:], (0, 4, 1, 2, 3)))
            glu, kv03, kv45, q, gates = inproj_even(xs, norm_mix[l], prm['w_in'], cc)
            a_out, conv_state = conv_sample(cache_conv[li], glu, *conv_args)
            q8 = q.reshape(bd, NSA_HEADS, hd)
            pt = page_table + li * n_pool
            o_c, sel_idx = nsa_sample_cmp(q8, pages_t, pt, prm['pe2'], prm['wbd'], rel_bias)
            b_out, win = nsa_sample_att(q8, gates, o_c, kv03, kv45, window_t, li, pages_t, pt, sel_idx, rel_bias)
            xs = outproj(xs, [a_out, b_out], w_parts)
            conv_s.append(conv_state)
            nsa_s.append(kv03.reshape(bd, 1, 4, kvh, hd))
            win_s.append(jnp.transpose(win.reshape(bd, 2, kvh, hd, wb), (0, 4, 1, 2, 3)))
        else:
            prm = prep_odd(w_in_odd[li], mlstm_b_i[li], mlstm_b_f[li])
            w_out = bf(w_out_odd[li])
            q, k, v, og, gc, gr = inproj_odd(xp, norm_mix[l], prm['w_in'], prm['bias'])
            hn, c_new, n_new, m_new = mlstm_prompt(q, k, v, og, gc, gr, mlstm_norm[li], b, s)
            mix_p = ([hn], [w_out])
            mc_p.append(c_new)
            mn_p.append(n_new)
            mm_p.append(m_new[:, :, 0])
            q, k, v, og, gc, gr = inproj_odd(xs, norm_mix[l], prm['w_in'], prm['bias'])
            hn, c_new, n_new, m_new = mlstm_sample(q, k, v, og, gc, mlstm_norm[li], state_mlstm_c[li],
                                                   state_mlstm_n[li], state_mlstm_m[li])
            xs = outproj(xs, [hn], [w_out])
            mc_s.append(c_new)
            mn_s.append(n_new)
            mm_s.append(m_new)
        wq, wo = bf(xattn_wq[l]), bf(xattn_wo[l])
        mkv_rows, mkv_b = memkv(mem, norm_mem[l], bf(xattn_wkv[l]), X_HEADS)
        mkv = jnp.swapaxes(mkv_rows.reshape(b, mt, 2, xhd // LANES, X_HEADS, LANES), 3, 4)
        memkv_p.append(mkv.reshape(b, mt, 2, X_HEADS, xhd))
        xs = xattn_sample(xs, norm_xattn[l], wq, memkv_rows, wo, l, mt)
        if l % 2 == 0:
            w_gu, w_dn = bf(ffn_w_gu[li]), bf(ffn_w_dn[li])
            xp = xattn_ffn(xp, norm_xattn[l], wq, mkv_b, wo, *mix_p, norm_ffn[l], w_gu, w_dn, b, s)
            xs = ffn(xs, norm_ffn[l], w_gu, w_dn)
        else:
            xp = xattn_prompt(xp, norm_xattn[l], wq, mkv_b, wo, b, s, *mix_p)
            e_gu, e_dn = bf(expert_w_gu[li]), bf(expert_w_dn[li])
            final_g = norm_final if l == depth - 1 else None
            comb, h, mask, counts = router(xp, norm_ffn[l], router_w[li], router_b[li])
            xp = moe_compact(xp, h, comb, mask, e_gu, e_dn, final_g)
            comb, h, _, _ = router(xs, norm_ffn[l], router_w[li], router_b[li])
            xs = moe(xs, h, comb, e_gu, e_dn, final_g)
    if depth % 2:
        xp, xs = rmsnorm(xp, norm_final), rmsnorm(xs, norm_final)
    y_prompt = xp.reshape(b, s, d)
    y_sample = xs.reshape(bd, 1, d)
    return (y_prompt, y_sample, jnp.stack(conv_p), jnp.stack(conv_s), jnp.stack(nsa_p), jnp.stack(nsa_s),
            jnp.stack(win_p), jnp.stack(win_s), jnp.stack(mc_p), jnp.stack(mc_s), jnp.stack(mn_p),
            jnp.stack(mn_s), jnp.stack(mm_p), jnp.stack(mm_s), jnp.stack(memkv_p))
```

```python
import functools
import math

import jax
import jax.numpy as jnp
import numpy as np
from jax import lax
from jax.experimental import pallas as pl
from jax.experimental.pallas import tpu as pltpu

F32 = jnp.float32
BF16 = jnp.bfloat16

PAGE_SIZE = 128
CONV_WIDTH = 31
NSA_HEADS = 8
NSA_KV_HEADS = 2
NSA_GROUP = NSA_HEADS // NSA_KV_HEADS
NSA_HD = 64
CMP_BLOCK = 32
SEL_BLOCK = 64
SEL_RATIO = SEL_BLOCK // CMP_BLOCK
TOP_N = 16
WINDOW = 512
Q_BLOCK = 128
FORCE_SCORE = 1.0e4
NUM_BUCKETS = 32
MAX_DISTANCE = 1024
MLSTM_HEADS = 4
X_HEADS = 4
N_EXPERTS = 8
TOP_K = 2
RMS_EPS = 1e-6
LN_EPS = 1e-5
NEG = -1e30

LANES = 128
SUBLANES = 8
VMEM_LIMIT = 56 * 1024 * 1024
MLSTM_CHUNK = 256
SEL_TILES = 4
WIN_TILES = 3


def _cparams(sem, vmem=None):
    return pltpu.CompilerParams(dimension_semantics=sem, vmem_limit_bytes=vmem)


def _rms(x, g):
    return x * lax.rsqrt(jnp.mean(x * x, -1, keepdims=True) + RMS_EPS) * g


def _dot(a, b):
    return jnp.dot(a, b, preferred_element_type=F32)


def _dot_nt(a, b):
    return lax.dot_general(a, b, (((1,), (1,)), ((), ())), preferred_element_type=F32)


def _dot_tn(a, b):
    return lax.dot_general(a, b, (((0,), (0,)), ((), ())), preferred_element_type=F32)


def _full(shape):
    n = len(shape)
    return pl.BlockSpec(shape, lambda *_: (0,) * n)


def _row_tile(m, pref):
    t = min(pref, m)
    while m % t:
        t //= 2
    return t


def _rmsnorm_kernel(x_ref, g_ref, o_ref):
    o_ref[...] = _rms(x_ref[...], g_ref[...])


def rmsnorm(x, g):
    m, d = x.shape
    tm = _row_tile(m, 1024)
    return pl.pallas_call(
        _rmsnorm_kernel, grid=(m // tm,),
        in_specs=[pl.BlockSpec((tm, d), lambda i: (i, 0)), _full((1, d))],
        out_specs=pl.BlockSpec((tm, d), lambda i: (i, 0)),
        out_shape=jax.ShapeDtypeStruct((m, d), F32),
        compiler_params=_cparams(("parallel",)), name="rmsnorm",
    )(x, g.reshape(1, d))


def _outproj_kernel(*refs, n_in):
    x_ref = refs[0]
    a_refs = refs[1:1 + n_in]
    w_refs = refs[1 + n_in:1 + 2 * n_in]
    o_ref = refs[1 + 2 * n_in]
    acc = x_ref[...]
    for a_ref, w_ref in zip(a_refs, w_refs):
        acc = acc + _dot(a_ref[...], w_ref[...])
    o_ref[...] = acc


def outproj(x, acts, ws):
    m, d = x.shape
    tm = _row_tile(m, 512)
    n_in = len(acts)
    in_specs = [pl.BlockSpec((tm, d), lambda i: (i, 0))]
    in_specs += [pl.BlockSpec((tm, a.shape[1]), lambda i: (i, 0)) for a in acts]
    in_specs += [_full(w.shape) for w in ws]
    return pl.pallas_call(
        functools.partial(_outproj_kernel, n_in=n_in), grid=(m // tm,),
        in_specs=in_specs, out_specs=pl.BlockSpec((tm, d), lambda i: (i, 0)),
        out_shape=jax.ShapeDtypeStruct((m, d), F32),
        compiler_params=_cparams(("parallel",)), name="outproj",
    )(x, *acts, *ws)


def _inproj_even_kernel(x_ref, g_ref, w_ref, glu_ref, *rest, cc, qd, kvd, tiles):
    xn = _rms(x_ref[...], g_ref[...]).astype(BF16)

    def mm(lo, hi):
        return _dot(xn, w_ref[:, lo:hi])

    o = 0
    a = mm(o, o + cc)
    b = mm(o + cc, o + 2 * cc)
    glu_ref[...] = a * jax.nn.sigmoid(b)
    o += 2 * cc
    q = mm(o, o + qd) * (NSA_HD ** -0.5)
    o += qd
    kv03 = mm(o, o + 4 * kvd)
    o += 4 * kvd
    kv45 = mm(o, o + 2 * kvd)
    o += 2 * kvd
    gates = jax.nn.sigmoid(mm(o, o + LANES))
    if tiles == 0:
        kv03_ref, kv45_ref, q_ref, gate_ref = rest
        kv03_ref[...] = kv03
        kv45_ref[...] = kv45
        q_ref[...] = q.astype(BF16)
        gate_ref[...] = gates
        return
    keys_ref, qt_ref, vt_ref, gt_ref, kvt_ref = rest
    kvt_ref[0] = jnp.concatenate([kv03, kv45], axis=1).T
    keys_ref[...] = jnp.concatenate([kv03[:, 2 * kvd:3 * kvd], kv45[:, 0:kvd]], axis=1).astype(BF16)
    vals = jnp.concatenate([kv03[:, 3 * kvd:4 * kvd], kv45[:, kvd:2 * kvd]], axis=1)
    for j in range(tiles):
        rows = slice(j * Q_BLOCK, (j + 1) * Q_BLOCK)
        qt_ref[j] = q[rows, :].T.astype(BF16)
        vt_ref[j] = vals[rows, :].T.astype(BF16)
        gt_ref[j] = gates[rows, :].T


def inproj_even(x, g, w_pad, cc, seq=None):
    m, d = x.shape
    qd = NSA_HEADS * NSA_HD
    kvd = NSA_KV_HEADS * NSA_HD
    tm = _row_tile(m, 256)
    row = lambda n: pl.BlockSpec((tm, n), lambda i: (i, 0))
    out_specs = [row(cc)]
    out_shape = [jax.ShapeDtypeStruct((m, cc), F32)]
    transposed = seq is not None
    tiles = tm // Q_BLOCK if transposed else 0
    if transposed:
        assert tm % Q_BLOCK == 0 and seq % tm == 0
        per_seq = seq // tm
        tile = lambda n: pl.BlockSpec((tiles, n, Q_BLOCK), lambda i: (i, 0, 0))
        out_specs += [row(2 * kvd), tile(qd), tile(2 * kvd), tile(LANES),
                      pl.BlockSpec((1, 6 * kvd, tm), lambda i: (i // per_seq, 0, i % per_seq))]
        out_shape += [jax.ShapeDtypeStruct((m, 2 * kvd), BF16),
                      jax.ShapeDtypeStruct((m // Q_BLOCK, qd, Q_BLOCK), BF16),
                      jax.ShapeDtypeStruct((m // Q_BLOCK, 2 * kvd, Q_BLOCK), BF16),
                      jax.ShapeDtypeStruct((m // Q_BLOCK, LANES, Q_BLOCK), F32),
                      jax.ShapeDtypeStruct((m // seq, 6 * kvd, seq), F32)]
    else:
        out_specs += [row(4 * kvd), row(2 * kvd), row(qd), row(LANES)]
        out_shape += [jax.ShapeDtypeStruct((m, 4 * kvd), F32), jax.ShapeDtypeStruct((m, 2 * kvd), F32),
                      jax.ShapeDtypeStruct((m, qd), BF16), jax.ShapeDtypeStruct((m, LANES), F32)]
    return pl.pallas_call(
        functools.partial(_inproj_even_kernel, cc=cc, qd=qd, kvd=kvd, tiles=tiles), grid=(m // tm,),
        in_specs=[row(d), _full((1, d)), _full(w_pad.shape)],
        out_specs=out_specs, out_shape=out_shape,
        compiler_params=_cparams(("parallel",)), name="inproj_even",
    )(x, g.reshape(1, d), w_pad)


def _conv_post(y, lg, lb):
    mu = jnp.mean(y, -1, keepdims=True)
    var = jnp.mean(jnp.square(y - mu), -1, keepdims=True)
    yn = (y - mu) * lax.rsqrt(var + LN_EPS) * lg + lb
    return yn * jax.nn.sigmoid(yn)


CONV_SUB = 64
CONV_PAD = 32


def _conv_prompt_kernel(glu_ref, cw_ref, cb_ref, lg_ref, lb_ref, o_ref, ext_ref, y_ref, *, ts, s):
    i = pl.program_id(1)
    c = glu_ref.shape[-1]

    @pl.when(i == 0)
    def _():
        ext_ref[0:CONV_PAD, :] = jnp.zeros((CONV_PAD, c), F32)
        ext_ref[CONV_PAD:CONV_PAD + s, :] = glu_ref[0]
        ext_ref[CONV_PAD + s:CONV_PAD + s + SUBLANES, :] = jnp.zeros((SUBLANES, c), F32)

    lead = CONV_PAD - (CONV_WIDTH - 1)
    span = CONV_SUB + CONV_PAD

    def sub(j, carry):
        r0 = pl.multiple_of(i * ts + j * CONV_SUB, CONV_SUB)
        for c0 in range(0, c, LANES):
            xw = ext_ref[pl.ds(r0, span + SUBLANES), c0:c0 + LANES]
            acc = jnp.zeros((CONV_SUB, LANES), F32) + cb_ref[:, c0:c0 + LANES]
            for r in range(SUBLANES):
                xr = xw if r == 0 else pltpu.roll(xw, span + SUBLANES - r, 0)
                for a in range(span // SUBLANES):
                    w = SUBLANES * a + r - lead
                    if 0 <= w < CONV_WIDTH:
                        acc = acc + xr[SUBLANES * a:SUBLANES * a + CONV_SUB, :] * cw_ref[w:w + 1, c0:c0 + LANES]
            y_ref[:, c0:c0 + LANES] = acc
        o_ref[0, pl.ds(pl.multiple_of(j * CONV_SUB, CONV_SUB), CONV_SUB), :] = _conv_post(
            y_ref[...], lg_ref[...], lb_ref[...]).astype(o_ref.dtype)
        return carry

    lax.fori_loop(0, ts // CONV_SUB, sub, 0)


def conv_prompt(glu, cw, cb, lg, lb, b, s):
    c = glu.shape[-1]
    ts = _row_tile(s, 256)
    vec = lambda a: a.reshape(1, c)
    out = pl.pallas_call(
        functools.partial(_conv_prompt_kernel, ts=ts, s=s), grid=(b, s // ts),
        in_specs=[pl.BlockSpec((1, s, c), lambda bi, i: (bi, 0, 0)), _full((CONV_WIDTH, c)),
                  _full((1, c)), _full((1, c)), _full((1, c))],
        out_specs=pl.BlockSpec((1, ts, c), lambda bi, i: (bi, i, 0)),
        out_shape=jax.ShapeDtypeStruct((b, s, c), BF16),
        scratch_shapes=[pltpu.VMEM((CONV_PAD + s + SUBLANES, c), F32), pltpu.VMEM((CONV_SUB, c), F32)],
        compiler_params=_cparams(("parallel", "arbitrary")), name="conv_prompt",
    )(glu.reshape(b, s, c), cw, vec(cb), vec(lg), vec(lb))
    return out.reshape(b * s, c)


def _conv_sample_kernel(cache_ref, glu_ref, cw_ref, cb_ref, lg_ref, lb_ref, o_ref, st_ref):
    hist = CONV_WIDTH - 1
    cache = cache_ref[...]
    glu = glu_ref[...]
    y = jnp.sum(cache * cw_ref[0:hist, :][None], axis=1) + glu * cw_ref[hist:hist + 1, :] + cb_ref[...]
    o_ref[...] = _conv_post(y, lg_ref[...], lb_ref[...]).astype(o_ref.dtype)
    st_ref[:, 0:hist - 1, :] = cache[:, 1:hist, :]
    st_ref[:, hist - 1:hist, :] = glu[:, None, :]


def conv_sample(cache, glu, cw, cb, lg, lb):
    bd, hist, c = cache.shape
    vec = lambda a: a.reshape(1, c)
    return pl.pallas_call(
        _conv_sample_kernel,
        out_shape=[jax.ShapeDtypeStruct((bd, c), BF16), jax.ShapeDtypeStruct((bd, hist, c), F32)],
        name="conv_sample",
    )(cache, glu, cw, vec(cb), vec(lg), vec(lb))


def _rel_bucket(dist):
    n = jnp.maximum(dist, 0)
    max_exact = NUM_BUCKETS // 2
    nf = jnp.maximum(n, 1).astype(F32)
    large = max_exact + (jnp.log(nf / max_exact) / math.log(MAX_DISTANCE / max_exact)
                         * (NUM_BUCKETS - max_exact)).astype(jnp.int32)
    large = jnp.minimum(large, NUM_BUCKETS - 1)
    return jnp.where(n < max_exact, n, large)


def _bias_of(rel_bias, dist):
    bucket = _rel_bucket(dist)[..., None]
    out = jnp.zeros(bucket.shape[:-1] + (rel_bias.shape[1],), F32)
    for k in range(NUM_BUCKETS):
        out = jnp.where(bucket == k, rel_bias[k].astype(F32), out)
    return out


def _compress_accumulate(load_rows, pe_ref, w_ref, nc):
    accs = []
    half = CMP_BLOCK // 2
    for slot in range(2):
        acc = jnp.zeros((nc, w_ref.shape[-1]), F32)
        for j in range(half):
            pair = [load_rows(slot, jj) + pe_ref[slot, jj:jj + 1, :] for jj in (j, j + half)]
            acc = acc + _dot(jnp.concatenate(pair, axis=1).astype(BF16), w_ref[slot, j])
        accs.append(acc)
    return jnp.concatenate(accs, axis=1)


def _file_rows(tile_t, xrow, sl, page):
    rows = tile_t.T
    per_page = PAGE_SIZE // CMP_BLOCK
    for cl in range(per_page):
        for a in range(CMP_BLOCK // SUBLANES):
            r0 = cl * CMP_BLOCK + a * SUBLANES
            dst = pl.multiple_of((page * per_page + cl) * SUBLANES, SUBLANES)
            xrow[sl, a, pl.ds(dst, SUBLANES), :] = rows[r0:r0 + SUBLANES, :]


def _compress_filed(xrow, pe_ref, w_ref, nc):
    return _compress_accumulate(
        lambda sl, j: xrow[sl, j // SUBLANES, pl.ds(j % SUBLANES, nc, stride=SUBLANES), :], pe_ref, w_ref, nc)


def _compress_prompt_kernel(x_ref, pe_ref, w_ref, kc_ref, vct_ref, xrow, *, nc, ncp, kvd):
    for p in range(x_ref.shape[2] // PAGE_SIZE):
        for sl in range(2):
            _file_rows(x_ref[0, sl * kvd:(sl + 1) * kvd, p * PAGE_SIZE:(p + 1) * PAGE_SIZE], xrow, sl, p)
    acc = _compress_filed(xrow, pe_ref, w_ref, nc)
    if ncp > nc:
        acc = jnp.concatenate([acc, jnp.zeros((ncp - nc, 2 * kvd), F32)], axis=0)
    kc_ref[0] = acc[:, 0:kvd].astype(BF16)
    vct_ref[0] = acc[:, kvd:2 * kvd].T.astype(BF16)


def compress_prompt(kv_t, pe2, wbd, ncp):
    b, _, s = kv_t.shape
    assert s % PAGE_SIZE == 0
    nc = s // CMP_BLOCK
    kvd = wbd.shape[-1]
    return pl.pallas_call(
        functools.partial(_compress_prompt_kernel, nc=nc, ncp=ncp, kvd=kvd), grid=(b,),
        in_specs=[pl.BlockSpec((1, 2 * kvd, s), lambda bi: (bi, 0, 0)), _full(pe2.shape), _full(wbd.shape)],
        out_specs=[pl.BlockSpec((1, ncp, kvd), lambda bi: (bi, 0, 0)), pl.BlockSpec((1, kvd, ncp), lambda bi: (bi, 0, 0))],
        out_shape=[jax.ShapeDtypeStruct((b, ncp, kvd), BF16), jax.ShapeDtypeStruct((b, kvd, ncp), BF16)],
        scratch_shapes=[pltpu.VMEM((2, CMP_BLOCK // SUBLANES, nc * SUBLANES, kvd), F32)],
        compiler_params=_cparams(("parallel",)), name="compress_prompt",
    )(kv_t, pe2, wbd)


def _select_blocks(score, n_sel, n_cand):
    lane = lax.broadcasted_iota(jnp.int32, score.shape, 1)
    rank = jnp.zeros(score.shape, F32)
    for i in range(n_cand):
        col = score[:, 2 * i:2 * i + 1]
        beats = (col > score) | ((col == score) & (lane > 2 * i))
        rank = rank + beats.astype(F32)
    is_cand = ((lane % 2) == 0) & (lane < 2 * n_cand)
    return (is_cand & (rank < n_sel) & (score >= 0)).astype(F32), rank


def _pair_sum(imp):
    n = imp.shape[1]
    return imp + pltpu.roll(imp, n - 1, 1)


def _rank_rows(score, n_sel, n_cand):
    blk = lax.broadcasted_iota(jnp.int32, score.shape, 0)
    rank = jnp.zeros(score.shape, F32)
    for i in range(n_cand):
        row = score[i:i + 1, :]
        beats = (row > score) | ((row == score) & (blk > i))
        rank = rank + beats.astype(F32)
    return ((rank < n_sel) & (score >= 0)).astype(F32)


def _nsa_prompt_kernel(qt_ref, gt_ref, kc_ref, vct_ref, keys_ref, vt_ref, biasc_ref, btile_ref, o_ref,
                       qt_scr, oc_scr, acc_s, acc_w, imp_scr, sel_scr, out_scr, *, nc, nsb, n_sel):
    qi = pl.program_id(1)
    g, hd, kvh, qb = NSA_GROUP, NSA_HD, NSA_KV_HEADS, Q_BLOCK
    kvd = kvh * hd
    ncp = kc_ref.shape[1]
    nsbp = sel_scr.shape[1]
    q_pos = qi * qb + lax.broadcasted_iota(jnp.int32, (1, qb), 1)
    key_row = lax.broadcasted_iota(jnp.int32, (qb, 1), 0)
    c_row = lax.broadcasted_iota(jnp.int32, (ncp, 1), 0)
    mask_c = (q_pos >= c_row * CMP_BLOCK + (CMP_BLOCK - 1)) & (c_row < nc)
    blk = lax.broadcasted_iota(jnp.int32, (nsbp, 1), 0)
    cur = q_pos // SEL_BLOCK
    forced = (blk == 0) | (blk == cur) | (blk == cur - 1)
    zeros = jnp.zeros((hd, qb), BF16)
    for k in range(kvh):
        for gi in range(g):
            h = k * g + gi
            parts = [zeros] * kvh
            parts[k] = qt_ref[0, h * hd:(h + 1) * hd, :]
            qt_scr[k, :, gi * qb:(gi + 1) * qb] = jnp.concatenate(parts, axis=0)
        s_c = _dot(kc_ref[0], qt_scr[k])
        imp = jnp.zeros((ncp, qb), F32)
        probs = []
        for gi in range(g):
            s = jnp.where(mask_c, s_c[:, gi * qb:(gi + 1) * qb] + biasc_ref[k * g + gi], NEG)
            e = jnp.where(mask_c, jnp.exp(s - jnp.max(s, 0, keepdims=True)), 0.0)
            den = jnp.sum(e, 0, keepdims=True)
            p = e / jnp.where(den > 0, den, 1.0)
            imp = imp + p
            probs.append(p.astype(BF16))
        oc_scr[k] = _dot(vct_ref[0, k * hd:(k + 1) * hd, :], jnp.concatenate(probs, axis=1))
        imp_scr[...] = imp + pltpu.roll(imp, ncp - 1, 0)
        cand = imp_scr[pl.ds(0, nsbp, stride=SEL_RATIO), :]
        score = jnp.where(forced, FORCE_SCORE, jnp.where(blk <= cur, cand, -1.0))
        sel_scr[k] = _rank_rows(jnp.where(blk < nsb, score, -2.0), n_sel, nsb)

    per_tile = qb // SEL_BLOCK
    n_tiles = keys_ref.shape[1] // qb
    first = ([jnp.full((1, qb), NEG, F32)] * g, [jnp.zeros((1, qb), F32)] * g)

    def tile_step(tiles, carry, key_col, val_row, acc_ref, window):
        kts = [jnp.clip(kt, 0, n_tiles - 1) for kt, _ in tiles]
        starts = [pl.multiple_of(kt * qb, qb) for kt in kts]
        k_t = jnp.concatenate([keys_ref[0, pl.ds(r0, qb), key_col:key_col + kvd] for r0 in starts], axis=0)
        dist = jnp.concatenate([jnp.where(active, q_pos - (r0 + key_row), -1)
                                for r0, (_, active) in zip(starts, tiles)], axis=0)
        in_range = dist >= 0
        scores = [_dot(k_t, qt_scr[k]) for k in range(kvh)]
        new, updates = [], []
        for k in range(kvh):
            if window:
                valid = in_range & (dist < WINDOW)
            else:
                pieces = []
                for kt in kts:
                    chosen = jnp.zeros((qb, qb), F32)
                    for j in range(per_tile):
                        row = sel_scr[k, pl.ds(per_tile * kt + j, 1), :]
                        chosen = jnp.where(key_row // SEL_BLOCK == j, row, chosen)
                    pieces.append(chosen)
                valid = in_range & (jnp.concatenate(pieces, axis=0) > 0.5)
            ms, ls = carry[k]
            ms2, ls2, alphas, probs = [], [], [], []
            for gi in range(g):
                bias = jnp.concatenate([btile_ref[jnp.maximum(qi - kt, 0), k * g + gi] for kt in kts], axis=0)
                s = jnp.where(valid, scores[k][:, gi * qb:(gi + 1) * qb] + bias, NEG)
                m_new = jnp.maximum(ms[gi], jnp.max(s, 0, keepdims=True))
                alpha = jnp.exp(ms[gi] - m_new)
                p = jnp.exp(s - jnp.where(m_new == NEG, 0.0, m_new))
                ms2.append(m_new)
                ls2.append(alpha * ls[gi] + jnp.sum(p, 0, keepdims=True))
                alphas.append(alpha)
                probs.append(p.astype(BF16))
            new.append((ms2, ls2))
            updates.append((jnp.concatenate(alphas, axis=1), jnp.concatenate(probs, axis=1)))
        for k, (alpha, prob) in enumerate(updates):
            v_t = jnp.concatenate([vt_ref[kt, val_row + k * hd:val_row + (k + 1) * hd, :] for kt in kts], axis=1)
            acc_ref[k] = acc_ref[k] * alpha + _dot(v_t, prob)
        return tuple(new)

    acc_s[...] = jnp.zeros_like(acc_s)
    acc_w[...] = jnp.zeros_like(acc_w)
    sel_args = dict(key_col=0, val_row=0, acc_ref=acc_s, window=False)
    win_args = dict(key_col=kvd, val_row=kvd, acc_ref=acc_w, window=True)

    def sel_group(i, carry):
        return tile_step([(SEL_TILES * i + j, SEL_TILES * i + j <= qi) for j in range(SEL_TILES)], carry, **sel_args)

    stat_s = lax.fori_loop(0, qi // SEL_TILES + 1, sel_group, (first,) * kvh)
    stat_w = (first,) * kvh
    win_tiles = [(qi - j, qi - j >= 0) for j in range(WINDOW // qb, -1, -1)]
    for j in range(0, len(win_tiles), WIN_TILES):
        stat_w = tile_step(win_tiles[j:j + WIN_TILES], stat_w, **win_args)
    gt = gt_ref[0]
    for k in range(kvh):
        for gi in range(g):
            h = k * g + gi
            cols = slice(gi * qb, (gi + 1) * qb)
            l_s, l_w = stat_s[k][1][gi], stat_w[k][1][gi]
            o_s = acc_s[k, :, cols] / jnp.where(l_s > 0, l_s, 1.0)
            o_w = acc_w[k, :, cols] / jnp.where(l_w > 0, l_w, 1.0)
            out_scr[h * hd:(h + 1) * hd, :] = (gt[3 * h:3 * h + 1, :] * oc_scr[k, :, cols]
                                               + gt[3 * h + 1:3 * h + 2, :] * o_s + gt[3 * h + 2:3 * h + 3, :] * o_w)
    o_ref[0] = out_scr[...].T.astype(o_ref.dtype)


def nsa_prompt(qt, gt, kc, vct, keys, vt, rel_bias, b, s):
    qb = Q_BLOCK
    nq = s // qb
    nc = s // CMP_BLOCK
    ncp = kc.shape[1]
    nsb = s // SEL_BLOCK
    nsbp = _round_up(nsb, SUBLANES)
    assert SEL_RATIO * nsbp <= ncp
    n_sel = min(TOP_N, nsb)
    nh = NSA_HEADS
    hq = qt.shape[1]
    kvd = kc.shape[2]
    glanes = NSA_GROUP * qb
    cend = jnp.arange(ncp)[:, None] * CMP_BLOCK + (CMP_BLOCK - 1)
    biasc = jnp.transpose(_bias_of(rel_bias, jnp.arange(s)[None, :] - cend), (2, 0, 1))
    r = jnp.arange(qb)
    dist = jnp.arange(nq)[:, None, None] * qb + r[None, None, :] - r[None, :, None]
    btile = jnp.transpose(_bias_of(rel_bias, dist), (0, 3, 1, 2))
    out = pl.pallas_call(
        functools.partial(_nsa_prompt_kernel, nc=nc, nsb=nsb, n_sel=n_sel), grid=(b, nq),
        in_specs=[pl.BlockSpec((1, hq, qb), lambda bi, i: (bi * nq + i, 0, 0)),
                  pl.BlockSpec((1, LANES, qb), lambda bi, i: (bi * nq + i, 0, 0)),
                  pl.BlockSpec((1, ncp, kvd), lambda bi, i: (bi, 0, 0)),
                  pl.BlockSpec((1, kvd, ncp), lambda bi, i: (bi, 0, 0)),
                  pl.BlockSpec((1, s, 2 * kvd), lambda bi, i: (bi, 0, 0)),
                  pl.BlockSpec((nq, 2 * kvd, qb), lambda bi, i: (bi, 0, 0)),
                  pl.BlockSpec((nh, ncp, qb), lambda bi, i: (0, 0, i)),
                  _full(btile.shape)],
        out_specs=pl.BlockSpec((1, qb, hq), lambda bi, i: (bi, i, 0)),
        out_shape=jax.ShapeDtypeStruct((b, s, hq), BF16),
        scratch_shapes=[pltpu.VMEM((NSA_KV_HEADS, kvd, glanes), BF16), pltpu.VMEM((NSA_KV_HEADS, NSA_HD, glanes), F32),
                        pltpu.VMEM((NSA_KV_HEADS, NSA_HD, glanes), F32), pltpu.VMEM((NSA_KV_HEADS, NSA_HD, glanes), F32),
                        pltpu.VMEM((ncp, qb), F32), pltpu.VMEM((NSA_KV_HEADS, nsbp, qb), F32),
                        pltpu.VMEM((hq, qb), F32)],
        compiler_params=_cparams(("parallel", "arbitrary"), VMEM_LIMIT), name="nsa_prompt",
    )(qt, gt, kc, vct, keys.reshape(b, s, 2 * kvd), vt, biasc, btile)
    return out.reshape(b * s, hq)


def _round_up(x, m):
    return (x + m - 1) // m * m


def prep_even(w_in, pe, wc):
    d, n = w_in.shape
    n_pad = _round_up(n - 3 * NSA_HEADS, LANES) + LANES
    w_pad = jnp.zeros((d, n_pad), BF16).at[:, :n].set(w_in.astype(BF16))
    hd = NSA_HD
    pe2 = jnp.tile(pe, (1, 1, NSA_KV_HEADS))
    zero = jnp.zeros_like(wc)
    wbd = jnp.concatenate([jnp.concatenate([wc if i == j else zero for j in range(NSA_KV_HEADS)], axis=-1)
                           for i in range(NSA_KV_HEADS)], axis=-2)
    half = CMP_BLOCK // 2
    wbd = jnp.concatenate([wbd[:, :half], wbd[:, half:]], axis=2)
    return dict(w_in=w_pad, pe2=pe2, wbd=wbd.astype(BF16))


def _memkv_kernel(x_ref, g_ref, w_ref, o_ref, ob_ref, *, hd, nh):
    y = _dot(_rms(x_ref[...], g_ref[...]).astype(BF16), w_ref[...])
    ob_ref[...] = y.astype(BF16)
    tm = y.shape[0]
    chunks = hd // LANES
    period = 2 * chunks * nh
    for kv in range(2):
        for h in range(nh):
            for c in range(chunks):
                col = (kv * nh + h) * hd + c * LANES
                o_ref[pl.ds((kv * chunks + c) * nh + h, tm, stride=period), :] = y[:, col:col + LANES]


def memkv(mem, g, w, nh):
    m, d = mem.shape
    n = w.shape[1]
    hd = n // (2 * nh)
    per_tok = n // LANES
    tm = _row_tile(m, 256)
    return pl.pallas_call(
        functools.partial(_memkv_kernel, hd=hd, nh=nh), grid=(m // tm,),
        in_specs=[pl.BlockSpec((tm, d), lambda i: (i, 0)), _full((1, d)), _full(w.shape)],
        out_specs=[pl.BlockSpec((tm * per_tok, LANES), lambda i: (i, 0)), pl.BlockSpec((tm, n), lambda i: (i, 0))],
        out_shape=[jax.ShapeDtypeStruct((m * per_tok, LANES), F32), jax.ShapeDtypeStruct((m, n), BF16)],
        compiler_params=_cparams(("parallel",)), name="memkv",
    )(mem, g.reshape(1, d), w)


def _xattn_core(q, kv, hd):
    nh = q.shape[1] // hd
    outs = []
    for h in range(nh):
        s = _dot_nt(q[:, h * hd:(h + 1) * hd], kv[:, h * hd:(h + 1) * hd])
        e = jnp.exp(s - jnp.max(s, -1, keepdims=True))
        p = e / jnp.sum(e, -1, keepdims=True)
        outs.append(_dot(p.astype(BF16), kv[:, (nh + h) * hd:(nh + h + 1) * hd]))
    return jnp.concatenate(outs, axis=1).astype(BF16)


def _xattn_prompt_kernel(*refs, hd, n_in):
    x_ref, g_ref, wq_ref, kv_ref, wo_ref = refs[:5]
    a_refs = refs[5:5 + n_in]
    w_refs = refs[5 + n_in:5 + 2 * n_in]
    o_ref = refs[5 + 2 * n_in]
    x = x_ref[0]
    for a_ref, w_ref in zip(a_refs, w_refs):
        x = x + _dot(a_ref[0], w_ref[...])
    q = (_dot(_rms(x, g_ref[...]).astype(BF16), wq_ref[...]) * (hd ** -0.5)).astype(BF16)
    o = _xattn_core(q, kv_ref[0], hd)
    o_ref[0] = x + _dot(o, wo_ref[...])


def xattn_prompt(x, g, wq, kvb, wo, b, s, acts=(), ws=()):
    d = x.shape[1]
    mt = kvb.shape[0] // b
    tm = _row_tile(s, 512)
    tile = lambda n: pl.BlockSpec((1, tm, n), lambda bi, i: (bi, i, 0))
    out = pl.pallas_call(
        functools.partial(_xattn_prompt_kernel, hd=d // X_HEADS, n_in=len(acts)), grid=(b, s // tm),
        in_specs=[tile(d), _full((1, d)), _full(wq.shape),
                  pl.BlockSpec((1, mt, kvb.shape[1]), lambda bi, i: (bi, 0, 0)), _full(wo.shape)]
        + [tile(a.shape[1]) for a in acts] + [_full(w.shape) for w in ws],
        out_specs=tile(d),
        out_shape=jax.ShapeDtypeStruct((b, s, d), F32),
        compiler_params=_cparams(("parallel", "parallel"), VMEM_LIMIT), name="xattn_prompt",
    )(x.reshape(b, s, d), g.reshape(1, d), wq, kvb.reshape(b, mt, -1), wo,
      *[a.reshape(b, s, -1) for a in acts], *ws)
    return out.reshape(b * s, d)


def _xattn_sample_kernel(x_ref, g_ref, wq_ref, kv_ref, wo_ref, o_ref, q_scr, a_scr, *, hd, mt):
    bi = pl.program_id(0)
    nb = pl.num_programs(0)
    nh = wq_ref.shape[1] // hd

    @pl.when(bi == 0)
    def _():
        q_scr[...] = _dot(_rms(x_ref[...], g_ref[...]).astype(BF16), wq_ref[...]) * (hd ** -0.5)

    q = jnp.broadcast_to(q_scr[pl.ds(bi, 1), :], (SUBLANES, q_scr.shape[1])).astype(BF16)
    chunks = hd // LANES
    period = 2 * chunks * nh

    def head_rows(kv, h):
        return jnp.concatenate([kv_ref[0, pl.ds((kv * chunks + c) * nh + h, mt, stride=period), :]
                                for c in range(chunks)], axis=1).astype(BF16)

    outs = []
    for h in range(nh):
        s = _dot_nt(q[:, h * hd:(h + 1) * hd], head_rows(0, h))
        e = jnp.exp(s - jnp.max(s, -1, keepdims=True))
        p = e / jnp.sum(e, -1, keepdims=True)
        outs.append(_dot(p.astype(BF16), head_rows(1, h)))
    a_scr[pl.ds(bi, 1), :] = jnp.concatenate(outs, axis=1)[0:1, :]

    @pl.when(bi == nb - 1)
    def _():
        o_ref[...] = x_ref[...] + _dot(a_scr[...].astype(BF16), wo_ref[...])


def xattn_sample(x, g, wq, kv_rows, wo, layer, mt):
    bd, d = x.shape
    rows = kv_rows.shape[1]
    return pl.pallas_call(
        functools.partial(_xattn_sample_kernel, hd=d // X_HEADS, mt=mt), grid=(bd,),
        in_specs=[_full((bd, d)), _full((1, d)), _full(wq.shape),
                  pl.BlockSpec((1, rows, LANES), lambda bi: (layer * bd + bi, 0, 0)), _full(wo.shape)],
        out_specs=_full((bd, d)),
        out_shape=jax.ShapeDtypeStruct((bd, d), F32),
        scratch_shapes=[pltpu.VMEM((bd, wq.shape[1]), F32), pltpu.VMEM((bd, wq.shape[1]), F32)],
        compiler_params=_cparams(("arbitrary",), VMEM_LIMIT), name="xattn_sample",
    )(x, g.reshape(1, d), wq, kv_rows, wo)


def _ffn_kernel(x_ref, g_ref, wg_ref, wu_ref, wd_ref, o_ref, h_scr, acc_scr):
    c = pl.program_id(1)

    @pl.when(c == 0)
    def _():
        h_scr[...] = _rms(x_ref[...], g_ref[...]).astype(BF16)
        acc_scr[...] = x_ref[...]

    h = h_scr[...]
    gate = _dot(h, wg_ref[...])
    up = _dot(h, wu_ref[...])
    act = (gate * jax.nn.sigmoid(gate) * up).astype(BF16)
    acc_scr[...] += _dot(act, wd_ref[...])

    @pl.when(c == pl.num_programs(1) - 1)
    def _():
        o_ref[...] = acc_scr[...]


def _xattn_ffn_kernel(*refs, hd, n_in):
    x_ref, gx_ref, wq_ref, kv_ref, wo_ref, gf_ref, wg_ref, wu_ref, wd_ref = refs[:9]
    a_refs = refs[9:9 + n_in]
    w_refs = refs[9 + n_in:9 + 2 * n_in]
    o_ref, h_scr, acc_scr = refs[9 + 2 * n_in:]
    c = pl.program_id(1)

    @pl.when(c == 0)
    def _():
        x = x_ref[...]
        for a_ref, w_ref in zip(a_refs, w_refs):
            x = x + _dot(a_ref[...], w_ref[...])
        q = (_dot(_rms(x, gx_ref[...]).astype(BF16), wq_ref[...]) * (hd ** -0.5)).astype(BF16)
        x = x + _dot(_xattn_core(q, kv_ref[0], hd), wo_ref[...])
        h_scr[...] = _rms(x, gf_ref[...]).astype(BF16)
        acc_scr[...] = x

    h = h_scr[...]
    gate = _dot(h, wg_ref[...])
    up = _dot(h, wu_ref[...])
    act = (gate * jax.nn.sigmoid(gate) * up).astype(BF16)
    acc_scr[...] += _dot(act, wd_ref[...])

    @pl.when(c == pl.num_programs(1) - 1)
    def _():
        o_ref[...] = acc_scr[...]


def xattn_ffn(x, gx, wq, kvb, wo, acts, ws, gf, w_gu, w_dn, b, s):
    m, d = x.shape
    dff = w_dn.shape[0]
    mt = kvb.shape[0] // b
    tm = _row_tile(s, 512)
    per_seq = s // tm
    fc = _ff_chunk(dff, 1408)
    nch = dff // fc
    row = lambda n: pl.BlockSpec((tm, n), lambda i, c: (i, 0))
    return pl.pallas_call(
        functools.partial(_xattn_ffn_kernel, hd=d // X_HEADS, n_in=len(acts)), grid=(m // tm, nch),
        in_specs=[row(d), _full((1, d)), _full(wq.shape),
                  pl.BlockSpec((1, mt, kvb.shape[1]), lambda i, c: (i // per_seq, 0, 0)), _full(wo.shape),
                  _full((1, d)),
                  pl.BlockSpec((d, fc), lambda i, c: (0, c)),
                  pl.BlockSpec((d, fc), lambda i, c: (0, nch + c)),
                  pl.BlockSpec((fc, d), lambda i, c: (c, 0))]
        + [row(a.shape[1]) for a in acts] + [_full(w.shape) for w in ws],
        out_specs=row(d),
        out_shape=jax.ShapeDtypeStruct((m, d), F32),
        scratch_shapes=[pltpu.VMEM((tm, d), BF16), pltpu.VMEM((tm, d), F32)],
        compiler_params=_cparams(("parallel", "arbitrary"), VMEM_LIMIT), name="xattn_ffn",
    )(x, gx.reshape(1, d), wq, kvb.reshape(b, mt, -1), wo, gf.reshape(1, d), w_gu, w_gu, w_dn, *acts, *ws)


def _ff_chunk(dff, pref):
    c = dff
    for n in range(1, dff // LANES + 1):
        if dff % n == 0 and (dff // n) % LANES == 0 and dff // n <= pref:
            c = dff // n
            break
    return c


def ffn(x, g, w_gu, w_dn):
    m, d = x.shape
    dff = w_dn.shape[0]
    tm = _row_tile(m, 512)
    fc = _ff_chunk(dff, 1408)
    nch = dff // fc
    return pl.pallas_call(
        _ffn_kernel, grid=(m // tm, nch),
        in_specs=[pl.BlockSpec((tm, d), lambda i, c: (i, 0)), _full((1, d)),
                  pl.BlockSpec((d, fc), lambda i, c: (0, c)),
                  pl.BlockSpec((d, fc), lambda i, c: (0, nch + c)),
                  pl.BlockSpec((fc, d), lambda i, c: (c, 0))],
        out_specs=pl.BlockSpec((tm, d), lambda i, c: (i, 0)),
        out_shape=jax.ShapeDtypeStruct((m, d), F32),
        scratch_shapes=[pltpu.VMEM((tm, d), BF16), pltpu.VMEM((tm, d), F32)],
        compiler_params=_cparams(("parallel", "arbitrary"), VMEM_LIMIT), name="ffn",
    )(x, g.reshape(1, d), w_gu, w_gu, w_dn)


def _router_kernel(x_ref, g_ref, w_ref, b_ref, comb_ref, h_ref, mask_ref, cnt_ref, *, ne):
    h = _rms(x_ref[...], g_ref[...]).astype(BF16)
    h_ref[...] = h
    logits = _dot(h, w_ref[...]) + b_ref[...]
    lane = lax.broadcasted_iota(jnp.int32, logits.shape, 1)
    logits = jnp.where(lane < ne, logits, -jnp.inf)
    v1 = jnp.max(logits, -1, keepdims=True)
    i1 = jnp.min(jnp.where(logits == v1, lane, LANES), -1, keepdims=True)
    rest = jnp.where(lane == i1, -jnp.inf, logits)
    v2 = jnp.max(rest, -1, keepdims=True)
    i2 = jnp.min(jnp.where(rest == v2, lane, LANES), -1, keepdims=True)
    e2 = jnp.exp(v2 - v1)
    den = 1.0 + e2
    comb_ref[...] = jnp.where(lane == i1, 1.0 / den, 0.0) + jnp.where(lane == i2, e2 / den, 0.0)
    chosen = jnp.where((lane == i1) | (lane == i2), 1.0, 0.0)
    mask_ref[...] = chosen.astype(BF16)

    @pl.when(pl.program_id(0) == 0)
    def _():
        cnt_ref[...] = jnp.zeros_like(cnt_ref)

    cnt_ref[0:1, :] += jnp.sum(chosen, axis=0, keepdims=True)


def router(x, g, w_r, b_r):
    m, d = x.shape
    ne = w_r.shape[1]
    w_pad = jnp.zeros((d, LANES), BF16).at[:, :ne].set(w_r.astype(BF16))
    b_pad = jnp.zeros((1, LANES), F32).at[0, :ne].set(b_r.astype(F32))
    tm = _row_tile(m, 512)
    row = lambda n: pl.BlockSpec((tm, n), lambda i: (i, 0))
    return pl.pallas_call(
        functools.partial(_router_kernel, ne=ne), grid=(m // tm,),
        in_specs=[row(d), _full((1, d)), _full((d, LANES)), _full((1, LANES))],
        out_specs=[row(LANES), row(d), row(LANES), _full((SUBLANES, LANES))],
        out_shape=[jax.ShapeDtypeStruct((m, LANES), F32), jax.ShapeDtypeStruct((m, d), BF16),
                   jax.ShapeDtypeStruct((m, LANES), BF16), jax.ShapeDtypeStruct((SUBLANES, LANES), F32)],
        compiler_params=_cparams(("arbitrary",)), name="router",
    )(x, g.reshape(1, d), w_pad, b_pad)


def _residual_out(y, gain_ref, norm):
    return _rms(y, gain_ref[...]) if norm else y


def _moe_kernel(x_ref, h_ref, comb_ref, wg_ref, wu_ref, wd_ref, gain_ref, o_ref, acc_scr, *, norm):
    e = pl.program_id(1)

    @pl.when(e == 0)
    def _():
        acc_scr[...] = jnp.zeros_like(acc_scr)

    h = h_ref[...]
    gate = _dot(h, wg_ref[0])
    up = _dot(h, wu_ref[0])
    act = (gate * jax.nn.sigmoid(gate) * up).astype(BF16)
    y = _dot(act, wd_ref[0])
    comb = comb_ref[...]
    lane = lax.broadcasted_iota(jnp.int32, comb.shape, 1)
    acc_scr[...] += jnp.sum(jnp.where(lane == e, comb, 0.0), -1, keepdims=True) * y

    @pl.when(e == pl.num_programs(1) - 1)
    def _():
        o_ref[...] = _residual_out(x_ref[...] + acc_scr[...], gain_ref, norm)


def moe(x, h, comb, w_gu, w_dn, final_g=None):
    m, d = x.shape
    ne, dfe = w_dn.shape[:2]
    tm = _row_tile(m, 512)
    gain = jnp.ones((1, d), F32) if final_g is None else final_g.reshape(1, d)
    return pl.pallas_call(
        functools.partial(_moe_kernel, norm=final_g is not None), grid=(m // tm, ne),
        in_specs=[pl.BlockSpec((tm, d), lambda i, e: (i, 0)), pl.BlockSpec((tm, d), lambda i, e: (i, 0)),
                  pl.BlockSpec((tm, LANES), lambda i, e: (i, 0)),
                  pl.BlockSpec((1, d, dfe), lambda i, e: (e, 0, 0)),
                  pl.BlockSpec((1, d, dfe), lambda i, e: (e, 0, 1)),
                  pl.BlockSpec((1, dfe, d), lambda i, e: (e, 0, 0)), _full((1, d))],
        out_specs=pl.BlockSpec((tm, d), lambda i, e: (i, 0)),
        out_shape=jax.ShapeDtypeStruct((m, d), F32),
        scratch_shapes=[pltpu.VMEM((tm, d), F32)],
        compiler_params=_cparams(("parallel", "arbitrary"), VMEM_LIMIT), name="moe",
    )(x, h, comb, w_gu, w_gu, w_dn, gain)


MOE_TILE = 256


def _moe_pos_kernel(mask_ref, comb_ref, tri_ref, base_ref, post_ref, pos2_ref, wab_ref, stab_ref, run_scr, *, nep):
    sb = pl.program_id(0)
    nb = pl.num_programs(0)

    @pl.when(sb == 0)
    def _():
        run_scr[...] = jnp.zeros_like(run_scr)
        stab_ref[...] = jnp.zeros_like(stab_ref)

    a = mask_ref[...]
    af = a.astype(F32)
    start = base_ref[...] + run_scr[...]
    stab_ref[pl.ds(sb, 1), :] = start.astype(jnp.int32)
    rank = _dot(tri_ref[...], a)
    pos = jnp.where(af > 0, start + rank, -1.0)
    lane = lax.broadcasted_iota(jnp.int32, pos.shape, 1)
    first_e = jnp.min(jnp.where(af > 0, lane, LANES), -1, keepdims=True)
    last_e = jnp.max(jnp.where(af > 0, lane, -1), -1, keepdims=True)
    comb = comb_ref[...]
    w_a = jnp.sum(jnp.where(lane == first_e, comb, 0.0), -1, keepdims=True)
    w_b = jnp.sum(jnp.where(lane == last_e, comb, 0.0), -1, keepdims=True)
    wab_ref[...] = jnp.where(lane == 0, w_a, jnp.where(lane == 1, w_b, 0.0))
    pos_t = pos.T[0:nep, :]
    post_ref[0] = pos_t.astype(jnp.int32)
    row = lax.broadcasted_iota(jnp.int32, pos_t.shape, 0)
    first_r = jnp.min(jnp.where(pos_t >= 0, row, nep), 0, keepdims=True)
    last_r = jnp.max(jnp.where(pos_t >= 0, row, -1), 0, keepdims=True)
    pos_a = jnp.sum(jnp.where(row == first_r, pos_t, 0.0), 0, keepdims=True)
    pos_b = jnp.sum(jnp.where(row == last_r, pos_t, 0.0), 0, keepdims=True)
    pos2_ref[0] = jnp.where(row == 0, pos_a, jnp.where(row == 1, pos_b, 0.0)).astype(jnp.int32)
    run_scr[...] += jnp.sum(af, axis=0, keepdims=True)

    @pl.when(sb == nb - 1)
    def _():
        stab_ref[pl.ds(nb, 1), :] = (base_ref[...] + run_scr[...]).astype(jnp.int32)


def _moe_expert_kernel(te_ref, lo_ref, cnt_ref, ring_ref, nt_ref, h_ref, post_ref, wg_ref, wu_ref, wd_ref, y_ref,
                       hbuf, xg_scr, sem, *, t):
    i = pl.program_id(0)

    @pl.when(i >= nt_ref[0])
    def _():
        y_ref[...] = jnp.zeros_like(y_ref)

    nbuf = hbuf.shape[0]
    ahead = nbuf - 1

    def copy(sb, slot):
        return pltpu.make_async_copy(h_ref.at[pl.ds(pl.multiple_of(sb * t, t), t), :], hbuf.at[slot], sem.at[slot])

    def start_first(tile):
        for j in range(ahead):
            @pl.when(j < cnt_ref[tile])
            def _():
                copy(lo_ref[tile] + j, (ring_ref[tile] + j) % nbuf).start()

    @pl.when(i == 0)
    def _():
        start_first(0)

    @pl.when(i < nt_ref[0])
    def _():
        e = te_ref[i]
        lo = lo_ref[i]
        n = cnt_ref[i]
        ring = ring_ref[i]
        xg_scr[...] = jnp.zeros_like(xg_scr)
        row = i * t + lax.broadcasted_iota(jnp.int32, (t, 1), 0)

        def body(j, c):
            slot = (ring + j) % nbuf
            copy(lo + j, slot).wait()

            @pl.when(j + ahead < n)
            def _():
                copy(lo + j + ahead, (ring + j + ahead) % nbuf).start()

            src_pos = post_ref[lo + j, pl.ds(e, 1), :]
            onehot = jnp.where(src_pos == row, 1.0, 0.0).astype(BF16)
            xg_scr[...] += _dot(onehot, hbuf[slot])
            return c

        lax.fori_loop(0, n, body, 0)

        @pl.when(i + 1 < nt_ref[0])
        def _():
            start_first(i + 1)

        x = xg_scr[...].astype(BF16)
        gate = _dot(x, wg_ref[0])
        up = _dot(x, wu_ref[0])
        act = (gate * jax.nn.sigmoid(gate) * up).astype(BF16)
        y_ref[...] = _dot(act, wd_ref[0])


def _moe_combine_kernel(x_ref, wab_ref, pos_ref, nxt_ref, gain_ref, ys_ref, o_ref, ybuf, sem, *, t, norm):
    sb = pl.program_id(0)
    nb = pl.num_programs(0)

    def row_copy(p_ref, which, slot, tok):
        return pltpu.make_async_copy(ys_ref.at[pl.ds(p_ref[0, which, tok], 1), :],
                                     ybuf.at[slot, which, pl.ds(tok, 1), :], sem.at[slot])

    def start_all(p_ref, slot):
        def body(tok, c):
            row_copy(p_ref, 0, slot, tok).start()
            row_copy(p_ref, 1, slot, tok).start()
            return c
        lax.fori_loop(0, t, body, 0, unroll=8)

    @pl.when(sb == 0)
    def _():
        start_all(pos_ref, 0)

    @pl.when(sb + 1 < nb)
    def _():
        start_all(nxt_ref, (sb + 1) % 2)

    slot = sb % 2

    def wait_body(tok, c):
        row_copy(pos_ref, 0, slot, tok).wait()
        row_copy(pos_ref, 1, slot, tok).wait()
        return c
    lax.fori_loop(0, t, wait_body, 0, unroll=8)
    w = wab_ref[...]
    y = x_ref[...] + w[:, 0:1] * ybuf[slot, 0] + w[:, 1:2] * ybuf[slot, 1]
    o_ref[...] = _residual_out(y, gain_ref, norm)


def moe_grouped(x, h, comb, mask, counts, w_gu, w_dn, final_g=None):
    m, d = x.shape
    ne, dfe = w_dn.shape[:2]
    t = MOE_TILE
    assert m % t == 0
    nb = m // t
    nep = _round_up(ne, SUBLANES)
    nbp = _round_up(nb + 1, SUBLANES)
    k_top = TOP_K
    nt_max = k_top * m // t + ne
    cnt = counts[0, :ne].astype(jnp.int32)
    cnt_pad = (cnt + t - 1) // t * t
    ends = jnp.cumsum(cnt_pad)
    base = ends - cnt_pad
    base_row = jnp.zeros((1, LANES), F32).at[0, :ne].set(base.astype(F32))
    idx = lax.broadcasted_iota(jnp.int32, (t, t), 0)
    tri = jnp.where(lax.broadcasted_iota(jnp.int32, (t, t), 1) < idx, 1.0, 0.0).astype(BF16)
    blk = lambda n2: pl.BlockSpec((t, n2), lambda i: (i, 0))
    post, pos2, wab, stab = pl.pallas_call(
        functools.partial(_moe_pos_kernel, nep=nep), grid=(nb,),
        in_specs=[blk(LANES), blk(LANES), _full((t, t)), _full((1, LANES))],
        out_specs=[pl.BlockSpec((1, nep, t), lambda i: (i, 0, 0)), pl.BlockSpec((1, nep, t), lambda i: (i, 0, 0)),
                   blk(LANES), _full((nbp, LANES))],
        out_shape=[jax.ShapeDtypeStruct((nb, nep, t), jnp.int32), jax.ShapeDtypeStruct((nb, nep, t), jnp.int32),
                   jax.ShapeDtypeStruct((m, LANES), F32), jax.ShapeDtypeStruct((nbp, LANES), jnp.int32)],
        scratch_shapes=[pltpu.VMEM((1, LANES), F32)],
        compiler_params=_cparams(("arbitrary",)), name="moe_positions",
    )(mask, comb, tri, base_row)
    r0 = jnp.arange(nt_max, dtype=jnp.int32) * t
    tile_e = jnp.minimum(jnp.sum(ends[None, :] <= r0[:, None], axis=1), ne - 1).astype(jnp.int32)
    n_tiles = (ends[-1] // t).astype(jnp.int32).reshape(1)
    s_e = stab[:nb + 1, :ne][:, tile_e]
    lo = jnp.sum(s_e[1:] <= r0[None, :], axis=0)
    hi = jnp.sum(s_e[:nb] < r0[None, :] + t, axis=0) - 1
    lo = jnp.clip(lo, 0, nb - 1).astype(jnp.int32)
    hi = jnp.clip(hi, lo, nb - 1).astype(jnp.int32)
    n_src = jnp.where(jnp.arange(nt_max) < n_tiles[0], hi - lo + 1, 0).astype(jnp.int32)
    n_ring = 6
    ring = ((jnp.cumsum(n_src) - n_src) % n_ring).astype(jnp.int32)
    w_spec = lambda shape, col: pl.BlockSpec(shape, lambda i, te, *_: (te[i], 0, col))
    grid_spec = pltpu.PrefetchScalarGridSpec(
        num_scalar_prefetch=5, grid=(nt_max,),
        in_specs=[pl.BlockSpec(memory_space=pl.ANY),
                  pl.BlockSpec((nb, nep, t), lambda i, *_: (0, 0, 0)),
                  w_spec((1, d, dfe), 0), w_spec((1, d, dfe), 1), w_spec((1, dfe, d), 0)],
        out_specs=pl.BlockSpec((t, d), lambda i, *_: (i, 0)),
        scratch_shapes=[pltpu.VMEM((n_ring, t, d), BF16), pltpu.VMEM((t, d), F32),
                        pltpu.SemaphoreType.DMA((n_ring,))])
    ys = pl.pallas_call(
        functools.partial(_moe_expert_kernel, t=t), grid_spec=grid_spec,
        out_shape=jax.ShapeDtypeStruct((nt_max * t, d), F32),
        compiler_params=_cparams(("arbitrary",), VMEM_LIMIT), name="moe_experts",
    )(tile_e, lo, n_src, ring, n_tiles, h, post, w_gu, w_gu, w_dn)
    smem_blk = lambda f: pl.BlockSpec((1, nep, t), f, memory_space=pltpu.SMEM)
    gain = jnp.ones((1, d), F32) if final_g is None else final_g.reshape(1, d)
    return pl.pallas_call(
        functools.partial(_moe_combine_kernel, t=t, norm=final_g is not None), grid=(nb,),
        in_specs=[blk(d), blk(LANES), smem_blk(lambda i: (i, 0, 0)),
                  smem_blk(lambda i: (jnp.minimum(i + 1, nb - 1), 0, 0)), _full((1, d)),
                  pl.BlockSpec(memory_space=pl.ANY)],
        out_specs=blk(d),
        out_shape=jax.ShapeDtypeStruct((m, d), F32),
        scratch_shapes=[pltpu.VMEM((2, 2, t, d), F32), pltpu.SemaphoreType.DMA((2,))],
        compiler_params=_cparams(("arbitrary",), VMEM_LIMIT), name="moe_combine",
    )(x, wab, pos2, pos2, gain, ys)


def _inproj_odd_kernel(x_ref, g_ref, w_ref, bias_ref, q_ref, k_ref, v_ref, og_ref, gc_ref, gr_ref, *, hq, hv, nh):
    xn = _rms(x_ref[...], g_ref[...]).astype(BF16)

    def mm(lo, hi):
        return _dot(xn, w_ref[:, lo:hi])

    dk = hq // nh
    q_ref[...] = mm(0, hq).astype(BF16)
    k_ref[...] = (mm(hq, 2 * hq) * (dk ** -0.5)).astype(BF16)
    v_ref[...] = mm(2 * hq, 2 * hq + hv).astype(BF16)
    og_ref[...] = jax.nn.sigmoid(mm(2 * hq + hv, 2 * hq + 2 * hv))
    gi = mm(2 * hq + 2 * hv, 2 * hq + 2 * hv + LANES) + bias_ref[...]
    lane = lax.broadcasted_iota(jnp.int32, gi.shape, 1)
    gates = jnp.where(lane < nh, gi, jax.nn.log_sigmoid(gi))
    gc_ref[...] = gates
    gr_ref[...] = gates.T[0:SUBLANES, :]


def inproj_odd(x, g, w_pad, gate_bias):
    m, d = x.shape
    nh = MLSTM_HEADS
    hq = hv = d
    tm = _row_tile(m, 256)
    row = lambda n: pl.BlockSpec((tm, n), lambda i: (i, 0))
    outs = [(hq, BF16), (hq, BF16), (hv, BF16), (hv, F32), (LANES, F32)]
    if tm % LANES:
        gr_spec = _full((SUBLANES, m))
    else:
        gr_spec = pl.BlockSpec((SUBLANES, tm), lambda i: (0, i))
    return pl.pallas_call(
        functools.partial(_inproj_odd_kernel, hq=hq, hv=hv, nh=nh), grid=(m // tm,),
        in_specs=[row(d), _full((1, d)), _full(w_pad.shape), _full((1, LANES))],
        out_specs=[row(n) for n, _ in outs] + [gr_spec],
        out_shape=[jax.ShapeDtypeStruct((m, n), t) for n, t in outs] + [jax.ShapeDtypeStruct((SUBLANES, m), F32)],
        compiler_params=_cparams(("parallel",), VMEM_LIMIT), name="inproj_odd",
    )(x, g.reshape(1, d), w_pad, gate_bias)


def prep_odd(w_in, b_i, b_f):
    d, n = w_in.shape
    n_pad = _round_up(n - 2 * MLSTM_HEADS, LANES) + LANES
    w_pad = jnp.zeros((d, n_pad), BF16).at[:, :n].set(w_in.astype(BF16))
    bias = jnp.zeros((1, LANES), F32).at[0, :2 * MLSTM_HEADS].set(jnp.concatenate([b_i, b_f]).astype(F32))
    return dict(w_in=w_pad, bias=bias)


def _mlstm_prompt_kernel(q_ref, k_ref, v_ref, og_ref, gc_ref, gr_ref, gain_ref, hn_ref, c_ref, n_ref, m_ref,
                         *, nh, dk, dv, ln):
    ci = pl.program_id(1)

    @pl.when(ci == 0)
    def _():
        c_ref[...] = jnp.zeros_like(c_ref)
        n_ref[...] = jnp.zeros_like(n_ref)
        m_ref[...] = jnp.full(m_ref.shape, NEG, F32)

    row = lax.broadcasted_iota(jnp.int32, (ln, ln), 0)
    col = lax.broadcasted_iota(jnp.int32, (ln, ln), 1)
    tri = row >= col
    gc = gc_ref[...]
    gr = gr_ref[...]
    for h in range(nh):
        q = q_ref[:, h * dk:(h + 1) * dk]
        k = k_ref[:, h * dk:(h + 1) * dk]
        v = v_ref[:, h * dv:(h + 1) * dv]
        ig_c, lf_c = gc[:, h:h + 1], gc[:, nh + h:nh + h + 1]
        ig_r, lf_r = gr[h:h + 1, :], gr[nh + h:nh + h + 1, :]
        b_c = jnp.sum(jnp.where(tri, lf_r, 0.0), axis=1, keepdims=True)
        b_r = jnp.sum(jnp.where(row <= col, lf_c, 0.0), axis=0, keepdims=True)
        m_prev = m_ref[0, h:h + 1, 0:1]
        c_prev = c_ref[0, h]
        n_prev = n_ref[0, h:h + 1, :]
        dmat = jnp.where(tri, b_c - b_r + ig_r, NEG)
        inter = b_c + m_prev
        mt = jnp.maximum(inter, jnp.max(dmat, -1, keepdims=True))
        wm = jnp.exp(dmat - mt)
        a = jnp.exp(inter - mt)
        wqk = wm * _dot_nt(q, k)
        num = a * _dot_nt(q, c_prev.astype(BF16)) + _dot(wqk.astype(BF16), v)
        den = a * jnp.sum(q.astype(F32) * n_prev, -1, keepdims=True) + jnp.sum(wqk, -1, keepdims=True)
        hh = num / jnp.maximum(jnp.abs(den), jnp.exp(-mt))
        b_end = b_c[ln - 1:ln, :]
        m_new = mt[ln - 1:ln, :]
        a_end = jnp.exp(b_end + m_prev - m_new)
        w_s = jnp.exp(b_end - b_c + ig_c - m_new)
        c_ref[0, h] = a_end * c_prev + _dot_tn((v.astype(F32) * w_s).astype(BF16), k)
        n_ref[0, h:h + 1, :] = a_end * n_prev + jnp.sum(w_s * k.astype(F32), axis=0, keepdims=True)
        m_ref[0, h:h + 1, :] = jnp.broadcast_to(m_new, (1, m_ref.shape[2]))
        hn = hh * lax.rsqrt(jnp.mean(hh * hh, -1, keepdims=True) + RMS_EPS)
        hn = hn * gain_ref[:, h * dv:(h + 1) * dv] * og_ref[:, h * dv:(h + 1) * dv]
        hn_ref[:, h * dv:(h + 1) * dv] = hn.astype(hn_ref.dtype)


def mlstm_prompt(q, k, v, og, gc, gr, gain, b, s):
    m, d = q.shape
    nh = MLSTM_HEADS
    dk = dv = d // nh
    ln = _row_tile(s, MLSTM_CHUNK)
    nch = s // ln
    row = lambda n: pl.BlockSpec((ln, n), lambda bi, ci: (bi * nch + ci, 0))
    return pl.pallas_call(
        functools.partial(_mlstm_prompt_kernel, nh=nh, dk=dk, dv=dv, ln=ln), grid=(b, nch),
        in_specs=[row(d), row(d), row(d), row(d), row(LANES),
                  pl.BlockSpec((SUBLANES, ln), lambda bi, ci: (0, bi * nch + ci)), _full((1, d))],
        out_specs=[row(d), pl.BlockSpec((1, nh, dv, dk), lambda bi, ci: (bi, 0, 0, 0)),
                   pl.BlockSpec((1, nh, dk), lambda bi, ci: (bi, 0, 0)),
                   pl.BlockSpec((1, nh, LANES), lambda bi, ci: (bi, 0, 0))],
        out_shape=[jax.ShapeDtypeStruct((m, d), BF16), jax.ShapeDtypeStruct((b, nh, dv, dk), F32),
                   jax.ShapeDtypeStruct((b, nh, dk), F32), jax.ShapeDtypeStruct((b, nh, LANES), F32)],
        compiler_params=_cparams(("parallel", "arbitrary"), VMEM_LIMIT), name="mlstm_prompt",
    )(q, k, v, og, gc, gr, gain.reshape(1, d))


def _mlstm_sample_kernel(q_ref, k_ref, v_ref, og_ref, g_ref, gain_ref, c_ref, n_ref, m_ref,
                         hn_ref, co_ref, no_ref, mo_ref, *, nh):
    row = lax.broadcasted_iota(jnp.int32, (SUBLANES, 1), 0)
    for h in range(nh):
        q = q_ref[0, h:h + 1, :]
        k = k_ref[0, h:h + 1, :]
        v = v_ref[0, h:h + 1, :].astype(F32)
        ig = g_ref[0, h:h + 1, 0:1]
        lf = g_ref[0, h:h + 1, 1:2]
        m_prev = m_ref[0, h:h + 1, :]
        c_prev = c_ref[0, h]
        n_prev = n_ref[0, h:h + 1, :]
        inter = lf + m_prev
        mt = jnp.maximum(inter, ig)
        wm = jnp.exp(ig - mt)
        a = jnp.exp(inter - mt)
        q8 = jnp.broadcast_to(q, (SUBLANES, q.shape[1]))
        cq = _dot_nt(q8, c_prev.astype(BF16))[0:1, :]
        wqk = wm * jnp.sum(q.astype(F32) * k.astype(F32), -1, keepdims=True)
        num = a * cq + wqk * v
        den = a * jnp.sum(n_prev * q.astype(F32), -1, keepdims=True) + wqk
        hh = num / jnp.maximum(jnp.abs(den), jnp.exp(-mt))
        v8 = jnp.where(row == 0, jnp.broadcast_to(v * wm, (SUBLANES, v.shape[1])), 0.0).astype(BF16)
        k8 = jnp.broadcast_to(k, (SUBLANES, k.shape[1]))
        co_ref[0, h] = a * c_prev + _dot_tn(v8, k8)
        no_ref[0, h:h + 1, :] = a * n_prev + wm * k.astype(F32)
        mo_ref[0, h:h + 1, :] = mt
        hn = hh * lax.rsqrt(jnp.mean(hh * hh, -1, keepdims=True) + RMS_EPS)
        hn_ref[0, h:h + 1, :] = (hn * gain_ref[h:h + 1, :] * og_ref[0, h:h + 1, :]).astype(hn_ref.dtype)


def mlstm_sample(q, k, v, og, gc, gain, c, n, m):
    bd, d = q.shape
    nh = MLSTM_HEADS
    dk = d // nh
    heads = lambda a: a.reshape(bd, nh, dk)
    g2 = jnp.transpose(gc[:, :2 * nh].reshape(bd, 2, nh), (0, 2, 1))
    blk3 = lambda n2: pl.BlockSpec((1, nh, n2), lambda bi: (bi, 0, 0))
    cspec = pl.BlockSpec((1, nh, dk, dk), lambda bi: (bi, 0, 0, 0))
    hn, co, no, mo = pl.pallas_call(
        functools.partial(_mlstm_sample_kernel, nh=nh), grid=(bd,),
        in_specs=[blk3(dk), blk3(dk), blk3(dk), blk3(dk), blk3(2), _full((nh, dk)), cspec, blk3(dk), blk3(1)],
        out_specs=[blk3(dk), cspec, blk3(dk), blk3(1)],
        out_shape=[jax.ShapeDtypeStruct((bd, nh, dk), BF16), jax.ShapeDtypeStruct(c.shape, F32),
                   jax.ShapeDtypeStruct(n.shape, F32), jax.ShapeDtypeStruct((bd, nh, 1), F32)],
        compiler_params=_cparams(("parallel",)), name="mlstm_sample",
    )(heads(q), heads(k), heads(v), heads(og), g2, gain.reshape(nh, dk), c, n, m.reshape(bd, nh, 1))
    return hn.reshape(bd, d), co, no, mo.reshape(bd, nh)


def _row_to_col(row):
    n = row.shape[1]
    eye = lax.broadcasted_iota(jnp.int32, (n, n), 0) == lax.broadcasted_iota(jnp.int32, (n, n), 1)
    return jnp.sum(jnp.where(eye, row, 0.0), axis=1, keepdims=True)


def _head_pad(q, keep):
    q2 = jnp.concatenate([q] * NSA_KV_HEADS, axis=1)
    row = lax.broadcasted_iota(jnp.int32, q2.shape, 0)
    lane = lax.broadcasted_iota(jnp.int32, q2.shape, 1)
    return jnp.where((lane // NSA_HD == row // NSA_GROUP) & keep(row), q2, jnp.zeros_like(q2))


def _nsa_sample_cmp_kernel(pt_ref, q_ref, pages_ref, pe_ref, w_ref, biasc_ref, oc_ref, idx_ref, xbuf, xrow, sem,
                           *, n_pages, nc, ncp, nsb, n_sel, past):
    b = pl.program_id(0)
    nb = pl.num_programs(0)
    kvd = NSA_KV_HEADS * NSA_HD

    def page_copy(bb, slot, p, sl):
        return pltpu.make_async_copy(pages_ref.at[pt_ref[bb * n_pages + p], pl.ds(sl * kvd, kvd), :],
                                     xbuf.at[slot, sl, p], sem.at[slot])

    def start_all(bb, slot):
        def body(p, c):
            page_copy(bb, slot, p, 0).start()
            page_copy(bb, slot, p, 1).start()
            return c
        lax.fori_loop(0, n_pages, body, 0)

    @pl.when(b == 0)
    def _():
        start_all(0, 0)

    @pl.when(b + 1 < nb)
    def _():
        start_all(b + 1, (b + 1) % 2)

    slot = b % 2

    def wait_body(p, c):
        page_copy(b, slot, p, 0).wait()
        page_copy(b, slot, p, 1).wait()
        return c
    lax.fori_loop(0, n_pages, wait_body, 0)

    def file_page(p, c):
        for sl in range(2):
            _file_rows(xbuf[slot, sl, p], xrow, sl, p)
        return c
    lax.fori_loop(0, n_pages, file_page, 0, unroll=8)

    acc = _compress_filed(xrow, pe_ref, w_ref, nc)
    kc = acc[:, 0:kvd].astype(BF16)
    vc = acc[:, kvd:2 * kvd].astype(BF16)
    q = q_ref[0]
    nh = q.shape[0]
    qpad = _head_pad(q, lambda r: r >= 0)
    s = _dot_nt(qpad, kc)
    s = s + biasc_ref[:, 0:nc]
    e = jnp.exp(s - jnp.max(s, -1, keepdims=True))
    p_c = e / jnp.sum(e, -1, keepdims=True)
    o = _dot(p_c.astype(BF16), vc)
    row = lax.broadcasted_iota(jnp.int32, (nh, NSA_HD), 0)
    o_h = o[:, 0:NSA_HD]
    for k in range(1, NSA_KV_HEADS):
        o_h = jnp.where(row // NSA_GROUP == k, o[:, k * NSA_HD:(k + 1) * NSA_HD], o_h)
    oc_ref[0] = o_h
    prow = lax.broadcasted_iota(jnp.int32, p_c.shape, 0)
    lane = lax.broadcasted_iota(jnp.int32, (1, ncp), 1)
    blk = lane // 2
    cur = past // SEL_BLOCK
    forced = (blk == 0) | (blk == cur) | (blk == cur - 1)
    is_cand = ((lane % 2) == 0) & (lane < 2 * nsb)
    nselp = idx_ref.shape[1]
    rsel = lax.broadcasted_iota(jnp.int32, (nselp, 1), 0).astype(F32)
    out_lane = lax.broadcasted_iota(jnp.int32, (nselp, LANES), 1)
    result = jnp.full((nselp, LANES), -1, jnp.int32)
    for k in range(NSA_KV_HEADS):
        imp = jnp.sum(jnp.where(prow // NSA_GROUP == k, p_c, 0.0), axis=0, keepdims=True)
        imp = jnp.concatenate([imp, jnp.zeros((1, ncp - nc), F32)], axis=1)
        imp = _pair_sum(imp)
        score = jnp.where(forced, FORCE_SCORE, jnp.where(blk <= cur, imp, -1.0))
        score = jnp.where(is_cand, score, -2.0)
        sel, rank = _select_blocks(score, n_sel, nsb)
        hit = (rank == rsel) & (sel > 0.5)
        idx = jnp.sum(jnp.where(hit, (blk + 1).astype(F32), 0.0), axis=1, keepdims=True) - 1.0
        result = jnp.where(out_lane == k, idx.astype(jnp.int32), result)
    idx_ref[0] = result


def nsa_sample_cmp(q8, pages, page_table, pe2, wbd, rel_bias):
    bd, nh, hd = q8.shape
    n_pages = page_table.shape[1]
    past = n_pages * PAGE_SIZE
    nc = past // CMP_BLOCK
    nsb = -(-(past + 1) // SEL_BLOCK)
    n_sel = min(TOP_N, nsb)
    ncp = _round_up(max(nc, SEL_RATIO * nsb), LANES)
    nselp = _round_up(n_sel, SUBLANES)
    kvd = wbd.shape[-1]
    cend = jnp.arange(nc) * CMP_BLOCK + (CMP_BLOCK - 1)
    biasc = jnp.zeros((nh, ncp), F32).at[:, :nc].set(_bias_of(rel_bias, past - cend).T)
    grid_spec = pltpu.PrefetchScalarGridSpec(
        num_scalar_prefetch=1, grid=(bd,),
        in_specs=[pl.BlockSpec((1, nh, hd), lambda bi, pt: (bi, 0, 0)),
                  pl.BlockSpec(memory_space=pl.ANY),
                  pl.BlockSpec(pe2.shape, lambda bi, pt: (0, 0, 0)),
                  pl.BlockSpec(wbd.shape, lambda bi, pt: (0, 0, 0, 0)),
                  pl.BlockSpec((nh, ncp), lambda bi, pt: (0, 0))],
        out_specs=[pl.BlockSpec((1, nh, hd), lambda bi, pt: (bi, 0, 0)),
                   pl.BlockSpec((1, nselp, LANES), lambda bi, pt: (bi, 0, 0))],
        scratch_shapes=[pltpu.VMEM((2, 2, n_pages, kvd, PAGE_SIZE), F32),
                        pltpu.VMEM((2, CMP_BLOCK // SUBLANES, nc * SUBLANES, kvd), F32),
                        pltpu.SemaphoreType.DMA((2,))])
    oc, idx = pl.pallas_call(
        functools.partial(_nsa_sample_cmp_kernel, n_pages=n_pages, nc=nc, ncp=ncp, nsb=nsb, n_sel=n_sel, past=past),
        grid_spec=grid_spec,
        out_shape=[jax.ShapeDtypeStruct((bd, nh, hd), F32), jax.ShapeDtypeStruct((bd, nselp, LANES), jnp.int32)],
        compiler_params=_cparams(("arbitrary",), VMEM_LIMIT), name="nsa_sample_cmp",
    )(page_table.reshape(-1), q8, pages, pe2, wbd, biasc)
    sel_idx = jnp.transpose(idx[:, :n_sel, :NSA_KV_HEADS], (0, 2, 1))
    return oc, sel_idx


def _nsa_sample_att_kernel(pt_ref, si_ref, q_ref, g_ref, oc_ref, kvn_ref, wn_ref, wc_ref, pages_ref,
                           bsel_ref, bwin_ref, ob_ref, win_ref, selbuf, wall, sem,
                           *, n_pages, n_sel, past, wb):
    b = pl.program_id(0)
    nb = pl.num_programs(0)
    kvd = NSA_KV_HEADS * NSA_HD
    hd = NSA_HD
    n_blk_pages = past // SEL_BLOCK
    per_page = PAGE_SIZE // SEL_BLOCK
    n_slots = NSA_KV_HEADS * n_sel

    def blk_of(bb, j):
        return si_ref[bb * n_slots + j]

    def blk_copy(bb, slot, j):
        blk = jnp.clip(blk_of(bb, j), 0, n_blk_pages - 1)
        page = pt_ref[bb * n_pages + blk // per_page]
        return pltpu.make_async_copy(pages_ref.at[page, pl.ds(2 * kvd, 2 * kvd), :], selbuf.at[slot, j], sem.at[slot])

    def in_pages(bb, j):
        blk = blk_of(bb, j)
        return (blk >= 0) & (blk < n_blk_pages)

    def start_all(bb, slot):
        def body(j, c):
            @pl.when(in_pages(bb, j))
            def _():
                blk_copy(bb, slot, j).start()
            return c
        lax.fori_loop(0, n_slots, body, 0)

    @pl.when(b == 0)
    def _():
        start_all(0, 0)

    @pl.when(b + 1 < nb)
    def _():
        start_all(b + 1, (b + 1) % 2)

    slot = b % 2
    new_sel = _row_to_col(kvn_ref[0][:, 2 * kvd:4 * kvd])
    lane = lax.broadcasted_iota(jnp.int32, (1, PAGE_SIZE), 1)

    def wait_body(j, c):
        @pl.when(in_pages(b, j))
        def _():
            blk_copy(b, slot, j).wait()

        @pl.when(jnp.logical_not(in_pages(b, j)))
        def _():
            is_new = blk_of(b, j) == n_blk_pages
            selbuf[slot, j] = jnp.where((lane == 0) & is_new, new_sel, 0.0)
        return c
    lax.fori_loop(0, n_slots, wait_body, 0)

    q = q_ref[0]
    nh = q.shape[0]
    gates = g_ref[0]
    head = lax.broadcasted_iota(jnp.int32, (nh, 1), 0)

    def attend(qp, keys_t, vals_t, bias, valid):
        s = jnp.where(valid, _dot(qp, keys_t) + bias, NEG)
        e = jnp.where(valid, jnp.exp(s - jnp.max(s, -1, keepdims=True)), 0.0)
        den = jnp.sum(e, -1, keepdims=True)
        p = e / jnp.where(den > 0, den, 1.0)
        return _dot_nt(p.astype(BF16), vals_t)

    o_s = jnp.zeros((nh, kvd), F32)
    for k in range(NSA_KV_HEADS):
        blks = [blk_of(b, k * n_sel + r) for r in range(n_sel)]
        tiles = [selbuf[slot, k * n_sel + r] for r in range(n_sel)]
        keys_t = jnp.concatenate([t_[0:kvd, :] for t_ in tiles], axis=1).astype(BF16)
        vals_t = jnp.concatenate([t_[kvd:2 * kvd, :] for t_ in tiles], axis=1).astype(BF16)
        bias = jnp.concatenate([bsel_ref[jnp.clip(bl // per_page, 0, n_pages)] for bl in blks], axis=1)
        valid = jnp.concatenate(
            [(lane // SEL_BLOCK == bl % per_page) & ((bl // per_page) * PAGE_SIZE + lane <= past) & (bl >= 0)
             for bl in blks], axis=1)
        o_k = attend(_head_pad(q, lambda r: r // NSA_GROUP == k), keys_t, vals_t, bias, valid)
        o_s = jnp.where(head // NSA_GROUP == k, o_k, o_s)
    wlanes = wall.shape[1]
    wall[:, 0:wb] = wc_ref[0]
    tail = lax.broadcasted_iota(jnp.int32, (1, wlanes - wb), 1)
    wall[:, wb:wlanes] = jnp.where(tail == 0, _row_to_col(wn_ref[0]), 0.0)
    win_ref[0] = pltpu.roll(wall[...], wlanes - 1, 1)[:, 0:wb]
    w_pos = lax.broadcasted_iota(jnp.int32, (1, wlanes), 1)
    valid_w = (w_pos <= wb) & (wb - w_pos < WINDOW) & (past - wb + w_pos >= 0)
    o_w = attend(_head_pad(q, lambda r: r >= 0), wall[0:kvd, :].astype(BF16), wall[kvd:2 * kvd, :].astype(BF16),
                 bwin_ref[...], valid_w)
    o_c = jnp.concatenate([oc_ref[0]] * NSA_KV_HEADS, axis=1)
    mix = gates[:, 0:1] * o_c + gates[:, 1:2] * o_s + gates[:, 2:3] * o_w
    out = mix[:, 0:hd]
    for k in range(1, NSA_KV_HEADS):
        out = jnp.where(head // NSA_GROUP == k, mix[:, k * hd:(k + 1) * hd], out)
    ob_ref[0] = out.astype(ob_ref.dtype)


def nsa_sample_att(q8, gates, oc, kv03, kv45, wcache_t, layer, pages_t, page_table, sel_idx, rel_bias):
    bd, nh, hd = q8.shape
    n_pages = page_table.shape[1]
    past = n_pages * PAGE_SIZE
    wb = wcache_t.shape[2]
    kvd = NSA_KV_HEADS * hd
    n_sel = sel_idx.shape[2]
    wlanes = _round_up(wb + 1, LANES)
    g3 = gates[:, :3 * nh].reshape(bd, nh, 3)
    dist = past - (jnp.arange(n_pages + 1)[:, None] * PAGE_SIZE + jnp.arange(PAGE_SIZE)[None, :])
    bsel = jnp.transpose(_bias_of(rel_bias, dist), (0, 2, 1))
    bwin = _bias_of(rel_bias, wb - jnp.arange(wlanes)).T
    blk = lambda n2, n3: pl.BlockSpec((1, n2, n3), lambda bi, pt, si: (bi, 0, 0))
    grid_spec = pltpu.PrefetchScalarGridSpec(
        num_scalar_prefetch=2, grid=(bd,),
        in_specs=[blk(nh, hd), blk(nh, 3), blk(nh, hd), blk(1, 4 * kvd), blk(1, 2 * kvd),
                  pl.BlockSpec((1, 2 * kvd, wb), lambda bi, pt, si: (layer * bd + bi, 0, 0)),
                  pl.BlockSpec(memory_space=pl.ANY),
                  pl.BlockSpec(bsel.shape, lambda bi, pt, si: (0, 0, 0)),
                  pl.BlockSpec(bwin.shape, lambda bi, pt, si: (0, 0))],
        out_specs=[blk(nh, hd), blk(2 * kvd, wb)],
        scratch_shapes=[pltpu.VMEM((2, NSA_KV_HEADS * n_sel, 2 * kvd, PAGE_SIZE), F32),
                        pltpu.VMEM((2 * kvd, wlanes), F32), pltpu.SemaphoreType.DMA((2,))])
    ob, win = pl.pallas_call(
        functools.partial(_nsa_sample_att_kernel, n_pages=n_pages, n_sel=n_sel, past=past, wb=wb),
        grid_spec=grid_spec,
        out_shape=[jax.ShapeDtypeStruct((bd, nh, hd), BF16), jax.ShapeDtypeStruct((bd, 2 * kvd, wb), F32)],
        compiler_params=_cparams(("arbitrary",), VMEM_LIMIT), name="nsa_sample_att",
    )(page_table.reshape(-1), sel_idx.reshape(-1), q8, g3, oc, kv03.reshape(bd, 1, -1), kv45.reshape(bd, 1, -1),
      wcache_t, pages_t, bsel, bwin)
    return ob.reshape(bd, nh * hd), win


def kernel(x_prompt, x_sample, mem_prompt, cache_conv, cache_nsa_pages, cache_nsa_window, state_mlstm_c,
           state_mlstm_n, state_mlstm_m, cache_mem_kv, page_table, rel_bias, norm_mix, norm_xattn, norm_mem,
           norm_ffn, norm_final, w_in_even, w_out_even, conv_w, conv_b, conv_ln_g, conv_ln_b, nsa_cmp_pe,
           nsa_cmp_w, w_in_odd, mlstm_b_i, mlstm_b_f, mlstm_norm, w_out_odd, xattn_wq, xattn_wkv, xattn_wo,
           ffn_w_gu, ffn_w_dn, router_w, router_b, expert_w_gu, expert_w_dn):
    b, s, d = x_prompt.shape
    bd, td, _ = x_sample.shape
    assert td == 1, "the sample group decodes one token per sequence"
    depth = norm_mix.shape[0]
    mt = mem_prompt.shape[1]
    cc = conv_w.shape[2]
    hist = conv_w.shape[1] - 1
    wb = cache_nsa_window.shape[2]
    kvh, hd = NSA_KV_HEADS, NSA_HD
    n_pool = cache_nsa_pages.shape[1]
    assert s >= hist and s >= wb and s % Q_BLOCK == 0
    xp = x_prompt.reshape(b * s, d)
    xs = x_sample.reshape(bd, d)
    mem = mem_prompt.reshape(b * mt, d)
    pages_t = jnp.swapaxes(cache_nsa_pages.reshape(-1, PAGE_SIZE, 4 * kvh * hd), 1, 2)
    window_t = jnp.swapaxes(cache_nsa_window.reshape(-1, wb, 2 * kvh * hd), 1, 2)
    xhd = d // X_HEADS
    memkv_rows = jnp.swapaxes(cache_mem_kv.reshape(depth * bd, mt, 2, X_HEADS, xhd // LANES, LANES), 3, 4)
    memkv_rows = memkv_rows.reshape(depth * bd, -1, LANES)
    bf = lambda a: a.astype(BF16)
    conv_p, conv_s, nsa_p, nsa_s, win_p, win_s = [], [], [], [], [], []
    mc_p, mc_s, mn_p, mn_s, mm_p, mm_s, memkv_p = [], [], [], [], [], [], []
    for l in range(depth):
        li = l // 2
        if l % 2 == 0:
            prm = prep_even(w_in_even[li], nsa_cmp_pe[li], nsa_cmp_w[li])
            w_out = bf(w_out_even[li])
            w_parts = [w_out[:cc], w_out[cc:]]
            conv_args = (conv_w[li], conv_b[li], conv_ln_g[li], conv_ln_b[li])
            glu, keys, qt, vt, gt, kv_t = inproj_even(xp, norm_mix[l], prm['w_in'], cc, s)
            a_out = conv_prompt(glu, *conv_args, b, s)
            kc, vct = compress_prompt(kv_t, prm['pe2'], prm['wbd'], _round_up(s // CMP_BLOCK, LANES))
            b_out = nsa_prompt(qt, gt, kc, vct, keys, vt, rel_bias, b, s)
            mix_p = ([a_out, b_out], w_parts)
            conv_p.append(glu.reshape(b, s, cc)[:, s - hist:])
            rows_t = kv_t.reshape(b, 6, kvh, hd, s)
            nsa_p.append(jnp.transpose(rows_t[:, :4], (0, 4, 1, 2, 3)))
            win_p.append(jnp.transpose(rows_t[:, 4:, :, :, s - wb:], (0, 4, 1, 2, 3)))
            glu, kv03, kv45, q, gates = inproj_even(xs, norm_mix[l], prm['w_in'], cc)
            a_out, conv_state = conv_sample(cache_conv[li], glu, *conv_args)
            q8 = q.reshape(bd, NSA_HEADS, hd)
            pt = page_table + li * n_pool
            o_c, sel_idx = nsa_sample_cmp(q8, pages_t, pt, prm['pe2'], prm['wbd'], rel_bias)
            b_out, win = nsa_sample_att(q8, gates, o_c, kv03, kv45, window_t, li, pages_t, pt, sel_idx, rel_bias)
            xs = outproj(xs, [a_out, b_out], w_parts)
            conv_s.append(conv_state)
            nsa_s.append(kv03.reshape(bd, 1, 4, kvh, hd))
            win_s.append(jnp.transpose(win.reshape(bd, 2, kvh, hd, wb), (0, 4, 1, 2, 3)))
        else:
            prm = prep_odd(w_in_odd[li], mlstm_b_i[li], mlstm_b_f[li])
            w_out = bf(w_out_odd[li])
            q, k, v, og, gc, gr = inproj_odd(xp, norm_mix[l], prm['w_in'], prm['bias'])
            hn, c_new, n_new, m_new = mlstm_prompt(q, k, v, og, gc, gr, mlstm_norm[li], b, s)
            mix_p = ([hn], [w_out])
            mc_p.append(c_new)
            mn_p.append(n_new)
            mm_p.append(m_new[:, :, 0])
            q, k, v, og, gc, gr = inproj_odd(xs, norm_mix[l], prm['w_in'], prm['bias'])
            hn, c_new, n_new, m_new = mlstm_sample(q, k, v, og, gc, mlstm_norm[li], state_mlstm_c[li],
                                                   state_mlstm_n[li], state_mlstm_m[li])
            xs = outproj(xs, [hn], [w_out])
            mc_s.append(c_new)
            mn_s.append(n_new)
            mm_s.append(m_new)
        wq, wo = bf(xattn_wq[l]), bf(xattn_wo[l])
        mkv_rows, mkv_b = memkv(mem, norm_mem[l], bf(xattn_wkv[l]), X_HEADS)
        mkv = jnp.swapaxes(mkv_rows.reshape(b, mt, 2, xhd // LANES, X_HEADS, LANES), 3, 4)
        memkv_p.append(mkv.reshape(b, mt, 2, X_HEADS, xhd))
        xs = xattn_sample(xs, norm_xattn[l], wq, memkv_rows, wo, l, mt)
        if l % 2 == 0:
            w_gu, w_dn = bf(ffn_w_gu[li]), bf(ffn_w_dn[li])
            xp = xattn_ffn(xp, norm_xattn[l], wq, mkv_b, wo, *mix_p, norm_ffn[l], w_gu, w_dn, b, s)
            xs = ffn(xs, norm_ffn[l], w_gu, w_dn)
        else:
            xp = xattn_prompt(xp, norm_xattn[l], wq, mkv_b, wo, b, s, *mix_p)
            e_gu, e_dn = bf(expert_w_gu[li]), bf(expert_w_dn[li])
            final_g = norm_final if l == depth - 1 else None
            comb, h, mask, counts = router(xp, norm_ffn[l], router_w[li], router_b[li])
            xp = moe_grouped(xp, h, comb, mask, counts, e_gu, e_dn, final_g)
            comb, h, _, _ = router(xs, norm_ffn[l], router_w[li], router_b[li])
            xs = moe(xs, h, comb, e_gu, e_dn, final_g)
    if depth % 2:
        xp, xs = rmsnorm(xp, norm_final), rmsnorm(xs, norm_final)
    y_prompt = xp.reshape(b, s, d)
    y_sample = xs.reshape(bd, 1, d)
    return (y_prompt, y_sample, jnp.stack(conv_p), jnp.stack(conv_s), jnp.stack(nsa_p), jnp.stack(nsa_s),
            jnp.stack(win_p), jnp.stack(win_s), jnp.stack(mc_p), jnp.stack(mc_s), jnp.stack(mn_p),
            jnp.stack(mn_s), jnp.stack(mm_p), jnp.stack(mm_s), jnp.stack(memkv_p))
```

```python
import functools
import math

import jax
import jax.numpy as jnp
import numpy as np
from jax import lax
from jax.experimental import pallas as pl
from jax.experimental.pallas import tpu as pltpu

F32 = jnp.float32
BF16 = jnp.bfloat16

PAGE_SIZE = 128
CONV_WIDTH = 31
NSA_HEADS = 8
NSA_KV_HEADS = 2
NSA_GROUP = NSA_HEADS // NSA_KV_HEADS
NSA_HD = 64
CMP_BLOCK = 32
SEL_BLOCK = 64
SEL_RATIO = SEL_BLOCK // CMP_BLOCK
TOP_N = 16
WINDOW = 512
Q_BLOCK = 128
FORCE_SCORE = 1.0e4
NUM_BUCKETS = 32
MAX_DISTANCE = 1024
MLSTM_HEADS = 4
X_HEADS = 4
N_EXPERTS = 8
TOP_K = 2
RMS_EPS = 1e-6
LN_EPS = 1e-5
NEG = -1e30

LANES = 128
SUBLANES = 8
VMEM_LIMIT = 56 * 1024 * 1024
MLSTM_CHUNK = 256
SEL_TILES = 4
WIN_TILES = 3


def _cparams(sem, vmem=None):
    return pltpu.CompilerParams(dimension_semantics=sem, vmem_limit_bytes=vmem)


def _rms(x, g):
    return x * lax.rsqrt(jnp.mean(x * x, -1, keepdims=True) + RMS_EPS) * g


def _dot(a, b):
    return jnp.dot(a, b, preferred_element_type=F32)


def _dot_nt(a, b):
    return lax.dot_general(a, b, (((1,), (1,)), ((), ())), preferred_element_type=F32)


def _dot_tn(a, b):
    return lax.dot_general(a, b, (((0,), (0,)), ((), ())), preferred_element_type=F32)


def _full(shape):
    n = len(shape)
    return pl.BlockSpec(shape, lambda *_: (0,) * n)


def _row_tile(m, pref):
    t = min(pref, m)
    while m % t:
        t //= 2
    return t


def _rmsnorm_kernel(x_ref, g_ref, o_ref):
    o_ref[...] = _rms(x_ref[...], g_ref[...])


def rmsnorm(x, g):
    m, d = x.shape
    tm = _row_tile(m, 1024)
    return pl.pallas_call(
        _rmsnorm_kernel, grid=(m // tm,),
        in_specs=[pl.BlockSpec((tm, d), lambda i: (i, 0)), _full((1, d))],
        out_specs=pl.BlockSpec((tm, d), lambda i: (i, 0)),
        out_shape=jax.ShapeDtypeStruct((m, d), F32),
        compiler_params=_cparams(("parallel",)), name="rmsnorm",
    )(x, g.reshape(1, d))


def _outproj_kernel(*refs, n_in):
    x_ref = refs[0]
    a_refs = refs[1:1 + n_in]
    w_refs = refs[1 + n_in:1 + 2 * n_in]
    o_ref = refs[1 + 2 * n_in]
    acc = x_ref[...]
    for a_ref, w_ref in zip(a_refs, w_refs):
        acc = acc + _dot(a_ref[...], w_ref[...])
    o_ref[...] = acc


def outproj(x, acts, ws):
    m, d = x.shape
    tm = _row_tile(m, 512)
    n_in = len(acts)
    in_specs = [pl.BlockSpec((tm, d), lambda i: (i, 0))]
    in_specs += [pl.BlockSpec((tm, a.shape[1]), lambda i: (i, 0)) for a in acts]
    in_specs += [_full(w.shape) for w in ws]
    return pl.pallas_call(
        functools.partial(_outproj_kernel, n_in=n_in), grid=(m // tm,),
        in_specs=in_specs, out_specs=pl.BlockSpec((tm, d), lambda i: (i, 0)),
        out_shape=jax.ShapeDtypeStruct((m, d), F32),
        compiler_params=_cparams(("parallel",)), name="outproj",
    )(x, *acts, *ws)


def _inproj_even_kernel(x_ref, g_ref, w_ref, glu_ref, *rest, cc, qd, kvd, tiles):
    xn = _rms(x_ref[...], g_ref[...]).astype(BF16)

    def mm(lo, hi):
        return _dot(xn, w_ref[:, lo:hi])

    o = 0
    a = mm(o, o + cc)
    b = mm(o + cc, o + 2 * cc)
    glu_ref[...] = a * jax.nn.sigmoid(b)
    o += 2 * cc
    q = mm(o, o + qd) * (NSA_HD ** -0.5)
    o += qd
    kv03 = mm(o, o + 4 * kvd)
    o += 4 * kvd
    kv45 = mm(o, o + 2 * kvd)
    o += 2 * kvd
    gates = jax.nn.sigmoid(mm(o, o + LANES))
    if tiles == 0:
        kv03_ref, kv45_ref, q_ref, gate_ref = rest
        kv03_ref[...] = kv03
        kv45_ref[...] = kv45
        q_ref[...] = q.astype(BF16)
        gate_ref[...] = gates
        return
    keys_ref, qt_ref, vt_ref, gt_ref, kvt_ref = rest
    kvt_ref[0] = jnp.concatenate([kv03, kv45], axis=1).T
    keys_ref[...] = jnp.concatenate([kv03[:, 2 * kvd:3 * kvd], kv45[:, 0:kvd]], axis=1).astype(BF16)
    vals = jnp.concatenate([kv03[:, 3 * kvd:4 * kvd], kv45[:, kvd:2 * kvd]], axis=1)
    for j in range(tiles):
        rows = slice(j * Q_BLOCK, (j + 1) * Q_BLOCK)
        qt_ref[j] = q[rows, :].T.astype(BF16)
        vt_ref[j] = vals[rows, :].T.astype(BF16)
        gt_ref[j] = gates[rows, :].T


def inproj_even(x, g, w_pad, cc, seq=None):
    m, d = x.shape
    qd = NSA_HEADS * NSA_HD
    kvd = NSA_KV_HEADS * NSA_HD
    tm = _row_tile(m, 256)
    row = lambda n: pl.BlockSpec((tm, n), lambda i: (i, 0))
    out_specs = [row(cc)]
    out_shape = [jax.ShapeDtypeStruct((m, cc), F32)]
    transposed = seq is not None
    tiles = tm // Q_BLOCK if transposed else 0
    if transposed:
        assert tm % Q_BLOCK == 0 and seq % tm == 0
        per_seq = seq // tm
        tile = lambda n: pl.BlockSpec((tiles, n, Q_BLOCK), lambda i: (i, 0, 0))
        out_specs += [row(2 * kvd), tile(qd), tile(2 * kvd), tile(LANES),
                      pl.BlockSpec((1, 6 * kvd, tm), lambda i: (i // per_seq, 0, i % per_seq))]
        out_shape += [jax.ShapeDtypeStruct((m, 2 * kvd), BF16),
                      jax.ShapeDtypeStruct((m // Q_BLOCK, qd, Q_BLOCK), BF16),
                      jax.ShapeDtypeStruct((m // Q_BLOCK, 2 * kvd, Q_BLOCK), BF16),
                      jax.ShapeDtypeStruct((m // Q_BLOCK, LANES, Q_BLOCK), F32),
                      jax.ShapeDtypeStruct((m // seq, 6 * kvd, seq), F32)]
    else:
        out_specs += [row(4 * kvd), row(2 * kvd), row(qd), row(LANES)]
        out_shape += [jax.ShapeDtypeStruct((m, 4 * kvd), F32), jax.ShapeDtypeStruct((m, 2 * kvd), F32),
                      jax.ShapeDtypeStruct((m, qd), BF16), jax.ShapeDtypeStruct((m, LANES), F32)]
    return pl.pallas_call(
        functools.partial(_inproj_even_kernel, cc=cc, qd=qd, kvd=kvd, tiles=tiles), grid=(m // tm,),
        in_specs=[row(d), _full((1, d)), _full(w_pad.shape)],
        out_specs=out_specs, out_shape=out_shape,
        compiler_params=_cparams(("parallel",)), name="inproj_even",
    )(x, g.reshape(1, d), w_pad)


def _conv_post(y, lg, lb):
    mu = jnp.mean(y, -1, keepdims=True)
    var = jnp.mean(jnp.square(y - mu), -1, keepdims=True)
    yn = (y - mu) * lax.rsqrt(var + LN_EPS) * lg + lb
    return yn * jax.nn.sigmoid(yn)


CONV_SUB = 64
CONV_PAD = 32


def _conv_prompt_kernel(glu_ref, cw_ref, cb_ref, lg_ref, lb_ref, o_ref, ext_ref, y_ref, *, ts, s):
    i = pl.program_id(1)
    c = glu_ref.shape[-1]

    @pl.when(i == 0)
    def _():
        ext_ref[0:CONV_PAD, :] = jnp.zeros((CONV_PAD, c), F32)
        ext_ref[CONV_PAD:CONV_PAD + s, :] = glu_ref[0]
        ext_ref[CONV_PAD + s:CONV_PAD + s + SUBLANES, :] = jnp.zeros((SUBLANES, c), F32)

    lead = CONV_PAD - (CONV_WIDTH - 1)
    span = CONV_SUB + CONV_PAD

    def sub(j, carry):
        r0 = pl.multiple_of(i * ts + j * CONV_SUB, CONV_SUB)
        for c0 in range(0, c, LANES):
            xw = ext_ref[pl.ds(r0, span + SUBLANES), c0:c0 + LANES]
            acc = jnp.zeros((CONV_SUB, LANES), F32) + cb_ref[:, c0:c0 + LANES]
            for r in range(SUBLANES):
                xr = xw if r == 0 else pltpu.roll(xw, span + SUBLANES - r, 0)
                for a in range(span // SUBLANES):
                    w = SUBLANES * a + r - lead
                    if 0 <= w < CONV_WIDTH:
                        acc = acc + xr[SUBLANES * a:SUBLANES * a + CONV_SUB, :] * cw_ref[w:w + 1, c0:c0 + LANES]
            y_ref[:, c0:c0 + LANES] = acc
        o_ref[0, pl.ds(pl.multiple_of(j * CONV_SUB, CONV_SUB), CONV_SUB), :] = _conv_post(
            y_ref[...], lg_ref[...], lb_ref[...]).astype(o_ref.dtype)
        return carry

    lax.fori_loop(0, ts // CONV_SUB, sub, 0)


def conv_prompt(glu, cw, cb, lg, lb, b, s):
    c = glu.shape[-1]
    ts = _row_tile(s, 256)
    vec = lambda a: a.reshape(1, c)
    out = pl.pallas_call(
        functools.partial(_conv_prompt_kernel, ts=ts, s=s), grid=(b, s // ts),
        in_specs=[pl.BlockSpec((1, s, c), lambda bi, i: (bi, 0, 0)), _full((CONV_WIDTH, c)),
                  _full((1, c)), _full((1, c)), _full((1, c))],
        out_specs=pl.BlockSpec((1, ts, c), lambda bi, i: (bi, i, 0)),
        out_shape=jax.ShapeDtypeStruct((b, s, c), BF16),
        scratch_shapes=[pltpu.VMEM((CONV_PAD + s + SUBLANES, c), F32), pltpu.VMEM((CONV_SUB, c), F32)],
        compiler_params=_cparams(("parallel", "arbitrary")), name="conv_prompt",
    )(glu.reshape(b, s, c), cw, vec(cb), vec(lg), vec(lb))
    return out.reshape(b * s, c)


def _conv_sample_kernel(cache_ref, glu_ref, cw_ref, cb_ref, lg_ref, lb_ref, o_ref, st_ref):
    hist = CONV_WIDTH - 1
    cache = cache_ref[...]
    glu = glu_ref[...]
    y = jnp.sum(cache * cw_ref[0:hist, :][None], axis=1) + glu * cw_ref[hist:hist + 1, :] + cb_ref[...]
    o_ref[...] = _conv_post(y, lg_ref[...], lb_ref[...]).astype(o_ref.dtype)
    st_ref[:, 0:hist - 1, :] = cache[:, 1:hist, :]
    st_ref[:, hist - 1:hist, :] = glu[:, None, :]


def conv_sample(cache, glu, cw, cb, lg, lb):
    bd, hist, c = cache.shape
    vec = lambda a: a.reshape(1, c)
    return pl.pallas_call(
        _conv_sample_kernel,
        out_shape=[jax.ShapeDtypeStruct((bd, c), BF16), jax.ShapeDtypeStruct((bd, hist, c), F32)],
        name="conv_sample",
    )(cache, glu, cw, vec(cb), vec(lg), vec(lb))


def _rel_bucket(dist):
    n = jnp.maximum(dist, 0)
    max_exact = NUM_BUCKETS // 2
    nf = jnp.maximum(n, 1).astype(F32)
    large = max_exact + (jnp.log(nf / max_exact) / math.log(MAX_DISTANCE / max_exact)
                         * (NUM_BUCKETS - max_exact)).astype(jnp.int32)
    large = jnp.minimum(large, NUM_BUCKETS - 1)
    return jnp.where(n < max_exact, n, large)


def _bias_of(rel_bias, dist):
    bucket = _rel_bucket(dist)[..., None]
    out = jnp.zeros(bucket.shape[:-1] + (rel_bias.shape[1],), F32)
    for k in range(NUM_BUCKETS):
        out = jnp.where(bucket == k, rel_bias[k].astype(F32), out)
    return out


def _compress_accumulate(load_rows, pe_ref, w_ref, nc):
    accs = []
    half = CMP_BLOCK // 2
    for slot in range(2):
        acc = jnp.zeros((nc, w_ref.shape[-1]), F32)
        for j in range(half):
            pair = [load_rows(slot, jj) + pe_ref[slot, jj:jj + 1, :] for jj in (j, j + half)]
            acc = acc + _dot(jnp.concatenate(pair, axis=1).astype(BF16), w_ref[slot, j])
        accs.append(acc)
    return jnp.concatenate(accs, axis=1)


def _file_rows(tile_t, xrow, sl, page):
    rows = tile_t.T
    per_page = PAGE_SIZE // CMP_BLOCK
    for cl in range(per_page):
        for a in range(CMP_BLOCK // SUBLANES):
            r0 = cl * CMP_BLOCK + a * SUBLANES
            dst = pl.multiple_of((page * per_page + cl) * SUBLANES, SUBLANES)
            xrow[sl, a, pl.ds(dst, SUBLANES), :] = rows[r0:r0 + SUBLANES, :]


def _compress_filed(xrow, pe_ref, w_ref, nc):
    return _compress_accumulate(
        lambda sl, j: xrow[sl, j // SUBLANES, pl.ds(j % SUBLANES, nc, stride=SUBLANES), :], pe_ref, w_ref, nc)


def _compress_prompt_kernel(x_ref, pe_ref, w_ref, kc_ref, vct_ref, xrow, *, nc, ncp, kvd):
    for p in range(x_ref.shape[2] // PAGE_SIZE):
        for sl in range(2):
            _file_rows(x_ref[0, sl * kvd:(sl + 1) * kvd, p * PAGE_SIZE:(p + 1) * PAGE_SIZE], xrow, sl, p)
    acc = _compress_filed(xrow, pe_ref, w_ref, nc)
    if ncp > nc:
        acc = jnp.concatenate([acc, jnp.zeros((ncp - nc, 2 * kvd), F32)], axis=0)
    kc_ref[0] = acc[:, 0:kvd].astype(BF16)
    vct_ref[0] = acc[:, kvd:2 * kvd].T.astype(BF16)


def compress_prompt(kv_t, pe2, wbd, ncp):
    b, _, s = kv_t.shape
    assert s % PAGE_SIZE == 0
    nc = s // CMP_BLOCK
    kvd = wbd.shape[-1]
    return pl.pallas_call(
        functools.partial(_compress_prompt_kernel, nc=nc, ncp=ncp, kvd=kvd), grid=(b,),
        in_specs=[pl.BlockSpec((1, 2 * kvd, s), lambda bi: (bi, 0, 0)), _full(pe2.shape), _full(wbd.shape)],
        out_specs=[pl.BlockSpec((1, ncp, kvd), lambda bi: (bi, 0, 0)), pl.BlockSpec((1, kvd, ncp), lambda bi: (bi, 0, 0))],
        out_shape=[jax.ShapeDtypeStruct((b, ncp, kvd), BF16), jax.ShapeDtypeStruct((b, kvd, ncp), BF16)],
        scratch_shapes=[pltpu.VMEM((2, CMP_BLOCK // SUBLANES, nc * SUBLANES, kvd), F32)],
        compiler_params=_cparams(("parallel",)), name="compress_prompt",
    )(kv_t, pe2, wbd)


def _select_blocks(score, n_sel, n_cand):
    lane = lax.broadcasted_iota(jnp.int32, score.shape, 1)
    rank = jnp.zeros(score.shape, F32)
    for i in range(n_cand):
        col = score[:, 2 * i:2 * i + 1]
        beats = (col > score) | ((col == score) & (lane > 2 * i))
        rank = rank + beats.astype(F32)
    is_cand = ((lane % 2) == 0) & (lane < 2 * n_cand)
    return (is_cand & (rank < n_sel) & (score >= 0)).astype(F32), rank


def _pair_sum(imp):
    n = imp.shape[1]
    return imp + pltpu.roll(imp, n - 1, 1)


def _rank_rows(score, n_sel, n_cand):
    blk = lax.broadcasted_iota(jnp.int32, score.shape, 0)
    rank = jnp.zeros(score.shape, F32)
    for i in range(n_cand):
        row = score[i:i + 1, :]
        beats = (row > score) | ((row == score) & (blk > i))
        rank = rank + beats.astype(F32)
    return ((rank < n_sel) & (score >= 0)).astype(F32)


def _nsa_prompt_kernel(qt_ref, gt_ref, kc_ref, vct_ref, keys_ref, vt_ref, biasc_ref, btile_ref, o_ref,
                       qt_scr, oc_scr, acc_s, acc_w, imp_scr, sel_scr, out_scr, *, nc, nsb, n_sel):
    qi = pl.program_id(1)
    g, hd, kvh, qb = NSA_GROUP, NSA_HD, NSA_KV_HEADS, Q_BLOCK
    kvd = kvh * hd
    ncp = kc_ref.shape[1]
    nsbp = sel_scr.shape[1]
    q_pos = qi * qb + lax.broadcasted_iota(jnp.int32, (1, qb), 1)
    key_row = lax.broadcasted_iota(jnp.int32, (qb, 1), 0)
    c_row = lax.broadcasted_iota(jnp.int32, (ncp, 1), 0)
    mask_c = (q_pos >= c_row * CMP_BLOCK + (CMP_BLOCK - 1)) & (c_row < nc)
    blk = lax.broadcasted_iota(jnp.int32, (nsbp, 1), 0)
    cur = q_pos // SEL_BLOCK
    forced = (blk == 0) | (blk == cur) | (blk == cur - 1)
    zeros = jnp.zeros((hd, qb), BF16)
    for k in range(kvh):
        for gi in range(g):
            h = k * g + gi
            parts = [zeros] * kvh
            parts[k] = qt_ref[0, h * hd:(h + 1) * hd, :]
            qt_scr[k, :, gi * qb:(gi + 1) * qb] = jnp.concatenate(parts, axis=0)
        s_c = _dot(kc_ref[0], qt_scr[k])
        imp = jnp.zeros((ncp, qb), F32)
        probs = []
        for gi in range(g):
            s = jnp.where(mask_c, s_c[:, gi * qb:(gi + 1) * qb] + biasc_ref[k * g + gi], NEG)
            e = jnp.where(mask_c, jnp.exp(s - jnp.max(s, 0, keepdims=True)), 0.0)
            den = jnp.sum(e, 0, keepdims=True)
            p = e / jnp.where(den > 0, den, 1.0)
            imp = imp + p
            probs.append(p.astype(BF16))
        oc_scr[k] = _dot(vct_ref[0, k * hd:(k + 1) * hd, :], jnp.concatenate(probs, axis=1))
        imp_scr[...] = imp + pltpu.roll(imp, ncp - 1, 0)
        cand = imp_scr[pl.ds(0, nsbp, stride=SEL_RATIO), :]
        score = jnp.where(forced, FORCE_SCORE, jnp.where(blk <= cur, cand, -1.0))
        sel_scr[k] = _rank_rows(jnp.where(blk < nsb, score, -2.0), n_sel, nsb)

    per_tile = qb // SEL_BLOCK
    n_tiles = keys_ref.shape[1] // qb
    first = ([jnp.full((1, qb), NEG, F32)] * g, [jnp.zeros((1, qb), F32)] * g)

    def tile_step(tiles, carry, key_col, val_row, acc_ref, window):
        kts = [jnp.clip(kt, 0, n_tiles - 1) for kt, _ in tiles]
        starts = [pl.multiple_of(kt * qb, qb) for kt in kts]
        k_t = jnp.concatenate([keys_ref[0, pl.ds(r0, qb), key_col:key_col + kvd] for r0 in starts], axis=0)
        dist = jnp.concatenate([jnp.where(active, q_pos - (r0 + key_row), -1)
                                for r0, (_, active) in zip(starts, tiles)], axis=0)
        in_range = dist >= 0
        scores = [_dot(k_t, qt_scr[k]) for k in range(kvh)]
        new, updates = [], []
        for k in range(kvh):
            if window:
                valid = in_range & (dist < WINDOW)
            else:
                pieces = []
                for kt in kts:
                    chosen = jnp.zeros((qb, qb), F32)
                    for j in range(per_tile):
                        row = sel_scr[k, pl.ds(per_tile * kt + j, 1), :]
                        chosen = jnp.where(key_row // SEL_BLOCK == j, row, chosen)
                    pieces.append(chosen)
                valid = in_range & (jnp.concatenate(pieces, axis=0) > 0.5)
            ms, ls = carry[k]
            ms2, ls2, alphas, probs = [], [], [], []
            for gi in range(g):
                bias = jnp.concatenate([btile_ref[jnp.maximum(qi - kt, 0), k * g + gi] for kt in kts], axis=0)
                s = jnp.where(valid, scores[k][:, gi * qb:(gi + 1) * qb] + bias, NEG)
                m_new = jnp.maximum(ms[gi], jnp.max(s, 0, keepdims=True))
                alpha = jnp.exp(ms[gi] - m_new)
                p = jnp.exp(s - jnp.where(m_new == NEG, 0.0, m_new))
                ms2.append(m_new)
                ls2.append(alpha * ls[gi] + jnp.sum(p, 0, keepdims=True))
                alphas.append(alpha)
                probs.append(p.astype(BF16))
            new.append((ms2, ls2))
            updates.append((jnp.concatenate(alphas, axis=1), jnp.concatenate(probs, axis=1)))
        for k, (alpha, prob) in enumerate(updates):
            v_t = jnp.concatenate([vt_ref[kt, val_row + k * hd:val_row + (k + 1) * hd, :] for kt in kts], axis=1)
            acc_ref[k] = acc_ref[k] * alpha + _dot(v_t, prob)
        return tuple(new)

    acc_s[...] = jnp.zeros_like(acc_s)
    acc_w[...] = jnp.zeros_like(acc_w)
    sel_args = dict(key_col=0, val_row=0, acc_ref=acc_s, window=False)
    win_args = dict(key_col=kvd, val_row=kvd, acc_ref=acc_w, window=True)

    def sel_group(i, carry):
        return tile_step([(SEL_TILES * i + j, SEL_TILES * i + j <= qi) for j in range(SEL_TILES)], carry, **sel_args)

    stat_s = lax.fori_loop(0, qi // SEL_TILES + 1, sel_group, (first,) * kvh)
    stat_w = (first,) * kvh
    win_tiles = [(qi - j, qi - j >= 0) for j in range(WINDOW // qb, -1, -1)]
    for j in range(0, len(win_tiles), WIN_TILES):
        stat_w = tile_step(win_tiles[j:j + WIN_TILES], stat_w, **win_args)
    gt = gt_ref[0]
    for k in range(kvh):
        for gi in range(g):
            h = k * g + gi
            cols = slice(gi * qb, (gi + 1) * qb)
            l_s, l_w = stat_s[k][1][gi], stat_w[k][1][gi]
            o_s = acc_s[k, :, cols] / jnp.where(l_s > 0, l_s, 1.0)
            o_w = acc_w[k, :, cols] / jnp.where(l_w > 0, l_w, 1.0)
            out_scr[h * hd:(h + 1) * hd, :] = (gt[3 * h:3 * h + 1, :] * oc_scr[k, :, cols]
                                               + gt[3 * h + 1:3 * h + 2, :] * o_s + gt[3 * h + 2:3 * h + 3, :] * o_w)
    o_ref[0] = out_scr[...].T.astype(o_ref.dtype)


def nsa_prompt(qt, gt, kc, vct, keys, vt, rel_bias, b, s):
    qb = Q_BLOCK
    nq = s // qb
    nc = s // CMP_BLOCK
    ncp = kc.shape[1]
    nsb = s // SEL_BLOCK
    nsbp = _round_up(nsb, SUBLANES)
    assert SEL_RATIO * nsbp <= ncp
    n_sel = min(TOP_N, nsb)
    nh = NSA_HEADS
    hq = qt.shape[1]
    kvd = kc.shape[2]
    glanes = NSA_GROUP * qb
    cend = jnp.arange(ncp)[:, None] * CMP_BLOCK + (CMP_BLOCK - 1)
    biasc = jnp.transpose(_bias_of(rel_bias, jnp.arange(s)[None, :] - cend), (2, 0, 1))
    r = jnp.arange(qb)
    dist = jnp.arange(nq)[:, None, None] * qb + r[None, None, :] - r[None, :, None]
    btile = jnp.transpose(_bias_of(rel_bias, dist), (0, 3, 1, 2))
    out = pl.pallas_call(
        functools.partial(_nsa_prompt_kernel, nc=nc, nsb=nsb, n_sel=n_sel), grid=(b, nq),
        in_specs=[pl.BlockSpec((1, hq, qb), lambda bi, i: (bi * nq + i, 0, 0)),
                  pl.BlockSpec((1, LANES, qb), lambda bi, i: (bi * nq + i, 0, 0)),
                  pl.BlockSpec((1, ncp, kvd), lambda bi, i: (bi, 0, 0)),
                  pl.BlockSpec((1, kvd, ncp), lambda bi, i: (bi, 0, 0)),
                  pl.BlockSpec((1, s, 2 * kvd), lambda bi, i: (bi, 0, 0)),
                  pl.BlockSpec((nq, 2 * kvd, qb), lambda bi, i: (bi, 0, 0)),
                  pl.BlockSpec((nh, ncp, qb), lambda bi, i: (0, 0, i)),
                  _full(btile.shape)],
        out_specs=pl.BlockSpec((1, qb, hq), lambda bi, i: (bi, i, 0)),
        out_shape=jax.ShapeDtypeStruct((b, s, hq), BF16),
        scratch_shapes=[pltpu.VMEM((NSA_KV_HEADS, kvd, glanes), BF16), pltpu.VMEM((NSA_KV_HEADS, NSA_HD, glanes), F32),
                        pltpu.VMEM((NSA_KV_HEADS, NSA_HD, glanes), F32), pltpu.VMEM((NSA_KV_HEADS, NSA_HD, glanes), F32),
                        pltpu.VMEM((ncp, qb), F32), pltpu.VMEM((NSA_KV_HEADS, nsbp, qb), F32),
                        pltpu.VMEM((hq, qb), F32)],
        compiler_params=_cparams(("parallel", "arbitrary"), VMEM_LIMIT), name="nsa_prompt",
    )(qt, gt, kc, vct, keys.reshape(b, s, 2 * kvd), vt, biasc, btile)
    return out.reshape(b * s, hq)


def _round_up(x, m):
    return (x + m - 1) // m * m


def prep_even(w_in, pe, wc):
    d, n = w_in.shape
    n_pad = _round_up(n - 3 * NSA_HEADS, LANES) + LANES
    w_pad = jnp.zeros((d, n_pad), BF16).at[:, :n].set(w_in.astype(BF16))
    hd = NSA_HD
    pe2 = jnp.tile(pe, (1, 1, NSA_KV_HEADS))
    zero = jnp.zeros_like(wc)
    wbd = jnp.concatenate([jnp.concatenate([wc if i == j else zero for j in range(NSA_KV_HEADS)], axis=-1)
                           for i in range(NSA_KV_HEADS)], axis=-2)
    half = CMP_BLOCK // 2
    wbd = jnp.concatenate([wbd[:, :half], wbd[:, half:]], axis=2)
    return dict(w_in=w_pad, pe2=pe2, wbd=wbd.astype(BF16))


def _memkv_kernel(x_ref, g_ref, w_ref, o_ref, ob_ref, *, hd, nh):
    y = _dot(_rms(x_ref[...], g_ref[...]).astype(BF16), w_ref[...])
    ob_ref[...] = y.astype(BF16)
    tm = y.shape[0]
    chunks = hd // LANES
    period = 2 * chunks * nh
    for kv in range(2):
        for h in range(nh):
            for c in range(chunks):
                col = (kv * nh + h) * hd + c * LANES
                o_ref[pl.ds((kv * chunks + c) * nh + h, tm, stride=period), :] = y[:, col:col + LANES]


def memkv(mem, g, w, nh):
    m, d = mem.shape
    n = w.shape[1]
    hd = n // (2 * nh)
    per_tok = n // LANES
    tm = _row_tile(m, 256)
    return pl.pallas_call(
        functools.partial(_memkv_kernel, hd=hd, nh=nh), grid=(m // tm,),
        in_specs=[pl.BlockSpec((tm, d), lambda i: (i, 0)), _full((1, d)), _full(w.shape)],
        out_specs=[pl.BlockSpec((tm * per_tok, LANES), lambda i: (i, 0)), pl.BlockSpec((tm, n), lambda i: (i, 0))],
        out_shape=[jax.ShapeDtypeStruct((m * per_tok, LANES), F32), jax.ShapeDtypeStruct((m, n), BF16)],
        compiler_params=_cparams(("parallel",)), name="memkv",
    )(mem, g.reshape(1, d), w)


def _xattn_core(q, kv, hd):
    nh = q.shape[1] // hd
    outs = []
    for h in range(nh):
        s = _dot_nt(q[:, h * hd:(h + 1) * hd], kv[:, h * hd:(h + 1) * hd])
        e = jnp.exp(s - jnp.max(s, -1, keepdims=True))
        p = e / jnp.sum(e, -1, keepdims=True)
        outs.append(_dot(p.astype(BF16), kv[:, (nh + h) * hd:(nh + h + 1) * hd]))
    return jnp.concatenate(outs, axis=1).astype(BF16)


def _xattn_prompt_kernel(*refs, hd, n_in):
    x_ref, g_ref, wq_ref, kv_ref, wo_ref = refs[:5]
    a_refs = refs[5:5 + n_in]
    w_refs = refs[5 + n_in:5 + 2 * n_in]
    o_ref = refs[5 + 2 * n_in]
    x = x_ref[0]
    for a_ref, w_ref in zip(a_refs, w_refs):
        x = x + _dot(a_ref[0], w_ref[...])
    q = (_dot(_rms(x, g_ref[...]).astype(BF16), wq_ref[...]) * (hd ** -0.5)).astype(BF16)
    o = _xattn_core(q, kv_ref[0], hd)
    o_ref[0] = x + _dot(o, wo_ref[...])


def xattn_prompt(x, g, wq, kvb, wo, b, s, acts=(), ws=()):
    d = x.shape[1]
    mt = kvb.shape[0] // b
    tm = _row_tile(s, 512)
    tile = lambda n: pl.BlockSpec((1, tm, n), lambda bi, i: (bi, i, 0))
    out = pl.pallas_call(
        functools.partial(_xattn_prompt_kernel, hd=d // X_HEADS, n_in=len(acts)), grid=(b, s // tm),
        in_specs=[tile(d), _full((1, d)), _full(wq.shape),
                  pl.BlockSpec((1, mt, kvb.shape[1]), lambda bi, i: (bi, 0, 0)), _full(wo.shape)]
        + [tile(a.shape[1]) for a in acts] + [_full(w.shape) for w in ws],
        out_specs=tile(d),
        out_shape=jax.ShapeDtypeStruct((b, s, d), F32),
        compiler_params=_cparams(("parallel", "parallel"), VMEM_LIMIT), name="xattn_prompt",
    )(x.reshape(b, s, d), g.reshape(1, d), wq, kvb.reshape(b, mt, -1), wo,
      *[a.reshape(b, s, -1) for a in acts], *ws)
    return out.reshape(b * s, d)


def _xattn_sample_kernel(x_ref, g_ref, wq_ref, kv_ref, wo_ref, o_ref, q_scr, a_scr, *, hd, mt):
    bi = pl.program_id(0)
    nb = pl.num_programs(0)
    nh = wq_ref.shape[1] // hd

    @pl.when(bi == 0)
    def _():
        q_scr[...] = _dot(_rms(x_ref[...], g_ref[...]).astype(BF16), wq_ref[...]) * (hd ** -0.5)

    q = jnp.broadcast_to(q_scr[pl.ds(bi, 1), :], (SUBLANES, q_scr.shape[1])).astype(BF16)
    chunks = hd // LANES
    period = 2 * chunks * nh

    def head_rows(kv, h):
        return jnp.concatenate([kv_ref[0, pl.ds((kv * chunks + c) * nh + h, mt, stride=period), :]
                                for c in range(chunks)], axis=1).astype(BF16)

    outs = []
    for h in range(nh):
        s = _dot_nt(q[:, h * hd:(h + 1) * hd], head_rows(0, h))
        e = jnp.exp(s - jnp.max(s, -1, keepdims=True))
        p = e / jnp.sum(e, -1, keepdims=True)
        outs.append(_dot(p.astype(BF16), head_rows(1, h)))
    a_scr[pl.ds(bi, 1), :] = jnp.concatenate(outs, axis=1)[0:1, :]

    @pl.when(bi == nb - 1)
    def _():
        o_ref[...] = x_ref[...] + _dot(a_scr[...].astype(BF16), wo_ref[...])


def xattn_sample(x, g, wq, kv_rows, wo, layer, mt):
    bd, d = x.shape
    rows = kv_rows.shape[1]
    return pl.pallas_call(
        functools.partial(_xattn_sample_kernel, hd=d // X_HEADS, mt=mt), grid=(bd,),
        in_specs=[_full((bd, d)), _full((1, d)), _full(wq.shape),
                  pl.BlockSpec((1, rows, LANES), lambda bi: (layer * bd + bi, 0, 0)), _full(wo.shape)],
        out_specs=_full((bd, d)),
        out_shape=jax.ShapeDtypeStruct((bd, d), F32),
        scratch_shapes=[pltpu.VMEM((bd, wq.shape[1]), F32), pltpu.VMEM((bd, wq.shape[1]), F32)],
        compiler_params=_cparams(("arbitrary",), VMEM_LIMIT), name="xattn_sample",
    )(x, g.reshape(1, d), wq, kv_rows, wo)


def _ffn_kernel(x_ref, g_ref, wg_ref, wu_ref, wd_ref, o_ref, h_scr, acc_scr):
    c = pl.program_id(1)

    @pl.when(c == 0)
    def _():
        h_scr[...] = _rms(x_ref[...], g_ref[...]).astype(BF16)
        acc_scr[...] = x_ref[...]

    h = h_scr[...]
    gate = _dot(h, wg_ref[...])
    up = _dot(h, wu_ref[...])
    act = (gate * jax.nn.sigmoid(gate) * up).astype(BF16)
    acc_scr[...] += _dot(act, wd_ref[...])

    @pl.when(c == pl.num_programs(1) - 1)
    def _():
        o_ref[...] = acc_scr[...]


def _xattn_ffn_kernel(*refs, hd, n_in):
    x_ref, gx_ref, wq_ref, kv_ref, wo_ref, gf_ref, wg_ref, wu_ref, wd_ref = refs[:9]
    a_refs = refs[9:9 + n_in]
    w_refs = refs[9 + n_in:9 + 2 * n_in]
    o_ref, h_scr, acc_scr = refs[9 + 2 * n_in:]
    c = pl.program_id(1)

    @pl.when(c == 0)
    def _():
        x = x_ref[...]
        for a_ref, w_ref in zip(a_refs, w_refs):
            x = x + _dot(a_ref[...], w_ref[...])
        q = (_dot(_rms(x, gx_ref[...]).astype(BF16), wq_ref[...]) * (hd ** -0.5)).astype(BF16)
        x = x + _dot(_xattn_core(q, kv_ref[0], hd), wo_ref[...])
        h_scr[...] = _rms(x, gf_ref[...]).astype(BF16)
        acc_scr[...] = x

    h = h_scr[...]
    gate = _dot(h, wg_ref[...])
    up = _dot(h, wu_ref[...])
    act = (gate * jax.nn.sigmoid(gate) * up).astype(BF16)
    acc_scr[...] += _dot(act, wd_ref[...])

    @pl.when(c == pl.num_programs(1) - 1)
    def _():
        o_ref[...] = acc_scr[...]


def xattn_ffn(x, gx, wq, kvb, wo, acts, ws, gf, w_gu, w_dn, b, s):
    m, d = x.shape
    dff = w_dn.shape[0]
    mt = kvb.shape[0] // b
    tm = _row_tile(s, 512)
    per_seq = s // tm
    fc = _ff_chunk(dff, 1408)
    nch = dff // fc
    row = lambda n: pl.BlockSpec((tm, n), lambda i, c: (i, 0))
    return pl.pallas_call(
        functools.partial(_xattn_ffn_kernel, hd=d // X_HEADS, n_in=len(acts)), grid=(m // tm, nch),
        in_specs=[row(d), _full((1, d)), _full(wq.shape),
                  pl.BlockSpec((1, mt, kvb.shape[1]), lambda i, c: (i // per_seq, 0, 0)), _full(wo.shape),
                  _full((1, d)),
                  pl.BlockSpec((d, fc), lambda i, c: (0, c)),
                  pl.BlockSpec((d, fc), lambda i, c: (0, nch + c)),
                  pl.BlockSpec((fc, d), lambda i, c: (c, 0))]
        + [row(a.shape[1]) for a in acts] + [_full(w.shape) for w in ws],
        out_specs=row(d),
        out_shape=jax.ShapeDtypeStruct((m, d), F32),
        scratch_shapes=[pltpu.VMEM((tm, d), BF16), pltpu.VMEM((tm, d), F32)],
        compiler_params=_cparams(("parallel", "arbitrary"), VMEM_LIMIT), name="xattn_ffn",
    )(x, gx.reshape(1, d), wq, kvb.reshape(b, mt, -1), wo, gf.reshape(1, d), w_gu, w_gu, w_dn, *acts, *ws)


def _ff_chunk(dff, pref):
    c = dff
    for n in range(1, dff // LANES + 1):
        if dff % n == 0 and (dff // n) % LANES == 0 and dff // n <= pref:
            c = dff // n
            break
    return c


def ffn(x, g, w_gu, w_dn):
    m, d = x.shape
    dff = w_dn.shape[0]
    tm = _row_tile(m, 512)
    fc = _ff_chunk(dff, 1408)
    nch = dff // fc
    return pl.pallas_call(
        _ffn_kernel, grid=(m // tm, nch),
        in_specs=[pl.BlockSpec((tm, d), lambda i, c: (i, 0)), _full((1, d)),
                  pl.BlockSpec((d, fc), lambda i, c: (0, c)),
                  pl.BlockSpec((d, fc), lambda i, c: (0, nch + c)),
                  pl.BlockSpec((fc, d), lambda i, c: (c, 0))],
        out_specs=pl.BlockSpec((tm, d), lambda i, c: (i, 0)),
        out_shape=jax.ShapeDtypeStruct((m, d), F32),
        scratch_shapes=[pltpu.VMEM((tm, d), BF16), pltpu.VMEM((tm, d), F32)],
        compiler_params=_cparams(("parallel", "arbitrary"), VMEM_LIMIT), name="ffn",
    )(x, g.reshape(1, d), w_gu, w_gu, w_dn)


def _router_kernel(x_ref, g_ref, w_ref, b_ref, comb_ref, h_ref, mask_ref, cnt_ref, *, ne):
    h = _rms(x_ref[...], g_ref[...]).astype(BF16)
    h_ref[...] = h
    logits = _dot(h, w_ref[...]) + b_ref[...]
    lane = lax.broadcasted_iota(jnp.int32, logits.shape, 1)
    logits = jnp.where(lane < ne, logits, -jnp.inf)
    v1 = jnp.max(logits, -1, keepdims=True)
    i1 = jnp.min(jnp.where(logits == v1, lane, LANES), -1, keepdims=True)
    rest = jnp.where(lane == i1, -jnp.inf, logits)
    v2 = jnp.max(rest, -1, keepdims=True)
    i2 = jnp.min(jnp.where(rest == v2, lane, LANES), -1, keepdims=True)
    e2 = jnp.exp(v2 - v1)
    den = 1.0 + e2
    comb_ref[...] = jnp.where(lane == i1, 1.0 / den, 0.0) + jnp.where(lane == i2, e2 / den, 0.0)
    chosen = jnp.where((lane == i1) | (lane == i2), 1.0, 0.0)
    mask_ref[...] = chosen.astype(BF16)

    @pl.when(pl.program_id(0) == 0)
    def _():
        cnt_ref[...] = jnp.zeros_like(cnt_ref)

    cnt_ref[0:1, :] += jnp.sum(chosen, axis=0, keepdims=True)


def router(x, g, w_r, b_r):
    m, d = x.shape
    ne = w_r.shape[1]
    w_pad = jnp.zeros((d, LANES), BF16).at[:, :ne].set(w_r.astype(BF16))
    b_pad = jnp.zeros((1, LANES), F32).at[0, :ne].set(b_r.astype(F32))
    tm = _row_tile(m, 512)
    row = lambda n: pl.BlockSpec((tm, n), lambda i: (i, 0))
    return pl.pallas_call(
        functools.partial(_router_kernel, ne=ne), grid=(m // tm,),
        in_specs=[row(d), _full((1, d)), _full((d, LANES)), _full((1, LANES))],
        out_specs=[row(LANES), row(d), row(LANES), _full((SUBLANES, LANES))],
        out_shape=[jax.ShapeDtypeStruct((m, LANES), F32), jax.ShapeDtypeStruct((m, d), BF16),
                   jax.ShapeDtypeStruct((m, LANES), BF16), jax.ShapeDtypeStruct((SUBLANES, LANES), F32)],
        compiler_params=_cparams(("arbitrary",)), name="router",
    )(x, g.reshape(1, d), w_pad, b_pad)


def _residual_out(y, gain_ref, norm):
    return _rms(y, gain_ref[...]) if norm else y


def _moe_kernel(x_ref, h_ref, comb_ref, wg_ref, wu_ref, wd_ref, gain_ref, o_ref, acc_scr, *, norm):
    e = pl.program_id(1)

    @pl.when(e == 0)
    def _():
        acc_scr[...] = jnp.zeros_like(acc_scr)

    h = h_ref[...]
    gate = _dot(h, wg_ref[0])
    up = _dot(h, wu_ref[0])
    act = (gate * jax.nn.sigmoid(gate) * up).astype(BF16)
    y = _dot(act, wd_ref[0])
    comb = comb_ref[...]
    lane = lax.broadcasted_iota(jnp.int32, comb.shape, 1)
    acc_scr[...] += jnp.sum(jnp.where(lane == e, comb, 0.0), -1, keepdims=True) * y

    @pl.when(e == pl.num_programs(1) - 1)
    def _():
        o_ref[...] = _residual_out(x_ref[...] + acc_scr[...], gain_ref, norm)


def moe(x, h, comb, w_gu, w_dn, final_g=None):
    m, d = x.shape
    ne, dfe = w_dn.shape[:2]
    tm = _row_tile(m, 512)
    gain = jnp.ones((1, d), F32) if final_g is None else final_g.reshape(1, d)
    return pl.pallas_call(
        functools.partial(_moe_kernel, norm=final_g is not None), grid=(m // tm, ne),
        in_specs=[pl.BlockSpec((tm, d), lambda i, e: (i, 0)), pl.BlockSpec((tm, d), lambda i, e: (i, 0)),
                  pl.BlockSpec((tm, LANES), lambda i, e: (i, 0)),
                  pl.BlockSpec((1, d, dfe), lambda i, e: (e, 0, 0)),
                  pl.BlockSpec((1, d, dfe), lambda i, e: (e, 0, 1)),
                  pl.BlockSpec((1, dfe, d), lambda i, e: (e, 0, 0)), _full((1, d))],
        out_specs=pl.BlockSpec((tm, d), lambda i, e: (i, 0)),
        out_shape=jax.ShapeDtypeStruct((m, d), F32),
        scratch_shapes=[pltpu.VMEM((tm, d), F32)],
        compiler_params=_cparams(("parallel", "arbitrary"), VMEM_LIMIT), name="moe",
    )(x, h, comb, w_gu, w_gu, w_dn, gain)


MOE_TILE = 256


def _moe_pos_kernel(mask_ref, tri_ref, base_ref, post_ref, pos_ref, stab_ref, run_scr, *, nep):
    sb = pl.program_id(0)
    nb = pl.num_programs(0)

    @pl.when(sb == 0)
    def _():
        run_scr[...] = jnp.zeros_like(run_scr)
        stab_ref[...] = jnp.zeros_like(stab_ref)

    a = mask_ref[...]
    af = a.astype(F32)
    start = base_ref[...] + run_scr[...]
    stab_ref[pl.ds(sb, 1), :] = start.astype(jnp.int32)
    rank = _dot(tri_ref[...], a)
    pos = jnp.where(af > 0, start + rank, -1.0)
    pos_ref[...] = pos.astype(jnp.int32)
    post_ref[0] = pos.T[0:nep, :].astype(jnp.int32)
    run_scr[...] += jnp.sum(af, axis=0, keepdims=True)

    @pl.when(sb == nb - 1)
    def _():
        stab_ref[pl.ds(nb, 1), :] = (base_ref[...] + run_scr[...]).astype(jnp.int32)


def _moe_expert_kernel(te_ref, lo_ref, cnt_ref, ring_ref, nt_ref, h_ref, post_ref, wg_ref, wu_ref, wd_ref, y_ref,
                       hbuf, xg_scr, sem, *, t):
    i = pl.program_id(0)

    @pl.when(i >= nt_ref[0])
    def _():
        y_ref[...] = jnp.zeros_like(y_ref)

    nbuf = hbuf.shape[0]
    ahead = nbuf - 1

    def copy(sb, slot):
        return pltpu.make_async_copy(h_ref.at[pl.ds(pl.multiple_of(sb * t, t), t), :], hbuf.at[slot], sem.at[slot])

    def start_first(tile):
        for j in range(ahead):
            @pl.when(j < cnt_ref[tile])
            def _():
                copy(lo_ref[tile] + j, (ring_ref[tile] + j) % nbuf).start()

    @pl.when(i == 0)
    def _():
        start_first(0)

    @pl.when(i < nt_ref[0])
    def _():
        e = te_ref[i]
        lo = lo_ref[i]
        n = cnt_ref[i]
        ring = ring_ref[i]
        xg_scr[...] = jnp.zeros_like(xg_scr)
        row = i * t + lax.broadcasted_iota(jnp.int32, (t, 1), 0)

        def body(j, c):
            slot = (ring + j) % nbuf
            copy(lo + j, slot).wait()

            @pl.when(j + ahead < n)
            def _():
                copy(lo + j + ahead, (ring + j + ahead) % nbuf).start()

            src_pos = post_ref[lo + j, pl.ds(e, 1), :]
            onehot = jnp.where(src_pos == row, 1.0, 0.0).astype(BF16)
            xg_scr[...] += _dot(onehot, hbuf[slot])
            return c

        lax.fori_loop(0, n, body, 0)

        @pl.when(i + 1 < nt_ref[0])
        def _():
            start_first(i + 1)

        x = xg_scr[...].astype(BF16)
        gate = _dot(x, wg_ref[0])
        up = _dot(x, wu_ref[0])
        act = (gate * jax.nn.sigmoid(gate) * up).astype(BF16)
        y_ref[...] = _dot(act, wd_ref[0])


MOE_WIN = 128


def _moe_combine_kernel(stab_ref, x_ref, comb_ref, pos_ref, gain_ref, ys_ref, o_ref, ybuf, extra, acc_scr,
                        sem, sem_x, *, t, ne, norm):
    sb = pl.program_id(0)
    nb = pl.num_programs(0)
    n_win = (t + SUBLANES - 1) // MOE_WIN + 1

    def window(b, e, w):
        s0 = stab_ref[b * ne + e]
        s1 = stab_ref[(b + 1) * ne + e]
        start = (s0 // SUBLANES) * SUBLANES + w * MOE_WIN
        return pl.multiple_of(start, SUBLANES), start < s1

    def first_copy(b, slot, e):
        start, _ = window(b, e, 0)
        return pltpu.make_async_copy(ys_ref.at[pl.ds(start, MOE_WIN), :], ybuf.at[slot, e], sem.at[slot, e])

    def start_all(b, slot):
        for e in range(ne):
            @pl.when(window(b, e, 0)[1])
            def _():
                first_copy(b, slot, e).start()

    @pl.when(sb == 0)
    def _():
        ybuf[...] = jnp.zeros_like(ybuf)
        start_all(0, 0)

    @pl.when(sb + 1 < nb)
    def _():
        start_all(sb + 1, (sb + 1) % 2)

    slot = sb % 2
    comb = comb_ref[...]
    pos = pos_ref[...]
    lane_r = lax.broadcasted_iota(jnp.int32, (1, MOE_WIN), 1)

    def split(a):
        hi = a.astype(BF16)
        return hi, (a - hi.astype(F32)).astype(BF16)

    def weighted_onehot(start, e):
        return jnp.where(pos[:, e:e + 1] - start == lane_r, comb[:, e:e + 1], 0.0)

    def gathered(sel, rows):
        s_hi, s_lo = split(sel)
        r_hi, r_lo = split(rows)
        return _dot(s_hi, r_hi) + _dot(s_hi, r_lo) + _dot(s_lo, r_hi)

    for e in range(ne):
        @pl.when(window(sb, e, 0)[1])
        def _():
            first_copy(sb, slot, e).wait()
    sel = jnp.concatenate([weighted_onehot(window(sb, e, 0)[0], e) for e in range(ne)], axis=1)
    acc_scr[...] = x_ref[...] + gathered(sel, ybuf[slot].reshape(ne * MOE_WIN, ybuf.shape[-1]))

    for e in range(ne):
        for w in range(1, n_win):
            start_w, needed_w = window(sb, e, w)

            @pl.when(needed_w)
            def _():
                cp = pltpu.make_async_copy(ys_ref.at[pl.ds(start_w, MOE_WIN), :], extra, sem_x.at[0])
                cp.start()
                cp.wait()
                acc_scr[...] += gathered(weighted_onehot(start_w, e), extra[...])

    o_ref[...] = _residual_out(acc_scr[...], gain_ref, norm)


def moe_grouped(x, h, comb, mask, counts, w_gu, w_dn, final_g=None):
    m, d = x.shape
    ne, dfe = w_dn.shape[:2]
    t = MOE_TILE
    assert m % t == 0
    nb = m // t
    nep = _round_up(ne, SUBLANES)
    nbp = _round_up(nb + 1, SUBLANES)
    k_top = TOP_K
    nt_max = k_top * m // t + ne + 1
    cnt = counts[0, :ne].astype(jnp.int32)
    cnt_pad = (cnt + t - 1) // t * t
    ends = jnp.cumsum(cnt_pad)
    base = ends - cnt_pad
    base_row = jnp.zeros((1, LANES), F32).at[0, :ne].set(base.astype(F32))
    idx = lax.broadcasted_iota(jnp.int32, (t, t), 0)
    tri = jnp.where(lax.broadcasted_iota(jnp.int32, (t, t), 1) < idx, 1.0, 0.0).astype(BF16)
    blk = lambda n2: pl.BlockSpec((t, n2), lambda i: (i, 0))
    post, pos, stab = pl.pallas_call(
        functools.partial(_moe_pos_kernel, nep=nep), grid=(nb,),
        in_specs=[blk(LANES), _full((t, t)), _full((1, LANES))],
        out_specs=[pl.BlockSpec((1, nep, t), lambda i: (i, 0, 0)), blk(LANES), _full((nbp, LANES))],
        out_shape=[jax.ShapeDtypeStruct((nb, nep, t), jnp.int32), jax.ShapeDtypeStruct((m, LANES), jnp.int32),
                   jax.ShapeDtypeStruct((nbp, LANES), jnp.int32)],
        scratch_shapes=[pltpu.VMEM((1, LANES), F32)],
        compiler_params=_cparams(("arbitrary",)), name="moe_positions",
    )(mask, tri, base_row)
    r0 = jnp.arange(nt_max, dtype=jnp.int32) * t
    tile_e = jnp.minimum(jnp.sum(ends[None, :] <= r0[:, None], axis=1), ne - 1).astype(jnp.int32)
    n_tiles = (ends[-1] // t).astype(jnp.int32).reshape(1)
    s_e = stab[:nb + 1, :ne][:, tile_e]
    lo = jnp.sum(s_e[1:] <= r0[None, :], axis=0)
    hi = jnp.sum(s_e[:nb] < r0[None, :] + t, axis=0) - 1
    lo = jnp.clip(lo, 0, nb - 1).astype(jnp.int32)
    hi = jnp.clip(hi, lo, nb - 1).astype(jnp.int32)
    n_src = jnp.where(jnp.arange(nt_max) < n_tiles[0], hi - lo + 1, 0).astype(jnp.int32)
    n_ring = 6
    ring = ((jnp.cumsum(n_src) - n_src) % n_ring).astype(jnp.int32)
    w_spec = lambda shape, col: pl.BlockSpec(shape, lambda i, te, *_: (te[i], 0, col))
    grid_spec = pltpu.PrefetchScalarGridSpec(
        num_scalar_prefetch=5, grid=(nt_max,),
        in_specs=[pl.BlockSpec(memory_space=pl.ANY),
                  pl.BlockSpec((nb, nep, t), lambda i, *_: (0, 0, 0)),
                  w_spec((1, d, dfe), 0), w_spec((1, d, dfe), 1), w_spec((1, dfe, d), 0)],
        out_specs=pl.BlockSpec((t, d), lambda i, *_: (i, 0)),
        scratch_shapes=[pltpu.VMEM((n_ring, t, d), BF16), pltpu.VMEM((t, d), F32),
                        pltpu.SemaphoreType.DMA((n_ring,))])
    ys = pl.pallas_call(
        functools.partial(_moe_expert_kernel, t=t), grid_spec=grid_spec,
        out_shape=jax.ShapeDtypeStruct((nt_max * t, d), F32),
        compiler_params=_cparams(("arbitrary",), VMEM_LIMIT), name="moe_experts",
    )(tile_e, lo, n_src, ring, n_tiles, h, post, w_gu, w_gu, w_dn)
    gain = jnp.ones((1, d), F32) if final_g is None else final_g.reshape(1, d)
    blk2 = lambda n2: pl.BlockSpec((t, n2), lambda i, st: (i, 0))
    grid_spec = pltpu.PrefetchScalarGridSpec(
        num_scalar_prefetch=1, grid=(nb,),
        in_specs=[blk2(d), blk2(LANES), blk2(LANES), pl.BlockSpec((1, d), lambda i, st: (0, 0)),
                  pl.BlockSpec(memory_space=pl.ANY)],
        out_specs=blk2(d),
        scratch_shapes=[pltpu.VMEM((2, ne, MOE_WIN, d), F32), pltpu.VMEM((MOE_WIN, d), F32), pltpu.VMEM((t, d), F32),
                        pltpu.SemaphoreType.DMA((2, ne)), pltpu.SemaphoreType.DMA((1,))])
    return pl.pallas_call(
        functools.partial(_moe_combine_kernel, t=t, ne=ne, norm=final_g is not None), grid_spec=grid_spec,
        out_shape=jax.ShapeDtypeStruct((m, d), F32),
        compiler_params=_cparams(("arbitrary",), VMEM_LIMIT), name="moe_combine",
    )(stab[:nb + 1, :ne].reshape(-1), x, comb, pos, gain, ys)


def _inproj_odd_kernel(x_ref, g_ref, w_ref, bias_ref, q_ref, k_ref, v_ref, og_ref, gc_ref, gr_ref, *, hq, hv, nh):
    xn = _rms(x_ref[...], g_ref[...]).astype(BF16)

    def mm(lo, hi):
        return _dot(xn, w_ref[:, lo:hi])

    dk = hq // nh
    q_ref[...] = mm(0, hq).astype(BF16)
    k_ref[...] = (mm(hq, 2 * hq) * (dk ** -0.5)).astype(BF16)
    v_ref[...] = mm(2 * hq, 2 * hq + hv).astype(BF16)
    og_ref[...] = jax.nn.sigmoid(mm(2 * hq + hv, 2 * hq + 2 * hv))
    gi = mm(2 * hq + 2 * hv, 2 * hq + 2 * hv + LANES) + bias_ref[...]
    lane = lax.broadcasted_iota(jnp.int32, gi.shape, 1)
    gates = jnp.where(lane < nh, gi, jax.nn.log_sigmoid(gi))
    gc_ref[...] = gates
    gr_ref[...] = gates.T[0:SUBLANES, :]


def inproj_odd(x, g, w_pad, gate_bias):
    m, d = x.shape
    nh = MLSTM_HEADS
    hq = hv = d
    tm = _row_tile(m, 256)
    row = lambda n: pl.BlockSpec((tm, n), lambda i: (i, 0))
    outs = [(hq, BF16), (hq, BF16), (hv, BF16), (hv, F32), (LANES, F32)]
    if tm % LANES:
        gr_spec = _full((SUBLANES, m))
    else:
        gr_spec = pl.BlockSpec((SUBLANES, tm), lambda i: (0, i))
    return pl.pallas_call(
        functools.partial(_inproj_odd_kernel, hq=hq, hv=hv, nh=nh), grid=(m // tm,),
        in_specs=[row(d), _full((1, d)), _full(w_pad.shape), _full((1, LANES))],
        out_specs=[row(n) for n, _ in outs] + [gr_spec],
        out_shape=[jax.ShapeDtypeStruct((m, n), t) for n, t in outs] + [jax.ShapeDtypeStruct((SUBLANES, m), F32)],
        compiler_params=_cparams(("parallel",), VMEM_LIMIT), name="inproj_odd",
    )(x, g.reshape(1, d), w_pad, gate_bias)


def prep_odd(w_in, b_i, b_f):
    d, n = w_in.shape
    n_pad = _round_up(n - 2 * MLSTM_HEADS, LANES) + LANES
    w_pad = jnp.zeros((d, n_pad), BF16).at[:, :n].set(w_in.astype(BF16))
    bias = jnp.zeros((1, LANES), F32).at[0, :2 * MLSTM_HEADS].set(jnp.concatenate([b_i, b_f]).astype(F32))
    return dict(w_in=w_pad, bias=bias)


def _mlstm_prompt_kernel(q_ref, k_ref, v_ref, og_ref, gc_ref, gr_ref, gain_ref, hn_ref, c_ref, n_ref, m_ref,
                         *, nh, dk, dv, ln):
    ci = pl.program_id(1)

    @pl.when(ci == 0)
    def _():
        c_ref[...] = jnp.zeros_like(c_ref)
        n_ref[...] = jnp.zeros_like(n_ref)
        m_ref[...] = jnp.full(m_ref.shape, NEG, F32)

    row = lax.broadcasted_iota(jnp.int32, (ln, ln), 0)
    col = lax.broadcasted_iota(jnp.int32, (ln, ln), 1)
    tri = row >= col
    gc = gc_ref[...]
    gr = gr_ref[...]
    for h in range(nh):
        q = q_ref[:, h * dk:(h + 1) * dk]
        k = k_ref[:, h * dk:(h + 1) * dk]
        v = v_ref[:, h * dv:(h + 1) * dv]
        ig_c, lf_c = gc[:, h:h + 1], gc[:, nh + h:nh + h + 1]
        ig_r, lf_r = gr[h:h + 1, :], gr[nh + h:nh + h + 1, :]
        b_c = jnp.sum(jnp.where(tri, lf_r, 0.0), axis=1, keepdims=True)
        b_r = jnp.sum(jnp.where(row <= col, lf_c, 0.0), axis=0, keepdims=True)
        m_prev = m_ref[0, h:h + 1, 0:1]
        c_prev = c_ref[0, h]
        n_prev = n_ref[0, h:h + 1, :]
        dmat = jnp.where(tri, b_c - b_r + ig_r, NEG)
        inter = b_c + m_prev
        mt = jnp.maximum(inter, jnp.max(dmat, -1, keepdims=True))
        wm = jnp.exp(dmat - mt)
        a = jnp.exp(inter - mt)
        wqk = wm * _dot_nt(q, k)
        num = a * _dot_nt(q, c_prev.astype(BF16)) + _dot(wqk.astype(BF16), v)
        den = a * jnp.sum(q.astype(F32) * n_prev, -1, keepdims=True) + jnp.sum(wqk, -1, keepdims=True)
        hh = num / jnp.maximum(jnp.abs(den), jnp.exp(-mt))
        b_end = b_c[ln - 1:ln, :]
        m_new = mt[ln - 1:ln, :]
        a_end = jnp.exp(b_end + m_prev - m_new)
        w_s = jnp.exp(b_end - b_c + ig_c - m_new)
        c_ref[0, h] = a_end * c_prev + _dot_tn((v.astype(F32) * w_s).astype(BF16), k)
        n_ref[0, h:h + 1, :] = a_end * n_prev + jnp.sum(w_s * k.astype(F32), axis=0, keepdims=True)
        m_ref[0, h:h + 1, :] = jnp.broadcast_to(m_new, (1, m_ref.shape[2]))
        hn = hh * lax.rsqrt(jnp.mean(hh * hh, -1, keepdims=True) + RMS_EPS)
        hn = hn * gain_ref[:, h * dv:(h + 1) * dv] * og_ref[:, h * dv:(h + 1) * dv]
        hn_ref[:, h * dv:(h + 1) * dv] = hn.astype(hn_ref.dtype)


def mlstm_prompt(q, k, v, og, gc, gr, gain, b, s):
    m, d = q.shape
    nh = MLSTM_HEADS
    dk = dv = d // nh
    ln = _row_tile(s, MLSTM_CHUNK)
    nch = s // ln
    row = lambda n: pl.BlockSpec((ln, n), lambda bi, ci: (bi * nch + ci, 0))
    return pl.pallas_call(
        functools.partial(_mlstm_prompt_kernel, nh=nh, dk=dk, dv=dv, ln=ln), grid=(b, nch),
        in_specs=[row(d), row(d), row(d), row(d), row(LANES),
                  pl.BlockSpec((SUBLANES, ln), lambda bi, ci: (0, bi * nch + ci)), _full((1, d))],
        out_specs=[row(d), pl.BlockSpec((1, nh, dv, dk), lambda bi, ci: (bi, 0, 0, 0)),
                   pl.BlockSpec((1, nh, dk), lambda bi, ci: (bi, 0, 0)),
                   pl.BlockSpec((1, nh, LANES), lambda bi, ci: (bi, 0, 0))],
        out_shape=[jax.ShapeDtypeStruct((m, d), BF16), jax.ShapeDtypeStruct((b, nh, dv, dk), F32),
                   jax.ShapeDtypeStruct((b, nh, dk), F32), jax.ShapeDtypeStruct((b, nh, LANES), F32)],
        compiler_params=_cparams(("parallel", "arbitrary"), VMEM_LIMIT), name="mlstm_prompt",
    )(q, k, v, og, gc, gr, gain.reshape(1, d))


def _mlstm_sample_kernel(q_ref, k_ref, v_ref, og_ref, g_ref, gain_ref, c_ref, n_ref, m_ref,
                         hn_ref, co_ref, no_ref, mo_ref, *, nh):
    row = lax.broadcasted_iota(jnp.int32, (SUBLANES, 1), 0)
    for h in range(nh):
        q = q_ref[0, h:h + 1, :]
        k = k_ref[0, h:h + 1, :]
        v = v_ref[0, h:h + 1, :].astype(F32)
        ig = g_ref[0, h:h + 1, 0:1]
        lf = g_ref[0, h:h + 1, 1:2]
        m_prev = m_ref[0, h:h + 1, :]
        c_prev = c_ref[0, h]
        n_prev = n_ref[0, h:h + 1, :]
        inter = lf + m_prev
        mt = jnp.maximum(inter, ig)
        wm = jnp.exp(ig - mt)
        a = jnp.exp(inter - mt)
        q8 = jnp.broadcast_to(q, (SUBLANES, q.shape[1]))
        cq = _dot_nt(q8, c_prev.astype(BF16))[0:1, :]
        wqk = wm * jnp.sum(q.astype(F32) * k.astype(F32), -1, keepdims=True)
        num = a * cq + wqk * v
        den = a * jnp.sum(n_prev * q.astype(F32), -1, keepdims=True) + wqk
        hh = num / jnp.maximum(jnp.abs(den), jnp.exp(-mt))
        v8 = jnp.where(row == 0, jnp.broadcast_to(v * wm, (SUBLANES, v.shape[1])), 0.0).astype(BF16)
        k8 = jnp.broadcast_to(k, (SUBLANES, k.shape[1]))
        co_ref[0, h] = a * c_prev + _dot_tn(v8, k8)
        no_ref[0, h:h + 1, :] = a * n_prev + wm * k.astype(F32)
        mo_ref[0, h:h + 1, :] = mt
        hn = hh * lax.rsqrt(jnp.mean(hh * hh, -1, keepdims=True) + RMS_EPS)
        hn_ref[0, h:h + 1, :] = (hn * gain_ref[h:h + 1, :] * og_ref[0, h:h + 1, :]).astype(hn_ref.dtype)


def mlstm_sample(q, k, v, og, gc, gain, c, n, m):
    bd, d = q.shape
    nh = MLSTM_HEADS
    dk = d // nh
    heads = lambda a: a.reshape(bd, nh, dk)
    g2 = jnp.transpose(gc[:, :2 * nh].reshape(bd, 2, nh), (0, 2, 1))
    blk3 = lambda n2: pl.BlockSpec((1, nh, n2), lambda bi: (bi, 0, 0))
    cspec = pl.BlockSpec((1, nh, dk, dk), lambda bi: (bi, 0, 0, 0))
    hn, co, no, mo = pl.pallas_call(
        functools.partial(_mlstm_sample_kernel, nh=nh), grid=(bd,),
        in_specs=[blk3(dk), blk3(dk), blk3(dk), blk3(dk), blk3(2), _full((nh, dk)), cspec, blk3(dk), blk3(1)],
        out_specs=[blk3(dk), cspec, blk3(dk), blk3(1)],
        out_shape=[jax.ShapeDtypeStruct((bd, nh, dk), BF16), jax.ShapeDtypeStruct(c.shape, F32),
                   jax.ShapeDtypeStruct(n.shape, F32), jax.ShapeDtypeStruct((bd, nh, 1), F32)],
        compiler_params=_cparams(("parallel",)), name="mlstm_sample",
    )(heads(q), heads(k), heads(v), heads(og), g2, gain.reshape(nh, dk), c, n, m.reshape(bd, nh, 1))
    return hn.reshape(bd, d), co, no, mo.reshape(bd, nh)


def _row_to_col(row):
    n = row.shape[1]
    eye = lax.broadcasted_iota(jnp.int32, (n, n), 0) == lax.broadcasted_iota(jnp.int32, (n, n), 1)
    return jnp.sum(jnp.where(eye, row, 0.0), axis=1, keepdims=True)


def _head_pad(q, keep):
    q2 = jnp.concatenate([q] * NSA_KV_HEADS, axis=1)
    row = lax.broadcasted_iota(jnp.int32, q2.shape, 0)
    lane = lax.broadcasted_iota(jnp.int32, q2.shape, 1)
    return jnp.where((lane // NSA_HD == row // NSA_GROUP) & keep(row), q2, jnp.zeros_like(q2))


def _nsa_sample_cmp_kernel(pt_ref, q_ref, pages_ref, pe_ref, w_ref, biasc_ref, oc_ref, idx_ref, xbuf, xrow, sem,
                           *, n_pages, nc, ncp, nsb, n_sel, past):
    b = pl.program_id(0)
    nb = pl.num_programs(0)
    kvd = NSA_KV_HEADS * NSA_HD

    def page_copy(bb, slot, p, sl):
        return pltpu.make_async_copy(pages_ref.at[pt_ref[bb * n_pages + p], pl.ds(sl * kvd, kvd), :],
                                     xbuf.at[slot, sl, p], sem.at[slot])

    def start_all(bb, slot):
        def body(p, c):
            page_copy(bb, slot, p, 0).start()
            page_copy(bb, slot, p, 1).start()
            return c
        lax.fori_loop(0, n_pages, body, 0)

    @pl.when(b == 0)
    def _():
        start_all(0, 0)

    @pl.when(b + 1 < nb)
    def _():
        start_all(b + 1, (b + 1) % 2)

    slot = b % 2

    def wait_body(p, c):
        page_copy(b, slot, p, 0).wait()
        page_copy(b, slot, p, 1).wait()
        return c
    lax.fori_loop(0, n_pages, wait_body, 0)

    def file_page(p, c):
        for sl in range(2):
            _file_rows(xbuf[slot, sl, p], xrow, sl, p)
        return c
    lax.fori_loop(0, n_pages, file_page, 0, unroll=8)

    acc = _compress_filed(xrow, pe_ref, w_ref, nc)
    kc = acc[:, 0:kvd].astype(BF16)
    vc = acc[:, kvd:2 * kvd].astype(BF16)
    q = q_ref[0]
    nh = q.shape[0]
    qpad = _head_pad(q, lambda r: r >= 0)
    s = _dot_nt(qpad, kc)
    s = s + biasc_ref[:, 0:nc]
    e = jnp.exp(s - jnp.max(s, -1, keepdims=True))
    p_c = e / jnp.sum(e, -1, keepdims=True)
    o = _dot(p_c.astype(BF16), vc)
    row = lax.broadcasted_iota(jnp.int32, (nh, NSA_HD), 0)
    o_h = o[:, 0:NSA_HD]
    for k in range(1, NSA_KV_HEADS):
        o_h = jnp.where(row // NSA_GROUP == k, o[:, k * NSA_HD:(k + 1) * NSA_HD], o_h)
    oc_ref[0] = o_h
    prow = lax.broadcasted_iota(jnp.int32, p_c.shape, 0)
    lane = lax.broadcasted_iota(jnp.int32, (1, ncp), 1)
    blk = lane // 2
    cur = past // SEL_BLOCK
    forced = (blk == 0) | (blk == cur) | (blk == cur - 1)
    is_cand = ((lane % 2) == 0) & (lane < 2 * nsb)
    nselp = idx_ref.shape[1]
    rsel = lax.broadcasted_iota(jnp.int32, (nselp, 1), 0).astype(F32)
    out_lane = lax.broadcasted_iota(jnp.int32, (nselp, LANES), 1)
    result = jnp.full((nselp, LANES), -1, jnp.int32)
    for k in range(NSA_KV_HEADS):
        imp = jnp.sum(jnp.where(prow // NSA_GROUP == k, p_c, 0.0), axis=0, keepdims=True)
        imp = jnp.concatenate([imp, jnp.zeros((1, ncp - nc), F32)], axis=1)
        imp = _pair_sum(imp)
        score = jnp.where(forced, FORCE_SCORE, jnp.where(blk <= cur, imp, -1.0))
        score = jnp.where(is_cand, score, -2.0)
        sel, rank = _select_blocks(score, n_sel, nsb)
        hit = (rank == rsel) & (sel > 0.5)
        idx = jnp.sum(jnp.where(hit, (blk + 1).astype(F32), 0.0), axis=1, keepdims=True) - 1.0
        result = jnp.where(out_lane == k, idx.astype(jnp.int32), result)
    idx_ref[0] = result


def nsa_sample_cmp(q8, pages, page_table, pe2, wbd, rel_bias):
    bd, nh, hd = q8.shape
    n_pages = page_table.shape[1]
    past = n_pages * PAGE_SIZE
    nc = past // CMP_BLOCK
    nsb = -(-(past + 1) // SEL_BLOCK)
    n_sel = min(TOP_N, nsb)
    ncp = _round_up(max(nc, SEL_RATIO * nsb), LANES)
    nselp = _round_up(n_sel, SUBLANES)
    kvd = wbd.shape[-1]
    cend = jnp.arange(nc) * CMP_BLOCK + (CMP_BLOCK - 1)
    biasc = jnp.zeros((nh, ncp), F32).at[:, :nc].set(_bias_of(rel_bias, past - cend).T)
    grid_spec = pltpu.PrefetchScalarGridSpec(
        num_scalar_prefetch=1, grid=(bd,),
        in_specs=[pl.BlockSpec((1, nh, hd), lambda bi, pt: (bi, 0, 0)),
                  pl.BlockSpec(memory_space=pl.ANY),
                  pl.BlockSpec(pe2.shape, lambda bi, pt: (0, 0, 0)),
                  pl.BlockSpec(wbd.shape, lambda bi, pt: (0, 0, 0, 0)),
                  pl.BlockSpec((nh, ncp), lambda bi, pt: (0, 0))],
        out_specs=[pl.BlockSpec((1, nh, hd), lambda bi, pt: (bi, 0, 0)),
                   pl.BlockSpec((1, nselp, LANES), lambda bi, pt: (bi, 0, 0))],
        scratch_shapes=[pltpu.VMEM((2, 2, n_pages, kvd, PAGE_SIZE), F32),
                        pltpu.VMEM((2, CMP_BLOCK // SUBLANES, nc * SUBLANES, kvd), F32),
                        pltpu.SemaphoreType.DMA((2,))])
    oc, idx = pl.pallas_call(
        functools.partial(_nsa_sample_cmp_kernel, n_pages=n_pages, nc=nc, ncp=ncp, nsb=nsb, n_sel=n_sel, past=past),
        grid_spec=grid_spec,
        out_shape=[jax.ShapeDtypeStruct((bd, nh, hd), F32), jax.ShapeDtypeStruct((bd, nselp, LANES), jnp.int32)],
        compiler_params=_cparams(("arbitrary",), VMEM_LIMIT), name="nsa_sample_cmp",
    )(page_table.reshape(-1), q8, pages, pe2, wbd, biasc)
    sel_idx = jnp.transpose(idx[:, :n_sel, :NSA_KV_HEADS], (0, 2, 1))
    return oc, sel_idx


def _nsa_sample_att_kernel(pt_ref, si_ref, q_ref, g_ref, oc_ref, kvn_ref, wn_ref, wc_ref, pages_ref,
                           bsel_ref, bwin_ref, ob_ref, win_ref, selbuf, wall, sem,
                           *, n_pages, n_sel, past, wb):
    b = pl.program_id(0)
    nb = pl.num_programs(0)
    kvd = NSA_KV_HEADS * NSA_HD
    hd = NSA_HD
    n_blk_pages = past // SEL_BLOCK
    per_page = PAGE_SIZE // SEL_BLOCK
    n_slots = NSA_KV_HEADS * n_sel

    def blk_of(bb, j):
        return si_ref[bb * n_slots + j]

    def blk_copy(bb, slot, j):
        blk = jnp.clip(blk_of(bb, j), 0, n_blk_pages - 1)
        page = pt_ref[bb * n_pages + blk // per_page]
        return pltpu.make_async_copy(pages_ref.at[page, pl.ds(2 * kvd, 2 * kvd), :], selbuf.at[slot, j], sem.at[slot])

    def in_pages(bb, j):
        blk = blk_of(bb, j)
        return (blk >= 0) & (blk < n_blk_pages)

    def start_all(bb, slot):
        def body(j, c):
            @pl.when(in_pages(bb, j))
            def _():
                blk_copy(bb, slot, j).start()
            return c
        lax.fori_loop(0, n_slots, body, 0)

    @pl.when(b == 0)
    def _():
        start_all(0, 0)

    @pl.when(b + 1 < nb)
    def _():
        start_all(b + 1, (b + 1) % 2)

    slot = b % 2
    new_sel = _row_to_col(kvn_ref[0][:, 2 * kvd:4 * kvd])
    lane = lax.broadcasted_iota(jnp.int32, (1, PAGE_SIZE), 1)

    def wait_body(j, c):
        @pl.when(in_pages(b, j))
        def _():
            blk_copy(b, slot, j).wait()

        @pl.when(jnp.logical_not(in_pages(b, j)))
        def _():
            is_new = blk_of(b, j) == n_blk_pages
            selbuf[slot, j] = jnp.where((lane == 0) & is_new, new_sel, 0.0)
        return c
    lax.fori_loop(0, n_slots, wait_body, 0)

    q = q_ref[0]
    nh = q.shape[0]
    gates = g_ref[0]
    head = lax.broadcasted_iota(jnp.int32, (nh, 1), 0)

    def attend(qp, keys_t, vals_t, bias, valid):
        s = jnp.where(valid, _dot(qp, keys_t) + bias, NEG)
        e = jnp.where(valid, jnp.exp(s - jnp.max(s, -1, keepdims=True)), 0.0)
        den = jnp.sum(e, -1, keepdims=True)
        p = e / jnp.where(den > 0, den, 1.0)
        return _dot_nt(p.astype(BF16), vals_t)

    o_s = jnp.zeros((nh, kvd), F32)
    for k in range(NSA_KV_HEADS):
        blks = [blk_of(b, k * n_sel + r) for r in range(n_sel)]
        tiles = [selbuf[slot, k * n_sel + r] for r in range(n_sel)]
        keys_t = jnp.concatenate([t_[0:kvd, :] for t_ in tiles], axis=1).astype(BF16)
        vals_t = jnp.concatenate([t_[kvd:2 * kvd, :] for t_ in tiles], axis=1).astype(BF16)
        bias = jnp.concatenate([bsel_ref[jnp.clip(bl // per_page, 0, n_pages)] for bl in blks], axis=1)
        valid = jnp.concatenate(
            [(lane // SEL_BLOCK == bl % per_page) & ((bl // per_page) * PAGE_SIZE + lane <= past) & (bl >= 0)
             for bl in blks], axis=1)
        o_k = attend(_head_pad(q, lambda r: r // NSA_GROUP == k), keys_t, vals_t, bias, valid)
        o_s = jnp.where(head // NSA_GROUP == k, o_k, o_s)
    wlanes = wall.shape[1]
    wall[:, 0:wb] = wc_ref[0]
    tail = lax.broadcasted_iota(jnp.int32, (1, wlanes - wb), 1)
    wall[:, wb:wlanes] = jnp.where(tail == 0, _row_to_col(wn_ref[0]), 0.0)
    win_ref[0] = pltpu.roll(wall[...], wlanes - 1, 1)[:, 0:wb]
    w_pos = lax.broadcasted_iota(jnp.int32, (1, wlanes), 1)
    valid_w = (w_pos <= wb) & (wb - w_pos < WINDOW) & (past - wb + w_pos >= 0)
    o_w = attend(_head_pad(q, lambda r: r >= 0), wall[0:kvd, :].astype(BF16), wall[kvd:2 * kvd, :].astype(BF16),
                 bwin_ref[...], valid_w)
    o_c = jnp.concatenate([oc_ref[0]] * NSA_KV_HEADS, axis=1)
    mix = gates[:, 0:1] * o_c + gates[:, 1:2] * o_s + gates[:, 2:3] * o_w
    out = mix[:, 0:hd]
    for k in range(1, NSA_KV_HEADS):
        out = jnp.where(head // NSA_GROUP == k, mix[:, k * hd:(k + 1) * hd], out)
    ob_ref[0] = out.astype(ob_ref.dtype)


def nsa_sample_att(q8, gates, oc, kv03, kv45, wcache_t, layer, pages_t, page_table, sel_idx, rel_bias):
    bd, nh, hd = q8.shape
    n_pages = page_table.shape[1]
    past = n_pages * PAGE_SIZE
    wb = wcache_t.shape[2]
    kvd = NSA_KV_HEADS * hd
    n_sel = sel_idx.shape[2]
    wlanes = _round_up(wb + 1, LANES)
    g3 = gates[:, :3 * nh].reshape(bd, nh, 3)
    dist = past - (jnp.arange(n_pages + 1)[:, None] * PAGE_SIZE + jnp.arange(PAGE_SIZE)[None, :])
    bsel = jnp.transpose(_bias_of(rel_bias, dist), (0, 2, 1))
    bwin = _bias_of(rel_bias, wb - jnp.arange(wlanes)).T
    blk = lambda n2, n3: pl.BlockSpec((1, n2, n3), lambda bi, pt, si: (bi, 0, 0))
    grid_spec = pltpu.PrefetchScalarGridSpec(
        num_scalar_prefetch=2, grid=(bd,),
        in_specs=[blk(nh, hd), blk(nh, 3), blk(nh, hd), blk(1, 4 * kvd), blk(1, 2 * kvd),
                  pl.BlockSpec((1, 2 * kvd, wb), lambda bi, pt, si: (layer * bd + bi, 0, 0)),
                  pl.BlockSpec(memory_space=pl.ANY),
                  pl.BlockSpec(bsel.shape, lambda bi, pt, si: (0, 0, 0)),
                  pl.BlockSpec(bwin.shape, lambda bi, pt, si: (0, 0))],
        out_specs=[blk(nh, hd), blk(2 * kvd, wb)],
        scratch_shapes=[pltpu.VMEM((2, NSA_KV_HEADS * n_sel, 2 * kvd, PAGE_SIZE), F32),
                        pltpu.VMEM((2 * kvd, wlanes), F32), pltpu.SemaphoreType.DMA((2,))])
    ob, win = pl.pallas_call(
        functools.partial(_nsa_sample_att_kernel, n_pages=n_pages, n_sel=n_sel, past=past, wb=wb),
        grid_spec=grid_spec,
        out_shape=[jax.ShapeDtypeStruct((bd, nh, hd), BF16), jax.ShapeDtypeStruct((bd, 2 * kvd, wb), F32)],
        compiler_params=_cparams(("arbitrary",), VMEM_LIMIT), name="nsa_sample_att",
    )(page_table.reshape(-1), sel_idx.reshape(-1), q8, g3, oc, kv03.reshape(bd, 1, -1), kv45.reshape(bd, 1, -1),
      wcache_t, pages_t, bsel, bwin)
    return ob.reshape(bd, nh * hd), win


def kernel(x_prompt, x_sample, mem_prompt, cache_conv, cache_nsa_pages, cache_nsa_window, state_mlstm_c,
           state_mlstm_n, state_mlstm_m, cache_mem_kv, page_table, rel_bias, norm_mix, norm_xattn, norm_mem,
           norm_ffn, norm_final, w_in_even, w_out_even, conv_w, conv_b, conv_ln_g, conv_ln_b, nsa_cmp_pe,
           nsa_cmp_w, w_in_odd, mlstm_b_i, mlstm_b_f, mlstm_norm, w_out_odd, xattn_wq, xattn_wkv, xattn_wo,
           ffn_w_gu, ffn_w_dn, router_w, router_b, expert_w_gu, expert_w_dn):
    b, s, d = x_prompt.shape
    bd, td, _ = x_sample.shape
    assert td == 1, "the sample group decodes one token per sequence"
    depth = norm_mix.shape[0]
    mt = mem_prompt.shape[1]
    cc = conv_w.shape[2]
    hist = conv_w.shape[1] - 1
    wb = cache_nsa_window.shape[2]
    kvh, hd = NSA_KV_HEADS, NSA_HD
    n_pool = cache_nsa_pages.shape[1]
    assert s >= hist and s >= wb and s % Q_BLOCK == 0
    xp = x_prompt.reshape(b * s, d)
    xs = x_sample.reshape(bd, d)
    mem = mem_prompt.reshape(b * mt, d)
    pages_t = jnp.swapaxes(cache_nsa_pages.reshape(-1, PAGE_SIZE, 4 * kvh * hd), 1, 2)
    window_t = jnp.swapaxes(cache_nsa_window.reshape(-1, wb, 2 * kvh * hd), 1, 2)
    xhd = d // X_HEADS
    memkv_rows = jnp.swapaxes(cache_mem_kv.reshape(depth * bd, mt, 2, X_HEADS, xhd // LANES, LANES), 3, 4)
    memkv_rows = memkv_rows.reshape(depth * bd, -1, LANES)
    bf = lambda a: a.astype(BF16)
    conv_p, conv_s, nsa_p, nsa_s, win_p, win_s = [], [], [], [], [], []
    mc_p, mc_s, mn_p, mn_s, mm_p, mm_s, memkv_p = [], [], [], [], [], [], []
    for l in range(depth):
        li = l // 2
        if l % 2 == 0:
            prm = prep_even(w_in_even[li], nsa_cmp_pe[li], nsa_cmp_w[li])
            w_out = bf(w_out_even[li])
            w_parts = [w_out[:cc], w_out[cc:]]
            conv_args = (conv_w[li], conv_b[li], conv_ln_g[li], conv_ln_b[li])
            glu, keys, qt, vt, gt, kv_t = inproj_even(xp, norm_mix[l], prm['w_in'], cc, s)
            a_out = conv_prompt(glu, *conv_args, b, s)
            kc, vct = compress_prompt(kv_t, prm['pe2'], prm['wbd'], _round_up(s // CMP_BLOCK, LANES))
            b_out = nsa_prompt(qt, gt, kc, vct, keys, vt, rel_bias, b, s)
            mix_p = ([a_out, b_out], w_parts)
            conv_p.append(glu.reshape(b, s, cc)[:, s - hist:])
            rows_t = kv_t.reshape(b, 6, kvh, hd, s)
            nsa_p.append(jnp.transpose(rows_t[:, :4], (0, 4, 1, 2, 3)))
            win_p.append(jnp.transpose(rows_t[:, 4:, :, :, s - wb:], (0, 4, 1, 2, 3)))
            glu, kv03, kv45, q, gates = inproj_even(xs, norm_mix[l], prm['w_in'], cc)
            a_out, conv_state = conv_sample(cache_conv[li], glu, *conv_args)
            q8 = q.reshape(bd, NSA_HEADS, hd)
            pt = page_table + li * n_pool
            o_c, sel_idx = nsa_sample_cmp(q8, pages_t, pt, prm['pe2'], prm['wbd'], rel_bias)
            b_out, win = nsa_sample_att(q8, gates, o_c, kv03, kv45, window_t, li, pages_t, pt, sel_idx, rel_bias)
            xs = outproj(xs, [a_out, b_out], w_parts)
            conv_s.append(conv_state)
            nsa_s.append(kv03.reshape(bd, 1, 4, kvh, hd))
            win_s.append(jnp.transpose(win.reshape(bd, 2, kvh, hd, wb), (0, 4, 1, 2, 3)))
        else:
            prm = prep_odd(w_in_odd[li], mlstm_b_i[li], mlstm_b_f[li])
            w_out = bf(w_out_odd[li])
            q, k, v, og, gc, gr = inproj_odd(xp, norm_mix[l], prm['w_in'], prm['bias'])
            hn, c_new, n_new, m_new = mlstm_prompt(q, k, v, og, gc, gr, mlstm_norm[li], b, s)
            mix_p = ([hn], [w_out])
            mc_p.append(c_new)
            mn_p.append(n_new)
            mm_p.append(m_new[:, :, 0])
            q, k, v, og, gc, gr = inproj_odd(xs, norm_mix[l], prm['w_in'], prm['bias'])
            hn, c_new, n_new, m_new = mlstm_sample(q, k, v, og, gc, mlstm_norm[li], state_mlstm_c[li],
                                                   state_mlstm_n[li], state_mlstm_m[li])
            xs = outproj(xs, [hn], [w_out])
            mc_s.append(c_new)
            mn_s.append(n_new)
            mm_s.append(m_new)
        wq, wo = bf(xattn_wq[l]), bf(xattn_wo[l])
        mkv_rows, mkv_b = memkv(mem, norm_mem[l], bf(xattn_wkv[l]), X_HEADS)
        mkv = jnp.swapaxes(mkv_rows.reshape(b, mt, 2, xhd // LANES, X_HEADS, LANES), 3, 4)
        memkv_p.append(mkv.reshape(b, mt, 2, X_HEADS, xhd))
        xs = xattn_sample(xs, norm_xattn[l], wq, memkv_rows, wo, l, mt)
        if l % 2 == 0:
            w_gu, w_dn = bf(ffn_w_gu[li]), bf(ffn_w_dn[li])
            xp = xattn_ffn(xp, norm_xattn[l], wq, mkv_b, wo, *mix_p, norm_ffn[l], w_gu, w_dn, b, s)
            xs = ffn(xs, norm_ffn[l], w_gu, w_dn)
        else:
            xp = xattn_prompt(xp, norm_xattn[l], wq, mkv_b, wo, b, s, *mix_p)
            e_gu, e_dn = bf(expert_w_gu[li]), bf(expert_w_dn[li])
            final_g = norm_final if l == depth - 1 else None
            comb, h, mask, counts = router(xp, norm_ffn[l], router_w[li], router_b[li])
            xp = moe_grouped(xp, h, comb, mask, counts, e_gu, e_dn, final_g)
            comb, h, _, _ = router(xs, norm_ffn[l], router_w[li], router_b[li])
            xs = moe(xs, h, comb, e_gu, e_dn, final_g)
    if depth % 2:
        xp, xs = rmsnorm(xp, norm_final), rmsnorm(xs, norm_final)
    y_prompt = xp.reshape(b, s, d)
    y_sample = xs.reshape(bd, 1, d)
    return (y_prompt, y_sample, jnp.stack(conv_p), jnp.stack(conv_s), jnp.stack(nsa_p), jnp.stack(nsa_s),
            jnp.stack(win_p), jnp.stack(win_s), jnp.stack(mc_p), jnp.stack(mc_s), jnp.stack(mn_p),
            jnp.stack(mn_s), jnp.stack(mm_p), jnp.stack(mm_s), jnp.stack(memkv_p))
```

```python
import functools
import math

import jax
import jax.numpy as jnp
from jax import lax
from jax.experimental import pallas as pl
from jax.experimental.pallas import tpu as pltpu

F32 = jnp.float32
BF16 = jnp.bfloat16

PAGE_SIZE = 128
CONV_WIDTH = 31
NSA_HEADS = 8
NSA_KV_HEADS = 2
NSA_GROUP = NSA_HEADS // NSA_KV_HEADS
NSA_HD = 64
CMP_BLOCK = 32
SEL_BLOCK = 64
SEL_RATIO = SEL_BLOCK // CMP_BLOCK
TOP_N = 16
WINDOW = 512
Q_BLOCK = 128
FORCE_SCORE = 1.0e4
NUM_BUCKETS = 32
MAX_DISTANCE = 1024
MLSTM_HEADS = 4
X_HEADS = 4
TOP_K = 2
RMS_EPS = 1e-6
LN_EPS = 1e-5
NEG = -1e30

LANES = 128
SUBLANES = 8
VMEM_LIMIT = 56 * 1024 * 1024
MLSTM_CHUNK = 256
SEL_TILES = 4
WIN_TILES = 5


def _cparams(sem, vmem=None):
    return pltpu.CompilerParams(dimension_semantics=sem, vmem_limit_bytes=vmem)


def _rms(x, g):
    return x * lax.rsqrt(jnp.mean(x * x, -1, keepdims=True) + RMS_EPS) * g


def _dot(a, b):
    return jnp.dot(a, b, preferred_element_type=F32)


def _dot_nt(a, b):
    return lax.dot_general(a, b, (((1,), (1,)), ((), ())), preferred_element_type=F32)


def _dot_tn(a, b):
    return lax.dot_general(a, b, (((0,), (0,)), ((), ())), preferred_element_type=F32)


def _full(shape):
    n = len(shape)
    return pl.BlockSpec(shape, lambda *_: (0,) * n)


def _row_tile(m, pref):
    t = min(pref, m)
    while m % t:
        t //= 2
    return t


def _rmsnorm_kernel(x_ref, g_ref, o_ref):
    o_ref[...] = _rms(x_ref[...], g_ref[...])


def rmsnorm(x, g):
    m, d = x.shape
    tm = _row_tile(m, 1024)
    return pl.pallas_call(
        _rmsnorm_kernel, grid=(m // tm,),
        in_specs=[pl.BlockSpec((tm, d), lambda i: (i, 0)), _full((1, d))],
        out_specs=pl.BlockSpec((tm, d), lambda i: (i, 0)),
        out_shape=jax.ShapeDtypeStruct((m, d), F32),
        compiler_params=_cparams(("parallel",)), name="rmsnorm",
    )(x, g.reshape(1, d))


def _outproj_kernel(*refs, n_in):
    x_ref = refs[0]
    a_refs = refs[1:1 + n_in]
    w_refs = refs[1 + n_in:1 + 2 * n_in]
    o_ref = refs[1 + 2 * n_in]
    acc = x_ref[...]
    for a_ref, w_ref in zip(a_refs, w_refs):
        acc = acc + _dot(a_ref[...], w_ref[...])
    o_ref[...] = acc


def outproj(x, acts, ws):
    m, d = x.shape
    tm = _row_tile(m, 512)
    n_in = len(acts)
    in_specs = [pl.BlockSpec((tm, d), lambda i: (i, 0))]
    in_specs += [pl.BlockSpec((tm, a.shape[1]), lambda i: (i, 0)) for a in acts]
    in_specs += [_full(w.shape) for w in ws]
    return pl.pallas_call(
        functools.partial(_outproj_kernel, n_in=n_in), grid=(m // tm,),
        in_specs=in_specs, out_specs=pl.BlockSpec((tm, d), lambda i: (i, 0)),
        out_shape=jax.ShapeDtypeStruct((m, d), F32),
        compiler_params=_cparams(("parallel",)), name="outproj",
    )(x, *acts, *ws)


def _inproj_even_kernel(x_ref, g_ref, w_ref, glu_ref, *rest, cc, qd, kvd, tiles):
    xn = _rms(x_ref[...], g_ref[...]).astype(BF16)

    def mm(lo, hi):
        return _dot(xn, w_ref[:, lo:hi])

    o = 0
    a = mm(o, o + cc)
    b = mm(o + cc, o + 2 * cc)
    glu_ref[...] = a * jax.nn.sigmoid(b)
    o += 2 * cc
    q = mm(o, o + qd) * (NSA_HD ** -0.5)
    o += qd
    kv03 = mm(o, o + 4 * kvd)
    o += 4 * kvd
    kv45 = mm(o, o + 2 * kvd)
    o += 2 * kvd
    gates = jax.nn.sigmoid(mm(o, o + LANES))
    if tiles == 0:
        kv03_ref, kv45_ref, q_ref, gate_ref = rest
        kv03_ref[...] = kv03
        kv45_ref[...] = kv45
        q_ref[...] = q.astype(BF16)
        gate_ref[...] = gates
        return
    keys_ref, qt_ref, vt_ref, gt_ref, kvt_ref = rest
    kvt_ref[0] = jnp.concatenate([kv03, kv45], axis=1).T
    keys_ref[...] = jnp.concatenate([kv03[:, 2 * kvd:3 * kvd], kv45[:, 0:kvd]], axis=1).astype(BF16)
    vals = jnp.concatenate([kv03[:, 3 * kvd:4 * kvd], kv45[:, kvd:2 * kvd]], axis=1)
    for j in range(tiles):
        rows = slice(j * Q_BLOCK, (j + 1) * Q_BLOCK)
        qt_ref[j] = q[rows, :].T.astype(BF16)
        vt_ref[j] = vals[rows, :].T.astype(BF16)
        gt_ref[j] = gates[rows, :].T


def inproj_even(x, g, w_pad, cc, seq=None):
    m, d = x.shape
    qd = NSA_HEADS * NSA_HD
    kvd = NSA_KV_HEADS * NSA_HD
    tm = _row_tile(m, 256)
    row = lambda n: pl.BlockSpec((tm, n), lambda i: (i, 0))
    out_specs = [row(cc)]
    out_shape = [jax.ShapeDtypeStruct((m, cc), F32)]
    transposed = seq is not None
    tiles = tm // Q_BLOCK if transposed else 0
    if transposed:
        assert tm % Q_BLOCK == 0 and seq % tm == 0
        per_seq = seq // tm
        tile = lambda n: pl.BlockSpec((tiles, n, Q_BLOCK), lambda i: (i, 0, 0))
        out_specs += [row(2 * kvd), tile(qd), tile(2 * kvd), tile(LANES),
                      pl.BlockSpec((1, 6 * kvd, tm), lambda i: (i // per_seq, 0, i % per_seq))]
        out_shape += [jax.ShapeDtypeStruct((m, 2 * kvd), BF16),
                      jax.ShapeDtypeStruct((m // Q_BLOCK, qd, Q_BLOCK), BF16),
                      jax.ShapeDtypeStruct((m // Q_BLOCK, 2 * kvd, Q_BLOCK), BF16),
                      jax.ShapeDtypeStruct((m // Q_BLOCK, LANES, Q_BLOCK), F32),
                      jax.ShapeDtypeStruct((m // seq, 6 * kvd, seq), F32)]
    else:
        out_specs += [row(4 * kvd), row(2 * kvd), row(qd), row(LANES)]
        out_shape += [jax.ShapeDtypeStruct((m, 4 * kvd), F32), jax.ShapeDtypeStruct((m, 2 * kvd), F32),
                      jax.ShapeDtypeStruct((m, qd), BF16), jax.ShapeDtypeStruct((m, LANES), F32)]
    return pl.pallas_call(
        functools.partial(_inproj_even_kernel, cc=cc, qd=qd, kvd=kvd, tiles=tiles), grid=(m // tm,),
        in_specs=[row(d), _full((1, d)), _full(w_pad.shape)],
        out_specs=out_specs, out_shape=out_shape,
        compiler_params=_cparams(("parallel",)), name="inproj_even",
    )(x, g.reshape(1, d), w_pad)


def _conv_post(y, lg, lb):
    mu = jnp.mean(y, -1, keepdims=True)
    var = jnp.mean(jnp.square(y - mu), -1, keepdims=True)
    yn = (y - mu) * lax.rsqrt(var + LN_EPS) * lg + lb
    return yn * jax.nn.sigmoid(yn)


CONV_SUB = 64
CONV_PAD = 32


def _conv_prompt_kernel(glu_ref, cw_ref, cb_ref, lg_ref, lb_ref, o_ref, ext_ref, y_ref, *, ts, s):
    i = pl.program_id(1)
    c = glu_ref.shape[-1]

    @pl.when(i == 0)
    def _():
        ext_ref[0:CONV_PAD, :] = jnp.zeros((CONV_PAD, c), F32)
        ext_ref[CONV_PAD:CONV_PAD + s, :] = glu_ref[0]
        ext_ref[CONV_PAD + s:CONV_PAD + s + SUBLANES, :] = jnp.zeros((SUBLANES, c), F32)

    lead = CONV_PAD - (CONV_WIDTH - 1)
    span = CONV_SUB + CONV_PAD

    def sub(j, carry):
        r0 = pl.multiple_of(i * ts + j * CONV_SUB, CONV_SUB)
        for c0 in range(0, c, LANES):
            xw = ext_ref[pl.ds(r0, span + SUBLANES), c0:c0 + LANES]
            acc = jnp.zeros((CONV_SUB, LANES), F32) + cb_ref[:, c0:c0 + LANES]
            for r in range(SUBLANES):
                xr = xw if r == 0 else pltpu.roll(xw, span + SUBLANES - r, 0)
                for a in range(span // SUBLANES):
                    w = SUBLANES * a + r - lead
                    if 0 <= w < CONV_WIDTH:
                        acc = acc + xr[SUBLANES * a:SUBLANES * a + CONV_SUB, :] * cw_ref[w:w + 1, c0:c0 + LANES]
            y_ref[:, c0:c0 + LANES] = acc
        o_ref[0, pl.ds(pl.multiple_of(j * CONV_SUB, CONV_SUB), CONV_SUB), :] = _conv_post(
            y_ref[...], lg_ref[...], lb_ref[...]).astype(o_ref.dtype)
        return carry

    lax.fori_loop(0, ts // CONV_SUB, sub, 0)


def conv_prompt(glu, cw, cb, lg, lb, b, s):
    c = glu.shape[-1]
    ts = _row_tile(s, 256)
    vec = lambda a: a.reshape(1, c)
    out = pl.pallas_call(
        functools.partial(_conv_prompt_kernel, ts=ts, s=s), grid=(b, s // ts),
        in_specs=[pl.BlockSpec((1, s, c), lambda bi, i: (bi, 0, 0)), _full((CONV_WIDTH, c)),
                  _full((1, c)), _full((1, c)), _full((1, c))],
        out_specs=pl.BlockSpec((1, ts, c), lambda bi, i: (bi, i, 0)),
        out_shape=jax.ShapeDtypeStruct((b, s, c), BF16),
        scratch_shapes=[pltpu.VMEM((CONV_PAD + s + SUBLANES, c), F32), pltpu.VMEM((CONV_SUB, c), F32)],
        compiler_params=_cparams(("parallel", "arbitrary")), name="conv_prompt",
    )(glu.reshape(b, s, c), cw, vec(cb), vec(lg), vec(lb))
    return out.reshape(b * s, c)


def _conv_sample_kernel(cache_ref, glu_ref, cw_ref, cb_ref, lg_ref, lb_ref, o_ref, st_ref):
    hist = CONV_WIDTH - 1
    cache = cache_ref[...]
    glu = glu_ref[...]
    y = jnp.sum(cache * cw_ref[0:hist, :][None], axis=1) + glu * cw_ref[hist:hist + 1, :] + cb_ref[...]
    o_ref[...] = _conv_post(y, lg_ref[...], lb_ref[...]).astype(o_ref.dtype)
    st_ref[:, 0:hist - 1, :] = cache[:, 1:hist, :]
    st_ref[:, hist - 1:hist, :] = glu[:, None, :]


def conv_sample(cache, glu, cw, cb, lg, lb):
    bd, hist, c = cache.shape
    vec = lambda a: a.reshape(1, c)
    return pl.pallas_call(
        _conv_sample_kernel,
        out_shape=[jax.ShapeDtypeStruct((bd, c), BF16), jax.ShapeDtypeStruct((bd, hist, c), F32)],
        name="conv_sample",
    )(cache, glu, cw, vec(cb), vec(lg), vec(lb))


def _rel_bucket(dist):
    n = jnp.maximum(dist, 0)
    max_exact = NUM_BUCKETS // 2
    nf = jnp.maximum(n, 1).astype(F32)
    large = max_exact + (jnp.log(nf / max_exact) / math.log(MAX_DISTANCE / max_exact)
                         * (NUM_BUCKETS - max_exact)).astype(jnp.int32)
    large = jnp.minimum(large, NUM_BUCKETS - 1)
    return jnp.where(n < max_exact, n, large)


def _bias_of(rel_bias, dist):
    bucket = _rel_bucket(dist)[..., None]
    out = jnp.zeros(bucket.shape[:-1] + (rel_bias.shape[1],), F32)
    for k in range(NUM_BUCKETS):
        out = jnp.where(bucket == k, rel_bias[k].astype(F32), out)
    return out


def _compress_accumulate(load_rows, pe_ref, w_ref, nc):
    accs = []
    half = CMP_BLOCK // 2
    for slot in range(2):
        acc = jnp.zeros((nc, w_ref.shape[-1]), F32)
        for j in range(half):
            pair = [load_rows(slot, jj) + pe_ref[slot, jj:jj + 1, :] for jj in (j, j + half)]
            acc = acc + _dot(jnp.concatenate(pair, axis=1).astype(BF16), w_ref[slot, j])
        accs.append(acc)
    return jnp.concatenate(accs, axis=1)


def _file_rows(tile_t, xrow, sl, page):
    rows = tile_t.T
    per_page = PAGE_SIZE // CMP_BLOCK
    for cl in range(per_page):
        for a in range(CMP_BLOCK // SUBLANES):
            r0 = cl * CMP_BLOCK + a * SUBLANES
            dst = pl.multiple_of((page * per_page + cl) * SUBLANES, SUBLANES)
            xrow[sl, a, pl.ds(dst, SUBLANES), :] = rows[r0:r0 + SUBLANES, :]


def _compress_filed(xrow, pe_ref, w_ref, nc):
    return _compress_accumulate(
        lambda sl, j: xrow[sl, j // SUBLANES, pl.ds(j % SUBLANES, nc, stride=SUBLANES), :], pe_ref, w_ref, nc)


def _compress_prompt_kernel(x_ref, pe_ref, w_ref, kc_ref, vct_ref, xrow, *, nc, ncp, kvd):
    for p in range(x_ref.shape[2] // PAGE_SIZE):
        for sl in range(2):
            _file_rows(x_ref[0, sl * kvd:(sl + 1) * kvd, p * PAGE_SIZE:(p + 1) * PAGE_SIZE], xrow, sl, p)
    acc = _compress_filed(xrow, pe_ref, w_ref, nc)
    if ncp > nc:
        acc = jnp.concatenate([acc, jnp.zeros((ncp - nc, 2 * kvd), F32)], axis=0)
    kc_ref[0] = acc[:, 0:kvd].astype(BF16)
    vct_ref[0] = acc[:, kvd:2 * kvd].T.astype(BF16)


def compress_prompt(kv_t, pe2, wbd, ncp):
    b, _, s = kv_t.shape
    assert s % PAGE_SIZE == 0
    nc = s // CMP_BLOCK
    kvd = wbd.shape[-1]
    return pl.pallas_call(
        functools.partial(_compress_prompt_kernel, nc=nc, ncp=ncp, kvd=kvd), grid=(b,),
        in_specs=[pl.BlockSpec((1, 2 * kvd, s), lambda bi: (bi, 0, 0)), _full(pe2.shape), _full(wbd.shape)],
        out_specs=[pl.BlockSpec((1, ncp, kvd), lambda bi: (bi, 0, 0)), pl.BlockSpec((1, kvd, ncp), lambda bi: (bi, 0, 0))],
        out_shape=[jax.ShapeDtypeStruct((b, ncp, kvd), BF16), jax.ShapeDtypeStruct((b, kvd, ncp), BF16)],
        scratch_shapes=[pltpu.VMEM((2, CMP_BLOCK // SUBLANES, nc * SUBLANES, kvd), F32)],
        compiler_params=_cparams(("parallel",)), name="compress_prompt",
    )(kv_t, pe2, wbd)


def _select_blocks(score, n_sel, n_cand):
    lane = lax.broadcasted_iota(jnp.int32, score.shape, 1)
    rank = jnp.zeros(score.shape, F32)
    for i in range(n_cand):
        col = score[:, 2 * i:2 * i + 1]
        beats = (col > score) | ((col == score) & (lane > 2 * i))
        rank = rank + beats.astype(F32)
    is_cand = ((lane % 2) == 0) & (lane < 2 * n_cand)
    return (is_cand & (rank < n_sel) & (score >= 0)).astype(F32), rank


def _pair_sum(imp):
    n = imp.shape[1]
    return imp + pltpu.roll(imp, n - 1, 1)


def _rank_rows(score, n_sel, n_cand):
    blk = lax.broadcasted_iota(jnp.int32, score.shape, 0)
    rank = jnp.zeros(score.shape, F32)
    for i in range(n_cand):
        row = score[i:i + 1, :]
        beats = (row > score) | ((row == score) & (blk > i))
        rank = rank + beats.astype(F32)
    return ((rank < n_sel) & (score >= 0)).astype(F32)


def _nsa_prompt_kernel(qt_ref, gt_ref, kc_ref, vct_ref, keys_ref, vt_ref, biasc_ref, btile_ref, o_ref,
                       qt_scr, oc_scr, acc_s, acc_w, imp_scr, sel_scr, out_scr, *, nc, nsb, n_sel):
    qi = pl.program_id(1)
    g, hd, kvh, qb = NSA_GROUP, NSA_HD, NSA_KV_HEADS, Q_BLOCK
    kvd = kvh * hd
    ncp = kc_ref.shape[1]
    nsbp = sel_scr.shape[1]
    q_pos = qi * qb + lax.broadcasted_iota(jnp.int32, (1, qb), 1)
    key_row = lax.broadcasted_iota(jnp.int32, (qb, 1), 0)
    c_row = lax.broadcasted_iota(jnp.int32, (ncp, 1), 0)
    mask_c = (q_pos >= c_row * CMP_BLOCK + (CMP_BLOCK - 1)) & (c_row < nc)
    blk = lax.broadcasted_iota(jnp.int32, (nsbp, 1), 0)
    cur = q_pos // SEL_BLOCK
    forced = (blk == 0) | (blk == cur) | (blk == cur - 1)
    zeros = jnp.zeros((hd, qb), BF16)
    for k in range(kvh):
        for gi in range(g):
            h = k * g + gi
            parts = [zeros] * kvh
            parts[k] = qt_ref[0, h * hd:(h + 1) * hd, :]
            qt_scr[k, :, gi * qb:(gi + 1) * qb] = jnp.concatenate(parts, axis=0)
        s_c = _dot(kc_ref[0], qt_scr[k])
        imp = jnp.zeros((ncp, qb), F32)
        probs = []
        for gi in range(g):
            s = jnp.where(mask_c, s_c[:, gi * qb:(gi + 1) * qb] + biasc_ref[k * g + gi], NEG)
            e = jnp.where(mask_c, jnp.exp(s - jnp.max(s, 0, keepdims=True)), 0.0)
            den = jnp.sum(e, 0, keepdims=True)
            p = e / jnp.where(den > 0, den, 1.0)
            imp = imp + p
            probs.append(p.astype(BF16))
        oc_scr[k] = _dot(vct_ref[0, k * hd:(k + 1) * hd, :], jnp.concatenate(probs, axis=1))
        imp_scr[...] = imp + pltpu.roll(imp, ncp - 1, 0)
        cand = imp_scr[pl.ds(0, nsbp, stride=SEL_RATIO), :]
        score = jnp.where(forced, FORCE_SCORE, jnp.where(blk <= cur, cand, -1.0))
        sel_scr[k] = _rank_rows(jnp.where(blk < nsb, score, -2.0), n_sel, nsb)

    per_tile = qb // SEL_BLOCK
    n_tiles = keys_ref.shape[1] // qb
    first = ([jnp.full((1, qb), NEG, F32)] * g, [jnp.zeros((1, qb), F32)] * g)

    def tile_step(tiles, carry, key_col, val_row, acc_ref, window):
        kts = [jnp.clip(kt, 0, n_tiles - 1) for kt, _ in tiles]
        starts = [pl.multiple_of(kt * qb, qb) for kt in kts]
        k_t = jnp.concatenate([keys_ref[0, pl.ds(r0, qb), key_col:key_col + kvd] for r0 in starts], axis=0)
        dist = jnp.concatenate([jnp.where(active, q_pos - (r0 + key_row), -1)
                                for r0, (_, active) in zip(starts, tiles)], axis=0)
        in_range = dist >= 0
        scores = [_dot(k_t, qt_scr[k]) for k in range(kvh)]
        new, updates = [], []
        for k in range(kvh):
            if window:
                valid = in_range & (dist < WINDOW)
            else:
                pieces = []
                for kt in kts:
                    chosen = jnp.zeros((qb, qb), F32)
                    for j in range(per_tile):
                        row = sel_scr[k, pl.ds(per_tile * kt + j, 1), :]
                        chosen = jnp.where(key_row // SEL_BLOCK == j, row, chosen)
                    pieces.append(chosen)
                valid = in_range & (jnp.concatenate(pieces, axis=0) > 0.5)
            ms, ls = carry[k]
            ms2, ls2, alphas, probs = [], [], [], []
            for gi in range(g):
                bias = jnp.concatenate([btile_ref[jnp.maximum(qi - kt, 0), k * g + gi] for kt in kts], axis=0)
                s = jnp.where(valid, scores[k][:, gi * qb:(gi + 1) * qb] + bias, NEG)
                m_new = jnp.maximum(ms[gi], jnp.max(s, 0, keepdims=True))
                alpha = jnp.exp(ms[gi] - m_new)
                p = jnp.exp(s - jnp.where(m_new == NEG, 0.0, m_new))
                ms2.append(m_new)
                ls2.append(alpha * ls[gi] + jnp.sum(p, 0, keepdims=True))
                alphas.append(alpha)
                probs.append(p.astype(BF16))
            new.append((ms2, ls2))
            updates.append((jnp.concatenate(alphas, axis=1), jnp.concatenate(probs, axis=1)))
        for k, (alpha, prob) in enumerate(updates):
            v_t = jnp.concatenate([vt_ref[kt, val_row + k * hd:val_row + (k + 1) * hd, :] for kt in kts], axis=1)
            acc_ref[k] = acc_ref[k] * alpha + _dot(v_t, prob)
        return tuple(new)

    acc_s[...] = jnp.zeros_like(acc_s)
    acc_w[...] = jnp.zeros_like(acc_w)
    sel_args = dict(key_col=0, val_row=0, acc_ref=acc_s, window=False)
    win_args = dict(key_col=kvd, val_row=kvd, acc_ref=acc_w, window=True)

    def sel_group(i, carry):
        return tile_step([(SEL_TILES * i + j, SEL_TILES * i + j <= qi) for j in range(SEL_TILES)], carry, **sel_args)

    stat_s = lax.fori_loop(0, qi // SEL_TILES + 1, sel_group, (first,) * kvh)
    stat_w = (first,) * kvh
    win_tiles = [(qi - j, qi - j >= 0) for j in range(WINDOW // qb, -1, -1)]
    for j in range(0, len(win_tiles), WIN_TILES):
        stat_w = tile_step(win_tiles[j:j + WIN_TILES], stat_w, **win_args)
    gt = gt_ref[0]
    for k in range(kvh):
        for gi in range(g):
            h = k * g + gi
            cols = slice(gi * qb, (gi + 1) * qb)
            l_s, l_w = stat_s[k][1][gi], stat_w[k][1][gi]
            o_s = acc_s[k, :, cols] / jnp.where(l_s > 0, l_s, 1.0)
            o_w = acc_w[k, :, cols] / jnp.where(l_w > 0, l_w, 1.0)
            out_scr[h * hd:(h + 1) * hd, :] = (gt[3 * h:3 * h + 1, :] * oc_scr[k, :, cols]
                                               + gt[3 * h + 1:3 * h + 2, :] * o_s + gt[3 * h + 2:3 * h + 3, :] * o_w)
    o_ref[0] = out_scr[...].T.astype(o_ref.dtype)


def nsa_prompt(qt, gt, kc, vct, keys, vt, rel_bias, b, s):
    qb = Q_BLOCK
    nq = s // qb
    nc = s // CMP_BLOCK
    ncp = kc.shape[1]
    nsb = s // SEL_BLOCK
    nsbp = _round_up(nsb, SUBLANES)
    assert SEL_RATIO * nsbp <= ncp
    n_sel = min(TOP_N, nsb)
    nh = NSA_HEADS
    hq = qt.shape[1]
    kvd = kc.shape[2]
    glanes = NSA_GROUP * qb
    cend = jnp.arange(ncp)[:, None] * CMP_BLOCK + (CMP_BLOCK - 1)
    biasc = jnp.transpose(_bias_of(rel_bias, jnp.arange(s)[None, :] - cend), (2, 0, 1))
    r = jnp.arange(qb)
    dist = jnp.arange(nq)[:, None, None] * qb + r[None, None, :] - r[None, :, None]
    btile = jnp.transpose(_bias_of(rel_bias, dist), (0, 3, 1, 2))
    out = pl.pallas_call(
        functools.partial(_nsa_prompt_kernel, nc=nc, nsb=nsb, n_sel=n_sel), grid=(b, nq),
        in_specs=[pl.BlockSpec((1, hq, qb), lambda bi, i: (bi * nq + i, 0, 0)),
                  pl.BlockSpec((1, LANES, qb), lambda bi, i: (bi * nq + i, 0, 0)),
                  pl.BlockSpec((1, ncp, kvd), lambda bi, i: (bi, 0, 0)),
                  pl.BlockSpec((1, kvd, ncp), lambda bi, i: (bi, 0, 0)),
                  pl.BlockSpec((1, s, 2 * kvd), lambda bi, i: (bi, 0, 0)),
                  pl.BlockSpec((nq, 2 * kvd, qb), lambda bi, i: (bi, 0, 0)),
                  pl.BlockSpec((nh, ncp, qb), lambda bi, i: (0, 0, i)),
                  _full(btile.shape)],
        out_specs=pl.BlockSpec((1, qb, hq), lambda bi, i: (bi, i, 0)),
        out_shape=jax.ShapeDtypeStruct((b, s, hq), BF16),
        scratch_shapes=[pltpu.VMEM((NSA_KV_HEADS, kvd, glanes), BF16), pltpu.VMEM((NSA_KV_HEADS, NSA_HD, glanes), F32),
                        pltpu.VMEM((NSA_KV_HEADS, NSA_HD, glanes), F32), pltpu.VMEM((NSA_KV_HEADS, NSA_HD, glanes), F32),
                        pltpu.VMEM((ncp, qb), F32), pltpu.VMEM((NSA_KV_HEADS, nsbp, qb), F32),
                        pltpu.VMEM((hq, qb), F32)],
        compiler_params=_cparams(("parallel", "arbitrary"), VMEM_LIMIT), name="nsa_prompt",
    )(qt, gt, kc, vct, keys.reshape(b, s, 2 * kvd), vt, biasc, btile)
    return out.reshape(b * s, hq)


def _round_up(x, m):
    return (x + m - 1) // m * m


def prep_even(w_in, pe, wc):
    d, n = w_in.shape
    n_pad = _round_up(n - 3 * NSA_HEADS, LANES) + LANES
    w_pad = jnp.zeros((d, n_pad), BF16).at[:, :n].set(w_in.astype(BF16))
    pe2 = jnp.tile(pe, (1, 1, NSA_KV_HEADS))
    zero = jnp.zeros_like(wc)
    wbd = jnp.concatenate([jnp.concatenate([wc if i == j else zero for j in range(NSA_KV_HEADS)], axis=-1)
                           for i in range(NSA_KV_HEADS)], axis=-2)
    half = CMP_BLOCK // 2
    wbd = jnp.concatenate([wbd[:, :half], wbd[:, half:]], axis=2)
    return dict(w_in=w_pad, pe2=pe2, wbd=wbd.astype(BF16))


def _memkv_kernel(x_ref, g_ref, w_ref, o_ref, ob_ref, *, hd, nh):
    y = _dot(_rms(x_ref[...], g_ref[...]).astype(BF16), w_ref[...])
    ob_ref[...] = y.astype(BF16)
    tm = y.shape[0]
    chunks = hd // LANES
    period = 2 * chunks * nh
    for kv in range(2):
        for h in range(nh):
            for c in range(chunks):
                col = (kv * nh + h) * hd + c * LANES
                o_ref[pl.ds((kv * chunks + c) * nh + h, tm, stride=period), :] = y[:, col:col + LANES]


def memkv(mem, g, w, nh):
    m, d = mem.shape
    n = w.shape[1]
    hd = n // (2 * nh)
    per_tok = n // LANES
    tm = _row_tile(m, 256)
    return pl.pallas_call(
        functools.partial(_memkv_kernel, hd=hd, nh=nh), grid=(m // tm,),
        in_specs=[pl.BlockSpec((tm, d), lambda i: (i, 0)), _full((1, d)), _full(w.shape)],
        out_specs=[pl.BlockSpec((tm * per_tok, LANES), lambda i: (i, 0)), pl.BlockSpec((tm, n), lambda i: (i, 0))],
        out_shape=[jax.ShapeDtypeStruct((m * per_tok, LANES), F32), jax.ShapeDtypeStruct((m, n), BF16)],
        compiler_params=_cparams(("parallel",)), name="memkv",
    )(mem, g.reshape(1, d), w)


def _xattn_core(q, kv, hd):
    nh = q.shape[1] // hd
    outs = []
    for h in range(nh):
        s = _dot_nt(q[:, h * hd:(h + 1) * hd], kv[:, h * hd:(h + 1) * hd])
        e = jnp.exp(s - jnp.max(s, -1, keepdims=True))
        p = e / jnp.sum(e, -1, keepdims=True)
        outs.append(_dot(p.astype(BF16), kv[:, (nh + h) * hd:(nh + h + 1) * hd]))
    return jnp.concatenate(outs, axis=1).astype(BF16)


def _xattn_prompt_kernel(*refs, hd, n_in):
    x_ref, g_ref, wq_ref, kv_ref, wo_ref = refs[:5]
    a_refs = refs[5:5 + n_in]
    w_refs = refs[5 + n_in:5 + 2 * n_in]
    o_ref = refs[5 + 2 * n_in]
    x = x_ref[0]
    for a_ref, w_ref in zip(a_refs, w_refs):
        x = x + _dot(a_ref[0], w_ref[...])
    q = (_dot(_rms(x, g_ref[...]).astype(BF16), wq_ref[...]) * (hd ** -0.5)).astype(BF16)
    o = _xattn_core(q, kv_ref[0], hd)
    o_ref[0] = x + _dot(o, wo_ref[...])


def xattn_prompt(x, g, wq, kvb, wo, b, s, acts=(), ws=()):
    d = x.shape[1]
    mt = kvb.shape[0] // b
    tm = _row_tile(s, 512)
    tile = lambda n: pl.BlockSpec((1, tm, n), lambda bi, i: (bi, i, 0))
    out = pl.pallas_call(
        functools.partial(_xattn_prompt_kernel, hd=d // X_HEADS, n_in=len(acts)), grid=(b, s // tm),
        in_specs=[tile(d), _full((1, d)), _full(wq.shape),
                  pl.BlockSpec((1, mt, kvb.shape[1]), lambda bi, i: (bi, 0, 0)), _full(wo.shape)]
        + [tile(a.shape[1]) for a in acts] + [_full(w.shape) for w in ws],
        out_specs=tile(d),
        out_shape=jax.ShapeDtypeStruct((b, s, d), F32),
        compiler_params=_cparams(("parallel", "parallel"), VMEM_LIMIT), name="xattn_prompt",
    )(x.reshape(b, s, d), g.reshape(1, d), wq, kvb.reshape(b, mt, -1), wo,
      *[a.reshape(b, s, -1) for a in acts], *ws)
    return out.reshape(b * s, d)


def _xattn_sample_kernel(x_ref, g_ref, wq_ref, kv_ref, wo_ref, o_ref, q_scr, a_scr, *, hd, mt):
    bi = pl.program_id(0)
    nb = pl.num_programs(0)
    nh = wq_ref.shape[1] // hd

    @pl.when(bi == 0)
    def _():
        q_scr[...] = _dot(_rms(x_ref[...], g_ref[...]).astype(BF16), wq_ref[...]) * (hd ** -0.5)

    q = jnp.broadcast_to(q_scr[pl.ds(bi, 1), :], (SUBLANES, q_scr.shape[1])).astype(BF16)
    chunks = hd // LANES
    period = 2 * chunks * nh

    def head_rows(kv, h):
        return jnp.concatenate([kv_ref[0, pl.ds((kv * chunks + c) * nh + h, mt, stride=period), :]
                                for c in range(chunks)], axis=1).astype(BF16)

    outs = []
    for h in range(nh):
        s = _dot_nt(q[:, h * hd:(h + 1) * hd], head_rows(0, h))
        e = jnp.exp(s - jnp.max(s, -1, keepdims=True))
        p = e / jnp.sum(e, -1, keepdims=True)
        outs.append(_dot(p.astype(BF16), head_rows(1, h)))
    a_scr[pl.ds(bi, 1), :] = jnp.concatenate(outs, axis=1)[0:1, :]

    @pl.when(bi == nb - 1)
    def _():
        o_ref[...] = x_ref[...] + _dot(a_scr[...].astype(BF16), wo_ref[...])


def xattn_sample(x, g, wq, kv_rows, wo, layer, mt):
    bd, d = x.shape
    rows = kv_rows.shape[1]
    return pl.pallas_call(
        functools.partial(_xattn_sample_kernel, hd=d // X_HEADS, mt=mt), grid=(bd,),
        in_specs=[_full((bd, d)), _full((1, d)), _full(wq.shape),
                  pl.BlockSpec((1, rows, LANES), lambda bi: (layer * bd + bi, 0, 0)), _full(wo.shape)],
        out_specs=_full((bd, d)),
        out_shape=jax.ShapeDtypeStruct((bd, d), F32),
        scratch_shapes=[pltpu.VMEM((bd, wq.shape[1]), F32), pltpu.VMEM((bd, wq.shape[1]), F32)],
        compiler_params=_cparams(("arbitrary",), VMEM_LIMIT), name="xattn_sample",
    )(x, g.reshape(1, d), wq, kv_rows, wo)


def _ffn_kernel(x_ref, g_ref, wg_ref, wu_ref, wd_ref, o_ref, h_scr, acc_scr):
    c = pl.program_id(1)

    @pl.when(c == 0)
    def _():
        h_scr[...] = _rms(x_ref[...], g_ref[...]).astype(BF16)
        acc_scr[...] = x_ref[...]

    h = h_scr[...]
    gate = _dot(h, wg_ref[...])
    up = _dot(h, wu_ref[...])
    act = (gate * jax.nn.sigmoid(gate) * up).astype(BF16)
    acc_scr[...] += _dot(act, wd_ref[...])

    @pl.when(c == pl.num_programs(1) - 1)
    def _():
        o_ref[...] = acc_scr[...]


def _xattn_ffn_kernel(*refs, hd, n_in):
    x_ref, gx_ref, wq_ref, kv_ref, wo_ref, gf_ref, wg_ref, wu_ref, wd_ref = refs[:9]
    a_refs = refs[9:9 + n_in]
    w_refs = refs[9 + n_in:9 + 2 * n_in]
    o_ref, h_scr, acc_scr = refs[9 + 2 * n_in:]
    c = pl.program_id(1)

    @pl.when(c == 0)
    def _():
        x = x_ref[...]
        for a_ref, w_ref in zip(a_refs, w_refs):
            x = x + _dot(a_ref[...], w_ref[...])
        q = (_dot(_rms(x, gx_ref[...]).astype(BF16), wq_ref[...]) * (hd ** -0.5)).astype(BF16)
        x = x + _dot(_xattn_core(q, kv_ref[0], hd), wo_ref[...])
        h_scr[...] = _rms(x, gf_ref[...]).astype(BF16)
        acc_scr[...] = x

    h = h_scr[...]
    gate = _dot(h, wg_ref[...])
    up = _dot(h, wu_ref[...])
    act = (gate * jax.nn.sigmoid(gate) * up).astype(BF16)
    acc_scr[...] += _dot(act, wd_ref[...])

    @pl.when(c == pl.num_programs(1) - 1)
    def _():
        o_ref[...] = acc_scr[...]


def xattn_ffn(x, gx, wq, kvb, wo, acts, ws, gf, w_gu, w_dn, b, s):
    m, d = x.shape
    dff = w_dn.shape[0]
    mt = kvb.shape[0] // b
    tm = _row_tile(s, 512)
    per_seq = s // tm
    fc = _ff_chunk(dff, 1408)
    nch = dff // fc
    row = lambda n: pl.BlockSpec((tm, n), lambda i, c: (i, 0))
    return pl.pallas_call(
        functools.partial(_xattn_ffn_kernel, hd=d // X_HEADS, n_in=len(acts)), grid=(m // tm, nch),
        in_specs=[row(d), _full((1, d)), _full(wq.shape),
                  pl.BlockSpec((1, mt, kvb.shape[1]), lambda i, c: (i // per_seq, 0, 0)), _full(wo.shape),
                  _full((1, d)),
                  pl.BlockSpec((d, fc), lambda i, c: (0, c)),
                  pl.BlockSpec((d, fc), lambda i, c: (0, nch + c)),
                  pl.BlockSpec((fc, d), lambda i, c: (c, 0))]
        + [row(a.shape[1]) for a in acts] + [_full(w.shape) for w in ws],
        out_specs=row(d),
        out_shape=jax.ShapeDtypeStruct((m, d), F32),
        scratch_shapes=[pltpu.VMEM((tm, d), BF16), pltpu.VMEM((tm, d), F32)],
        compiler_params=_cparams(("parallel", "arbitrary"), VMEM_LIMIT), name="xattn_ffn",
    )(x, gx.reshape(1, d), wq, kvb.reshape(b, mt, -1), wo, gf.reshape(1, d), w_gu, w_gu, w_dn, *acts, *ws)


def _ff_chunk(dff, pref):
    c = dff
    for n in range(1, dff // LANES + 1):
        if dff % n == 0 and (dff // n) % LANES == 0 and dff // n <= pref:
            c = dff // n
            break
    return c


def ffn(x, g, w_gu, w_dn):
    m, d = x.shape
    dff = w_dn.shape[0]
    tm = _row_tile(m, 512)
    fc = _ff_chunk(dff, 1408)
    nch = dff // fc
    return pl.pallas_call(
        _ffn_kernel, grid=(m // tm, nch),
        in_specs=[pl.BlockSpec((tm, d), lambda i, c: (i, 0)), _full((1, d)),
                  pl.BlockSpec((d, fc), lambda i, c: (0, c)),
                  pl.BlockSpec((d, fc), lambda i, c: (0, nch + c)),
                  pl.BlockSpec((fc, d), lambda i, c: (c, 0))],
        out_specs=pl.BlockSpec((tm, d), lambda i, c: (i, 0)),
        out_shape=jax.ShapeDtypeStruct((m, d), F32),
        scratch_shapes=[pltpu.VMEM((tm, d), BF16), pltpu.VMEM((tm, d), F32)],
        compiler_params=_cparams(("parallel", "arbitrary"), VMEM_LIMIT), name="ffn",
    )(x, g.reshape(1, d), w_gu, w_gu, w_dn)


def _router_kernel(x_ref, g_ref, w_ref, b_ref, comb_ref, h_ref, mask_ref, cnt_ref, *, ne):
    h = _rms(x_ref[...], g_ref[...]).astype(BF16)
    h_ref[...] = h
    logits = _dot(h, w_ref[...]) + b_ref[...]
    lane = lax.broadcasted_iota(jnp.int32, logits.shape, 1)
    logits = jnp.where(lane < ne, logits, -jnp.inf)
    v1 = jnp.max(logits, -1, keepdims=True)
    i1 = jnp.min(jnp.where(logits == v1, lane, LANES), -1, keepdims=True)
    rest = jnp.where(lane == i1, -jnp.inf, logits)
    v2 = jnp.max(rest, -1, keepdims=True)
    i2 = jnp.min(jnp.where(rest == v2, lane, LANES), -1, keepdims=True)
    e2 = jnp.exp(v2 - v1)
    den = 1.0 + e2
    comb_ref[...] = jnp.where(lane == i1, 1.0 / den, 0.0) + jnp.where(lane == i2, e2 / den, 0.0)
    chosen = jnp.where((lane == i1) | (lane == i2), 1.0, 0.0)
    mask_ref[...] = chosen.astype(BF16)

    @pl.when(pl.program_id(0) == 0)
    def _():
        cnt_ref[...] = jnp.zeros_like(cnt_ref)

    cnt_ref[0:1, :] += jnp.sum(chosen, axis=0, keepdims=True)


def router(x, g, w_r, b_r):
    m, d = x.shape
    ne = w_r.shape[1]
    w_pad = jnp.zeros((d, LANES), BF16).at[:, :ne].set(w_r.astype(BF16))
    b_pad = jnp.zeros((1, LANES), F32).at[0, :ne].set(b_r.astype(F32))
    tm = _row_tile(m, 512)
    row = lambda n: pl.BlockSpec((tm, n), lambda i: (i, 0))
    return pl.pallas_call(
        functools.partial(_router_kernel, ne=ne), grid=(m // tm,),
        in_specs=[row(d), _full((1, d)), _full((d, LANES)), _full((1, LANES))],
        out_specs=[row(LANES), row(d), row(LANES), _full((SUBLANES, LANES))],
        out_shape=[jax.ShapeDtypeStruct((m, LANES), F32), jax.ShapeDtypeStruct((m, d), BF16),
                   jax.ShapeDtypeStruct((m, LANES), BF16), jax.ShapeDtypeStruct((SUBLANES, LANES), F32)],
        compiler_params=_cparams(("arbitrary",)), name="router",
    )(x, g.reshape(1, d), w_pad, b_pad)


def _residual_out(y, gain_ref, norm):
    return _rms(y, gain_ref[...]) if norm else y


def _moe_kernel(x_ref, h_ref, comb_ref, wg_ref, wu_ref, wd_ref, gain_ref, o_ref, acc_scr, *, norm):
    e = pl.program_id(1)

    @pl.when(e == 0)
    def _():
        acc_scr[...] = jnp.zeros_like(acc_scr)

    h = h_ref[...]
    gate = _dot(h, wg_ref[0])
    up = _dot(h, wu_ref[0])
    act = (gate * jax.nn.sigmoid(gate) * up).astype(BF16)
    y = _dot(act, wd_ref[0])
    comb = comb_ref[...]
    lane = lax.broadcasted_iota(jnp.int32, comb.shape, 1)
    acc_scr[...] += jnp.sum(jnp.where(lane == e, comb, 0.0), -1, keepdims=True) * y

    @pl.when(e == pl.num_programs(1) - 1)
    def _():
        o_ref[...] = _residual_out(x_ref[...] + acc_scr[...], gain_ref, norm)


def moe(x, h, comb, w_gu, w_dn, final_g=None):
    m, d = x.shape
    ne, dfe = w_dn.shape[:2]
    tm = _row_tile(m, 512)
    gain = jnp.ones((1, d), F32) if final_g is None else final_g.reshape(1, d)
    return pl.pallas_call(
        functools.partial(_moe_kernel, norm=final_g is not None), grid=(m // tm, ne),
        in_specs=[pl.BlockSpec((tm, d), lambda i, e: (i, 0)), pl.BlockSpec((tm, d), lambda i, e: (i, 0)),
                  pl.BlockSpec((tm, LANES), lambda i, e: (i, 0)),
                  pl.BlockSpec((1, d, dfe), lambda i, e: (e, 0, 0)),
                  pl.BlockSpec((1, d, dfe), lambda i, e: (e, 0, 1)),
                  pl.BlockSpec((1, dfe, d), lambda i, e: (e, 0, 0)), _full((1, d))],
        out_specs=pl.BlockSpec((tm, d), lambda i, e: (i, 0)),
        out_shape=jax.ShapeDtypeStruct((m, d), F32),
        scratch_shapes=[pltpu.VMEM((tm, d), F32)],
        compiler_params=_cparams(("parallel", "arbitrary"), VMEM_LIMIT), name="moe",
    )(x, h, comb, w_gu, w_gu, w_dn, gain)


MOE_TILE = 256


def _moe_pos_kernel(mask_ref, tri_ref, base_ref, post_ref, pos_ref, stab_ref, run_scr, *, nep):
    sb = pl.program_id(0)
    nb = pl.num_programs(0)

    @pl.when(sb == 0)
    def _():
        run_scr[...] = jnp.zeros_like(run_scr)
        stab_ref[...] = jnp.zeros_like(stab_ref)

    a = mask_ref[...]
    af = a.astype(F32)
    start = base_ref[...] + run_scr[...]
    stab_ref[pl.ds(sb, 1), :] = start.astype(jnp.int32)
    rank = _dot(tri_ref[...], a)
    pos = jnp.where(af > 0, start + rank, -1.0)
    pos_ref[...] = pos.astype(jnp.int32)
    post_ref[0] = pos.T[0:nep, :].astype(jnp.int32)
    run_scr[...] += jnp.sum(af, axis=0, keepdims=True)

    @pl.when(sb == nb - 1)
    def _():
        stab_ref[pl.ds(nb, 1), :] = (base_ref[...] + run_scr[...]).astype(jnp.int32)


def _moe_expert_kernel(te_ref, lo_ref, cnt_ref, ring_ref, nt_ref, h_ref, post_ref, wg_ref, wu_ref, wd_ref, y_ref,
                       hbuf, xg_scr, sem, *, t):
    i = pl.program_id(0)

    @pl.when(i >= nt_ref[0])
    def _():
        y_ref[...] = jnp.zeros_like(y_ref)

    nbuf = hbuf.shape[0]
    ahead = nbuf - 1

    def copy(sb, slot):
        return pltpu.make_async_copy(h_ref.at[pl.ds(pl.multiple_of(sb * t, t), t), :], hbuf.at[slot], sem.at[slot])

    def start_first(tile):
        for j in range(ahead):
            @pl.when(j < cnt_ref[tile])
            def _():
                copy(lo_ref[tile] + j, (ring_ref[tile] + j) % nbuf).start()

    @pl.when(i == 0)
    def _():
        start_first(0)

    @pl.when(i < nt_ref[0])
    def _():
        e = te_ref[i]
        lo = lo_ref[i]
        n = cnt_ref[i]
        ring = ring_ref[i]
        xg_scr[...] = jnp.zeros_like(xg_scr)
        row = i * t + lax.broadcasted_iota(jnp.int32, (t, 1), 0)

        def body(j, c):
            slot = (ring + j) % nbuf
            copy(lo + j, slot).wait()

            @pl.when(j + ahead < n)
            def _():
                copy(lo + j + ahead, (ring + j + ahead) % nbuf).start()

            src_pos = post_ref[lo + j, pl.ds(e, 1), :]
            onehot = jnp.where(src_pos == row, 1.0, 0.0).astype(BF16)
            xg_scr[...] += _dot(onehot, hbuf[slot])
            return c

        lax.fori_loop(0, n, body, 0)

        @pl.when(i + 1 < nt_ref[0])
        def _():
            start_first(i + 1)

        x = xg_scr[...].astype(BF16)
        gate = _dot(x, wg_ref[0])
        up = _dot(x, wu_ref[0])
        act = (gate * jax.nn.sigmoid(gate) * up).astype(BF16)
        y_ref[...] = _dot(act, wd_ref[0])


MOE_WIN = 128


def _moe_combine_kernel(stab_ref, x_ref, comb_ref, pos_ref, gain_ref, ys_ref, o_ref, ybuf, extra, acc_scr,
                        sem, sem_x, *, t, ne, norm):
    sb = pl.program_id(0)
    nb = pl.num_programs(0)
    n_win = (t + SUBLANES - 1) // MOE_WIN + 1

    def window(b, e, w):
        s0 = stab_ref[b * ne + e]
        s1 = stab_ref[(b + 1) * ne + e]
        start = (s0 // SUBLANES) * SUBLANES + w * MOE_WIN
        return pl.multiple_of(start, SUBLANES), start < s1

    def first_copy(b, slot, e):
        start, _ = window(b, e, 0)
        return pltpu.make_async_copy(ys_ref.at[pl.ds(start, MOE_WIN), :], ybuf.at[slot, e], sem.at[slot, e])

    def start_all(b, slot):
        for e in range(ne):
            @pl.when(window(b, e, 0)[1])
            def _():
                first_copy(b, slot, e).start()

    @pl.when(sb == 0)
    def _():
        ybuf[...] = jnp.zeros_like(ybuf)
        start_all(0, 0)

    @pl.when(sb + 1 < nb)
    def _():
        start_all(sb + 1, (sb + 1) % 2)

    slot = sb % 2
    comb = comb_ref[...]
    pos = pos_ref[...]
    lane_r = lax.broadcasted_iota(jnp.int32, (1, MOE_WIN), 1)

    def split(a):
        hi = a.astype(BF16)
        return hi, (a - hi.astype(F32)).astype(BF16)

    def weighted_onehot(start, e):
        return jnp.where(pos[:, e:e + 1] - start == lane_r, comb[:, e:e + 1], 0.0)

    def gathered(sel, rows):
        s_hi, s_lo = split(sel)
        r_hi, r_lo = split(rows)
        return _dot(s_hi, r_hi) + _dot(s_hi, r_lo) + _dot(s_lo, r_hi)

    for e in range(ne):
        @pl.when(window(sb, e, 0)[1])
        def _():
            first_copy(sb, slot, e).wait()
    sel = jnp.concatenate([weighted_onehot(window(sb, e, 0)[0], e) for e in range(ne)], axis=1)
    acc_scr[...] = x_ref[...] + gathered(sel, ybuf[slot].reshape(ne * MOE_WIN, ybuf.shape[-1]))

    for e in range(ne):
        for w in range(1, n_win):
            start_w, needed_w = window(sb, e, w)

            @pl.when(needed_w)
            def _():
                cp = pltpu.make_async_copy(ys_ref.at[pl.ds(start_w, MOE_WIN), :], extra, sem_x.at[0])
                cp.start()
                cp.wait()
                acc_scr[...] += gathered(weighted_onehot(start_w, e), extra[...])

    o_ref[...] = _residual_out(acc_scr[...], gain_ref, norm)


def moe_grouped(x, h, comb, mask, counts, w_gu, w_dn, final_g=None):
    m, d = x.shape
    ne, dfe = w_dn.shape[:2]
    t = MOE_TILE
    assert m % t == 0
    nb = m // t
    nep = _round_up(ne, SUBLANES)
    nbp = _round_up(nb + 1, SUBLANES)
    k_top = TOP_K
    nt_max = k_top * m // t + ne + 1
    cnt = counts[0, :ne].astype(jnp.int32)
    cnt_pad = (cnt + t - 1) // t * t
    ends = jnp.cumsum(cnt_pad)
    base = ends - cnt_pad
    base_row = jnp.zeros((1, LANES), F32).at[0, :ne].set(base.astype(F32))
    idx = lax.broadcasted_iota(jnp.int32, (t, t), 0)
    tri = jnp.where(lax.broadcasted_iota(jnp.int32, (t, t), 1) < idx, 1.0, 0.0).astype(BF16)
    blk = lambda n2: pl.BlockSpec((t, n2), lambda i: (i, 0))
    post, pos, stab = pl.pallas_call(
        functools.partial(_moe_pos_kernel, nep=nep), grid=(nb,),
        in_specs=[blk(LANES), _full((t, t)), _full((1, LANES))],
        out_specs=[pl.BlockSpec((1, nep, t), lambda i: (i, 0, 0)), blk(LANES), _full((nbp, LANES))],
        out_shape=[jax.ShapeDtypeStruct((nb, nep, t), jnp.int32), jax.ShapeDtypeStruct((m, LANES), jnp.int32),
                   jax.ShapeDtypeStruct((nbp, LANES), jnp.int32)],
        scratch_shapes=[pltpu.VMEM((1, LANES), F32)],
        compiler_params=_cparams(("arbitrary",)), name="moe_positions",
    )(mask, tri, base_row)
    r0 = jnp.arange(nt_max, dtype=jnp.int32) * t
    tile_e = jnp.minimum(jnp.sum(ends[None, :] <= r0[:, None], axis=1), ne - 1).astype(jnp.int32)
    n_tiles = (ends[-1] // t).astype(jnp.int32).reshape(1)
    s_e = stab[:nb + 1, :ne][:, tile_e]
    lo = jnp.sum(s_e[1:] <= r0[None, :], axis=0)
    hi = jnp.sum(s_e[:nb] < r0[None, :] + t, axis=0) - 1
    lo = jnp.clip(lo, 0, nb - 1).astype(jnp.int32)
    hi = jnp.clip(hi, lo, nb - 1).astype(jnp.int32)
    n_src = jnp.where(jnp.arange(nt_max) < n_tiles[0], hi - lo + 1, 0).astype(jnp.int32)
    n_ring = 6
    ring = ((jnp.cumsum(n_src) - n_src) % n_ring).astype(jnp.int32)
    w_spec = lambda shape, col: pl.BlockSpec(shape, lambda i, te, *_: (te[i], 0, col))
    grid_spec = pltpu.PrefetchScalarGridSpec(
        num_scalar_prefetch=5, grid=(nt_max,),
        in_specs=[pl.BlockSpec(memory_space=pl.ANY),
                  pl.BlockSpec((nb, nep, t), lambda i, *_: (0, 0, 0)),
                  w_spec((1, d, dfe), 0), w_spec((1, d, dfe), 1), w_spec((1, dfe, d), 0)],
        out_specs=pl.BlockSpec((t, d), lambda i, *_: (i, 0)),
        scratch_shapes=[pltpu.VMEM((n_ring, t, d), BF16), pltpu.VMEM((t, d), F32),
                        pltpu.SemaphoreType.DMA((n_ring,))])
    ys = pl.pallas_call(
        functools.partial(_moe_expert_kernel, t=t), grid_spec=grid_spec,
        out_shape=jax.ShapeDtypeStruct((nt_max * t, d), F32),
        compiler_params=_cparams(("arbitrary",), VMEM_LIMIT), name="moe_experts",
    )(tile_e, lo, n_src, ring, n_tiles, h, post, w_gu, w_gu, w_dn)
    gain = jnp.ones((1, d), F32) if final_g is None else final_g.reshape(1, d)
    blk2 = lambda n2: pl.BlockSpec((t, n2), lambda i, st: (i, 0))
    grid_spec = pltpu.PrefetchScalarGridSpec(
        num_scalar_prefetch=1, grid=(nb,),
        in_specs=[blk2(d), blk2(LANES), blk2(LANES), pl.BlockSpec((1, d), lambda i, st: (0, 0)),
                  pl.BlockSpec(memory_space=pl.ANY)],
        out_specs=blk2(d),
        scratch_shapes=[pltpu.VMEM((2, ne, MOE_WIN, d), F32), pltpu.VMEM((MOE_WIN, d), F32), pltpu.VMEM((t, d), F32),
                        pltpu.SemaphoreType.DMA((2, ne)), pltpu.SemaphoreType.DMA((1,))])
    return pl.pallas_call(
        functools.partial(_moe_combine_kernel, t=t, ne=ne, norm=final_g is not None), grid_spec=grid_spec,
        out_shape=jax.ShapeDtypeStruct((m, d), F32),
        compiler_params=_cparams(("arbitrary",), VMEM_LIMIT), name="moe_combine",
    )(stab[:nb + 1, :ne].reshape(-1), x, comb, pos, gain, ys)


def _inproj_odd_kernel(x_ref, g_ref, w_ref, bias_ref, q_ref, k_ref, v_ref, og_ref, gc_ref, gr_ref, *, hq, hv, nh):
    xn = _rms(x_ref[...], g_ref[...]).astype(BF16)

    def mm(lo, hi):
        return _dot(xn, w_ref[:, lo:hi])

    dk = hq // nh
    q_ref[...] = mm(0, hq).astype(BF16)
    k_ref[...] = (mm(hq, 2 * hq) * (dk ** -0.5)).astype(BF16)
    v_ref[...] = mm(2 * hq, 2 * hq + hv).astype(BF16)
    og_ref[...] = jax.nn.sigmoid(mm(2 * hq + hv, 2 * hq + 2 * hv))
    gi = mm(2 * hq + 2 * hv, 2 * hq + 2 * hv + LANES) + bias_ref[...]
    lane = lax.broadcasted_iota(jnp.int32, gi.shape, 1)
    gates = jnp.where(lane < nh, gi, jax.nn.log_sigmoid(gi))
    gc_ref[...] = gates
    gr_ref[...] = gates.T[0:SUBLANES, :]


def inproj_odd(x, g, w_pad, gate_bias):
    m, d = x.shape
    nh = MLSTM_HEADS
    hq = hv = d
    tm = _row_tile(m, 256)
    row = lambda n: pl.BlockSpec((tm, n), lambda i: (i, 0))
    outs = [(hq, BF16), (hq, BF16), (hv, BF16), (hv, F32), (LANES, F32)]
    if tm % LANES:
        gr_spec = _full((SUBLANES, m))
    else:
        gr_spec = pl.BlockSpec((SUBLANES, tm), lambda i: (0, i))
    return pl.pallas_call(
        functools.partial(_inproj_odd_kernel, hq=hq, hv=hv, nh=nh), grid=(m // tm,),
        in_specs=[row(d), _full((1, d)), _full(w_pad.shape), _full((1, LANES))],
        out_specs=[row(n) for n, _ in outs] + [gr_spec],
        out_shape=[jax.ShapeDtypeStruct((m, n), t) for n, t in outs] + [jax.ShapeDtypeStruct((SUBLANES, m), F32)],
        compiler_params=_cparams(("parallel",), VMEM_LIMIT), name="inproj_odd",
    )(x, g.reshape(1, d), w_pad, gate_bias)


def prep_odd(w_in, b_i, b_f):
    d, n = w_in.shape
    n_pad = _round_up(n - 2 * MLSTM_HEADS, LANES) + LANES
    w_pad = jnp.zeros((d, n_pad), BF16).at[:, :n].set(w_in.astype(BF16))
    bias = jnp.zeros((1, LANES), F32).at[0, :2 * MLSTM_HEADS].set(jnp.concatenate([b_i, b_f]).astype(F32))
    return dict(w_in=w_pad, bias=bias)


def _mlstm_prompt_kernel(q_ref, k_ref, v_ref, og_ref, gc_ref, gr_ref, gain_ref, hn_ref, c_ref, n_ref, m_ref,
                         *, nh, dk, dv, ln):
    ci = pl.program_id(1)

    @pl.when(ci == 0)
    def _():
        c_ref[...] = jnp.zeros_like(c_ref)
        n_ref[...] = jnp.zeros_like(n_ref)
        m_ref[...] = jnp.full(m_ref.shape, NEG, F32)

    row = lax.broadcasted_iota(jnp.int32, (ln, ln), 0)
    col = lax.broadcasted_iota(jnp.int32, (ln, ln), 1)
    tri = row >= col
    gc = gc_ref[...]
    gr = gr_ref[...]
    for h in range(nh):
        q = q_ref[:, h * dk:(h + 1) * dk]
        k = k_ref[:, h * dk:(h + 1) * dk]
        v = v_ref[:, h * dv:(h + 1) * dv]
        ig_c, lf_c = gc[:, h:h + 1], gc[:, nh + h:nh + h + 1]
        ig_r, lf_r = gr[h:h + 1, :], gr[nh + h:nh + h + 1, :]
        b_c = jnp.sum(jnp.where(tri, lf_r, 0.0), axis=1, keepdims=True)
        b_r = jnp.sum(jnp.where(row <= col, lf_c, 0.0), axis=0, keepdims=True)
        m_prev = m_ref[0, h:h + 1, 0:1]
        c_prev = c_ref[0, h]
        n_prev = n_ref[0, h:h + 1, :]
        dmat = jnp.where(tri, b_c - b_r + ig_r, NEG)
        inter = b_c + m_prev
        mt = jnp.maximum(inter, jnp.max(dmat, -1, keepdims=True))
        wm = jnp.exp(dmat - mt)
        a = jnp.exp(inter - mt)
        wqk = wm * _dot_nt(q, k)
        num = a * _dot_nt(q, c_prev.astype(BF16)) + _dot(wqk.astype(BF16), v)
        den = a * jnp.sum(q.astype(F32) * n_prev, -1, keepdims=True) + jnp.sum(wqk, -1, keepdims=True)
        hh = num / jnp.maximum(jnp.abs(den), jnp.exp(-mt))
        b_end = b_c[ln - 1:ln, :]
        m_new = mt[ln - 1:ln, :]
        a_end = jnp.exp(b_end + m_prev - m_new)
        w_s = jnp.exp(b_end - b_c + ig_c - m_new)
        c_ref[0, h] = a_end * c_prev + _dot_tn((v.astype(F32) * w_s).astype(BF16), k)
        n_ref[0, h:h + 1, :] = a_end * n_prev + jnp.sum(w_s * k.astype(F32), axis=0, keepdims=True)
        m_ref[0, h:h + 1, :] = jnp.broadcast_to(m_new, (1, m_ref.shape[2]))
        hn = hh * lax.rsqrt(jnp.mean(hh * hh, -1, keepdims=True) + RMS_EPS)
        hn = hn * gain_ref[:, h * dv:(h + 1) * dv] * og_ref[:, h * dv:(h + 1) * dv]
        hn_ref[:, h * dv:(h + 1) * dv] = hn.astype(hn_ref.dtype)


def mlstm_prompt(q, k, v, og, gc, gr, gain, b, s):
    m, d = q.shape
    nh = MLSTM_HEADS
    dk = dv = d // nh
    ln = _row_tile(s, MLSTM_CHUNK)
    nch = s // ln
    row = lambda n: pl.BlockSpec((ln, n), lambda bi, ci: (bi * nch + ci, 0))
    return pl.pallas_call(
        functools.partial(_mlstm_prompt_kernel, nh=nh, dk=dk, dv=dv, ln=ln), grid=(b, nch),
        in_specs=[row(d), row(d), row(d), row(d), row(LANES),
                  pl.BlockSpec((SUBLANES, ln), lambda bi, ci: (0, bi * nch + ci)), _full((1, d))],
        out_specs=[row(d), pl.BlockSpec((1, nh, dv, dk), lambda bi, ci: (bi, 0, 0, 0)),
                   pl.BlockSpec((1, nh, dk), lambda bi, ci: (bi, 0, 0)),
                   pl.BlockSpec((1, nh, LANES), lambda bi, ci: (bi, 0, 0))],
        out_shape=[jax.ShapeDtypeStruct((m, d), BF16), jax.ShapeDtypeStruct((b, nh, dv, dk), F32),
                   jax.ShapeDtypeStruct((b, nh, dk), F32), jax.ShapeDtypeStruct((b, nh, LANES), F32)],
        compiler_params=_cparams(("parallel", "arbitrary"), VMEM_LIMIT), name="mlstm_prompt",
    )(q, k, v, og, gc, gr, gain.reshape(1, d))


def _mlstm_sample_kernel(q_ref, k_ref, v_ref, og_ref, g_ref, gain_ref, c_ref, n_ref, m_ref,
                         hn_ref, co_ref, no_ref, mo_ref, *, nh):
    row = lax.broadcasted_iota(jnp.int32, (SUBLANES, 1), 0)
    for h in range(nh):
        q = q_ref[0, h:h + 1, :]
        k = k_ref[0, h:h + 1, :]
        v = v_ref[0, h:h + 1, :].astype(F32)
        ig = g_ref[0, h:h + 1, 0:1]
        lf = g_ref[0, h:h + 1, 1:2]
        m_prev = m_ref[0, h:h + 1, :]
        c_prev = c_ref[0, h]
        n_prev = n_ref[0, h:h + 1, :]
        inter = lf + m_prev
        mt = jnp.maximum(inter, ig)
        wm = jnp.exp(ig - mt)
        a = jnp.exp(inter - mt)
        q8 = jnp.broadcast_to(q, (SUBLANES, q.shape[1]))
        cq = _dot_nt(q8, c_prev.astype(BF16))[0:1, :]
        wqk = wm * jnp.sum(q.astype(F32) * k.astype(F32), -1, keepdims=True)
        num = a * cq + wqk * v
        den = a * jnp.sum(n_prev * q.astype(F32), -1, keepdims=True) + wqk
        hh = num / jnp.maximum(jnp.abs(den), jnp.exp(-mt))
        v8 = jnp.where(row == 0, jnp.broadcast_to(v * wm, (SUBLANES, v.shape[1])), 0.0).astype(BF16)
        k8 = jnp.broadcast_to(k, (SUBLANES, k.shape[1]))
        co_ref[0, h] = a * c_prev + _dot_tn(v8, k8)
        no_ref[0, h:h + 1, :] = a * n_prev + wm * k.astype(F32)
        mo_ref[0, h:h + 1, :] = mt
        hn = hh * lax.rsqrt(jnp.mean(hh * hh, -1, keepdims=True) + RMS_EPS)
        hn_ref[0, h:h + 1, :] = (hn * gain_ref[h:h + 1, :] * og_ref[0, h:h + 1, :]).astype(hn_ref.dtype)


def mlstm_sample(q, k, v, og, gc, gain, c, n, m):
    bd, d = q.shape
    nh = MLSTM_HEADS
    dk = d // nh
    heads = lambda a: a.reshape(bd, nh, dk)
    g2 = jnp.transpose(gc[:, :2 * nh].reshape(bd, 2, nh), (0, 2, 1))
    blk3 = lambda n2: pl.BlockSpec((1, nh, n2), lambda bi: (bi, 0, 0))
    cspec = pl.BlockSpec((1, nh, dk, dk), lambda bi: (bi, 0, 0, 0))
    hn, co, no, mo = pl.pallas_call(
        functools.partial(_mlstm_sample_kernel, nh=nh), grid=(bd,),
        in_specs=[blk3(dk), blk3(dk), blk3(dk), blk3(dk), blk3(2), _full((nh, dk)), cspec, blk3(dk), blk3(1)],
        out_specs=[blk3(dk), cspec, blk3(dk), blk3(1)],
        out_shape=[jax.ShapeDtypeStruct((bd, nh, dk), BF16), jax.ShapeDtypeStruct(c.shape, F32),
                   jax.ShapeDtypeStruct(n.shape, F32), jax.ShapeDtypeStruct((bd, nh, 1), F32)],
        compiler_params=_cparams(("parallel",)), name="mlstm_sample",
    )(heads(q), heads(k), heads(v), heads(og), g2, gain.reshape(nh, dk), c, n, m.reshape(bd, nh, 1))
    return hn.reshape(bd, d), co, no, mo.reshape(bd, nh)


def _row_to_col(row):
    n = row.shape[1]
    eye = lax.broadcasted_iota(jnp.int32, (n, n), 0) == lax.broadcasted_iota(jnp.int32, (n, n), 1)
    return jnp.sum(jnp.where(eye, row, 0.0), axis=1, keepdims=True)


def _head_pad(q, keep):
    q2 = jnp.concatenate([q] * NSA_KV_HEADS, axis=1)
    row = lax.broadcasted_iota(jnp.int32, q2.shape, 0)
    lane = lax.broadcasted_iota(jnp.int32, q2.shape, 1)
    return jnp.where((lane // NSA_HD == row // NSA_GROUP) & keep(row), q2, jnp.zeros_like(q2))


def _nsa_sample_cmp_kernel(pt_ref, q_ref, pages_ref, pe_ref, w_ref, biasc_ref, oc_ref, idx_ref, xbuf, xrow, sem,
                           *, n_pages, nc, ncp, nsb, n_sel, past):
    b = pl.program_id(0)
    nb = pl.num_programs(0)
    kvd = NSA_KV_HEADS * NSA_HD

    def page_copy(bb, slot, p, sl):
        return pltpu.make_async_copy(pages_ref.at[pt_ref[bb * n_pages + p], pl.ds(sl * kvd, kvd), :],
                                     xbuf.at[slot, sl, p], sem.at[slot])

    def start_all(bb, slot):
        def body(p, c):
            page_copy(bb, slot, p, 0).start()
            page_copy(bb, slot, p, 1).start()
            return c
        lax.fori_loop(0, n_pages, body, 0)

    @pl.when(b == 0)
    def _():
        start_all(0, 0)

    @pl.when(b + 1 < nb)
    def _():
        start_all(b + 1, (b + 1) % 2)

    slot = b % 2

    def wait_body(p, c):
        page_copy(b, slot, p, 0).wait()
        page_copy(b, slot, p, 1).wait()
        return c
    lax.fori_loop(0, n_pages, wait_body, 0)

    def file_page(p, c):
        for sl in range(2):
            _file_rows(xbuf[slot, sl, p], xrow, sl, p)
        return c
    lax.fori_loop(0, n_pages, file_page, 0, unroll=8)

    acc = _compress_filed(xrow, pe_ref, w_ref, nc)
    kc = acc[:, 0:kvd].astype(BF16)
    vc = acc[:, kvd:2 * kvd].astype(BF16)
    q = q_ref[0]
    nh = q.shape[0]
    qpad = _head_pad(q, lambda r: r >= 0)
    s = _dot_nt(qpad, kc)
    s = s + biasc_ref[:, 0:nc]
    e = jnp.exp(s - jnp.max(s, -1, keepdims=True))
    p_c = e / jnp.sum(e, -1, keepdims=True)
    o = _dot(p_c.astype(BF16), vc)
    row = lax.broadcasted_iota(jnp.int32, (nh, NSA_HD), 0)
    o_h = o[:, 0:NSA_HD]
    for k in range(1, NSA_KV_HEADS):
        o_h = jnp.where(row // NSA_GROUP == k, o[:, k * NSA_HD:(k + 1) * NSA_HD], o_h)
    oc_ref[0] = o_h
    prow = lax.broadcasted_iota(jnp.int32, p_c.shape, 0)
    lane = lax.broadcasted_iota(jnp.int32, (1, ncp), 1)
    blk = lane // 2
    cur = past // SEL_BLOCK
    forced = (blk == 0) | (blk == cur) | (blk == cur - 1)
    is_cand = ((lane % 2) == 0) & (lane < 2 * nsb)
    nselp = idx_ref.shape[1]
    rsel = lax.broadcasted_iota(jnp.int32, (nselp, 1), 0).astype(F32)
    out_lane = lax.broadcasted_iota(jnp.int32, (nselp, LANES), 1)
    result = jnp.full((nselp, LANES), -1, jnp.int32)
    for k in range(NSA_KV_HEADS):
        imp = jnp.sum(jnp.where(prow // NSA_GROUP == k, p_c, 0.0), axis=0, keepdims=True)
        imp = jnp.concatenate([imp, jnp.zeros((1, ncp - nc), F32)], axis=1)
        imp = _pair_sum(imp)
        score = jnp.where(forced, FORCE_SCORE, jnp.where(blk <= cur, imp, -1.0))
        score = jnp.where(is_cand, score, -2.0)
        sel, rank = _select_blocks(score, n_sel, nsb)
        hit = (rank == rsel) & (sel > 0.5)
        idx = jnp.sum(jnp.where(hit, (blk + 1).astype(F32), 0.0), axis=1, keepdims=True) - 1.0
        result = jnp.where(out_lane == k, idx.astype(jnp.int32), result)
    idx_ref[0] = result


def nsa_sample_cmp(q8, pages, page_table, pe2, wbd, rel_bias):
    bd, nh, hd = q8.shape
    n_pages = page_table.shape[1]
    past = n_pages * PAGE_SIZE
    nc = past // CMP_BLOCK
    nsb = -(-(past + 1) // SEL_BLOCK)
    n_sel = min(TOP_N, nsb)
    ncp = _round_up(max(nc, SEL_RATIO * nsb), LANES)
    nselp = _round_up(n_sel, SUBLANES)
    kvd = wbd.shape[-1]
    cend = jnp.arange(nc) * CMP_BLOCK + (CMP_BLOCK - 1)
    biasc = jnp.zeros((nh, ncp), F32).at[:, :nc].set(_bias_of(rel_bias, past - cend).T)
    grid_spec = pltpu.PrefetchScalarGridSpec(
        num_scalar_prefetch=1, grid=(bd,),
        in_specs=[pl.BlockSpec((1, nh, hd), lambda bi, pt: (bi, 0, 0)),
                  pl.BlockSpec(memory_space=pl.ANY),
                  pl.BlockSpec(pe2.shape, lambda bi, pt: (0, 0, 0)),
                  pl.BlockSpec(wbd.shape, lambda bi, pt: (0, 0, 0, 0)),
                  pl.BlockSpec((nh, ncp), lambda bi, pt: (0, 0))],
        out_specs=[pl.BlockSpec((1, nh, hd), lambda bi, pt: (bi, 0, 0)),
                   pl.BlockSpec((1, nselp, LANES), lambda bi, pt: (bi, 0, 0))],
        scratch_shapes=[pltpu.VMEM((2, 2, n_pages, kvd, PAGE_SIZE), F32),
                        pltpu.VMEM((2, CMP_BLOCK // SUBLANES, nc * SUBLANES, kvd), F32),
                        pltpu.SemaphoreType.DMA((2,))])
    oc, idx = pl.pallas_call(
        functools.partial(_nsa_sample_cmp_kernel, n_pages=n_pages, nc=nc, ncp=ncp, nsb=nsb, n_sel=n_sel, past=past),
        grid_spec=grid_spec,
        out_shape=[jax.ShapeDtypeStruct((bd, nh, hd), F32), jax.ShapeDtypeStruct((bd, nselp, LANES), jnp.int32)],
        compiler_params=_cparams(("arbitrary",), VMEM_LIMIT), name="nsa_sample_cmp",
    )(page_table.reshape(-1), q8, pages, pe2, wbd, biasc)
    sel_idx = jnp.transpose(idx[:, :n_sel, :NSA_KV_HEADS], (0, 2, 1))
    return oc, sel_idx


def _nsa_sample_att_kernel(pt_ref, si_ref, q_ref, g_ref, oc_ref, kvn_ref, wn_ref, wc_ref, pages_ref,
                           bsel_ref, bwin_ref, ob_ref, win_ref, selbuf, wall, sem,
                           *, n_pages, n_sel, past, wb):
    b = pl.program_id(0)
    nb = pl.num_programs(0)
    kvd = NSA_KV_HEADS * NSA_HD
    hd = NSA_HD
    n_blk_pages = past // SEL_BLOCK
    per_page = PAGE_SIZE // SEL_BLOCK
    n_slots = NSA_KV_HEADS * n_sel

    def blk_of(bb, j):
        return si_ref[bb * n_slots + j]

    def blk_copy(bb, slot, j):
        blk = jnp.clip(blk_of(bb, j), 0, n_blk_pages - 1)
        page = pt_ref[bb * n_pages + blk // per_page]
        return pltpu.make_async_copy(pages_ref.at[page, pl.ds(2 * kvd, 2 * kvd), :], selbuf.at[slot, j], sem.at[slot])

    def in_pages(bb, j):
        blk = blk_of(bb, j)
        return (blk >= 0) & (blk < n_blk_pages)

    def start_all(bb, slot):
        def body(j, c):
            @pl.when(in_pages(bb, j))
            def _():
                blk_copy(bb, slot, j).start()
            return c
        lax.fori_loop(0, n_slots, body, 0)

    @pl.when(b == 0)
    def _():
        start_all(0, 0)

    @pl.when(b + 1 < nb)
    def _():
        start_all(b + 1, (b + 1) % 2)

    slot = b % 2
    new_sel = _row_to_col(kvn_ref[0][:, 2 * kvd:4 * kvd])
    lane = lax.broadcasted_iota(jnp.int32, (1, PAGE_SIZE), 1)

    def wait_body(j, c):
        @pl.when(in_pages(b, j))
        def _():
            blk_copy(b, slot, j).wait()

        @pl.when(jnp.logical_not(in_pages(b, j)))
        def _():
            is_new = blk_of(b, j) == n_blk_pages
            selbuf[slot, j] = jnp.where((lane == 0) & is_new, new_sel, 0.0)
        return c
    lax.fori_loop(0, n_slots, wait_body, 0)

    q = q_ref[0]
    nh = q.shape[0]
    gates = g_ref[0]
    head = lax.broadcasted_iota(jnp.int32, (nh, 1), 0)

    def attend(qp, keys_t, vals_t, bias, valid):
        s = jnp.where(valid, _dot(qp, keys_t) + bias, NEG)
        e = jnp.where(valid, jnp.exp(s - jnp.max(s, -1, keepdims=True)), 0.0)
        den = jnp.sum(e, -1, keepdims=True)
        p = e / jnp.where(den > 0, den, 1.0)
        return _dot_nt(p.astype(BF16), vals_t)

    o_s = jnp.zeros((nh, kvd), F32)
    for k in range(NSA_KV_HEADS):
        blks = [blk_of(b, k * n_sel + r) for r in range(n_sel)]
        tiles = [selbuf[slot, k * n_sel + r] for r in range(n_sel)]
        keys_t = jnp.concatenate([t_[0:kvd, :] for t_ in tiles], axis=1).astype(BF16)
        vals_t = jnp.concatenate([t_[kvd:2 * kvd, :] for t_ in tiles], axis=1).astype(BF16)
        bias = jnp.concatenate([bsel_ref[jnp.clip(bl // per_page, 0, n_pages)] for bl in blks], axis=1)
        valid = jnp.concatenate(
            [(lane // SEL_BLOCK == bl % per_page) & ((bl // per_page) * PAGE_SIZE + lane <= past) & (bl >= 0)
             for bl in blks], axis=1)
        o_k = attend(_head_pad(q, lambda r: r // NSA_GROUP == k), keys_t, vals_t, bias, valid)
        o_s = jnp.where(head // NSA_GROUP == k, o_k, o_s)
    wlanes = wall.shape[1]
    wall[:, 0:wb] = wc_ref[0]
    tail = lax.broadcasted_iota(jnp.int32, (1, wlanes - wb), 1)
    wall[:, wb:wlanes] = jnp.where(tail == 0, _row_to_col(wn_ref[0]), 0.0)
    win_ref[0] = pltpu.roll(wall[...], wlanes - 1, 1)[:, 0:wb]
    w_pos = lax.broadcasted_iota(jnp.int32, (1, wlanes), 1)
    valid_w = (w_pos <= wb) & (wb - w_pos < WINDOW) & (past - wb + w_pos >= 0)
    o_w = attend(_head_pad(q, lambda r: r >= 0), wall[0:kvd, :].astype(BF16), wall[kvd:2 * kvd, :].astype(BF16),
                 bwin_ref[...], valid_w)
    o_c = jnp.concatenate([oc_ref[0]] * NSA_KV_HEADS, axis=1)
    mix = gates[:, 0:1] * o_c + gates[:, 1:2] * o_s + gates[:, 2:3] * o_w
    out = mix[:, 0:hd]
    for k in range(1, NSA_KV_HEADS):
        out = jnp.where(head // NSA_GROUP == k, mix[:, k * hd:(k + 1) * hd], out)
    ob_ref[0] = out.astype(ob_ref.dtype)


def nsa_sample_att(q8, gates, oc, kv03, kv45, wcache_t, layer, pages_t, page_table, sel_idx, rel_bias):
    bd, nh, hd = q8.shape
    n_pages = page_table.shape[1]
    past = n_pages * PAGE_SIZE
    wb = wcache_t.shape[2]
    kvd = NSA_KV_HEADS * hd
    n_sel = sel_idx.shape[2]
    wlanes = _round_up(wb + 1, LANES)
    g3 = gates[:, :3 * nh].reshape(bd, nh, 3)
    dist = past - (jnp.arange(n_pages + 1)[:, None] * PAGE_SIZE + jnp.arange(PAGE_SIZE)[None, :])
    bsel = jnp.transpose(_bias_of(rel_bias, dist), (0, 2, 1))
    bwin = _bias_of(rel_bias, wb - jnp.arange(wlanes)).T
    blk = lambda n2, n3: pl.BlockSpec((1, n2, n3), lambda bi, pt, si: (bi, 0, 0))
    grid_spec = pltpu.PrefetchScalarGridSpec(
        num_scalar_prefetch=2, grid=(bd,),
        in_specs=[blk(nh, hd), blk(nh, 3), blk(nh, hd), blk(1, 4 * kvd), blk(1, 2 * kvd),
                  pl.BlockSpec((1, 2 * kvd, wb), lambda bi, pt, si: (layer * bd + bi, 0, 0)),
                  pl.BlockSpec(memory_space=pl.ANY),
                  pl.BlockSpec(bsel.shape, lambda bi, pt, si: (0, 0, 0)),
                  pl.BlockSpec(bwin.shape, lambda bi, pt, si: (0, 0))],
        out_specs=[blk(nh, hd), blk(2 * kvd, wb)],
        scratch_shapes=[pltpu.VMEM((2, NSA_KV_HEADS * n_sel, 2 * kvd, PAGE_SIZE), F32),
                        pltpu.VMEM((2 * kvd, wlanes), F32), pltpu.SemaphoreType.DMA((2,))])
    ob, win = pl.pallas_call(
        functools.partial(_nsa_sample_att_kernel, n_pages=n_pages, n_sel=n_sel, past=past, wb=wb),
        grid_spec=grid_spec,
        out_shape=[jax.ShapeDtypeStruct((bd, nh, hd), BF16), jax.ShapeDtypeStruct((bd, 2 * kvd, wb), F32)],
        compiler_params=_cparams(("arbitrary",), VMEM_LIMIT), name="nsa_sample_att",
    )(page_table.reshape(-1), sel_idx.reshape(-1), q8, g3, oc, kv03.reshape(bd, 1, -1), kv45.reshape(bd, 1, -1),
      wcache_t, pages_t, bsel, bwin)
    return ob.reshape(bd, nh * hd), win


def kernel(x_prompt, x_sample, mem_prompt, cache_conv, cache_nsa_pages, cache_nsa_window, state_mlstm_c,
           state_mlstm_n, state_mlstm_m, cache_mem_kv, page_table, rel_bias, norm_mix, norm_xattn, norm_mem,
           norm_ffn, norm_final, w_in_even, w_out_even, conv_w, conv_b, conv_ln_g, conv_ln_b, nsa_cmp_pe,
           nsa_cmp_w, w_in_odd, mlstm_b_i, mlstm_b_f, mlstm_norm, w_out_odd, xattn_wq, xattn_wkv, xattn_wo,
           ffn_w_gu, ffn_w_dn, router_w, router_b, expert_w_gu, expert_w_dn):
    b, s, d = x_prompt.shape
    bd, td, _ = x_sample.shape
    assert td == 1, "the sample group decodes one token per sequence"
    depth = norm_mix.shape[0]
    mt = mem_prompt.shape[1]
    cc = conv_w.shape[2]
    hist = conv_w.shape[1] - 1
    wb = cache_nsa_window.shape[2]
    kvh, hd = NSA_KV_HEADS, NSA_HD
    n_pool = cache_nsa_pages.shape[1]
    assert s >= hist and s >= wb and s % Q_BLOCK == 0
    xp = x_prompt.reshape(b * s, d)
    xs = x_sample.reshape(bd, d)
    mem = mem_prompt.reshape(b * mt, d)
    pages_t = jnp.swapaxes(cache_nsa_pages.reshape(-1, PAGE_SIZE, 4 * kvh * hd), 1, 2)
    window_t = jnp.swapaxes(cache_nsa_window.reshape(-1, wb, 2 * kvh * hd), 1, 2)
    xhd = d // X_HEADS
    memkv_rows = jnp.swapaxes(cache_mem_kv.reshape(depth * bd, mt, 2, X_HEADS, xhd // LANES, LANES), 3, 4)
    memkv_rows = memkv_rows.reshape(depth * bd, -1, LANES)
    bf = lambda a: a.astype(BF16)
    conv_p, conv_s, nsa_p, nsa_s, win_p, win_s = [], [], [], [], [], []
    mc_p, mc_s, mn_p, mn_s, mm_p, mm_s, memkv_p = [], [], [], [], [], [], []
    for l in range(depth):
        li = l // 2
        if l % 2 == 0:
            prm = prep_even(w_in_even[li], nsa_cmp_pe[li], nsa_cmp_w[li])
            w_out = bf(w_out_even[li])
            w_parts = [w_out[:cc], w_out[cc:]]
            conv_args = (conv_w[li], conv_b[li], conv_ln_g[li], conv_ln_b[li])
            glu, keys, qt, vt, gt, kv_t = inproj_even(xp, norm_mix[l], prm['w_in'], cc, s)
            a_out = conv_prompt(glu, *conv_args, b, s)
            kc, vct = compress_prompt(kv_t, prm['pe2'], prm['wbd'], _round_up(s // CMP_BLOCK, LANES))
            b_out = nsa_prompt(qt, gt, kc, vct, keys, vt, rel_bias, b, s)
            mix_p = ([a_out, b_out], w_parts)
            conv_p.append(glu.reshape(b, s, cc)[:, s - hist:])
            rows_t = kv_t.reshape(b, 6, kvh, hd, s)
            nsa_p.append(jnp.transpose(rows_t[:, :4], (0, 4, 1, 2, 3)))
            win_p.append(jnp.transpose(rows_t[:, 4:, :, :, s - wb:], (0, 4, 1, 2, 3)))
            glu, kv03, kv45, q, gates = inproj_even(xs, norm_mix[l], prm['w_in'], cc)
            a_out, conv_state = conv_sample(cache_conv[li], glu, *conv_args)
            q8 = q.reshape(bd, NSA_HEADS, hd)
            pt = page_table + li * n_pool
            o_c, sel_idx = nsa_sample_cmp(q8, pages_t, pt, prm['pe2'], prm['wbd'], rel_bias)
            b_out, win = nsa_sample_att(q8, gates, o_c, kv03, kv45, window_t, li, pages_t, pt, sel_idx, rel_bias)
            xs = outproj(xs, [a_out, b_out], w_parts)
            conv_s.append(conv_state)
            nsa_s.append(kv03.reshape(bd, 1, 4, kvh, hd))
            win_s.append(jnp.transpose(win.reshape(bd, 2, kvh, hd, wb), (0, 4, 1, 2, 3)))
        else:
            prm = prep_odd(w_in_odd[li], mlstm_b_i[li], mlstm_b_f[li])
            w_out = bf(w_out_odd[li])
            q, k, v, og, gc, gr = inproj_odd(xp, norm_mix[l], prm['w_in'], prm['bias'])
            hn, c_new, n_new, m_new = mlstm_prompt(q, k, v, og, gc, gr, mlstm_norm[li], b, s)
            mix_p = ([hn], [w_out])
            mc_p.append(c_new)
            mn_p.append(n_new)
            mm_p.append(m_new[:, :, 0])
            q, k, v, og, gc, gr = inproj_odd(xs, norm_mix[l], prm['w_in'], prm['bias'])
            hn, c_new, n_new, m_new = mlstm_sample(q, k, v, og, gc, mlstm_norm[li], state_mlstm_c[li],
                                                   state_mlstm_n[li], state_mlstm_m[li])
            xs = outproj(xs, [hn], [w_out])
            mc_s.append(c_new)
            mn_s.append(n_new)
            mm_s.append(m_new)
        wq, wo = bf(xattn_wq[l]), bf(xattn_wo[l])
        mkv_rows, mkv_b = memkv(mem, norm_mem[l], bf(xattn_wkv[l]), X_HEADS)
        mkv = jnp.swapaxes(mkv_rows.reshape(b, mt, 2, xhd // LANES, X_HEADS, LANES), 3, 4)
        memkv_p.append(mkv.reshape(b, mt, 2, X_HEADS, xhd))
        xs = xattn_sample(xs, norm_xattn[l], wq, memkv_rows, wo, l, mt)
        if l % 2 == 0:
            w_gu, w_dn = bf(ffn_w_gu[li]), bf(ffn_w_dn[li])
            xp = xattn_ffn(xp, norm_xattn[l], wq, mkv_b, wo, *mix_p, norm_ffn[l], w_gu, w_dn, b, s)
            xs = ffn(xs, norm_ffn[l], w_gu, w_dn)
        else:
            xp = xattn_prompt(xp, norm_xattn[l], wq, mkv_b, wo, b, s, *mix_p)
            e_gu, e_dn = bf(expert_w_gu[li]), bf(expert_w_dn[li])
            final_g = norm_final if l == depth - 1 else None
            comb, h, mask, counts = router(xp, norm_ffn[l], router_w[li], router_b[li])
            xp = moe_grouped(xp, h, comb, mask, counts, e_gu, e_dn, final_g)
            comb, h, _, _ = router(xs, norm_ffn[l], router_w[li], router_b[li])
            xs = moe(xs, h, comb, e_gu, e_dn, final_g)
    if depth % 2:
        xp, xs = rmsnorm(xp, norm_final), rmsnorm(xs, norm_final)
    y_prompt = xp.reshape(b, s, d)
    y_sample = xs.reshape(bd, 1, d)
    return (y_prompt, y_sample, jnp.stack(conv_p), jnp.stack(conv_s), jnp.stack(nsa_p), jnp.stack(nsa_s),
            jnp.stack(win_p), jnp.stack(win_s), jnp.stack(mc_p), jnp.stack(mc_s), jnp.stack(mn_p),
            jnp.stack(mn_s), jnp.stack(mm_p), jnp.stack(mm_s), jnp.stack(memkv_p))
```

```python
import functools
import math

import jax
import jax.numpy as jnp
from jax import lax
from jax.experimental import pallas as pl
from jax.experimental.pallas import tpu as pltpu

F32 = jnp.float32
BF16 = jnp.bfloat16

PAGE_SIZE = 128
CONV_WIDTH = 31
NSA_HEADS = 8
NSA_KV_HEADS = 2
NSA_GROUP = NSA_HEADS // NSA_KV_HEADS
NSA_HD = 64
CMP_BLOCK = 32
SEL_BLOCK = 64
SEL_RATIO = SEL_BLOCK // CMP_BLOCK
TOP_N = 16
WINDOW = 512
Q_BLOCK = 128
FORCE_SCORE = 1.0e4
NUM_BUCKETS = 32
MAX_DISTANCE = 1024
MLSTM_HEADS = 4
X_HEADS = 4
TOP_K = 2
RMS_EPS = 1e-6
LN_EPS = 1e-5
NEG = -1e30

LANES = 128
SUBLANES = 8
VMEM_LIMIT = 56 * 1024 * 1024
MLSTM_CHUNK = 256
SEL_TILES = 4
WIN_TILES = 5


def _cparams(sem, vmem=None):
    return pltpu.CompilerParams(dimension_semantics=sem, vmem_limit_bytes=vmem)


def _rms(x, g):
    return x * lax.rsqrt(jnp.mean(x * x, -1, keepdims=True) + RMS_EPS) * g


def _dot(a, b):
    return jnp.dot(a, b, preferred_element_type=F32)


def _dot_nt(a, b):
    return lax.dot_general(a, b, (((1,), (1,)), ((), ())), preferred_element_type=F32)


def _dot_tn(a, b):
    return lax.dot_general(a, b, (((0,), (0,)), ((), ())), preferred_element_type=F32)


def _full(shape):
    n = len(shape)
    return pl.BlockSpec(shape, lambda *_: (0,) * n)


def _row_tile(m, pref):
    t = min(pref, m)
    while m % t:
        t //= 2
    return t


def _rmsnorm_kernel(x_ref, g_ref, o_ref):
    o_ref[...] = _rms(x_ref[...], g_ref[...])


def rmsnorm(x, g):
    m, d = x.shape
    tm = _row_tile(m, 1024)
    return pl.pallas_call(
        _rmsnorm_kernel, grid=(m // tm,),
        in_specs=[pl.BlockSpec((tm, d), lambda i: (i, 0)), _full((1, d))],
        out_specs=pl.BlockSpec((tm, d), lambda i: (i, 0)),
        out_shape=jax.ShapeDtypeStruct((m, d), F32),
        compiler_params=_cparams(("parallel",)), name="rmsnorm",
    )(x, g.reshape(1, d))


def _outproj_kernel(*refs, n_in):
    x_ref = refs[0]
    a_refs = refs[1:1 + n_in]
    w_refs = refs[1 + n_in:1 + 2 * n_in]
    o_ref = refs[1 + 2 * n_in]
    acc = x_ref[...]
    for a_ref, w_ref in zip(a_refs, w_refs):
        acc = acc + _dot(a_ref[...], w_ref[...])
    o_ref[...] = acc


def outproj(x, acts, ws):
    m, d = x.shape
    tm = _row_tile(m, 512)
    n_in = len(acts)
    in_specs = [pl.BlockSpec((tm, d), lambda i: (i, 0))]
    in_specs += [pl.BlockSpec((tm, a.shape[1]), lambda i: (i, 0)) for a in acts]
    in_specs += [_full(w.shape) for w in ws]
    return pl.pallas_call(
        functools.partial(_outproj_kernel, n_in=n_in), grid=(m // tm,),
        in_specs=in_specs, out_specs=pl.BlockSpec((tm, d), lambda i: (i, 0)),
        out_shape=jax.ShapeDtypeStruct((m, d), F32),
        compiler_params=_cparams(("parallel",)), name="outproj",
    )(x, *acts, *ws)


def _inproj_even_kernel(x_ref, g_ref, w_ref, glu_ref, *rest, cc, qd, kvd, tiles):
    xn = _rms(x_ref[...], g_ref[...]).astype(BF16)

    def mm(lo, hi):
        return _dot(xn, w_ref[:, lo:hi])

    o = 0
    a = mm(o, o + cc)
    b = mm(o + cc, o + 2 * cc)
    glu_ref[...] = a * jax.nn.sigmoid(b)
    o += 2 * cc
    q = mm(o, o + qd) * (NSA_HD ** -0.5)
    o += qd
    kv03 = mm(o, o + 4 * kvd)
    o += 4 * kvd
    kv45 = mm(o, o + 2 * kvd)
    o += 2 * kvd
    gates = jax.nn.sigmoid(mm(o, o + LANES))
    if tiles == 0:
        kv03_ref, kv45_ref, q_ref, gate_ref = rest
        kv03_ref[...] = kv03
        kv45_ref[...] = kv45
        q_ref[...] = q.astype(BF16)
        gate_ref[...] = gates
        return
    keys_ref, qt_ref, vt_ref, gt_ref, kvt_ref = rest
    kvt_ref[0] = jnp.concatenate([kv03, kv45], axis=1).T
    keys_ref[...] = jnp.concatenate([kv03[:, 2 * kvd:3 * kvd], kv45[:, 0:kvd]], axis=1).astype(BF16)
    vals = jnp.concatenate([kv03[:, 3 * kvd:4 * kvd], kv45[:, kvd:2 * kvd]], axis=1)
    for j in range(tiles):
        rows = slice(j * Q_BLOCK, (j + 1) * Q_BLOCK)
        qt_ref[j] = q[rows, :].T.astype(BF16)
        vt_ref[j] = vals[rows, :].T.astype(BF16)
        gt_ref[j] = gates[rows, :].T


def inproj_even(x, g, w_pad, cc, seq=None):
    m, d = x.shape
    qd = NSA_HEADS * NSA_HD
    kvd = NSA_KV_HEADS * NSA_HD
    tm = _row_tile(m, 256)
    row = lambda n: pl.BlockSpec((tm, n), lambda i: (i, 0))
    out_specs = [row(cc)]
    out_shape = [jax.ShapeDtypeStruct((m, cc), F32)]
    transposed = seq is not None
    tiles = tm // Q_BLOCK if transposed else 0
    if transposed:
        assert tm % Q_BLOCK == 0 and seq % tm == 0
        per_seq = seq // tm
        tile = lambda n: pl.BlockSpec((tiles, n, Q_BLOCK), lambda i: (i, 0, 0))
        out_specs += [row(2 * kvd), tile(qd), tile(2 * kvd), tile(LANES),
                      pl.BlockSpec((1, 6 * kvd, tm), lambda i: (i // per_seq, 0, i % per_seq))]
        out_shape += [jax.ShapeDtypeStruct((m, 2 * kvd), BF16),
                      jax.ShapeDtypeStruct((m // Q_BLOCK, qd, Q_BLOCK), BF16),
                      jax.ShapeDtypeStruct((m // Q_BLOCK, 2 * kvd, Q_BLOCK), BF16),
                      jax.ShapeDtypeStruct((m // Q_BLOCK, LANES, Q_BLOCK), F32),
                      jax.ShapeDtypeStruct((m // seq, 6 * kvd, seq), F32)]
    else:
        out_specs += [row(4 * kvd), row(2 * kvd), row(qd), row(LANES)]
        out_shape += [jax.ShapeDtypeStruct((m, 4 * kvd), F32), jax.ShapeDtypeStruct((m, 2 * kvd), F32),
                      jax.ShapeDtypeStruct((m, qd), BF16), jax.ShapeDtypeStruct((m, LANES), F32)]
    return pl.pallas_call(
        functools.partial(_inproj_even_kernel, cc=cc, qd=qd, kvd=kvd, tiles=tiles), grid=(m // tm,),
        in_specs=[row(d), _full((1, d)), _full(w_pad.shape)],
        out_specs=out_specs, out_shape=out_shape,
        compiler_params=_cparams(("parallel",)), name="inproj_even",
    )(x, g.reshape(1, d), w_pad)


def _conv_post(y, lg, lb):
    mu = jnp.mean(y, -1, keepdims=True)
    var = jnp.mean(jnp.square(y - mu), -1, keepdims=True)
    yn = (y - mu) * lax.rsqrt(var + LN_EPS) * lg + lb
    return yn * jax.nn.sigmoid(yn)


CONV_SUB = 64
CONV_PAD = 32


def _conv_prompt_kernel(glu_ref, cw_ref, cb_ref, lg_ref, lb_ref, o_ref, ext_ref, y_ref, *, ts, s):
    i = pl.program_id(1)
    c = glu_ref.shape[-1]

    @pl.when(i == 0)
    def _():
        ext_ref[0:CONV_PAD, :] = jnp.zeros((CONV_PAD, c), F32)
        ext_ref[CONV_PAD:CONV_PAD + s, :] = glu_ref[0]
        ext_ref[CONV_PAD + s:CONV_PAD + s + SUBLANES, :] = jnp.zeros((SUBLANES, c), F32)

    lead = CONV_PAD - (CONV_WIDTH - 1)
    span = CONV_SUB + CONV_PAD

    def sub(j, carry):
        r0 = pl.multiple_of(i * ts + j * CONV_SUB, CONV_SUB)
        for c0 in range(0, c, LANES):
            xw = ext_ref[pl.ds(r0, span + SUBLANES), c0:c0 + LANES]
            acc = jnp.zeros((CONV_SUB, LANES), F32) + cb_ref[:, c0:c0 + LANES]
            for r in range(SUBLANES):
                xr = xw if r == 0 else pltpu.roll(xw, span + SUBLANES - r, 0)
                for a in range(span // SUBLANES):
                    w = SUBLANES * a + r - lead
                    if 0 <= w < CONV_WIDTH:
                        acc = acc + xr[SUBLANES * a:SUBLANES * a + CONV_SUB, :] * cw_ref[w:w + 1, c0:c0 + LANES]
            y_ref[:, c0:c0 + LANES] = acc
        o_ref[0, pl.ds(pl.multiple_of(j * CONV_SUB, CONV_SUB), CONV_SUB), :] = _conv_post(
            y_ref[...], lg_ref[...], lb_ref[...]).astype(o_ref.dtype)
        return carry

    lax.fori_loop(0, ts // CONV_SUB, sub, 0)


def conv_prompt(glu, cw, cb, lg, lb, b, s):
    c = glu.shape[-1]
    ts = _row_tile(s, 256)
    vec = lambda a: a.reshape(1, c)
    out = pl.pallas_call(
        functools.partial(_conv_prompt_kernel, ts=ts, s=s), grid=(b, s // ts),
        in_specs=[pl.BlockSpec((1, s, c), lambda bi, i: (bi, 0, 0)), _full((CONV_WIDTH, c)),
                  _full((1, c)), _full((1, c)), _full((1, c))],
        out_specs=pl.BlockSpec((1, ts, c), lambda bi, i: (bi, i, 0)),
        out_shape=jax.ShapeDtypeStruct((b, s, c), BF16),
        scratch_shapes=[pltpu.VMEM((CONV_PAD + s + SUBLANES, c), F32), pltpu.VMEM((CONV_SUB, c), F32)],
        compiler_params=_cparams(("parallel", "arbitrary")), name="conv_prompt",
    )(glu.reshape(b, s, c), cw, vec(cb), vec(lg), vec(lb))
    return out.reshape(b * s, c)


def _conv_sample_kernel(cache_ref, glu_ref, cw_ref, cb_ref, lg_ref, lb_ref, o_ref, st_ref):
    hist = CONV_WIDTH - 1
    cache = cache_ref[...]
    glu = glu_ref[...]
    y = jnp.sum(cache * cw_ref[0:hist, :][None], axis=1) + glu * cw_ref[hist:hist + 1, :] + cb_ref[...]
    o_ref[...] = _conv_post(y, lg_ref[...], lb_ref[...]).astype(o_ref.dtype)
    st_ref[:, 0:hist - 1, :] = cache[:, 1:hist, :]
    st_ref[:, hist - 1:hist, :] = glu[:, None, :]


def conv_sample(cache, glu, cw, cb, lg, lb):
    bd, hist, c = cache.shape
    vec = lambda a: a.reshape(1, c)
    return pl.pallas_call(
        _conv_sample_kernel,
        out_shape=[jax.ShapeDtypeStruct((bd, c), BF16), jax.ShapeDtypeStruct((bd, hist, c), F32)],
        name="conv_sample",
    )(cache, glu, cw, vec(cb), vec(lg), vec(lb))


def _rel_bucket(dist):
    n = jnp.maximum(dist, 0)
    max_exact = NUM_BUCKETS // 2
    nf = jnp.maximum(n, 1).astype(F32)
    large = max_exact + (jnp.log(nf / max_exact) / math.log(MAX_DISTANCE / max_exact)
                         * (NUM_BUCKETS - max_exact)).astype(jnp.int32)
    large = jnp.minimum(large, NUM_BUCKETS - 1)
    return jnp.where(n < max_exact, n, large)


def _bias_of(rel_bias, dist):
    bucket = _rel_bucket(dist)[..., None]
    out = jnp.zeros(bucket.shape[:-1] + (rel_bias.shape[1],), F32)
    for k in range(NUM_BUCKETS):
        out = jnp.where(bucket == k, rel_bias[k].astype(F32), out)
    return out


def _compress_accumulate(load_rows, pe_ref, w_ref, nc):
    accs = []
    half = CMP_BLOCK // 2
    for slot in range(2):
        acc = jnp.zeros((nc, w_ref.shape[-1]), F32)
        for j in range(half):
            pair = [load_rows(slot, jj) + pe_ref[slot, jj:jj + 1, :] for jj in (j, j + half)]
            acc = acc + _dot(jnp.concatenate(pair, axis=1).astype(BF16), w_ref[slot, j])
        accs.append(acc)
    return jnp.concatenate(accs, axis=1)


def _file_rows(tile_t, xrow, sl, page):
    rows = tile_t.T
    per_page = PAGE_SIZE // CMP_BLOCK
    for cl in range(per_page):
        for a in range(CMP_BLOCK // SUBLANES):
            r0 = cl * CMP_BLOCK + a * SUBLANES
            dst = pl.multiple_of((page * per_page + cl) * SUBLANES, SUBLANES)
            xrow[sl, a, pl.ds(dst, SUBLANES), :] = rows[r0:r0 + SUBLANES, :]


def _compress_filed(xrow, pe_ref, w_ref, nc):
    return _compress_accumulate(
        lambda sl, j: xrow[sl, j // SUBLANES, pl.ds(j % SUBLANES, nc, stride=SUBLANES), :], pe_ref, w_ref, nc)


def _compress_prompt_kernel(x_ref, pe_ref, w_ref, kc_ref, vct_ref, xrow, *, nc, ncp, kvd):
    for p in range(x_ref.shape[2] // PAGE_SIZE):
        for sl in range(2):
            _file_rows(x_ref[0, sl * kvd:(sl + 1) * kvd, p * PAGE_SIZE:(p + 1) * PAGE_SIZE], xrow, sl, p)
    acc = _compress_filed(xrow, pe_ref, w_ref, nc)
    if ncp > nc:
        acc = jnp.concatenate([acc, jnp.zeros((ncp - nc, 2 * kvd), F32)], axis=0)
    kc_ref[0] = acc[:, 0:kvd].astype(BF16)
    vct_ref[0] = acc[:, kvd:2 * kvd].T.astype(BF16)


def compress_prompt(kv_t, pe2, wbd, ncp):
    b, _, s = kv_t.shape
    assert s % PAGE_SIZE == 0
    nc = s // CMP_BLOCK
    kvd = wbd.shape[-1]
    return pl.pallas_call(
        functools.partial(_compress_prompt_kernel, nc=nc, ncp=ncp, kvd=kvd), grid=(b,),
        in_specs=[pl.BlockSpec((1, 2 * kvd, s), lambda bi: (bi, 0, 0)), _full(pe2.shape), _full(wbd.shape)],
        out_specs=[pl.BlockSpec((1, ncp, kvd), lambda bi: (bi, 0, 0)), pl.BlockSpec((1, kvd, ncp), lambda bi: (bi, 0, 0))],
        out_shape=[jax.ShapeDtypeStruct((b, ncp, kvd), BF16), jax.ShapeDtypeStruct((b, kvd, ncp), BF16)],
        scratch_shapes=[pltpu.VMEM((2, CMP_BLOCK // SUBLANES, nc * SUBLANES, kvd), F32)],
        compiler_params=_cparams(("parallel",)), name="compress_prompt",
    )(kv_t, pe2, wbd)


def _select_blocks(score, n_sel, n_cand):
    n = score.shape[1]
    col = jnp.broadcast_to(score, (LANES, n)).T[:, 0:1]
    i = lax.broadcasted_iota(jnp.int32, (n, 1), 0)
    lane = lax.broadcasted_iota(jnp.int32, (1, n), 1)
    beats = ((i % 2) == 0) & (i < 2 * n_cand) & ((col > score) | ((col == score) & (i < lane)))
    rank = jnp.sum(jnp.where(beats, 1.0, 0.0), axis=0, keepdims=True)
    is_cand = ((lane % 2) == 0) & (lane < 2 * n_cand)
    return (is_cand & (rank < n_sel) & (score >= 0)).astype(F32), rank


def _pair_sum(imp):
    n = imp.shape[1]
    return imp + pltpu.roll(imp, n - 1, 1)


def _rank_rows(score, n_sel, n_cand):
    blk = lax.broadcasted_iota(jnp.int32, score.shape, 0)
    rank = jnp.zeros(score.shape, F32)
    for i in range(n_cand):
        row = score[i:i + 1, :]
        beats = (row > score) | ((row == score) & (blk > i))
        rank = rank + beats.astype(F32)
    return ((rank < n_sel) & (score >= 0)).astype(F32)


def _nsa_prompt_kernel(qt_ref, gt_ref, kc_ref, vct_ref, keys_ref, vt_ref, biasc_ref, btile_ref, o_ref,
                       qt_scr, oc_scr, acc_s, acc_w, imp_scr, sel_scr, out_scr, *, nc, nsb, n_sel):
    qi = pl.program_id(1)
    g, hd, kvh, qb = NSA_GROUP, NSA_HD, NSA_KV_HEADS, Q_BLOCK
    kvd = kvh * hd
    ncp = kc_ref.shape[1]
    nsbp = sel_scr.shape[1]
    q_pos = qi * qb + lax.broadcasted_iota(jnp.int32, (1, qb), 1)
    key_row = lax.broadcasted_iota(jnp.int32, (qb, 1), 0)
    c_row = lax.broadcasted_iota(jnp.int32, (ncp, 1), 0)
    mask_c = (q_pos >= c_row * CMP_BLOCK + (CMP_BLOCK - 1)) & (c_row < nc)
    blk = lax.broadcasted_iota(jnp.int32, (nsbp, 1), 0)
    cur = q_pos // SEL_BLOCK
    forced = (blk == 0) | (blk == cur) | (blk == cur - 1)
    zeros = jnp.zeros((hd, qb), BF16)
    for k in range(kvh):
        for gi in range(g):
            h = k * g + gi
            parts = [zeros] * kvh
            parts[k] = qt_ref[0, h * hd:(h + 1) * hd, :]
            qt_scr[k, :, gi * qb:(gi + 1) * qb] = jnp.concatenate(parts, axis=0)
        s_c = _dot(kc_ref[0], qt_scr[k])
        imp = jnp.zeros((ncp, qb), F32)
        probs = []
        for gi in range(g):
            s = jnp.where(mask_c, s_c[:, gi * qb:(gi + 1) * qb] + biasc_ref[k * g + gi], NEG)
            e = jnp.where(mask_c, jnp.exp(s - jnp.max(s, 0, keepdims=True)), 0.0)
            den = jnp.sum(e, 0, keepdims=True)
            p = e / jnp.where(den > 0, den, 1.0)
            imp = imp + p
            probs.append(p.astype(BF16))
        oc_scr[k] = _dot(vct_ref[0, k * hd:(k + 1) * hd, :], jnp.concatenate(probs, axis=1))
        imp_scr[...] = imp + pltpu.roll(imp, ncp - 1, 0)
        cand = imp_scr[pl.ds(0, nsbp, stride=SEL_RATIO), :]
        score = jnp.where(forced, FORCE_SCORE, jnp.where(blk <= cur, cand, -1.0))
        sel_scr[k] = _rank_rows(jnp.where(blk < nsb, score, -2.0), n_sel, nsb)

    per_tile = qb // SEL_BLOCK
    n_tiles = keys_ref.shape[1] // qb
    first = ([jnp.full((1, qb), NEG, F32)] * g, [jnp.zeros((1, qb), F32)] * g)

    def tile_step(tiles, carry, key_col, val_row, acc_ref, window):
        kts = [jnp.clip(kt, 0, n_tiles - 1) for kt, _ in tiles]
        starts = [pl.multiple_of(kt * qb, qb) for kt in kts]
        k_t = jnp.concatenate([keys_ref[0, pl.ds(r0, qb), key_col:key_col + kvd] for r0 in starts], axis=0)
        dist = jnp.concatenate([jnp.where(active, q_pos - (r0 + key_row), -1)
                                for r0, (_, active) in zip(starts, tiles)], axis=0)
        in_range = dist >= 0
        scores = [_dot(k_t, qt_scr[k]) for k in range(kvh)]
        new, updates = [], []
        for k in range(kvh):
            if window:
                valid = in_range & (dist < WINDOW)
            else:
                pieces = []
                for kt in kts:
                    chosen = jnp.zeros((qb, qb), F32)
                    for j in range(per_tile):
                        row = sel_scr[k, pl.ds(per_tile * kt + j, 1), :]
                        chosen = jnp.where(key_row // SEL_BLOCK == j, row, chosen)
                    pieces.append(chosen)
                valid = in_range & (jnp.concatenate(pieces, axis=0) > 0.5)
            ms, ls = carry[k]
            ms2, ls2, alphas, probs = [], [], [], []
            for gi in range(g):
                bias = jnp.concatenate([btile_ref[jnp.maximum(qi - kt, 0), k * g + gi] for kt in kts], axis=0)
                s = jnp.where(valid, scores[k][:, gi * qb:(gi + 1) * qb] + bias, NEG)
                m_new = jnp.maximum(ms[gi], jnp.max(s, 0, keepdims=True))
                alpha = jnp.exp(ms[gi] - m_new)
                p = jnp.exp(s - jnp.where(m_new == NEG, 0.0, m_new))
                ms2.append(m_new)
                ls2.append(alpha * ls[gi] + jnp.sum(p, 0, keepdims=True))
                alphas.append(alpha)
                probs.append(p.astype(BF16))
            new.append((ms2, ls2))
            updates.append((jnp.concatenate(alphas, axis=1), jnp.concatenate(probs, axis=1)))
        for k, (alpha, prob) in enumerate(updates):
            v_t = jnp.concatenate([vt_ref[kt, val_row + k * hd:val_row + (k + 1) * hd, :] for kt in kts], axis=1)
            acc_ref[k] = acc_ref[k] * alpha + _dot(v_t, prob)
        return tuple(new)

    acc_s[...] = jnp.zeros_like(acc_s)
    acc_w[...] = jnp.zeros_like(acc_w)
    sel_args = dict(key_col=0, val_row=0, acc_ref=acc_s, window=False)
    win_args = dict(key_col=kvd, val_row=kvd, acc_ref=acc_w, window=True)

    def sel_group(i, carry):
        return tile_step([(SEL_TILES * i + j, SEL_TILES * i + j <= qi) for j in range(SEL_TILES)], carry, **sel_args)

    stat_s = lax.fori_loop(0, qi // SEL_TILES + 1, sel_group, (first,) * kvh)
    stat_w = (first,) * kvh
    win_tiles = [(qi - j, qi - j >= 0) for j in range(WINDOW // qb, -1, -1)]
    for j in range(0, len(win_tiles), WIN_TILES):
        stat_w = tile_step(win_tiles[j:j + WIN_TILES], stat_w, **win_args)
    gt = gt_ref[0]
    for k in range(kvh):
        for gi in range(g):
            h = k * g + gi
            cols = slice(gi * qb, (gi + 1) * qb)
            l_s, l_w = stat_s[k][1][gi], stat_w[k][1][gi]
            o_s = acc_s[k, :, cols] / jnp.where(l_s > 0, l_s, 1.0)
            o_w = acc_w[k, :, cols] / jnp.where(l_w > 0, l_w, 1.0)
            out_scr[h * hd:(h + 1) * hd, :] = (gt[3 * h:3 * h + 1, :] * oc_scr[k, :, cols]
                                               + gt[3 * h + 1:3 * h + 2, :] * o_s + gt[3 * h + 2:3 * h + 3, :] * o_w)
    o_ref[0] = out_scr[...].T.astype(o_ref.dtype)


def nsa_prompt(qt, gt, kc, vct, keys, vt, rel_bias, b, s):
    qb = Q_BLOCK
    nq = s // qb
    nc = s // CMP_BLOCK
    ncp = kc.shape[1]
    nsb = s // SEL_BLOCK
    nsbp = _round_up(nsb, SUBLANES)
    assert SEL_RATIO * nsbp <= ncp
    n_sel = min(TOP_N, nsb)
    nh = NSA_HEADS
    hq = qt.shape[1]
    kvd = kc.shape[2]
    glanes = NSA_GROUP * qb
    cend = jnp.arange(ncp)[:, None] * CMP_BLOCK + (CMP_BLOCK - 1)
    biasc = jnp.transpose(_bias_of(rel_bias, jnp.arange(s)[None, :] - cend), (2, 0, 1))
    r = jnp.arange(qb)
    dist = jnp.arange(nq)[:, None, None] * qb + r[None, None, :] - r[None, :, None]
    btile = jnp.transpose(_bias_of(rel_bias, dist), (0, 3, 1, 2))
    out = pl.pallas_call(
        functools.partial(_nsa_prompt_kernel, nc=nc, nsb=nsb, n_sel=n_sel), grid=(b, nq),
        in_specs=[pl.BlockSpec((1, hq, qb), lambda bi, i: (bi * nq + i, 0, 0)),
                  pl.BlockSpec((1, LANES, qb), lambda bi, i: (bi * nq + i, 0, 0)),
                  pl.BlockSpec((1, ncp, kvd), lambda bi, i: (bi, 0, 0)),
                  pl.BlockSpec((1, kvd, ncp), lambda bi, i: (bi, 0, 0)),
                  pl.BlockSpec((1, s, 2 * kvd), lambda bi, i: (bi, 0, 0)),
                  pl.BlockSpec((nq, 2 * kvd, qb), lambda bi, i: (bi, 0, 0)),
                  pl.BlockSpec((nh, ncp, qb), lambda bi, i: (0, 0, i)),
                  _full(btile.shape)],
        out_specs=pl.BlockSpec((1, qb, hq), lambda bi, i: (bi, i, 0)),
        out_shape=jax.ShapeDtypeStruct((b, s, hq), BF16),
        scratch_shapes=[pltpu.VMEM((NSA_KV_HEADS, kvd, glanes), BF16), pltpu.VMEM((NSA_KV_HEADS, NSA_HD, glanes), F32),
                        pltpu.VMEM((NSA_KV_HEADS, NSA_HD, glanes), F32), pltpu.VMEM((NSA_KV_HEADS, NSA_HD, glanes), F32),
                        pltpu.VMEM((ncp, qb), F32), pltpu.VMEM((NSA_KV_HEADS, nsbp, qb), F32),
                        pltpu.VMEM((hq, qb), F32)],
        compiler_params=_cparams(("parallel", "arbitrary"), VMEM_LIMIT), name="nsa_prompt",
    )(qt, gt, kc, vct, keys.reshape(b, s, 2 * kvd), vt, biasc, btile)
    return out.reshape(b * s, hq)


def _round_up(x, m):
    return (x + m - 1) // m * m


def prep_even(w_in, pe, wc):
    d, n = w_in.shape
    n_pad = _round_up(n - 3 * NSA_HEADS, LANES) + LANES
    w_pad = jnp.zeros((d, n_pad), BF16).at[:, :n].set(w_in.astype(BF16))
    pe2 = jnp.tile(pe, (1, 1, NSA_KV_HEADS))
    zero = jnp.zeros_like(wc)
    wbd = jnp.concatenate([jnp.concatenate([wc if i == j else zero for j in range(NSA_KV_HEADS)], axis=-1)
                           for i in range(NSA_KV_HEADS)], axis=-2)
    half = CMP_BLOCK // 2
    wbd = jnp.concatenate([wbd[:, :half], wbd[:, half:]], axis=2)
    return dict(w_in=w_pad, pe2=pe2, wbd=wbd.astype(BF16))


def _memkv_kernel(x_ref, g_ref, w_ref, o_ref, ob_ref, *, hd, nh):
    y = _dot(_rms(x_ref[...], g_ref[...]).astype(BF16), w_ref[...])
    ob_ref[...] = y.astype(BF16)
    tm = y.shape[0]
    chunks = hd // LANES
    period = 2 * chunks * nh
    for kv in range(2):
        for h in range(nh):
            for c in range(chunks):
                col = (kv * nh + h) * hd + c * LANES
                o_ref[pl.ds((kv * chunks + c) * nh + h, tm, stride=period), :] = y[:, col:col + LANES]


def memkv(mem, g, w, nh):
    m, d = mem.shape
    n = w.shape[1]
    hd = n // (2 * nh)
    per_tok = n // LANES
    tm = _row_tile(m, 256)
    return pl.pallas_call(
        functools.partial(_memkv_kernel, hd=hd, nh=nh), grid=(m // tm,),
        in_specs=[pl.BlockSpec((tm, d), lambda i: (i, 0)), _full((1, d)), _full(w.shape)],
        out_specs=[pl.BlockSpec((tm * per_tok, LANES), lambda i: (i, 0)), pl.BlockSpec((tm, n), lambda i: (i, 0))],
        out_shape=[jax.ShapeDtypeStruct((m * per_tok, LANES), F32), jax.ShapeDtypeStruct((m, n), BF16)],
        compiler_params=_cparams(("parallel",)), name="memkv",
    )(mem, g.reshape(1, d), w)


def _xattn_core(q, kv, hd):
    nh = q.shape[1] // hd
    outs = []
    for h in range(nh):
        s = _dot_nt(q[:, h * hd:(h + 1) * hd], kv[:, h * hd:(h + 1) * hd])
        e = jnp.exp(s - jnp.max(s, -1, keepdims=True))
        p = e / jnp.sum(e, -1, keepdims=True)
        outs.append(_dot(p.astype(BF16), kv[:, (nh + h) * hd:(nh + h + 1) * hd]))
    return jnp.concatenate(outs, axis=1).astype(BF16)


def _xattn_prompt_kernel(*refs, hd, n_in):
    x_ref, g_ref, wq_ref, kv_ref, wo_ref = refs[:5]
    a_refs = refs[5:5 + n_in]
    w_refs = refs[5 + n_in:5 + 2 * n_in]
    o_ref = refs[5 + 2 * n_in]
    x = x_ref[0]
    for a_ref, w_ref in zip(a_refs, w_refs):
        x = x + _dot(a_ref[0], w_ref[...])
    q = (_dot(_rms(x, g_ref[...]).astype(BF16), wq_ref[...]) * (hd ** -0.5)).astype(BF16)
    o = _xattn_core(q, kv_ref[0], hd)
    o_ref[0] = x + _dot(o, wo_ref[...])


def xattn_prompt(x, g, wq, kvb, wo, b, s, acts=(), ws=()):
    d = x.shape[1]
    mt = kvb.shape[0] // b
    tm = _row_tile(s, 512)
    tile = lambda n: pl.BlockSpec((1, tm, n), lambda bi, i: (bi, i, 0))
    out = pl.pallas_call(
        functools.partial(_xattn_prompt_kernel, hd=d // X_HEADS, n_in=len(acts)), grid=(b, s // tm),
        in_specs=[tile(d), _full((1, d)), _full(wq.shape),
                  pl.BlockSpec((1, mt, kvb.shape[1]), lambda bi, i: (bi, 0, 0)), _full(wo.shape)]
        + [tile(a.shape[1]) for a in acts] + [_full(w.shape) for w in ws],
        out_specs=tile(d),
        out_shape=jax.ShapeDtypeStruct((b, s, d), F32),
        compiler_params=_cparams(("parallel", "parallel"), VMEM_LIMIT), name="xattn_prompt",
    )(x.reshape(b, s, d), g.reshape(1, d), wq, kvb.reshape(b, mt, -1), wo,
      *[a.reshape(b, s, -1) for a in acts], *ws)
    return out.reshape(b * s, d)


def _xattn_sample_kernel(x_ref, g_ref, wq_ref, kv_ref, wo_ref, o_ref, q_scr, a_scr, *, hd, mt):
    bi = pl.program_id(0)
    nb = pl.num_programs(0)
    nh = wq_ref.shape[1] // hd

    @pl.when(bi == 0)
    def _():
        q_scr[...] = _dot(_rms(x_ref[...], g_ref[...]).astype(BF16), wq_ref[...]) * (hd ** -0.5)

    q = jnp.broadcast_to(q_scr[pl.ds(bi, 1), :], (SUBLANES, q_scr.shape[1])).astype(BF16)
    chunks = hd // LANES
    period = 2 * chunks * nh

    def head_rows(kv, h):
        return jnp.concatenate([kv_ref[0, pl.ds((kv * chunks + c) * nh + h, mt, stride=period), :]
                                for c in range(chunks)], axis=1).astype(BF16)

    outs = []
    for h in range(nh):
        s = _dot_nt(q[:, h * hd:(h + 1) * hd], head_rows(0, h))
        e = jnp.exp(s - jnp.max(s, -1, keepdims=True))
        p = e / jnp.sum(e, -1, keepdims=True)
        outs.append(_dot(p.astype(BF16), head_rows(1, h)))
    a_scr[pl.ds(bi, 1), :] = jnp.concatenate(outs, axis=1)[0:1, :]

    @pl.when(bi == nb - 1)
    def _():
        o_ref[...] = x_ref[...] + _dot(a_scr[...].astype(BF16), wo_ref[...])


def xattn_sample(x, g, wq, kv_rows, wo, layer, mt):
    bd, d = x.shape
    rows = kv_rows.shape[1]
    return pl.pallas_call(
        functools.partial(_xattn_sample_kernel, hd=d // X_HEADS, mt=mt), grid=(bd,),
        in_specs=[_full((bd, d)), _full((1, d)), _full(wq.shape),
                  pl.BlockSpec((1, rows, LANES), lambda bi: (layer * bd + bi, 0, 0)), _full(wo.shape)],
        out_specs=_full((bd, d)),
        out_shape=jax.ShapeDtypeStruct((bd, d), F32),
        scratch_shapes=[pltpu.VMEM((bd, wq.shape[1]), F32), pltpu.VMEM((bd, wq.shape[1]), F32)],
        compiler_params=_cparams(("arbitrary",), VMEM_LIMIT), name="xattn_sample",
    )(x, g.reshape(1, d), wq, kv_rows, wo)


def _ffn_kernel(x_ref, g_ref, wg_ref, wu_ref, wd_ref, o_ref, h_scr, acc_scr):
    c = pl.program_id(1)

    @pl.when(c == 0)
    def _():
        h_scr[...] = _rms(x_ref[...], g_ref[...]).astype(BF16)
        acc_scr[...] = x_ref[...]

    h = h_scr[...]
    gate = _dot(h, wg_ref[...])
    up = _dot(h, wu_ref[...])
    act = (gate * jax.nn.sigmoid(gate) * up).astype(BF16)
    acc_scr[...] += _dot(act, wd_ref[...])

    @pl.when(c == pl.num_programs(1) - 1)
    def _():
        o_ref[...] = acc_scr[...]


def _xattn_ffn_kernel(*refs, hd, n_in):
    x_ref, gx_ref, wq_ref, kv_ref, wo_ref, gf_ref, wg_ref, wu_ref, wd_ref = refs[:9]
    a_refs = refs[9:9 + n_in]
    w_refs = refs[9 + n_in:9 + 2 * n_in]
    o_ref, h_scr, acc_scr = refs[9 + 2 * n_in:]
    c = pl.program_id(1)

    @pl.when(c == 0)
    def _():
        x = x_ref[...]
        for a_ref, w_ref in zip(a_refs, w_refs):
            x = x + _dot(a_ref[...], w_ref[...])
        q = (_dot(_rms(x, gx_ref[...]).astype(BF16), wq_ref[...]) * (hd ** -0.5)).astype(BF16)
        x = x + _dot(_xattn_core(q, kv_ref[0], hd), wo_ref[...])
        h_scr[...] = _rms(x, gf_ref[...]).astype(BF16)
        acc_scr[...] = x

    h = h_scr[...]
    gate = _dot(h, wg_ref[...])
    up = _dot(h, wu_ref[...])
    act = (gate * jax.nn.sigmoid(gate) * up).astype(BF16)
    acc_scr[...] += _dot(act, wd_ref[...])

    @pl.when(c == pl.num_programs(1) - 1)
    def _():
        o_ref[...] = acc_scr[...]


def xattn_ffn(x, gx, wq, kvb, wo, acts, ws, gf, w_gu, w_dn, b, s):
    m, d = x.shape
    dff = w_dn.shape[0]
    mt = kvb.shape[0] // b
    tm = _row_tile(s, 512)
    per_seq = s // tm
    fc = _ff_chunk(dff, 1408)
    nch = dff // fc
    row = lambda n: pl.BlockSpec((tm, n), lambda i, c: (i, 0))
    return pl.pallas_call(
        functools.partial(_xattn_ffn_kernel, hd=d // X_HEADS, n_in=len(acts)), grid=(m // tm, nch),
        in_specs=[row(d), _full((1, d)), _full(wq.shape),
                  pl.BlockSpec((1, mt, kvb.shape[1]), lambda i, c: (i // per_seq, 0, 0)), _full(wo.shape),
                  _full((1, d)),
                  pl.BlockSpec((d, fc), lambda i, c: (0, c)),
                  pl.BlockSpec((d, fc), lambda i, c: (0, nch + c)),
                  pl.BlockSpec((fc, d), lambda i, c: (c, 0))]
        + [row(a.shape[1]) for a in acts] + [_full(w.shape) for w in ws],
        out_specs=row(d),
        out_shape=jax.ShapeDtypeStruct((m, d), F32),
        scratch_shapes=[pltpu.VMEM((tm, d), BF16), pltpu.VMEM((tm, d), F32)],
        compiler_params=_cparams(("parallel", "arbitrary"), VMEM_LIMIT), name="xattn_ffn",
    )(x, gx.reshape(1, d), wq, kvb.reshape(b, mt, -1), wo, gf.reshape(1, d), w_gu, w_gu, w_dn, *acts, *ws)


def _ff_chunk(dff, pref):
    c = dff
    for n in range(1, dff // LANES + 1):
        if dff % n == 0 and (dff // n) % LANES == 0 and dff // n <= pref:
            c = dff // n
            break
    return c


def ffn(x, g, w_gu, w_dn):
    m, d = x.shape
    dff = w_dn.shape[0]
    tm = _row_tile(m, 512)
    fc = _ff_chunk(dff, 1408)
    nch = dff // fc
    return pl.pallas_call(
        _ffn_kernel, grid=(m // tm, nch),
        in_specs=[pl.BlockSpec((tm, d), lambda i, c: (i, 0)), _full((1, d)),
                  pl.BlockSpec((d, fc), lambda i, c: (0, c)),
                  pl.BlockSpec((d, fc), lambda i, c: (0, nch + c)),
                  pl.BlockSpec((fc, d), lambda i, c: (c, 0))],
        out_specs=pl.BlockSpec((tm, d), lambda i, c: (i, 0)),
        out_shape=jax.ShapeDtypeStruct((m, d), F32),
        scratch_shapes=[pltpu.VMEM((tm, d), BF16), pltpu.VMEM((tm, d), F32)],
        compiler_params=_cparams(("parallel", "arbitrary"), VMEM_LIMIT), name="ffn",
    )(x, g.reshape(1, d), w_gu, w_gu, w_dn)


def _router_kernel(x_ref, g_ref, w_ref, b_ref, comb_ref, h_ref, mask_ref, cnt_ref, *, ne):
    h = _rms(x_ref[...], g_ref[...]).astype(BF16)
    h_ref[...] = h
    logits = _dot(h, w_ref[...]) + b_ref[...]
    lane = lax.broadcasted_iota(jnp.int32, logits.shape, 1)
    logits = jnp.where(lane < ne, logits, -jnp.inf)
    v1 = jnp.max(logits, -1, keepdims=True)
    i1 = jnp.min(jnp.where(logits == v1, lane, LANES), -1, keepdims=True)
    rest = jnp.where(lane == i1, -jnp.inf, logits)
    v2 = jnp.max(rest, -1, keepdims=True)
    i2 = jnp.min(jnp.where(rest == v2, lane, LANES), -1, keepdims=True)
    e2 = jnp.exp(v2 - v1)
    den = 1.0 + e2
    comb_ref[...] = jnp.where(lane == i1, 1.0 / den, 0.0) + jnp.where(lane == i2, e2 / den, 0.0)
    chosen = jnp.where((lane == i1) | (lane == i2), 1.0, 0.0)
    mask_ref[...] = chosen.astype(BF16)

    @pl.when(pl.program_id(0) == 0)
    def _():
        cnt_ref[...] = jnp.zeros_like(cnt_ref)

    cnt_ref[0:1, :] += jnp.sum(chosen, axis=0, keepdims=True)


def router(x, g, w_r, b_r):
    m, d = x.shape
    ne = w_r.shape[1]
    w_pad = jnp.zeros((d, LANES), BF16).at[:, :ne].set(w_r.astype(BF16))
    b_pad = jnp.zeros((1, LANES), F32).at[0, :ne].set(b_r.astype(F32))
    tm = _row_tile(m, 512)
    row = lambda n: pl.BlockSpec((tm, n), lambda i: (i, 0))
    return pl.pallas_call(
        functools.partial(_router_kernel, ne=ne), grid=(m // tm,),
        in_specs=[row(d), _full((1, d)), _full((d, LANES)), _full((1, LANES))],
        out_specs=[row(LANES), row(d), row(LANES), _full((SUBLANES, LANES))],
        out_shape=[jax.ShapeDtypeStruct((m, LANES), F32), jax.ShapeDtypeStruct((m, d), BF16),
                   jax.ShapeDtypeStruct((m, LANES), BF16), jax.ShapeDtypeStruct((SUBLANES, LANES), F32)],
        compiler_params=_cparams(("arbitrary",)), name="router",
    )(x, g.reshape(1, d), w_pad, b_pad)


def _residual_out(y, gain_ref, norm):
    return _rms(y, gain_ref[...]) if norm else y


def _moe_kernel(x_ref, h_ref, comb_ref, wg_ref, wu_ref, wd_ref, gain_ref, o_ref, acc_scr, *, norm):
    e = pl.program_id(1)

    @pl.when(e == 0)
    def _():
        acc_scr[...] = jnp.zeros_like(acc_scr)

    h = h_ref[...]
    gate = _dot(h, wg_ref[0])
    up = _dot(h, wu_ref[0])
    act = (gate * jax.nn.sigmoid(gate) * up).astype(BF16)
    y = _dot(act, wd_ref[0])
    comb = comb_ref[...]
    lane = lax.broadcasted_iota(jnp.int32, comb.shape, 1)
    acc_scr[...] += jnp.sum(jnp.where(lane == e, comb, 0.0), -1, keepdims=True) * y

    @pl.when(e == pl.num_programs(1) - 1)
    def _():
        o_ref[...] = _residual_out(x_ref[...] + acc_scr[...], gain_ref, norm)


def moe(x, h, comb, w_gu, w_dn, final_g=None):
    m, d = x.shape
    ne, dfe = w_dn.shape[:2]
    tm = _row_tile(m, 512)
    gain = jnp.ones((1, d), F32) if final_g is None else final_g.reshape(1, d)
    return pl.pallas_call(
        functools.partial(_moe_kernel, norm=final_g is not None), grid=(m // tm, ne),
        in_specs=[pl.BlockSpec((tm, d), lambda i, e: (i, 0)), pl.BlockSpec((tm, d), lambda i, e: (i, 0)),
                  pl.BlockSpec((tm, LANES), lambda i, e: (i, 0)),
                  pl.BlockSpec((1, d, dfe), lambda i, e: (e, 0, 0)),
                  pl.BlockSpec((1, d, dfe), lambda i, e: (e, 0, 1)),
                  pl.BlockSpec((1, dfe, d), lambda i, e: (e, 0, 0)), _full((1, d))],
        out_specs=pl.BlockSpec((tm, d), lambda i, e: (i, 0)),
        out_shape=jax.ShapeDtypeStruct((m, d), F32),
        scratch_shapes=[pltpu.VMEM((tm, d), F32)],
        compiler_params=_cparams(("parallel", "arbitrary"), VMEM_LIMIT), name="moe",
    )(x, h, comb, w_gu, w_gu, w_dn, gain)


MOE_TILE = 256


def _moe_pos_kernel(mask_ref, tri_ref, base_ref, post_ref, pos_ref, stab_ref, run_scr, *, nep):
    sb = pl.program_id(0)
    nb = pl.num_programs(0)

    @pl.when(sb == 0)
    def _():
        run_scr[...] = jnp.zeros_like(run_scr)
        stab_ref[...] = jnp.zeros_like(stab_ref)

    a = mask_ref[...]
    af = a.astype(F32)
    start = base_ref[...] + run_scr[...]
    stab_ref[pl.ds(sb, 1), :] = start.astype(jnp.int32)
    rank = _dot(tri_ref[...], a)
    pos = jnp.where(af > 0, start + rank, -1.0)
    pos_ref[...] = pos.astype(jnp.int32)
    post_ref[0] = pos.T[0:nep, :].astype(jnp.int32)
    run_scr[...] += jnp.sum(af, axis=0, keepdims=True)

    @pl.when(sb == nb - 1)
    def _():
        stab_ref[pl.ds(nb, 1), :] = (base_ref[...] + run_scr[...]).astype(jnp.int32)


def _moe_expert_kernel(te_ref, lo_ref, cnt_ref, ring_ref, nt_ref, h_ref, post_ref, wg_ref, wu_ref, wd_ref, y_ref,
                       hbuf, xg_scr, sem, *, t):
    i = pl.program_id(0)

    @pl.when(i >= nt_ref[0])
    def _():
        y_ref[...] = jnp.zeros_like(y_ref)

    nbuf = hbuf.shape[0]
    ahead = nbuf - 1

    def copy(sb, slot):
        return pltpu.make_async_copy(h_ref.at[pl.ds(pl.multiple_of(sb * t, t), t), :], hbuf.at[slot], sem.at[slot])

    def start_first(tile):
        for j in range(ahead):
            @pl.when(j < cnt_ref[tile])
            def _():
                copy(lo_ref[tile] + j, (ring_ref[tile] + j) % nbuf).start()

    @pl.when(i == 0)
    def _():
        start_first(0)

    @pl.when(i < nt_ref[0])
    def _():
        e = te_ref[i]
        lo = lo_ref[i]
        n = cnt_ref[i]
        ring = ring_ref[i]
        xg_scr[...] = jnp.zeros_like(xg_scr)
        row = i * t + lax.broadcasted_iota(jnp.int32, (t, 1), 0)

        def body(j, c):
            slot = (ring + j) % nbuf
            copy(lo + j, slot).wait()

            @pl.when(j + ahead < n)
            def _():
                copy(lo + j + ahead, (ring + j + ahead) % nbuf).start()

            src_pos = post_ref[lo + j, pl.ds(e, 1), :]
            onehot = jnp.where(src_pos == row, 1.0, 0.0).astype(BF16)
            xg_scr[...] += _dot(onehot, hbuf[slot])
            return c

        lax.fori_loop(0, n, body, 0)

        @pl.when(i + 1 < nt_ref[0])
        def _():
            start_first(i + 1)

        x = xg_scr[...].astype(BF16)
        gate = _dot(x, wg_ref[0])
        up = _dot(x, wu_ref[0])
        act = (gate * jax.nn.sigmoid(gate) * up).astype(BF16)
        y_ref[...] = _dot(act, wd_ref[0])


MOE_WIN = 128


def _moe_combine_kernel(stab_ref, x_ref, comb_ref, pos_ref, gain_ref, ys_ref, o_ref, ybuf, extra, acc_scr,
                        sem, sem_x, *, t, ne, norm):
    sb = pl.program_id(0)
    nb = pl.num_programs(0)
    n_win = (t + SUBLANES - 1) // MOE_WIN + 1

    def window(b, e, w):
        s0 = stab_ref[b * ne + e]
        s1 = stab_ref[(b + 1) * ne + e]
        start = (s0 // SUBLANES) * SUBLANES + w * MOE_WIN
        return pl.multiple_of(start, SUBLANES), start < s1

    def first_copy(b, slot, e):
        start, _ = window(b, e, 0)
        return pltpu.make_async_copy(ys_ref.at[pl.ds(start, MOE_WIN), :], ybuf.at[slot, e], sem.at[slot, e])

    def start_all(b, slot):
        for e in range(ne):
            @pl.when(window(b, e, 0)[1])
            def _():
                first_copy(b, slot, e).start()

    @pl.when(sb == 0)
    def _():
        ybuf[...] = jnp.zeros_like(ybuf)
        start_all(0, 0)

    @pl.when(sb + 1 < nb)
    def _():
        start_all(sb + 1, (sb + 1) % 2)

    slot = sb % 2
    comb = comb_ref[...]
    pos = pos_ref[...]
    lane_r = lax.broadcasted_iota(jnp.int32, (1, MOE_WIN), 1)

    def split(a):
        hi = a.astype(BF16)
        return hi, (a - hi.astype(F32)).astype(BF16)

    def weighted_onehot(start, e):
        return jnp.where(pos[:, e:e + 1] - start == lane_r, comb[:, e:e + 1], 0.0)

    def gathered(sel, rows):
        s_hi, s_lo = split(sel)
        r_hi, r_lo = split(rows)
        return _dot(s_hi, r_hi) + _dot(s_hi, r_lo) + _dot(s_lo, r_hi)

    for e in range(ne):
        @pl.when(window(sb, e, 0)[1])
        def _():
            first_copy(sb, slot, e).wait()
    sel = jnp.concatenate([weighted_onehot(window(sb, e, 0)[0], e) for e in range(ne)], axis=1)
    acc_scr[...] = x_ref[...] + gathered(sel, ybuf[slot].reshape(ne * MOE_WIN, ybuf.shape[-1]))

    for e in range(ne):
        for w in range(1, n_win):
            start_w, needed_w = window(sb, e, w)

            @pl.when(needed_w)
            def _():
                cp = pltpu.make_async_copy(ys_ref.at[pl.ds(start_w, MOE_WIN), :], extra, sem_x.at[0])
                cp.start()
                cp.wait()
                acc_scr[...] += gathered(weighted_onehot(start_w, e), extra[...])

    o_ref[...] = _residual_out(acc_scr[...], gain_ref, norm)


def moe_grouped(x, h, comb, mask, counts, w_gu, w_dn, final_g=None):
    m, d = x.shape
    ne, dfe = w_dn.shape[:2]
    t = MOE_TILE
    assert m % t == 0
    nb = m // t
    nep = _round_up(ne, SUBLANES)
    nbp = _round_up(nb + 1, SUBLANES)
    k_top = TOP_K
    nt_max = k_top * m // t + ne + 1
    cnt = counts[0, :ne].astype(jnp.int32)
    cnt_pad = (cnt + t - 1) // t * t
    ends = jnp.cumsum(cnt_pad)
    base = ends - cnt_pad
    base_row = jnp.zeros((1, LANES), F32).at[0, :ne].set(base.astype(F32))
    idx = lax.broadcasted_iota(jnp.int32, (t, t), 0)
    tri = jnp.where(lax.broadcasted_iota(jnp.int32, (t, t), 1) < idx, 1.0, 0.0).astype(BF16)
    blk = lambda n2: pl.BlockSpec((t, n2), lambda i: (i, 0))
    post, pos, stab = pl.pallas_call(
        functools.partial(_moe_pos_kernel, nep=nep), grid=(nb,),
        in_specs=[blk(LANES), _full((t, t)), _full((1, LANES))],
        out_specs=[pl.BlockSpec((1, nep, t), lambda i: (i, 0, 0)), blk(LANES), _full((nbp, LANES))],
        out_shape=[jax.ShapeDtypeStruct((nb, nep, t), jnp.int32), jax.ShapeDtypeStruct((m, LANES), jnp.int32),
                   jax.ShapeDtypeStruct((nbp, LANES), jnp.int32)],
        scratch_shapes=[pltpu.VMEM((1, LANES), F32)],
        compiler_params=_cparams(("arbitrary",)), name="moe_positions",
    )(mask, tri, base_row)
    r0 = jnp.arange(nt_max, dtype=jnp.int32) * t
    tile_e = jnp.minimum(jnp.sum(ends[None, :] <= r0[:, None], axis=1), ne - 1).astype(jnp.int32)
    n_tiles = (ends[-1] // t).astype(jnp.int32).reshape(1)
    s_e = stab[:nb + 1, :ne][:, tile_e]
    lo = jnp.sum(s_e[1:] <= r0[None, :], axis=0)
    hi = jnp.sum(s_e[:nb] < r0[None, :] + t, axis=0) - 1
    lo = jnp.clip(lo, 0, nb - 1).astype(jnp.int32)
    hi = jnp.clip(hi, lo, nb - 1).astype(jnp.int32)
    n_src = jnp.where(jnp.arange(nt_max) < n_tiles[0], hi - lo + 1, 0).astype(jnp.int32)
    n_ring = 8
    ring = ((jnp.cumsum(n_src) - n_src) % n_ring).astype(jnp.int32)
    w_spec = lambda shape, col: pl.BlockSpec(shape, lambda i, te, *_: (te[i], 0, col))
    grid_spec = pltpu.PrefetchScalarGridSpec(
        num_scalar_prefetch=5, grid=(nt_max,),
        in_specs=[pl.BlockSpec(memory_space=pl.ANY),
                  pl.BlockSpec((nb, nep, t), lambda i, *_: (0, 0, 0)),
                  w_spec((1, d, dfe), 0), w_spec((1, d, dfe), 1), w_spec((1, dfe, d), 0)],
        out_specs=pl.BlockSpec((t, d), lambda i, *_: (i, 0)),
        scratch_shapes=[pltpu.VMEM((n_ring, t, d), BF16), pltpu.VMEM((t, d), F32),
                        pltpu.SemaphoreType.DMA((n_ring,))])
    ys = pl.pallas_call(
        functools.partial(_moe_expert_kernel, t=t), grid_spec=grid_spec,
        out_shape=jax.ShapeDtypeStruct((nt_max * t, d), F32),
        compiler_params=_cparams(("arbitrary",), VMEM_LIMIT), name="moe_experts",
    )(tile_e, lo, n_src, ring, n_tiles, h, post, w_gu, w_gu, w_dn)
    gain = jnp.ones((1, d), F32) if final_g is None else final_g.reshape(1, d)
    blk2 = lambda n2: pl.BlockSpec((t, n2), lambda i, st: (i, 0))
    grid_spec = pltpu.PrefetchScalarGridSpec(
        num_scalar_prefetch=1, grid=(nb,),
        in_specs=[blk2(d), blk2(LANES), blk2(LANES), pl.BlockSpec((1, d), lambda i, st: (0, 0)),
                  pl.BlockSpec(memory_space=pl.ANY)],
        out_specs=blk2(d),
        scratch_shapes=[pltpu.VMEM((2, ne, MOE_WIN, d), F32), pltpu.VMEM((MOE_WIN, d), F32), pltpu.VMEM((t, d), F32),
                        pltpu.SemaphoreType.DMA((2, ne)), pltpu.SemaphoreType.DMA((1,))])
    return pl.pallas_call(
        functools.partial(_moe_combine_kernel, t=t, ne=ne, norm=final_g is not None), grid_spec=grid_spec,
        out_shape=jax.ShapeDtypeStruct((m, d), F32),
        compiler_params=_cparams(("arbitrary",), VMEM_LIMIT), name="moe_combine",
    )(stab[:nb + 1, :ne].reshape(-1), x, comb, pos, gain, ys)


def _inproj_odd_kernel(x_ref, g_ref, w_ref, bias_ref, q_ref, k_ref, v_ref, og_ref, gc_ref, gr_ref, *, hq, hv, nh):
    xn = _rms(x_ref[...], g_ref[...]).astype(BF16)

    def mm(lo, hi):
        return _dot(xn, w_ref[:, lo:hi])

    dk = hq // nh
    q_ref[...] = mm(0, hq).astype(BF16)
    k_ref[...] = (mm(hq, 2 * hq) * (dk ** -0.5)).astype(BF16)
    v_ref[...] = mm(2 * hq, 2 * hq + hv).astype(BF16)
    og_ref[...] = jax.nn.sigmoid(mm(2 * hq + hv, 2 * hq + 2 * hv))
    gi = mm(2 * hq + 2 * hv, 2 * hq + 2 * hv + LANES) + bias_ref[...]
    lane = lax.broadcasted_iota(jnp.int32, gi.shape, 1)
    gates = jnp.where(lane < nh, gi, jax.nn.log_sigmoid(gi))
    gc_ref[...] = gates
    gr_ref[...] = gates.T[0:SUBLANES, :]


def inproj_odd(x, g, w_pad, gate_bias):
    m, d = x.shape
    nh = MLSTM_HEADS
    hq = hv = d
    tm = _row_tile(m, 256)
    row = lambda n: pl.BlockSpec((tm, n), lambda i: (i, 0))
    outs = [(hq, BF16), (hq, BF16), (hv, BF16), (hv, F32), (LANES, F32)]
    if tm % LANES:
        gr_spec = _full((SUBLANES, m))
    else:
        gr_spec = pl.BlockSpec((SUBLANES, tm), lambda i: (0, i))
    return pl.pallas_call(
        functools.partial(_inproj_odd_kernel, hq=hq, hv=hv, nh=nh), grid=(m // tm,),
        in_specs=[row(d), _full((1, d)), _full(w_pad.shape), _full((1, LANES))],
        out_specs=[row(n) for n, _ in outs] + [gr_spec],
        out_shape=[jax.ShapeDtypeStruct((m, n), t) for n, t in outs] + [jax.ShapeDtypeStruct((SUBLANES, m), F32)],
        compiler_params=_cparams(("parallel",), VMEM_LIMIT), name="inproj_odd",
    )(x, g.reshape(1, d), w_pad, gate_bias)


def prep_odd(w_in, b_i, b_f):
    d, n = w_in.shape
    n_pad = _round_up(n - 2 * MLSTM_HEADS, LANES) + LANES
    w_pad = jnp.zeros((d, n_pad), BF16).at[:, :n].set(w_in.astype(BF16))
    bias = jnp.zeros((1, LANES), F32).at[0, :2 * MLSTM_HEADS].set(jnp.concatenate([b_i, b_f]).astype(F32))
    return dict(w_in=w_pad, bias=bias)


def _mlstm_prompt_kernel(q_ref, k_ref, v_ref, og_ref, gc_ref, gr_ref, gain_ref, hn_ref, c_ref, n_ref, m_ref,
                         *, nh, dk, dv, ln):
    ci = pl.program_id(1)

    @pl.when(ci == 0)
    def _():
        c_ref[...] = jnp.zeros_like(c_ref)
        n_ref[...] = jnp.zeros_like(n_ref)
        m_ref[...] = jnp.full(m_ref.shape, NEG, F32)

    row = lax.broadcasted_iota(jnp.int32, (ln, ln), 0)
    col = lax.broadcasted_iota(jnp.int32, (ln, ln), 1)
    tri = row >= col
    gc = gc_ref[...]
    gr = gr_ref[...]
    for h in range(nh):
        q = q_ref[:, h * dk:(h + 1) * dk]
        k = k_ref[:, h * dk:(h + 1) * dk]
        v = v_ref[:, h * dv:(h + 1) * dv]
        ig_c, lf_c = gc[:, h:h + 1], gc[:, nh + h:nh + h + 1]
        ig_r, lf_r = gr[h:h + 1, :], gr[nh + h:nh + h + 1, :]
        b_c = jnp.sum(jnp.where(tri, lf_r, 0.0), axis=1, keepdims=True)
        b_r = jnp.sum(jnp.where(row <= col, lf_c, 0.0), axis=0, keepdims=True)
        m_prev = m_ref[0, h:h + 1, 0:1]
        c_prev = c_ref[0, h]
        n_prev = n_ref[0, h:h + 1, :]
        dmat = jnp.where(tri, b_c - b_r + ig_r, NEG)
        inter = b_c + m_prev
        mt = jnp.maximum(inter, jnp.max(dmat, -1, keepdims=True))
        wm = jnp.exp(dmat - mt)
        a = jnp.exp(inter - mt)
        wqk = wm * _dot_nt(q, k)
        num = a * _dot_nt(q, c_prev.astype(BF16)) + _dot(wqk.astype(BF16), v)
        den = a * jnp.sum(q.astype(F32) * n_prev, -1, keepdims=True) + jnp.sum(wqk, -1, keepdims=True)
        hh = num / jnp.maximum(jnp.abs(den), jnp.exp(-mt))
        b_end = b_c[ln - 1:ln, :]
        m_new = mt[ln - 1:ln, :]
        a_end = jnp.exp(b_end + m_prev - m_new)
        w_s = jnp.exp(b_end - b_c + ig_c - m_new)
        c_ref[0, h] = a_end * c_prev + _dot_tn((v.astype(F32) * w_s).astype(BF16), k)
        n_ref[0, h:h + 1, :] = a_end * n_prev + jnp.sum(w_s * k.astype(F32), axis=0, keepdims=True)
        m_ref[0, h:h + 1, :] = jnp.broadcast_to(m_new, (1, m_ref.shape[2]))
        hn = hh * lax.rsqrt(jnp.mean(hh * hh, -1, keepdims=True) + RMS_EPS)
        hn = hn * gain_ref[:, h * dv:(h + 1) * dv] * og_ref[:, h * dv:(h + 1) * dv]
        hn_ref[:, h * dv:(h + 1) * dv] = hn.astype(hn_ref.dtype)


def mlstm_prompt(q, k, v, og, gc, gr, gain, b, s):
    m, d = q.shape
    nh = MLSTM_HEADS
    dk = dv = d // nh
    ln = _row_tile(s, MLSTM_CHUNK)
    nch = s // ln
    row = lambda n: pl.BlockSpec((ln, n), lambda bi, ci: (bi * nch + ci, 0))
    return pl.pallas_call(
        functools.partial(_mlstm_prompt_kernel, nh=nh, dk=dk, dv=dv, ln=ln), grid=(b, nch),
        in_specs=[row(d), row(d), row(d), row(d), row(LANES),
                  pl.BlockSpec((SUBLANES, ln), lambda bi, ci: (0, bi * nch + ci)), _full((1, d))],
        out_specs=[row(d), pl.BlockSpec((1, nh, dv, dk), lambda bi, ci: (bi, 0, 0, 0)),
                   pl.BlockSpec((1, nh, dk), lambda bi, ci: (bi, 0, 0)),
                   pl.BlockSpec((1, nh, LANES), lambda bi, ci: (bi, 0, 0))],
        out_shape=[jax.ShapeDtypeStruct((m, d), BF16), jax.ShapeDtypeStruct((b, nh, dv, dk), F32),
                   jax.ShapeDtypeStruct((b, nh, dk), F32), jax.ShapeDtypeStruct((b, nh, LANES), F32)],
        compiler_params=_cparams(("parallel", "arbitrary"), VMEM_LIMIT), name="mlstm_prompt",
    )(q, k, v, og, gc, gr, gain.reshape(1, d))


def _mlstm_sample_kernel(q_ref, k_ref, v_ref, og_ref, g_ref, gain_ref, c_ref, n_ref, m_ref,
                         hn_ref, co_ref, no_ref, mo_ref, *, nh):
    row = lax.broadcasted_iota(jnp.int32, (SUBLANES, 1), 0)
    for h in range(nh):
        q = q_ref[0, h:h + 1, :]
        k = k_ref[0, h:h + 1, :]
        v = v_ref[0, h:h + 1, :].astype(F32)
        ig = g_ref[0, h:h + 1, 0:1]
        lf = g_ref[0, h:h + 1, 1:2]
        m_prev = m_ref[0, h:h + 1, :]
        c_prev = c_ref[0, h]
        n_prev = n_ref[0, h:h + 1, :]
        inter = lf + m_prev
        mt = jnp.maximum(inter, ig)
        wm = jnp.exp(ig - mt)
        a = jnp.exp(inter - mt)
        q8 = jnp.broadcast_to(q, (SUBLANES, q.shape[1]))
        cq = _dot_nt(q8, c_prev.astype(BF16))[0:1, :]
        wqk = wm * jnp.sum(q.astype(F32) * k.astype(F32), -1, keepdims=True)
        num = a * cq + wqk * v
        den = a * jnp.sum(n_prev * q.astype(F32), -1, keepdims=True) + wqk
        hh = num / jnp.maximum(jnp.abs(den), jnp.exp(-mt))
        v8 = jnp.where(row == 0, jnp.broadcast_to(v * wm, (SUBLANES, v.shape[1])), 0.0).astype(BF16)
        k8 = jnp.broadcast_to(k, (SUBLANES, k.shape[1]))
        co_ref[0, h] = a * c_prev + _dot_tn(v8, k8)
        no_ref[0, h:h + 1, :] = a * n_prev + wm * k.astype(F32)
        mo_ref[0, h:h + 1, :] = mt
        hn = hh * lax.rsqrt(jnp.mean(hh * hh, -1, keepdims=True) + RMS_EPS)
        hn_ref[0, h:h + 1, :] = (hn * gain_ref[h:h + 1, :] * og_ref[0, h:h + 1, :]).astype(hn_ref.dtype)


def mlstm_sample(q, k, v, og, gc, gain, c, n, m):
    bd, d = q.shape
    nh = MLSTM_HEADS
    dk = d // nh
    heads = lambda a: a.reshape(bd, nh, dk)
    g2 = jnp.transpose(gc[:, :2 * nh].reshape(bd, 2, nh), (0, 2, 1))
    blk3 = lambda n2: pl.BlockSpec((1, nh, n2), lambda bi: (bi, 0, 0))
    cspec = pl.BlockSpec((1, nh, dk, dk), lambda bi: (bi, 0, 0, 0))
    hn, co, no, mo = pl.pallas_call(
        functools.partial(_mlstm_sample_kernel, nh=nh), grid=(bd,),
        in_specs=[blk3(dk), blk3(dk), blk3(dk), blk3(dk), blk3(2), _full((nh, dk)), cspec, blk3(dk), blk3(1)],
        out_specs=[blk3(dk), cspec, blk3(dk), blk3(1)],
        out_shape=[jax.ShapeDtypeStruct((bd, nh, dk), BF16), jax.ShapeDtypeStruct(c.shape, F32),
                   jax.ShapeDtypeStruct(n.shape, F32), jax.ShapeDtypeStruct((bd, nh, 1), F32)],
        compiler_params=_cparams(("parallel",)), name="mlstm_sample",
    )(heads(q), heads(k), heads(v), heads(og), g2, gain.reshape(nh, dk), c, n, m.reshape(bd, nh, 1))
    return hn.reshape(bd, d), co, no, mo.reshape(bd, nh)


def _row_to_col(row):
    n = row.shape[1]
    eye = lax.broadcasted_iota(jnp.int32, (n, n), 0) == lax.broadcasted_iota(jnp.int32, (n, n), 1)
    return jnp.sum(jnp.where(eye, row, 0.0), axis=1, keepdims=True)


def _head_pad(q, keep):
    q2 = jnp.concatenate([q] * NSA_KV_HEADS, axis=1)
    row = lax.broadcasted_iota(jnp.int32, q2.shape, 0)
    lane = lax.broadcasted_iota(jnp.int32, q2.shape, 1)
    return jnp.where((lane // NSA_HD == row // NSA_GROUP) & keep(row), q2, jnp.zeros_like(q2))


def _nsa_sample_cmp_kernel(pt_ref, q_ref, pages_ref, pe_ref, w_ref, biasc_ref, oc_ref, idx_ref, xbuf, xrow, sem,
                           *, n_pages, nc, ncp, nsb, n_sel, past):
    b = pl.program_id(0)
    nb = pl.num_programs(0)
    kvd = NSA_KV_HEADS * NSA_HD

    def page_copy(bb, slot, p, sl):
        return pltpu.make_async_copy(pages_ref.at[pt_ref[bb * n_pages + p], pl.ds(sl * kvd, kvd), :],
                                     xbuf.at[slot, sl, p], sem.at[slot])

    def start_all(bb, slot):
        def body(p, c):
            page_copy(bb, slot, p, 0).start()
            page_copy(bb, slot, p, 1).start()
            return c
        lax.fori_loop(0, n_pages, body, 0)

    @pl.when(b == 0)
    def _():
        start_all(0, 0)

    @pl.when(b + 1 < nb)
    def _():
        start_all(b + 1, (b + 1) % 2)

    slot = b % 2

    def wait_body(p, c):
        page_copy(b, slot, p, 0).wait()
        page_copy(b, slot, p, 1).wait()
        return c
    lax.fori_loop(0, n_pages, wait_body, 0)

    def file_page(p, c):
        for sl in range(2):
            _file_rows(xbuf[slot, sl, p], xrow, sl, p)
        return c
    lax.fori_loop(0, n_pages, file_page, 0, unroll=8)

    acc = _compress_filed(xrow, pe_ref, w_ref, nc)
    kc = acc[:, 0:kvd].astype(BF16)
    vc = acc[:, kvd:2 * kvd].astype(BF16)
    q = q_ref[0]
    nh = q.shape[0]
    qpad = _head_pad(q, lambda r: r >= 0)
    s = _dot_nt(qpad, kc)
    s = s + biasc_ref[:, 0:nc]
    e = jnp.exp(s - jnp.max(s, -1, keepdims=True))
    p_c = e / jnp.sum(e, -1, keepdims=True)
    o = _dot(p_c.astype(BF16), vc)
    row = lax.broadcasted_iota(jnp.int32, (nh, NSA_HD), 0)
    o_h = o[:, 0:NSA_HD]
    for k in range(1, NSA_KV_HEADS):
        o_h = jnp.where(row // NSA_GROUP == k, o[:, k * NSA_HD:(k + 1) * NSA_HD], o_h)
    oc_ref[0] = o_h
    prow = lax.broadcasted_iota(jnp.int32, p_c.shape, 0)
    lane = lax.broadcasted_iota(jnp.int32, (1, ncp), 1)
    blk = lane // 2
    cur = past // SEL_BLOCK
    forced = (blk == 0) | (blk == cur) | (blk == cur - 1)
    is_cand = ((lane % 2) == 0) & (lane < 2 * nsb)
    nselp = idx_ref.shape[1]
    rsel = lax.broadcasted_iota(jnp.int32, (nselp, 1), 0).astype(F32)
    out_lane = lax.broadcasted_iota(jnp.int32, (nselp, LANES), 1)
    result = jnp.full((nselp, LANES), -1, jnp.int32)
    for k in range(NSA_KV_HEADS):
        imp = jnp.sum(jnp.where(prow // NSA_GROUP == k, p_c, 0.0), axis=0, keepdims=True)
        imp = jnp.concatenate([imp, jnp.zeros((1, ncp - nc), F32)], axis=1)
        imp = _pair_sum(imp)
        score = jnp.where(forced, FORCE_SCORE, jnp.where(blk <= cur, imp, -1.0))
        score = jnp.where(is_cand, score, -2.0)
        sel, rank = _select_blocks(score, n_sel, nsb)
        hit = (rank == rsel) & (sel > 0.5)
        idx = jnp.sum(jnp.where(hit, (blk + 1).astype(F32), 0.0), axis=1, keepdims=True) - 1.0
        result = jnp.where(out_lane == k, idx.astype(jnp.int32), result)
    idx_ref[0] = result


def nsa_sample_cmp(q8, pages, page_table, pe2, wbd, rel_bias):
    bd, nh, hd = q8.shape
    n_pages = page_table.shape[1]
    past = n_pages * PAGE_SIZE
    nc = past // CMP_BLOCK
    nsb = -(-(past + 1) // SEL_BLOCK)
    n_sel = min(TOP_N, nsb)
    ncp = _round_up(max(nc, SEL_RATIO * nsb), LANES)
    nselp = _round_up(n_sel, SUBLANES)
    kvd = wbd.shape[-1]
    cend = jnp.arange(nc) * CMP_BLOCK + (CMP_BLOCK - 1)
    biasc = jnp.zeros((nh, ncp), F32).at[:, :nc].set(_bias_of(rel_bias, past - cend).T)
    grid_spec = pltpu.PrefetchScalarGridSpec(
        num_scalar_prefetch=1, grid=(bd,),
        in_specs=[pl.BlockSpec((1, nh, hd), lambda bi, pt: (bi, 0, 0)),
                  pl.BlockSpec(memory_space=pl.ANY),
                  pl.BlockSpec(pe2.shape, lambda bi, pt: (0, 0, 0)),
                  pl.BlockSpec(wbd.shape, lambda bi, pt: (0, 0, 0, 0)),
                  pl.BlockSpec((nh, ncp), lambda bi, pt: (0, 0))],
        out_specs=[pl.BlockSpec((1, nh, hd), lambda bi, pt: (bi, 0, 0)),
                   pl.BlockSpec((1, nselp, LANES), lambda bi, pt: (bi, 0, 0))],
        scratch_shapes=[pltpu.VMEM((2, 2, n_pages, kvd, PAGE_SIZE), F32),
                        pltpu.VMEM((2, CMP_BLOCK // SUBLANES, nc * SUBLANES, kvd), F32),
                        pltpu.SemaphoreType.DMA((2,))])
    oc, idx = pl.pallas_call(
        functools.partial(_nsa_sample_cmp_kernel, n_pages=n_pages, nc=nc, ncp=ncp, nsb=nsb, n_sel=n_sel, past=past),
        grid_spec=grid_spec,
        out_shape=[jax.ShapeDtypeStruct((bd, nh, hd), F32), jax.ShapeDtypeStruct((bd, nselp, LANES), jnp.int32)],
        compiler_params=_cparams(("arbitrary",), VMEM_LIMIT), name="nsa_sample_cmp",
    )(page_table.reshape(-1), q8, pages, pe2, wbd, biasc)
    sel_idx = jnp.transpose(idx[:, :n_sel, :NSA_KV_HEADS], (0, 2, 1))
    return oc, sel_idx


def _nsa_sample_att_kernel(pt_ref, si_ref, q_ref, g_ref, oc_ref, kvn_ref, wn_ref, wc_ref, pages_ref,
                           bsel_ref, bwin_ref, ob_ref, win_ref, selbuf, wall, sem,
                           *, n_pages, n_sel, past, wb):
    b = pl.program_id(0)
    nb = pl.num_programs(0)
    kvd = NSA_KV_HEADS * NSA_HD
    hd = NSA_HD
    n_blk_pages = past // SEL_BLOCK
    per_page = PAGE_SIZE // SEL_BLOCK
    n_slots = NSA_KV_HEADS * n_sel

    def blk_of(bb, j):
        return si_ref[bb * n_slots + j]

    def blk_copy(bb, slot, j):
        blk = jnp.clip(blk_of(bb, j), 0, n_blk_pages - 1)
        page = pt_ref[bb * n_pages + blk // per_page]
        return pltpu.make_async_copy(pages_ref.at[page, pl.ds(2 * kvd, 2 * kvd), :], selbuf.at[slot, j], sem.at[slot])

    def in_pages(bb, j):
        blk = blk_of(bb, j)
        return (blk >= 0) & (blk < n_blk_pages)

    def start_all(bb, slot):
        def body(j, c):
            @pl.when(in_pages(bb, j))
            def _():
                blk_copy(bb, slot, j).start()
            return c
        lax.fori_loop(0, n_slots, body, 0)

    @pl.when(b == 0)
    def _():
        start_all(0, 0)

    @pl.when(b + 1 < nb)
    def _():
        start_all(b + 1, (b + 1) % 2)

    slot = b % 2
    new_sel = _row_to_col(kvn_ref[0][:, 2 * kvd:4 * kvd])
    lane = lax.broadcasted_iota(jnp.int32, (1, PAGE_SIZE), 1)

    def wait_body(j, c):
        @pl.when(in_pages(b, j))
        def _():
            blk_copy(b, slot, j).wait()

        @pl.when(jnp.logical_not(in_pages(b, j)))
        def _():
            is_new = blk_of(b, j) == n_blk_pages
            selbuf[slot, j] = jnp.where((lane == 0) & is_new, new_sel, 0.0)
        return c
    lax.fori_loop(0, n_slots, wait_body, 0)

    q = q_ref[0]
    nh = q.shape[0]
    gates = g_ref[0]
    head = lax.broadcasted_iota(jnp.int32, (nh, 1), 0)

    def attend(qp, keys_t, vals_t, bias, valid):
        s = jnp.where(valid, _dot(qp, keys_t) + bias, NEG)
        e = jnp.where(valid, jnp.exp(s - jnp.max(s, -1, keepdims=True)), 0.0)
        den = jnp.sum(e, -1, keepdims=True)
        p = e / jnp.where(den > 0, den, 1.0)
        return _dot_nt(p.astype(BF16), vals_t)

    o_s = jnp.zeros((nh, kvd), F32)
    for k in range(NSA_KV_HEADS):
        blks = [blk_of(b, k * n_sel + r) for r in range(n_sel)]
        tiles = [selbuf[slot, k * n_sel + r] for r in range(n_sel)]
        keys_t = jnp.concatenate([t_[0:kvd, :] for t_ in tiles], axis=1).astype(BF16)
        vals_t = jnp.concatenate([t_[kvd:2 * kvd, :] for t_ in tiles], axis=1).astype(BF16)
        bias = jnp.concatenate([bsel_ref[jnp.clip(bl // per_page, 0, n_pages)] for bl in blks], axis=1)
        valid = jnp.concatenate(
            [(lane // SEL_BLOCK == bl % per_page) & ((bl // per_page) * PAGE_SIZE + lane <= past) & (bl >= 0)
             for bl in blks], axis=1)
        o_k = attend(_head_pad(q, lambda r: r // NSA_GROUP == k), keys_t, vals_t, bias, valid)
        o_s = jnp.where(head // NSA_GROUP == k, o_k, o_s)
    wlanes = wall.shape[1]
    wall[:, 0:wb] = wc_ref[0]
    tail = lax.broadcasted_iota(jnp.int32, (1, wlanes - wb), 1)
    wall[:, wb:wlanes] = jnp.where(tail == 0, _row_to_col(wn_ref[0]), 0.0)
    win_ref[0] = pltpu.roll(wall[...], wlanes - 1, 1)[:, 0:wb]
    w_pos = lax.broadcasted_iota(jnp.int32, (1, wlanes), 1)
    valid_w = (w_pos <= wb) & (wb - w_pos < WINDOW) & (past - wb + w_pos >= 0)
    o_w = attend(_head_pad(q, lambda r: r >= 0), wall[0:kvd, :].astype(BF16), wall[kvd:2 * kvd, :].astype(BF16),
                 bwin_ref[...], valid_w)
    o_c = jnp.concatenate([oc_ref[0]] * NSA_KV_HEADS, axis=1)
    mix = gates[:, 0:1] * o_c + gates[:, 1:2] * o_s + gates[:, 2:3] * o_w
    out = mix[:, 0:hd]
    for k in range(1, NSA_KV_HEADS):
        out = jnp.where(head // NSA_GROUP == k, mix[:, k * hd:(k + 1) * hd], out)
    ob_ref[0] = out.astype(ob_ref.dtype)


def nsa_sample_att(q8, gates, oc, kv03, kv45, wcache_t, layer, pages_t, page_table, sel_idx, rel_bias):
    bd, nh, hd = q8.shape
    n_pages = page_table.shape[1]
    past = n_pages * PAGE_SIZE
    wb = wcache_t.shape[2]
    kvd = NSA_KV_HEADS * hd
    n_sel = sel_idx.shape[2]
    wlanes = _round_up(wb + 1, LANES)
    g3 = gates[:, :3 * nh].reshape(bd, nh, 3)
    dist = past - (jnp.arange(n_pages + 1)[:, None] * PAGE_SIZE + jnp.arange(PAGE_SIZE)[None, :])
    bsel = jnp.transpose(_bias_of(rel_bias, dist), (0, 2, 1))
    bwin = _bias_of(rel_bias, wb - jnp.arange(wlanes)).T
    blk = lambda n2, n3: pl.BlockSpec((1, n2, n3), lambda bi, pt, si: (bi, 0, 0))
    grid_spec = pltpu.PrefetchScalarGridSpec(
        num_scalar_prefetch=2, grid=(bd,),
        in_specs=[blk(nh, hd), blk(nh, 3), blk(nh, hd), blk(1, 4 * kvd), blk(1, 2 * kvd),
                  pl.BlockSpec((1, 2 * kvd, wb), lambda bi, pt, si: (layer * bd + bi, 0, 0)),
                  pl.BlockSpec(memory_space=pl.ANY),
                  pl.BlockSpec(bsel.shape, lambda bi, pt, si: (0, 0, 0)),
                  pl.BlockSpec(bwin.shape, lambda bi, pt, si: (0, 0))],
        out_specs=[blk(nh, hd), blk(2 * kvd, wb)],
        scratch_shapes=[pltpu.VMEM((2, NSA_KV_HEADS * n_sel, 2 * kvd, PAGE_SIZE), F32),
                        pltpu.VMEM((2 * kvd, wlanes), F32), pltpu.SemaphoreType.DMA((2,))])
    ob, win = pl.pallas_call(
        functools.partial(_nsa_sample_att_kernel, n_pages=n_pages, n_sel=n_sel, past=past, wb=wb),
        grid_spec=grid_spec,
        out_shape=[jax.ShapeDtypeStruct((bd, nh, hd), BF16), jax.ShapeDtypeStruct((bd, 2 * kvd, wb), F32)],
        compiler_params=_cparams(("arbitrary",), VMEM_LIMIT), name="nsa_sample_att",
    )(page_table.reshape(-1), sel_idx.reshape(-1), q8, g3, oc, kv03.reshape(bd, 1, -1), kv45.reshape(bd, 1, -1),
      wcache_t, pages_t, bsel, bwin)
    return ob.reshape(bd, nh * hd), win


def kernel(x_prompt, x_sample, mem_prompt, cache_conv, cache_nsa_pages, cache_nsa_window, state_mlstm_c,
           state_mlstm_n, state_mlstm_m, cache_mem_kv, page_table, rel_bias, norm_mix, norm_xattn, norm_mem,
           norm_ffn, norm_final, w_in_even, w_out_even, conv_w, conv_b, conv_ln_g, conv_ln_b, nsa_cmp_pe,
           nsa_cmp_w, w_in_odd, mlstm_b_i, mlstm_b_f, mlstm_norm, w_out_odd, xattn_wq, xattn_wkv, xattn_wo,
           ffn_w_gu, ffn_w_dn, router_w, router_b, expert_w_gu, expert_w_dn):
    b, s, d = x_prompt.shape
    bd, td, _ = x_sample.shape
    assert td == 1, "the sample group decodes one token per sequence"
    depth = norm_mix.shape[0]
    mt = mem_prompt.shape[1]
    cc = conv_w.shape[2]
    hist = conv_w.shape[1] - 1
    wb = cache_nsa_window.shape[2]
    kvh, hd = NSA_KV_HEADS, NSA_HD
    n_pool = cache_nsa_pages.shape[1]
    assert s >= hist and s >= wb and s % Q_BLOCK == 0
    xp = x_prompt.reshape(b * s, d)
    xs = x_sample.reshape(bd, d)
    mem = mem_prompt.reshape(b * mt, d)
    pages_t = jnp.swapaxes(cache_nsa_pages.reshape(-1, PAGE_SIZE, 4 * kvh * hd), 1, 2)
    window_t = jnp.swapaxes(cache_nsa_window.reshape(-1, wb, 2 * kvh * hd), 1, 2)
    xhd = d // X_HEADS
    memkv_rows = jnp.swapaxes(cache_mem_kv.reshape(depth * bd, mt, 2, X_HEADS, xhd // LANES, LANES), 3, 4)
    memkv_rows = memkv_rows.reshape(depth * bd, -1, LANES)
    bf = lambda a: a.astype(BF16)
    conv_p, conv_s, nsa_p, nsa_s, win_p, win_s = [], [], [], [], [], []
    mc_p, mc_s, mn_p, mn_s, mm_p, mm_s, memkv_p = [], [], [], [], [], [], []
    for l in range(depth):
        li = l // 2
        if l % 2 == 0:
            prm = prep_even(w_in_even[li], nsa_cmp_pe[li], nsa_cmp_w[li])
            w_out = bf(w_out_even[li])
            w_parts = [w_out[:cc], w_out[cc:]]
            conv_args = (conv_w[li], conv_b[li], conv_ln_g[li], conv_ln_b[li])
            glu, keys, qt, vt, gt, kv_t = inproj_even(xp, norm_mix[l], prm['w_in'], cc, s)
            a_out = conv_prompt(glu, *conv_args, b, s)
            kc, vct = compress_prompt(kv_t, prm['pe2'], prm['wbd'], _round_up(s // CMP_BLOCK, LANES))
            b_out = nsa_prompt(qt, gt, kc, vct, keys, vt, rel_bias, b, s)
            mix_p = ([a_out, b_out], w_parts)
            conv_p.append(glu.reshape(b, s, cc)[:, s - hist:])
            rows_t = kv_t.reshape(b, 6, kvh, hd, s)
            nsa_p.append(jnp.transpose(rows_t[:, :4], (0, 4, 1, 2, 3)))
            win_p.append(jnp.transpose(rows_t[:, 4:, :, :, s - wb:], (0, 4, 1, 2, 3)))
            glu, kv03, kv45, q, gates = inproj_even(xs, norm_mix[l], prm['w_in'], cc)
            a_out, conv_state = conv_sample(cache_conv[li], glu, *conv_args)
            q8 = q.reshape(bd, NSA_HEADS, hd)
            pt = page_table + li * n_pool
            o_c, sel_idx = nsa_sample_cmp(q8, pages_t, pt, prm['pe2'], prm['wbd'], rel_bias)
            b_out, win = nsa_sample_att(q8, gates, o_c, kv03, kv45, window_t, li, pages_t, pt, sel_idx, rel_bias)
            xs = outproj(xs, [a_out, b_out], w_parts)
            conv_s.append(conv_state)
            nsa_s.append(kv03.reshape(bd, 1, 4, kvh, hd))
            win_s.append(jnp.transpose(win.reshape(bd, 2, kvh, hd, wb), (0, 4, 1, 2, 3)))
        else:
            prm = prep_odd(w_in_odd[li], mlstm_b_i[li], mlstm_b_f[li])
            w_out = bf(w_out_odd[li])
            q, k, v, og, gc, gr = inproj_odd(xp, norm_mix[l], prm['w_in'], prm['bias'])
            hn, c_new, n_new, m_new = mlstm_prompt(q, k, v, og, gc, gr, mlstm_norm[li], b, s)
            mix_p = ([hn], [w_out])
            mc_p.append(c_new)
            mn_p.append(n_new)
            mm_p.append(m_new[:, :, 0])
            q, k, v, og, gc, gr = inproj_odd(xs, norm_mix[l], prm['w_in'], prm['bias'])
            hn, c_new, n_new, m_new = mlstm_sample(q, k, v, og, gc, mlstm_norm[li], state_mlstm_c[li],
                                                   state_mlstm_n[li], state_mlstm_m[li])
            xs = outproj(xs, [hn], [w_out])
            mc_s.append(c_new)
            mn_s.append(n_new)
            mm_s.append(m_new)
        wq, wo = bf(xattn_wq[l]), bf(xattn_wo[l])
        mkv_rows, mkv_b = memkv(mem, norm_mem[l], bf(xattn_wkv[l]), X_HEADS)
        mkv = jnp.swapaxes(mkv_rows.reshape(b, mt, 2, xhd // LANES, X_HEADS, LANES), 3, 4)
        memkv_p.append(mkv.reshape(b, mt, 2, X_HEADS, xhd))
        xs = xattn_sample(xs, norm_xattn[l], wq, memkv_rows, wo, l, mt)
        if l % 2 == 0:
            w_gu, w_dn = bf(ffn_w_gu[li]), bf(ffn_w_dn[li])
            xp = xattn_ffn(xp, norm_xattn[l], wq, mkv_b, wo, *mix_p, norm_ffn[l], w_gu, w_dn, b, s)
            xs = ffn(xs, norm_ffn[l], w_gu, w_dn)
        else:
            xp = xattn_prompt(xp, norm_xattn[l], wq, mkv_b, wo, b, s, *mix_p)
            e_gu, e_dn = bf(expert_w_gu[li]), bf(expert_w_dn[li])
            final_g = norm_final if l == depth - 1 else None
            comb, h, mask, counts = router(xp, norm_ffn[l], router_w[li], router_b[li])
            xp = moe_grouped(xp, h, comb, mask, counts, e_gu, e_dn, final_g)
            comb, h, _, _ = router(xs, norm_ffn[l], router_w[li], router_b[li])
            xs = moe(xs, h, comb, e_gu, e_dn, final_g)
    if depth % 2:
        xp, xs = rmsnorm(xp, norm_final), rmsnorm(xs, norm_final)
    y_prompt = xp.reshape(b, s, d)
    y_sample = xs.reshape(bd, 1, d)
    return (y_prompt, y_sample, jnp.stack(conv_p), jnp.stack(conv_s), jnp.stack(nsa_p), jnp.stack(nsa_s),
            jnp.stack(win_p), jnp.stack(win_s), jnp.stack(mc_p), jnp.stack(mc_s), jnp.stack(mn_p),
            jnp.stack(mn_s), jnp.stack(mm_p), jnp.stack(mm_s), jnp.stack(memkv_p))
```

```python
import functools
import math

import jax
import jax.numpy as jnp
from jax import lax
from jax.experimental import pallas as pl
from jax.experimental.pallas import tpu as pltpu

F32 = jnp.float32
BF16 = jnp.bfloat16

PAGE_SIZE = 128
CONV_WIDTH = 31
NSA_HEADS = 8
NSA_KV_HEADS = 2
NSA_GROUP = NSA_HEADS // NSA_KV_HEADS
NSA_HD = 64
CMP_BLOCK = 32
SEL_BLOCK = 64
SEL_RATIO = SEL_BLOCK // CMP_BLOCK
TOP_N = 16
WINDOW = 512
Q_BLOCK = 128
FORCE_SCORE = 1.0e4
NUM_BUCKETS = 32
MAX_DISTANCE = 1024
MLSTM_HEADS = 4
X_HEADS = 4
TOP_K = 2
RMS_EPS = 1e-6
LN_EPS = 1e-5
NEG = -1e30

LANES = 128
SUBLANES = 8
VMEM_LIMIT = 56 * 1024 * 1024
MLSTM_CHUNK = 256
SEL_TILES = 4
WIN_TILES = 5


def _cparams(sem, vmem=None):
    return pltpu.CompilerParams(dimension_semantics=sem, vmem_limit_bytes=vmem)


def _rms(x, g):
    return x * lax.rsqrt(jnp.mean(x * x, -1, keepdims=True) + RMS_EPS) * g


def _dot(a, b):
    return jnp.dot(a, b, preferred_element_type=F32)


def _dot_nt(a, b):
    return lax.dot_general(a, b, (((1,), (1,)), ((), ())), preferred_element_type=F32)


def _dot_tn(a, b):
    return lax.dot_general(a, b, (((0,), (0,)), ((), ())), preferred_element_type=F32)


def _full(shape):
    n = len(shape)
    return pl.BlockSpec(shape, lambda *_: (0,) * n)


def _row_tile(m, pref):
    t = min(pref, m)
    while m % t:
        t //= 2
    return t


def _rmsnorm_kernel(x_ref, g_ref, o_ref):
    o_ref[...] = _rms(x_ref[...], g_ref[...])


def rmsnorm(x, g):
    m, d = x.shape
    tm = _row_tile(m, 1024)
    return pl.pallas_call(
        _rmsnorm_kernel, grid=(m // tm,),
        in_specs=[pl.BlockSpec((tm, d), lambda i: (i, 0)), _full((1, d))],
        out_specs=pl.BlockSpec((tm, d), lambda i: (i, 0)),
        out_shape=jax.ShapeDtypeStruct((m, d), F32),
        compiler_params=_cparams(("parallel",)), name="rmsnorm",
    )(x, g.reshape(1, d))


def _outproj_kernel(*refs, n_in):
    x_ref = refs[0]
    a_refs = refs[1:1 + n_in]
    w_refs = refs[1 + n_in:1 + 2 * n_in]
    o_ref = refs[1 + 2 * n_in]
    acc = x_ref[...]
    for a_ref, w_ref in zip(a_refs, w_refs):
        acc = acc + _dot(a_ref[...], w_ref[...])
    o_ref[...] = acc


def outproj(x, acts, ws):
    m, d = x.shape
    tm = _row_tile(m, 512)
    n_in = len(acts)
    in_specs = [pl.BlockSpec((tm, d), lambda i: (i, 0))]
    in_specs += [pl.BlockSpec((tm, a.shape[1]), lambda i: (i, 0)) for a in acts]
    in_specs += [_full(w.shape) for w in ws]
    return pl.pallas_call(
        functools.partial(_outproj_kernel, n_in=n_in), grid=(m // tm,),
        in_specs=in_specs, out_specs=pl.BlockSpec((tm, d), lambda i: (i, 0)),
        out_shape=jax.ShapeDtypeStruct((m, d), F32),
        compiler_params=_cparams(("parallel",)), name="outproj",
    )(x, *acts, *ws)


def _inproj_even_kernel(x_ref, g_ref, w_ref, glu_ref, *rest, cc, qd, kvd, tiles):
    xn = _rms(x_ref[...], g_ref[...]).astype(BF16)

    def mm(lo, hi):
        return _dot(xn, w_ref[:, lo:hi])

    o = 0
    a = mm(o, o + cc)
    b = mm(o + cc, o + 2 * cc)
    glu_ref[...] = a * jax.nn.sigmoid(b)
    o += 2 * cc
    q = mm(o, o + qd) * (NSA_HD ** -0.5)
    o += qd
    kv03 = mm(o, o + 4 * kvd)
    o += 4 * kvd
    kv45 = mm(o, o + 2 * kvd)
    o += 2 * kvd
    gates = jax.nn.sigmoid(mm(o, o + LANES))
    if tiles == 0:
        kv03_ref, kv45_ref, q_ref, gate_ref = rest
        kv03_ref[...] = kv03
        kv45_ref[...] = kv45
        q_ref[...] = q.astype(BF16)
        gate_ref[...] = gates
        return
    keys_ref, qt_ref, vt_ref, gt_ref, kvt_ref = rest
    kvt_ref[0] = jnp.concatenate([kv03, kv45], axis=1).T
    keys_ref[...] = jnp.concatenate([kv03[:, 2 * kvd:3 * kvd], kv45[:, 0:kvd]], axis=1).astype(BF16)
    vals = jnp.concatenate([kv03[:, 3 * kvd:4 * kvd], kv45[:, kvd:2 * kvd]], axis=1)
    for j in range(tiles):
        rows = slice(j * Q_BLOCK, (j + 1) * Q_BLOCK)
        qt_ref[j] = q[rows, :].T.astype(BF16)
        vt_ref[j] = vals[rows, :].T.astype(BF16)
        gt_ref[j] = gates[rows, :].T


def inproj_even(x, g, w_pad, cc, seq=None):
    m, d = x.shape
    qd = NSA_HEADS * NSA_HD
    kvd = NSA_KV_HEADS * NSA_HD
    tm = _row_tile(m, 256)
    row = lambda n: pl.BlockSpec((tm, n), lambda i: (i, 0))
    out_specs = [row(cc)]
    out_shape = [jax.ShapeDtypeStruct((m, cc), F32)]
    transposed = seq is not None
    tiles = tm // Q_BLOCK if transposed else 0
    if transposed:
        assert tm % Q_BLOCK == 0 and seq % tm == 0
        per_seq = seq // tm
        tile = lambda n: pl.BlockSpec((tiles, n, Q_BLOCK), lambda i: (i, 0, 0))
        out_specs += [row(2 * kvd), tile(qd), tile(2 * kvd), tile(LANES),
                      pl.BlockSpec((1, 6 * kvd, tm), lambda i: (i // per_seq, 0, i % per_seq))]
        out_shape += [jax.ShapeDtypeStruct((m, 2 * kvd), BF16),
                      jax.ShapeDtypeStruct((m // Q_BLOCK, qd, Q_BLOCK), BF16),
                      jax.ShapeDtypeStruct((m // Q_BLOCK, 2 * kvd, Q_BLOCK), BF16),
                      jax.ShapeDtypeStruct((m // Q_BLOCK, LANES, Q_BLOCK), F32),
                      jax.ShapeDtypeStruct((m // seq, 6 * kvd, seq), F32)]
    else:
        out_specs += [row(4 * kvd), row(2 * kvd), row(qd), row(LANES)]
        out_shape += [jax.ShapeDtypeStruct((m, 4 * kvd), F32), jax.ShapeDtypeStruct((m, 2 * kvd), F32),
                      jax.ShapeDtypeStruct((m, qd), BF16), jax.ShapeDtypeStruct((m, LANES), F32)]
    return pl.pallas_call(
        functools.partial(_inproj_even_kernel, cc=cc, qd=qd, kvd=kvd, tiles=tiles), grid=(m // tm,),
        in_specs=[row(d), _full((1, d)), _full(w_pad.shape)],
        out_specs=out_specs, out_shape=out_shape,
        compiler_params=_cparams(("parallel",)), name="inproj_even",
    )(x, g.reshape(1, d), w_pad)


def _conv_post(y, lg, lb):
    mu = jnp.mean(y, -1, keepdims=True)
    var = jnp.mean(jnp.square(y - mu), -1, keepdims=True)
    yn = (y - mu) * lax.rsqrt(var + LN_EPS) * lg + lb
    return yn * jax.nn.sigmoid(yn)


CONV_SUB = 64
CONV_PAD = 32


def _conv_prompt_kernel(glu_ref, cw_ref, cb_ref, lg_ref, lb_ref, o_ref, ext_ref, y_ref, *, ts, s):
    i = pl.program_id(1)
    c = glu_ref.shape[-1]

    @pl.when(i == 0)
    def _():
        ext_ref[0:CONV_PAD, :] = jnp.zeros((CONV_PAD, c), F32)
        ext_ref[CONV_PAD:CONV_PAD + s, :] = glu_ref[0]
        ext_ref[CONV_PAD + s:CONV_PAD + s + SUBLANES, :] = jnp.zeros((SUBLANES, c), F32)

    lead = CONV_PAD - (CONV_WIDTH - 1)
    span = CONV_SUB + CONV_PAD

    def sub(j, carry):
        r0 = pl.multiple_of(i * ts + j * CONV_SUB, CONV_SUB)
        for c0 in range(0, c, LANES):
            xw = ext_ref[pl.ds(r0, span + SUBLANES), c0:c0 + LANES]
            acc = jnp.zeros((CONV_SUB, LANES), F32) + cb_ref[:, c0:c0 + LANES]
            for r in range(SUBLANES):
                xr = xw if r == 0 else pltpu.roll(xw, span + SUBLANES - r, 0)
                for a in range(span // SUBLANES):
                    w = SUBLANES * a + r - lead
                    if 0 <= w < CONV_WIDTH:
                        acc = acc + xr[SUBLANES * a:SUBLANES * a + CONV_SUB, :] * cw_ref[w:w + 1, c0:c0 + LANES]
            y_ref[:, c0:c0 + LANES] = acc
        o_ref[0, pl.ds(pl.multiple_of(j * CONV_SUB, CONV_SUB), CONV_SUB), :] = _conv_post(
            y_ref[...], lg_ref[...], lb_ref[...]).astype(o_ref.dtype)
        return carry

    lax.fori_loop(0, ts // CONV_SUB, sub, 0)


def conv_prompt(glu, cw, cb, lg, lb, b, s):
    c = glu.shape[-1]
    ts = _row_tile(s, 256)
    vec = lambda a: a.reshape(1, c)
    out = pl.pallas_call(
        functools.partial(_conv_prompt_kernel, ts=ts, s=s), grid=(b, s // ts),
        in_specs=[pl.BlockSpec((1, s, c), lambda bi, i: (bi, 0, 0)), _full((CONV_WIDTH, c)),
                  _full((1, c)), _full((1, c)), _full((1, c))],
        out_specs=pl.BlockSpec((1, ts, c), lambda bi, i: (bi, i, 0)),
        out_shape=jax.ShapeDtypeStruct((b, s, c), BF16),
        scratch_shapes=[pltpu.VMEM((CONV_PAD + s + SUBLANES, c), F32), pltpu.VMEM((CONV_SUB, c), F32)],
        compiler_params=_cparams(("parallel", "arbitrary")), name="conv_prompt",
    )(glu.reshape(b, s, c), cw, vec(cb), vec(lg), vec(lb))
    return out.reshape(b * s, c)


def _conv_sample_kernel(cache_ref, glu_ref, cw_ref, cb_ref, lg_ref, lb_ref, o_ref, st_ref):
    hist = CONV_WIDTH - 1
    cache = cache_ref[...]
    glu = glu_ref[...]
    y = jnp.sum(cache * cw_ref[0:hist, :][None], axis=1) + glu * cw_ref[hist:hist + 1, :] + cb_ref[...]
    o_ref[...] = _conv_post(y, lg_ref[...], lb_ref[...]).astype(o_ref.dtype)
    st_ref[:, 0:hist - 1, :] = cache[:, 1:hist, :]
    st_ref[:, hist - 1:hist, :] = glu[:, None, :]


def conv_sample(cache, glu, cw, cb, lg, lb):
    bd, hist, c = cache.shape
    vec = lambda a: a.reshape(1, c)
    return pl.pallas_call(
        _conv_sample_kernel,
        out_shape=[jax.ShapeDtypeStruct((bd, c), BF16), jax.ShapeDtypeStruct((bd, hist, c), F32)],
        name="conv_sample",
    )(cache, glu, cw, vec(cb), vec(lg), vec(lb))


def _rel_bucket(dist):
    n = jnp.maximum(dist, 0)
    max_exact = NUM_BUCKETS // 2
    nf = jnp.maximum(n, 1).astype(F32)
    large = max_exact + (jnp.log(nf / max_exact) / math.log(MAX_DISTANCE / max_exact)
                         * (NUM_BUCKETS - max_exact)).astype(jnp.int32)
    large = jnp.minimum(large, NUM_BUCKETS - 1)
    return jnp.where(n < max_exact, n, large)


def _bias_of(rel_bias, dist):
    bucket = _rel_bucket(dist)[..., None]
    out = jnp.zeros(bucket.shape[:-1] + (rel_bias.shape[1],), F32)
    for k in range(NUM_BUCKETS):
        out = jnp.where(bucket == k, rel_bias[k].astype(F32), out)
    return out


def _compress_accumulate(load_rows, pe_ref, w_ref, nc):
    accs = []
    half = CMP_BLOCK // 2
    for slot in range(2):
        acc = jnp.zeros((nc, w_ref.shape[-1]), F32)
        for j in range(half):
            pair = [load_rows(slot, jj) + pe_ref[slot, jj:jj + 1, :] for jj in (j, j + half)]
            acc = acc + _dot(jnp.concatenate(pair, axis=1).astype(BF16), w_ref[slot, j])
        accs.append(acc)
    return jnp.concatenate(accs, axis=1)


def _file_rows(tile_t, xrow, sl, page):
    rows = tile_t.T
    per_page = PAGE_SIZE // CMP_BLOCK
    for cl in range(per_page):
        for a in range(CMP_BLOCK // SUBLANES):
            r0 = cl * CMP_BLOCK + a * SUBLANES
            dst = pl.multiple_of((page * per_page + cl) * SUBLANES, SUBLANES)
            xrow[sl, a, pl.ds(dst, SUBLANES), :] = rows[r0:r0 + SUBLANES, :]


def _compress_filed(xrow, pe_ref, w_ref, nc):
    return _compress_accumulate(
        lambda sl, j: xrow[sl, j // SUBLANES, pl.ds(j % SUBLANES, nc, stride=SUBLANES), :], pe_ref, w_ref, nc)


def _compress_prompt_kernel(x_ref, pe_ref, w_ref, kc_ref, vct_ref, xrow, *, nc, ncp, kvd):
    for p in range(x_ref.shape[2] // PAGE_SIZE):
        for sl in range(2):
            _file_rows(x_ref[0, sl * kvd:(sl + 1) * kvd, p * PAGE_SIZE:(p + 1) * PAGE_SIZE], xrow, sl, p)
    acc = _compress_filed(xrow, pe_ref, w_ref, nc)
    if ncp > nc:
        acc = jnp.concatenate([acc, jnp.zeros((ncp - nc, 2 * kvd), F32)], axis=0)
    kc_ref[0] = acc[:, 0:kvd].astype(BF16)
    vct_ref[0] = acc[:, kvd:2 * kvd].T.astype(BF16)


def compress_prompt(kv_t, pe2, wbd, ncp):
    b, _, s = kv_t.shape
    assert s % PAGE_SIZE == 0
    nc = s // CMP_BLOCK
    kvd = wbd.shape[-1]
    return pl.pallas_call(
        functools.partial(_compress_prompt_kernel, nc=nc, ncp=ncp, kvd=kvd), grid=(b,),
        in_specs=[pl.BlockSpec((1, 2 * kvd, s), lambda bi: (bi, 0, 0)), _full(pe2.shape), _full(wbd.shape)],
        out_specs=[pl.BlockSpec((1, ncp, kvd), lambda bi: (bi, 0, 0)), pl.BlockSpec((1, kvd, ncp), lambda bi: (bi, 0, 0))],
        out_shape=[jax.ShapeDtypeStruct((b, ncp, kvd), BF16), jax.ShapeDtypeStruct((b, kvd, ncp), BF16)],
        scratch_shapes=[pltpu.VMEM((2, CMP_BLOCK // SUBLANES, nc * SUBLANES, kvd), F32)],
        compiler_params=_cparams(("parallel",)), name="compress_prompt",
    )(kv_t, pe2, wbd)


def _select_blocks(score, n_sel, n_cand):
    n = score.shape[1]
    col = jnp.broadcast_to(score, (LANES, n)).T[:, 0:1]
    i = lax.broadcasted_iota(jnp.int32, (n, 1), 0)
    lane = lax.broadcasted_iota(jnp.int32, (1, n), 1)
    beats = ((i % 2) == 0) & (i < 2 * n_cand) & ((col > score) | ((col == score) & (i < lane)))
    rank = jnp.sum(jnp.where(beats, 1.0, 0.0), axis=0, keepdims=True)
    is_cand = ((lane % 2) == 0) & (lane < 2 * n_cand)
    return (is_cand & (rank < n_sel) & (score >= 0)).astype(F32), rank


def _pair_sum(imp):
    n = imp.shape[1]
    return imp + pltpu.roll(imp, n - 1, 1)


def _rank_rows(score, n_sel, n_cand):
    blk = lax.broadcasted_iota(jnp.int32, score.shape, 0)
    rank = jnp.zeros(score.shape, F32)
    for i in range(n_cand):
        row = score[i:i + 1, :]
        beats = (row > score) | ((row == score) & (blk > i))
        rank = rank + beats.astype(F32)
    return ((rank < n_sel) & (score >= 0)).astype(F32)


def _nsa_prompt_kernel(qt_ref, gt_ref, kc_ref, vct_ref, keys_ref, vt_ref, biasc_ref, btile_ref, o_ref,
                       qt_scr, oc_scr, acc_s, acc_w, imp_scr, sel_scr, out_scr, *, nc, nsb, n_sel):
    qi = pl.program_id(1)
    g, hd, kvh, qb = NSA_GROUP, NSA_HD, NSA_KV_HEADS, Q_BLOCK
    kvd = kvh * hd
    ncp = kc_ref.shape[1]
    nsbp = sel_scr.shape[1]
    q_pos = qi * qb + lax.broadcasted_iota(jnp.int32, (1, qb), 1)
    key_row = lax.broadcasted_iota(jnp.int32, (qb, 1), 0)
    c_row = lax.broadcasted_iota(jnp.int32, (ncp, 1), 0)
    mask_c = (q_pos >= c_row * CMP_BLOCK + (CMP_BLOCK - 1)) & (c_row < nc)
    blk = lax.broadcasted_iota(jnp.int32, (nsbp, 1), 0)
    cur = q_pos // SEL_BLOCK
    forced = (blk == 0) | (blk == cur) | (blk == cur - 1)
    zeros = jnp.zeros((hd, qb), BF16)
    for k in range(kvh):
        for gi in range(g):
            h = k * g + gi
            parts = [zeros] * kvh
            parts[k] = qt_ref[0, h * hd:(h + 1) * hd, :]
            qt_scr[k, :, gi * qb:(gi + 1) * qb] = jnp.concatenate(parts, axis=0)
        s_c = _dot(kc_ref[0], qt_scr[k])
        imp = jnp.zeros((ncp, qb), F32)
        probs = []
        for gi in range(g):
            s = jnp.where(mask_c, s_c[:, gi * qb:(gi + 1) * qb] + biasc_ref[k * g + gi], NEG)
            e = jnp.where(mask_c, jnp.exp(s - jnp.max(s, 0, keepdims=True)), 0.0)
            den = jnp.sum(e, 0, keepdims=True)
            p = e / jnp.where(den > 0, den, 1.0)
            imp = imp + p
            probs.append(p.astype(BF16))
        oc_scr[k] = _dot(vct_ref[0, k * hd:(k + 1) * hd, :], jnp.concatenate(probs, axis=1))
        imp_scr[...] = imp + pltpu.roll(imp, ncp - 1, 0)
        cand = imp_scr[pl.ds(0, nsbp, stride=SEL_RATIO), :]
        score = jnp.where(forced, FORCE_SCORE, jnp.where(blk <= cur, cand, -1.0))
        sel_scr[k] = _rank_rows(jnp.where(blk < nsb, score, -2.0), n_sel, nsb)

    per_tile = qb // SEL_BLOCK
    n_tiles = keys_ref.shape[1] // qb
    first = ([jnp.full((1, qb), NEG, F32)] * g, [jnp.zeros((1, qb), F32)] * g)

    def tile_step(tiles, carry, key_col, val_row, acc_ref, window):
        kts = [jnp.clip(kt, 0, n_tiles - 1) for kt, _ in tiles]
        starts = [pl.multiple_of(kt * qb, qb) for kt in kts]
        k_t = jnp.concatenate([keys_ref[0, pl.ds(r0, qb), key_col:key_col + kvd] for r0 in starts], axis=0)
        dist = jnp.concatenate([jnp.where(active, q_pos - (r0 + key_row), -1)
                                for r0, (_, active) in zip(starts, tiles)], axis=0)
        in_range = dist >= 0
        scores = [_dot(k_t, qt_scr[k]) for k in range(kvh)]
        new, updates = [], []
        for k in range(kvh):
            if window:
                valid = in_range & (dist < WINDOW)
            else:
                pieces = []
                for kt in kts:
                    chosen = jnp.zeros((qb, qb), F32)
                    for j in range(per_tile):
                        row = sel_scr[k, pl.ds(per_tile * kt + j, 1), :]
                        chosen = jnp.where(key_row // SEL_BLOCK == j, row, chosen)
                    pieces.append(chosen)
                valid = in_range & (jnp.concatenate(pieces, axis=0) > 0.5)
            ms, ls = carry[k]
            ms2, ls2, alphas, probs = [], [], [], []
            for gi in range(g):
                bias = jnp.concatenate([btile_ref[jnp.maximum(qi - kt, 0), k * g + gi] for kt in kts], axis=0)
                s = jnp.where(valid, scores[k][:, gi * qb:(gi + 1) * qb] + bias, NEG)
                m_new = jnp.maximum(ms[gi], jnp.max(s, 0, keepdims=True))
                alpha = jnp.exp(ms[gi] - m_new)
                p = jnp.exp(s - jnp.where(m_new == NEG, 0.0, m_new))
                ms2.append(m_new)
                ls2.append(alpha * ls[gi] + jnp.sum(p, 0, keepdims=True))
                alphas.append(alpha)
                probs.append(p.astype(BF16))
            new.append((ms2, ls2))
            updates.append((jnp.concatenate(alphas, axis=1), jnp.concatenate(probs, axis=1)))
        for k, (alpha, prob) in enumerate(updates):
            v_t = jnp.concatenate([vt_ref[kt, val_row + k * hd:val_row + (k + 1) * hd, :] for kt in kts], axis=1)
            acc_ref[k] = acc_ref[k] * alpha + _dot(v_t, prob)
        return tuple(new)

    acc_s[...] = jnp.zeros_like(acc_s)
    acc_w[...] = jnp.zeros_like(acc_w)
    sel_args = dict(key_col=0, val_row=0, acc_ref=acc_s, window=False)
    win_args = dict(key_col=kvd, val_row=kvd, acc_ref=acc_w, window=True)

    def sel_group(i, carry):
        return tile_step([(SEL_TILES * i + j, SEL_TILES * i + j <= qi) for j in range(SEL_TILES)], carry, **sel_args)

    stat_s = lax.fori_loop(0, qi // SEL_TILES + 1, sel_group, (first,) * kvh)
    stat_w = (first,) * kvh
    win_tiles = [(qi - j, qi - j >= 0) for j in range(WINDOW // qb, -1, -1)]
    for j in range(0, len(win_tiles), WIN_TILES):
        stat_w = tile_step(win_tiles[j:j + WIN_TILES], stat_w, **win_args)
    gt = gt_ref[0]
    for k in range(kvh):
        for gi in range(g):
            h = k * g + gi
            cols = slice(gi * qb, (gi + 1) * qb)
            l_s, l_w = stat_s[k][1][gi], stat_w[k][1][gi]
            o_s = acc_s[k, :, cols] / jnp.where(l_s > 0, l_s, 1.0)
            o_w = acc_w[k, :, cols] / jnp.where(l_w > 0, l_w, 1.0)
            out_scr[h * hd:(h + 1) * hd, :] = (gt[3 * h:3 * h + 1, :] * oc_scr[k, :, cols]
                                               + gt[3 * h + 1:3 * h + 2, :] * o_s + gt[3 * h + 2:3 * h + 3, :] * o_w)
    o_ref[0] = out_scr[...].T.astype(o_ref.dtype)


def nsa_prompt(qt, gt, kc, vct, keys, vt, rel_bias, b, s):
    qb = Q_BLOCK
    nq = s // qb
    nc = s // CMP_BLOCK
    ncp = kc.shape[1]
    nsb = s // SEL_BLOCK
    nsbp = _round_up(nsb, SUBLANES)
    assert SEL_RATIO * nsbp <= ncp
    n_sel = min(TOP_N, nsb)
    nh = NSA_HEADS
    hq = qt.shape[1]
    kvd = kc.shape[2]
    glanes = NSA_GROUP * qb
    cend = jnp.arange(ncp)[:, None] * CMP_BLOCK + (CMP_BLOCK - 1)
    biasc = jnp.transpose(_bias_of(rel_bias, jnp.arange(s)[None, :] - cend), (2, 0, 1))
    r = jnp.arange(qb)
    dist = jnp.arange(nq)[:, None, None] * qb + r[None, None, :] - r[None, :, None]
    btile = jnp.transpose(_bias_of(rel_bias, dist), (0, 3, 1, 2))
    out = pl.pallas_call(
        functools.partial(_nsa_prompt_kernel, nc=nc, nsb=nsb, n_sel=n_sel), grid=(b, nq),
        in_specs=[pl.BlockSpec((1, hq, qb), lambda bi, i: (bi * nq + i, 0, 0)),
                  pl.BlockSpec((1, LANES, qb), lambda bi, i: (bi * nq + i, 0, 0)),
                  pl.BlockSpec((1, ncp, kvd), lambda bi, i: (bi, 0, 0)),
                  pl.BlockSpec((1, kvd, ncp), lambda bi, i: (bi, 0, 0)),
                  pl.BlockSpec((1, s, 2 * kvd), lambda bi, i: (bi, 0, 0)),
                  pl.BlockSpec((nq, 2 * kvd, qb), lambda bi, i: (bi, 0, 0)),
                  pl.BlockSpec((nh, ncp, qb), lambda bi, i: (0, 0, i)),
                  _full(btile.shape)],
        out_specs=pl.BlockSpec((1, qb, hq), lambda bi, i: (bi, i, 0)),
        out_shape=jax.ShapeDtypeStruct((b, s, hq), BF16),
        scratch_shapes=[pltpu.VMEM((NSA_KV_HEADS, kvd, glanes), BF16), pltpu.VMEM((NSA_KV_HEADS, NSA_HD, glanes), F32),
                        pltpu.VMEM((NSA_KV_HEADS, NSA_HD, glanes), F32), pltpu.VMEM((NSA_KV_HEADS, NSA_HD, glanes), F32),
                        pltpu.VMEM((ncp, qb), F32), pltpu.VMEM((NSA_KV_HEADS, nsbp, qb), F32),
                        pltpu.VMEM((hq, qb), F32)],
        compiler_params=_cparams(("parallel", "arbitrary"), VMEM_LIMIT), name="nsa_prompt",
    )(qt, gt, kc, vct, keys.reshape(b, s, 2 * kvd), vt, biasc, btile)
    return out.reshape(b * s, hq)


def _round_up(x, m):
    return (x + m - 1) // m * m


def prep_even(w_in, pe, wc):
    d, n = w_in.shape
    n_pad = _round_up(n - 3 * NSA_HEADS, LANES) + LANES
    w_pad = jnp.zeros((d, n_pad), BF16).at[:, :n].set(w_in.astype(BF16))
    pe2 = jnp.tile(pe, (1, 1, NSA_KV_HEADS))
    zero = jnp.zeros_like(wc)
    wbd = jnp.concatenate([jnp.concatenate([wc if i == j else zero for j in range(NSA_KV_HEADS)], axis=-1)
                           for i in range(NSA_KV_HEADS)], axis=-2)
    half = CMP_BLOCK // 2
    wbd = jnp.concatenate([wbd[:, :half], wbd[:, half:]], axis=2)
    return dict(w_in=w_pad, pe2=pe2, wbd=wbd.astype(BF16))


def _memkv_kernel(x_ref, g_ref, w_ref, o_ref, ob_ref, *, hd, nh):
    y = _dot(_rms(x_ref[...], g_ref[...]).astype(BF16), w_ref[...])
    ob_ref[...] = y.astype(BF16)
    tm = y.shape[0]
    chunks = hd // LANES
    period = 2 * chunks * nh
    for kv in range(2):
        for h in range(nh):
            for c in range(chunks):
                col = (kv * nh + h) * hd + c * LANES
                o_ref[pl.ds((kv * chunks + c) * nh + h, tm, stride=period), :] = y[:, col:col + LANES]


def memkv(mem, g, w, nh):
    m, d = mem.shape
    n = w.shape[1]
    hd = n // (2 * nh)
    per_tok = n // LANES
    tm = _row_tile(m, 256)
    return pl.pallas_call(
        functools.partial(_memkv_kernel, hd=hd, nh=nh), grid=(m // tm,),
        in_specs=[pl.BlockSpec((tm, d), lambda i: (i, 0)), _full((1, d)), _full(w.shape)],
        out_specs=[pl.BlockSpec((tm * per_tok, LANES), lambda i: (i, 0)), pl.BlockSpec((tm, n), lambda i: (i, 0))],
        out_shape=[jax.ShapeDtypeStruct((m * per_tok, LANES), F32), jax.ShapeDtypeStruct((m, n), BF16)],
        compiler_params=_cparams(("parallel",)), name="memkv",
    )(mem, g.reshape(1, d), w)


def _xattn_core(q, kv, hd):
    nh = q.shape[1] // hd
    outs = []
    for h in range(nh):
        s = _dot_nt(q[:, h * hd:(h + 1) * hd], kv[:, h * hd:(h + 1) * hd])
        e = jnp.exp(s - jnp.max(s, -1, keepdims=True))
        p = e / jnp.sum(e, -1, keepdims=True)
        outs.append(_dot(p.astype(BF16), kv[:, (nh + h) * hd:(nh + h + 1) * hd]))
    return jnp.concatenate(outs, axis=1).astype(BF16)


def _xattn_prompt_kernel(*refs, hd, n_in):
    x_ref, g_ref, wq_ref, kv_ref, wo_ref = refs[:5]
    a_refs = refs[5:5 + n_in]
    w_refs = refs[5 + n_in:5 + 2 * n_in]
    o_ref = refs[5 + 2 * n_in]
    x = x_ref[0]
    for a_ref, w_ref in zip(a_refs, w_refs):
        x = x + _dot(a_ref[0], w_ref[...])
    q = (_dot(_rms(x, g_ref[...]).astype(BF16), wq_ref[...]) * (hd ** -0.5)).astype(BF16)
    o = _xattn_core(q, kv_ref[0], hd)
    o_ref[0] = x + _dot(o, wo_ref[...])


def xattn_prompt(x, g, wq, kvb, wo, b, s, acts=(), ws=()):
    d = x.shape[1]
    mt = kvb.shape[0] // b
    tm = _row_tile(s, 512)
    tile = lambda n: pl.BlockSpec((1, tm, n), lambda bi, i: (bi, i, 0))
    out = pl.pallas_call(
        functools.partial(_xattn_prompt_kernel, hd=d // X_HEADS, n_in=len(acts)), grid=(b, s // tm),
        in_specs=[tile(d), _full((1, d)), _full(wq.shape),
                  pl.BlockSpec((1, mt, kvb.shape[1]), lambda bi, i: (bi, 0, 0)), _full(wo.shape)]
        + [tile(a.shape[1]) for a in acts] + [_full(w.shape) for w in ws],
        out_specs=tile(d),
        out_shape=jax.ShapeDtypeStruct((b, s, d), F32),
        compiler_params=_cparams(("parallel", "parallel"), VMEM_LIMIT), name="xattn_prompt",
    )(x.reshape(b, s, d), g.reshape(1, d), wq, kvb.reshape(b, mt, -1), wo,
      *[a.reshape(b, s, -1) for a in acts], *ws)
    return out.reshape(b * s, d)


def _xattn_sample_kernel(x_ref, g_ref, wq_ref, kv_ref, wo_ref, o_ref, q_scr, a_scr, *, hd, mt):
    bi = pl.program_id(0)
    nb = pl.num_programs(0)
    nh = wq_ref.shape[1] // hd

    @pl.when(bi == 0)
    def _():
        q_scr[...] = _dot(_rms(x_ref[...], g_ref[...]).astype(BF16), wq_ref[...]) * (hd ** -0.5)

    q = jnp.broadcast_to(q_scr[pl.ds(bi, 1), :], (SUBLANES, q_scr.shape[1])).astype(BF16)
    chunks = hd // LANES
    period = 2 * chunks * nh

    def head_rows(kv, h):
        return jnp.concatenate([kv_ref[0, pl.ds((kv * chunks + c) * nh + h, mt, stride=period), :]
                                for c in range(chunks)], axis=1).astype(BF16)

    outs = []
    for h in range(nh):
        s = _dot_nt(q[:, h * hd:(h + 1) * hd], head_rows(0, h))
        e = jnp.exp(s - jnp.max(s, -1, keepdims=True))
        p = e / jnp.sum(e, -1, keepdims=True)
        outs.append(_dot(p.astype(BF16), head_rows(1, h)))
    a_scr[pl.ds(bi, 1), :] = jnp.concatenate(outs, axis=1)[0:1, :]

    @pl.when(bi == nb - 1)
    def _():
        o_ref[...] = x_ref[...] + _dot(a_scr[...].astype(BF16), wo_ref[...])


def xattn_sample(x, g, wq, kv_rows, wo, layer, mt):
    bd, d = x.shape
    rows = kv_rows.shape[1]
    return pl.pallas_call(
        functools.partial(_xattn_sample_kernel, hd=d // X_HEADS, mt=mt), grid=(bd,),
        in_specs=[_full((bd, d)), _full((1, d)), _full(wq.shape),
                  pl.BlockSpec((1, rows, LANES), lambda bi: (layer * bd + bi, 0, 0)), _full(wo.shape)],
        out_specs=_full((bd, d)),
        out_shape=jax.ShapeDtypeStruct((bd, d), F32),
        scratch_shapes=[pltpu.VMEM((bd, wq.shape[1]), F32), pltpu.VMEM((bd, wq.shape[1]), F32)],
        compiler_params=_cparams(("arbitrary",), VMEM_LIMIT), name="xattn_sample",
    )(x, g.reshape(1, d), wq, kv_rows, wo)


def _ffn_kernel(x_ref, g_ref, wg_ref, wu_ref, wd_ref, o_ref, h_scr, acc_scr):
    c = pl.program_id(1)

    @pl.when(c == 0)
    def _():
        h_scr[...] = _rms(x_ref[...], g_ref[...]).astype(BF16)
        acc_scr[...] = x_ref[...]

    h = h_scr[...]
    gate = _dot(h, wg_ref[...])
    up = _dot(h, wu_ref[...])
    act = (gate * jax.nn.sigmoid(gate) * up).astype(BF16)
    acc_scr[...] += _dot(act, wd_ref[...])

    @pl.when(c == pl.num_programs(1) - 1)
    def _():
        o_ref[...] = acc_scr[...]


def _xattn_ffn_kernel(*refs, hd, n_in):
    x_ref, gx_ref, wq_ref, kv_ref, wo_ref, gf_ref, wg_ref, wu_ref, wd_ref = refs[:9]
    a_refs = refs[9:9 + n_in]
    w_refs = refs[9 + n_in:9 + 2 * n_in]
    o_ref, h_scr, acc_scr = refs[9 + 2 * n_in:]
    c = pl.program_id(1)

    @pl.when(c == 0)
    def _():
        x = x_ref[...]
        for a_ref, w_ref in zip(a_refs, w_refs):
            x = x + _dot(a_ref[...], w_ref[...])
        q = (_dot(_rms(x, gx_ref[...]).astype(BF16), wq_ref[...]) * (hd ** -0.5)).astype(BF16)
        x = x + _dot(_xattn_core(q, kv_ref[0], hd), wo_ref[...])
        h_scr[...] = _rms(x, gf_ref[...]).astype(BF16)
        acc_scr[...] = x

    h = h_scr[...]
    gate = _dot(h, wg_ref[...])
    up = _dot(h, wu_ref[...])
    act = (gate * jax.nn.sigmoid(gate) * up).astype(BF16)
    acc_scr[...] += _dot(act, wd_ref[...])

    @pl.when(c == pl.num_programs(1) - 1)
    def _():
        o_ref[...] = acc_scr[...]


def xattn_ffn(x, gx, wq, kvb, wo, acts, ws, gf, w_gu, w_dn, b, s):
    m, d = x.shape
    dff = w_dn.shape[0]
    mt = kvb.shape[0] // b
    tm = _row_tile(s, 512)
    per_seq = s // tm
    fc = _ff_chunk(dff, 1408)
    nch = dff // fc
    row = lambda n: pl.BlockSpec((tm, n), lambda i, c: (i, 0))
    return pl.pallas_call(
        functools.partial(_xattn_ffn_kernel, hd=d // X_HEADS, n_in=len(acts)), grid=(m // tm, nch),
        in_specs=[row(d), _full((1, d)), _full(wq.shape),
                  pl.BlockSpec((1, mt, kvb.shape[1]), lambda i, c: (i // per_seq, 0, 0)), _full(wo.shape),
                  _full((1, d)),
                  pl.BlockSpec((d, fc), lambda i, c: (0, c)),
                  pl.BlockSpec((d, fc), lambda i, c: (0, nch + c)),
                  pl.BlockSpec((fc, d), lambda i, c: (c, 0))]
        + [row(a.shape[1]) for a in acts] + [_full(w.shape) for w in ws],
        out_specs=row(d),
        out_shape=jax.ShapeDtypeStruct((m, d), F32),
        scratch_shapes=[pltpu.VMEM((tm, d), BF16), pltpu.VMEM((tm, d), F32)],
        compiler_params=_cparams(("parallel", "arbitrary"), VMEM_LIMIT), name="xattn_ffn",
    )(x, gx.reshape(1, d), wq, kvb.reshape(b, mt, -1), wo, gf.reshape(1, d), w_gu, w_gu, w_dn, *acts, *ws)


def _ff_chunk(dff, pref):
    c = dff
    for n in range(1, dff // LANES + 1):
        if dff % n == 0 and (dff // n) % LANES == 0 and dff // n <= pref:
            c = dff // n
            break
    return c


def ffn(x, g, w_gu, w_dn):
    m, d = x.shape
    dff = w_dn.shape[0]
    tm = _row_tile(m, 512)
    fc = _ff_chunk(dff, 1408)
    nch = dff // fc
    return pl.pallas_call(
        _ffn_kernel, grid=(m // tm, nch),
        in_specs=[pl.BlockSpec((tm, d), lambda i, c: (i, 0)), _full((1, d)),
                  pl.BlockSpec((d, fc), lambda i, c: (0, c)),
                  pl.BlockSpec((d, fc), lambda i, c: (0, nch + c)),
                  pl.BlockSpec((fc, d), lambda i, c: (c, 0))],
        out_specs=pl.BlockSpec((tm, d), lambda i, c: (i, 0)),
        out_shape=jax.ShapeDtypeStruct((m, d), F32),
        scratch_shapes=[pltpu.VMEM((tm, d), BF16), pltpu.VMEM((tm, d), F32)],
        compiler_params=_cparams(("parallel", "arbitrary"), VMEM_LIMIT), name="ffn",
    )(x, g.reshape(1, d), w_gu, w_gu, w_dn)


def _router_kernel(x_ref, g_ref, w_ref, b_ref, *refs, ne, blocks):
    if blocks:
        comb_ref, h_ref, cnt_ref, tri_ref, pos_ref, post_ref, stab_ref = refs
    else:
        comb_ref, h_ref, cnt_ref = refs
    h = _rms(x_ref[...], g_ref[...]).astype(BF16)
    h_ref[...] = h
    logits = _dot(h, w_ref[...]) + b_ref[...]
    lane = lax.broadcasted_iota(jnp.int32, logits.shape, 1)
    logits = jnp.where(lane < ne, logits, -jnp.inf)
    v1 = jnp.max(logits, -1, keepdims=True)
    i1 = jnp.min(jnp.where(logits == v1, lane, LANES), -1, keepdims=True)
    rest = jnp.where(lane == i1, -jnp.inf, logits)
    v2 = jnp.max(rest, -1, keepdims=True)
    i2 = jnp.min(jnp.where(rest == v2, lane, LANES), -1, keepdims=True)
    e2 = jnp.exp(v2 - v1)
    den = 1.0 + e2
    comb_ref[...] = jnp.where(lane == i1, 1.0 / den, 0.0) + jnp.where(lane == i2, e2 / den, 0.0)
    chosen = jnp.where((lane == i1) | (lane == i2), 1.0, 0.0)
    step = pl.program_id(0)

    @pl.when(step == 0)
    def _():
        cnt_ref[...] = jnp.zeros_like(cnt_ref)
        if blocks:
            stab_ref[...] = jnp.zeros_like(stab_ref)

    if blocks:
        t = MOE_TILE
        before = cnt_ref[0:1, :]
        rank = before + _dot(tri_ref[...], chosen.astype(BF16))
        pos = jnp.where(chosen > 0, rank, -1.0)
        pos_ref[...] = pos.astype(jnp.int32)
        pos_t = pos.T
        nep = post_ref.shape[1]
        for j in range(blocks):
            post_ref[j] = pos_t[0:nep, j * t:(j + 1) * t].astype(jnp.int32)
            start = before + jnp.sum(chosen[0:j * t, :], axis=0, keepdims=True) if j else before
            stab_ref[pl.ds(step * blocks + j, 1), :] = start.astype(jnp.int32)

    cnt_ref[0:1, :] += jnp.sum(chosen, axis=0, keepdims=True)

    if blocks:
        @pl.when(step == pl.num_programs(0) - 1)
        def _():
            stab_ref[pl.ds(pl.num_programs(0) * blocks, 1), :] = cnt_ref[0:1, :].astype(jnp.int32)


def router(x, g, w_r, b_r, grouped=False):
    m, d = x.shape
    ne = w_r.shape[1]
    w_pad = jnp.zeros((d, LANES), BF16).at[:, :ne].set(w_r.astype(BF16))
    b_pad = jnp.zeros((1, LANES), F32).at[0, :ne].set(b_r.astype(F32))
    tm = _row_tile(m, 512)
    row = lambda n: pl.BlockSpec((tm, n), lambda i: (i, 0))
    in_specs = [row(d), _full((1, d)), _full((d, LANES)), _full((1, LANES))]
    out_specs = [row(LANES), row(d), _full((SUBLANES, LANES))]
    out_shape = [jax.ShapeDtypeStruct((m, LANES), F32), jax.ShapeDtypeStruct((m, d), BF16),
                 jax.ShapeDtypeStruct((SUBLANES, LANES), F32)]
    args = [x, g.reshape(1, d), w_pad, b_pad]
    blocks = 0
    if grouped:
        assert tm % MOE_TILE == 0
        blocks = tm // MOE_TILE
        nb = m // MOE_TILE
        nep = _round_up(ne, SUBLANES)
        nbp = _round_up(nb + 1, SUBLANES)
        idx = lax.broadcasted_iota(jnp.int32, (tm, tm), 0)
        args.append(jnp.where(lax.broadcasted_iota(jnp.int32, (tm, tm), 1) < idx, 1.0, 0.0).astype(BF16))
        in_specs.append(_full((tm, tm)))
        out_specs += [row(LANES), pl.BlockSpec((blocks, nep, MOE_TILE), lambda i: (i, 0, 0)), _full((nbp, LANES))]
        out_shape += [jax.ShapeDtypeStruct((m, LANES), jnp.int32),
                      jax.ShapeDtypeStruct((nb, nep, MOE_TILE), jnp.int32),
                      jax.ShapeDtypeStruct((nbp, LANES), jnp.int32)]
    def body(x_ref, g_ref, w_ref, b_ref, *rest):
        if grouped:
            tri_ref, comb_ref, h_ref, cnt_ref, pos_ref, post_ref, stab_ref = rest
            _router_kernel(x_ref, g_ref, w_ref, b_ref, comb_ref, h_ref, cnt_ref, tri_ref, pos_ref, post_ref, stab_ref,
                           ne=ne, blocks=blocks)
        else:
            _router_kernel(x_ref, g_ref, w_ref, b_ref, *rest, ne=ne, blocks=0)
    return pl.pallas_call(
        body, grid=(m // tm,), in_specs=in_specs, out_specs=out_specs, out_shape=out_shape,
        compiler_params=_cparams(("arbitrary",)), name="router",
    )(*args)


def _residual_out(y, gain_ref, norm):
    return _rms(y, gain_ref[...]) if norm else y


def _moe_kernel(x_ref, h_ref, comb_ref, wg_ref, wu_ref, wd_ref, gain_ref, o_ref, acc_scr, *, norm):
    e = pl.program_id(1)

    @pl.when(e == 0)
    def _():
        acc_scr[...] = jnp.zeros_like(acc_scr)

    h = h_ref[...]
    gate = _dot(h, wg_ref[0])
    up = _dot(h, wu_ref[0])
    act = (gate * jax.nn.sigmoid(gate) * up).astype(BF16)
    y = _dot(act, wd_ref[0])
    comb = comb_ref[...]
    lane = lax.broadcasted_iota(jnp.int32, comb.shape, 1)
    acc_scr[...] += jnp.sum(jnp.where(lane == e, comb, 0.0), -1, keepdims=True) * y

    @pl.when(e == pl.num_programs(1) - 1)
    def _():
        o_ref[...] = _residual_out(x_ref[...] + acc_scr[...], gain_ref, norm)


def moe(x, h, comb, w_gu, w_dn, final_g=None):
    m, d = x.shape
    ne, dfe = w_dn.shape[:2]
    tm = _row_tile(m, 512)
    gain = jnp.ones((1, d), F32) if final_g is None else final_g.reshape(1, d)
    return pl.pallas_call(
        functools.partial(_moe_kernel, norm=final_g is not None), grid=(m // tm, ne),
        in_specs=[pl.BlockSpec((tm, d), lambda i, e: (i, 0)), pl.BlockSpec((tm, d), lambda i, e: (i, 0)),
                  pl.BlockSpec((tm, LANES), lambda i, e: (i, 0)),
                  pl.BlockSpec((1, d, dfe), lambda i, e: (e, 0, 0)),
                  pl.BlockSpec((1, d, dfe), lambda i, e: (e, 0, 1)),
                  pl.BlockSpec((1, dfe, d), lambda i, e: (e, 0, 0)), _full((1, d))],
        out_specs=pl.BlockSpec((tm, d), lambda i, e: (i, 0)),
        out_shape=jax.ShapeDtypeStruct((m, d), F32),
        scratch_shapes=[pltpu.VMEM((tm, d), F32)],
        compiler_params=_cparams(("parallel", "arbitrary"), VMEM_LIMIT), name="moe",
    )(x, h, comb, w_gu, w_gu, w_dn, gain)


MOE_TILE = 256


def _moe_expert_kernel(te_ref, lo_ref, cnt_ref, ring_ref, tb_ref, nt_ref, h_ref, post_ref, wg_ref, wu_ref, wd_ref,
                       y_ref, hbuf, xg_scr, sem, *, t):
    i = pl.program_id(0)

    @pl.when(i >= nt_ref[0])
    def _():
        y_ref[...] = jnp.zeros_like(y_ref)

    nbuf = hbuf.shape[0]
    ahead = nbuf - 1

    def copy(sb, slot):
        return pltpu.make_async_copy(h_ref.at[pl.ds(pl.multiple_of(sb * t, t), t), :], hbuf.at[slot], sem.at[slot])

    def start_first(tile):
        for j in range(ahead):
            @pl.when(j < cnt_ref[tile])
            def _():
                copy(lo_ref[tile] + j, (ring_ref[tile] + j) % nbuf).start()

    @pl.when(i == 0)
    def _():
        start_first(0)

    @pl.when(i < nt_ref[0])
    def _():
        e = te_ref[i]
        lo = lo_ref[i]
        n = cnt_ref[i]
        ring = ring_ref[i]
        xg_scr[...] = jnp.zeros_like(xg_scr)
        row = i * t - tb_ref[i] + lax.broadcasted_iota(jnp.int32, (t, 1), 0)

        def body(j, c):
            slot = (ring + j) % nbuf
            copy(lo + j, slot).wait()

            @pl.when(j + ahead < n)
            def _():
                copy(lo + j + ahead, (ring + j + ahead) % nbuf).start()

            src_pos = post_ref[lo + j, pl.ds(e, 1), :]
            onehot = jnp.where(src_pos == row, 1.0, 0.0).astype(BF16)
            xg_scr[...] += _dot(onehot, hbuf[slot])
            return c

        lax.fori_loop(0, n, body, 0)

        @pl.when(i + 1 < nt_ref[0])
        def _():
            start_first(i + 1)

        x = xg_scr[...].astype(BF16)
        gate = _dot(x, wg_ref[0])
        up = _dot(x, wu_ref[0])
        act = (gate * jax.nn.sigmoid(gate) * up).astype(BF16)
        y_ref[...] = _dot(act, wd_ref[0])


MOE_WIN = 128


def _moe_combine_kernel(stab_ref, base_ref, x_ref, comb_ref, pos_ref, gain_ref, ys_ref, o_ref, ybuf, extra, acc_scr,
                        sem, sem_x, *, t, ne, norm):
    sb = pl.program_id(0)
    nb = pl.num_programs(0)
    n_win = (t + SUBLANES - 1) // MOE_WIN + 1

    def window(b, e, w):
        s0 = stab_ref[b * ne + e]
        s1 = stab_ref[(b + 1) * ne + e]
        start = (s0 // SUBLANES) * SUBLANES + w * MOE_WIN
        return pl.multiple_of(start, SUBLANES), start < s1

    def first_copy(b, slot, e):
        start, _ = window(b, e, 0)
        return pltpu.make_async_copy(ys_ref.at[pl.ds(start, MOE_WIN), :], ybuf.at[slot, e], sem.at[slot, e])

    def start_all(b, slot):
        for e in range(ne):
            @pl.when(window(b, e, 0)[1])
            def _():
                first_copy(b, slot, e).start()

    @pl.when(sb == 0)
    def _():
        ybuf[...] = jnp.zeros_like(ybuf)
        start_all(0, 0)

    @pl.when(sb + 1 < nb)
    def _():
        start_all(sb + 1, (sb + 1) % 2)

    slot = sb % 2
    comb = comb_ref[...]
    pos = pos_ref[...]
    lane_r = lax.broadcasted_iota(jnp.int32, (1, MOE_WIN), 1)

    def split(a):
        hi = a.astype(BF16)
        return hi, (a - hi.astype(F32)).astype(BF16)

    def weighted_onehot(start, e):
        return jnp.where(pos[:, e:e + 1] - (start - base_ref[e]) == lane_r, comb[:, e:e + 1], 0.0)

    def gathered(sel, rows):
        s_hi, s_lo = split(sel)
        r_hi, r_lo = split(rows)
        return _dot(s_hi, r_hi) + _dot(s_hi, r_lo) + _dot(s_lo, r_hi)

    for e in range(ne):
        @pl.when(window(sb, e, 0)[1])
        def _():
            first_copy(sb, slot, e).wait()
    sel = jnp.concatenate([weighted_onehot(window(sb, e, 0)[0], e) for e in range(ne)], axis=1)
    acc_scr[...] = x_ref[...] + gathered(sel, ybuf[slot].reshape(ne * MOE_WIN, ybuf.shape[-1]))

    for e in range(ne):
        for w in range(1, n_win):
            start_w, needed_w = window(sb, e, w)

            @pl.when(needed_w)
            def _():
                cp = pltpu.make_async_copy(ys_ref.at[pl.ds(start_w, MOE_WIN), :], extra, sem_x.at[0])
                cp.start()
                cp.wait()
                acc_scr[...] += gathered(weighted_onehot(start_w, e), extra[...])

    o_ref[...] = _residual_out(acc_scr[...], gain_ref, norm)


def moe_grouped(x, h, comb, counts, pos, post, ranks_at, w_gu, w_dn, final_g=None):
    m, d = x.shape
    ne, dfe = w_dn.shape[:2]
    t = MOE_TILE
    assert m % t == 0
    nb = m // t
    nep = post.shape[1]
    k_top = TOP_K
    nt_max = k_top * m // t + ne + 1
    cnt = counts[0, :ne].astype(jnp.int32)
    cnt_pad = (cnt + t - 1) // t * t
    ends = jnp.cumsum(cnt_pad)
    base = (ends - cnt_pad).astype(jnp.int32)
    stab = ranks_at[:nb + 1, :ne] + base[None, :]
    r0 = jnp.arange(nt_max, dtype=jnp.int32) * t
    tile_e = jnp.minimum(jnp.sum(ends[None, :] <= r0[:, None], axis=1), ne - 1).astype(jnp.int32)
    n_tiles = (ends[-1] // t).astype(jnp.int32).reshape(1)
    s_e = stab[:, tile_e]
    lo = jnp.sum(s_e[1:] <= r0[None, :], axis=0)
    hi = jnp.sum(s_e[:nb] < r0[None, :] + t, axis=0) - 1
    lo = jnp.clip(lo, 0, nb - 1).astype(jnp.int32)
    hi = jnp.clip(hi, lo, nb - 1).astype(jnp.int32)
    n_src = jnp.where(jnp.arange(nt_max) < n_tiles[0], hi - lo + 1, 0).astype(jnp.int32)
    n_ring = 8
    ring = ((jnp.cumsum(n_src) - n_src) % n_ring).astype(jnp.int32)
    w_spec = lambda shape, col: pl.BlockSpec(shape, lambda i, te, *_: (te[i], 0, col))
    tile_base = base[tile_e]
    grid_spec = pltpu.PrefetchScalarGridSpec(
        num_scalar_prefetch=6, grid=(nt_max,),
        in_specs=[pl.BlockSpec(memory_space=pl.ANY),
                  pl.BlockSpec((nb, nep, t), lambda i, *_: (0, 0, 0)),
                  w_spec((1, d, dfe), 0), w_spec((1, d, dfe), 1), w_spec((1, dfe, d), 0)],
        out_specs=pl.BlockSpec((t, d), lambda i, *_: (i, 0)),
        scratch_shapes=[pltpu.VMEM((n_ring, t, d), BF16), pltpu.VMEM((t, d), F32),
                        pltpu.SemaphoreType.DMA((n_ring,))])
    ys = pl.pallas_call(
        functools.partial(_moe_expert_kernel, t=t), grid_spec=grid_spec,
        out_shape=jax.ShapeDtypeStruct((nt_max * t, d), F32),
        compiler_params=_cparams(("arbitrary",), VMEM_LIMIT), name="moe_experts",
    )(tile_e, lo, n_src, ring, tile_base, n_tiles, h, post, w_gu, w_gu, w_dn)
    gain = jnp.ones((1, d), F32) if final_g is None else final_g.reshape(1, d)
    blk2 = lambda n2: pl.BlockSpec((t, n2), lambda i, *_: (i, 0))
    grid_spec = pltpu.PrefetchScalarGridSpec(
        num_scalar_prefetch=2, grid=(nb,),
        in_specs=[blk2(d), blk2(LANES), blk2(LANES), pl.BlockSpec((1, d), lambda i, *_: (0, 0)),
                  pl.BlockSpec(memory_space=pl.ANY)],
        out_specs=blk2(d),
        scratch_shapes=[pltpu.VMEM((2, ne, MOE_WIN, d), F32), pltpu.VMEM((MOE_WIN, d), F32), pltpu.VMEM((t, d), F32),
                        pltpu.SemaphoreType.DMA((2, ne)), pltpu.SemaphoreType.DMA((1,))])
    return pl.pallas_call(
        functools.partial(_moe_combine_kernel, t=t, ne=ne, norm=final_g is not None), grid_spec=grid_spec,
        out_shape=jax.ShapeDtypeStruct((m, d), F32),
        compiler_params=_cparams(("arbitrary",), VMEM_LIMIT), name="moe_combine",
    )(stab.reshape(-1), base, x, comb, pos, gain, ys)


def _inproj_odd_kernel(x_ref, g_ref, w_ref, bias_ref, q_ref, k_ref, v_ref, og_ref, gc_ref, gr_ref, *, hq, hv, nh):
    xn = _rms(x_ref[...], g_ref[...]).astype(BF16)

    def mm(lo, hi):
        return _dot(xn, w_ref[:, lo:hi])

    dk = hq // nh
    q_ref[...] = mm(0, hq).astype(BF16)
    k_ref[...] = (mm(hq, 2 * hq) * (dk ** -0.5)).astype(BF16)
    v_ref[...] = mm(2 * hq, 2 * hq + hv).astype(BF16)
    og_ref[...] = jax.nn.sigmoid(mm(2 * hq + hv, 2 * hq + 2 * hv))
    gi = mm(2 * hq + 2 * hv, 2 * hq + 2 * hv + LANES) + bias_ref[...]
    lane = lax.broadcasted_iota(jnp.int32, gi.shape, 1)
    gates = jnp.where(lane < nh, gi, jax.nn.log_sigmoid(gi))
    gc_ref[...] = gates
    gr_ref[...] = gates.T[0:SUBLANES, :]


def inproj_odd(x, g, w_pad, gate_bias):
    m, d = x.shape
    nh = MLSTM_HEADS
    hq = hv = d
    tm = _row_tile(m, 256)
    row = lambda n: pl.BlockSpec((tm, n), lambda i: (i, 0))
    outs = [(hq, BF16), (hq, BF16), (hv, BF16), (hv, F32), (LANES, F32)]
    if tm % LANES:
        gr_spec = _full((SUBLANES, m))
    else:
        gr_spec = pl.BlockSpec((SUBLANES, tm), lambda i: (0, i))
    return pl.pallas_call(
        functools.partial(_inproj_odd_kernel, hq=hq, hv=hv, nh=nh), grid=(m // tm,),
        in_specs=[row(d), _full((1, d)), _full(w_pad.shape), _full((1, LANES))],
        out_specs=[row(n) for n, _ in outs] + [gr_spec],
        out_shape=[jax.ShapeDtypeStruct((m, n), t) for n, t in outs] + [jax.ShapeDtypeStruct((SUBLANES, m), F32)],
        compiler_params=_cparams(("parallel",), VMEM_LIMIT), name="inproj_odd",
    )(x, g.reshape(1, d), w_pad, gate_bias)


def prep_odd(w_in, b_i, b_f):
    d, n = w_in.shape
    n_pad = _round_up(n - 2 * MLSTM_HEADS, LANES) + LANES
    w_pad = jnp.zeros((d, n_pad), BF16).at[:, :n].set(w_in.astype(BF16))
    bias = jnp.zeros((1, LANES), F32).at[0, :2 * MLSTM_HEADS].set(jnp.concatenate([b_i, b_f]).astype(F32))
    return dict(w_in=w_pad, bias=bias)


def _mlstm_prompt_kernel(q_ref, k_ref, v_ref, og_ref, gc_ref, gr_ref, gain_ref, hn_ref, c_ref, n_ref, m_ref,
                         *, nh, dk, dv, ln):
    ci = pl.program_id(1)

    @pl.when(ci == 0)
    def _():
        c_ref[...] = jnp.zeros_like(c_ref)
        n_ref[...] = jnp.zeros_like(n_ref)
        m_ref[...] = jnp.full(m_ref.shape, NEG, F32)

    row = lax.broadcasted_iota(jnp.int32, (ln, ln), 0)
    col = lax.broadcasted_iota(jnp.int32, (ln, ln), 1)
    tri = row >= col
    gc = gc_ref[...]
    gr = gr_ref[...]
    for h in range(nh):
        q = q_ref[:, h * dk:(h + 1) * dk]
        k = k_ref[:, h * dk:(h + 1) * dk]
        v = v_ref[:, h * dv:(h + 1) * dv]
        ig_c, lf_c = gc[:, h:h + 1], gc[:, nh + h:nh + h + 1]
        ig_r, lf_r = gr[h:h + 1, :], gr[nh + h:nh + h + 1, :]
        b_c = jnp.sum(jnp.where(tri, lf_r, 0.0), axis=1, keepdims=True)
        b_r = jnp.sum(jnp.where(row <= col, lf_c, 0.0), axis=0, keepdims=True)
        m_prev = m_ref[0, h:h + 1, 0:1]
        c_prev = c_ref[0, h]
        n_prev = n_ref[0, h:h + 1, :]
        dmat = jnp.where(tri, b_c - b_r + ig_r, NEG)
        inter = b_c + m_prev
        mt = jnp.maximum(inter, jnp.max(dmat, -1, keepdims=True))
        wm = jnp.exp(dmat - mt)
        a = jnp.exp(inter - mt)
        wqk = wm * _dot_nt(q, k)
        num = a * _dot_nt(q, c_prev.astype(BF16)) + _dot(wqk.astype(BF16), v)
        den = a * jnp.sum(q.astype(F32) * n_prev, -1, keepdims=True) + jnp.sum(wqk, -1, keepdims=True)
        hh = num / jnp.maximum(jnp.abs(den), jnp.exp(-mt))
        b_end = b_c[ln - 1:ln, :]
        m_new = mt[ln - 1:ln, :]
        a_end = jnp.exp(b_end + m_prev - m_new)
        w_s = jnp.exp(b_end - b_c + ig_c - m_new)
        c_ref[0, h] = a_end * c_prev + _dot_tn((v.astype(F32) * w_s).astype(BF16), k)
        n_ref[0, h:h + 1, :] = a_end * n_prev + jnp.sum(w_s * k.astype(F32), axis=0, keepdims=True)
        m_ref[0, h:h + 1, :] = jnp.broadcast_to(m_new, (1, m_ref.shape[2]))
        hn = hh * lax.rsqrt(jnp.mean(hh * hh, -1, keepdims=True) + RMS_EPS)
        hn = hn * gain_ref[:, h * dv:(h + 1) * dv] * og_ref[:, h * dv:(h + 1) * dv]
        hn_ref[:, h * dv:(h + 1) * dv] = hn.astype(hn_ref.dtype)


def mlstm_prompt(q, k, v, og, gc, gr, gain, b, s):
    m, d = q.shape
    nh = MLSTM_HEADS
    dk = dv = d // nh
    ln = _row_tile(s, MLSTM_CHUNK)
    nch = s // ln
    row = lambda n: pl.BlockSpec((ln, n), lambda bi, ci: (bi * nch + ci, 0))
    return pl.pallas_call(
        functools.partial(_mlstm_prompt_kernel, nh=nh, dk=dk, dv=dv, ln=ln), grid=(b, nch),
        in_specs=[row(d), row(d), row(d), row(d), row(LANES),
                  pl.BlockSpec((SUBLANES, ln), lambda bi, ci: (0, bi * nch + ci)), _full((1, d))],
        out_specs=[row(d), pl.BlockSpec((1, nh, dv, dk), lambda bi, ci: (bi, 0, 0, 0)),
                   pl.BlockSpec((1, nh, dk), lambda bi, ci: (bi, 0, 0)),
                   pl.BlockSpec((1, nh, LANES), lambda bi, ci: (bi, 0, 0))],
        out_shape=[jax.ShapeDtypeStruct((m, d), BF16), jax.ShapeDtypeStruct((b, nh, dv, dk), F32),
                   jax.ShapeDtypeStruct((b, nh, dk), F32), jax.ShapeDtypeStruct((b, nh, LANES), F32)],
        compiler_params=_cparams(("parallel", "arbitrary"), VMEM_LIMIT), name="mlstm_prompt",
    )(q, k, v, og, gc, gr, gain.reshape(1, d))


def _mlstm_sample_kernel(q_ref, k_ref, v_ref, og_ref, g_ref, gain_ref, c_ref, n_ref, m_ref,
                         hn_ref, co_ref, no_ref, mo_ref, *, nh):
    row = lax.broadcasted_iota(jnp.int32, (SUBLANES, 1), 0)
    for h in range(nh):
        q = q_ref[0, h:h + 1, :]
        k = k_ref[0, h:h + 1, :]
        v = v_ref[0, h:h + 1, :].astype(F32)
        ig = g_ref[0, h:h + 1, 0:1]
        lf = g_ref[0, h:h + 1, 1:2]
        m_prev = m_ref[0, h:h + 1, :]
        c_prev = c_ref[0, h]
        n_prev = n_ref[0, h:h + 1, :]
        inter = lf + m_prev
        mt = jnp.maximum(inter, ig)
        wm = jnp.exp(ig - mt)
        a = jnp.exp(inter - mt)
        q8 = jnp.broadcast_to(q, (SUBLANES, q.shape[1]))
        cq = _dot_nt(q8, c_prev.astype(BF16))[0:1, :]
        wqk = wm * jnp.sum(q.astype(F32) * k.astype(F32), -1, keepdims=True)
        num = a * cq + wqk * v
        den = a * jnp.sum(n_prev * q.astype(F32), -1, keepdims=True) + wqk
        hh = num / jnp.maximum(jnp.abs(den), jnp.exp(-mt))
        v8 = jnp.where(row == 0, jnp.broadcast_to(v * wm, (SUBLANES, v.shape[1])), 0.0).astype(BF16)
        k8 = jnp.broadcast_to(k, (SUBLANES, k.shape[1]))
        co_ref[0, h] = a * c_prev + _dot_tn(v8, k8)
        no_ref[0, h:h + 1, :] = a * n_prev + wm * k.astype(F32)
        mo_ref[0, h:h + 1, :] = mt
        hn = hh * lax.rsqrt(jnp.mean(hh * hh, -1, keepdims=True) + RMS_EPS)
        hn_ref[0, h:h + 1, :] = (hn * gain_ref[h:h + 1, :] * og_ref[0, h:h + 1, :]).astype(hn_ref.dtype)


def mlstm_sample(q, k, v, og, gc, gain, c, n, m):
    bd, d = q.shape
    nh = MLSTM_HEADS
    dk = d // nh
    heads = lambda a: a.reshape(bd, nh, dk)
    g2 = jnp.transpose(gc[:, :2 * nh].reshape(bd, 2, nh), (0, 2, 1))
    blk3 = lambda n2: pl.BlockSpec((1, nh, n2), lambda bi: (bi, 0, 0))
    cspec = pl.BlockSpec((1, nh, dk, dk), lambda bi: (bi, 0, 0, 0))
    hn, co, no, mo = pl.pallas_call(
        functools.partial(_mlstm_sample_kernel, nh=nh), grid=(bd,),
        in_specs=[blk3(dk), blk3(dk), blk3(dk), blk3(dk), blk3(2), _full((nh, dk)), cspec, blk3(dk), blk3(1)],
        out_specs=[blk3(dk), cspec, blk3(dk), blk3(1)],
        out_shape=[jax.ShapeDtypeStruct((bd, nh, dk), BF16), jax.ShapeDtypeStruct(c.shape, F32),
                   jax.ShapeDtypeStruct(n.shape, F32), jax.ShapeDtypeStruct((bd, nh, 1), F32)],
        compiler_params=_cparams(("parallel",)), name="mlstm_sample",
    )(heads(q), heads(k), heads(v), heads(og), g2, gain.reshape(nh, dk), c, n, m.reshape(bd, nh, 1))
    return hn.reshape(bd, d), co, no, mo.reshape(bd, nh)


def _row_to_col(row):
    n = row.shape[1]
    eye = lax.broadcasted_iota(jnp.int32, (n, n), 0) == lax.broadcasted_iota(jnp.int32, (n, n), 1)
    return jnp.sum(jnp.where(eye, row, 0.0), axis=1, keepdims=True)


def _head_pad(q, keep):
    q2 = jnp.concatenate([q] * NSA_KV_HEADS, axis=1)
    row = lax.broadcasted_iota(jnp.int32, q2.shape, 0)
    lane = lax.broadcasted_iota(jnp.int32, q2.shape, 1)
    return jnp.where((lane // NSA_HD == row // NSA_GROUP) & keep(row), q2, jnp.zeros_like(q2))


def _nsa_sample_cmp_kernel(pt_ref, q_ref, pages_ref, pe_ref, w_ref, biasc_ref, oc_ref, idx_ref, xbuf, xrow, sem,
                           *, n_pages, nc, ncp, nsb, n_sel, past):
    b = pl.program_id(0)
    nb = pl.num_programs(0)
    kvd = NSA_KV_HEADS * NSA_HD

    def page_copy(bb, slot, p, sl):
        return pltpu.make_async_copy(pages_ref.at[pt_ref[bb * n_pages + p], pl.ds(sl * kvd, kvd), :],
                                     xbuf.at[slot, sl, p], sem.at[slot])

    def start_all(bb, slot):
        def body(p, c):
            page_copy(bb, slot, p, 0).start()
            page_copy(bb, slot, p, 1).start()
            return c
        lax.fori_loop(0, n_pages, body, 0)

    @pl.when(b == 0)
    def _():
        start_all(0, 0)

    @pl.when(b + 1 < nb)
    def _():
        start_all(b + 1, (b + 1) % 2)

    slot = b % 2

    def wait_body(p, c):
        page_copy(b, slot, p, 0).wait()
        page_copy(b, slot, p, 1).wait()
        return c
    lax.fori_loop(0, n_pages, wait_body, 0)

    def file_page(p, c):
        for sl in range(2):
            _file_rows(xbuf[slot, sl, p], xrow, sl, p)
        return c
    lax.fori_loop(0, n_pages, file_page, 0, unroll=8)

    acc = _compress_filed(xrow, pe_ref, w_ref, nc)
    kc = acc[:, 0:kvd].astype(BF16)
    vc = acc[:, kvd:2 * kvd].astype(BF16)
    q = q_ref[0]
    nh = q.shape[0]
    qpad = _head_pad(q, lambda r: r >= 0)
    s = _dot_nt(qpad, kc)
    s = s + biasc_ref[:, 0:nc]
    e = jnp.exp(s - jnp.max(s, -1, keepdims=True))
    p_c = e / jnp.sum(e, -1, keepdims=True)
    o = _dot(p_c.astype(BF16), vc)
    row = lax.broadcasted_iota(jnp.int32, (nh, NSA_HD), 0)
    o_h = o[:, 0:NSA_HD]
    for k in range(1, NSA_KV_HEADS):
        o_h = jnp.where(row // NSA_GROUP == k, o[:, k * NSA_HD:(k + 1) * NSA_HD], o_h)
    oc_ref[0] = o_h
    prow = lax.broadcasted_iota(jnp.int32, p_c.shape, 0)
    lane = lax.broadcasted_iota(jnp.int32, (1, ncp), 1)
    blk = lane // 2
    cur = past // SEL_BLOCK
    forced = (blk == 0) | (blk == cur) | (blk == cur - 1)
    is_cand = ((lane % 2) == 0) & (lane < 2 * nsb)
    nselp = idx_ref.shape[1]
    rsel = lax.broadcasted_iota(jnp.int32, (nselp, 1), 0).astype(F32)
    out_lane = lax.broadcasted_iota(jnp.int32, (nselp, LANES), 1)
    result = jnp.full((nselp, LANES), -1, jnp.int32)
    for k in range(NSA_KV_HEADS):
        imp = jnp.sum(jnp.where(prow // NSA_GROUP == k, p_c, 0.0), axis=0, keepdims=True)
        imp = jnp.concatenate([imp, jnp.zeros((1, ncp - nc), F32)], axis=1)
        imp = _pair_sum(imp)
        score = jnp.where(forced, FORCE_SCORE, jnp.where(blk <= cur, imp, -1.0))
        score = jnp.where(is_cand, score, -2.0)
        sel, rank = _select_blocks(score, n_sel, nsb)
        hit = (rank == rsel) & (sel > 0.5)
        idx = jnp.sum(jnp.where(hit, (blk + 1).astype(F32), 0.0), axis=1, keepdims=True) - 1.0
        result = jnp.where(out_lane == k, idx.astype(jnp.int32), result)
    idx_ref[0] = result


def nsa_sample_cmp(q8, pages, page_table, pe2, wbd, rel_bias):
    bd, nh, hd = q8.shape
    n_pages = page_table.shape[1]
    past = n_pages * PAGE_SIZE
    nc = past // CMP_BLOCK
    nsb = -(-(past + 1) // SEL_BLOCK)
    n_sel = min(TOP_N, nsb)
    ncp = _round_up(max(nc, SEL_RATIO * nsb), LANES)
    nselp = _round_up(n_sel, SUBLANES)
    kvd = wbd.shape[-1]
    cend = jnp.arange(nc) * CMP_BLOCK + (CMP_BLOCK - 1)
    biasc = jnp.zeros((nh, ncp), F32).at[:, :nc].set(_bias_of(rel_bias, past - cend).T)
    grid_spec = pltpu.PrefetchScalarGridSpec(
        num_scalar_prefetch=1, grid=(bd,),
        in_specs=[pl.BlockSpec((1, nh, hd), lambda bi, pt: (bi, 0, 0)),
                  pl.BlockSpec(memory_space=pl.ANY),
                  pl.BlockSpec(pe2.shape, lambda bi, pt: (0, 0, 0)),
                  pl.BlockSpec(wbd.shape, lambda bi, pt: (0, 0, 0, 0)),
                  pl.BlockSpec((nh, ncp), lambda bi, pt: (0, 0))],
        out_specs=[pl.BlockSpec((1, nh, hd), lambda bi, pt: (bi, 0, 0)),
                   pl.BlockSpec((1, nselp, LANES), lambda bi, pt: (bi, 0, 0))],
        scratch_shapes=[pltpu.VMEM((2, 2, n_pages, kvd, PAGE_SIZE), F32),
                        pltpu.VMEM((2, CMP_BLOCK // SUBLANES, nc * SUBLANES, kvd), F32),
                        pltpu.SemaphoreType.DMA((2,))])
    oc, idx = pl.pallas_call(
        functools.partial(_nsa_sample_cmp_kernel, n_pages=n_pages, nc=nc, ncp=ncp, nsb=nsb, n_sel=n_sel, past=past),
        grid_spec=grid_spec,
        out_shape=[jax.ShapeDtypeStruct((bd, nh, hd), F32), jax.ShapeDtypeStruct((bd, nselp, LANES), jnp.int32)],
        compiler_params=_cparams(("arbitrary",), VMEM_LIMIT), name="nsa_sample_cmp",
    )(page_table.reshape(-1), q8, pages, pe2, wbd, biasc)
    sel_idx = jnp.transpose(idx[:, :n_sel, :NSA_KV_HEADS], (0, 2, 1))
    return oc, sel_idx


def _nsa_sample_att_kernel(pt_ref, si_ref, q_ref, g_ref, oc_ref, kvn_ref, wn_ref, wc_ref, pages_ref,
                           bsel_ref, bwin_ref, ob_ref, win_ref, selbuf, wall, sem,
                           *, n_pages, n_sel, past, wb):
    b = pl.program_id(0)
    nb = pl.num_programs(0)
    kvd = NSA_KV_HEADS * NSA_HD
    hd = NSA_HD
    n_blk_pages = past // SEL_BLOCK
    per_page = PAGE_SIZE // SEL_BLOCK
    n_slots = NSA_KV_HEADS * n_sel

    def blk_of(bb, j):
        return si_ref[bb * n_slots + j]

    def blk_copy(bb, slot, j):
        blk = jnp.clip(blk_of(bb, j), 0, n_blk_pages - 1)
        page = pt_ref[bb * n_pages + blk // per_page]
        return pltpu.make_async_copy(pages_ref.at[page, pl.ds(2 * kvd, 2 * kvd), :], selbuf.at[slot, j], sem.at[slot])

    def in_pages(bb, j):
        blk = blk_of(bb, j)
        return (blk >= 0) & (blk < n_blk_pages)

    def start_all(bb, slot):
        def body(j, c):
            @pl.when(in_pages(bb, j))
            def _():
                blk_copy(bb, slot, j).start()
            return c
        lax.fori_loop(0, n_slots, body, 0)

    @pl.when(b == 0)
    def _():
        start_all(0, 0)

    @pl.when(b + 1 < nb)
    def _():
        start_all(b + 1, (b + 1) % 2)

    slot = b % 2
    new_sel = _row_to_col(kvn_ref[0][:, 2 * kvd:4 * kvd])
    lane = lax.broadcasted_iota(jnp.int32, (1, PAGE_SIZE), 1)

    def wait_body(j, c):
        @pl.when(in_pages(b, j))
        def _():
            blk_copy(b, slot, j).wait()

        @pl.when(jnp.logical_not(in_pages(b, j)))
        def _():
            is_new = blk_of(b, j) == n_blk_pages
            selbuf[slot, j] = jnp.where((lane == 0) & is_new, new_sel, 0.0)
        return c
    lax.fori_loop(0, n_slots, wait_body, 0)

    q = q_ref[0]
    nh = q.shape[0]
    gates = g_ref[0]
    head = lax.broadcasted_iota(jnp.int32, (nh, 1), 0)

    def attend(qp, keys_t, vals_t, bias, valid):
        s = jnp.where(valid, _dot(qp, keys_t) + bias, NEG)
        e = jnp.where(valid, jnp.exp(s - jnp.max(s, -1, keepdims=True)), 0.0)
        den = jnp.sum(e, -1, keepdims=True)
        p = e / jnp.where(den > 0, den, 1.0)
        return _dot_nt(p.astype(BF16), vals_t)

    o_s = jnp.zeros((nh, kvd), F32)
    for k in range(NSA_KV_HEADS):
        blks = [blk_of(b, k * n_sel + r) for r in range(n_sel)]
        tiles = [selbuf[slot, k * n_sel + r] for r in range(n_sel)]
        keys_t = jnp.concatenate([t_[0:kvd, :] for t_ in tiles], axis=1).astype(BF16)
        vals_t = jnp.concatenate([t_[kvd:2 * kvd, :] for t_ in tiles], axis=1).astype(BF16)
        bias = jnp.concatenate([bsel_ref[jnp.clip(bl // per_page, 0, n_pages)] for bl in blks], axis=1)
        valid = jnp.concatenate(
            [(lane // SEL_BLOCK == bl % per_page) & ((bl // per_page) * PAGE_SIZE + lane <= past) & (bl >= 0)
             for bl in blks], axis=1)
        o_k = attend(_head_pad(q, lambda r: r // NSA_GROUP == k), keys_t, vals_t, bias, valid)
        o_s = jnp.where(head // NSA_GROUP == k, o_k, o_s)
    wlanes = wall.shape[1]
    wall[:, 0:wb] = wc_ref[0]
    tail = lax.broadcasted_iota(jnp.int32, (1, wlanes - wb), 1)
    wall[:, wb:wlanes] = jnp.where(tail == 0, _row_to_col(wn_ref[0]), 0.0)
    win_ref[0] = pltpu.roll(wall[...], wlanes - 1, 1)[:, 0:wb]
    w_pos = lax.broadcasted_iota(jnp.int32, (1, wlanes), 1)
    valid_w = (w_pos <= wb) & (wb - w_pos < WINDOW) & (past - wb + w_pos >= 0)
    o_w = attend(_head_pad(q, lambda r: r >= 0), wall[0:kvd, :].astype(BF16), wall[kvd:2 * kvd, :].astype(BF16),
                 bwin_ref[...], valid_w)
    o_c = jnp.concatenate([oc_ref[0]] * NSA_KV_HEADS, axis=1)
    mix = gates[:, 0:1] * o_c + gates[:, 1:2] * o_s + gates[:, 2:3] * o_w
    out = mix[:, 0:hd]
    for k in range(1, NSA_KV_HEADS):
        out = jnp.where(head // NSA_GROUP == k, mix[:, k * hd:(k + 1) * hd], out)
    ob_ref[0] = out.astype(ob_ref.dtype)


def nsa_sample_att(q8, gates, oc, kv03, kv45, wcache_t, layer, pages_t, page_table, sel_idx, rel_bias):
    bd, nh, hd = q8.shape
    n_pages = page_table.shape[1]
    past = n_pages * PAGE_SIZE
    wb = wcache_t.shape[2]
    kvd = NSA_KV_HEADS * hd
    n_sel = sel_idx.shape[2]
    wlanes = _round_up(wb + 1, LANES)
    g3 = gates[:, :3 * nh].reshape(bd, nh, 3)
    dist = past - (jnp.arange(n_pages + 1)[:, None] * PAGE_SIZE + jnp.arange(PAGE_SIZE)[None, :])
    bsel = jnp.transpose(_bias_of(rel_bias, dist), (0, 2, 1))
    bwin = _bias_of(rel_bias, wb - jnp.arange(wlanes)).T
    blk = lambda n2, n3: pl.BlockSpec((1, n2, n3), lambda bi, pt, si: (bi, 0, 0))
    grid_spec = pltpu.PrefetchScalarGridSpec(
        num_scalar_prefetch=2, grid=(bd,),
        in_specs=[blk(nh, hd), blk(nh, 3), blk(nh, hd), blk(1, 4 * kvd), blk(1, 2 * kvd),
                  pl.BlockSpec((1, 2 * kvd, wb), lambda bi, pt, si: (layer * bd + bi, 0, 0)),
                  pl.BlockSpec(memory_space=pl.ANY),
                  pl.BlockSpec(bsel.shape, lambda bi, pt, si: (0, 0, 0)),
                  pl.BlockSpec(bwin.shape, lambda bi, pt, si: (0, 0))],
        out_specs=[blk(nh, hd), blk(2 * kvd, wb)],
        scratch_shapes=[pltpu.VMEM((2, NSA_KV_HEADS * n_sel, 2 * kvd, PAGE_SIZE), F32),
                        pltpu.VMEM((2 * kvd, wlanes), F32), pltpu.SemaphoreType.DMA((2,))])
    ob, win = pl.pallas_call(
        functools.partial(_nsa_sample_att_kernel, n_pages=n_pages, n_sel=n_sel, past=past, wb=wb),
        grid_spec=grid_spec,
        out_shape=[jax.ShapeDtypeStruct((bd, nh, hd), BF16), jax.ShapeDtypeStruct((bd, 2 * kvd, wb), F32)],
        compiler_params=_cparams(("arbitrary",), VMEM_LIMIT), name="nsa_sample_att",
    )(page_table.reshape(-1), sel_idx.reshape(-1), q8, g3, oc, kv03.reshape(bd, 1, -1), kv45.reshape(bd, 1, -1),
      wcache_t, pages_t, bsel, bwin)
    return ob.reshape(bd, nh * hd), win


def kernel(x_prompt, x_sample, mem_prompt, cache_conv, cache_nsa_pages, cache_nsa_window, state_mlstm_c,
           state_mlstm_n, state_mlstm_m, cache_mem_kv, page_table, rel_bias, norm_mix, norm_xattn, norm_mem,
           norm_ffn, norm_final, w_in_even, w_out_even, conv_w, conv_b, conv_ln_g, conv_ln_b, nsa_cmp_pe,
           nsa_cmp_w, w_in_odd, mlstm_b_i, mlstm_b_f, mlstm_norm, w_out_odd, xattn_wq, xattn_wkv, xattn_wo,
           ffn_w_gu, ffn_w_dn, router_w, router_b, expert_w_gu, expert_w_dn):
    b, s, d = x_prompt.shape
    bd, td, _ = x_sample.shape
    assert td == 1, "the sample group decodes one token per sequence"
    depth = norm_mix.shape[0]
    mt = mem_prompt.shape[1]
    cc = conv_w.shape[2]
    hist = conv_w.shape[1] - 1
    wb = cache_nsa_window.shape[2]
    kvh, hd = NSA_KV_HEADS, NSA_HD
    n_pool = cache_nsa_pages.shape[1]
    assert s >= hist and s >= wb and s % Q_BLOCK == 0
    xp = x_prompt.reshape(b * s, d)
    xs = x_sample.reshape(bd, d)
    mem = mem_prompt.reshape(b * mt, d)
    pages_t = jnp.swapaxes(cache_nsa_pages.reshape(-1, PAGE_SIZE, 4 * kvh * hd), 1, 2)
    window_t = jnp.swapaxes(cache_nsa_window.reshape(-1, wb, 2 * kvh * hd), 1, 2)
    xhd = d // X_HEADS
    memkv_rows = jnp.swapaxes(cache_mem_kv.reshape(depth * bd, mt, 2, X_HEADS, xhd // LANES, LANES), 3, 4)
    memkv_rows = memkv_rows.reshape(depth * bd, -1, LANES)
    bf = lambda a: a.astype(BF16)
    conv_p, conv_s, nsa_p, nsa_s, win_p, win_s = [], [], [], [], [], []
    mc_p, mc_s, mn_p, mn_s, mm_p, mm_s, memkv_p = [], [], [], [], [], [], []
    for l in range(depth):
        li = l // 2
        if l % 2 == 0:
            prm = prep_even(w_in_even[li], nsa_cmp_pe[li], nsa_cmp_w[li])
            w_out = bf(w_out_even[li])
            w_parts = [w_out[:cc], w_out[cc:]]
            conv_args = (conv_w[li], conv_b[li], conv_ln_g[li], conv_ln_b[li])
            glu, keys, qt, vt, gt, kv_t = inproj_even(xp, norm_mix[l], prm['w_in'], cc, s)
            a_out = conv_prompt(glu, *conv_args, b, s)
            kc, vct = compress_prompt(kv_t, prm['pe2'], prm['wbd'], _round_up(s // CMP_BLOCK, LANES))
            b_out = nsa_prompt(qt, gt, kc, vct, keys, vt, rel_bias, b, s)
            mix_p = ([a_out, b_out], w_parts)
            conv_p.append(glu.reshape(b, s, cc)[:, s - hist:])
            rows_t = kv_t.reshape(b, 6, kvh, hd, s)
            nsa_p.append(jnp.transpose(rows_t[:, :4], (0, 4, 1, 2, 3)))
            win_p.append(jnp.transpose(rows_t[:, 4:, :, :, s - wb:], (0, 4, 1, 2, 3)))
            glu, kv03, kv45, q, gates = inproj_even(xs, norm_mix[l], prm['w_in'], cc)
            a_out, conv_state = conv_sample(cache_conv[li], glu, *conv_args)
            q8 = q.reshape(bd, NSA_HEADS, hd)
            pt = page_table + li * n_pool
            o_c, sel_idx = nsa_sample_cmp(q8, pages_t, pt, prm['pe2'], prm['wbd'], rel_bias)
            b_out, win = nsa_sample_att(q8, gates, o_c, kv03, kv45, window_t, li, pages_t, pt, sel_idx, rel_bias)
            xs = outproj(xs, [a_out, b_out], w_parts)
            conv_s.append(conv_state)
            nsa_s.append(kv03.reshape(bd, 1, 4, kvh, hd))
            win_s.append(jnp.transpose(win.reshape(bd, 2, kvh, hd, wb), (0, 4, 1, 2, 3)))
        else:
            prm = prep_odd(w_in_odd[li], mlstm_b_i[li], mlstm_b_f[li])
            w_out = bf(w_out_odd[li])
            q, k, v, og, gc, gr = inproj_odd(xp, norm_mix[l], prm['w_in'], prm['bias'])
            hn, c_new, n_new, m_new = mlstm_prompt(q, k, v, og, gc, gr, mlstm_norm[li], b, s)
            mix_p = ([hn], [w_out])
            mc_p.append(c_new)
            mn_p.append(n_new)
            mm_p.append(m_new[:, :, 0])
            q, k, v, og, gc, gr = inproj_odd(xs, norm_mix[l], prm['w_in'], prm['bias'])
            hn, c_new, n_new, m_new = mlstm_sample(q, k, v, og, gc, mlstm_norm[li], state_mlstm_c[li],
                                                   state_mlstm_n[li], state_mlstm_m[li])
            xs = outproj(xs, [hn], [w_out])
            mc_s.append(c_new)
            mn_s.append(n_new)
            mm_s.append(m_new)
        wq, wo = bf(xattn_wq[l]), bf(xattn_wo[l])
        mkv_rows, mkv_b = memkv(mem, norm_mem[l], bf(xattn_wkv[l]), X_HEADS)
        mkv = jnp.swapaxes(mkv_rows.reshape(b, mt, 2, xhd // LANES, X_HEADS, LANES), 3, 4)
        memkv_p.append(mkv.reshape(b, mt, 2, X_HEADS, xhd))
        xs = xattn_sample(xs, norm_xattn[l], wq, memkv_rows, wo, l, mt)
        if l % 2 == 0:
            w_gu, w_dn = bf(ffn_w_gu[li]), bf(ffn_w_dn[li])
            xp = xattn_ffn(xp, norm_xattn[l], wq, mkv_b, wo, *mix_p, norm_ffn[l], w_gu, w_dn, b, s)
            xs = ffn(xs, norm_ffn[l], w_gu, w_dn)
        else:
            xp = xattn_prompt(xp, norm_xattn[l], wq, mkv_b, wo, b, s, *mix_p)
            e_gu, e_dn = bf(expert_w_gu[li]), bf(expert_w_dn[li])
            final_g = norm_final if l == depth - 1 else None
            comb, h, counts, pos, post, ranks_at = router(xp, norm_ffn[l], router_w[li], router_b[li], grouped=True)
            xp = moe_grouped(xp, h, comb, counts, pos, post, ranks_at, e_gu, e_dn, final_g)
            comb, h, _ = router(xs, norm_ffn[l], router_w[li], router_b[li])
            xs = moe(xs, h, comb, e_gu, e_dn, final_g)
    if depth % 2:
        xp, xs = rmsnorm(xp, norm_final), rmsnorm(xs, norm_final)
    y_prompt = xp.reshape(b, s, d)
    y_sample = xs.reshape(bd, 1, d)
    return (y_prompt, y_sample, jnp.stack(conv_p), jnp.stack(conv_s), jnp.stack(nsa_p), jnp.stack(nsa_s),
            jnp.stack(win_p), jnp.stack(win_s), jnp.stack(mc_p), jnp.stack(mc_s), jnp.stack(mn_p),
            jnp.stack(mn_s), jnp.stack(mm_p), jnp.stack(mm_s), jnp.stack(memkv_p))
```

```python
import functools
import math

import jax
import jax.numpy as jnp
from jax import lax
from jax.experimental import pallas as pl
from jax.experimental.pallas import tpu as pltpu

F32 = jnp.float32
BF16 = jnp.bfloat16

PAGE_SIZE = 128
CONV_WIDTH = 31
NSA_HEADS = 8
NSA_KV_HEADS = 2
NSA_GROUP = NSA_HEADS // NSA_KV_HEADS
NSA_HD = 64
CMP_BLOCK = 32
SEL_BLOCK = 64
SEL_RATIO = SEL_BLOCK // CMP_BLOCK
TOP_N = 16
WINDOW = 512
Q_BLOCK = 128
FORCE_SCORE = 1.0e4
NUM_BUCKETS = 32
MAX_DISTANCE = 1024
MLSTM_HEADS = 4
X_HEADS = 4
TOP_K = 2
RMS_EPS = 1e-6
LN_EPS = 1e-5
NEG = -1e30

LANES = 128
SUBLANES = 8
VMEM_LIMIT = 56 * 1024 * 1024
MLSTM_CHUNK = 256
SEL_TILES = 4
WIN_TILES = 5


def _cparams(sem, vmem=None):
    return pltpu.CompilerParams(dimension_semantics=sem, vmem_limit_bytes=vmem)


def _rms(x, g):
    return x * lax.rsqrt(jnp.mean(x * x, -1, keepdims=True) + RMS_EPS) * g


def _dot(a, b):
    return jnp.dot(a, b, preferred_element_type=F32)


def _dot_nt(a, b):
    return lax.dot_general(a, b, (((1,), (1,)), ((), ())), preferred_element_type=F32)


def _dot_tn(a, b):
    return lax.dot_general(a, b, (((0,), (0,)), ((), ())), preferred_element_type=F32)


def _full(shape):
    n = len(shape)
    return pl.BlockSpec(shape, lambda *_: (0,) * n)


def _row_tile(m, pref):
    t = min(pref, m)
    while m % t:
        t //= 2
    return t


def _rmsnorm_kernel(x_ref, g_ref, o_ref):
    o_ref[...] = _rms(x_ref[...], g_ref[...])


def rmsnorm(x, g):
    m, d = x.shape
    tm = _row_tile(m, 1024)
    return pl.pallas_call(
        _rmsnorm_kernel, grid=(m // tm,),
        in_specs=[pl.BlockSpec((tm, d), lambda i: (i, 0)), _full((1, d))],
        out_specs=pl.BlockSpec((tm, d), lambda i: (i, 0)),
        out_shape=jax.ShapeDtypeStruct((m, d), F32),
        compiler_params=_cparams(("parallel",)), name="rmsnorm",
    )(x, g.reshape(1, d))


def _outproj_kernel(*refs, n_in):
    x_ref = refs[0]
    a_refs = refs[1:1 + n_in]
    w_refs = refs[1 + n_in:1 + 2 * n_in]
    o_ref = refs[1 + 2 * n_in]
    acc = x_ref[...]
    for a_ref, w_ref in zip(a_refs, w_refs):
        acc = acc + _dot(a_ref[...], w_ref[...])
    o_ref[...] = acc


def outproj(x, acts, ws):
    m, d = x.shape
    tm = _row_tile(m, 512)
    n_in = len(acts)
    in_specs = [pl.BlockSpec((tm, d), lambda i: (i, 0))]
    in_specs += [pl.BlockSpec((tm, a.shape[1]), lambda i: (i, 0)) for a in acts]
    in_specs += [_full(w.shape) for w in ws]
    return pl.pallas_call(
        functools.partial(_outproj_kernel, n_in=n_in), grid=(m // tm,),
        in_specs=in_specs, out_specs=pl.BlockSpec((tm, d), lambda i: (i, 0)),
        out_shape=jax.ShapeDtypeStruct((m, d), F32),
        compiler_params=_cparams(("parallel",)), name="outproj",
    )(x, *acts, *ws)


def _inproj_even_kernel(x_ref, g_ref, w_ref, glu_ref, *rest, cc, qd, kvd, tiles):
    xn = _rms(x_ref[...], g_ref[...]).astype(BF16)

    def mm(lo, hi):
        return _dot(xn, w_ref[:, lo:hi])

    o = 0
    a = mm(o, o + cc)
    b = mm(o + cc, o + 2 * cc)
    glu_ref[...] = a * jax.nn.sigmoid(b)
    o += 2 * cc
    q = mm(o, o + qd) * (NSA_HD ** -0.5)
    o += qd
    kv03 = mm(o, o + 4 * kvd)
    o += 4 * kvd
    kv45 = mm(o, o + 2 * kvd)
    o += 2 * kvd
    gates = jax.nn.sigmoid(mm(o, o + LANES))
    if tiles == 0:
        kv03_ref, kv45_ref, q_ref, gate_ref = rest
        kv03_ref[...] = kv03
        kv45_ref[...] = kv45
        q_ref[...] = q.astype(BF16)
        gate_ref[...] = gates
        return
    keys_ref, qt_ref, vt_ref, gt_ref, kvt_ref = rest
    kvt_ref[0] = jnp.concatenate([kv03, kv45], axis=1).T
    keys_ref[...] = jnp.concatenate([kv03[:, 2 * kvd:3 * kvd], kv45[:, 0:kvd]], axis=1).astype(BF16)
    vals = jnp.concatenate([kv03[:, 3 * kvd:4 * kvd], kv45[:, kvd:2 * kvd]], axis=1)
    for j in range(tiles):
        rows = slice(j * Q_BLOCK, (j + 1) * Q_BLOCK)
        qt_ref[j] = q[rows, :].T.astype(BF16)
        vt_ref[j] = vals[rows, :].T.astype(BF16)
        gt_ref[j] = gates[rows, :].T


def inproj_even(x, g, w_pad, cc, seq=None):
    m, d = x.shape
    qd = NSA_HEADS * NSA_HD
    kvd = NSA_KV_HEADS * NSA_HD
    tm = _row_tile(m, 512)
    row = lambda n: pl.BlockSpec((tm, n), lambda i: (i, 0))
    out_specs = [row(cc)]
    out_shape = [jax.ShapeDtypeStruct((m, cc), F32)]
    transposed = seq is not None
    tiles = tm // Q_BLOCK if transposed else 0
    if transposed:
        assert tm % Q_BLOCK == 0 and seq % tm == 0
        per_seq = seq // tm
        tile = lambda n: pl.BlockSpec((tiles, n, Q_BLOCK), lambda i: (i, 0, 0))
        out_specs += [row(2 * kvd), tile(qd), tile(2 * kvd), tile(LANES),
                      pl.BlockSpec((1, 6 * kvd, tm), lambda i: (i // per_seq, 0, i % per_seq))]
        out_shape += [jax.ShapeDtypeStruct((m, 2 * kvd), BF16),
                      jax.ShapeDtypeStruct((m // Q_BLOCK, qd, Q_BLOCK), BF16),
                      jax.ShapeDtypeStruct((m // Q_BLOCK, 2 * kvd, Q_BLOCK), BF16),
                      jax.ShapeDtypeStruct((m // Q_BLOCK, LANES, Q_BLOCK), F32),
                      jax.ShapeDtypeStruct((m // seq, 6 * kvd, seq), F32)]
    else:
        out_specs += [row(4 * kvd), row(2 * kvd), row(qd), row(LANES)]
        out_shape += [jax.ShapeDtypeStruct((m, 4 * kvd), F32), jax.ShapeDtypeStruct((m, 2 * kvd), F32),
                      jax.ShapeDtypeStruct((m, qd), BF16), jax.ShapeDtypeStruct((m, LANES), F32)]
    return pl.pallas_call(
        functools.partial(_inproj_even_kernel, cc=cc, qd=qd, kvd=kvd, tiles=tiles), grid=(m // tm,),
        in_specs=[row(d), _full((1, d)), _full(w_pad.shape)],
        out_specs=out_specs, out_shape=out_shape,
        compiler_params=_cparams(("parallel",)), name="inproj_even",
    )(x, g.reshape(1, d), w_pad)


def _conv_post(y, lg, lb):
    mu = jnp.mean(y, -1, keepdims=True)
    var = jnp.mean(jnp.square(y - mu), -1, keepdims=True)
    yn = (y - mu) * lax.rsqrt(var + LN_EPS) * lg + lb
    return yn * jax.nn.sigmoid(yn)


CONV_SUB = 64
CONV_PAD = 32


def _conv_prompt_kernel(glu_ref, cw_ref, cb_ref, lg_ref, lb_ref, o_ref, ext_ref, y_ref, *, ts, s):
    i = pl.program_id(1)
    c = glu_ref.shape[-1]

    @pl.when(i == 0)
    def _():
        ext_ref[0:CONV_PAD, :] = jnp.zeros((CONV_PAD, c), F32)
        ext_ref[CONV_PAD:CONV_PAD + s, :] = glu_ref[0]
        ext_ref[CONV_PAD + s:CONV_PAD + s + SUBLANES, :] = jnp.zeros((SUBLANES, c), F32)

    lead = CONV_PAD - (CONV_WIDTH - 1)
    span = CONV_SUB + CONV_PAD

    def sub(j, carry):
        r0 = pl.multiple_of(i * ts + j * CONV_SUB, CONV_SUB)
        for c0 in range(0, c, LANES):
            xw = ext_ref[pl.ds(r0, span + SUBLANES), c0:c0 + LANES]
            acc = jnp.zeros((CONV_SUB, LANES), F32) + cb_ref[:, c0:c0 + LANES]
            for r in range(SUBLANES):
                xr = xw if r == 0 else pltpu.roll(xw, span + SUBLANES - r, 0)
                for a in range(span // SUBLANES):
                    w = SUBLANES * a + r - lead
                    if 0 <= w < CONV_WIDTH:
                        acc = acc + xr[SUBLANES * a:SUBLANES * a + CONV_SUB, :] * cw_ref[w:w + 1, c0:c0 + LANES]
            y_ref[:, c0:c0 + LANES] = acc
        o_ref[0, pl.ds(pl.multiple_of(j * CONV_SUB, CONV_SUB), CONV_SUB), :] = _conv_post(
            y_ref[...], lg_ref[...], lb_ref[...]).astype(o_ref.dtype)
        return carry

    lax.fori_loop(0, ts // CONV_SUB, sub, 0)


def conv_prompt(glu, cw, cb, lg, lb, b, s):
    c = glu.shape[-1]
    ts = _row_tile(s, 256)
    vec = lambda a: a.reshape(1, c)
    out = pl.pallas_call(
        functools.partial(_conv_prompt_kernel, ts=ts, s=s), grid=(b, s // ts),
        in_specs=[pl.BlockSpec((1, s, c), lambda bi, i: (bi, 0, 0)), _full((CONV_WIDTH, c)),
                  _full((1, c)), _full((1, c)), _full((1, c))],
        out_specs=pl.BlockSpec((1, ts, c), lambda bi, i: (bi, i, 0)),
        out_shape=jax.ShapeDtypeStruct((b, s, c), BF16),
        scratch_shapes=[pltpu.VMEM((CONV_PAD + s + SUBLANES, c), F32), pltpu.VMEM((CONV_SUB, c), F32)],
        compiler_params=_cparams(("parallel", "arbitrary")), name="conv_prompt",
    )(glu.reshape(b, s, c), cw, vec(cb), vec(lg), vec(lb))
    return out.reshape(b * s, c)


def _conv_sample_kernel(cache_ref, glu_ref, cw_ref, cb_ref, lg_ref, lb_ref, o_ref, st_ref):
    hist = CONV_WIDTH - 1
    cache = cache_ref[...]
    glu = glu_ref[...]
    y = jnp.sum(cache * cw_ref[0:hist, :][None], axis=1) + glu * cw_ref[hist:hist + 1, :] + cb_ref[...]
    o_ref[...] = _conv_post(y, lg_ref[...], lb_ref[...]).astype(o_ref.dtype)
    st_ref[:, 0:hist - 1, :] = cache[:, 1:hist, :]
    st_ref[:, hist - 1:hist, :] = glu[:, None, :]


def conv_sample(cache, glu, cw, cb, lg, lb):
    bd, hist, c = cache.shape
    vec = lambda a: a.reshape(1, c)
    return pl.pallas_call(
        _conv_sample_kernel,
        out_shape=[jax.ShapeDtypeStruct((bd, c), BF16), jax.ShapeDtypeStruct((bd, hist, c), F32)],
        name="conv_sample",
    )(cache, glu, cw, vec(cb), vec(lg), vec(lb))


def _rel_bucket(dist):
    n = jnp.maximum(dist, 0)
    max_exact = NUM_BUCKETS // 2
    nf = jnp.maximum(n, 1).astype(F32)
    large = max_exact + (jnp.log(nf / max_exact) / math.log(MAX_DISTANCE / max_exact)
                         * (NUM_BUCKETS - max_exact)).astype(jnp.int32)
    large = jnp.minimum(large, NUM_BUCKETS - 1)
    return jnp.where(n < max_exact, n, large)


def _bias_of(rel_bias, dist):
    bucket = _rel_bucket(dist)[..., None]
    out = jnp.zeros(bucket.shape[:-1] + (rel_bias.shape[1],), F32)
    for k in range(NUM_BUCKETS):
        out = jnp.where(bucket == k, rel_bias[k].astype(F32), out)
    return out


def _compress_accumulate(load_rows, pe_ref, w_ref, nc):
    accs = []
    half = CMP_BLOCK // 2
    for slot in range(2):
        acc = jnp.zeros((nc, w_ref.shape[-1]), F32)
        for j in range(half):
            pair = [load_rows(slot, jj) + pe_ref[slot, jj:jj + 1, :] for jj in (j, j + half)]
            acc = acc + _dot(jnp.concatenate(pair, axis=1).astype(BF16), w_ref[slot, j])
        accs.append(acc)
    return jnp.concatenate(accs, axis=1)


def _file_rows(tile_t, xrow, sl, page):
    rows = tile_t.T
    per_page = PAGE_SIZE // CMP_BLOCK
    for cl in range(per_page):
        for a in range(CMP_BLOCK // SUBLANES):
            r0 = cl * CMP_BLOCK + a * SUBLANES
            dst = pl.multiple_of((page * per_page + cl) * SUBLANES, SUBLANES)
            xrow[sl, a, pl.ds(dst, SUBLANES), :] = rows[r0:r0 + SUBLANES, :]


def _compress_filed(xrow, pe_ref, w_ref, nc):
    return _compress_accumulate(
        lambda sl, j: xrow[sl, j // SUBLANES, pl.ds(j % SUBLANES, nc, stride=SUBLANES), :], pe_ref, w_ref, nc)


def _compress_prompt_kernel(x_ref, pe_ref, w_ref, kc_ref, vct_ref, xrow, *, nc, ncp, kvd):
    for p in range(x_ref.shape[2] // PAGE_SIZE):
        for sl in range(2):
            _file_rows(x_ref[0, sl * kvd:(sl + 1) * kvd, p * PAGE_SIZE:(p + 1) * PAGE_SIZE], xrow, sl, p)
    acc = _compress_filed(xrow, pe_ref, w_ref, nc)
    if ncp > nc:
        acc = jnp.concatenate([acc, jnp.zeros((ncp - nc, 2 * kvd), F32)], axis=0)
    kc_ref[0] = acc[:, 0:kvd].astype(BF16)
    vct_ref[0] = acc[:, kvd:2 * kvd].T.astype(BF16)


def compress_prompt(kv_t, pe2, wbd, ncp):
    b, _, s = kv_t.shape
    assert s % PAGE_SIZE == 0
    nc = s // CMP_BLOCK
    kvd = wbd.shape[-1]
    return pl.pallas_call(
        functools.partial(_compress_prompt_kernel, nc=nc, ncp=ncp, kvd=kvd), grid=(b,),
        in_specs=[pl.BlockSpec((1, 2 * kvd, s), lambda bi: (bi, 0, 0)), _full(pe2.shape), _full(wbd.shape)],
        out_specs=[pl.BlockSpec((1, ncp, kvd), lambda bi: (bi, 0, 0)), pl.BlockSpec((1, kvd, ncp), lambda bi: (bi, 0, 0))],
        out_shape=[jax.ShapeDtypeStruct((b, ncp, kvd), BF16), jax.ShapeDtypeStruct((b, kvd, ncp), BF16)],
        scratch_shapes=[pltpu.VMEM((2, CMP_BLOCK // SUBLANES, nc * SUBLANES, kvd), F32)],
        compiler_params=_cparams(("parallel",)), name="compress_prompt",
    )(kv_t, pe2, wbd)


def _select_blocks(score, n_sel, n_cand):
    n = score.shape[1]
    col = jnp.broadcast_to(score, (LANES, n)).T[:, 0:1]
    i = lax.broadcasted_iota(jnp.int32, (n, 1), 0)
    lane = lax.broadcasted_iota(jnp.int32, (1, n), 1)
    beats = ((i % 2) == 0) & (i < 2 * n_cand) & ((col > score) | ((col == score) & (i < lane)))
    rank = jnp.sum(jnp.where(beats, 1.0, 0.0), axis=0, keepdims=True)
    is_cand = ((lane % 2) == 0) & (lane < 2 * n_cand)
    return (is_cand & (rank < n_sel) & (score >= 0)).astype(F32), rank


def _pair_sum(imp):
    n = imp.shape[1]
    return imp + pltpu.roll(imp, n - 1, 1)


def _rank_rows(score, n_sel, n_cand):
    blk = lax.broadcasted_iota(jnp.int32, score.shape, 0)
    rank = jnp.zeros(score.shape, F32)
    for i in range(n_cand):
        row = score[i:i + 1, :]
        beats = (row > score) | ((row == score) & (blk > i))
        rank = rank + beats.astype(F32)
    return ((rank < n_sel) & (score >= 0)).astype(F32)


def _nsa_prompt_kernel(qt_ref, gt_ref, kc_ref, vct_ref, keys_ref, vt_ref, biasc_ref, btile_ref, o_ref,
                       qt_scr, oc_scr, acc_s, acc_w, imp_scr, sel_scr, out_scr, *, nc, nsb, n_sel):
    qi = pl.program_id(1)
    g, hd, kvh, qb = NSA_GROUP, NSA_HD, NSA_KV_HEADS, Q_BLOCK
    kvd = kvh * hd
    ncp = kc_ref.shape[1]
    nsbp = sel_scr.shape[1]
    q_pos = qi * qb + lax.broadcasted_iota(jnp.int32, (1, qb), 1)
    key_row = lax.broadcasted_iota(jnp.int32, (qb, 1), 0)
    c_row = lax.broadcasted_iota(jnp.int32, (ncp, 1), 0)
    mask_c = (q_pos >= c_row * CMP_BLOCK + (CMP_BLOCK - 1)) & (c_row < nc)
    blk = lax.broadcasted_iota(jnp.int32, (nsbp, 1), 0)
    cur = q_pos // SEL_BLOCK
    forced = (blk == 0) | (blk == cur) | (blk == cur - 1)
    zeros = jnp.zeros((hd, qb), BF16)
    for k in range(kvh):
        for gi in range(g):
            h = k * g + gi
            parts = [zeros] * kvh
            parts[k] = qt_ref[0, h * hd:(h + 1) * hd, :]
            qt_scr[k, :, gi * qb:(gi + 1) * qb] = jnp.concatenate(parts, axis=0)
        s_c = _dot(kc_ref[0], qt_scr[k])
        imp = jnp.zeros((ncp, qb), F32)
        probs = []
        for gi in range(g):
            s = jnp.where(mask_c, s_c[:, gi * qb:(gi + 1) * qb] + biasc_ref[k * g + gi], NEG)
            e = jnp.where(mask_c, jnp.exp(s - jnp.max(s, 0, keepdims=True)), 0.0)
            den = jnp.sum(e, 0, keepdims=True)
            p = e / jnp.where(den > 0, den, 1.0)
            imp = imp + p
            probs.append(p.astype(BF16))
        oc_scr[k] = _dot(vct_ref[0, k * hd:(k + 1) * hd, :], jnp.concatenate(probs, axis=1))
        imp_scr[...] = imp + pltpu.roll(imp, ncp - 1, 0)
        cand = imp_scr[pl.ds(0, nsbp, stride=SEL_RATIO), :]
        score = jnp.where(forced, FORCE_SCORE, jnp.where(blk <= cur, cand, -1.0))
        sel_scr[k] = _rank_rows(jnp.where(blk < nsb, score, -2.0), n_sel, nsb)

    per_tile = qb // SEL_BLOCK
    n_tiles = keys_ref.shape[1] // qb
    first = ([jnp.full((1, qb), NEG, F32)] * g, [jnp.zeros((1, qb), F32)] * g)

    def tile_step(tiles, carry, key_col, val_row, acc_ref, window):
        kts = [jnp.clip(kt, 0, n_tiles - 1) for kt, _ in tiles]
        starts = [pl.multiple_of(kt * qb, qb) for kt in kts]
        k_t = jnp.concatenate([keys_ref[0, pl.ds(r0, qb), key_col:key_col + kvd] for r0 in starts], axis=0)
        dist = jnp.concatenate([jnp.where(active, q_pos - (r0 + key_row), -1)
                                for r0, (_, active) in zip(starts, tiles)], axis=0)
        in_range = dist >= 0
        scores = [_dot(k_t, qt_scr[k]) for k in range(kvh)]
        new, updates = [], []
        for k in range(kvh):
            if window:
                valid = in_range & (dist < WINDOW)
            else:
                pieces = []
                for kt in kts:
                    chosen = jnp.zeros((qb, qb), F32)
                    for j in range(per_tile):
                        row = sel_scr[k, pl.ds(per_tile * kt + j, 1), :]
                        chosen = jnp.where(key_row // SEL_BLOCK == j, row, chosen)
                    pieces.append(chosen)
                valid = in_range & (jnp.concatenate(pieces, axis=0) > 0.5)
            ms, ls = carry[k]
            ms2, ls2, alphas, probs = [], [], [], []
            for gi in range(g):
                bias = jnp.concatenate([btile_ref[jnp.maximum(qi - kt, 0), k * g + gi] for kt in kts], axis=0)
                s = jnp.where(valid, scores[k][:, gi * qb:(gi + 1) * qb] + bias, NEG)
                m_new = jnp.maximum(ms[gi], jnp.max(s, 0, keepdims=True))
                alpha = jnp.exp(ms[gi] - m_new)
                p = jnp.exp(s - jnp.where(m_new == NEG, 0.0, m_new))
                ms2.append(m_new)
                ls2.append(alpha * ls[gi] + jnp.sum(p, 0, keepdims=True))
                alphas.append(alpha)
                probs.append(p.astype(BF16))
            new.append((ms2, ls2))
            updates.append((jnp.concatenate(alphas, axis=1), jnp.concatenate(probs, axis=1)))
        for k, (alpha, prob) in enumerate(updates):
            v_t = jnp.concatenate([vt_ref[kt, val_row + k * hd:val_row + (k + 1) * hd, :] for kt in kts], axis=1)
            acc_ref[k] = acc_ref[k] * alpha + _dot(v_t, prob)
        return tuple(new)

    acc_s[...] = jnp.zeros_like(acc_s)
    acc_w[...] = jnp.zeros_like(acc_w)
    sel_args = dict(key_col=0, val_row=0, acc_ref=acc_s, window=False)
    win_args = dict(key_col=kvd, val_row=kvd, acc_ref=acc_w, window=True)

    def sel_group(i, carry):
        return tile_step([(SEL_TILES * i + j, SEL_TILES * i + j <= qi) for j in range(SEL_TILES)], carry, **sel_args)

    stat_s = lax.fori_loop(0, qi // SEL_TILES + 1, sel_group, (first,) * kvh)
    stat_w = (first,) * kvh
    win_tiles = [(qi - j, qi - j >= 0) for j in range(WINDOW // qb, -1, -1)]
    for j in range(0, len(win_tiles), WIN_TILES):
        stat_w = tile_step(win_tiles[j:j + WIN_TILES], stat_w, **win_args)
    gt = gt_ref[0]
    for k in range(kvh):
        for gi in range(g):
            h = k * g + gi
            cols = slice(gi * qb, (gi + 1) * qb)
            l_s, l_w = stat_s[k][1][gi], stat_w[k][1][gi]
            o_s = acc_s[k, :, cols] / jnp.where(l_s > 0, l_s, 1.0)
            o_w = acc_w[k, :, cols] / jnp.where(l_w > 0, l_w, 1.0)
            out_scr[h * hd:(h + 1) * hd, :] = (gt[3 * h:3 * h + 1, :] * oc_scr[k, :, cols]
                                               + gt[3 * h + 1:3 * h + 2, :] * o_s + gt[3 * h + 2:3 * h + 3, :] * o_w)
    o_ref[0] = out_scr[...].T.astype(o_ref.dtype)


def nsa_prompt(qt, gt, kc, vct, keys, vt, rel_bias, b, s):
    qb = Q_BLOCK
    nq = s // qb
    nc = s // CMP_BLOCK
    ncp = kc.shape[1]
    nsb = s // SEL_BLOCK
    nsbp = _round_up(nsb, SUBLANES)
    assert SEL_RATIO * nsbp <= ncp
    n_sel = min(TOP_N, nsb)
    nh = NSA_HEADS
    hq = qt.shape[1]
    kvd = kc.shape[2]
    glanes = NSA_GROUP * qb
    cend = jnp.arange(ncp)[:, None] * CMP_BLOCK + (CMP_BLOCK - 1)
    biasc = jnp.transpose(_bias_of(rel_bias, jnp.arange(s)[None, :] - cend), (2, 0, 1))
    r = jnp.arange(qb)
    dist = jnp.arange(nq)[:, None, None] * qb + r[None, None, :] - r[None, :, None]
    btile = jnp.transpose(_bias_of(rel_bias, dist), (0, 3, 1, 2))
    out = pl.pallas_call(
        functools.partial(_nsa_prompt_kernel, nc=nc, nsb=nsb, n_sel=n_sel), grid=(b, nq),
        in_specs=[pl.BlockSpec((1, hq, qb), lambda bi, i: (bi * nq + i, 0, 0)),
                  pl.BlockSpec((1, LANES, qb), lambda bi, i: (bi * nq + i, 0, 0)),
                  pl.BlockSpec((1, ncp, kvd), lambda bi, i: (bi, 0, 0)),
                  pl.BlockSpec((1, kvd, ncp), lambda bi, i: (bi, 0, 0)),
                  pl.BlockSpec((1, s, 2 * kvd), lambda bi, i: (bi, 0, 0)),
                  pl.BlockSpec((nq, 2 * kvd, qb), lambda bi, i: (bi, 0, 0)),
                  pl.BlockSpec((nh, ncp, qb), lambda bi, i: (0, 0, i)),
                  _full(btile.shape)],
        out_specs=pl.BlockSpec((1, qb, hq), lambda bi, i: (bi, i, 0)),
        out_shape=jax.ShapeDtypeStruct((b, s, hq), BF16),
        scratch_shapes=[pltpu.VMEM((NSA_KV_HEADS, kvd, glanes), BF16), pltpu.VMEM((NSA_KV_HEADS, NSA_HD, glanes), F32),
                        pltpu.VMEM((NSA_KV_HEADS, NSA_HD, glanes), F32), pltpu.VMEM((NSA_KV_HEADS, NSA_HD, glanes), F32),
                        pltpu.VMEM((ncp, qb), F32), pltpu.VMEM((NSA_KV_HEADS, nsbp, qb), F32),
                        pltpu.VMEM((hq, qb), F32)],
        compiler_params=_cparams(("parallel", "arbitrary"), VMEM_LIMIT), name="nsa_prompt",
    )(qt, gt, kc, vct, keys.reshape(b, s, 2 * kvd), vt, biasc, btile)
    return out.reshape(b * s, hq)


def _round_up(x, m):
    return (x + m - 1) // m * m


def prep_even(w_in, pe, wc):
    d, n = w_in.shape
    n_pad = _round_up(n - 3 * NSA_HEADS, LANES) + LANES
    w_pad = jnp.zeros((d, n_pad), BF16).at[:, :n].set(w_in.astype(BF16))
    pe2 = jnp.tile(pe, (1, 1, NSA_KV_HEADS))
    zero = jnp.zeros_like(wc)
    wbd = jnp.concatenate([jnp.concatenate([wc if i == j else zero for j in range(NSA_KV_HEADS)], axis=-1)
                           for i in range(NSA_KV_HEADS)], axis=-2)
    half = CMP_BLOCK // 2
    wbd = jnp.concatenate([wbd[:, :half], wbd[:, half:]], axis=2)
    return dict(w_in=w_pad, pe2=pe2, wbd=wbd.astype(BF16))


def _memkv_kernel(x_ref, g_ref, w_ref, o_ref, ob_ref, *, hd, nh):
    y = _dot(_rms(x_ref[...], g_ref[...]).astype(BF16), w_ref[...])
    ob_ref[...] = y.astype(BF16)
    tm = y.shape[0]
    chunks = hd // LANES
    period = 2 * chunks * nh
    for kv in range(2):
        for h in range(nh):
            for c in range(chunks):
                col = (kv * nh + h) * hd + c * LANES
                o_ref[pl.ds((kv * chunks + c) * nh + h, tm, stride=period), :] = y[:, col:col + LANES]


def memkv(mem, g, w, nh):
    m, d = mem.shape
    n = w.shape[1]
    hd = n // (2 * nh)
    per_tok = n // LANES
    tm = _row_tile(m, 256)
    return pl.pallas_call(
        functools.partial(_memkv_kernel, hd=hd, nh=nh), grid=(m // tm,),
        in_specs=[pl.BlockSpec((tm, d), lambda i: (i, 0)), _full((1, d)), _full(w.shape)],
        out_specs=[pl.BlockSpec((tm * per_tok, LANES), lambda i: (i, 0)), pl.BlockSpec((tm, n), lambda i: (i, 0))],
        out_shape=[jax.ShapeDtypeStruct((m * per_tok, LANES), F32), jax.ShapeDtypeStruct((m, n), BF16)],
        compiler_params=_cparams(("parallel",)), name="memkv",
    )(mem, g.reshape(1, d), w)


def _xattn_core(q, kv, hd):
    nh = q.shape[1] // hd
    outs = []
    for h in range(nh):
        s = _dot_nt(q[:, h * hd:(h + 1) * hd], kv[:, h * hd:(h + 1) * hd])
        e = jnp.exp(s - jnp.max(s, -1, keepdims=True))
        p = e / jnp.sum(e, -1, keepdims=True)
        outs.append(_dot(p.astype(BF16), kv[:, (nh + h) * hd:(nh + h + 1) * hd]))
    return jnp.concatenate(outs, axis=1).astype(BF16)


def _xattn_prompt_kernel(*refs, hd, n_in):
    x_ref, g_ref, wq_ref, kv_ref, wo_ref = refs[:5]
    a_refs = refs[5:5 + n_in]
    w_refs = refs[5 + n_in:5 + 2 * n_in]
    o_ref = refs[5 + 2 * n_in]
    x = x_ref[0]
    for a_ref, w_ref in zip(a_refs, w_refs):
        x = x + _dot(a_ref[0], w_ref[...])
    q = (_dot(_rms(x, g_ref[...]).astype(BF16), wq_ref[...]) * (hd ** -0.5)).astype(BF16)
    o = _xattn_core(q, kv_ref[0], hd)
    o_ref[0] = x + _dot(o, wo_ref[...])


def xattn_prompt(x, g, wq, kvb, wo, b, s, acts=(), ws=()):
    d = x.shape[1]
    mt = kvb.shape[0] // b
    tm = _row_tile(s, 512)
    tile = lambda n: pl.BlockSpec((1, tm, n), lambda bi, i: (bi, i, 0))
    out = pl.pallas_call(
        functools.partial(_xattn_prompt_kernel, hd=d // X_HEADS, n_in=len(acts)), grid=(b, s // tm),
        in_specs=[tile(d), _full((1, d)), _full(wq.shape),
                  pl.BlockSpec((1, mt, kvb.shape[1]), lambda bi, i: (bi, 0, 0)), _full(wo.shape)]
        + [tile(a.shape[1]) for a in acts] + [_full(w.shape) for w in ws],
        out_specs=tile(d),
        out_shape=jax.ShapeDtypeStruct((b, s, d), F32),
        compiler_params=_cparams(("parallel", "parallel"), VMEM_LIMIT), name="xattn_prompt",
    )(x.reshape(b, s, d), g.reshape(1, d), wq, kvb.reshape(b, mt, -1), wo,
      *[a.reshape(b, s, -1) for a in acts], *ws)
    return out.reshape(b * s, d)


def _xattn_sample_kernel(x_ref, g_ref, wq_ref, kv_ref, wo_ref, o_ref, q_scr, a_scr, *, hd, mt):
    bi = pl.program_id(0)
    nb = pl.num_programs(0)
    nh = wq_ref.shape[1] // hd

    @pl.when(bi == 0)
    def _():
        q_scr[...] = _dot(_rms(x_ref[...], g_ref[...]).astype(BF16), wq_ref[...]) * (hd ** -0.5)

    q = jnp.broadcast_to(q_scr[pl.ds(bi, 1), :], (SUBLANES, q_scr.shape[1])).astype(BF16)
    chunks = hd // LANES
    period = 2 * chunks * nh

    def head_rows(kv, h):
        return jnp.concatenate([kv_ref[0, pl.ds((kv * chunks + c) * nh + h, mt, stride=period), :]
                                for c in range(chunks)], axis=1).astype(BF16)

    outs = []
    for h in range(nh):
        s = _dot_nt(q[:, h * hd:(h + 1) * hd], head_rows(0, h))
        e = jnp.exp(s - jnp.max(s, -1, keepdims=True))
        p = e / jnp.sum(e, -1, keepdims=True)
        outs.append(_dot(p.astype(BF16), head_rows(1, h)))
    a_scr[pl.ds(bi, 1), :] = jnp.concatenate(outs, axis=1)[0:1, :]

    @pl.when(bi == nb - 1)
    def _():
        o_ref[...] = x_ref[...] + _dot(a_scr[...].astype(BF16), wo_ref[...])


def xattn_sample(x, g, wq, kv_rows, wo, layer, mt):
    bd, d = x.shape
    rows = kv_rows.shape[1]
    return pl.pallas_call(
        functools.partial(_xattn_sample_kernel, hd=d // X_HEADS, mt=mt), grid=(bd,),
        in_specs=[_full((bd, d)), _full((1, d)), _full(wq.shape),
                  pl.BlockSpec((1, rows, LANES), lambda bi: (layer * bd + bi, 0, 0)), _full(wo.shape)],
        out_specs=_full((bd, d)),
        out_shape=jax.ShapeDtypeStruct((bd, d), F32),
        scratch_shapes=[pltpu.VMEM((bd, wq.shape[1]), F32), pltpu.VMEM((bd, wq.shape[1]), F32)],
        compiler_params=_cparams(("arbitrary",), VMEM_LIMIT), name="xattn_sample",
    )(x, g.reshape(1, d), wq, kv_rows, wo)


def _ffn_kernel(x_ref, g_ref, wg_ref, wu_ref, wd_ref, o_ref, h_scr, acc_scr):
    c = pl.program_id(1)

    @pl.when(c == 0)
    def _():
        h_scr[...] = _rms(x_ref[...], g_ref[...]).astype(BF16)
        acc_scr[...] = x_ref[...]

    h = h_scr[...]
    gate = _dot(h, wg_ref[...])
    up = _dot(h, wu_ref[...])
    act = (gate * jax.nn.sigmoid(gate) * up).astype(BF16)
    acc_scr[...] += _dot(act, wd_ref[...])

    @pl.when(c == pl.num_programs(1) - 1)
    def _():
        o_ref[...] = acc_scr[...]


def _xattn_ffn_kernel(*refs, hd, n_in):
    x_ref, gx_ref, wq_ref, kv_ref, wo_ref, gf_ref, wg_ref, wu_ref, wd_ref = refs[:9]
    a_refs = refs[9:9 + n_in]
    w_refs = refs[9 + n_in:9 + 2 * n_in]
    o_ref, h_scr, acc_scr = refs[9 + 2 * n_in:]
    c = pl.program_id(1)

    @pl.when(c == 0)
    def _():
        x = x_ref[...]
        for a_ref, w_ref in zip(a_refs, w_refs):
            x = x + _dot(a_ref[...], w_ref[...])
        q = (_dot(_rms(x, gx_ref[...]).astype(BF16), wq_ref[...]) * (hd ** -0.5)).astype(BF16)
        x = x + _dot(_xattn_core(q, kv_ref[0], hd), wo_ref[...])
        h_scr[...] = _rms(x, gf_ref[...]).astype(BF16)
        acc_scr[...] = x

    h = h_scr[...]
    gate = _dot(h, wg_ref[...])
    up = _dot(h, wu_ref[...])
    act = (gate * jax.nn.sigmoid(gate) * up).astype(BF16)
    acc_scr[...] += _dot(act, wd_ref[...])

    @pl.when(c == pl.num_programs(1) - 1)
    def _():
        o_ref[...] = acc_scr[...]


def xattn_ffn(x, gx, wq, kvb, wo, acts, ws, gf, w_gu, w_dn, b, s):
    m, d = x.shape
    dff = w_dn.shape[0]
    mt = kvb.shape[0] // b
    tm = _row_tile(s, 512)
    per_seq = s // tm
    fc = _ff_chunk(dff, 1408)
    nch = dff // fc
    row = lambda n: pl.BlockSpec((tm, n), lambda i, c: (i, 0))
    return pl.pallas_call(
        functools.partial(_xattn_ffn_kernel, hd=d // X_HEADS, n_in=len(acts)), grid=(m // tm, nch),
        in_specs=[row(d), _full((1, d)), _full(wq.shape),
                  pl.BlockSpec((1, mt, kvb.shape[1]), lambda i, c: (i // per_seq, 0, 0)), _full(wo.shape),
                  _full((1, d)),
                  pl.BlockSpec((d, fc), lambda i, c: (0, c)),
                  pl.BlockSpec((d, fc), lambda i, c: (0, nch + c)),
                  pl.BlockSpec((fc, d), lambda i, c: (c, 0))]
        + [row(a.shape[1]) for a in acts] + [_full(w.shape) for w in ws],
        out_specs=row(d),
        out_shape=jax.ShapeDtypeStruct((m, d), F32),
        scratch_shapes=[pltpu.VMEM((tm, d), BF16), pltpu.VMEM((tm, d), F32)],
        compiler_params=_cparams(("parallel", "arbitrary"), VMEM_LIMIT), name="xattn_ffn",
    )(x, gx.reshape(1, d), wq, kvb.reshape(b, mt, -1), wo, gf.reshape(1, d), w_gu, w_gu, w_dn, *acts, *ws)


def _ff_chunk(dff, pref):
    c = dff
    for n in range(1, dff // LANES + 1):
        if dff % n == 0 and (dff // n) % LANES == 0 and dff // n <= pref:
            c = dff // n
            break
    return c


def ffn(x, g, w_gu, w_dn):
    m, d = x.shape
    dff = w_dn.shape[0]
    tm = _row_tile(m, 512)
    fc = _ff_chunk(dff, 1408)
    nch = dff // fc
    return pl.pallas_call(
        _ffn_kernel, grid=(m // tm, nch),
        in_specs=[pl.BlockSpec((tm, d), lambda i, c: (i, 0)), _full((1, d)),
                  pl.BlockSpec((d, fc), lambda i, c: (0, c)),
                  pl.BlockSpec((d, fc), lambda i, c: (0, nch + c)),
                  pl.BlockSpec((fc, d), lambda i, c: (c, 0))],
        out_specs=pl.BlockSpec((tm, d), lambda i, c: (i, 0)),
        out_shape=jax.ShapeDtypeStruct((m, d), F32),
        scratch_shapes=[pltpu.VMEM((tm, d), BF16), pltpu.VMEM((tm, d), F32)],
        compiler_params=_cparams(("parallel", "arbitrary"), VMEM_LIMIT), name="ffn",
    )(x, g.reshape(1, d), w_gu, w_gu, w_dn)


def _router_kernel(x_ref, g_ref, w_ref, b_ref, *refs, ne, blocks):
    if blocks:
        comb_ref, h_ref, cnt_ref, tri_ref, pos_ref, post_ref, stab_ref = refs
    else:
        comb_ref, h_ref, cnt_ref = refs
    h = _rms(x_ref[...], g_ref[...]).astype(BF16)
    h_ref[...] = h
    logits = _dot(h, w_ref[...]) + b_ref[...]
    lane = lax.broadcasted_iota(jnp.int32, logits.shape, 1)
    logits = jnp.where(lane < ne, logits, -jnp.inf)
    v1 = jnp.max(logits, -1, keepdims=True)
    i1 = jnp.min(jnp.where(logits == v1, lane, LANES), -1, keepdims=True)
    rest = jnp.where(lane == i1, -jnp.inf, logits)
    v2 = jnp.max(rest, -1, keepdims=True)
    i2 = jnp.min(jnp.where(rest == v2, lane, LANES), -1, keepdims=True)
    e2 = jnp.exp(v2 - v1)
    den = 1.0 + e2
    comb_ref[...] = jnp.where(lane == i1, 1.0 / den, 0.0) + jnp.where(lane == i2, e2 / den, 0.0)
    chosen = jnp.where((lane == i1) | (lane == i2), 1.0, 0.0)
    step = pl.program_id(0)

    @pl.when(step == 0)
    def _():
        cnt_ref[...] = jnp.zeros_like(cnt_ref)
        if blocks:
            stab_ref[...] = jnp.zeros_like(stab_ref)

    if blocks:
        t = MOE_TILE
        before = cnt_ref[0:1, :]
        rank = before + _dot(tri_ref[...], chosen.astype(BF16))
        pos = jnp.where(chosen > 0, rank, -1.0)
        pos_ref[...] = pos.astype(jnp.int32)
        pos_t = pos.T
        nep = post_ref.shape[1]
        for j in range(blocks):
            post_ref[j] = pos_t[0:nep, j * t:(j + 1) * t].astype(jnp.int32)
            start = before + jnp.sum(chosen[0:j * t, :], axis=0, keepdims=True) if j else before
            stab_ref[pl.ds(step * blocks + j, 1), :] = start.astype(jnp.int32)

    cnt_ref[0:1, :] += jnp.sum(chosen, axis=0, keepdims=True)

    if blocks:
        @pl.when(step == pl.num_programs(0) - 1)
        def _():
            stab_ref[pl.ds(pl.num_programs(0) * blocks, 1), :] = cnt_ref[0:1, :].astype(jnp.int32)


def router(x, g, w_r, b_r, grouped=False):
    m, d = x.shape
    ne = w_r.shape[1]
    w_pad = jnp.zeros((d, LANES), BF16).at[:, :ne].set(w_r.astype(BF16))
    b_pad = jnp.zeros((1, LANES), F32).at[0, :ne].set(b_r.astype(F32))
    tm = _row_tile(m, 512)
    row = lambda n: pl.BlockSpec((tm, n), lambda i: (i, 0))
    in_specs = [row(d), _full((1, d)), _full((d, LANES)), _full((1, LANES))]
    out_specs = [row(LANES), row(d), _full((SUBLANES, LANES))]
    out_shape = [jax.ShapeDtypeStruct((m, LANES), F32), jax.ShapeDtypeStruct((m, d), BF16),
                 jax.ShapeDtypeStruct((SUBLANES, LANES), F32)]
    args = [x, g.reshape(1, d), w_pad, b_pad]
    blocks = 0
    if grouped:
        assert tm % MOE_TILE == 0
        blocks = tm // MOE_TILE
        nb = m // MOE_TILE
        nep = _round_up(ne, SUBLANES)
        nbp = _round_up(nb + 1, SUBLANES)
        idx = lax.broadcasted_iota(jnp.int32, (tm, tm), 0)
        args.append(jnp.where(lax.broadcasted_iota(jnp.int32, (tm, tm), 1) < idx, 1.0, 0.0).astype(BF16))
        in_specs.append(_full((tm, tm)))
        out_specs += [row(LANES), pl.BlockSpec((blocks, nep, MOE_TILE), lambda i: (i, 0, 0)), _full((nbp, LANES))]
        out_shape += [jax.ShapeDtypeStruct((m, LANES), jnp.int32),
                      jax.ShapeDtypeStruct((nb, nep, MOE_TILE), jnp.int32),
                      jax.ShapeDtypeStruct((nbp, LANES), jnp.int32)]
    def body(x_ref, g_ref, w_ref, b_ref, *rest):
        if grouped:
            tri_ref, comb_ref, h_ref, cnt_ref, pos_ref, post_ref, stab_ref = rest
            _router_kernel(x_ref, g_ref, w_ref, b_ref, comb_ref, h_ref, cnt_ref, tri_ref, pos_ref, post_ref, stab_ref,
                           ne=ne, blocks=blocks)
        else:
            _router_kernel(x_ref, g_ref, w_ref, b_ref, *rest, ne=ne, blocks=0)
    return pl.pallas_call(
        body, grid=(m // tm,), in_specs=in_specs, out_specs=out_specs, out_shape=out_shape,
        compiler_params=_cparams(("arbitrary",)), name="router",
    )(*args)


def _residual_out(y, gain_ref, norm):
    return _rms(y, gain_ref[...]) if norm else y


def _moe_kernel(x_ref, h_ref, comb_ref, wg_ref, wu_ref, wd_ref, gain_ref, o_ref, acc_scr, *, norm):
    e = pl.program_id(1)

    @pl.when(e == 0)
    def _():
        acc_scr[...] = jnp.zeros_like(acc_scr)

    h = h_ref[...]
    gate = _dot(h, wg_ref[0])
    up = _dot(h, wu_ref[0])
    act = (gate * jax.nn.sigmoid(gate) * up).astype(BF16)
    y = _dot(act, wd_ref[0])
    comb = comb_ref[...]
    lane = lax.broadcasted_iota(jnp.int32, comb.shape, 1)
    acc_scr[...] += jnp.sum(jnp.where(lane == e, comb, 0.0), -1, keepdims=True) * y

    @pl.when(e == pl.num_programs(1) - 1)
    def _():
        o_ref[...] = _residual_out(x_ref[...] + acc_scr[...], gain_ref, norm)


def moe(x, h, comb, w_gu, w_dn, final_g=None):
    m, d = x.shape
    ne, dfe = w_dn.shape[:2]
    tm = _row_tile(m, 512)
    gain = jnp.ones((1, d), F32) if final_g is None else final_g.reshape(1, d)
    return pl.pallas_call(
        functools.partial(_moe_kernel, norm=final_g is not None), grid=(m // tm, ne),
        in_specs=[pl.BlockSpec((tm, d), lambda i, e: (i, 0)), pl.BlockSpec((tm, d), lambda i, e: (i, 0)),
                  pl.BlockSpec((tm, LANES), lambda i, e: (i, 0)),
                  pl.BlockSpec((1, d, dfe), lambda i, e: (e, 0, 0)),
                  pl.BlockSpec((1, d, dfe), lambda i, e: (e, 0, 1)),
                  pl.BlockSpec((1, dfe, d), lambda i, e: (e, 0, 0)), _full((1, d))],
        out_specs=pl.BlockSpec((tm, d), lambda i, e: (i, 0)),
        out_shape=jax.ShapeDtypeStruct((m, d), F32),
        scratch_shapes=[pltpu.VMEM((tm, d), F32)],
        compiler_params=_cparams(("parallel", "arbitrary"), VMEM_LIMIT), name="moe",
    )(x, h, comb, w_gu, w_gu, w_dn, gain)


MOE_TILE = 256


def _moe_expert_kernel(te_ref, lo_ref, cnt_ref, ring_ref, tb_ref, nt_ref, h_ref, post_ref, wg_ref, wu_ref, wd_ref,
                       y_ref, hbuf, xg_scr, sem, *, t):
    i = pl.program_id(0)

    @pl.when(i >= nt_ref[0])
    def _():
        y_ref[...] = jnp.zeros_like(y_ref)

    nbuf = hbuf.shape[0]
    ahead = nbuf - 1

    def copy(sb, slot):
        return pltpu.make_async_copy(h_ref.at[pl.ds(pl.multiple_of(sb * t, t), t), :], hbuf.at[slot], sem.at[slot])

    def start_first(tile):
        for j in range(ahead):
            @pl.when(j < cnt_ref[tile])
            def _():
                copy(lo_ref[tile] + j, (ring_ref[tile] + j) % nbuf).start()

    @pl.when(i == 0)
    def _():
        start_first(0)

    @pl.when(i < nt_ref[0])
    def _():
        e = te_ref[i]
        lo = lo_ref[i]
        n = cnt_ref[i]
        ring = ring_ref[i]
        xg_scr[...] = jnp.zeros_like(xg_scr)
        row = i * t - tb_ref[i] + lax.broadcasted_iota(jnp.int32, (t, 1), 0)

        def body(j, c):
            slot = (ring + j) % nbuf
            copy(lo + j, slot).wait()

            @pl.when(j + ahead < n)
            def _():
                copy(lo + j + ahead, (ring + j + ahead) % nbuf).start()

            src_pos = post_ref[lo + j, pl.ds(e, 1), :]
            onehot = jnp.where(src_pos == row, 1.0, 0.0).astype(BF16)
            xg_scr[...] += _dot(onehot, hbuf[slot])
            return c

        lax.fori_loop(0, n, body, 0)

        @pl.when(i + 1 < nt_ref[0])
        def _():
            start_first(i + 1)

        x = xg_scr[...].astype(BF16)
        gate = _dot(x, wg_ref[0])
        up = _dot(x, wu_ref[0])
        act = (gate * jax.nn.sigmoid(gate) * up).astype(BF16)
        y_ref[...] = _dot(act, wd_ref[0])


MOE_WIN = 128


def _moe_combine_kernel(stab_ref, base_ref, x_ref, comb_ref, pos_ref, gain_ref, ys_ref, o_ref, ybuf, extra, acc_scr,
                        sem, sem_x, *, t, ne, norm):
    sb = pl.program_id(0)
    nb = pl.num_programs(0)
    n_win = (t + SUBLANES - 1) // MOE_WIN + 1

    def window(b, e, w):
        s0 = stab_ref[b * ne + e]
        s1 = stab_ref[(b + 1) * ne + e]
        start = (s0 // SUBLANES) * SUBLANES + w * MOE_WIN
        return pl.multiple_of(start, SUBLANES), start < s1

    def first_copy(b, slot, e):
        start, _ = window(b, e, 0)
        return pltpu.make_async_copy(ys_ref.at[pl.ds(start, MOE_WIN), :], ybuf.at[slot, e], sem.at[slot, e])

    def start_all(b, slot):
        for e in range(ne):
            @pl.when(window(b, e, 0)[1])
            def _():
                first_copy(b, slot, e).start()

    @pl.when(sb == 0)
    def _():
        ybuf[...] = jnp.zeros_like(ybuf)
        start_all(0, 0)

    @pl.when(sb + 1 < nb)
    def _():
        start_all(sb + 1, (sb + 1) % 2)

    slot = sb % 2
    comb = comb_ref[...]
    pos = pos_ref[...]
    lane_r = lax.broadcasted_iota(jnp.int32, (1, MOE_WIN), 1)

    def split(a):
        hi = a.astype(BF16)
        return hi, (a - hi.astype(F32)).astype(BF16)

    def weighted_onehot(start, e):
        return jnp.where(pos[:, e:e + 1] - (start - base_ref[e]) == lane_r, comb[:, e:e + 1], 0.0)

    def gathered(sel, rows):
        s_hi, s_lo = split(sel)
        r_hi, r_lo = split(rows)
        return _dot(s_hi, r_hi) + _dot(s_hi, r_lo) + _dot(s_lo, r_hi)

    for e in range(ne):
        @pl.when(window(sb, e, 0)[1])
        def _():
            first_copy(sb, slot, e).wait()
    sel = jnp.concatenate([weighted_onehot(window(sb, e, 0)[0], e) for e in range(ne)], axis=1)
    acc_scr[...] = x_ref[...] + gathered(sel, ybuf[slot].reshape(ne * MOE_WIN, ybuf.shape[-1]))

    for e in range(ne):
        for w in range(1, n_win):
            start_w, needed_w = window(sb, e, w)

            @pl.when(needed_w)
            def _():
                cp = pltpu.make_async_copy(ys_ref.at[pl.ds(start_w, MOE_WIN), :], extra, sem_x.at[0])
                cp.start()
                cp.wait()
                acc_scr[...] += gathered(weighted_onehot(start_w, e), extra[...])

    o_ref[...] = _residual_out(acc_scr[...], gain_ref, norm)


def moe_grouped(x, h, comb, counts, pos, post, ranks_at, w_gu, w_dn, final_g=None):
    m, d = x.shape
    ne, dfe = w_dn.shape[:2]
    t = MOE_TILE
    assert m % t == 0
    nb = m // t
    nep = post.shape[1]
    k_top = TOP_K
    nt_max = k_top * m // t + ne + 1
    cnt = counts[0, :ne].astype(jnp.int32)
    cnt_pad = (cnt + t - 1) // t * t
    ends = jnp.cumsum(cnt_pad)
    base = (ends - cnt_pad).astype(jnp.int32)
    stab = ranks_at[:nb + 1, :ne] + base[None, :]
    r0 = jnp.arange(nt_max, dtype=jnp.int32) * t
    tile_e = jnp.minimum(jnp.sum(ends[None, :] <= r0[:, None], axis=1), ne - 1).astype(jnp.int32)
    n_tiles = (ends[-1] // t).astype(jnp.int32).reshape(1)
    s_e = stab[:, tile_e]
    lo = jnp.sum(s_e[1:] <= r0[None, :], axis=0)
    hi = jnp.sum(s_e[:nb] < r0[None, :] + t, axis=0) - 1
    lo = jnp.clip(lo, 0, nb - 1).astype(jnp.int32)
    hi = jnp.clip(hi, lo, nb - 1).astype(jnp.int32)
    n_src = jnp.where(jnp.arange(nt_max) < n_tiles[0], hi - lo + 1, 0).astype(jnp.int32)
    n_ring = 8
    ring = ((jnp.cumsum(n_src) - n_src) % n_ring).astype(jnp.int32)
    w_spec = lambda shape, col: pl.BlockSpec(shape, lambda i, te, *_: (te[i], 0, col))
    tile_base = base[tile_e]
    grid_spec = pltpu.PrefetchScalarGridSpec(
        num_scalar_prefetch=6, grid=(nt_max,),
        in_specs=[pl.BlockSpec(memory_space=pl.ANY),
                  pl.BlockSpec((nb, nep, t), lambda i, *_: (0, 0, 0)),
                  w_spec((1, d, dfe), 0), w_spec((1, d, dfe), 1), w_spec((1, dfe, d), 0)],
        out_specs=pl.BlockSpec((t, d), lambda i, *_: (i, 0)),
        scratch_shapes=[pltpu.VMEM((n_ring, t, d), BF16), pltpu.VMEM((t, d), F32),
                        pltpu.SemaphoreType.DMA((n_ring,))])
    ys = pl.pallas_call(
        functools.partial(_moe_expert_kernel, t=t), grid_spec=grid_spec,
        out_shape=jax.ShapeDtypeStruct((nt_max * t, d), F32),
        compiler_params=_cparams(("arbitrary",), VMEM_LIMIT), name="moe_experts",
    )(tile_e, lo, n_src, ring, tile_base, n_tiles, h, post, w_gu, w_gu, w_dn)
    gain = jnp.ones((1, d), F32) if final_g is None else final_g.reshape(1, d)
    blk2 = lambda n2: pl.BlockSpec((t, n2), lambda i, *_: (i, 0))
    grid_spec = pltpu.PrefetchScalarGridSpec(
        num_scalar_prefetch=2, grid=(nb,),
        in_specs=[blk2(d), blk2(LANES), blk2(LANES), pl.BlockSpec((1, d), lambda i, *_: (0, 0)),
                  pl.BlockSpec(memory_space=pl.ANY)],
        out_specs=blk2(d),
        scratch_shapes=[pltpu.VMEM((2, ne, MOE_WIN, d), F32), pltpu.VMEM((MOE_WIN, d), F32), pltpu.VMEM((t, d), F32),
                        pltpu.SemaphoreType.DMA((2, ne)), pltpu.SemaphoreType.DMA((1,))])
    return pl.pallas_call(
        functools.partial(_moe_combine_kernel, t=t, ne=ne, norm=final_g is not None), grid_spec=grid_spec,
        out_shape=jax.ShapeDtypeStruct((m, d), F32),
        compiler_params=_cparams(("arbitrary",), VMEM_LIMIT), name="moe_combine",
    )(stab.reshape(-1), base, x, comb, pos, gain, ys)


def _inproj_odd_kernel(x_ref, g_ref, w_ref, bias_ref, q_ref, k_ref, v_ref, og_ref, gc_ref, gr_ref, *, hq, hv, nh):
    xn = _rms(x_ref[...], g_ref[...]).astype(BF16)

    def mm(lo, hi):
        return _dot(xn, w_ref[:, lo:hi])

    dk = hq // nh
    q_ref[...] = mm(0, hq).astype(BF16)
    k_ref[...] = (mm(hq, 2 * hq) * (dk ** -0.5)).astype(BF16)
    v_ref[...] = mm(2 * hq, 2 * hq + hv).astype(BF16)
    og_ref[...] = jax.nn.sigmoid(mm(2 * hq + hv, 2 * hq + 2 * hv))
    gi = mm(2 * hq + 2 * hv, 2 * hq + 2 * hv + LANES) + bias_ref[...]
    lane = lax.broadcasted_iota(jnp.int32, gi.shape, 1)
    gates = jnp.where(lane < nh, gi, jax.nn.log_sigmoid(gi))
    gc_ref[...] = gates
    gr_ref[...] = gates.T[0:SUBLANES, :]


def inproj_odd(x, g, w_pad, gate_bias):
    m, d = x.shape
    nh = MLSTM_HEADS
    hq = hv = d
    tm = _row_tile(m, 512)
    row = lambda n: pl.BlockSpec((tm, n), lambda i: (i, 0))
    outs = [(hq, BF16), (hq, BF16), (hv, BF16), (hv, F32), (LANES, F32)]
    if tm % LANES:
        gr_spec = _full((SUBLANES, m))
    else:
        gr_spec = pl.BlockSpec((SUBLANES, tm), lambda i: (0, i))
    return pl.pallas_call(
        functools.partial(_inproj_odd_kernel, hq=hq, hv=hv, nh=nh), grid=(m // tm,),
        in_specs=[row(d), _full((1, d)), _full(w_pad.shape), _full((1, LANES))],
        out_specs=[row(n) for n, _ in outs] + [gr_spec],
        out_shape=[jax.ShapeDtypeStruct((m, n), t) for n, t in outs] + [jax.ShapeDtypeStruct((SUBLANES, m), F32)],
        compiler_params=_cparams(("parallel",), VMEM_LIMIT), name="inproj_odd",
    )(x, g.reshape(1, d), w_pad, gate_bias)


def prep_odd(w_in, b_i, b_f):
    d, n = w_in.shape
    n_pad = _round_up(n - 2 * MLSTM_HEADS, LANES) + LANES
    w_pad = jnp.zeros((d, n_pad), BF16).at[:, :n].set(w_in.astype(BF16))
    bias = jnp.zeros((1, LANES), F32).at[0, :2 * MLSTM_HEADS].set(jnp.concatenate([b_i, b_f]).astype(F32))
    return dict(w_in=w_pad, bias=bias)


def _mlstm_prompt_kernel(q_ref, k_ref, v_ref, og_ref, gc_ref, gr_ref, gain_ref, hn_ref, c_ref, n_ref, m_ref,
                         *, nh, dk, dv, ln):
    ci = pl.program_id(1)

    @pl.when(ci == 0)
    def _():
        c_ref[...] = jnp.zeros_like(c_ref)
        n_ref[...] = jnp.zeros_like(n_ref)
        m_ref[...] = jnp.full(m_ref.shape, NEG, F32)

    row = lax.broadcasted_iota(jnp.int32, (ln, ln), 0)
    col = lax.broadcasted_iota(jnp.int32, (ln, ln), 1)
    tri = row >= col
    gc = gc_ref[...]
    gr = gr_ref[...]
    for h in range(nh):
        q = q_ref[:, h * dk:(h + 1) * dk]
        k = k_ref[:, h * dk:(h + 1) * dk]
        v = v_ref[:, h * dv:(h + 1) * dv]
        ig_c, lf_c = gc[:, h:h + 1], gc[:, nh + h:nh + h + 1]
        ig_r, lf_r = gr[h:h + 1, :], gr[nh + h:nh + h + 1, :]
        b_c = jnp.sum(jnp.where(tri, lf_r, 0.0), axis=1, keepdims=True)
        b_r = jnp.sum(jnp.where(row <= col, lf_c, 0.0), axis=0, keepdims=True)
        m_prev = m_ref[0, h:h + 1, 0:1]
        c_prev = c_ref[0, h]
        n_prev = n_ref[0, h:h + 1, :]
        dmat = jnp.where(tri, b_c - b_r + ig_r, NEG)
        inter = b_c + m_prev
        mt = jnp.maximum(inter, jnp.max(dmat, -1, keepdims=True))
        wm = jnp.exp(dmat - mt)
        a = jnp.exp(inter - mt)
        wqk = wm * _dot_nt(q, k)
        num = a * _dot_nt(q, c_prev.astype(BF16)) + _dot(wqk.astype(BF16), v)
        den = a * jnp.sum(q.astype(F32) * n_prev, -1, keepdims=True) + jnp.sum(wqk, -1, keepdims=True)
        hh = num / jnp.maximum(jnp.abs(den), jnp.exp(-mt))
        b_end = b_c[ln - 1:ln, :]
        m_new = mt[ln - 1:ln, :]
        a_end = jnp.exp(b_end + m_prev - m_new)
        w_s = jnp.exp(b_end - b_c + ig_c - m_new)
        c_ref[0, h] = a_end * c_prev + _dot_tn((v.astype(F32) * w_s).astype(BF16), k)
        n_ref[0, h:h + 1, :] = a_end * n_prev + jnp.sum(w_s * k.astype(F32), axis=0, keepdims=True)
        m_ref[0, h:h + 1, :] = jnp.broadcast_to(m_new, (1, m_ref.shape[2]))
        hn = hh * lax.rsqrt(jnp.mean(hh * hh, -1, keepdims=True) + RMS_EPS)
        hn = hn * gain_ref[:, h * dv:(h + 1) * dv] * og_ref[:, h * dv:(h + 1) * dv]
        hn_ref[:, h * dv:(h + 1) * dv] = hn.astype(hn_ref.dtype)


def mlstm_prompt(q, k, v, og, gc, gr, gain, b, s):
    m, d = q.shape
    nh = MLSTM_HEADS
    dk = dv = d // nh
    ln = _row_tile(s, MLSTM_CHUNK)
    nch = s // ln
    row = lambda n: pl.BlockSpec((ln, n), lambda bi, ci: (bi * nch + ci, 0))
    return pl.pallas_call(
        functools.partial(_mlstm_prompt_kernel, nh=nh, dk=dk, dv=dv, ln=ln), grid=(b, nch),
        in_specs=[row(d), row(d), row(d), row(d), row(LANES),
                  pl.BlockSpec((SUBLANES, ln), lambda bi, ci: (0, bi * nch + ci)), _full((1, d))],
        out_specs=[row(d), pl.BlockSpec((1, nh, dv, dk), lambda bi, ci: (bi, 0, 0, 0)),
                   pl.BlockSpec((1, nh, dk), lambda bi, ci: (bi, 0, 0)),
                   pl.BlockSpec((1, nh, LANES), lambda bi, ci: (bi, 0, 0))],
        out_shape=[jax.ShapeDtypeStruct((m, d), BF16), jax.ShapeDtypeStruct((b, nh, dv, dk), F32),
                   jax.ShapeDtypeStruct((b, nh, dk), F32), jax.ShapeDtypeStruct((b, nh, LANES), F32)],
        compiler_params=_cparams(("parallel", "arbitrary"), VMEM_LIMIT), name="mlstm_prompt",
    )(q, k, v, og, gc, gr, gain.reshape(1, d))


def _mlstm_sample_kernel(q_ref, k_ref, v_ref, og_ref, g_ref, gain_ref, c_ref, n_ref, m_ref,
                         hn_ref, co_ref, no_ref, mo_ref, *, nh):
    row = lax.broadcasted_iota(jnp.int32, (SUBLANES, 1), 0)
    for h in range(nh):
        q = q_ref[0, h:h + 1, :]
        k = k_ref[0, h:h + 1, :]
        v = v_ref[0, h:h + 1, :].astype(F32)
        ig = g_ref[0, h:h + 1, 0:1]
        lf = g_ref[0, h:h + 1, 1:2]
        m_prev = m_ref[0, h:h + 1, :]
        c_prev = c_ref[0, h]
        n_prev = n_ref[0, h:h + 1, :]
        inter = lf + m_prev
        mt = jnp.maximum(inter, ig)
        wm = jnp.exp(ig - mt)
        a = jnp.exp(inter - mt)
        q8 = jnp.broadcast_to(q, (SUBLANES, q.shape[1]))
        cq = _dot_nt(q8, c_prev.astype(BF16))[0:1, :]
        wqk = wm * jnp.sum(q.astype(F32) * k.astype(F32), -1, keepdims=True)
        num = a * cq + wqk * v
        den = a * jnp.sum(n_prev * q.astype(F32), -1, keepdims=True) + wqk
        hh = num / jnp.maximum(jnp.abs(den), jnp.exp(-mt))
        v8 = jnp.where(row == 0, jnp.broadcast_to(v * wm, (SUBLANES, v.shape[1])), 0.0).astype(BF16)
        k8 = jnp.broadcast_to(k, (SUBLANES, k.shape[1]))
        co_ref[0, h] = a * c_prev + _dot_tn(v8, k8)
        no_ref[0, h:h + 1, :] = a * n_prev + wm * k.astype(F32)
        mo_ref[0, h:h + 1, :] = mt
        hn = hh * lax.rsqrt(jnp.mean(hh * hh, -1, keepdims=True) + RMS_EPS)
        hn_ref[0, h:h + 1, :] = (hn * gain_ref[h:h + 1, :] * og_ref[0, h:h + 1, :]).astype(hn_ref.dtype)


def mlstm_sample(q, k, v, og, gc, gain, c, n, m):
    bd, d = q.shape
    nh = MLSTM_HEADS
    dk = d // nh
    heads = lambda a: a.reshape(bd, nh, dk)
    g2 = jnp.transpose(gc[:, :2 * nh].reshape(bd, 2, nh), (0, 2, 1))
    blk3 = lambda n2: pl.BlockSpec((1, nh, n2), lambda bi: (bi, 0, 0))
    cspec = pl.BlockSpec((1, nh, dk, dk), lambda bi: (bi, 0, 0, 0))
    hn, co, no, mo = pl.pallas_call(
        functools.partial(_mlstm_sample_kernel, nh=nh), grid=(bd,),
        in_specs=[blk3(dk), blk3(dk), blk3(dk), blk3(dk), blk3(2), _full((nh, dk)), cspec, blk3(dk), blk3(1)],
        out_specs=[blk3(dk), cspec, blk3(dk), blk3(1)],
        out_shape=[jax.ShapeDtypeStruct((bd, nh, dk), BF16), jax.ShapeDtypeStruct(c.shape, F32),
                   jax.ShapeDtypeStruct(n.shape, F32), jax.ShapeDtypeStruct((bd, nh, 1), F32)],
        compiler_params=_cparams(("parallel",)), name="mlstm_sample",
    )(heads(q), heads(k), heads(v), heads(og), g2, gain.reshape(nh, dk), c, n, m.reshape(bd, nh, 1))
    return hn.reshape(bd, d), co, no, mo.reshape(bd, nh)


def _row_to_col(row):
    n = row.shape[1]
    eye = lax.broadcasted_iota(jnp.int32, (n, n), 0) == lax.broadcasted_iota(jnp.int32, (n, n), 1)
    return jnp.sum(jnp.where(eye, row, 0.0), axis=1, keepdims=True)


def _head_pad(q, keep):
    q2 = jnp.concatenate([q] * NSA_KV_HEADS, axis=1)
    row = lax.broadcasted_iota(jnp.int32, q2.shape, 0)
    lane = lax.broadcasted_iota(jnp.int32, q2.shape, 1)
    return jnp.where((lane // NSA_HD == row // NSA_GROUP) & keep(row), q2, jnp.zeros_like(q2))


def _nsa_sample_cmp_kernel(pt_ref, q_ref, pages_ref, pe_ref, w_ref, biasc_ref, oc_ref, idx_ref, xbuf, xrow, sem,
                           *, n_pages, nc, ncp, nsb, n_sel, past):
    b = pl.program_id(0)
    nb = pl.num_programs(0)
    kvd = NSA_KV_HEADS * NSA_HD

    def page_copy(bb, slot, p, sl):
        return pltpu.make_async_copy(pages_ref.at[pt_ref[bb * n_pages + p], pl.ds(sl * kvd, kvd), :],
                                     xbuf.at[slot, sl, p], sem.at[slot])

    def start_all(bb, slot):
        def body(p, c):
            page_copy(bb, slot, p, 0).start()
            page_copy(bb, slot, p, 1).start()
            return c
        lax.fori_loop(0, n_pages, body, 0)

    @pl.when(b == 0)
    def _():
        start_all(0, 0)

    @pl.when(b + 1 < nb)
    def _():
        start_all(b + 1, (b + 1) % 2)

    slot = b % 2

    def wait_body(p, c):
        page_copy(b, slot, p, 0).wait()
        page_copy(b, slot, p, 1).wait()
        return c
    lax.fori_loop(0, n_pages, wait_body, 0)

    def file_page(p, c):
        for sl in range(2):
            _file_rows(xbuf[slot, sl, p], xrow, sl, p)
        return c
    lax.fori_loop(0, n_pages, file_page, 0, unroll=8)

    acc = _compress_filed(xrow, pe_ref, w_ref, nc)
    kc = acc[:, 0:kvd].astype(BF16)
    vc = acc[:, kvd:2 * kvd].astype(BF16)
    q = q_ref[0]
    nh = q.shape[0]
    qpad = _head_pad(q, lambda r: r >= 0)
    s = _dot_nt(qpad, kc)
    s = s + biasc_ref[:, 0:nc]
    e = jnp.exp(s - jnp.max(s, -1, keepdims=True))
    p_c = e / jnp.sum(e, -1, keepdims=True)
    o = _dot(p_c.astype(BF16), vc)
    row = lax.broadcasted_iota(jnp.int32, (nh, NSA_HD), 0)
    o_h = o[:, 0:NSA_HD]
    for k in range(1, NSA_KV_HEADS):
        o_h = jnp.where(row // NSA_GROUP == k, o[:, k * NSA_HD:(k + 1) * NSA_HD], o_h)
    oc_ref[0] = o_h
    prow = lax.broadcasted_iota(jnp.int32, p_c.shape, 0)
    lane = lax.broadcasted_iota(jnp.int32, (1, ncp), 1)
    blk = lane // 2
    cur = past // SEL_BLOCK
    forced = (blk == 0) | (blk == cur) | (blk == cur - 1)
    is_cand = ((lane % 2) == 0) & (lane < 2 * nsb)
    nselp = idx_ref.shape[1]
    rsel = lax.broadcasted_iota(jnp.int32, (nselp, 1), 0).astype(F32)
    out_lane = lax.broadcasted_iota(jnp.int32, (nselp, LANES), 1)
    result = jnp.full((nselp, LANES), -1, jnp.int32)
    for k in range(NSA_KV_HEADS):
        imp = jnp.sum(jnp.where(prow // NSA_GROUP == k, p_c, 0.0), axis=0, keepdims=True)
        imp = jnp.concatenate([imp, jnp.zeros((1, ncp - nc), F32)], axis=1)
        imp = _pair_sum(imp)
        score = jnp.where(forced, FORCE_SCORE, jnp.where(blk <= cur, imp, -1.0))
        score = jnp.where(is_cand, score, -2.0)
        sel, rank = _select_blocks(score, n_sel, nsb)
        hit = (rank == rsel) & (sel > 0.5)
        idx = jnp.sum(jnp.where(hit, (blk + 1).astype(F32), 0.0), axis=1, keepdims=True) - 1.0
        result = jnp.where(out_lane == k, idx.astype(jnp.int32), result)
    idx_ref[0] = result


def nsa_sample_cmp(q8, pages, page_table, pe2, wbd, rel_bias):
    bd, nh, hd = q8.shape
    n_pages = page_table.shape[1]
    past = n_pages * PAGE_SIZE
    nc = past // CMP_BLOCK
    nsb = -(-(past + 1) // SEL_BLOCK)
    n_sel = min(TOP_N, nsb)
    ncp = _round_up(max(nc, SEL_RATIO * nsb), LANES)
    nselp = _round_up(n_sel, SUBLANES)
    kvd = wbd.shape[-1]
    cend = jnp.arange(nc) * CMP_BLOCK + (CMP_BLOCK - 1)
    biasc = jnp.zeros((nh, ncp), F32).at[:, :nc].set(_bias_of(rel_bias, past - cend).T)
    grid_spec = pltpu.PrefetchScalarGridSpec(
        num_scalar_prefetch=1, grid=(bd,),
        in_specs=[pl.BlockSpec((1, nh, hd), lambda bi, pt: (bi, 0, 0)),
                  pl.BlockSpec(memory_space=pl.ANY),
                  pl.BlockSpec(pe2.shape, lambda bi, pt: (0, 0, 0)),
                  pl.BlockSpec(wbd.shape, lambda bi, pt: (0, 0, 0, 0)),
                  pl.BlockSpec((nh, ncp), lambda bi, pt: (0, 0))],
        out_specs=[pl.BlockSpec((1, nh, hd), lambda bi, pt: (bi, 0, 0)),
                   pl.BlockSpec((1, nselp, LANES), lambda bi, pt: (bi, 0, 0))],
        scratch_shapes=[pltpu.VMEM((2, 2, n_pages, kvd, PAGE_SIZE), F32),
                        pltpu.VMEM((2, CMP_BLOCK // SUBLANES, nc * SUBLANES, kvd), F32),
                        pltpu.SemaphoreType.DMA((2,))])
    oc, idx = pl.pallas_call(
        functools.partial(_nsa_sample_cmp_kernel, n_pages=n_pages, nc=nc, ncp=ncp, nsb=nsb, n_sel=n_sel, past=past),
        grid_spec=grid_spec,
        out_shape=[jax.ShapeDtypeStruct((bd, nh, hd), F32), jax.ShapeDtypeStruct((bd, nselp, LANES), jnp.int32)],
        compiler_params=_cparams(("arbitrary",), VMEM_LIMIT), name="nsa_sample_cmp",
    )(page_table.reshape(-1), q8, pages, pe2, wbd, biasc)
    sel_idx = jnp.transpose(idx[:, :n_sel, :NSA_KV_HEADS], (0, 2, 1))
    return oc, sel_idx


def _nsa_sample_att_kernel(pt_ref, si_ref, q_ref, g_ref, oc_ref, kvn_ref, wn_ref, wc_ref, pages_ref,
                           bsel_ref, bwin_ref, ob_ref, win_ref, selbuf, wall, sem,
                           *, n_pages, n_sel, past, wb):
    b = pl.program_id(0)
    nb = pl.num_programs(0)
    kvd = NSA_KV_HEADS * NSA_HD
    hd = NSA_HD
    n_blk_pages = past // SEL_BLOCK
    per_page = PAGE_SIZE // SEL_BLOCK
    n_slots = NSA_KV_HEADS * n_sel

    def blk_of(bb, j):
        return si_ref[bb * n_slots + j]

    def blk_copy(bb, slot, j):
        blk = jnp.clip(blk_of(bb, j), 0, n_blk_pages - 1)
        page = pt_ref[bb * n_pages + blk // per_page]
        return pltpu.make_async_copy(pages_ref.at[page, pl.ds(2 * kvd, 2 * kvd), :], selbuf.at[slot, j], sem.at[slot])

    def in_pages(bb, j):
        blk = blk_of(bb, j)
        return (blk >= 0) & (blk < n_blk_pages)

    def start_all(bb, slot):
        def body(j, c):
            @pl.when(in_pages(bb, j))
            def _():
                blk_copy(bb, slot, j).start()
            return c
        lax.fori_loop(0, n_slots, body, 0)

    @pl.when(b == 0)
    def _():
        start_all(0, 0)

    @pl.when(b + 1 < nb)
    def _():
        start_all(b + 1, (b + 1) % 2)

    slot = b % 2
    new_sel = _row_to_col(kvn_ref[0][:, 2 * kvd:4 * kvd])
    lane = lax.broadcasted_iota(jnp.int32, (1, PAGE_SIZE), 1)

    def wait_body(j, c):
        @pl.when(in_pages(b, j))
        def _():
            blk_copy(b, slot, j).wait()

        @pl.when(jnp.logical_not(in_pages(b, j)))
        def _():
            is_new = blk_of(b, j) == n_blk_pages
            selbuf[slot, j] = jnp.where((lane == 0) & is_new, new_sel, 0.0)
        return c
    lax.fori_loop(0, n_slots, wait_body, 0)

    q = q_ref[0]
    nh = q.shape[0]
    gates = g_ref[0]
    head = lax.broadcasted_iota(jnp.int32, (nh, 1), 0)

    def attend(qp, keys_t, vals_t, bias, valid):
        s = jnp.where(valid, _dot(qp, keys_t) + bias, NEG)
        e = jnp.where(valid, jnp.exp(s - jnp.max(s, -1, keepdims=True)), 0.0)
        den = jnp.sum(e, -1, keepdims=True)
        p = e / jnp.where(den > 0, den, 1.0)
        return _dot_nt(p.astype(BF16), vals_t)

    o_s = jnp.zeros((nh, kvd), F32)
    for k in range(NSA_KV_HEADS):
        blks = [blk_of(b, k * n_sel + r) for r in range(n_sel)]
        tiles = [selbuf[slot, k * n_sel + r] for r in range(n_sel)]
        keys_t = jnp.concatenate([t_[0:kvd, :] for t_ in tiles], axis=1).astype(BF16)
        vals_t = jnp.concatenate([t_[kvd:2 * kvd, :] for t_ in tiles], axis=1).astype(BF16)
        bias = jnp.concatenate([bsel_ref[jnp.clip(bl // per_page, 0, n_pages)] for bl in blks], axis=1)
        valid = jnp.concatenate(
            [(lane // SEL_BLOCK == bl % per_page) & ((bl // per_page) * PAGE_SIZE + lane <= past) & (bl >= 0)
             for bl in blks], axis=1)
        o_k = attend(_head_pad(q, lambda r: r // NSA_GROUP == k), keys_t, vals_t, bias, valid)
        o_s = jnp.where(head // NSA_GROUP == k, o_k, o_s)
    wlanes = wall.shape[1]
    wall[:, 0:wb] = wc_ref[0]
    tail = lax.broadcasted_iota(jnp.int32, (1, wlanes - wb), 1)
    wall[:, wb:wlanes] = jnp.where(tail == 0, _row_to_col(wn_ref[0]), 0.0)
    win_ref[0] = pltpu.roll(wall[...], wlanes - 1, 1)[:, 0:wb]
    w_pos = lax.broadcasted_iota(jnp.int32, (1, wlanes), 1)
    valid_w = (w_pos <= wb) & (wb - w_pos < WINDOW) & (past - wb + w_pos >= 0)
    o_w = attend(_head_pad(q, lambda r: r >= 0), wall[0:kvd, :].astype(BF16), wall[kvd:2 * kvd, :].astype(BF16),
                 bwin_ref[...], valid_w)
    o_c = jnp.concatenate([oc_ref[0]] * NSA_KV_HEADS, axis=1)
    mix = gates[:, 0:1] * o_c + gates[:, 1:2] * o_s + gates[:, 2:3] * o_w
    out = mix[:, 0:hd]
    for k in range(1, NSA_KV_HEADS):
        out = jnp.where(head // NSA_GROUP == k, mix[:, k * hd:(k + 1) * hd], out)
    ob_ref[0] = out.astype(ob_ref.dtype)


def nsa_sample_att(q8, gates, oc, kv03, kv45, wcache_t, layer, pages_t, page_table, sel_idx, rel_bias):
    bd, nh, hd = q8.shape
    n_pages = page_table.shape[1]
    past = n_pages * PAGE_SIZE
    wb = wcache_t.shape[2]
    kvd = NSA_KV_HEADS * hd
    n_sel = sel_idx.shape[2]
    wlanes = _round_up(wb + 1, LANES)
    g3 = gates[:, :3 * nh].reshape(bd, nh, 3)
    dist = past - (jnp.arange(n_pages + 1)[:, None] * PAGE_SIZE + jnp.arange(PAGE_SIZE)[None, :])
    bsel = jnp.transpose(_bias_of(rel_bias, dist), (0, 2, 1))
    bwin = _bias_of(rel_bias, wb - jnp.arange(wlanes)).T
    blk = lambda n2, n3: pl.BlockSpec((1, n2, n3), lambda bi, pt, si: (bi, 0, 0))
    grid_spec = pltpu.PrefetchScalarGridSpec(
        num_scalar_prefetch=2, grid=(bd,),
        in_specs=[blk(nh, hd), blk(nh, 3), blk(nh, hd), blk(1, 4 * kvd), blk(1, 2 * kvd),
                  pl.BlockSpec((1, 2 * kvd, wb), lambda bi, pt, si: (layer * bd + bi, 0, 0)),
                  pl.BlockSpec(memory_space=pl.ANY),
                  pl.BlockSpec(bsel.shape, lambda bi, pt, si: (0, 0, 0)),
                  pl.BlockSpec(bwin.shape, lambda bi, pt, si: (0, 0))],
        out_specs=[blk(nh, hd), blk(2 * kvd, wb)],
        scratch_shapes=[pltpu.VMEM((2, NSA_KV_HEADS * n_sel, 2 * kvd, PAGE_SIZE), F32),
                        pltpu.VMEM((2 * kvd, wlanes), F32), pltpu.SemaphoreType.DMA((2,))])
    ob, win = pl.pallas_call(
        functools.partial(_nsa_sample_att_kernel, n_pages=n_pages, n_sel=n_sel, past=past, wb=wb),
        grid_spec=grid_spec,
        out_shape=[jax.ShapeDtypeStruct((bd, nh, hd), BF16), jax.ShapeDtypeStruct((bd, 2 * kvd, wb), F32)],
        compiler_params=_cparams(("arbitrary",), VMEM_LIMIT), name="nsa_sample_att",
    )(page_table.reshape(-1), sel_idx.reshape(-1), q8, g3, oc, kv03.reshape(bd, 1, -1), kv45.reshape(bd, 1, -1),
      wcache_t, pages_t, bsel, bwin)
    return ob.reshape(bd, nh * hd), win


def kernel(x_prompt, x_sample, mem_prompt, cache_conv, cache_nsa_pages, cache_nsa_window, state_mlstm_c,
           state_mlstm_n, state_mlstm_m, cache_mem_kv, page_table, rel_bias, norm_mix, norm_xattn, norm_mem,
           norm_ffn, norm_final, w_in_even, w_out_even, conv_w, conv_b, conv_ln_g, conv_ln_b, nsa_cmp_pe,
           nsa_cmp_w, w_in_odd, mlstm_b_i, mlstm_b_f, mlstm_norm, w_out_odd, xattn_wq, xattn_wkv, xattn_wo,
           ffn_w_gu, ffn_w_dn, router_w, router_b, expert_w_gu, expert_w_dn):
    b, s, d = x_prompt.shape
    bd, td, _ = x_sample.shape
    assert td == 1, "the sample group decodes one token per sequence"
    depth = norm_mix.shape[0]
    mt = mem_prompt.shape[1]
    cc = conv_w.shape[2]
    hist = conv_w.shape[1] - 1
    wb = cache_nsa_window.shape[2]
    kvh, hd = NSA_KV_HEADS, NSA_HD
    n_pool = cache_nsa_pages.shape[1]
    assert s >= hist and s >= wb and s % Q_BLOCK == 0
    xp = x_prompt.reshape(b * s, d)
    xs = x_sample.reshape(bd, d)
    mem = mem_prompt.reshape(b * mt, d)
    pages_t = jnp.swapaxes(cache_nsa_pages.reshape(-1, PAGE_SIZE, 4 * kvh * hd), 1, 2)
    window_t = jnp.swapaxes(cache_nsa_window.reshape(-1, wb, 2 * kvh * hd), 1, 2)
    xhd = d // X_HEADS
    memkv_rows = jnp.swapaxes(cache_mem_kv.reshape(depth * bd, mt, 2, X_HEADS, xhd // LANES, LANES), 3, 4)
    memkv_rows = memkv_rows.reshape(depth * bd, -1, LANES)
    bf = lambda a: a.astype(BF16)
    conv_p, conv_s, nsa_p, nsa_s, win_p, win_s = [], [], [], [], [], []
    mc_p, mc_s, mn_p, mn_s, mm_p, mm_s, memkv_p = [], [], [], [], [], [], []
    for l in range(depth):
        li = l // 2
        if l % 2 == 0:
            prm = prep_even(w_in_even[li], nsa_cmp_pe[li], nsa_cmp_w[li])
            w_out = bf(w_out_even[li])
            w_parts = [w_out[:cc], w_out[cc:]]
            conv_args = (conv_w[li], conv_b[li], conv_ln_g[li], conv_ln_b[li])
            glu, keys, qt, vt, gt, kv_t = inproj_even(xp, norm_mix[l], prm['w_in'], cc, s)
            a_out = conv_prompt(glu, *conv_args, b, s)
            kc, vct = compress_prompt(kv_t, prm['pe2'], prm['wbd'], _round_up(s // CMP_BLOCK, LANES))
            b_out = nsa_prompt(qt, gt, kc, vct, keys, vt, rel_bias, b, s)
            mix_p = ([a_out, b_out], w_parts)
            conv_p.append(glu.reshape(b, s, cc)[:, s - hist:])
            rows_t = kv_t.reshape(b, 6, kvh, hd, s)
            nsa_p.append(jnp.transpose(rows_t[:, :4], (0, 4, 1, 2, 3)))
            win_p.append(jnp.transpose(rows_t[:, 4:, :, :, s - wb:], (0, 4, 1, 2, 3)))
            glu, kv03, kv45, q, gates = inproj_even(xs, norm_mix[l], prm['w_in'], cc)
            a_out, conv_state = conv_sample(cache_conv[li], glu, *conv_args)
            q8 = q.reshape(bd, NSA_HEADS, hd)
            pt = page_table + li * n_pool
            o_c, sel_idx = nsa_sample_cmp(q8, pages_t, pt, prm['pe2'], prm['wbd'], rel_bias)
            b_out, win = nsa_sample_att(q8, gates, o_c, kv03, kv45, window_t, li, pages_t, pt, sel_idx, rel_bias)
            xs = outproj(xs, [a_out, b_out], w_parts)
            conv_s.append(conv_state)
            nsa_s.append(kv03.reshape(bd, 1, 4, kvh, hd))
            win_s.append(jnp.transpose(win.reshape(bd, 2, kvh, hd, wb), (0, 4, 1, 2, 3)))
        else:
            prm = prep_odd(w_in_odd[li], mlstm_b_i[li], mlstm_b_f[li])
            w_out = bf(w_out_odd[li])
            q, k, v, og, gc, gr = inproj_odd(xp, norm_mix[l], prm['w_in'], prm['bias'])
            hn, c_new, n_new, m_new = mlstm_prompt(q, k, v, og, gc, gr, mlstm_norm[li], b, s)
            mix_p = ([hn], [w_out])
            mc_p.append(c_new)
            mn_p.append(n_new)
            mm_p.append(m_new[:, :, 0])
            q, k, v, og, gc, gr = inproj_odd(xs, norm_mix[l], prm['w_in'], prm['bias'])
            hn, c_new, n_new, m_new = mlstm_sample(q, k, v, og, gc, mlstm_norm[li], state_mlstm_c[li],
                                                   state_mlstm_n[li], state_mlstm_m[li])
            xs = outproj(xs, [hn], [w_out])
            mc_s.append(c_new)
            mn_s.append(n_new)
            mm_s.append(m_new)
        wq, wo = bf(xattn_wq[l]), bf(xattn_wo[l])
        mkv_rows, mkv_b = memkv(mem, norm_mem[l], bf(xattn_wkv[l]), X_HEADS)
        mkv = jnp.swapaxes(mkv_rows.reshape(b, mt, 2, xhd // LANES, X_HEADS, LANES), 3, 4)
        memkv_p.append(mkv.reshape(b, mt, 2, X_HEADS, xhd))
        xs = xattn_sample(xs, norm_xattn[l], wq, memkv_rows, wo, l, mt)
        if l % 2 == 0:
            w_gu, w_dn = bf(ffn_w_gu[li]), bf(ffn_w_dn[li])
            xp = xattn_ffn(xp, norm_xattn[l], wq, mkv_b, wo, *mix_p, norm_ffn[l], w_gu, w_dn, b, s)
            xs = ffn(xs, norm_ffn[l], w_gu, w_dn)
        else:
            xp = xattn_prompt(xp, norm_xattn[l], wq, mkv_b, wo, b, s, *mix_p)
            e_gu, e_dn = bf(expert_w_gu[li]), bf(expert_w_dn[li])
            final_g = norm_final if l == depth - 1 else None
            comb, h, counts, pos, post, ranks_at = router(xp, norm_ffn[l], router_w[li], router_b[li], grouped=True)
            xp = moe_grouped(xp, h, comb, counts, pos, post, ranks_at, e_gu, e_dn, final_g)
            comb, h, _ = router(xs, norm_ffn[l], router_w[li], router_b[li])
            xs = moe(xs, h, comb, e_gu, e_dn, final_g)
    if depth % 2:
        xp, xs = rmsnorm(xp, norm_final), rmsnorm(xs, norm_final)
    y_prompt = xp.reshape(b, s, d)
    y_sample = xs.reshape(bd, 1, d)
    return (y_prompt, y_sample, jnp.stack(conv_p), jnp.stack(conv_s), jnp.stack(nsa_p), jnp.stack(nsa_s),
            jnp.stack(win_p), jnp.stack(win_s), jnp.stack(mc_p), jnp.stack(mc_s), jnp.stack(mn_p),
            jnp.stack(mn_s), jnp.stack(mm_p), jnp.stack(mm_s), jnp.stack(memkv_p))
```
